```python
import jax, jax.numpy as jnp
from jax import lax
import numpy as np

D_MODEL = 1024
BATCH = 4
SEQ = 4096
DEPTH = 1

N_MEM = 256
EPS = 1e-6
NEG = -1e30
ML_HEADS = 4
ML_DH = 128
ML_WIDTH = ML_HEADS * ML_DH
ML_CHUNK = 64
ML_CONV = 4
NSA_HEADS = 8
NSA_KV = 2
NSA_DH = 64
NSA_WIDTH = NSA_HEADS * NSA_DH
NSA_KV_WIDTH = NSA_KV * NSA_DH
CMP_LEN = 32
CMP_STRIDE = 16
CMP_HIDDEN = 256
SEL_LEN = 64
SEL_TOP = 16
WINDOW = 512
Q_BLOCK = 128
FORCE_BONUS = 1e3
XA_HEADS = 4
XA_DH = 128
XA_WIDTH = XA_HEADS * XA_DH
N_BRANCH = 3
BRANCH_WIDTH = 512
D_FF = 4 * D_MODEL
IN_SPLITS = (ML_WIDTH, ML_WIDTH, ML_WIDTH, ML_WIDTH, ML_HEADS, ML_HEADS,
             NSA_WIDTH, NSA_KV_WIDTH, NSA_KV_WIDTH, NSA_KV_WIDTH, NSA_KV_WIDTH, NSA_KV_WIDTH, NSA_KV_WIDTH,
             3 * NSA_HEADS, XA_WIDTH, N_BRANCH * D_MODEL)
D_IN = 4 * ML_WIDTH + 2 * ML_HEADS + NSA_WIDTH + 6 * NSA_KV_WIDTH + 3 * NSA_HEADS + XA_WIDTH + N_BRANCH * D_MODEL
ML_F_OFF = 4 * ML_WIDTH + ML_HEADS

kernel_name = 'hybrid_mlstm_nsa_memory_block'


def rmsnorm(x, g):
    xf = x.astype(jnp.float32)
    y = xf * lax.rsqrt(jnp.mean(xf * xf, axis=-1, keepdims=True) + EPS)
    return (y * g.astype(jnp.float32)).astype(x.dtype)


def alibi_slopes(n):
    return jnp.asarray(2.0 ** (-8.0 * np.arange(1, n + 1) / n), dtype=jnp.float32)


def causal_dwconv(x, w):
    K = w.shape[0]
    T = x.shape[1]
    xp = jnp.pad(x, ((0, 0), (K - 1, 0), (0, 0)))
    y = xp[:, K - 1:K - 1 + T] * w[0]
    for j in range(1, K):
        y = y + xp[:, K - 1 - j:K - 1 - j + T] * w[j]
    return y


def mlstm_chunkwise(q, k, v, i_pre, f_pre):
    f32 = jnp.float32
    B, T, H, Dh = q.shape
    L = ML_CHUNK
    NC = T // L
    def chunks(a):
        return a.astype(f32).reshape(B, NC, L, H, Dh).transpose(0, 3, 1, 2, 4)
    qc, kc, vc = chunks(q), chunks(k) * (Dh ** -0.5), chunks(v)
    li = i_pre.astype(f32).reshape(B, NC, L, H).transpose(0, 3, 1, 2)
    lf = jax.nn.log_sigmoid(f_pre.astype(f32)).reshape(B, NC, L, H).transpose(0, 3, 1, 2)
    b = jnp.cumsum(lf, axis=-1)
    g = b[..., -1]
    causal = np.tril(np.ones((L, L), dtype=bool))
    logD = jnp.where(causal, b[..., :, None] - b[..., None, :] + li[..., None, :], -jnp.inf)
    w_end = g[..., None] - b + li
    m_loc = jnp.max(w_end, axis=-1)
    e = jnp.exp(w_end - m_loc[..., None])
    A = jnp.einsum('bhcs,bhcsk,bhcsv->bhckv', e, kc, vc)
    nA = jnp.einsum('bhcs,bhcsk->bhck', e, kc)

    def step(carry, xs):
        C, n, m = carry
        g_c, m_c, A_c, nA_c = xs
        m_new = jnp.maximum(g_c + m, m_c)
        a = jnp.exp(g_c + m - m_new)
        bb = jnp.exp(m_c - m_new)
        C_new = a[..., None, None] * C + bb[..., None, None] * A_c
        n_new = a[..., None] * n + bb[..., None] * nA_c
        return (C_new, n_new, m_new), (C, n, m)

    init = (jnp.zeros((B, H, Dh, Dh), f32), jnp.zeros((B, H, Dh), f32), jnp.zeros((B, H), f32))
    xs = (jnp.moveaxis(g, 2, 0), jnp.moveaxis(m_loc, 2, 0), jnp.moveaxis(A, 2, 0), jnp.moveaxis(nA, 2, 0))
    _, (C_prev, n_prev, m_prev) = lax.scan(step, init, xs)
    C_prev = jnp.moveaxis(C_prev, 0, 2)
    n_prev = jnp.moveaxis(n_prev, 0, 2)
    m_prev = jnp.moveaxis(m_prev, 0, 2)
    inter_log = b + m_prev[..., None]
    m_t = jnp.maximum(inter_log, jnp.max(logD, axis=-1))
    S = jnp.einsum('bhctd,bhcsd->bhcts', qc, kc) * jnp.exp(logD - m_t[..., None])
    sc = jnp.exp(inter_log - m_t)
    num = jnp.einsum('bhcts,bhcsv->bhctv', S, vc) + sc[..., None] * jnp.einsum('bhctk,bhckv->bhctv', qc, C_prev)
    den = jnp.sum(S, axis=-1) + sc * jnp.einsum('bhctk,bhck->bhct', qc, n_prev)
    h = num / jnp.maximum(jnp.abs(den), jnp.exp(-m_t))[..., None]
    return h.transpose(0, 2, 3, 1, 4).reshape(B, T, H, Dh)


def compress_blocks(kv, pe, w1, w2):
    B, T, G, dh = kv.shape
    nb = (T - CMP_LEN) // CMP_STRIDE + 1
    idx = np.arange(nb)[:, None] * CMP_STRIDE + np.arange(CMP_LEN)[None, :]
    blk = kv[:, idx] + pe[:, None, :]
    blk = blk.transpose(0, 1, 3, 2, 4).reshape(B, nb, G, CMP_LEN * dh)
    return jax.nn.gelu(blk @ w1) @ w2


def nsa_attention(q, kc, vc, ks, vs, kw, vw, gates):
    f32 = jnp.float32
    B, T, H, dh = q.shape
    G = kc.shape[2]
    R = H // G
    nbc = kc.shape[1]
    nbs = T // SEL_LEN
    topn = min(SEL_TOP, nbs)
    scale = dh ** -0.5
    slopes = alibi_slopes(H).reshape(G, R)
    cmp_end = jnp.asarray(np.arange(nbc) * CMP_STRIDE + CMP_LEN - 1, jnp.int32)
    cs = np.arange(nbc) * CMP_STRIDE
    js = np.arange(nbs) * SEL_LEN
    overlap = jnp.asarray(((cs[:, None] < js[None, :] + SEL_LEN) & (cs[:, None] + CMP_LEN > js[None, :])).astype(np.float32))
    qg = q.astype(f32).reshape(B, T, G, R, dh)
    gg = gates.astype(f32).reshape(B, T, G, R, 3)
    kc = kc.astype(f32)
    vc = vc.astype(f32)
    ks_blk = ks.astype(f32).reshape(B, nbs, SEL_LEN, G, dh).transpose(0, 3, 1, 2, 4)
    vs_blk = vs.astype(f32).reshape(B, nbs, SEL_LEN, G, dh).transpose(0, 3, 1, 2, 4)
    kw_pad = jnp.pad(kw.astype(f32), ((0, 0), (WINDOW, 0), (0, 0), (0, 0)))
    vw_pad = jnp.pad(vw.astype(f32), ((0, 0), (WINDOW, 0), (0, 0), (0, 0)))
    b_ix = jnp.arange(B)[:, None, None, None]
    g_ix = jnp.arange(G)[None, None, :, None]
    blk_ids = jnp.arange(nbs)

    def one_block(c):
        t0 = c * Q_BLOCK
        qb = lax.dynamic_slice_in_dim(qg, t0, Q_BLOCK, axis=1) * scale
        gb = lax.dynamic_slice_in_dim(gg, t0, Q_BLOCK, axis=1)
        tpos = t0 + jnp.arange(Q_BLOCK)
        dist_c = tpos[:, None] - cmp_end[None, :]
        s_c = jnp.einsum('bqgrd,bngd->bqgrn', qb, kc) - slopes[:, :, None] * dist_c[:, None, None, :]
        ok_c = (dist_c >= 0)[:, None, None, :]
        p_c = jax.nn.softmax(jnp.where(ok_c, s_c, NEG), axis=-1) * ok_c
        o_c = jnp.einsum('bqgrn,bngd->bqgrd', p_c, vc)
        imp = jnp.einsum('bqgrn,nj->bqgj', p_c, overlap)
        cur = tpos // SEL_LEN
        sel_ok = (blk_ids[None, :] <= cur[:, None])[:, None, :]
        forced = ((blk_ids[None, :] == 0) | (blk_ids[None, :] == cur[:, None]) | (blk_ids[None, :] == cur[:, None] - 1))[:, None, :]
        score = jnp.where(sel_ok, imp + jnp.where(forced, FORCE_BONUS, 0.0), NEG)
        _, idx = lax.top_k(score, topn)
        k_sel = ks_blk[b_ix, g_ix, idx].reshape(B, Q_BLOCK, G, topn * SEL_LEN, dh)
        v_sel = vs_blk[b_ix, g_ix, idx].reshape(B, Q_BLOCK, G, topn * SEL_LEN, dh)
        kpos = (idx[..., None] * SEL_LEN + jnp.arange(SEL_LEN)).reshape(B, Q_BLOCK, G, topn * SEL_LEN)
        dist_s = tpos[None, :, None, None] - kpos
        s_s = jnp.einsum('bqgrd,bqgkd->bqgrk', qb, k_sel) - slopes[None, None, :, :, None] * dist_s[:, :, :, None, :]
        ok_s = (dist_s >= 0)[:, :, :, None, :]
        p_s = jax.nn.softmax(jnp.where(ok_s, s_s, NEG), axis=-1)
        o_s = jnp.einsum('bqgrk,bqgkd->bqgrd', p_s, v_sel)
        k_win = lax.dynamic_slice_in_dim(kw_pad, t0, Q_BLOCK + WINDOW, axis=1)
        v_win = lax.dynamic_slice_in_dim(vw_pad, t0, Q_BLOCK + WINDOW, axis=1)
        wpos = t0 - WINDOW + jnp.arange(Q_BLOCK + WINDOW)
        dist_w = tpos[:, None] - wpos[None, :]
        ok_w = ((dist_w >= 0) & (dist_w < WINDOW) & (wpos[None, :] >= 0))[:, None, None, :]
        s_w = jnp.einsum('bqgrd,bkgd->bqgrk', qb, k_win) - slopes[:, :, None] * dist_w[:, None, None, :]
        p_w = jax.nn.softmax(jnp.where(ok_w, s_w, NEG), axis=-1)
        o_w = jnp.einsum('bqgrk,bkgd->bqgrd', p_w, v_win)
        return gb[..., 0:1] * o_c + gb[..., 1:2] * o_s + gb[..., 2:3] * o_w

    out = lax.map(one_block, jnp.arange(T // Q_BLOCK))
    return jnp.moveaxis(out, 0, 1).reshape(B, T, H * dh)


def memory_cross_attention(q, mem_n, w_mem_kv):
    f32 = jnp.float32
    B, T, _ = q.shape
    M = mem_n.shape[1]
    kv = mem_n @ w_mem_kv
    mk, mv = jnp.split(kv, 2, axis=-1)
    qh = q.astype(f32).reshape(B, T, XA_HEADS, XA_DH) * (XA_DH ** -0.5)
    mk = mk.astype(f32).reshape(B, M, XA_HEADS, XA_DH)
    mv = mv.astype(f32).reshape(B, M, XA_HEADS, XA_DH)
    p = jax.nn.softmax(jnp.einsum('bthd,bmhd->bhtm', qh, mk), axis=-1)
    return jnp.einsum('bhtm,bmhd->bthd', p, mv).reshape(B, T, XA_WIDTH)


def hybrid_layer(h, mem, g_mix, w_in, b_in, ml_conv, ml_norm_g, cmp_pe, cmp_w1, cmp_w2,
                 g_mem, w_mem_kv, w_branch, w_out, g_ffn, w_ff1, w_ff2):
    f32 = jnp.float32
    B, T, _ = h.shape
    dt = h.dtype
    u = rmsnorm(h, g_mix)
    proj = u @ w_in + b_in
    splits = [int(s) for s in np.cumsum(IN_SPLITS)[:-1]]
    (ml_q, ml_k, ml_v, ml_o, ml_i, ml_f, ns_q, ns_kc, ns_vc, ns_ks, ns_vs, ns_kw, ns_vw,
     ns_g, xa_q, mg) = jnp.split(proj, splits, axis=-1)
    qk = jax.nn.silu(causal_dwconv(jnp.concatenate([ml_q, ml_k], axis=-1), ml_conv))
    mq, mk = jnp.split(qk, 2, axis=-1)
    hml = mlstm_chunkwise(mq.reshape(B, T, ML_HEADS, ML_DH), mk.reshape(B, T, ML_HEADS, ML_DH),
                          ml_v.reshape(B, T, ML_HEADS, ML_DH), ml_i, ml_f)
    hml = (hml * lax.rsqrt(jnp.mean(hml * hml, axis=-1, keepdims=True) + EPS)).reshape(B, T, ML_WIDTH)
    y_ml = (jax.nn.sigmoid(ml_o.astype(f32)) * hml * ml_norm_g.astype(f32)).astype(dt)
    kc = compress_blocks(ns_kc.reshape(B, T, NSA_KV, NSA_DH), cmp_pe[0], cmp_w1[0], cmp_w2[0])
    vc = compress_blocks(ns_vc.reshape(B, T, NSA_KV, NSA_DH), cmp_pe[1], cmp_w1[1], cmp_w2[1])
    y_nsa = nsa_attention(ns_q.reshape(B, T, NSA_HEADS, NSA_DH), kc, vc,
                          ns_ks.reshape(B, T, NSA_KV, NSA_DH), ns_vs.reshape(B, T, NSA_KV, NSA_DH),
                          ns_kw.reshape(B, T, NSA_KV, NSA_DH), ns_vw.reshape(B, T, NSA_KV, NSA_DH),
                          jax.nn.sigmoid(ns_g)).astype(dt)
    y_xa = memory_cross_attention(xa_q, rmsnorm(mem, g_mem), w_mem_kv).astype(dt)
    ys = jnp.stack([y_ml, y_nsa, y_xa], axis=2)
    ups = jnp.einsum('btjc,jcd->btjd', ys, w_branch)
    gates = jax.nn.sigmoid(mg.reshape(B, T, N_BRANCH, D_MODEL))
    merged = jnp.sum(gates * ups, axis=2)
    h = h + merged @ w_out
    a = rmsnorm(h, g_ffn) @ w_ff1
    return h + jnp.square(jax.nn.relu(a)) @ w_ff2


def setup_inputs(seed: int = 0) -> dict:
    key = jax.random.key(seed)
    ks = jax.random.split(key, 20)
    f32 = jnp.float32

    def nrm(k, shape, scale):
        return jax.random.normal(k, shape, f32) * scale

    def gain(k, n):
        return 1.0 + 0.02 * jax.random.normal(k, (DEPTH, n), f32)

    b_in = nrm(ks[4], (DEPTH, D_IN), 0.01)
    b_in = b_in.at[:, ML_F_OFF:ML_F_OFF + ML_HEADS].add(jnp.linspace(3.0, 6.0, ML_HEADS, dtype=f32))
    return {
        'x': nrm(ks[0], (BATCH, SEQ, D_MODEL), 1.0),
        'mem': nrm(ks[1], (BATCH, N_MEM, D_MODEL), 1.0),
        'g_mix': gain(ks[2], D_MODEL),
        'w_in': nrm(ks[3], (DEPTH, D_MODEL, D_IN), D_MODEL ** -0.5),
        'b_in': b_in,
        'ml_conv': nrm(ks[5], (DEPTH, ML_CONV, 2 * ML_WIDTH), ML_CONV ** -0.5),
        'ml_norm_g': gain(ks[6], ML_WIDTH),
        'cmp_pe': nrm(ks[7], (DEPTH, 2, CMP_LEN, NSA_DH), 0.1),
        'cmp_w1': nrm(ks[8], (DEPTH, 2, CMP_LEN * NSA_DH, CMP_HIDDEN), (CMP_LEN * NSA_DH) ** -0.5),
        'cmp_w2': nrm(ks[9], (DEPTH, 2, CMP_HIDDEN, NSA_DH), CMP_HIDDEN ** -0.5),
        'g_mem': gain(ks[10], D_MODEL),
        'w_mem_kv': nrm(ks[11], (DEPTH, D_MODEL, 2 * XA_WIDTH), D_MODEL ** -0.5),
        'w_branch': nrm(ks[12], (DEPTH, N_BRANCH, BRANCH_WIDTH, D_MODEL), BRANCH_WIDTH ** -0.5),
        'w_out': nrm(ks[13], (DEPTH, D_MODEL, D_MODEL), D_MODEL ** -0.5),
        'g_ffn': gain(ks[14], D_MODEL),
        'w_ff1': nrm(ks[15], (DEPTH, D_MODEL, D_FF), D_MODEL ** -0.5),
        'w_ff2': nrm(ks[16], (DEPTH, D_FF, D_MODEL), D_FF ** -0.5),
        'g_final': 1.0 + 0.02 * jax.random.normal(ks[17], (D_MODEL,), f32),
    }


def reference(x, mem, g_mix, w_in, b_in, ml_conv, ml_norm_g, cmp_pe, cmp_w1, cmp_w2,
              g_mem, w_mem_kv, w_branch, w_out, g_ffn, w_ff1, w_ff2, g_final):
    h = x
    for l in range(DEPTH):
        h = hybrid_layer(h, mem, g_mix[l], w_in[l], b_in[l], ml_conv[l], ml_norm_g[l], cmp_pe[l],
                         cmp_w1[l], cmp_w2[l], g_mem[l], w_mem_kv[l], w_branch[l], w_out[l],
                         g_ffn[l], w_ff1[l], w_ff2[l])
    return rmsnorm(h, g_final)
```

```python
import functools

import numpy as np
import jax
import jax.numpy as jnp
from jax import lax
from jax.experimental import pallas as pl
from jax.experimental.pallas import tpu as pltpu

F32 = jnp.float32
BF16 = jnp.bfloat16

EPS = 1e-6
NEG = -1e30
ML_HEADS = 4
ML_DH = 128
ML_WIDTH = ML_HEADS * ML_DH
ML_CONV = 4
ML_CHUNK = 128
NSA_HEADS = 8
NSA_KV = 2
NSA_REP = NSA_HEADS // NSA_KV
NSA_DH = 64
NSA_WIDTH = NSA_HEADS * NSA_DH
NSA_KV_WIDTH = NSA_KV * NSA_DH
CMP_LEN = 32
CMP_STRIDE = 16
CMP_HIDDEN = 256
SEL_LEN = 64
SEL_TOP = 16
WINDOW = 512
Q_BLOCK = 128
FORCE_BONUS = 1e3
SEL_KT = 256
XA_HEADS = 4
XA_DH = 128
XA_WIDTH = XA_HEADS * XA_DH
N_BRANCH = 3

LANES = 128
SUBLANES = 8
VMEM_LIMIT = 56 * 1024 * 1024

TM_PROJ = 256
TM_TAIL = 256
FF_SLAB = 1024

_NT = (((1,), (1,)), ((), ()))
_TN = (((0,), (0,)), ((), ()))


def _dot(a, b):
    return jnp.dot(a, b, preferred_element_type=F32)


def _dot_nt(a, b):
    return lax.dot_general(a, b, _NT, preferred_element_type=F32)


def _resident(shape):
    nd = len(shape)
    return pl.BlockSpec(shape, lambda *_: (0,) * nd, pipeline_mode=pl.Buffered(1))


def _rms(x, g):
    return x * lax.rsqrt(jnp.mean(x * x, axis=-1, keepdims=True) + EPS) * g


_PROJ_COLS = (
    ("qk", 2 * ML_WIDTH, F32),
    ("o", ML_WIDTH, F32),
    ("mg", None, F32),
    ("gcol", LANES, F32),
    ("kc", NSA_KV_WIDTH, F32),
    ("vc", NSA_KV_WIDTH, F32),
    ("v", ML_WIDTH, BF16),
    ("nq", NSA_HEADS * LANES, BF16),
    ("xq", XA_WIDTH, BF16),
    ("ks", NSA_KV_WIDTH, BF16),
    ("kw", NSA_KV_WIDTH, BF16),
)
_PROJ_ROWS = (
    ("smallT", 32, F32),
    ("vsT", NSA_KV_WIDTH, BF16),
    ("vwT", NSA_KV_WIDTH, BF16),
)
_PROJ_CHUNK = 512


def _proj_layout(d_model):
    cols, off = [], 0
    for name, width, dt in _PROJ_COLS:
        width = N_BRANCH * d_model if width is None else width
        cols.append((name, off, width, dt))
        off += width
    rows, roff = [], 0
    for name, r, dt in _PROJ_ROWS:
        rows.append((name, roff, r, dt))
        roff += r
    return cols, off, rows, roff


def _inproj_kernel(cols, rows, x_ref, g_ref, w_ref, b_ref, wt_ref, bt_ref, *out_refs):
    u = _rms(x_ref[...], g_ref[...]).astype(BF16)
    col_refs = out_refs[:len(cols)]
    row_refs = out_refs[len(cols):]
    for (name, off, width, dt), o_ref in zip(cols, col_refs):
        for c0 in range(0, width, _PROJ_CHUNK):
            cw = min(_PROJ_CHUNK, width - c0)
            acc = _dot(u, w_ref[:, off + c0:off + c0 + cw]) + b_ref[:, off + c0:off + c0 + cw]
            o_ref[:, c0:c0 + cw] = acc.astype(dt)
    t = _dot_nt(wt_ref[...], u) + bt_ref[...]
    for (name, roff, r, dt), o_ref in zip(rows, row_refs):
        o_ref[...] = t[roff:roff + r, :].astype(dt)


def _split_w_in(w_in, b_in):
    widths = (ML_WIDTH, ML_WIDTH, ML_WIDTH, ML_WIDTH, ML_HEADS, ML_HEADS,
              NSA_WIDTH, NSA_KV_WIDTH, NSA_KV_WIDTH, NSA_KV_WIDTH, NSA_KV_WIDTH, NSA_KV_WIDTH,
              NSA_KV_WIDTH, 3 * NSA_HEADS, XA_WIDTH, w_in.shape[1])
    names = ("ml_q", "ml_k", "ml_v", "ml_o", "ml_i", "ml_f", "ns_q", "ns_kc", "ns_vc", "ns_ks",
             "ns_vs", "ns_kw", "ns_vw", "ns_g", "xa_q", "mg")
    out, off = {}, 0
    for n, wd in zip(names, widths):
        end = w_in.shape[1] if n == "mg" else off + wd
        out[n] = (w_in[:, off:end], b_in[off:end])
        off = end
    return out


def _inproj(x2, g_mix, w_in, b_in):
    n_tok, d = x2.shape
    cols, ncols, rows, nrows = _proj_layout(d)
    p = _split_w_in(w_in, b_in)

    def pad_cols(w, b, width):
        return (jnp.pad(w, ((0, 0), (0, width - w.shape[1]))), jnp.pad(b, (0, width - b.shape[0])))

    wq, bq = p["ns_q"]
    wq = wq.reshape(d, NSA_HEADS, NSA_DH)
    bq = bq.reshape(NSA_HEADS, NSA_DH)
    wq_slots = jnp.zeros((d, NSA_HEADS, NSA_KV, NSA_DH), w_in.dtype)
    bq_slots = jnp.zeros((NSA_HEADS, NSA_KV, NSA_DH), w_in.dtype)
    for h in range(NSA_HEADS):
        wq_slots = wq_slots.at[:, h, h // NSA_REP].set(wq[:, h])
        bq_slots = bq_slots.at[h, h // NSA_REP].set(bq[h])
    gate_w = jnp.concatenate([p["ml_i"][0], p["ml_f"][0]], axis=1)
    gate_b = jnp.concatenate([p["ml_i"][1], p["ml_f"][1]])
    pieces = {
        "qk": (jnp.concatenate([p["ml_q"][0], p["ml_k"][0]], axis=1),
               jnp.concatenate([p["ml_q"][1], p["ml_k"][1]])),
        "o": p["ml_o"], "mg": p["mg"],
        "gcol": pad_cols(gate_w, gate_b, LANES),
        "kc": p["ns_kc"], "vc": p["ns_vc"], "v": p["ml_v"],
        "nq": (wq_slots.reshape(d, NSA_HEADS * LANES), bq_slots.reshape(NSA_HEADS * LANES)),
        "xq": p["xa_q"], "ks": p["ns_ks"], "kw": p["ns_kw"],
    }
    w_cols = jnp.concatenate([pieces[name][0] for name, *_ in cols], axis=1).astype(BF16)
    b_cols = jnp.concatenate([pieces[name][1] for name, *_ in cols])[None, :]
    small_w = jnp.concatenate([gate_w, p["ns_g"][0]], axis=1)
    small_b = jnp.concatenate([gate_b, p["ns_g"][1]])
    w_rows = jnp.concatenate([small_w, p["ns_vs"][0], p["ns_vw"][0]], axis=1).T.astype(BF16)
    b_rows = jnp.concatenate([small_b, p["ns_vs"][1], p["ns_vw"][1]])[:, None]

    tm = TM_PROJ
    out_shape = ([jax.ShapeDtypeStruct((n_tok, width), dt) for _, _, width, dt in cols]
                 + [jax.ShapeDtypeStruct((r, n_tok), dt) for _, _, r, dt in rows])
    out_specs = ([pl.BlockSpec((tm, width), lambda i: (i, 0)) for _, _, width, _ in cols]
                 + [pl.BlockSpec((r, tm), lambda i: (0, i)) for _, _, r, _ in rows])
    outs = pl.pallas_call(
        functools.partial(_inproj_kernel, cols, rows),
        out_shape=out_shape,
        grid=(n_tok // tm,),
        in_specs=[pl.BlockSpec((tm, d), lambda i: (i, 0)),
                  _resident((1, d)), _resident((d, ncols)), _resident((1, ncols)),
                  _resident((nrows, d)), _resident((nrows, 1))],
        out_specs=out_specs,
        compiler_params=pltpu.CompilerParams(dimension_semantics=("parallel",),
                                             vmem_limit_bytes=VMEM_LIMIT),
        name="inproj",
    )(x2, g_mix[None, :], w_cols, b_cols, w_rows, b_rows)
    names = [c[0] for c in cols] + [r[0] for r in rows]
    return dict(zip(names, outs))


def _mlstm_kernel(qk_ref, v_ref, o_ref, gcol_ref, grow_ref, conv_ref, ng_ref, y_ref,
                  xbuf, c_scr, n_scr, m_scr):
    L = ML_CHUNK
    halo = SUBLANES

    @pl.when(pl.program_id(1) == 0)
    def _():
        xbuf[0:halo, :] = jnp.zeros((halo, xbuf.shape[1]), F32)
        c_scr[...] = jnp.zeros(c_scr.shape, F32)
        n_scr[...] = jnp.zeros(n_scr.shape, F32)
        m_scr[...] = jnp.zeros(m_scr.shape, F32)

    xbuf[halo:halo + L, :] = qk_ref[...]
    w = conv_ref[...]
    acc = xbuf[halo:halo + L, :] * w[0:1, :]
    for j in range(1, ML_CONV):
        acc = acc + xbuf[halo - j:halo - j + L, :] * w[j:j + 1, :]
    xbuf[0:halo, :] = xbuf[L:L + halo, :]
    qk = acc * jax.nn.sigmoid(acc)

    gcol = gcol_ref[...]
    grow = grow_ref[...]
    r_i = lax.broadcasted_iota(jnp.int32, (L, L), 0)
    c_i = lax.broadcasted_iota(jnp.int32, (L, L), 1)
    causal = c_i <= r_i
    tril = causal.astype(F32)
    triu = (r_i <= c_i).astype(F32)
    b_col = jnp.dot(tril, jax.nn.log_sigmoid(gcol), precision=lax.Precision.HIGHEST,
                    preferred_element_type=F32)
    b_row = jnp.dot(jax.nn.log_sigmoid(grow), triu, precision=lax.Precision.HIGHEST,
                    preferred_element_type=F32)

    for hh in range(ML_HEADS):
        sl = slice(hh * ML_DH, (hh + 1) * ML_DH)
        q = qk[:, sl].astype(BF16)
        k = qk[:, ML_WIDTH + hh * ML_DH:ML_WIDTH + (hh + 1) * ML_DH] * (ML_DH ** -0.5)
        v = v_ref[:, sl]
        li_c = gcol[:, hh:hh + 1]
        b_c = b_col[:, ML_HEADS + hh:ML_HEADS + hh + 1]
        li_r = grow[hh:hh + 1, :]
        b_r = b_row[ML_HEADS + hh:ML_HEADS + hh + 1, :]
        g = b_c[L - 1:L, :]
        m_prev = m_scr[hh:hh + 1, 0:1]
        c_prev = c_scr[hh]
        n_prev = n_scr[hh:hh + 1, :]

        log_d = jnp.where(causal, b_c - b_r + li_r, -jnp.inf)
        inter_log = b_c + m_prev
        m_t = jnp.maximum(inter_log, jnp.max(log_d, axis=-1, keepdims=True))
        s = _dot_nt(q, k.astype(BF16)) * jnp.exp(log_d - m_t)
        sc = jnp.exp(inter_log - m_t)
        num = _dot(s.astype(BF16), v) + sc * _dot(q, c_prev.astype(BF16))
        qn = jnp.sum(qk[:, sl] * n_prev, axis=-1, keepdims=True)
        den = jnp.sum(s, axis=-1, keepdims=True) + sc * qn
        h = num / jnp.maximum(jnp.abs(den), jnp.exp(-m_t))

        w_end = g - b_c + li_c
        m_loc = jnp.max(w_end, axis=0, keepdims=True)
        ke = k * jnp.exp(w_end - m_loc)
        a_c = lax.dot_general(ke.astype(BF16), v, _TN, preferred_element_type=F32)
        n_a = jnp.sum(ke, axis=0, keepdims=True)
        m_new = jnp.maximum(g + m_prev, m_loc)
        a = jnp.exp(g + m_prev - m_new)
        bb = jnp.exp(m_loc - m_new)
        c_scr[hh] = a * c_prev + bb * a_c
        n_scr[hh:hh + 1, :] = a * n_prev + bb * n_a
        m_scr[hh:hh + 1, :] = jnp.broadcast_to(m_new, (1, m_scr.shape[1]))

        hn = h * lax.rsqrt(jnp.mean(h * h, axis=-1, keepdims=True) + EPS)
        y = jax.nn.sigmoid(o_ref[:, sl]) * hn * ng_ref[:, sl]
        y_ref[:, sl] = y.astype(y_ref.dtype)


def _mlstm(pr, ml_conv, ml_norm_g, batch, seq):
    L = ML_CHUNK
    nc = seq // L
    tok = lambda b, c: (b * nc + c, 0)
    return pl.pallas_call(
        _mlstm_kernel,
        out_shape=jax.ShapeDtypeStruct((batch * seq, ML_WIDTH), BF16),
        grid=(batch, nc),
        in_specs=[pl.BlockSpec((L, 2 * ML_WIDTH), tok),
                  pl.BlockSpec((L, ML_WIDTH), tok),
                  pl.BlockSpec((L, ML_WIDTH), tok),
                  pl.BlockSpec((L, LANES), tok),
                  pl.BlockSpec((SUBLANES, L), lambda b, c: (0, b * nc + c)),
                  pl.BlockSpec((ML_CONV, 2 * ML_WIDTH), lambda b, c: (0, 0)),
                  pl.BlockSpec((1, ML_WIDTH), lambda b, c: (0, 0))],
        out_specs=pl.BlockSpec((L, ML_WIDTH), tok),
        scratch_shapes=[pltpu.VMEM((L + SUBLANES, 2 * ML_WIDTH), F32),
                        pltpu.VMEM((ML_HEADS, ML_DH, ML_DH), F32),
                        pltpu.VMEM((SUBLANES, ML_DH), F32),
                        pltpu.VMEM((SUBLANES, LANES), F32)],
        compiler_params=pltpu.CompilerParams(dimension_semantics=("parallel", "arbitrary"),
                                             vmem_limit_bytes=VMEM_LIMIT),
        name="mlstm",
    )(pr["qk"], pr["v"], pr["o"], pr["gcol"], pr["smallT"], ml_conv, ml_norm_g[None, :])


def _gelu_tanh(x):
    return 0.5 * x * (1.0 + jnp.tanh(np.sqrt(2.0 / np.pi) * (x + 0.044715 * (x * x * x))))


def _compress_kernel(rk_ref, rv_ref, pe_ref, w1_ref, w2k_ref, w2vt_ref, kc_ref, vct_ref):
    nrow = rk_ref.shape[1]

    def hidden(r_ref, which):
        r = r_ref[0]
        lo = _dot((r + pe_ref[which, 0:1, :]).astype(BF16), w1_ref[which, 0])
        hi = _dot((r + pe_ref[which, 1:2, :]).astype(BF16), w1_ref[which, 1])
        return _gelu_tanh(lo + pltpu.roll(hi, nrow - 1, 0)).astype(BF16)

    kc_ref[0] = _dot(hidden(rk_ref, 0), w2k_ref[...]).astype(kc_ref.dtype)
    vct_ref[0] = _dot_nt(w2vt_ref[...], hidden(rv_ref, 1)).astype(vct_ref.dtype)


def _compress(pr, cmp_pe, cmp_w1, cmp_w2, batch, seq):
    per_row = CMP_STRIDE
    nrow = seq // per_row
    width = per_row * NSA_KV_WIDTH
    half = CMP_LEN // per_row
    eye = jnp.eye(NSA_KV, dtype=F32)
    w1 = cmp_w1.reshape(2, half, per_row, NSA_DH, CMP_HIDDEN)
    w1e = jnp.einsum("khldc,gj->khlgdjc", w1, eye).reshape(2, half, width, NSA_KV * CMP_HIDDEN)
    w2e = jnp.einsum("khd,gj->kghjd", cmp_w2, eye).reshape(2, NSA_KV * CMP_HIDDEN, NSA_KV_WIDTH)
    pe = jnp.broadcast_to(cmp_pe.reshape(2, half, per_row, 1, NSA_DH),
                          (2, half, per_row, NSA_KV, NSA_DH)).reshape(2, half, width)
    rk = pr["kc"].reshape(batch, nrow, width)
    rv = pr["vc"].reshape(batch, nrow, width)
    hid = NSA_KV * CMP_HIDDEN
    return pl.pallas_call(
        _compress_kernel,
        out_shape=(jax.ShapeDtypeStruct((batch, nrow, NSA_KV_WIDTH), BF16),
                   jax.ShapeDtypeStruct((batch, NSA_KV_WIDTH, nrow), BF16)),
        grid=(batch,),
        in_specs=[pl.BlockSpec((1, nrow, width), lambda b: (b, 0, 0)),
                  pl.BlockSpec((1, nrow, width), lambda b: (b, 0, 0)),
                  _resident((2, half, width)),
                  _resident((2, half, width, hid)),
                  _resident((hid, NSA_KV_WIDTH)),
                  _resident((NSA_KV_WIDTH, hid))],
        out_specs=(pl.BlockSpec((1, nrow, NSA_KV_WIDTH), lambda b: (b, 0, 0)),
                   pl.BlockSpec((1, NSA_KV_WIDTH, nrow), lambda b: (b, 0, 0))),
        compiler_params=pltpu.CompilerParams(dimension_semantics=("parallel",),
                                             vmem_limit_bytes=VMEM_LIMIT),
        name="compress",
    )(rk, rv, pe, w1e.astype(BF16), w2e[0].astype(BF16), w2e[1].T.astype(BF16))


def _alibi_slope(h):
    return float(2.0 ** (-8.0 * (h + 1) / NSA_HEADS))


def _nsa_kernel(nbs, q_ref, kc_ref, vct_ref, ks_ref, vst_ref, kw_ref, vwt_ref, gt_ref, ovt_ref,
                y_ref, selx):
    QB = Q_BLOCK
    t0 = pl.program_id(1) * QB
    ncmp = kc_ref.shape[1]
    gates = jax.nn.sigmoid(gt_ref[...])
    q_lane = lax.broadcasted_iota(jnp.int32, (1, QB), 1)
    tpos = t0 + q_lane

    def head_q(h):
        return q_ref[:, h * LANES:(h + 1) * LANES] * 0.125

    def group_rows(x, g):
        return x[g * NSA_DH:(g + 1) * NSA_DH, :]

    n_sub = lax.broadcasted_iota(jnp.int32, (ncmp, QB), 0)
    dist_c = tpos - (n_sub * CMP_STRIDE + (CMP_LEN - 1))
    ok_c = dist_c >= 0
    dist_cf = dist_c.astype(F32)
    kc = kc_ref[0]
    vct = vct_ref[0]
    o_cmp = []
    for g in range(NSA_KV):
        psum = jnp.zeros((ncmp, QB), F32)
        for r in range(NSA_REP):
            h = g * NSA_REP + r
            s = _dot_nt(kc, head_q(h)) - _alibi_slope(h) * dist_cf
            s = jnp.where(ok_c, s, NEG)
            e = jnp.exp(s - jnp.max(s, axis=0, keepdims=True))
            p = jnp.where(ok_c, e / jnp.sum(e, axis=0, keepdims=True), 0.0)
            psum = psum + p
            o_cmp.append(group_rows(_dot(vct, p.astype(BF16)), g))
        imp = jnp.dot(ovt_ref[...], psum, precision=lax.Precision.HIGHEST,
                      preferred_element_type=F32)
        j_sub = lax.broadcasted_iota(jnp.int32, (nbs, QB), 0)
        cur = tpos // SEL_LEN
        forced = (j_sub == 0) | (j_sub == cur) | (j_sub == cur - 1)
        score = jnp.where(j_sub <= cur, imp + jnp.where(forced, FORCE_BONUS, 0.0), NEG)
        n_grp = nbs // SUBLANES
        parts = [score[a * SUBLANES:(a + 1) * SUBLANES, :] for a in range(n_grp)]
        ranks = [jnp.zeros((SUBLANES, QB), F32) for _ in range(n_grp)]
        sub8 = lax.broadcasted_iota(jnp.int32, (SUBLANES, QB), 0)
        for i in range(nbs):
            row = jnp.broadcast_to(score[i:i + 1, :], (SUBLANES, QB))
            for a in range(n_grp):
                if a * SUBLANES > i:
                    beats = jnp.where(row >= parts[a], 1.0, 0.0)
                elif (a + 1) * SUBLANES - 1 < i:
                    beats = jnp.where(row > parts[a], 1.0, 0.0)
                else:
                    beats = jnp.where(sub8 + a * SUBLANES > i,
                                      jnp.where(row >= parts[a], 1.0, 0.0),
                                      jnp.where(row > parts[a], 1.0, 0.0))
                ranks[a] = ranks[a] + beats
        for a in range(n_grp):
            sel_bias = jnp.where(ranks[a] < float(min(SEL_TOP, nbs)), 0.0, NEG)
            for jj in range(SUBLANES):
                j = a * SUBLANES + jj
                selx[g, j * SUBLANES:(j + 1) * SUBLANES, :] = jnp.broadcast_to(
                    sel_bias[jj:jj + 1, :], (SUBLANES, QB))

    KT = SEL_KT
    blocks_per_tile = KT // SEL_LEN
    k_sub = lax.broadcasted_iota(jnp.int32, (KT, QB), 0)
    d0 = (q_lane - k_sub).astype(F32)
    n_tiles = (t0 + QB + KT - 1) // KT
    o_sel = []
    for g in range(NSA_KV):
        for r in range(NSA_REP):
            h = g * NSA_REP + r
            qh = head_q(h)
            slope = _alibi_slope(h)

            def tile_step(kt, carry, g=g, qh=qh, slope=slope):
                m, l, acc = carry
                k0 = pl.multiple_of(kt * KT, KT)
                s = _dot_nt(ks_ref[pl.ds(k0, KT), :], qh)
                dist = d0 + (t0 - k0).astype(F32)
                row0 = pl.multiple_of(kt * (blocks_per_tile * SUBLANES), SUBLANES)
                bias = jnp.concatenate(
                    [jnp.tile(selx[g, pl.ds(row0 + jj * SUBLANES, SUBLANES), :],
                              (SEL_LEN // SUBLANES, 1)) for jj in range(blocks_per_tile)], axis=0)
                s = jnp.where(dist >= 0.0, s - slope * dist + bias, NEG)
                m_new = jnp.maximum(m, jnp.max(s, axis=0, keepdims=True))
                alpha = jnp.exp(m - m_new)
                p = jnp.exp(s - m_new)
                l = alpha * l + jnp.sum(p, axis=0, keepdims=True)
                pv = _dot(vst_ref[:, pl.ds(k0, KT)], p.astype(BF16))
                return m_new, l, alpha * acc + group_rows(pv, g)

            init = (jnp.full((1, QB), NEG, F32), jnp.zeros((1, QB), F32),
                    jnp.zeros((NSA_DH, QB), F32))
            m, l, acc = lax.fori_loop(0, n_tiles, tile_step, init)
            o_sel.append(acc / l)

    WK = WINDOW + QB
    ws = pl.multiple_of(jnp.maximum(t0 - WINDOW, 0), QB)
    kw_sub = lax.broadcasted_iota(jnp.int32, (WK, QB), 0)
    dist_w = (t0 - ws) + q_lane - kw_sub
    ok_w = (dist_w >= 0) & (dist_w < WINDOW)
    dist_wf = dist_w.astype(F32)
    kwin = kw_ref[pl.ds(ws, WK), :]
    vwin = vwt_ref[:, pl.ds(ws, WK)]
    o_win = []
    for g in range(NSA_KV):
        for r in range(NSA_REP):
            h = g * NSA_REP + r
            s = _dot_nt(kwin, head_q(h)) - _alibi_slope(h) * dist_wf
            s = jnp.where(ok_w, s, NEG)
            p = jnp.exp(s - jnp.max(s, axis=0, keepdims=True))
            l = jnp.sum(p, axis=0, keepdims=True)
            o_win.append(group_rows(_dot(vwin, p.astype(BF16)), g) / l)

    outs = []
    for h in range(NSA_HEADS):
        g0 = SUBLANES + 3 * h
        outs.append(gates[g0:g0 + 1, :] * o_cmp[h] + gates[g0 + 1:g0 + 2, :] * o_sel[h]
                    + gates[g0 + 2:g0 + 3, :] * o_win[h])
    for pair in range(NSA_HEADS // 2):
        both = jnp.concatenate([outs[2 * pair], outs[2 * pair + 1]], axis=0)
        y_ref[:, pair * LANES:(pair + 1) * LANES] = both.T.astype(y_ref.dtype)


def _nsa(pr, kc, vct, batch, seq):
    QB = Q_BLOCK
    nq = seq // QB
    nbs = seq // SEL_LEN
    ncmp = kc.shape[1]
    cs = np.arange(ncmp) * CMP_STRIDE
    js = np.arange(nbs) * SEL_LEN
    ovt = jnp.asarray(((cs[None, :] < js[:, None] + SEL_LEN)
                       & (cs[None, :] + CMP_LEN > js[:, None])).astype(np.float32))
    per_b = lambda b, i: (b, 0)
    return pl.pallas_call(
        functools.partial(_nsa_kernel, nbs),
        out_shape=jax.ShapeDtypeStruct((batch * seq, NSA_WIDTH), BF16),
        grid=(batch, nq),
        in_specs=[pl.BlockSpec((QB, NSA_HEADS * LANES), lambda b, i: (b * nq + i, 0)),
                  pl.BlockSpec((1, ncmp, NSA_KV_WIDTH), lambda b, i: (b, 0, 0)),
                  pl.BlockSpec((1, NSA_KV_WIDTH, ncmp), lambda b, i: (b, 0, 0)),
                  pl.BlockSpec((seq, NSA_KV_WIDTH), per_b),
                  pl.BlockSpec((NSA_KV_WIDTH, seq), lambda b, i: (0, b)),
                  pl.BlockSpec((seq, NSA_KV_WIDTH), per_b),
                  pl.BlockSpec((NSA_KV_WIDTH, seq), lambda b, i: (0, b)),
                  pl.BlockSpec((32, QB), lambda b, i: (0, b * nq + i)),
                  pl.BlockSpec((nbs, ncmp), lambda b, i: (0, 0))],
        out_specs=pl.BlockSpec((QB, NSA_WIDTH), lambda b, i: (b * nq + i, 0)),
        scratch_shapes=[pltpu.VMEM((NSA_KV, nbs * SUBLANES, QB), F32)],
        compiler_params=pltpu.CompilerParams(dimension_semantics=("parallel", "arbitrary"),
                                             vmem_limit_bytes=VMEM_LIMIT),
        name="nsa",
    )(pr["nq"], kc, vct, pr["ks"], pr["vsT"], pr["kw"], pr["vwT"], pr["smallT"], ovt)


def _memkv_kernel(mem_ref, g_ref, w_ref, kv_ref):
    u = _rms(mem_ref[0], g_ref[...]).astype(BF16)
    kv_ref[0] = _dot(u, w_ref[...]).astype(kv_ref.dtype)


def _memkv(mem, g_mem, w_mem_kv):
    batch, n_mem, d = mem.shape
    return pl.pallas_call(
        _memkv_kernel,
        out_shape=jax.ShapeDtypeStruct((batch, n_mem, 2 * XA_WIDTH), BF16),
        grid=(batch,),
        in_specs=[pl.BlockSpec((1, n_mem, d), lambda b: (b, 0, 0)),
                  _resident((1, d)), _resident((d, 2 * XA_WIDTH))],
        out_specs=pl.BlockSpec((1, n_mem, 2 * XA_WIDTH), lambda b: (b, 0, 0)),
        compiler_params=pltpu.CompilerParams(dimension_semantics=("parallel",),
                                             vmem_limit_bytes=VMEM_LIMIT),
        name="memkv",
    )(mem, g_mem[None, :], w_mem_kv.astype(BF16))


def _tail_kernel(x_ref, yml_ref, ynsa_ref, xq_ref, kv_ref, mg_ref, wb_ref, wo_ref, gf_ref,
                 w1_ref, w2_ref, gl_ref, out_ref):
    d = x_ref.shape[1]
    kv = kv_ref[0]
    y_xa = []
    for hh in range(XA_HEADS):
        sl = slice(hh * XA_DH, (hh + 1) * XA_DH)
        s = _dot_nt(xq_ref[:, sl], kv[:, sl]) * (XA_DH ** -0.5)
        p = jnp.exp(s - jnp.max(s, axis=-1, keepdims=True))
        l = jnp.sum(p, axis=-1, keepdims=True)
        y_xa.append((_dot(p.astype(BF16), kv[:, XA_WIDTH + hh * XA_DH:XA_WIDTH + (hh + 1) * XA_DH])
                     / l).astype(BF16))
    ys = (yml_ref[...], ynsa_ref[...], jnp.concatenate(y_xa, axis=-1))
    merged = None
    for j in range(N_BRANCH):
        term = jax.nn.sigmoid(mg_ref[:, j * d:(j + 1) * d]) * _dot(ys[j], wb_ref[j])
        merged = term if merged is None else merged + term
    h = x_ref[...] + _dot(merged.astype(BF16), wo_ref[...])
    u = _rms(h, gf_ref[...]).astype(BF16)
    acc = h
    for c0 in range(0, w1_ref.shape[1], FF_SLAB):
        a = jnp.maximum(_dot(u, w1_ref[:, c0:c0 + FF_SLAB]), 0.0)
        acc = acc + _dot((a * a).astype(BF16), w2_ref[c0:c0 + FF_SLAB, :])
    out_ref[...] = _rms(acc, gl_ref[...])


def _tail(x2, y_ml, y_nsa, pr, mem_kv, w_branch, w_out, g_ffn, w_ff1, w_ff2, g_final, seq):
    n_tok, d = x2.shape
    tm = TM_TAIL
    d_ff = w_ff1.shape[1]
    n_mem = mem_kv.shape[1]
    tiles_per_b = seq // tm
    row = lambda i: (i, 0)
    return pl.pallas_call(
        _tail_kernel,
        out_shape=jax.ShapeDtypeStruct((n_tok, d), F32),
        grid=(n_tok // tm,),
        in_specs=[pl.BlockSpec((tm, d), row),
                  pl.BlockSpec((tm, ML_WIDTH), row),
                  pl.BlockSpec((tm, NSA_WIDTH), row),
                  pl.BlockSpec((tm, XA_WIDTH), row),
                  pl.BlockSpec((1, n_mem, 2 * XA_WIDTH), lambda i: (i // tiles_per_b, 0, 0)),
                  pl.BlockSpec((tm, N_BRANCH * d), row),
                  _resident((N_BRANCH, ML_WIDTH, d)), _resident((d, d)), _resident((1, d)),
                  _resident((d, d_ff)), _resident((d_ff, d)), _resident((1, d))],
        out_specs=pl.BlockSpec((tm, d), row),
        compiler_params=pltpu.CompilerParams(dimension_semantics=("parallel",),
                                             vmem_limit_bytes=VMEM_LIMIT),
        name="tail",
    )(x2, y_ml, y_nsa, pr["xq"], mem_kv, pr["mg"], w_branch.astype(BF16), w_out.astype(BF16),
      g_ffn[None, :], w_ff1.astype(BF16), w_ff2.astype(BF16), g_final[None, :])


def _layer(x, mem, g_mix, w_in, b_in, ml_conv, ml_norm_g, cmp_pe, cmp_w1, cmp_w2, g_mem, w_mem_kv,
           w_branch, w_out, g_ffn, w_ff1, w_ff2, g_final):
    batch, seq, d = x.shape
    x2 = x.reshape(batch * seq, d)
    pr = _inproj(x2, g_mix, w_in, b_in)
    y_ml = _mlstm(pr, ml_conv, ml_norm_g, batch, seq)
    kc, vct = _compress(pr, cmp_pe, cmp_w1, cmp_w2, batch, seq)
    y_nsa = _nsa(pr, kc, vct, batch, seq)
    mem_kv = _memkv(mem, g_mem, w_mem_kv)
    out = _tail(x2, y_ml, y_nsa, pr, mem_kv, w_branch, w_out, g_ffn, w_ff1, w_ff2, g_final, seq)
    return out.reshape(batch, seq, d)


def kernel(x, mem, g_mix, w_in, b_in, ml_conv, ml_norm_g, cmp_pe, cmp_w1, cmp_w2, g_mem, w_mem_kv,
           w_branch, w_out, g_ffn, w_ff1, w_ff2, g_final):
    assert g_mix.shape[0] == 1, "single-layer block"
    return _layer(x, mem, g_mix[0], w_in[0], b_in[0], ml_conv[0], ml_norm_g[0], cmp_pe[0],
                  cmp_w1[0], cmp_w2[0], g_mem[0], w_mem_kv[0], w_branch[0], w_out[0], g_ffn[0],
                  w_ff1[0], w_ff2[0], g_final)
```

```python
import functools

import numpy as np
import jax
import jax.numpy as jnp
from jax import lax
from jax.experimental import pallas as pl
from jax.experimental.pallas import tpu as pltpu

F32 = jnp.float32
BF16 = jnp.bfloat16

EPS = 1e-6
NEG = -1e30
ML_HEADS = 4
ML_DH = 128
ML_WIDTH = ML_HEADS * ML_DH
ML_CONV = 4
ML_CHUNK = 128
NSA_HEADS = 8
NSA_KV = 2
NSA_REP = NSA_HEADS // NSA_KV
NSA_DH = 64
NSA_WIDTH = NSA_HEADS * NSA_DH
NSA_KV_WIDTH = NSA_KV * NSA_DH
CMP_LEN = 32
CMP_STRIDE = 16
CMP_HIDDEN = 256
SEL_LEN = 64
SEL_TOP = 16
WINDOW = 512
Q_BLOCK = 128
FORCE_BONUS = 1e3
SEL_KT = 256
XA_HEADS = 4
XA_DH = 128
XA_WIDTH = XA_HEADS * XA_DH
N_BRANCH = 3

LANES = 128
SUBLANES = 8
VMEM_LIMIT = 56 * 1024 * 1024

TM_PROJ = 256
TM_TAIL = 256
FF_SLAB = 1024

_NT = (((1,), (1,)), ((), ()))
_TN = (((0,), (0,)), ((), ()))


def _dot(a, b):
    return jnp.dot(a, b, preferred_element_type=F32)


def _dot_nt(a, b):
    return lax.dot_general(a, b, _NT, preferred_element_type=F32)


def _resident(shape):
    nd = len(shape)
    return pl.BlockSpec(shape, lambda *_: (0,) * nd, pipeline_mode=pl.Buffered(1))


def _rms(x, g):
    return x * lax.rsqrt(jnp.mean(x * x, axis=-1, keepdims=True) + EPS) * g


_PROJ_COLS = (
    ("qk", 2 * ML_WIDTH, F32),
    ("o", ML_WIDTH, F32),
    ("mg", None, F32),
    ("gcol", LANES, F32),
    ("kc", NSA_KV_WIDTH, F32),
    ("vc", NSA_KV_WIDTH, F32),
    ("v", ML_WIDTH, BF16),
    ("nq", NSA_HEADS * LANES, BF16),
    ("xq", XA_WIDTH, BF16),
    ("ks", NSA_KV_WIDTH, BF16),
    ("kw", NSA_KV_WIDTH, BF16),
)
_PROJ_ROWS = (
    ("smallT", 32, F32),
    ("vsT", NSA_KV_WIDTH, BF16),
    ("vwT", NSA_KV_WIDTH, BF16),
)
_PROJ_CHUNK = 512


def _proj_layout(d_model):
    cols, off = [], 0
    for name, width, dt in _PROJ_COLS:
        width = N_BRANCH * d_model if width is None else width
        cols.append((name, off, width, dt))
        off += width
    rows, roff = [], 0
    for name, r, dt in _PROJ_ROWS:
        rows.append((name, roff, r, dt))
        roff += r
    return cols, off, rows, roff


def _inproj_kernel(cols, rows, x_ref, g_ref, w_ref, b_ref, wt_ref, bt_ref, *out_refs):
    u = _rms(x_ref[...], g_ref[...]).astype(BF16)
    col_refs = out_refs[:len(cols)]
    row_refs = out_refs[len(cols):]
    for (name, off, width, dt), o_ref in zip(cols, col_refs):
        for c0 in range(0, width, _PROJ_CHUNK):
            cw = min(_PROJ_CHUNK, width - c0)
            acc = _dot(u, w_ref[:, off + c0:off + c0 + cw]) + b_ref[:, off + c0:off + c0 + cw]
            o_ref[:, c0:c0 + cw] = acc.astype(dt)
    t = _dot_nt(wt_ref[...], u) + bt_ref[...]
    for (name, roff, r, dt), o_ref in zip(rows, row_refs):
        o_ref[...] = t[roff:roff + r, :].astype(dt)


def _split_w_in(w_in, b_in):
    widths = (ML_WIDTH, ML_WIDTH, ML_WIDTH, ML_WIDTH, ML_HEADS, ML_HEADS,
              NSA_WIDTH, NSA_KV_WIDTH, NSA_KV_WIDTH, NSA_KV_WIDTH, NSA_KV_WIDTH, NSA_KV_WIDTH,
              NSA_KV_WIDTH, 3 * NSA_HEADS, XA_WIDTH, w_in.shape[1])
    names = ("ml_q", "ml_k", "ml_v", "ml_o", "ml_i", "ml_f", "ns_q", "ns_kc", "ns_vc", "ns_ks",
             "ns_vs", "ns_kw", "ns_vw", "ns_g", "xa_q", "mg")
    out, off = {}, 0
    for n, wd in zip(names, widths):
        end = w_in.shape[1] if n == "mg" else off + wd
        out[n] = (w_in[:, off:end], b_in[off:end])
        off = end
    return out


def _inproj(x2, g_mix, w_in, b_in):
    n_tok, d = x2.shape
    cols, ncols, rows, nrows = _proj_layout(d)
    p = _split_w_in(w_in, b_in)

    def pad_cols(w, b, width):
        return (jnp.pad(w, ((0, 0), (0, width - w.shape[1]))), jnp.pad(b, (0, width - b.shape[0])))

    wq, bq = p["ns_q"]
    wq = wq.reshape(d, NSA_HEADS, NSA_DH)
    bq = bq.reshape(NSA_HEADS, NSA_DH)
    wq_slots = jnp.zeros((d, NSA_HEADS, NSA_KV, NSA_DH), w_in.dtype)
    bq_slots = jnp.zeros((NSA_HEADS, NSA_KV, NSA_DH), w_in.dtype)
    for h in range(NSA_HEADS):
        wq_slots = wq_slots.at[:, h, h // NSA_REP].set(wq[:, h])
        bq_slots = bq_slots.at[h, h // NSA_REP].set(bq[h])
    gate_w = jnp.concatenate([p["ml_i"][0], p["ml_f"][0]], axis=1)
    gate_b = jnp.concatenate([p["ml_i"][1], p["ml_f"][1]])
    pieces = {
        "qk": (jnp.concatenate([p["ml_q"][0], p["ml_k"][0]], axis=1),
               jnp.concatenate([p["ml_q"][1], p["ml_k"][1]])),
        "o": p["ml_o"], "mg": p["mg"],
        "gcol": pad_cols(gate_w, gate_b, LANES),
        "kc": p["ns_kc"], "vc": p["ns_vc"], "v": p["ml_v"],
        "nq": (wq_slots.reshape(d, NSA_HEADS * LANES), bq_slots.reshape(NSA_HEADS * LANES)),
        "xq": p["xa_q"], "ks": p["ns_ks"], "kw": p["ns_kw"],
    }
    w_cols = jnp.concatenate([pieces[name][0] for name, *_ in cols], axis=1).astype(BF16)
    b_cols = jnp.concatenate([pieces[name][1] for name, *_ in cols])[None, :]
    small_w = jnp.concatenate([gate_w, p["ns_g"][0]], axis=1)
    small_b = jnp.concatenate([gate_b, p["ns_g"][1]])
    w_rows = jnp.concatenate([small_w, p["ns_vs"][0], p["ns_vw"][0]], axis=1).T.astype(BF16)
    b_rows = jnp.concatenate([small_b, p["ns_vs"][1], p["ns_vw"][1]])[:, None]

    tm = TM_PROJ
    out_shape = ([jax.ShapeDtypeStruct((n_tok, width), dt) for _, _, width, dt in cols]
                 + [jax.ShapeDtypeStruct((r, n_tok), dt) for _, _, r, dt in rows])
    out_specs = ([pl.BlockSpec((tm, width), lambda i: (i, 0)) for _, _, width, _ in cols]
                 + [pl.BlockSpec((r, tm), lambda i: (0, i)) for _, _, r, _ in rows])
    outs = pl.pallas_call(
        functools.partial(_inproj_kernel, cols, rows),
        out_shape=out_shape,
        grid=(n_tok // tm,),
        in_specs=[pl.BlockSpec((tm, d), lambda i: (i, 0)),
                  _resident((1, d)), _resident((d, ncols)), _resident((1, ncols)),
                  _resident((nrows, d)), _resident((nrows, 1))],
        out_specs=out_specs,
        compiler_params=pltpu.CompilerParams(dimension_semantics=("parallel",),
                                             vmem_limit_bytes=VMEM_LIMIT),
        name="inproj",
    )(x2, g_mix[None, :], w_cols, b_cols, w_rows, b_rows)
    names = [c[0] for c in cols] + [r[0] for r in rows]
    return dict(zip(names, outs))


def _mlstm_kernel(qk_ref, v_ref, o_ref, gcol_ref, grow_ref, conv_ref, ng_ref, y_ref,
                  xbuf, c_scr, n_scr, m_scr):
    L = ML_CHUNK
    halo = SUBLANES

    @pl.when(pl.program_id(1) == 0)
    def _():
        xbuf[0:halo, :] = jnp.zeros((halo, xbuf.shape[1]), F32)
        c_scr[...] = jnp.zeros(c_scr.shape, F32)
        n_scr[...] = jnp.zeros(n_scr.shape, F32)
        m_scr[...] = jnp.zeros(m_scr.shape, F32)

    xbuf[halo:halo + L, :] = qk_ref[...]
    w = conv_ref[...]
    acc = xbuf[halo:halo + L, :] * w[0:1, :]
    for j in range(1, ML_CONV):
        acc = acc + xbuf[halo - j:halo - j + L, :] * w[j:j + 1, :]
    xbuf[0:halo, :] = xbuf[L:L + halo, :]
    qk = acc * jax.nn.sigmoid(acc)

    gcol = gcol_ref[...]
    grow = grow_ref[...]
    r_i = lax.broadcasted_iota(jnp.int32, (L, L), 0)
    c_i = lax.broadcasted_iota(jnp.int32, (L, L), 1)
    causal = c_i <= r_i
    tril = causal.astype(F32)
    triu = (r_i <= c_i).astype(F32)
    b_col = jnp.dot(tril, jax.nn.log_sigmoid(gcol), precision=lax.Precision.HIGHEST,
                    preferred_element_type=F32)
    b_row = jnp.dot(jax.nn.log_sigmoid(grow), triu, precision=lax.Precision.HIGHEST,
                    preferred_element_type=F32)

    for hh in range(ML_HEADS):
        sl = slice(hh * ML_DH, (hh + 1) * ML_DH)
        q = qk[:, sl].astype(BF16)
        k = qk[:, ML_WIDTH + hh * ML_DH:ML_WIDTH + (hh + 1) * ML_DH] * (ML_DH ** -0.5)
        v = v_ref[:, sl]
        li_c = gcol[:, hh:hh + 1]
        b_c = b_col[:, ML_HEADS + hh:ML_HEADS + hh + 1]
        li_r = grow[hh:hh + 1, :]
        b_r = b_row[ML_HEADS + hh:ML_HEADS + hh + 1, :]
        g = b_c[L - 1:L, :]
        m_prev = m_scr[hh:hh + 1, 0:1]
        c_prev = c_scr[hh]
        n_prev = n_scr[hh:hh + 1, :]

        log_d = jnp.where(causal, b_c - b_r + li_r, -jnp.inf)
        inter_log = b_c + m_prev
        m_t = jnp.maximum(inter_log, jnp.max(log_d, axis=-1, keepdims=True))
        s = _dot_nt(q, k.astype(BF16)) * jnp.exp(log_d - m_t)
        sc = jnp.exp(inter_log - m_t)
        num = _dot(s.astype(BF16), v) + sc * _dot(q, c_prev.astype(BF16))
        qn = jnp.sum(qk[:, sl] * n_prev, axis=-1, keepdims=True)
        den = jnp.sum(s, axis=-1, keepdims=True) + sc * qn
        h = num / jnp.maximum(jnp.abs(den), jnp.exp(-m_t))

        w_end = g - b_c + li_c
        m_loc = jnp.max(w_end, axis=0, keepdims=True)
        ke = k * jnp.exp(w_end - m_loc)
        a_c = lax.dot_general(ke.astype(BF16), v, _TN, preferred_element_type=F32)
        n_a = jnp.sum(ke, axis=0, keepdims=True)
        m_new = jnp.maximum(g + m_prev, m_loc)
        a = jnp.exp(g + m_prev - m_new)
        bb = jnp.exp(m_loc - m_new)
        c_scr[hh] = a * c_prev + bb * a_c
        n_scr[hh:hh + 1, :] = a * n_prev + bb * n_a
        m_scr[hh:hh + 1, :] = jnp.broadcast_to(m_new, (1, m_scr.shape[1]))

        hn = h * lax.rsqrt(jnp.mean(h * h, axis=-1, keepdims=True) + EPS)
        y = jax.nn.sigmoid(o_ref[:, sl]) * hn * ng_ref[:, sl]
        y_ref[:, sl] = y.astype(y_ref.dtype)


def _mlstm(pr, ml_conv, ml_norm_g, batch, seq):
    L = ML_CHUNK
    nc = seq // L
    tok = lambda b, c: (b * nc + c, 0)
    return pl.pallas_call(
        _mlstm_kernel,
        out_shape=jax.ShapeDtypeStruct((batch * seq, ML_WIDTH), BF16),
        grid=(batch, nc),
        in_specs=[pl.BlockSpec((L, 2 * ML_WIDTH), tok),
                  pl.BlockSpec((L, ML_WIDTH), tok),
                  pl.BlockSpec((L, ML_WIDTH), tok),
                  pl.BlockSpec((L, LANES), tok),
                  pl.BlockSpec((SUBLANES, L), lambda b, c: (0, b * nc + c)),
                  pl.BlockSpec((ML_CONV, 2 * ML_WIDTH), lambda b, c: (0, 0)),
                  pl.BlockSpec((1, ML_WIDTH), lambda b, c: (0, 0))],
        out_specs=pl.BlockSpec((L, ML_WIDTH), tok),
        scratch_shapes=[pltpu.VMEM((L + SUBLANES, 2 * ML_WIDTH), F32),
                        pltpu.VMEM((ML_HEADS, ML_DH, ML_DH), F32),
                        pltpu.VMEM((SUBLANES, ML_DH), F32),
                        pltpu.VMEM((SUBLANES, LANES), F32)],
        compiler_params=pltpu.CompilerParams(dimension_semantics=("parallel", "arbitrary"),
                                             vmem_limit_bytes=VMEM_LIMIT),
        name="mlstm",
    )(pr["qk"], pr["v"], pr["o"], pr["gcol"], pr["smallT"], ml_conv, ml_norm_g[None, :])


def _gelu_tanh(x):
    return 0.5 * x * (1.0 + jnp.tanh(np.sqrt(2.0 / np.pi) * (x + 0.044715 * (x * x * x))))


def _compress_kernel(rk_ref, rv_ref, pe_ref, w1_ref, w2k_ref, w2vt_ref, kc_ref, vct_ref):
    nrow = rk_ref.shape[1]

    def hidden(r_ref, which):
        r = r_ref[0]
        lo = _dot((r + pe_ref[which, 0:1, :]).astype(BF16), w1_ref[which, 0])
        hi = _dot((r + pe_ref[which, 1:2, :]).astype(BF16), w1_ref[which, 1])
        return _gelu_tanh(lo + pltpu.roll(hi, nrow - 1, 0)).astype(BF16)

    kc_ref[0] = _dot(hidden(rk_ref, 0), w2k_ref[...]).astype(kc_ref.dtype)
    vct_ref[0] = _dot_nt(w2vt_ref[...], hidden(rv_ref, 1)).astype(vct_ref.dtype)


def _compress(pr, cmp_pe, cmp_w1, cmp_w2, batch, seq):
    per_row = CMP_STRIDE
    nrow = seq // per_row
    width = per_row * NSA_KV_WIDTH
    half = CMP_LEN // per_row
    eye = jnp.eye(NSA_KV, dtype=F32)
    w1 = cmp_w1.reshape(2, half, per_row, NSA_DH, CMP_HIDDEN)
    w1e = jnp.einsum("khldc,gj->khlgdjc", w1, eye).reshape(2, half, width, NSA_KV * CMP_HIDDEN)
    w2e = jnp.einsum("khd,gj->kghjd", cmp_w2, eye).reshape(2, NSA_KV * CMP_HIDDEN, NSA_KV_WIDTH)
    pe = jnp.broadcast_to(cmp_pe.reshape(2, half, per_row, 1, NSA_DH),
                          (2, half, per_row, NSA_KV, NSA_DH)).reshape(2, half, width)
    rk = pr["kc"].reshape(batch, nrow, width)
    rv = pr["vc"].reshape(batch, nrow, width)
    hid = NSA_KV * CMP_HIDDEN
    return pl.pallas_call(
        _compress_kernel,
        out_shape=(jax.ShapeDtypeStruct((batch, nrow, NSA_KV_WIDTH), BF16),
                   jax.ShapeDtypeStruct((batch, NSA_KV_WIDTH, nrow), BF16)),
        grid=(batch,),
        in_specs=[pl.BlockSpec((1, nrow, width), lambda b: (b, 0, 0)),
                  pl.BlockSpec((1, nrow, width), lambda b: (b, 0, 0)),
                  _resident((2, half, width)),
                  _resident((2, half, width, hid)),
                  _resident((hid, NSA_KV_WIDTH)),
                  _resident((NSA_KV_WIDTH, hid))],
        out_specs=(pl.BlockSpec((1, nrow, NSA_KV_WIDTH), lambda b: (b, 0, 0)),
                   pl.BlockSpec((1, NSA_KV_WIDTH, nrow), lambda b: (b, 0, 0))),
        compiler_params=pltpu.CompilerParams(dimension_semantics=("parallel",),
                                             vmem_limit_bytes=VMEM_LIMIT),
        name="compress",
    )(rk, rv, pe, w1e.astype(BF16), w2e[0].astype(BF16), w2e[1].T.astype(BF16))


def _alibi_slope(h):
    return float(2.0 ** (-8.0 * (h + 1) / NSA_HEADS))


def _nsa_kernel(nbs, q_ref, kc_ref, vct_ref, ks_ref, vst_ref, kw_ref, vwt_ref, gt_ref, ovt_ref,
                y_ref, selx, qs_scr, ab_scr, m_scr, l_scr, acc_scr):
    QB = Q_BLOCK
    KT = SEL_KT
    t0 = pl.program_id(1) * QB
    ncmp = kc_ref.shape[1]
    gates = jax.nn.sigmoid(gt_ref[...])
    q_lane = lax.broadcasted_iota(jnp.int32, (1, QB), 1)
    tpos = t0 + q_lane
    qs_scr[...] = q_ref[...] * 0.125

    @pl.when(pl.program_id(1) == 0)
    def _():
        rel = (lax.broadcasted_iota(jnp.int32, (KT, QB), 1)
               - lax.broadcasted_iota(jnp.int32, (KT, QB), 0)).astype(F32)
        for h in range(NSA_HEADS):
            ab_scr[h] = -_alibi_slope(h) * rel

    def head_q(h):
        return qs_scr[:, h * LANES:(h + 1) * LANES]

    def group_rows(x, g):
        return x[g * NSA_DH:(g + 1) * NSA_DH, :]

    n_sub = lax.broadcasted_iota(jnp.int32, (ncmp, QB), 0)
    dist_c = tpos - (n_sub * CMP_STRIDE + (CMP_LEN - 1))
    ok_c = dist_c >= 0
    dist_cf = dist_c.astype(F32)
    kc = kc_ref[0]
    vct = vct_ref[0]
    o_cmp = []
    for g in range(NSA_KV):
        psum = jnp.zeros((ncmp, QB), F32)
        for r in range(NSA_REP):
            h = g * NSA_REP + r
            s = _dot_nt(kc, head_q(h)) - _alibi_slope(h) * dist_cf
            s = jnp.where(ok_c, s, NEG)
            e = jnp.exp(s - jnp.max(s, axis=0, keepdims=True))
            p = jnp.where(ok_c, e / jnp.sum(e, axis=0, keepdims=True), 0.0)
            psum = psum + p
            o_cmp.append(group_rows(_dot(vct, p.astype(BF16)), g))
        imp = jnp.dot(ovt_ref[...], psum, precision=lax.Precision.HIGHEST,
                      preferred_element_type=F32)
        j_sub = lax.broadcasted_iota(jnp.int32, (nbs, QB), 0)
        cur = tpos // SEL_LEN
        forced = (j_sub == 0) | (j_sub == cur) | (j_sub == cur - 1)
        score = jnp.where(j_sub <= cur, imp + jnp.where(forced, FORCE_BONUS, 0.0), NEG)
        n_grp = nbs // SUBLANES
        parts = [score[a * SUBLANES:(a + 1) * SUBLANES, :] for a in range(n_grp)]
        ranks = [jnp.zeros((SUBLANES, QB), F32) for _ in range(n_grp)]
        sub8 = lax.broadcasted_iota(jnp.int32, (SUBLANES, QB), 0)
        for i in range(nbs):
            row = jnp.broadcast_to(score[i:i + 1, :], (SUBLANES, QB))
            for a in range(n_grp):
                if a * SUBLANES > i:
                    beats = jnp.where(row >= parts[a], 1.0, 0.0)
                elif (a + 1) * SUBLANES - 1 < i:
                    beats = jnp.where(row > parts[a], 1.0, 0.0)
                else:
                    beats = jnp.where(sub8 + a * SUBLANES > i,
                                      jnp.where(row >= parts[a], 1.0, 0.0),
                                      jnp.where(row > parts[a], 1.0, 0.0))
                ranks[a] = ranks[a] + beats
        for a in range(n_grp):
            sel_bias = jnp.where(ranks[a] < float(min(SEL_TOP, nbs)), 0.0, NEG)
            for jj in range(SUBLANES):
                j = a * SUBLANES + jj
                selx[g, j * SUBLANES:(j + 1) * SUBLANES, :] = jnp.broadcast_to(
                    sel_bias[jj:jj + 1, :], (SUBLANES, QB))

    blocks_per_tile = KT // SEL_LEN
    m_scr[...] = jnp.full(m_scr.shape, NEG, F32)
    l_scr[...] = jnp.zeros(l_scr.shape, F32)
    acc_scr[...] = jnp.zeros(acc_scr.shape, F32)
    last_tile = t0 // KT

    def sel_tile(kt, causal):
        k0 = pl.multiple_of(kt * KT, KT)
        ktile = ks_ref[pl.ds(k0, KT), :]
        row0 = pl.multiple_of(kt * (blocks_per_tile * SUBLANES), SUBLANES)
        if causal:
            k_sub = lax.broadcasted_iota(jnp.int32, (KT, QB), 0)
            visible = (q_lane - k_sub) >= (k0 - t0)
        for g in range(NSA_KV):
            sel = [selx[g, pl.ds(row0 + jj * SUBLANES, SUBLANES), :] for jj in range(blocks_per_tile)]
            vt = vst_ref[g * NSA_DH:(g + 1) * NSA_DH, pl.ds(k0, KT)]
            for r in range(NSA_REP):
                h = g * NSA_REP + r
                shift = _alibi_slope(h) * k0.astype(F32)
                bias = jnp.concatenate([jnp.tile(sb + shift, (SEL_LEN // SUBLANES, 1)) for sb in sel],
                                       axis=0)
                s = _dot_nt(ktile, head_q(h)) + ab_scr[h] + bias
                if causal:
                    s = jnp.where(visible, s, NEG)
                m_old = m_scr[h]
                m_new = jnp.maximum(m_old, jnp.max(s, axis=0, keepdims=True))
                alpha = jnp.exp(m_old - m_new)
                p = jnp.exp(s - m_new)
                l_scr[h] = alpha * l_scr[h] + jnp.sum(p, axis=0, keepdims=True)
                acc_scr[h] = alpha * acc_scr[h] + _dot(vt, p.astype(BF16))
                m_scr[h] = m_new

    def full_tile(kt, carry):
        sel_tile(kt, False)
        return carry

    lax.fori_loop(0, last_tile, full_tile, 0)
    sel_tile(last_tile, True)
    o_sel = [acc_scr[h] / l_scr[h] for h in range(NSA_HEADS)]

    WK = WINDOW + QB
    ws = pl.multiple_of(jnp.maximum(t0 - WINDOW, 0), QB)
    kw_sub = lax.broadcasted_iota(jnp.int32, (WK, QB), 0)
    dist_w = (t0 - ws) + q_lane - kw_sub
    ok_w = (dist_w >= 0) & (dist_w < WINDOW)
    dist_wf = dist_w.astype(F32)
    kwin = kw_ref[pl.ds(ws, WK), :]
    vwin = vwt_ref[:, pl.ds(ws, WK)]
    o_win = []
    for g in range(NSA_KV):
        for r in range(NSA_REP):
            h = g * NSA_REP + r
            s = _dot_nt(kwin, head_q(h)) - _alibi_slope(h) * dist_wf
            s = jnp.where(ok_w, s, NEG)
            p = jnp.exp(s - jnp.max(s, axis=0, keepdims=True))
            l = jnp.sum(p, axis=0, keepdims=True)
            o_win.append(group_rows(_dot(vwin, p.astype(BF16)), g) / l)

    outs = []
    for h in range(NSA_HEADS):
        g0 = SUBLANES + 3 * h
        outs.append(gates[g0:g0 + 1, :] * o_cmp[h] + gates[g0 + 1:g0 + 2, :] * o_sel[h]
                    + gates[g0 + 2:g0 + 3, :] * o_win[h])
    for pair in range(NSA_HEADS // 2):
        both = jnp.concatenate([outs[2 * pair], outs[2 * pair + 1]], axis=0)
        y_ref[:, pair * LANES:(pair + 1) * LANES] = both.T.astype(y_ref.dtype)


def _nsa(pr, kc, vct, batch, seq):
    QB = Q_BLOCK
    nq = seq // QB
    nbs = seq // SEL_LEN
    ncmp = kc.shape[1]
    cs = np.arange(ncmp) * CMP_STRIDE
    js = np.arange(nbs) * SEL_LEN
    ovt = jnp.asarray(((cs[None, :] < js[:, None] + SEL_LEN)
                       & (cs[None, :] + CMP_LEN > js[:, None])).astype(np.float32))
    per_b = lambda b, i: (b, 0)
    return pl.pallas_call(
        functools.partial(_nsa_kernel, nbs),
        out_shape=jax.ShapeDtypeStruct((batch * seq, NSA_WIDTH), BF16),
        grid=(batch, nq),
        in_specs=[pl.BlockSpec((QB, NSA_HEADS * LANES), lambda b, i: (b * nq + i, 0)),
                  pl.BlockSpec((1, ncmp, NSA_KV_WIDTH), lambda b, i: (b, 0, 0)),
                  pl.BlockSpec((1, NSA_KV_WIDTH, ncmp), lambda b, i: (b, 0, 0)),
                  pl.BlockSpec((seq, NSA_KV_WIDTH), per_b),
                  pl.BlockSpec((NSA_KV_WIDTH, seq), lambda b, i: (0, b)),
                  pl.BlockSpec((seq, NSA_KV_WIDTH), per_b),
                  pl.BlockSpec((NSA_KV_WIDTH, seq), lambda b, i: (0, b)),
                  pl.BlockSpec((32, QB), lambda b, i: (0, b * nq + i)),
                  pl.BlockSpec((nbs, ncmp), lambda b, i: (0, 0))],
        out_specs=pl.BlockSpec((QB, NSA_WIDTH), lambda b, i: (b * nq + i, 0)),
        scratch_shapes=[pltpu.VMEM((NSA_KV, nbs * SUBLANES, QB), F32),
                        pltpu.VMEM((QB, NSA_HEADS * LANES), BF16),
                        pltpu.VMEM((NSA_HEADS, SEL_KT, QB), F32),
                        pltpu.VMEM((NSA_HEADS, 1, QB), F32),
                        pltpu.VMEM((NSA_HEADS, 1, QB), F32),
                        pltpu.VMEM((NSA_HEADS, NSA_DH, QB), F32)],
        compiler_params=pltpu.CompilerParams(dimension_semantics=("parallel", "arbitrary"),
                                             vmem_limit_bytes=VMEM_LIMIT),
        name="nsa",
    )(pr["nq"], kc, vct, pr["ks"], pr["vsT"], pr["kw"], pr["vwT"], pr["smallT"], ovt)


def _memkv_kernel(mem_ref, g_ref, w_ref, kv_ref):
    u = _rms(mem_ref[0], g_ref[...]).astype(BF16)
    kv_ref[0] = _dot(u, w_ref[...]).astype(kv_ref.dtype)


def _memkv(mem, g_mem, w_mem_kv):
    batch, n_mem, d = mem.shape
    return pl.pallas_call(
        _memkv_kernel,
        out_shape=jax.ShapeDtypeStruct((batch, n_mem, 2 * XA_WIDTH), BF16),
        grid=(batch,),
        in_specs=[pl.BlockSpec((1, n_mem, d), lambda b: (b, 0, 0)),
                  _resident((1, d)), _resident((d, 2 * XA_WIDTH))],
        out_specs=pl.BlockSpec((1, n_mem, 2 * XA_WIDTH), lambda b: (b, 0, 0)),
        compiler_params=pltpu.CompilerParams(dimension_semantics=("parallel",),
                                             vmem_limit_bytes=VMEM_LIMIT),
        name="memkv",
    )(mem, g_mem[None, :], w_mem_kv.astype(BF16))


def _tail_kernel(x_ref, yml_ref, ynsa_ref, xq_ref, kv_ref, mg_ref, wb_ref, wo_ref, gf_ref,
                 w1_ref, w2_ref, gl_ref, out_ref):
    d = x_ref.shape[1]
    kv = kv_ref[0]
    y_xa = []
    for hh in range(XA_HEADS):
        sl = slice(hh * XA_DH, (hh + 1) * XA_DH)
        s = _dot_nt(xq_ref[:, sl], kv[:, sl]) * (XA_DH ** -0.5)
        p = jnp.exp(s - jnp.max(s, axis=-1, keepdims=True))
        l = jnp.sum(p, axis=-1, keepdims=True)
        y_xa.append((_dot(p.astype(BF16), kv[:, XA_WIDTH + hh * XA_DH:XA_WIDTH + (hh + 1) * XA_DH])
                     / l).astype(BF16))
    ys = (yml_ref[...], ynsa_ref[...], jnp.concatenate(y_xa, axis=-1))
    merged = None
    for j in range(N_BRANCH):
        term = jax.nn.sigmoid(mg_ref[:, j * d:(j + 1) * d]) * _dot(ys[j], wb_ref[j])
        merged = term if merged is None else merged + term
    h = x_ref[...] + _dot(merged.astype(BF16), wo_ref[...])
    u = _rms(h, gf_ref[...]).astype(BF16)
    acc = h
    for c0 in range(0, w1_ref.shape[1], FF_SLAB):
        a = jnp.maximum(_dot(u, w1_ref[:, c0:c0 + FF_SLAB]), 0.0)
        acc = acc + _dot((a * a).astype(BF16), w2_ref[c0:c0 + FF_SLAB, :])
    out_ref[...] = _rms(acc, gl_ref[...])


def _tail(x2, y_ml, y_nsa, pr, mem_kv, w_branch, w_out, g_ffn, w_ff1, w_ff2, g_final, seq):
    n_tok, d = x2.shape
    tm = TM_TAIL
    d_ff = w_ff1.shape[1]
    n_mem = mem_kv.shape[1]
    tiles_per_b = seq // tm
    row = lambda i: (i, 0)
    return pl.pallas_call(
        _tail_kernel,
        out_shape=jax.ShapeDtypeStruct((n_tok, d), F32),
        grid=(n_tok // tm,),
        in_specs=[pl.BlockSpec((tm, d), row),
                  pl.BlockSpec((tm, ML_WIDTH), row),
                  pl.BlockSpec((tm, NSA_WIDTH), row),
                  pl.BlockSpec((tm, XA_WIDTH), row),
                  pl.BlockSpec((1, n_mem, 2 * XA_WIDTH), lambda i: (i // tiles_per_b, 0, 0)),
                  pl.BlockSpec((tm, N_BRANCH * d), row),
                  _resident((N_BRANCH, ML_WIDTH, d)), _resident((d, d)), _resident((1, d)),
                  _resident((d, d_ff)), _resident((d_ff, d)), _resident((1, d))],
        out_specs=pl.BlockSpec((tm, d), row),
        compiler_params=pltpu.CompilerParams(dimension_semantics=("parallel",),
                                             vmem_limit_bytes=VMEM_LIMIT),
        name="tail",
    )(x2, y_ml, y_nsa, pr["xq"], mem_kv, pr["mg"], w_branch.astype(BF16), w_out.astype(BF16),
      g_ffn[None, :], w_ff1.astype(BF16), w_ff2.astype(BF16), g_final[None, :])


def _layer(x, mem, g_mix, w_in, b_in, ml_conv, ml_norm_g, cmp_pe, cmp_w1, cmp_w2, g_mem, w_mem_kv,
           w_branch, w_out, g_ffn, w_ff1, w_ff2, g_final):
    batch, seq, d = x.shape
    x2 = x.reshape(batch * seq, d)
    pr = _inproj(x2, g_mix, w_in, b_in)
    y_ml = _mlstm(pr, ml_conv, ml_norm_g, batch, seq)
    kc, vct = _compress(pr, cmp_pe, cmp_w1, cmp_w2, batch, seq)
    y_nsa = _nsa(pr, kc, vct, batch, seq)
    mem_kv = _memkv(mem, g_mem, w_mem_kv)
    out = _tail(x2, y_ml, y_nsa, pr, mem_kv, w_branch, w_out, g_ffn, w_ff1, w_ff2, g_final, seq)
    return out.reshape(batch, seq, d)


def kernel(x, mem, g_mix, w_in, b_in, ml_conv, ml_norm_g, cmp_pe, cmp_w1, cmp_w2, g_mem, w_mem_kv,
           w_branch, w_out, g_ffn, w_ff1, w_ff2, g_final):
    assert g_mix.shape[0] == 1, "single-layer block"
    return _layer(x, mem, g_mix[0], w_in[0], b_in[0], ml_conv[0], ml_norm_g[0], cmp_pe[0],
                  cmp_w1[0], cmp_w2[0], g_mem[0], w_mem_kv[0], w_branch[0], w_out[0], g_ffn[0],
                  w_ff1[0], w_ff2[0], g_final)
```

```python
import functools

import numpy as np
import jax
import jax.numpy as jnp
from jax import lax
from jax.experimental import pallas as pl
from jax.experimental.pallas import tpu as pltpu

F32 = jnp.float32
BF16 = jnp.bfloat16

EPS = 1e-6
NEG = -1e30
ML_HEADS = 4
ML_DH = 128
ML_WIDTH = ML_HEADS * ML_DH
ML_CONV = 4
ML_CHUNK = 128
NSA_HEADS = 8
NSA_KV = 2
NSA_REP = NSA_HEADS // NSA_KV
NSA_DH = 64
NSA_WIDTH = NSA_HEADS * NSA_DH
NSA_KV_WIDTH = NSA_KV * NSA_DH
CMP_LEN = 32
CMP_STRIDE = 16
CMP_HIDDEN = 256
SEL_LEN = 64
SEL_TOP = 16
WINDOW = 512
Q_BLOCK = 128
FORCE_BONUS = 1e3
SEL_KT = 256
XA_HEADS = 4
XA_DH = 128
XA_WIDTH = XA_HEADS * XA_DH
N_BRANCH = 3

LANES = 128
SUBLANES = 8
VMEM_LIMIT = 56 * 1024 * 1024

TM_PROJ = 256
TM_TAIL = 256
FF_SLAB = 1024

_NT = (((1,), (1,)), ((), ()))
_TN = (((0,), (0,)), ((), ()))


def _dot(a, b):
    return jnp.dot(a, b, preferred_element_type=F32)


def _dot_nt(a, b):
    return lax.dot_general(a, b, _NT, preferred_element_type=F32)


def _resident(shape):
    nd = len(shape)
    return pl.BlockSpec(shape, lambda *_: (0,) * nd, pipeline_mode=pl.Buffered(1))


def _rms(x, g):
    return x * lax.rsqrt(jnp.mean(x * x, axis=-1, keepdims=True) + EPS) * g


_PROJ_COLS = (
    ("qk", 2 * ML_WIDTH, F32),
    ("o", ML_WIDTH, F32),
    ("mg", None, F32),
    ("gcol", LANES, F32),
    ("kc", NSA_KV_WIDTH, F32),
    ("vc", NSA_KV_WIDTH, F32),
    ("v", ML_WIDTH, BF16),
    ("nq", NSA_HEADS * LANES, BF16),
    ("xq", XA_WIDTH, BF16),
    ("ks", NSA_KV_WIDTH, BF16),
    ("kw", NSA_KV_WIDTH, BF16),
    ("vs0", LANES, BF16),
    ("vs1", LANES, BF16),
    ("vw0", LANES, BF16),
    ("vw1", LANES, BF16),
)
_PROJ_ROWS = (
    ("smallT", SUBLANES, F32),
)
_PROJ_CHUNK = 512


def _proj_layout(d_model):
    cols, off = [], 0
    for name, width, dt in _PROJ_COLS:
        width = N_BRANCH * d_model if width is None else width
        cols.append((name, off, width, dt))
        off += width
    rows, roff = [], 0
    for name, r, dt in _PROJ_ROWS:
        rows.append((name, roff, r, dt))
        roff += r
    return cols, off, rows, roff


def _inproj_kernel(cols, rows, x_ref, g_ref, w_ref, b_ref, wt_ref, bt_ref, *out_refs):
    u = _rms(x_ref[...], g_ref[...]).astype(BF16)
    col_refs = out_refs[:len(cols)]
    row_refs = out_refs[len(cols):]
    for (name, off, width, dt), o_ref in zip(cols, col_refs):
        for c0 in range(0, width, _PROJ_CHUNK):
            cw = min(_PROJ_CHUNK, width - c0)
            acc = _dot(u, w_ref[:, off + c0:off + c0 + cw]) + b_ref[:, off + c0:off + c0 + cw]
            o_ref[:, c0:c0 + cw] = acc.astype(dt)
    t = _dot_nt(wt_ref[...], u) + bt_ref[...]
    for (name, roff, r, dt), o_ref in zip(rows, row_refs):
        o_ref[...] = t[roff:roff + r, :].astype(dt)


def _split_w_in(w_in, b_in):
    widths = (ML_WIDTH, ML_WIDTH, ML_WIDTH, ML_WIDTH, ML_HEADS, ML_HEADS,
              NSA_WIDTH, NSA_KV_WIDTH, NSA_KV_WIDTH, NSA_KV_WIDTH, NSA_KV_WIDTH, NSA_KV_WIDTH,
              NSA_KV_WIDTH, 3 * NSA_HEADS, XA_WIDTH, w_in.shape[1])
    names = ("ml_q", "ml_k", "ml_v", "ml_o", "ml_i", "ml_f", "ns_q", "ns_kc", "ns_vc", "ns_ks",
             "ns_vs", "ns_kw", "ns_vw", "ns_g", "xa_q", "mg")
    out, off = {}, 0
    for n, wd in zip(names, widths):
        end = w_in.shape[1] if n == "mg" else off + wd
        out[n] = (w_in[:, off:end], b_in[off:end])
        off = end
    return out


def _inproj(x2, g_mix, w_in, b_in):
    n_tok, d = x2.shape
    cols, ncols, rows, nrows = _proj_layout(d)
    p = _split_w_in(w_in, b_in)

    def pad_cols(w, b, width):
        return (jnp.pad(w, ((0, 0), (0, width - w.shape[1]))), jnp.pad(b, (0, width - b.shape[0])))

    wq, bq = p["ns_q"]
    wq = wq.reshape(d, NSA_HEADS, NSA_DH)
    bq = bq.reshape(NSA_HEADS, NSA_DH)
    wq_slots = jnp.zeros((d, NSA_HEADS, NSA_KV, NSA_DH), w_in.dtype)
    bq_slots = jnp.zeros((NSA_HEADS, NSA_KV, NSA_DH), w_in.dtype)
    for h in range(NSA_HEADS):
        wq_slots = wq_slots.at[:, h, h // NSA_REP].set(wq[:, h])
        bq_slots = bq_slots.at[h, h // NSA_REP].set(bq[h])
    gate_w = jnp.concatenate([p["ml_i"][0], p["ml_f"][0]], axis=1)
    gate_b = jnp.concatenate([p["ml_i"][1], p["ml_f"][1]])
    small_w = jnp.concatenate([gate_w, p["ns_g"][0]], axis=1)
    small_b = jnp.concatenate([gate_b, p["ns_g"][1]])

    def with_ones(wb, g):
        w, b = wb
        w = jnp.pad(w[:, g * NSA_DH:(g + 1) * NSA_DH], ((0, 0), (0, LANES - NSA_DH)))
        b = jnp.concatenate([b[g * NSA_DH:(g + 1) * NSA_DH], jnp.ones((LANES - NSA_DH,), b.dtype)])
        return w, b

    pieces = {
        "qk": (jnp.concatenate([p["ml_q"][0], p["ml_k"][0]], axis=1),
               jnp.concatenate([p["ml_q"][1], p["ml_k"][1]])),
        "o": p["ml_o"], "mg": p["mg"],
        "gcol": pad_cols(small_w, small_b, LANES),
        "kc": p["ns_kc"], "vc": p["ns_vc"], "v": p["ml_v"],
        "nq": (wq_slots.reshape(d, NSA_HEADS * LANES), bq_slots.reshape(NSA_HEADS * LANES)),
        "xq": p["xa_q"], "ks": p["ns_ks"], "kw": p["ns_kw"],
        "vs0": with_ones(p["ns_vs"], 0), "vs1": with_ones(p["ns_vs"], 1),
        "vw0": with_ones(p["ns_vw"], 0), "vw1": with_ones(p["ns_vw"], 1),
    }
    w_cols = jnp.concatenate([pieces[name][0] for name, *_ in cols], axis=1).astype(BF16)
    b_cols = jnp.concatenate([pieces[name][1] for name, *_ in cols])[None, :]
    w_rows = gate_w.T.astype(BF16)
    b_rows = gate_b[:, None]

    tm = TM_PROJ
    out_shape = ([jax.ShapeDtypeStruct((n_tok, width), dt) for _, _, width, dt in cols]
                 + [jax.ShapeDtypeStruct((r, n_tok), dt) for _, _, r, dt in rows])
    out_specs = ([pl.BlockSpec((tm, width), lambda i: (i, 0)) for _, _, width, _ in cols]
                 + [pl.BlockSpec((r, tm), lambda i: (0, i)) for _, _, r, _ in rows])
    outs = pl.pallas_call(
        functools.partial(_inproj_kernel, cols, rows),
        out_shape=out_shape,
        grid=(n_tok // tm,),
        in_specs=[pl.BlockSpec((tm, d), lambda i: (i, 0)),
                  _resident((1, d)), _resident((d, ncols)), _resident((1, ncols)),
                  _resident((nrows, d)), _resident((nrows, 1))],
        out_specs=out_specs,
        compiler_params=pltpu.CompilerParams(dimension_semantics=("parallel",),
                                             vmem_limit_bytes=VMEM_LIMIT),
        name="inproj",
    )(x2, g_mix[None, :], w_cols, b_cols, w_rows, b_rows)
    names = [c[0] for c in cols] + [r[0] for r in rows]
    return dict(zip(names, outs))


def _mlstm_kernel(qk_ref, v_ref, o_ref, gcol_ref, grow_ref, conv_ref, ng_ref, y_ref,
                  xbuf, c_scr, n_scr, m_scr):
    L = ML_CHUNK
    halo = SUBLANES

    @pl.when(pl.program_id(1) == 0)
    def _():
        xbuf[0:halo, :] = jnp.zeros((halo, xbuf.shape[1]), F32)
        c_scr[...] = jnp.zeros(c_scr.shape, F32)
        n_scr[...] = jnp.zeros(n_scr.shape, F32)
        m_scr[...] = jnp.zeros(m_scr.shape, F32)

    xbuf[halo:halo + L, :] = qk_ref[...]
    w = conv_ref[...]
    acc = xbuf[halo:halo + L, :] * w[0:1, :]
    for j in range(1, ML_CONV):
        acc = acc + xbuf[halo - j:halo - j + L, :] * w[j:j + 1, :]
    xbuf[0:halo, :] = xbuf[L:L + halo, :]
    qk = acc * jax.nn.sigmoid(acc)

    gcol = gcol_ref[...]
    grow = grow_ref[...]
    r_i = lax.broadcasted_iota(jnp.int32, (L, L), 0)
    c_i = lax.broadcasted_iota(jnp.int32, (L, L), 1)
    causal = c_i <= r_i
    tril = causal.astype(F32)
    triu = (r_i <= c_i).astype(F32)
    b_col = jnp.dot(tril, jax.nn.log_sigmoid(gcol), precision=lax.Precision.HIGHEST,
                    preferred_element_type=F32)
    b_row = jnp.dot(jax.nn.log_sigmoid(grow), triu, precision=lax.Precision.HIGHEST,
                    preferred_element_type=F32)

    for hh in range(ML_HEADS):
        sl = slice(hh * ML_DH, (hh + 1) * ML_DH)
        q = qk[:, sl].astype(BF16)
        k = qk[:, ML_WIDTH + hh * ML_DH:ML_WIDTH + (hh + 1) * ML_DH] * (ML_DH ** -0.5)
        v = v_ref[:, sl]
        li_c = gcol[:, hh:hh + 1]
        b_c = b_col[:, ML_HEADS + hh:ML_HEADS + hh + 1]
        li_r = grow[hh:hh + 1, :]
        b_r = b_row[ML_HEADS + hh:ML_HEADS + hh + 1, :]
        g = b_c[L - 1:L, :]
        m_prev = m_scr[hh:hh + 1, 0:1]
        c_prev = c_scr[hh]
        n_prev = n_scr[hh:hh + 1, :]

        log_d = jnp.where(causal, b_c - b_r + li_r, -jnp.inf)
        inter_log = b_c + m_prev
        m_t = jnp.maximum(inter_log, jnp.max(log_d, axis=-1, keepdims=True))
        s = _dot_nt(q, k.astype(BF16)) * jnp.exp(log_d - m_t)
        sc = jnp.exp(inter_log - m_t)
        num = _dot(s.astype(BF16), v) + sc * _dot(q, c_prev.astype(BF16))
        qn = jnp.sum(qk[:, sl] * n_prev, axis=-1, keepdims=True)
        den = jnp.sum(s, axis=-1, keepdims=True) + sc * qn
        h = num / jnp.maximum(jnp.abs(den), jnp.exp(-m_t))

        w_end = g - b_c + li_c
        m_loc = jnp.max(w_end, axis=0, keepdims=True)
        ke = k * jnp.exp(w_end - m_loc)
        a_c = lax.dot_general(ke.astype(BF16), v, _TN, preferred_element_type=F32)
        n_a = jnp.sum(ke, axis=0, keepdims=True)
        m_new = jnp.maximum(g + m_prev, m_loc)
        a = jnp.exp(g + m_prev - m_new)
        bb = jnp.exp(m_loc - m_new)
        c_scr[hh] = a * c_prev + bb * a_c
        n_scr[hh:hh + 1, :] = a * n_prev + bb * n_a
        m_scr[hh:hh + 1, :] = jnp.broadcast_to(m_new, (1, m_scr.shape[1]))

        hn = h * lax.rsqrt(jnp.mean(h * h, axis=-1, keepdims=True) + EPS)
        y = jax.nn.sigmoid(o_ref[:, sl]) * hn * ng_ref[:, sl]
        y_ref[:, sl] = y.astype(y_ref.dtype)


def _mlstm(pr, ml_conv, ml_norm_g, batch, seq):
    L = ML_CHUNK
    nc = seq // L
    tok = lambda b, c: (b * nc + c, 0)
    return pl.pallas_call(
        _mlstm_kernel,
        out_shape=jax.ShapeDtypeStruct((batch * seq, ML_WIDTH), BF16),
        grid=(batch, nc),
        in_specs=[pl.BlockSpec((L, 2 * ML_WIDTH), tok),
                  pl.BlockSpec((L, ML_WIDTH), tok),
                  pl.BlockSpec((L, ML_WIDTH), tok),
                  pl.BlockSpec((L, LANES), tok),
                  pl.BlockSpec((SUBLANES, L), lambda b, c: (0, b * nc + c)),
                  pl.BlockSpec((ML_CONV, 2 * ML_WIDTH), lambda b, c: (0, 0)),
                  pl.BlockSpec((1, ML_WIDTH), lambda b, c: (0, 0))],
        out_specs=pl.BlockSpec((L, ML_WIDTH), tok),
        scratch_shapes=[pltpu.VMEM((L + SUBLANES, 2 * ML_WIDTH), F32),
                        pltpu.VMEM((ML_HEADS, ML_DH, ML_DH), F32),
                        pltpu.VMEM((SUBLANES, ML_DH), F32),
                        pltpu.VMEM((SUBLANES, LANES), F32)],
        compiler_params=pltpu.CompilerParams(dimension_semantics=("parallel", "arbitrary"),
                                             vmem_limit_bytes=VMEM_LIMIT),
        name="mlstm",
    )(pr["qk"], pr["v"], pr["o"], pr["gcol"], pr["smallT"], ml_conv, ml_norm_g[None, :])


def _gelu_tanh(x):
    return 0.5 * x * (1.0 + jnp.tanh(np.sqrt(2.0 / np.pi) * (x + 0.044715 * (x * x * x))))


def _compress_kernel(rk_ref, rv_ref, pe_ref, w1_ref, w2k_ref, w2v_ref, kc_ref, vc0_ref, vc1_ref):
    nrow = rk_ref.shape[1]

    def hidden(r_ref, which):
        r = r_ref[0]
        lo = _dot((r + pe_ref[which, 0:1, :]).astype(BF16), w1_ref[which, 0])
        hi = _dot((r + pe_ref[which, 1:2, :]).astype(BF16), w1_ref[which, 1])
        return _gelu_tanh(lo + pltpu.roll(hi, nrow - 1, 0)).astype(BF16)

    kc_ref[0] = _dot(hidden(rk_ref, 0), w2k_ref[...]).astype(kc_ref.dtype)
    hv = hidden(rv_ref, 1)
    ones_half = lax.broadcasted_iota(jnp.int32, (nrow, LANES), 1) >= NSA_DH
    for g, o_ref in enumerate((vc0_ref, vc1_ref)):
        o_ref[0] = jnp.where(ones_half, 1.0, _dot(hv, w2v_ref[g])).astype(o_ref.dtype)


def _compress(pr, cmp_pe, cmp_w1, cmp_w2, batch, seq):
    per_row = CMP_STRIDE
    nrow = seq // per_row
    width = per_row * NSA_KV_WIDTH
    half = CMP_LEN // per_row
    eye = jnp.eye(NSA_KV, dtype=F32)
    w1 = cmp_w1.reshape(2, half, per_row, NSA_DH, CMP_HIDDEN)
    w1e = jnp.einsum("khldc,gj->khlgdjc", w1, eye).reshape(2, half, width, NSA_KV * CMP_HIDDEN)
    w2e = jnp.einsum("khd,gj->kghjd", cmp_w2, eye).reshape(2, NSA_KV * CMP_HIDDEN, NSA_KV_WIDTH)
    pe = jnp.broadcast_to(cmp_pe.reshape(2, half, per_row, 1, NSA_DH),
                          (2, half, per_row, NSA_KV, NSA_DH)).reshape(2, half, width)
    rk = pr["kc"].reshape(batch, nrow, width)
    rv = pr["vc"].reshape(batch, nrow, width)
    hid = NSA_KV * CMP_HIDDEN
    w2v = jnp.stack([jnp.pad(w2e[1][:, g * NSA_DH:(g + 1) * NSA_DH], ((0, 0), (0, LANES - NSA_DH)))
                     for g in range(NSA_KV)])
    out_blk = pl.BlockSpec((1, nrow, LANES), lambda b: (b, 0, 0))
    return pl.pallas_call(
        _compress_kernel,
        out_shape=(jax.ShapeDtypeStruct((batch, nrow, LANES), BF16),) * 3,
        grid=(batch,),
        in_specs=[pl.BlockSpec((1, nrow, width), lambda b: (b, 0, 0)),
                  pl.BlockSpec((1, nrow, width), lambda b: (b, 0, 0)),
                  _resident((2, half, width)),
                  _resident((2, half, width, hid)),
                  _resident((hid, NSA_KV_WIDTH)),
                  _resident((NSA_KV, hid, LANES))],
        out_specs=(out_blk, out_blk, out_blk),
        compiler_params=pltpu.CompilerParams(dimension_semantics=("parallel",),
                                             vmem_limit_bytes=VMEM_LIMIT),
        name="compress",
    )(rk, rv, pe, w1e.astype(BF16), w2e[0].astype(BF16), w2v.astype(BF16))


def _alibi_slope(h):
    return float(2.0 ** (-8.0 * (h + 1) / NSA_HEADS))


def _nsa_kernel_v2(nbs, q_ref, kc_ref, vct_ref, ks_ref, vst_ref, kw_ref, vwt_ref, gt_ref, ovt_ref,
                y_ref, selx, qs_scr, ab_scr, m_scr, l_scr, acc_scr):
    QB = Q_BLOCK
    KT = SEL_KT
    t0 = pl.program_id(1) * QB
    ncmp = kc_ref.shape[1]
    gates = jax.nn.sigmoid(gt_ref[...])
    q_lane = lax.broadcasted_iota(jnp.int32, (1, QB), 1)
    tpos = t0 + q_lane
    qs_scr[...] = q_ref[...] * 0.125

    @pl.when(pl.program_id(1) == 0)
    def _():
        rel = (lax.broadcasted_iota(jnp.int32, (KT, QB), 1)
               - lax.broadcasted_iota(jnp.int32, (KT, QB), 0)).astype(F32)
        for h in range(NSA_HEADS):
            ab_scr[h] = -_alibi_slope(h) * rel

    def head_q(h):
        return qs_scr[:, h * LANES:(h + 1) * LANES]

    def group_rows(x, g):
        return x[g * NSA_DH:(g + 1) * NSA_DH, :]

    n_sub = lax.broadcasted_iota(jnp.int32, (ncmp, QB), 0)
    dist_c = tpos - (n_sub * CMP_STRIDE + (CMP_LEN - 1))
    ok_c = dist_c >= 0
    dist_cf = dist_c.astype(F32)
    kc = kc_ref[0]
    vct = vct_ref[0]
    o_cmp = []
    for g in range(NSA_KV):
        psum = jnp.zeros((ncmp, QB), F32)
        for r in range(NSA_REP):
            h = g * NSA_REP + r
            s = _dot_nt(kc, head_q(h)) - _alibi_slope(h) * dist_cf
            s = jnp.where(ok_c, s, NEG)
            e = jnp.exp(s - jnp.max(s, axis=0, keepdims=True))
            p = jnp.where(ok_c, e / jnp.sum(e, axis=0, keepdims=True), 0.0)
            psum = psum + p
            o_cmp.append(group_rows(_dot(vct, p.astype(BF16)), g))
        imp = jnp.dot(ovt_ref[...], psum, precision=lax.Precision.HIGHEST,
                      preferred_element_type=F32)
        j_sub = lax.broadcasted_iota(jnp.int32, (nbs, QB), 0)
        cur = tpos // SEL_LEN
        forced = (j_sub == 0) | (j_sub == cur) | (j_sub == cur - 1)
        score = jnp.where(j_sub <= cur, imp + jnp.where(forced, FORCE_BONUS, 0.0), NEG)
        n_grp = nbs // SUBLANES
        parts = [score[a * SUBLANES:(a + 1) * SUBLANES, :] for a in range(n_grp)]
        ranks = [jnp.zeros((SUBLANES, QB), F32) for _ in range(n_grp)]
        sub8 = lax.broadcasted_iota(jnp.int32, (SUBLANES, QB), 0)
        for i in range(nbs):
            row = jnp.broadcast_to(score[i:i + 1, :], (SUBLANES, QB))
            for a in range(n_grp):
                if a * SUBLANES > i:
                    beats = jnp.where(row >= parts[a], 1.0, 0.0)
                elif (a + 1) * SUBLANES - 1 < i:
                    beats = jnp.where(row > parts[a], 1.0, 0.0)
                else:
                    beats = jnp.where(sub8 + a * SUBLANES > i,
                                      jnp.where(row >= parts[a], 1.0, 0.0),
                                      jnp.where(row > parts[a], 1.0, 0.0))
                ranks[a] = ranks[a] + beats
        for a in range(n_grp):
            sel_bias = jnp.where(ranks[a] < float(min(SEL_TOP, nbs)), 0.0, NEG)
            for jj in range(SUBLANES):
                j = a * SUBLANES + jj
                selx[g, j * SUBLANES:(j + 1) * SUBLANES, :] = jnp.broadcast_to(
                    sel_bias[jj:jj + 1, :], (SUBLANES, QB))

    blocks_per_tile = KT // SEL_LEN
    m_scr[...] = jnp.full(m_scr.shape, NEG, F32)
    l_scr[...] = jnp.zeros(l_scr.shape, F32)
    acc_scr[...] = jnp.zeros(acc_scr.shape, F32)
    last_tile = t0 // KT

    def sel_tile(kt, causal):
        k0 = pl.multiple_of(kt * KT, KT)
        ktile = ks_ref[pl.ds(k0, KT), :]
        row0 = pl.multiple_of(kt * (blocks_per_tile * SUBLANES), SUBLANES)
        if causal:
            k_sub = lax.broadcasted_iota(jnp.int32, (KT, QB), 0)
            visible = (q_lane - k_sub) >= (k0 - t0)
        for g in range(NSA_KV):
            sel = [selx[g, pl.ds(row0 + jj * SUBLANES, SUBLANES), :] for jj in range(blocks_per_tile)]
            vt = vst_ref[g * NSA_DH:(g + 1) * NSA_DH, pl.ds(k0, KT)]
            for r in range(NSA_REP):
                h = g * NSA_REP + r
                shift = _alibi_slope(h) * k0.astype(F32)
                bias = jnp.concatenate([jnp.tile(sb + shift, (SEL_LEN // SUBLANES, 1)) for sb in sel],
                                       axis=0)
                s = _dot_nt(ktile, head_q(h)) + ab_scr[h] + bias
                if causal:
                    s = jnp.where(visible, s, NEG)
                m_old = m_scr[h]
                m_new = jnp.maximum(m_old, jnp.max(s, axis=0, keepdims=True))
                alpha = jnp.exp(m_old - m_new)
                p = jnp.exp(s - m_new)
                l_scr[h] = alpha * l_scr[h] + jnp.sum(p, axis=0, keepdims=True)
                acc_scr[h] = alpha * acc_scr[h] + _dot(vt, p.astype(BF16))
                m_scr[h] = m_new

    def full_tile(kt, carry):
        sel_tile(kt, False)
        return carry

    lax.fori_loop(0, last_tile, full_tile, 0)
    sel_tile(last_tile, True)
    o_sel = [acc_scr[h] / l_scr[h] for h in range(NSA_HEADS)]

    WK = WINDOW + QB
    ws = pl.multiple_of(jnp.maximum(t0 - WINDOW, 0), QB)
    kw_sub = lax.broadcasted_iota(jnp.int32, (WK, QB), 0)
    dist_w = (t0 - ws) + q_lane - kw_sub
    ok_w = (dist_w >= 0) & (dist_w < WINDOW)
    dist_wf = dist_w.astype(F32)
    kwin = kw_ref[pl.ds(ws, WK), :]
    vwin = vwt_ref[:, pl.ds(ws, WK)]
    o_win = []
    for g in range(NSA_KV):
        for r in range(NSA_REP):
            h = g * NSA_REP + r
            s = _dot_nt(kwin, head_q(h)) - _alibi_slope(h) * dist_wf
            s = jnp.where(ok_w, s, NEG)
            p = jnp.exp(s - jnp.max(s, axis=0, keepdims=True))
            l = jnp.sum(p, axis=0, keepdims=True)
            o_win.append(group_rows(_dot(vwin, p.astype(BF16)), g) / l)

    outs = []
    for h in range(NSA_HEADS):
        g0 = SUBLANES + 3 * h
        outs.append(gates[g0:g0 + 1, :] * o_cmp[h] + gates[g0 + 1:g0 + 2, :] * o_sel[h]
                    + gates[g0 + 2:g0 + 3, :] * o_win[h])
    for pair in range(NSA_HEADS // 2):
        both = jnp.concatenate([outs[2 * pair], outs[2 * pair + 1]], axis=0)
        y_ref[:, pair * LANES:(pair + 1) * LANES] = both.T.astype(y_ref.dtype)


def _nsa_v2(pr, kc, vct, batch, seq):
    QB = Q_BLOCK
    nq = seq // QB
    nbs = seq // SEL_LEN
    ncmp = kc.shape[1]
    cs = np.arange(ncmp) * CMP_STRIDE
    js = np.arange(nbs) * SEL_LEN
    ovt = jnp.asarray(((cs[None, :] < js[:, None] + SEL_LEN)
                       & (cs[None, :] + CMP_LEN > js[:, None])).astype(np.float32))
    per_b = lambda b, i: (b, 0)
    return pl.pallas_call(
        functools.partial(_nsa_kernel, nbs),
        out_shape=jax.ShapeDtypeStruct((batch * seq, NSA_WIDTH), BF16),
        grid=(batch, nq),
        in_specs=[pl.BlockSpec((QB, NSA_HEADS * LANES), lambda b, i: (b * nq + i, 0)),
                  pl.BlockSpec((1, ncmp, NSA_KV_WIDTH), lambda b, i: (b, 0, 0)),
                  pl.BlockSpec((1, NSA_KV_WIDTH, ncmp), lambda b, i: (b, 0, 0)),
                  pl.BlockSpec((seq, NSA_KV_WIDTH), per_b),
                  pl.BlockSpec((NSA_KV_WIDTH, seq), lambda b, i: (0, b)),
                  pl.BlockSpec((seq, NSA_KV_WIDTH), per_b),
                  pl.BlockSpec((NSA_KV_WIDTH, seq), lambda b, i: (0, b)),
                  pl.BlockSpec((32, QB), lambda b, i: (0, b * nq + i)),
                  pl.BlockSpec((nbs, ncmp), lambda b, i: (0, 0))],
        out_specs=pl.BlockSpec((QB, NSA_WIDTH), lambda b, i: (b * nq + i, 0)),
        scratch_shapes=[pltpu.VMEM((NSA_KV, nbs * SUBLANES, QB), F32),
                        pltpu.VMEM((QB, NSA_HEADS * LANES), BF16),
                        pltpu.VMEM((NSA_HEADS, SEL_KT, QB), F32),
                        pltpu.VMEM((NSA_HEADS, 1, QB), F32),
                        pltpu.VMEM((NSA_HEADS, 1, QB), F32),
                        pltpu.VMEM((NSA_HEADS, NSA_DH, QB), F32)],
        compiler_params=pltpu.CompilerParams(dimension_semantics=("parallel", "arbitrary"),
                                             vmem_limit_bytes=VMEM_LIMIT),
        name="nsa",
    )(pr["nq"], kc, vct, pr["ks"], pr["vsT"], pr["kw"], pr["vwT"], pr["smallT"], ovt)


NSA_ROWS = NSA_HEADS * Q_BLOCK
WIN_KEYS = WINDOW + Q_BLOCK
F_SEL_BLOCK = 0
F_SEL_OFF = SEL_LEN
F_WIN_OFF = SEL_LEN + 1
F_WIN_BLK = SEL_LEN + 2
F_CMP = SEL_LEN + 3


def _tile_heads(x):
    return jnp.concatenate([x] * NSA_HEADS, axis=0)


def _nsa_kernel(nbs, q_ref, g_ref, kc_ref, vc0_ref, vc1_ref, ks_ref, vs0_ref, vs1_ref,
                kw_ref, vw0_ref, vw1_ref, ov_ref, y_ref,
                qe_scr, posf_scr, wposf_scr, cposf_scr, s_scr, mrun_scr, mb_scr, acc_scr):
    QB = Q_BLOCK
    KT = SEL_KT
    HALF = NSA_REP * QB
    t0 = pl.program_id(1) * QB
    ncmp = kc_ref.shape[1]
    lane = lax.broadcasted_iota(jnp.int32, (1, LANES), 1)
    lo_half = lane < NSA_DH

    @pl.when(pl.program_id(1) == 0)
    def _():
        seq = posf_scr.shape[0]
        k = lax.broadcasted_iota(jnp.int32, (seq, LANES), 0)
        c = lax.broadcasted_iota(jnp.int32, (seq, LANES), 1)
        posf = jnp.where(c < SEL_LEN, jnp.where(k // SEL_LEN == c, 1.0, 0.0),
                         jnp.where(c == F_SEL_OFF, (k % SEL_LEN).astype(F32), 0.0))
        posf_scr[...] = posf.astype(BF16)
        k = lax.broadcasted_iota(jnp.int32, (WIN_KEYS, LANES), 0)
        c = lax.broadcasted_iota(jnp.int32, (WIN_KEYS, LANES), 1)
        wposf = jnp.where(c == F_WIN_OFF, (k % SEL_LEN).astype(F32),
                          jnp.where(c == F_WIN_BLK, (k // SEL_LEN).astype(F32), 0.0))
        wposf_scr[...] = wposf.astype(BF16)
        k = lax.broadcasted_iota(jnp.int32, (ncmp, LANES), 0)
        c = lax.broadcasted_iota(jnp.int32, (ncmp, LANES), 1)
        cposf_scr[...] = jnp.where(c == F_CMP, k.astype(F32), 0.0).astype(BF16)

    def const_feat(h):
        s = _alibi_slope(h)
        return jnp.where((lane == F_SEL_OFF) | (lane == F_WIN_OFF), s,
                         jnp.where(lane == F_WIN_BLK, SEL_LEN * s,
                                   jnp.where(lane == F_CMP, CMP_STRIDE * s, 0.0)))

    for h in range(NSA_HEADS):
        rows = slice(h * QB, (h + 1) * QB)
        qe_scr[rows, 0:LANES] = q_ref[:, h * LANES:(h + 1) * LANES] * 0.125
        qe_scr[rows, LANES:2 * LANES] = jnp.broadcast_to(const_feat(h), (QB, LANES)).astype(BF16)

    def normalised(o, guard):
        outs = []
        for pair in range(NSA_HEADS // 2):
            e = o[(2 * pair) * QB:(2 * pair + 1) * QB]
            d = o[(2 * pair + 1) * QB:(2 * pair + 2) * QB]
            num = jnp.where(lo_half, e, pltpu.roll(d, NSA_DH, 1))
            den = jnp.where(lo_half, pltpu.roll(e, NSA_DH, 1), d)
            if guard:
                den = jnp.where(den > 0.0, den, 1.0)
            outs.append(num / den)
        return outs

    q_sub = lax.broadcasted_iota(jnp.int32, (QB, ncmp), 0)
    n_lane = lax.broadcasted_iota(jnp.int32, (QB, ncmp), 1)
    ok_c = n_lane * CMP_STRIDE + (CMP_LEN - 1) <= t0 + q_sub
    s = _dot_nt(qe_scr[...], jnp.concatenate([kc_ref[0], cposf_scr[...]], axis=1))
    s = s + _tile_heads(jnp.where(ok_c, 0.0, NEG))
    e = jnp.exp(s - jnp.max(s, axis=-1, keepdims=True)) * _tile_heads(jnp.where(ok_c, 1.0, 0.0))
    e_hi = e.astype(BF16)
    e_lo = (e - e_hi.astype(F32)).astype(BF16)
    o_cmp = normalised(jnp.concatenate([_dot(e_hi[0:HALF], vc0_ref[0]),
                                        _dot(e_hi[HALF:2 * HALF], vc1_ref[0])], axis=0), True)
    imp_l = _dot(e_hi, ov_ref[...]) + _dot(e_lo, ov_ref[...])
    l_c = pltpu.roll(imp_l, NSA_DH, 1)
    imp_rows = imp_l / jnp.where(lo_half, jnp.where(l_c > 0.0, l_c, 1.0), 1.0)

    n_grp = nbs // SUBLANES
    sub8 = lax.broadcasted_iota(jnp.int32, (SUBLANES, QB), 0)
    tpos = t0 + lax.broadcasted_iota(jnp.int32, (1, QB), 1)
    cur = tpos // SEL_LEN
    j_lane = lane.astype(F32)
    for g in range(NSA_KV):
        imp = imp_rows[(g * NSA_REP) * QB:(g * NSA_REP + 1) * QB]
        for r in range(1, NSA_REP):
            imp = imp + imp_rows[(g * NSA_REP + r) * QB:(g * NSA_REP + r + 1) * QB]
        imp_t = imp.T[0:nbs, :]
        j_sub = lax.broadcasted_iota(jnp.int32, (nbs, QB), 0)
        forced = (j_sub == 0) | (j_sub == cur) | (j_sub == cur - 1)
        score = jnp.where(j_sub <= cur, imp_t + jnp.where(forced, FORCE_BONUS, 0.0), NEG)
        parts = [score[a * SUBLANES:(a + 1) * SUBLANES, :] for a in range(n_grp)]
        ranks = [jnp.zeros((SUBLANES, QB), F32) for _ in range(n_grp)]
        for i in range(nbs):
            row = jnp.broadcast_to(score[i:i + 1, :], (SUBLANES, QB))
            for a in range(n_grp):
                if a * SUBLANES > i:
                    beats = jnp.where(row >= parts[a], 1.0, 0.0)
                elif (a + 1) * SUBLANES - 1 < i:
                    beats = jnp.where(row > parts[a], 1.0, 0.0)
                else:
                    beats = jnp.where(sub8 + a * SUBLANES > i,
                                      jnp.where(row >= parts[a], 1.0, 0.0),
                                      jnp.where(row > parts[a], 1.0, 0.0))
                ranks[a] = ranks[a] + beats
        sel_t = [jnp.where(rk < float(min(SEL_TOP, nbs)), 0.0, NEG) for rk in ranks]
        if nbs < LANES:
            sel_t.append(jnp.zeros((LANES - nbs, QB), F32))
        sel_bias = jnp.concatenate(sel_t, axis=0).T
        for r in range(NSA_REP):
            h = g * NSA_REP + r
            feat = jnp.where(lo_half, sel_bias + (_alibi_slope(h) * SEL_LEN) * j_lane, const_feat(h))
            qe_scr[h * QB:(h + 1) * QB, LANES:2 * LANES] = feat.astype(BF16)

    last_tile = t0 // KT
    mrun_scr[...] = jnp.full(mrun_scr.shape, NEG, F32)

    def score_tile(kt, causal):
        k0 = pl.multiple_of(kt * KT, KT)
        kx = jnp.concatenate([ks_ref[pl.ds(k0, KT), :], posf_scr[pl.ds(k0, KT), :]], axis=1)
        s = _dot_nt(qe_scr[...], kx)
        if causal:
            qs = lax.broadcasted_iota(jnp.int32, (QB, KT), 0)
            kl = lax.broadcasted_iota(jnp.int32, (QB, KT), 1)
            s = s + _tile_heads(jnp.where(k0 + kl <= t0 + qs, 0.0, NEG))
        s_scr[kt] = s
        tile_max = s[:, 0:LANES]
        for c0 in range(LANES, KT, LANES):
            tile_max = jnp.maximum(tile_max, s[:, c0:c0 + LANES])
        mrun_scr[...] = jnp.maximum(mrun_scr[...], tile_max)

    def full_tile(kt, carry):
        score_tile(kt, False)
        return carry

    lax.fori_loop(0, last_tile, full_tile, 0)
    score_tile(last_tile, True)
    mb_scr[...] = jnp.broadcast_to(jnp.max(mrun_scr[...], axis=-1, keepdims=True), mb_scr.shape)
    acc_scr[...] = jnp.zeros(acc_scr.shape, F32)

    def value_tile(kt, carry):
        k0 = pl.multiple_of(kt * KT, KT)
        mb = mb_scr[...]
        p = jnp.exp(s_scr[kt] - jnp.concatenate([mb] * (KT // LANES), axis=1)).astype(BF16)
        acc_scr[0:HALF, :] += _dot(p[0:HALF], vs0_ref[pl.ds(k0, KT), :])
        acc_scr[HALF:2 * HALF, :] += _dot(p[HALF:2 * HALF], vs1_ref[pl.ds(k0, KT), :])
        return carry

    lax.fori_loop(0, last_tile + 1, value_tile, 0)
    o_sel = normalised(acc_scr[...], False)

    ws = pl.multiple_of(jnp.maximum(t0 - WINDOW, 0), QB)
    qs = lax.broadcasted_iota(jnp.int32, (QB, WIN_KEYS), 0)
    kl = lax.broadcasted_iota(jnp.int32, (QB, WIN_KEYS), 1)
    dist_w = (t0 - ws) + qs - kl
    ok_w = (dist_w >= 0) & (dist_w < WINDOW)
    s = _dot_nt(qe_scr[...], jnp.concatenate([kw_ref[pl.ds(ws, WIN_KEYS), :], wposf_scr[...]], axis=1))
    s = s + _tile_heads(jnp.where(ok_w, 0.0, NEG))
    p = jnp.exp(s - jnp.max(s, axis=-1, keepdims=True)).astype(BF16)
    o_win = normalised(jnp.concatenate([_dot(p[0:HALF], vw0_ref[pl.ds(ws, WIN_KEYS), :]),
                                        _dot(p[HALF:2 * HALF], vw1_ref[pl.ds(ws, WIN_KEYS), :])],
                                       axis=0), False)

    gate = jax.nn.sigmoid(g_ref[...])
    for pair in range(NSA_HEADS // 2):
        y = None
        for b, o in enumerate((o_cmp, o_sel, o_win)):
            c_e = SUBLANES + 3 * (2 * pair) + b
            c_o = SUBLANES + 3 * (2 * pair + 1) + b
            term = jnp.where(lo_half, gate[:, c_e:c_e + 1], gate[:, c_o:c_o + 1]) * o[pair]
            y = term if y is None else y + term
        y_ref[:, pair * LANES:(pair + 1) * LANES] = y.astype(y_ref.dtype)


def _nsa(pr, kc, vc0, vc1, batch, seq):
    QB = Q_BLOCK
    nq = seq // QB
    nbs = seq // SEL_LEN
    ncmp = kc.shape[1]
    cs = np.arange(ncmp) * CMP_STRIDE
    js = np.arange(nbs) * SEL_LEN
    ov = np.zeros((ncmp, LANES), np.float32)
    ov[:, :nbs] = (cs[:, None] < js[None, :] + SEL_LEN) & (cs[:, None] + CMP_LEN > js[None, :])
    ov[:, NSA_DH:] = 1.0
    tok = lambda b, i: (b * nq + i, 0)
    per_b = lambda b, i: (b, 0)
    per_b3 = lambda b, i: (b, 0, 0)
    kv_spec = pl.BlockSpec((seq, LANES), per_b)
    cmp_spec = pl.BlockSpec((1, ncmp, LANES), per_b3)
    return pl.pallas_call(
        functools.partial(_nsa_kernel, nbs),
        out_shape=jax.ShapeDtypeStruct((batch * seq, NSA_WIDTH), BF16),
        grid=(batch, nq),
        in_specs=[pl.BlockSpec((QB, NSA_HEADS * LANES), tok),
                  pl.BlockSpec((QB, LANES), tok),
                  cmp_spec, cmp_spec, cmp_spec,
                  kv_spec, kv_spec, kv_spec, kv_spec, kv_spec, kv_spec,
                  pl.BlockSpec((ncmp, LANES), lambda b, i: (0, 0))],
        out_specs=pl.BlockSpec((QB, NSA_WIDTH), tok),
        scratch_shapes=[pltpu.VMEM((NSA_ROWS, 2 * LANES), BF16),
                        pltpu.VMEM((seq, LANES), BF16),
                        pltpu.VMEM((WIN_KEYS, LANES), BF16),
                        pltpu.VMEM((ncmp, LANES), BF16),
                        pltpu.VMEM((seq // SEL_KT, NSA_ROWS, SEL_KT), F32),
                        pltpu.VMEM((NSA_ROWS, LANES), F32),
                        pltpu.VMEM((NSA_ROWS, LANES), F32),
                        pltpu.VMEM((NSA_ROWS, LANES), F32)],
        compiler_params=pltpu.CompilerParams(dimension_semantics=("parallel", "arbitrary"),
                                             vmem_limit_bytes=VMEM_LIMIT),
        name="nsa",
    )(pr["nq"], pr["gcol"], kc, vc0, vc1, pr["ks"], pr["vs0"], pr["vs1"],
      pr["kw"], pr["vw0"], pr["vw1"], jnp.asarray(ov, BF16))


def _memkv_kernel(mem_ref, g_ref, w_ref, kv_ref):
    u = _rms(mem_ref[0], g_ref[...]).astype(BF16)
    kv_ref[0] = _dot(u, w_ref[...]).astype(kv_ref.dtype)


def _memkv(mem, g_mem, w_mem_kv):
    batch, n_mem, d = mem.shape
    return pl.pallas_call(
        _memkv_kernel,
        out_shape=jax.ShapeDtypeStruct((batch, n_mem, 2 * XA_WIDTH), BF16),
        grid=(batch,),
        in_specs=[pl.BlockSpec((1, n_mem, d), lambda b: (b, 0, 0)),
                  _resident((1, d)), _resident((d, 2 * XA_WIDTH))],
        out_specs=pl.BlockSpec((1, n_mem, 2 * XA_WIDTH), lambda b: (b, 0, 0)),
        compiler_params=pltpu.CompilerParams(dimension_semantics=("parallel",),
                                             vmem_limit_bytes=VMEM_LIMIT),
        name="memkv",
    )(mem, g_mem[None, :], w_mem_kv.astype(BF16))


def _tail_kernel(x_ref, yml_ref, ynsa_ref, xq_ref, kv_ref, mg_ref, wb_ref, wo_ref, gf_ref,
                 w1_ref, w2_ref, gl_ref, out_ref):
    d = x_ref.shape[1]
    kv = kv_ref[0]
    y_xa = []
    for hh in range(XA_HEADS):
        sl = slice(hh * XA_DH, (hh + 1) * XA_DH)
        s = _dot_nt(xq_ref[:, sl], kv[:, sl]) * (XA_DH ** -0.5)
        p = jnp.exp(s - jnp.max(s, axis=-1, keepdims=True))
        l = jnp.sum(p, axis=-1, keepdims=True)
        y_xa.append((_dot(p.astype(BF16), kv[:, XA_WIDTH + hh * XA_DH:XA_WIDTH + (hh + 1) * XA_DH])
                     / l).astype(BF16))
    ys = (yml_ref[...], ynsa_ref[...], jnp.concatenate(y_xa, axis=-1))
    merged = None
    for j in range(N_BRANCH):
        term = jax.nn.sigmoid(mg_ref[:, j * d:(j + 1) * d]) * _dot(ys[j], wb_ref[j])
        merged = term if merged is None else merged + term
    h = x_ref[...] + _dot(merged.astype(BF16), wo_ref[...])
    u = _rms(h, gf_ref[...]).astype(BF16)
    acc = h
    for c0 in range(0, w1_ref.shape[1], FF_SLAB):
        a = jnp.maximum(_dot(u, w1_ref[:, c0:c0 + FF_SLAB]), 0.0)
        acc = acc + _dot((a * a).astype(BF16), w2_ref[c0:c0 + FF_SLAB, :])
    out_ref[...] = _rms(acc, gl_ref[...])


def _tail(x2, y_ml, y_nsa, pr, mem_kv, w_branch, w_out, g_ffn, w_ff1, w_ff2, g_final, seq):
    n_tok, d = x2.shape
    tm = TM_TAIL
    d_ff = w_ff1.shape[1]
    n_mem = mem_kv.shape[1]
    tiles_per_b = seq // tm
    row = lambda i: (i, 0)
    return pl.pallas_call(
        _tail_kernel,
        out_shape=jax.ShapeDtypeStruct((n_tok, d), F32),
        grid=(n_tok // tm,),
        in_specs=[pl.BlockSpec((tm, d), row),
                  pl.BlockSpec((tm, ML_WIDTH), row),
                  pl.BlockSpec((tm, NSA_WIDTH), row),
                  pl.BlockSpec((tm, XA_WIDTH), row),
                  pl.BlockSpec((1, n_mem, 2 * XA_WIDTH), lambda i: (i // tiles_per_b, 0, 0)),
                  pl.BlockSpec((tm, N_BRANCH * d), row),
                  _resident((N_BRANCH, ML_WIDTH, d)), _resident((d, d)), _resident((1, d)),
                  _resident((d, d_ff)), _resident((d_ff, d)), _resident((1, d))],
        out_specs=pl.BlockSpec((tm, d), row),
        compiler_params=pltpu.CompilerParams(dimension_semantics=("parallel",),
                                             vmem_limit_bytes=VMEM_LIMIT),
        name="tail",
    )(x2, y_ml, y_nsa, pr["xq"], mem_kv, pr["mg"], w_branch.astype(BF16), w_out.astype(BF16),
      g_ffn[None, :], w_ff1.astype(BF16), w_ff2.astype(BF16), g_final[None, :])


def _layer(x, mem, g_mix, w_in, b_in, ml_conv, ml_norm_g, cmp_pe, cmp_w1, cmp_w2, g_mem, w_mem_kv,
           w_branch, w_out, g_ffn, w_ff1, w_ff2, g_final):
    batch, seq, d = x.shape
    x2 = x.reshape(batch * seq, d)
    pr = _inproj(x2, g_mix, w_in, b_in)
    y_ml = _mlstm(pr, ml_conv, ml_norm_g, batch, seq)
    kc, vc0, vc1 = _compress(pr, cmp_pe, cmp_w1, cmp_w2, batch, seq)
    y_nsa = _nsa(pr, kc, vc0, vc1, batch, seq)
    mem_kv = _memkv(mem, g_mem, w_mem_kv)
    out = _tail(x2, y_ml, y_nsa, pr, mem_kv, w_branch, w_out, g_ffn, w_ff1, w_ff2, g_final, seq)
    return out.reshape(batch, seq, d)


def kernel(x, mem, g_mix, w_in, b_in, ml_conv, ml_norm_g, cmp_pe, cmp_w1, cmp_w2, g_mem, w_mem_kv,
           w_branch, w_out, g_ffn, w_ff1, w_ff2, g_final):
    assert g_mix.shape[0] == 1, "single-layer block"
    return _layer(x, mem, g_mix[0], w_in[0], b_in[0], ml_conv[0], ml_norm_g[0], cmp_pe[0],
                  cmp_w1[0], cmp_w2[0], g_mem[0], w_mem_kv[0], w_branch[0], w_out[0], g_ffn[0],
                  w_ff1[0], w_ff2[0], g_final)
```

```python
import functools

import numpy as np
import jax
import jax.numpy as jnp
from jax import lax
from jax.experimental import pallas as pl
from jax.experimental.pallas import tpu as pltpu

F32 = jnp.float32
BF16 = jnp.bfloat16

EPS = 1e-6
NEG = -1e30
ML_HEADS = 4
ML_DH = 128
ML_WIDTH = ML_HEADS * ML_DH
ML_CONV = 4
ML_CHUNK = 128
NSA_HEADS = 8
NSA_KV = 2
NSA_REP = NSA_HEADS // NSA_KV
NSA_DH = 64
NSA_WIDTH = NSA_HEADS * NSA_DH
NSA_KV_WIDTH = NSA_KV * NSA_DH
CMP_LEN = 32
CMP_STRIDE = 16
CMP_HIDDEN = 256
SEL_LEN = 64
SEL_TOP = 16
WINDOW = 512
Q_BLOCK = 128
FORCE_BONUS = 1e3
SEL_KT = 256
XA_HEADS = 4
XA_DH = 128
XA_WIDTH = XA_HEADS * XA_DH
N_BRANCH = 3

LANES = 128
SUBLANES = 8
VMEM_LIMIT = 56 * 1024 * 1024

TM_PROJ = 256
TM_TAIL = 256
FF_SLAB = 1024

_NT = (((1,), (1,)), ((), ()))
_TN = (((0,), (0,)), ((), ()))


def _dot(a, b):
    return jnp.dot(a, b, preferred_element_type=F32)


def _dot_nt(a, b):
    return lax.dot_general(a, b, _NT, preferred_element_type=F32)


def _resident(shape):
    nd = len(shape)
    return pl.BlockSpec(shape, lambda *_: (0,) * nd, pipeline_mode=pl.Buffered(1))


def _rms(x, g):
    return x * lax.rsqrt(jnp.mean(x * x, axis=-1, keepdims=True) + EPS) * g


_PROJ_COLS = (
    ("qk", 2 * ML_WIDTH, F32),
    ("o", ML_WIDTH, F32),
    ("mg", None, F32),
    ("gcol", LANES, F32),
    ("kc", NSA_KV_WIDTH, F32),
    ("vc", NSA_KV_WIDTH, F32),
    ("v", ML_WIDTH, BF16),
    ("nq", NSA_HEADS * LANES, BF16),
    ("xq", XA_WIDTH, BF16),
    ("ks", NSA_KV_WIDTH, BF16),
    ("kw", NSA_KV_WIDTH, BF16),
    ("vs0", LANES, BF16),
    ("vs1", LANES, BF16),
    ("vw0", LANES, BF16),
    ("vw1", LANES, BF16),
)
_PROJ_ROWS = (
    ("smallT", SUBLANES, F32),
)
_PROJ_CHUNK = 512


def _proj_layout(d_model):
    cols, off = [], 0
    for name, width, dt in _PROJ_COLS:
        width = N_BRANCH * d_model if width is None else width
        cols.append((name, off, width, dt))
        off += width
    rows, roff = [], 0
    for name, r, dt in _PROJ_ROWS:
        rows.append((name, roff, r, dt))
        roff += r
    return cols, off, rows, roff


def _inproj_kernel(cols, rows, x_ref, g_ref, w_ref, b_ref, wt_ref, bt_ref, *out_refs):
    u = _rms(x_ref[...], g_ref[...]).astype(BF16)
    col_refs = out_refs[:len(cols)]
    row_refs = out_refs[len(cols):]
    for (name, off, width, dt), o_ref in zip(cols, col_refs):
        for c0 in range(0, width, _PROJ_CHUNK):
            cw = min(_PROJ_CHUNK, width - c0)
            acc = _dot(u, w_ref[:, off + c0:off + c0 + cw]) + b_ref[:, off + c0:off + c0 + cw]
            o_ref[:, c0:c0 + cw] = acc.astype(dt)
    t = _dot_nt(wt_ref[...], u) + bt_ref[...]
    for (name, roff, r, dt), o_ref in zip(rows, row_refs):
        o_ref[...] = t[roff:roff + r, :].astype(dt)


def _split_w_in(w_in, b_in):
    widths = (ML_WIDTH, ML_WIDTH, ML_WIDTH, ML_WIDTH, ML_HEADS, ML_HEADS,
              NSA_WIDTH, NSA_KV_WIDTH, NSA_KV_WIDTH, NSA_KV_WIDTH, NSA_KV_WIDTH, NSA_KV_WIDTH,
              NSA_KV_WIDTH, 3 * NSA_HEADS, XA_WIDTH, w_in.shape[1])
    names = ("ml_q", "ml_k", "ml_v", "ml_o", "ml_i", "ml_f", "ns_q", "ns_kc", "ns_vc", "ns_ks",
             "ns_vs", "ns_kw", "ns_vw", "ns_g", "xa_q", "mg")
    out, off = {}, 0
    for n, wd in zip(names, widths):
        end = w_in.shape[1] if n == "mg" else off + wd
        out[n] = (w_in[:, off:end], b_in[off:end])
        off = end
    return out


def _inproj(x2, g_mix, w_in, b_in):
    n_tok, d = x2.shape
    cols, ncols, rows, nrows = _proj_layout(d)
    p = _split_w_in(w_in, b_in)

    def pad_cols(w, b, width):
        return (jnp.pad(w, ((0, 0), (0, width - w.shape[1]))), jnp.pad(b, (0, width - b.shape[0])))

    wq, bq = p["ns_q"]
    wq = wq.reshape(d, NSA_HEADS, NSA_DH)
    bq = bq.reshape(NSA_HEADS, NSA_DH)
    wq_slots = jnp.zeros((d, NSA_HEADS, NSA_KV, NSA_DH), w_in.dtype)
    bq_slots = jnp.zeros((NSA_HEADS, NSA_KV, NSA_DH), w_in.dtype)
    for h in range(NSA_HEADS):
        wq_slots = wq_slots.at[:, h, h // NSA_REP].set(wq[:, h])
        bq_slots = bq_slots.at[h, h // NSA_REP].set(bq[h])
    gate_w = jnp.concatenate([p["ml_i"][0], p["ml_f"][0]], axis=1)
    gate_b = jnp.concatenate([p["ml_i"][1], p["ml_f"][1]])
    small_w = jnp.concatenate([gate_w, p["ns_g"][0]], axis=1)
    small_b = jnp.concatenate([gate_b, p["ns_g"][1]])

    def with_ones(wb, g):
        w, b = wb
        w = jnp.pad(w[:, g * NSA_DH:(g + 1) * NSA_DH], ((0, 0), (0, LANES - NSA_DH)))
        b = jnp.concatenate([b[g * NSA_DH:(g + 1) * NSA_DH], jnp.ones((LANES - NSA_DH,), b.dtype)])
        return w, b

    pieces = {
        "qk": (jnp.concatenate([p["ml_q"][0], p["ml_k"][0]], axis=1),
               jnp.concatenate([p["ml_q"][1], p["ml_k"][1]])),
        "o": p["ml_o"], "mg": p["mg"],
        "gcol": pad_cols(small_w, small_b, LANES),
        "kc": p["ns_kc"], "vc": p["ns_vc"], "v": p["ml_v"],
        "nq": (wq_slots.reshape(d, NSA_HEADS * LANES), bq_slots.reshape(NSA_HEADS * LANES)),
        "xq": p["xa_q"], "ks": p["ns_ks"], "kw": p["ns_kw"],
        "vs0": with_ones(p["ns_vs"], 0), "vs1": with_ones(p["ns_vs"], 1),
        "vw0": with_ones(p["ns_vw"], 0), "vw1": with_ones(p["ns_vw"], 1),
    }
    w_cols = jnp.concatenate([pieces[name][0] for name, *_ in cols], axis=1).astype(BF16)
    b_cols = jnp.concatenate([pieces[name][1] for name, *_ in cols])[None, :]
    w_rows = gate_w.T.astype(BF16)
    b_rows = gate_b[:, None]

    tm = TM_PROJ
    out_shape = ([jax.ShapeDtypeStruct((n_tok, width), dt) for _, _, width, dt in cols]
                 + [jax.ShapeDtypeStruct((r, n_tok), dt) for _, _, r, dt in rows])
    out_specs = ([pl.BlockSpec((tm, width), lambda i: (i, 0)) for _, _, width, _ in cols]
                 + [pl.BlockSpec((r, tm), lambda i: (0, i)) for _, _, r, _ in rows])
    outs = pl.pallas_call(
        functools.partial(_inproj_kernel, cols, rows),
        out_shape=out_shape,
        grid=(n_tok // tm,),
        in_specs=[pl.BlockSpec((tm, d), lambda i: (i, 0)),
                  _resident((1, d)), _resident((d, ncols)), _resident((1, ncols)),
                  _resident((nrows, d)), _resident((nrows, 1))],
        out_specs=out_specs,
        compiler_params=pltpu.CompilerParams(dimension_semantics=("parallel",),
                                             vmem_limit_bytes=VMEM_LIMIT),
        name="inproj",
    )(x2, g_mix[None, :], w_cols, b_cols, w_rows, b_rows)
    names = [c[0] for c in cols] + [r[0] for r in rows]
    return dict(zip(names, outs))


def _mlstm_kernel(qk_ref, v_ref, o_ref, gcol_ref, grow_ref, conv_ref, ng_ref, y_ref,
                  xbuf, c_scr, n_scr, m_scr):
    L = ML_CHUNK
    halo = SUBLANES

    @pl.when(pl.program_id(1) == 0)
    def _():
        xbuf[0:halo, :] = jnp.zeros((halo, xbuf.shape[1]), F32)
        c_scr[...] = jnp.zeros(c_scr.shape, F32)
        n_scr[...] = jnp.zeros(n_scr.shape, F32)
        m_scr[...] = jnp.zeros(m_scr.shape, F32)

    xbuf[halo:halo + L, :] = qk_ref[...]
    w = conv_ref[...]
    acc = xbuf[halo:halo + L, :] * w[0:1, :]
    for j in range(1, ML_CONV):
        acc = acc + xbuf[halo - j:halo - j + L, :] * w[j:j + 1, :]
    xbuf[0:halo, :] = xbuf[L:L + halo, :]
    qk = acc * jax.nn.sigmoid(acc)

    gcol = gcol_ref[...]
    grow = grow_ref[...]
    r_i = lax.broadcasted_iota(jnp.int32, (L, L), 0)
    c_i = lax.broadcasted_iota(jnp.int32, (L, L), 1)
    causal = c_i <= r_i
    tril = causal.astype(F32)
    triu = (r_i <= c_i).astype(F32)
    b_col = jnp.dot(tril, jax.nn.log_sigmoid(gcol), precision=lax.Precision.HIGHEST,
                    preferred_element_type=F32)
    b_row = jnp.dot(jax.nn.log_sigmoid(grow), triu, precision=lax.Precision.HIGHEST,
                    preferred_element_type=F32)

    for hh in range(ML_HEADS):
        sl = slice(hh * ML_DH, (hh + 1) * ML_DH)
        q = qk[:, sl].astype(BF16)
        k = qk[:, ML_WIDTH + hh * ML_DH:ML_WIDTH + (hh + 1) * ML_DH] * (ML_DH ** -0.5)
        v = v_ref[:, sl]
        li_c = gcol[:, hh:hh + 1]
        b_c = b_col[:, ML_HEADS + hh:ML_HEADS + hh + 1]
        li_r = grow[hh:hh + 1, :]
        b_r = b_row[ML_HEADS + hh:ML_HEADS + hh + 1, :]
        g = b_c[L - 1:L, :]
        m_prev = m_scr[hh:hh + 1, 0:1]
        c_prev = c_scr[hh]
        n_prev = n_scr[hh:hh + 1, :]

        log_d = jnp.where(causal, b_c - b_r + li_r, -jnp.inf)
        inter_log = b_c + m_prev
        m_t = jnp.maximum(inter_log, jnp.max(log_d, axis=-1, keepdims=True))
        s = _dot_nt(q, k.astype(BF16)) * jnp.exp(log_d - m_t)
        sc = jnp.exp(inter_log - m_t)
        num = _dot(s.astype(BF16), v) + sc * _dot(q, c_prev.astype(BF16))
        qn = jnp.sum(qk[:, sl] * n_prev, axis=-1, keepdims=True)
        den = jnp.sum(s, axis=-1, keepdims=True) + sc * qn
        h = num / jnp.maximum(jnp.abs(den), jnp.exp(-m_t))

        w_end = g - b_c + li_c
        m_loc = jnp.max(w_end, axis=0, keepdims=True)
        ke = k * jnp.exp(w_end - m_loc)
        a_c = lax.dot_general(ke.astype(BF16), v, _TN, preferred_element_type=F32)
        n_a = jnp.sum(ke, axis=0, keepdims=True)
        m_new = jnp.maximum(g + m_prev, m_loc)
        a = jnp.exp(g + m_prev - m_new)
        bb = jnp.exp(m_loc - m_new)
        c_scr[hh] = a * c_prev + bb * a_c
        n_scr[hh:hh + 1, :] = a * n_prev + bb * n_a
        m_scr[hh:hh + 1, :] = jnp.broadcast_to(m_new, (1, m_scr.shape[1]))

        hn = h * lax.rsqrt(jnp.mean(h * h, axis=-1, keepdims=True) + EPS)
        y = jax.nn.sigmoid(o_ref[:, sl]) * hn * ng_ref[:, sl]
        y_ref[:, sl] = y.astype(y_ref.dtype)


def _mlstm(pr, ml_conv, ml_norm_g, batch, seq):
    L = ML_CHUNK
    nc = seq // L
    tok = lambda b, c: (b * nc + c, 0)
    return pl.pallas_call(
        _mlstm_kernel,
        out_shape=jax.ShapeDtypeStruct((batch * seq, ML_WIDTH), BF16),
        grid=(batch, nc),
        in_specs=[pl.BlockSpec((L, 2 * ML_WIDTH), tok),
                  pl.BlockSpec((L, ML_WIDTH), tok),
                  pl.BlockSpec((L, ML_WIDTH), tok),
                  pl.BlockSpec((L, LANES), tok),
                  pl.BlockSpec((SUBLANES, L), lambda b, c: (0, b * nc + c)),
                  pl.BlockSpec((ML_CONV, 2 * ML_WIDTH), lambda b, c: (0, 0)),
                  pl.BlockSpec((1, ML_WIDTH), lambda b, c: (0, 0))],
        out_specs=pl.BlockSpec((L, ML_WIDTH), tok),
        scratch_shapes=[pltpu.VMEM((L + SUBLANES, 2 * ML_WIDTH), F32),
                        pltpu.VMEM((ML_HEADS, ML_DH, ML_DH), F32),
                        pltpu.VMEM((SUBLANES, ML_DH), F32),
                        pltpu.VMEM((SUBLANES, LANES), F32)],
        compiler_params=pltpu.CompilerParams(dimension_semantics=("parallel", "arbitrary"),
                                             vmem_limit_bytes=VMEM_LIMIT),
        name="mlstm",
    )(pr["qk"], pr["v"], pr["o"], pr["gcol"], pr["smallT"], ml_conv, ml_norm_g[None, :])


def _gelu_tanh(x):
    return 0.5 * x * (1.0 + jnp.tanh(np.sqrt(2.0 / np.pi) * (x + 0.044715 * (x * x * x))))


def _compress_kernel(rk_ref, rv_ref, pe_ref, w1_ref, w2k_ref, w2v_ref, kc_ref, vc0_ref, vc1_ref):
    nrow = rk_ref.shape[1]

    def hidden(r_ref, which):
        r = r_ref[0]
        lo = _dot((r + pe_ref[which, 0:1, :]).astype(BF16), w1_ref[which, 0])
        hi = _dot((r + pe_ref[which, 1:2, :]).astype(BF16), w1_ref[which, 1])
        return _gelu_tanh(lo + pltpu.roll(hi, nrow - 1, 0)).astype(BF16)

    kc_ref[0] = _dot(hidden(rk_ref, 0), w2k_ref[...]).astype(kc_ref.dtype)
    hv = hidden(rv_ref, 1)
    ones_half = lax.broadcasted_iota(jnp.int32, (nrow, LANES), 1) >= NSA_DH
    for g, o_ref in enumerate((vc0_ref, vc1_ref)):
        o_ref[0] = jnp.where(ones_half, 1.0, _dot(hv, w2v_ref[g])).astype(o_ref.dtype)


def _compress(pr, cmp_pe, cmp_w1, cmp_w2, batch, seq):
    per_row = CMP_STRIDE
    nrow = seq // per_row
    width = per_row * NSA_KV_WIDTH
    half = CMP_LEN // per_row
    eye = jnp.eye(NSA_KV, dtype=F32)
    w1 = cmp_w1.reshape(2, half, per_row, NSA_DH, CMP_HIDDEN)
    w1e = jnp.einsum("khldc,gj->khlgdjc", w1, eye).reshape(2, half, width, NSA_KV * CMP_HIDDEN)
    w2e = jnp.einsum("khd,gj->kghjd", cmp_w2, eye).reshape(2, NSA_KV * CMP_HIDDEN, NSA_KV_WIDTH)
    pe = jnp.broadcast_to(cmp_pe.reshape(2, half, per_row, 1, NSA_DH),
                          (2, half, per_row, NSA_KV, NSA_DH)).reshape(2, half, width)
    rk = pr["kc"].reshape(batch, nrow, width)
    rv = pr["vc"].reshape(batch, nrow, width)
    hid = NSA_KV * CMP_HIDDEN
    w2v = jnp.stack([jnp.pad(w2e[1][:, g * NSA_DH:(g + 1) * NSA_DH], ((0, 0), (0, LANES - NSA_DH)))
                     for g in range(NSA_KV)])
    out_blk = pl.BlockSpec((1, nrow, LANES), lambda b: (b, 0, 0))
    return pl.pallas_call(
        _compress_kernel,
        out_shape=(jax.ShapeDtypeStruct((batch, nrow, LANES), BF16),) * 3,
        grid=(batch,),
        in_specs=[pl.BlockSpec((1, nrow, width), lambda b: (b, 0, 0)),
                  pl.BlockSpec((1, nrow, width), lambda b: (b, 0, 0)),
                  _resident((2, half, width)),
                  _resident((2, half, width, hid)),
                  _resident((hid, NSA_KV_WIDTH)),
                  _resident((NSA_KV, hid, LANES))],
        out_specs=(out_blk, out_blk, out_blk),
        compiler_params=pltpu.CompilerParams(dimension_semantics=("parallel",),
                                             vmem_limit_bytes=VMEM_LIMIT),
        name="compress",
    )(rk, rv, pe, w1e.astype(BF16), w2e[0].astype(BF16), w2v.astype(BF16))


def _alibi_slope(h):
    return float(2.0 ** (-8.0 * (h + 1) / NSA_HEADS))


def _nsa_kernel_v2(nbs, q_ref, kc_ref, vct_ref, ks_ref, vst_ref, kw_ref, vwt_ref, gt_ref, ovt_ref,
                y_ref, selx, qs_scr, ab_scr, m_scr, l_scr, acc_scr):
    QB = Q_BLOCK
    KT = SEL_KT
    t0 = pl.program_id(1) * QB
    ncmp = kc_ref.shape[1]
    gates = jax.nn.sigmoid(gt_ref[...])
    q_lane = lax.broadcasted_iota(jnp.int32, (1, QB), 1)
    tpos = t0 + q_lane
    qs_scr[...] = q_ref[...] * 0.125

    @pl.when(pl.program_id(1) == 0)
    def _():
        rel = (lax.broadcasted_iota(jnp.int32, (KT, QB), 1)
               - lax.broadcasted_iota(jnp.int32, (KT, QB), 0)).astype(F32)
        for h in range(NSA_HEADS):
            ab_scr[h] = -_alibi_slope(h) * rel

    def head_q(h):
        return qs_scr[:, h * LANES:(h + 1) * LANES]

    def group_rows(x, g):
        return x[g * NSA_DH:(g + 1) * NSA_DH, :]

    n_sub = lax.broadcasted_iota(jnp.int32, (ncmp, QB), 0)
    dist_c = tpos - (n_sub * CMP_STRIDE + (CMP_LEN - 1))
    ok_c = dist_c >= 0
    dist_cf = dist_c.astype(F32)
    kc = kc_ref[0]
    vct = vct_ref[0]
    o_cmp = []
    for g in range(NSA_KV):
        psum = jnp.zeros((ncmp, QB), F32)
        for r in range(NSA_REP):
            h = g * NSA_REP + r
            s = _dot_nt(kc, head_q(h)) - _alibi_slope(h) * dist_cf
            s = jnp.where(ok_c, s, NEG)
            e = jnp.exp(s - jnp.max(s, axis=0, keepdims=True))
            p = jnp.where(ok_c, e / jnp.sum(e, axis=0, keepdims=True), 0.0)
            psum = psum + p
            o_cmp.append(group_rows(_dot(vct, p.astype(BF16)), g))
        imp = jnp.dot(ovt_ref[...], psum, precision=lax.Precision.HIGHEST,
                      preferred_element_type=F32)
        j_sub = lax.broadcasted_iota(jnp.int32, (nbs, QB), 0)
        cur = tpos // SEL_LEN
        forced = (j_sub == 0) | (j_sub == cur) | (j_sub == cur - 1)
        score = jnp.where(j_sub <= cur, imp + jnp.where(forced, FORCE_BONUS, 0.0), NEG)
        n_grp = nbs // SUBLANES
        parts = [score[a * SUBLANES:(a + 1) * SUBLANES, :] for a in range(n_grp)]
        ranks = [jnp.zeros((SUBLANES, QB), F32) for _ in range(n_grp)]
        sub8 = lax.broadcasted_iota(jnp.int32, (SUBLANES, QB), 0)
        for i in range(nbs):
            row = jnp.broadcast_to(score[i:i + 1, :], (SUBLANES, QB))
            for a in range(n_grp):
                if a * SUBLANES > i:
                    beats = jnp.where(row >= parts[a], 1.0, 0.0)
                elif (a + 1) * SUBLANES - 1 < i:
                    beats = jnp.where(row > parts[a], 1.0, 0.0)
                else:
                    beats = jnp.where(sub8 + a * SUBLANES > i,
                                      jnp.where(row >= parts[a], 1.0, 0.0),
                                      jnp.where(row > parts[a], 1.0, 0.0))
                ranks[a] = ranks[a] + beats
        for a in range(n_grp):
            sel_bias = jnp.where(ranks[a] < float(min(SEL_TOP, nbs)), 0.0, NEG)
            for jj in range(SUBLANES):
                j = a * SUBLANES + jj
                selx[g, j * SUBLANES:(j + 1) * SUBLANES, :] = jnp.broadcast_to(
                    sel_bias[jj:jj + 1, :], (SUBLANES, QB))

    blocks_per_tile = KT // SEL_LEN
    m_scr[...] = jnp.full(m_scr.shape, NEG, F32)
    l_scr[...] = jnp.zeros(l_scr.shape, F32)
    acc_scr[...] = jnp.zeros(acc_scr.shape, F32)
    last_tile = t0 // KT

    def sel_tile(kt, causal):
        k0 = pl.multiple_of(kt * KT, KT)
        ktile = ks_ref[pl.ds(k0, KT), :]
        row0 = pl.multiple_of(kt * (blocks_per_tile * SUBLANES), SUBLANES)
        if causal:
            k_sub = lax.broadcasted_iota(jnp.int32, (KT, QB), 0)
            visible = (q_lane - k_sub) >= (k0 - t0)
        for g in range(NSA_KV):
            sel = [selx[g, pl.ds(row0 + jj * SUBLANES, SUBLANES), :] for jj in range(blocks_per_tile)]
            vt = vst_ref[g * NSA_DH:(g + 1) * NSA_DH, pl.ds(k0, KT)]
            for r in range(NSA_REP):
                h = g * NSA_REP + r
                shift = _alibi_slope(h) * k0.astype(F32)
                bias = jnp.concatenate([jnp.tile(sb + shift, (SEL_LEN // SUBLANES, 1)) for sb in sel],
                                       axis=0)
                s = _dot_nt(ktile, head_q(h)) + ab_scr[h] + bias
                if causal:
                    s = jnp.where(visible, s, NEG)
                m_old = m_scr[h]
                m_new = jnp.maximum(m_old, jnp.max(s, axis=0, keepdims=True))
                alpha = jnp.exp(m_old - m_new)
                p = jnp.exp(s - m_new)
                l_scr[h] = alpha * l_scr[h] + jnp.sum(p, axis=0, keepdims=True)
                acc_scr[h] = alpha * acc_scr[h] + _dot(vt, p.astype(BF16))
                m_scr[h] = m_new

    def full_tile(kt, carry):
        sel_tile(kt, False)
        return carry

    lax.fori_loop(0, last_tile, full_tile, 0)
    sel_tile(last_tile, True)
    o_sel = [acc_scr[h] / l_scr[h] for h in range(NSA_HEADS)]

    WK = WINDOW + QB
    ws = pl.multiple_of(jnp.maximum(t0 - WINDOW, 0), QB)
    kw_sub = lax.broadcasted_iota(jnp.int32, (WK, QB), 0)
    dist_w = (t0 - ws) + q_lane - kw_sub
    ok_w = (dist_w >= 0) & (dist_w < WINDOW)
    dist_wf = dist_w.astype(F32)
    kwin = kw_ref[pl.ds(ws, WK), :]
    vwin = vwt_ref[:, pl.ds(ws, WK)]
    o_win = []
    for g in range(NSA_KV):
        for r in range(NSA_REP):
            h = g * NSA_REP + r
            s = _dot_nt(kwin, head_q(h)) - _alibi_slope(h) * dist_wf
            s = jnp.where(ok_w, s, NEG)
            p = jnp.exp(s - jnp.max(s, axis=0, keepdims=True))
            l = jnp.sum(p, axis=0, keepdims=True)
            o_win.append(group_rows(_dot(vwin, p.astype(BF16)), g) / l)

    outs = []
    for h in range(NSA_HEADS):
        g0 = SUBLANES + 3 * h
        outs.append(gates[g0:g0 + 1, :] * o_cmp[h] + gates[g0 + 1:g0 + 2, :] * o_sel[h]
                    + gates[g0 + 2:g0 + 3, :] * o_win[h])
    for pair in range(NSA_HEADS // 2):
        both = jnp.concatenate([outs[2 * pair], outs[2 * pair + 1]], axis=0)
        y_ref[:, pair * LANES:(pair + 1) * LANES] = both.T.astype(y_ref.dtype)


def _nsa_v2(pr, kc, vct, batch, seq):
    QB = Q_BLOCK
    nq = seq // QB
    nbs = seq // SEL_LEN
    ncmp = kc.shape[1]
    cs = np.arange(ncmp) * CMP_STRIDE
    js = np.arange(nbs) * SEL_LEN
    ovt = jnp.asarray(((cs[None, :] < js[:, None] + SEL_LEN)
                       & (cs[None, :] + CMP_LEN > js[:, None])).astype(np.float32))
    per_b = lambda b, i: (b, 0)
    return pl.pallas_call(
        functools.partial(_nsa_kernel, nbs),
        out_shape=jax.ShapeDtypeStruct((batch * seq, NSA_WIDTH), BF16),
        grid=(batch, nq),
        in_specs=[pl.BlockSpec((QB, NSA_HEADS * LANES), lambda b, i: (b * nq + i, 0)),
                  pl.BlockSpec((1, ncmp, NSA_KV_WIDTH), lambda b, i: (b, 0, 0)),
                  pl.BlockSpec((1, NSA_KV_WIDTH, ncmp), lambda b, i: (b, 0, 0)),
                  pl.BlockSpec((seq, NSA_KV_WIDTH), per_b),
                  pl.BlockSpec((NSA_KV_WIDTH, seq), lambda b, i: (0, b)),
                  pl.BlockSpec((seq, NSA_KV_WIDTH), per_b),
                  pl.BlockSpec((NSA_KV_WIDTH, seq), lambda b, i: (0, b)),
                  pl.BlockSpec((32, QB), lambda b, i: (0, b * nq + i)),
                  pl.BlockSpec((nbs, ncmp), lambda b, i: (0, 0))],
        out_specs=pl.BlockSpec((QB, NSA_WIDTH), lambda b, i: (b * nq + i, 0)),
        scratch_shapes=[pltpu.VMEM((NSA_KV, nbs * SUBLANES, QB), F32),
                        pltpu.VMEM((QB, NSA_HEADS * LANES), BF16),
                        pltpu.VMEM((NSA_HEADS, SEL_KT, QB), F32),
                        pltpu.VMEM((NSA_HEADS, 1, QB), F32),
                        pltpu.VMEM((NSA_HEADS, 1, QB), F32),
                        pltpu.VMEM((NSA_HEADS, NSA_DH, QB), F32)],
        compiler_params=pltpu.CompilerParams(dimension_semantics=("parallel", "arbitrary"),
                                             vmem_limit_bytes=VMEM_LIMIT),
        name="nsa",
    )(pr["nq"], kc, vct, pr["ks"], pr["vsT"], pr["kw"], pr["vwT"], pr["smallT"], ovt)


NSA_ROWS = NSA_HEADS * Q_BLOCK
WIN_KEYS = WINDOW + Q_BLOCK
F_SEL_BLOCK = 0
F_SEL_OFF = SEL_LEN
F_WIN_OFF = SEL_LEN + 1
F_WIN_BLK = SEL_LEN + 2
F_CMP = SEL_LEN + 3


def _tile_heads(x):
    return jnp.concatenate([x] * NSA_HEADS, axis=0)


def _nsa_kernel(nbs, q_ref, g_ref, kc_ref, vc0_ref, vc1_ref, ks_ref, vs0_ref, vs1_ref,
                kw_ref, vw0_ref, vw1_ref, ov_ref, y_ref,
                qe_scr, posf_scr, wposf_scr, cposf_scr, s_scr, mrun_scr, mb_scr, acc_scr):
    QB = Q_BLOCK
    KT = SEL_KT
    HALF = NSA_REP * QB
    t0 = pl.program_id(1) * QB
    ncmp = kc_ref.shape[1]
    lane = lax.broadcasted_iota(jnp.int32, (1, LANES), 1)
    lo_half = lane < NSA_DH

    @pl.when(pl.program_id(1) == 0)
    def _():
        seq = posf_scr.shape[0]
        k = lax.broadcasted_iota(jnp.int32, (seq, LANES), 0)
        c = lax.broadcasted_iota(jnp.int32, (seq, LANES), 1)
        posf = jnp.where(c < SEL_LEN, jnp.where(k // SEL_LEN == c, 1.0, 0.0),
                         jnp.where(c == F_SEL_OFF, (k % SEL_LEN).astype(F32), 0.0))
        posf_scr[...] = posf.astype(BF16)
        k = lax.broadcasted_iota(jnp.int32, (WIN_KEYS, LANES), 0)
        c = lax.broadcasted_iota(jnp.int32, (WIN_KEYS, LANES), 1)
        wposf = jnp.where(c == F_WIN_OFF, (k % SEL_LEN).astype(F32),
                          jnp.where(c == F_WIN_BLK, (k // SEL_LEN).astype(F32), 0.0))
        wposf_scr[...] = wposf.astype(BF16)
        k = lax.broadcasted_iota(jnp.int32, (ncmp, LANES), 0)
        c = lax.broadcasted_iota(jnp.int32, (ncmp, LANES), 1)
        cposf_scr[...] = jnp.where(c == F_CMP, k.astype(F32), 0.0).astype(BF16)

    def const_feat(h):
        s = _alibi_slope(h)
        return jnp.where((lane == F_SEL_OFF) | (lane == F_WIN_OFF), s,
                         jnp.where(lane == F_WIN_BLK, SEL_LEN * s,
                                   jnp.where(lane == F_CMP, CMP_STRIDE * s, 0.0)))

    for h in range(NSA_HEADS):
        rows = slice(h * QB, (h + 1) * QB)
        qe_scr[rows, 0:LANES] = q_ref[:, h * LANES:(h + 1) * LANES] * 0.125
        qe_scr[rows, LANES:2 * LANES] = jnp.broadcast_to(const_feat(h), (QB, LANES)).astype(BF16)

    def normalised(o, guard):
        outs = []
        for pair in range(NSA_HEADS // 2):
            e = o[(2 * pair) * QB:(2 * pair + 1) * QB]
            d = o[(2 * pair + 1) * QB:(2 * pair + 2) * QB]
            num = jnp.where(lo_half, e, pltpu.roll(d, NSA_DH, 1))
            den = jnp.where(lo_half, pltpu.roll(e, NSA_DH, 1), d)
            if guard:
                den = jnp.where(den > 0.0, den, 1.0)
            outs.append(num / den)
        return outs

    q_sub = lax.broadcasted_iota(jnp.int32, (QB, ncmp), 0)
    n_lane = lax.broadcasted_iota(jnp.int32, (QB, ncmp), 1)
    ok_c = n_lane * CMP_STRIDE + (CMP_LEN - 1) <= t0 + q_sub
    s = _dot_nt(qe_scr[...], jnp.concatenate([kc_ref[0], cposf_scr[...]], axis=1))
    s = s + _tile_heads(jnp.where(ok_c, 0.0, NEG))
    e = jnp.exp(s - jnp.max(s, axis=-1, keepdims=True)) * _tile_heads(jnp.where(ok_c, 1.0, 0.0))
    e_bf = e.astype(BF16)
    oc = jnp.concatenate([_dot(e_bf[0:HALF], vc0_ref[0]), _dot(e_bf[HALF:2 * HALF], vc1_ref[0])],
                         axis=0)
    o_cmp = normalised(oc, True)
    l_c = jnp.where(lo_half, pltpu.roll(oc, NSA_DH, 1), oc)
    inv_l = 1.0 / jnp.where(l_c > 0.0, l_c, 1.0)
    p_c = e * jnp.concatenate([inv_l] * (ncmp // LANES), axis=1)

    ws = pl.multiple_of(jnp.maximum(t0 - WINDOW, 0), QB)
    qs = lax.broadcasted_iota(jnp.int32, (QB, WIN_KEYS), 0)
    kl = lax.broadcasted_iota(jnp.int32, (QB, WIN_KEYS), 1)
    dist_w = (t0 - ws) + qs - kl
    ok_w = (dist_w >= 0) & (dist_w < WINDOW)
    s = _dot_nt(qe_scr[...], jnp.concatenate([kw_ref[pl.ds(ws, WIN_KEYS), :], wposf_scr[...]], axis=1))
    s = s + _tile_heads(jnp.where(ok_w, 0.0, NEG))
    p = jnp.exp(s - jnp.max(s, axis=-1, keepdims=True)).astype(BF16)
    o_win = normalised(jnp.concatenate([_dot(p[0:HALF], vw0_ref[pl.ds(ws, WIN_KEYS), :]),
                                        _dot(p[HALF:2 * HALF], vw1_ref[pl.ds(ws, WIN_KEYS), :])],
                                       axis=0), False)

    n_grp = nbs // SUBLANES
    sub8 = lax.broadcasted_iota(jnp.int32, (SUBLANES, QB), 0)
    tpos = t0 + lax.broadcasted_iota(jnp.int32, (1, QB), 1)
    cur = tpos // SEL_LEN
    j_lane = lane.astype(F32)
    for g in range(NSA_KV):
        psum = p_c[(g * NSA_REP) * QB:(g * NSA_REP + 1) * QB]
        for r in range(1, NSA_REP):
            psum = psum + p_c[(g * NSA_REP + r) * QB:(g * NSA_REP + r + 1) * QB]
        p_hi = psum.astype(BF16)
        p_lo = (psum - p_hi.astype(F32)).astype(BF16)
        imp = _dot(p_hi, ov_ref[...]) + _dot(p_lo, ov_ref[...])
        imp_t = imp.T[0:nbs, :]
        j_sub = lax.broadcasted_iota(jnp.int32, (nbs, QB), 0)
        forced = (j_sub == 0) | (j_sub == cur) | (j_sub == cur - 1)
        score = jnp.where(j_sub <= cur, imp_t + jnp.where(forced, FORCE_BONUS, 0.0), NEG)
        parts = [score[a * SUBLANES:(a + 1) * SUBLANES, :] for a in range(n_grp)]
        ranks = [jnp.zeros((SUBLANES, QB), F32) for _ in range(n_grp)]
        for i in range(nbs):
            row = jnp.broadcast_to(score[i:i + 1, :], (SUBLANES, QB))
            for a in range(n_grp):
                if a * SUBLANES > i:
                    beats = jnp.where(row >= parts[a], 1.0, 0.0)
                elif (a + 1) * SUBLANES - 1 < i:
                    beats = jnp.where(row > parts[a], 1.0, 0.0)
                else:
                    beats = jnp.where(sub8 + a * SUBLANES > i,
                                      jnp.where(row >= parts[a], 1.0, 0.0),
                                      jnp.where(row > parts[a], 1.0, 0.0))
                ranks[a] = ranks[a] + beats
        sel_t = [jnp.where(rk < float(min(SEL_TOP, nbs)), 0.0, NEG) for rk in ranks]
        if nbs < LANES:
            sel_t.append(jnp.zeros((LANES - nbs, QB), F32))
        sel_bias = jnp.concatenate(sel_t, axis=0).T
        for r in range(NSA_REP):
            h = g * NSA_REP + r
            feat = jnp.where(lo_half, sel_bias + (_alibi_slope(h) * SEL_LEN) * j_lane, const_feat(h))
            qe_scr[h * QB:(h + 1) * QB, LANES:2 * LANES] = feat.astype(BF16)

    last_tile = t0 // KT
    mrun_scr[...] = jnp.full(mrun_scr.shape, NEG, F32)

    def score_tiles(tiles):
        tile_max = None
        for kt, causal in tiles:
            k0 = pl.multiple_of(kt * KT, KT)
            kx = jnp.concatenate([ks_ref[pl.ds(k0, KT), :], posf_scr[pl.ds(k0, KT), :]], axis=1)
            s = _dot_nt(qe_scr[...], kx)
            if causal:
                qs = lax.broadcasted_iota(jnp.int32, (QB, KT), 0)
                kl = lax.broadcasted_iota(jnp.int32, (QB, KT), 1)
                s = s + _tile_heads(jnp.where(k0 + kl <= t0 + qs, 0.0, NEG))
            s_scr[kt] = s
            for c0 in range(0, KT, LANES):
                part = s[:, c0:c0 + LANES]
                tile_max = part if tile_max is None else jnp.maximum(tile_max, part)
        mrun_scr[...] = jnp.maximum(mrun_scr[...], tile_max)

    def score_pair(i, carry):
        score_tiles([(2 * i, False), (2 * i + 1, False)])
        return carry

    lax.fori_loop(0, last_tile // 2, score_pair, 0)

    @pl.when(last_tile % 2 == 1)
    def _():
        score_tiles([(last_tile - 1, False), (last_tile, True)])

    @pl.when(last_tile % 2 == 0)
    def _():
        score_tiles([(last_tile, True)])

    mb_scr[...] = jnp.broadcast_to(jnp.max(mrun_scr[...], axis=-1, keepdims=True), mb_scr.shape)
    acc_scr[...] = jnp.zeros(acc_scr.shape, F32)

    def value_tiles(tiles):
        mb = mb_scr[...]
        mbw = jnp.concatenate([mb] * (KT // LANES), axis=1)
        new = [None, None]
        for kt in tiles:
            k0 = pl.multiple_of(kt * KT, KT)
            p = jnp.exp(s_scr[kt] - mbw).astype(BF16)
            for g, v_ref in enumerate((vs0_ref, vs1_ref)):
                pv = _dot(p[g * HALF:(g + 1) * HALF], v_ref[pl.ds(k0, KT), :])
                new[g] = pv if new[g] is None else new[g] + pv
        acc_scr[0:HALF, :] += new[0]
        acc_scr[HALF:2 * HALF, :] += new[1]

    def value_pair(i, carry):
        value_tiles([2 * i, 2 * i + 1])
        return carry

    n_tiles = last_tile + 1
    lax.fori_loop(0, n_tiles // 2, value_pair, 0)

    @pl.when(n_tiles % 2 == 1)
    def _():
        value_tiles([last_tile])

    o_sel = normalised(acc_scr[...], False)

    gate = jax.nn.sigmoid(g_ref[...])
    for pair in range(NSA_HEADS // 2):
        y = None
        for b, o in enumerate((o_cmp, o_sel, o_win)):
            c_e = SUBLANES + 3 * (2 * pair) + b
            c_o = SUBLANES + 3 * (2 * pair + 1) + b
            term = jnp.where(lo_half, gate[:, c_e:c_e + 1], gate[:, c_o:c_o + 1]) * o[pair]
            y = term if y is None else y + term
        y_ref[:, pair * LANES:(pair + 1) * LANES] = y.astype(y_ref.dtype)


def _nsa(pr, kc, vc0, vc1, batch, seq):
    QB = Q_BLOCK
    nq = seq // QB
    nbs = seq // SEL_LEN
    ncmp = kc.shape[1]
    cs = np.arange(ncmp) * CMP_STRIDE
    js = np.arange(nbs) * SEL_LEN
    ov = np.zeros((ncmp, LANES), np.float32)
    ov[:, :nbs] = (cs[:, None] < js[None, :] + SEL_LEN) & (cs[:, None] + CMP_LEN > js[None, :])
    tok = lambda b, i: (b * nq + i, 0)
    per_b = lambda b, i: (b, 0)
    per_b3 = lambda b, i: (b, 0, 0)
    kv_spec = pl.BlockSpec((seq, LANES), per_b)
    cmp_spec = pl.BlockSpec((1, ncmp, LANES), per_b3)
    return pl.pallas_call(
        functools.partial(_nsa_kernel, nbs),
        out_shape=jax.ShapeDtypeStruct((batch * seq, NSA_WIDTH), BF16),
        grid=(batch, nq),
        in_specs=[pl.BlockSpec((QB, NSA_HEADS * LANES), tok),
                  pl.BlockSpec((QB, LANES), tok),
                  cmp_spec, cmp_spec, cmp_spec,
                  kv_spec, kv_spec, kv_spec, kv_spec, kv_spec, kv_spec,
                  pl.BlockSpec((ncmp, LANES), lambda b, i: (0, 0))],
        out_specs=pl.BlockSpec((QB, NSA_WIDTH), tok),
        scratch_shapes=[pltpu.VMEM((NSA_ROWS, 2 * LANES), BF16),
                        pltpu.VMEM((seq, LANES), BF16),
                        pltpu.VMEM((WIN_KEYS, LANES), BF16),
                        pltpu.VMEM((ncmp, LANES), BF16),
                        pltpu.VMEM((seq // SEL_KT, NSA_ROWS, SEL_KT), F32),
                        pltpu.VMEM((NSA_ROWS, LANES), F32),
                        pltpu.VMEM((NSA_ROWS, LANES), F32),
                        pltpu.VMEM((NSA_ROWS, LANES), F32)],
        compiler_params=pltpu.CompilerParams(dimension_semantics=("parallel", "arbitrary"),
                                             vmem_limit_bytes=VMEM_LIMIT),
        name="nsa",
    )(pr["nq"], pr["gcol"], kc, vc0, vc1, pr["ks"], pr["vs0"], pr["vs1"],
      pr["kw"], pr["vw0"], pr["vw1"], jnp.asarray(ov, BF16))


def _memkv_kernel(mem_ref, g_ref, w_ref, kv_ref):
    u = _rms(mem_ref[0], g_ref[...]).astype(BF16)
    kv_ref[0] = _dot(u, w_ref[...]).astype(kv_ref.dtype)


def _memkv(mem, g_mem, w_mem_kv):
    batch, n_mem, d = mem.shape
    return pl.pallas_call(
        _memkv_kernel,
        out_shape=jax.ShapeDtypeStruct((batch, n_mem, 2 * XA_WIDTH), BF16),
        grid=(batch,),
        in_specs=[pl.BlockSpec((1, n_mem, d), lambda b: (b, 0, 0)),
                  _resident((1, d)), _resident((d, 2 * XA_WIDTH))],
        out_specs=pl.BlockSpec((1, n_mem, 2 * XA_WIDTH), lambda b: (b, 0, 0)),
        compiler_params=pltpu.CompilerParams(dimension_semantics=("parallel",),
                                             vmem_limit_bytes=VMEM_LIMIT),
        name="memkv",
    )(mem, g_mem[None, :], w_mem_kv.astype(BF16))


def _tail_kernel(x_ref, yml_ref, ynsa_ref, xq_ref, kv_ref, mg_ref, wb_ref, wo_ref, gf_ref,
                 w1_ref, w2_ref, gl_ref, out_ref):
    d = x_ref.shape[1]
    kv = kv_ref[0]
    y_xa = []
    for hh in range(XA_HEADS):
        sl = slice(hh * XA_DH, (hh + 1) * XA_DH)
        s = _dot_nt(xq_ref[:, sl], kv[:, sl]) * (XA_DH ** -0.5)
        p = jnp.exp(s - jnp.max(s, axis=-1, keepdims=True))
        l = jnp.sum(p, axis=-1, keepdims=True)
        y_xa.append((_dot(p.astype(BF16), kv[:, XA_WIDTH + hh * XA_DH:XA_WIDTH + (hh + 1) * XA_DH])
                     / l).astype(BF16))
    ys = (yml_ref[...], ynsa_ref[...], jnp.concatenate(y_xa, axis=-1))
    merged = None
    for j in range(N_BRANCH):
        term = jax.nn.sigmoid(mg_ref[:, j * d:(j + 1) * d]) * _dot(ys[j], wb_ref[j])
        merged = term if merged is None else merged + term
    h = x_ref[...] + _dot(merged.astype(BF16), wo_ref[...])
    u = _rms(h, gf_ref[...]).astype(BF16)
    acc = h
    for c0 in range(0, w1_ref.shape[1], FF_SLAB):
        a = jnp.maximum(_dot(u, w1_ref[:, c0:c0 + FF_SLAB]), 0.0)
        acc = acc + _dot((a * a).astype(BF16), w2_ref[c0:c0 + FF_SLAB, :])
    out_ref[...] = _rms(acc, gl_ref[...])


def _tail(x2, y_ml, y_nsa, pr, mem_kv, w_branch, w_out, g_ffn, w_ff1, w_ff2, g_final, seq):
    n_tok, d = x2.shape
    tm = TM_TAIL
    d_ff = w_ff1.shape[1]
    n_mem = mem_kv.shape[1]
    tiles_per_b = seq // tm
    row = lambda i: (i, 0)
    return pl.pallas_call(
        _tail_kernel,
        out_shape=jax.ShapeDtypeStruct((n_tok, d), F32),
        grid=(n_tok // tm,),
        in_specs=[pl.BlockSpec((tm, d), row),
                  pl.BlockSpec((tm, ML_WIDTH), row),
                  pl.BlockSpec((tm, NSA_WIDTH), row),
                  pl.BlockSpec((tm, XA_WIDTH), row),
                  pl.BlockSpec((1, n_mem, 2 * XA_WIDTH), lambda i: (i // tiles_per_b, 0, 0)),
                  pl.BlockSpec((tm, N_BRANCH * d), row),
                  _resident((N_BRANCH, ML_WIDTH, d)), _resident((d, d)), _resident((1, d)),
                  _resident((d, d_ff)), _resident((d_ff, d)), _resident((1, d))],
        out_specs=pl.BlockSpec((tm, d), row),
        compiler_params=pltpu.CompilerParams(dimension_semantics=("parallel",),
                                             vmem_limit_bytes=VMEM_LIMIT),
        name="tail",
    )(x2, y_ml, y_nsa, pr["xq"], mem_kv, pr["mg"], w_branch.astype(BF16), w_out.astype(BF16),
      g_ffn[None, :], w_ff1.astype(BF16), w_ff2.astype(BF16), g_final[None, :])


def _layer(x, mem, g_mix, w_in, b_in, ml_conv, ml_norm_g, cmp_pe, cmp_w1, cmp_w2, g_mem, w_mem_kv,
           w_branch, w_out, g_ffn, w_ff1, w_ff2, g_final):
    batch, seq, d = x.shape
    x2 = x.reshape(batch * seq, d)
    pr = _inproj(x2, g_mix, w_in, b_in)
    y_ml = _mlstm(pr, ml_conv, ml_norm_g, batch, seq)
    kc, vc0, vc1 = _compress(pr, cmp_pe, cmp_w1, cmp_w2, batch, seq)
    y_nsa = _nsa(pr, kc, vc0, vc1, batch, seq)
    mem_kv = _memkv(mem, g_mem, w_mem_kv)
    out = _tail(x2, y_ml, y_nsa, pr, mem_kv, w_branch, w_out, g_ffn, w_ff1, w_ff2, g_final, seq)
    return out.reshape(batch, seq, d)


def kernel(x, mem, g_mix, w_in, b_in, ml_conv, ml_norm_g, cmp_pe, cmp_w1, cmp_w2, g_mem, w_mem_kv,
           w_branch, w_out, g_ffn, w_ff1, w_ff2, g_final):
    assert g_mix.shape[0] == 1, "single-layer block"
    return _layer(x, mem, g_mix[0], w_in[0], b_in[0], ml_conv[0], ml_norm_g[0], cmp_pe[0],
                  cmp_w1[0], cmp_w2[0], g_mem[0], w_mem_kv[0], w_branch[0], w_out[0], g_ffn[0],
                  w_ff1[0], w_ff2[0], g_final)
```

```python
import functools

import numpy as np
import jax
import jax.numpy as jnp
from jax import lax
from jax.experimental import pallas as pl
from jax.experimental.pallas import tpu as pltpu

F32 = jnp.float32
BF16 = jnp.bfloat16

EPS = 1e-6
NEG = -1e30
ML_HEADS = 4
ML_DH = 128
ML_WIDTH = ML_HEADS * ML_DH
ML_CONV = 4
ML_CHUNK = 128
ML_BATCH_PER_STEP = 1
NSA_HEADS = 8
NSA_KV = 2
NSA_REP = NSA_HEADS // NSA_KV
NSA_DH = 64
NSA_WIDTH = NSA_HEADS * NSA_DH
NSA_KV_WIDTH = NSA_KV * NSA_DH
CMP_LEN = 32
CMP_STRIDE = 16
CMP_HIDDEN = 256
SEL_LEN = 64
SEL_TOP = 16
WINDOW = 512
Q_BLOCK = 128
FORCE_BONUS = 1e3
SEL_KT = 256
XA_HEADS = 4
XA_DH = 128
XA_WIDTH = XA_HEADS * XA_DH
N_BRANCH = 3

LANES = 128
SUBLANES = 8
VMEM_LIMIT = 56 * 1024 * 1024

TM_PROJ = 256
TM_TAIL = 256
FF_SLAB = 1024

_NT = (((1,), (1,)), ((), ()))
_TN = (((0,), (0,)), ((), ()))


def _dot(a, b):
    return jnp.dot(a, b, preferred_element_type=F32)


def _dot_nt(a, b):
    return lax.dot_general(a, b, _NT, preferred_element_type=F32)


def _resident(shape):
    nd = len(shape)
    return pl.BlockSpec(shape, lambda *_: (0,) * nd, pipeline_mode=pl.Buffered(1))


def _rms(x, g):
    return x * lax.rsqrt(jnp.mean(x * x, axis=-1, keepdims=True) + EPS) * g


_PROJ_COLS = (
    ("qk", 2 * ML_WIDTH, BF16),
    ("o", ML_WIDTH, BF16),
    ("mg", None, F32),
    ("gcol", LANES, F32),
    ("kc", NSA_KV_WIDTH, F32),
    ("vc", NSA_KV_WIDTH, F32),
    ("v", ML_WIDTH, BF16),
    ("nq", NSA_HEADS * LANES, BF16),
    ("xq", XA_WIDTH, BF16),
    ("ks", NSA_KV_WIDTH, BF16),
    ("kw", NSA_KV_WIDTH, BF16),
    ("vs0", LANES, BF16),
    ("vs1", LANES, BF16),
    ("vw0", LANES, BF16),
    ("vw1", LANES, BF16),
)
_PROJ_ROWS = (
    ("smallT", SUBLANES, F32),
)
_PROJ_CHUNK = 512


def _proj_layout(d_model):
    cols, off = [], 0
    for name, width, dt in _PROJ_COLS:
        width = N_BRANCH * d_model if width is None else width
        cols.append((name, off, width, dt))
        off += width
    rows, roff = [], 0
    for name, r, dt in _PROJ_ROWS:
        rows.append((name, roff, r, dt))
        roff += r
    return cols, off, rows, roff


def _inproj_kernel(cols, rows, tiles_per_b, x_ref, g_ref, w_ref, b_ref, wt_ref, bt_ref, conv_ref,
                   *refs):
    out_refs, xbuf = refs[:-1], refs[-1]
    tm = x_ref.shape[0]
    halo = SUBLANES

    @pl.when(pl.program_id(0) % tiles_per_b == 0)
    def _():
        xbuf[0:halo, :] = jnp.zeros((halo, xbuf.shape[1]), F32)

    u = _rms(x_ref[...], g_ref[...]).astype(BF16)
    col_refs = out_refs[:len(cols)]
    row_refs = out_refs[len(cols):]
    for (name, off, width, dt), o_ref in zip(cols, col_refs):
        for c0 in range(0, width, _PROJ_CHUNK):
            cw = min(_PROJ_CHUNK, width - c0)
            cs = slice(c0, c0 + cw)
            acc = _dot(u, w_ref[:, off + c0:off + c0 + cw]) + b_ref[:, off + c0:off + c0 + cw]
            if name == "qk":
                xbuf[halo:halo + tm, cs] = acc
                y = acc * conv_ref[0:1, cs]
                for j in range(1, ML_CONV):
                    y = y + xbuf[halo - j:halo - j + tm, cs] * conv_ref[j:j + 1, cs]
                xbuf[0:halo, cs] = xbuf[tm:tm + halo, cs]
                acc = y * jax.nn.sigmoid(y)
                if c0 >= ML_WIDTH:
                    acc = acc * (ML_DH ** -0.5)
            elif name == "o":
                acc = jax.nn.sigmoid(acc)
            o_ref[:, cs] = acc.astype(dt)
    t = _dot_nt(wt_ref[...], u) + bt_ref[...]
    for (name, roff, r, dt), o_ref in zip(rows, row_refs):
        o_ref[0] = t[roff:roff + r, :].astype(dt)


def _split_w_in(w_in, b_in):
    widths = (ML_WIDTH, ML_WIDTH, ML_WIDTH, ML_WIDTH, ML_HEADS, ML_HEADS,
              NSA_WIDTH, NSA_KV_WIDTH, NSA_KV_WIDTH, NSA_KV_WIDTH, NSA_KV_WIDTH, NSA_KV_WIDTH,
              NSA_KV_WIDTH, 3 * NSA_HEADS, XA_WIDTH, w_in.shape[1])
    names = ("ml_q", "ml_k", "ml_v", "ml_o", "ml_i", "ml_f", "ns_q", "ns_kc", "ns_vc", "ns_ks",
             "ns_vs", "ns_kw", "ns_vw", "ns_g", "xa_q", "mg")
    out, off = {}, 0
    for n, wd in zip(names, widths):
        end = w_in.shape[1] if n == "mg" else off + wd
        out[n] = (w_in[:, off:end], b_in[off:end])
        off = end
    return out


def _inproj(x2, g_mix, w_in, b_in, ml_conv, seq):
    n_tok, d = x2.shape
    cols, ncols, rows, nrows = _proj_layout(d)
    p = _split_w_in(w_in, b_in)

    def pad_cols(w, b, width):
        return (jnp.pad(w, ((0, 0), (0, width - w.shape[1]))), jnp.pad(b, (0, width - b.shape[0])))

    wq, bq = p["ns_q"]
    wq = wq.reshape(d, NSA_HEADS, NSA_DH)
    bq = bq.reshape(NSA_HEADS, NSA_DH)
    wq_slots = jnp.zeros((d, NSA_HEADS, NSA_KV, NSA_DH), w_in.dtype)
    bq_slots = jnp.zeros((NSA_HEADS, NSA_KV, NSA_DH), w_in.dtype)
    for h in range(NSA_HEADS):
        wq_slots = wq_slots.at[:, h, h // NSA_REP].set(wq[:, h])
        bq_slots = bq_slots.at[h, h // NSA_REP].set(bq[h])
    gate_w = jnp.concatenate([p["ml_i"][0], p["ml_f"][0]], axis=1)
    gate_b = jnp.concatenate([p["ml_i"][1], p["ml_f"][1]])
    small_w = jnp.concatenate([gate_w, p["ns_g"][0]], axis=1)
    small_b = jnp.concatenate([gate_b, p["ns_g"][1]])

    def with_ones(wb, g):
        w, b = wb
        w = jnp.pad(w[:, g * NSA_DH:(g + 1) * NSA_DH], ((0, 0), (0, LANES - NSA_DH)))
        b = jnp.concatenate([b[g * NSA_DH:(g + 1) * NSA_DH], jnp.ones((LANES - NSA_DH,), b.dtype)])
        return w, b

    pieces = {
        "qk": (jnp.concatenate([p["ml_q"][0], p["ml_k"][0]], axis=1),
               jnp.concatenate([p["ml_q"][1], p["ml_k"][1]])),
        "o": p["ml_o"], "mg": p["mg"],
        "gcol": pad_cols(small_w, small_b, LANES),
        "kc": p["ns_kc"], "vc": p["ns_vc"], "v": p["ml_v"],
        "nq": (wq_slots.reshape(d, NSA_HEADS * LANES), bq_slots.reshape(NSA_HEADS * LANES)),
        "xq": p["xa_q"], "ks": p["ns_ks"], "kw": p["ns_kw"],
        "vs0": with_ones(p["ns_vs"], 0), "vs1": with_ones(p["ns_vs"], 1),
        "vw0": with_ones(p["ns_vw"], 0), "vw1": with_ones(p["ns_vw"], 1),
    }
    w_cols = jnp.concatenate([pieces[name][0] for name, *_ in cols], axis=1).astype(BF16)
    b_cols = jnp.concatenate([pieces[name][1] for name, *_ in cols])[None, :]
    w_rows = gate_w.T.astype(BF16)
    b_rows = gate_b[:, None]

    tm = TM_PROJ
    tiles_per_b = seq // tm
    out_shape = ([jax.ShapeDtypeStruct((n_tok, width), dt) for _, _, width, dt in cols]
                 + [jax.ShapeDtypeStruct((n_tok // seq, r, seq), dt) for _, _, r, dt in rows])
    out_specs = ([pl.BlockSpec((tm, width), lambda i: (i, 0)) for _, _, width, _ in cols]
                 + [pl.BlockSpec((1, r, tm), lambda i: (i // tiles_per_b, 0, i % tiles_per_b))
                    for _, _, r, _ in rows])
    outs = pl.pallas_call(
        functools.partial(_inproj_kernel, cols, rows, tiles_per_b),
        out_shape=out_shape,
        grid=(n_tok // tm,),
        in_specs=[pl.BlockSpec((tm, d), lambda i: (i, 0)),
                  _resident((1, d)), _resident((d, ncols)), _resident((1, ncols)),
                  _resident((nrows, d)), _resident((nrows, 1)),
                  _resident((ML_CONV, 2 * ML_WIDTH))],
        out_specs=out_specs,
        scratch_shapes=[pltpu.VMEM((tm + SUBLANES, 2 * ML_WIDTH), F32)],
        compiler_params=pltpu.CompilerParams(dimension_semantics=("arbitrary",),
                                             vmem_limit_bytes=VMEM_LIMIT),
        name="inproj",
    )(x2, g_mix[None, :], w_cols, b_cols, w_rows, b_rows, ml_conv)
    names = [c[0] for c in cols] + [r[0] for r in rows]
    return dict(zip(names, outs))


def _mlstm_kernel(qk_ref, v_ref, o_ref, gcol_ref, grow_ref, ng_ref, y_ref, c_scr, m_scr):
    L = ML_CHUNK
    nb = qk_ref.shape[0]

    @pl.when(pl.program_id(1) == 0)
    def _():
        c_scr[...] = jnp.zeros(c_scr.shape, F32)
        m_scr[...] = jnp.zeros(m_scr.shape, F32)

    r_i = lax.broadcasted_iota(jnp.int32, (L, L), 0)
    c_i = lax.broadcasted_iota(jnp.int32, (L, L), 1)
    causal = c_i <= r_i
    tril = causal.astype(F32)
    triu = (r_i <= c_i).astype(F32)
    gcols = [gcol_ref[b] for b in range(nb)]
    grows = [grow_ref[b] for b in range(nb)]
    b_cols = jnp.dot(tril, jax.nn.log_sigmoid(jnp.concatenate(gcols, axis=1)),
                     precision=lax.Precision.HIGHEST, preferred_element_type=F32)
    b_rows = jnp.dot(jax.nn.log_sigmoid(jnp.concatenate(grows, axis=0)), triu,
                     precision=lax.Precision.HIGHEST, preferred_element_type=F32)
    ones_col = jnp.where(lax.broadcasted_iota(jnp.int32, (L, ML_DH), 1) == 0, 1.0, 0.0).astype(BF16)

    chains = [(b, hh) for b in range(nb) for hh in range(ML_HEADS)]

    def operands(b, hh):
        sl = slice(hh * ML_DH, (hh + 1) * ML_DH)
        q = qk_ref[b, :, sl]
        k = qk_ref[b, :, ML_WIDTH + hh * ML_DH:ML_WIDTH + (hh + 1) * ML_DH]
        v1 = jnp.concatenate([v_ref[b, :, sl], ones_col], axis=1)
        return q, k, v1, c_scr[b, hh]

    ops = [operands(b, hh) for b, hh in chains]
    qk_t = [_dot_nt(q, k) for q, k, _, _ in ops]
    inter = [_dot(q, c_prev.astype(BF16)) for q, _, _, c_prev in ops]

    stab, ke_all, s_all = [], [], []
    for i, (b, hh) in enumerate(chains):
        gcol, grow = gcols[b], grows[b]
        li_c = gcol[:, hh:hh + 1]
        b_c = b_cols[:, b * LANES + ML_HEADS + hh:b * LANES + ML_HEADS + hh + 1]
        li_r = grow[hh:hh + 1, :]
        b_r = b_rows[b * SUBLANES + ML_HEADS + hh:b * SUBLANES + ML_HEADS + hh + 1, :]
        g = b_c[L - 1:L, :]
        m_prev = m_scr[b, hh:hh + 1, 0:1]
        log_d = jnp.where(causal, b_c - b_r + li_r, -jnp.inf)
        inter_log = b_c + m_prev
        m_t = jnp.maximum(inter_log, jnp.max(log_d, axis=-1, keepdims=True))
        s_all.append((qk_t[i] * jnp.exp(log_d - m_t)).astype(BF16))
        w_end = g - b_c + li_c
        m_loc = jnp.max(w_end, axis=0, keepdims=True)
        ke_all.append((ops[i][1] * jnp.exp(w_end - m_loc)).astype(BF16))
        m_new = jnp.maximum(g + m_prev, m_loc)
        stab.append((m_t, jnp.exp(inter_log - m_t), jnp.exp(g + m_prev - m_new),
                     jnp.exp(m_loc - m_new), m_new))

    intra = [_dot(s, o[2]) for s, o in zip(s_all, ops)]
    a_c = [lax.dot_general(ke, o[2], _TN, preferred_element_type=F32) for ke, o in zip(ke_all, ops)]

    ys, new_c, new_m = [], [], []
    for i, (b, hh) in enumerate(chains):
        sl = slice(hh * ML_DH, (hh + 1) * ML_DH)
        m_t, sc, a, bb, m_new = stab[i]
        nd = intra[i] + sc * inter[i]
        den = nd[:, ML_DH:ML_DH + 1]
        h = nd[:, 0:ML_DH] / jnp.maximum(jnp.abs(den), jnp.exp(-m_t))
        hn = h * lax.rsqrt(jnp.mean(h * h, axis=-1, keepdims=True) + EPS)
        ys.append((o_ref[b, :, sl] * hn * ng_ref[:, sl]).astype(y_ref.dtype))
        new_c.append(a * ops[i][3] + bb * a_c[i])
        new_m.append(jnp.broadcast_to(m_new, (1, m_scr.shape[2])))

    per_b = lambda xs, axis: [jnp.concatenate(xs[b * ML_HEADS:(b + 1) * ML_HEADS], axis=axis)
                              for b in range(nb)]
    y_ref[...] = jnp.stack(per_b(ys, 1))
    c_scr[...] = jnp.stack([jnp.stack(new_c[b * ML_HEADS:(b + 1) * ML_HEADS]) for b in range(nb)])
    m_scr[:, 0:ML_HEADS, :] = jnp.stack(per_b(new_m, 0))


def _mlstm(pr, ml_norm_g, batch, seq):
    L = ML_CHUNK
    nb = ML_BATCH_PER_STEP
    chunk = lambda b, c: (b, c, 0)
    per_tok = lambda name: pr[name].reshape(batch, seq, pr[name].shape[-1])
    y = pl.pallas_call(
        _mlstm_kernel,
        out_shape=jax.ShapeDtypeStruct((batch, seq, ML_WIDTH), BF16),
        grid=(batch // nb, seq // L),
        in_specs=[pl.BlockSpec((nb, L, 2 * ML_WIDTH), chunk),
                  pl.BlockSpec((nb, L, ML_WIDTH), chunk),
                  pl.BlockSpec((nb, L, ML_WIDTH), chunk),
                  pl.BlockSpec((nb, L, LANES), chunk),
                  pl.BlockSpec((nb, SUBLANES, L), lambda b, c: (b, 0, c)),
                  pl.BlockSpec((1, ML_WIDTH), lambda b, c: (0, 0))],
        out_specs=pl.BlockSpec((nb, L, ML_WIDTH), chunk),
        scratch_shapes=[pltpu.VMEM((nb, ML_HEADS, ML_DH, 2 * ML_DH), F32),
                        pltpu.VMEM((nb, SUBLANES, LANES), F32)],
        compiler_params=pltpu.CompilerParams(dimension_semantics=("parallel", "arbitrary"),
                                             vmem_limit_bytes=VMEM_LIMIT),
        name="mlstm",
    )(per_tok("qk"), per_tok("v"), per_tok("o"), per_tok("gcol"), pr["smallT"],
      ml_norm_g[None, :])
    return y.reshape(batch * seq, ML_WIDTH)


def _gelu_tanh(x):
    return 0.5 * x * (1.0 + jnp.tanh(np.sqrt(2.0 / np.pi) * (x + 0.044715 * (x * x * x))))


def _compress_kernel(k_ref, v_ref, pe_ref, w1_ref, w2k_ref, w2v_ref, kc_ref, vc0_ref, vc1_ref):
    nrow = kc_ref.shape[1]
    S = CMP_STRIDE

    def hidden(x_ref, which):
        lo = hi = None
        for j in range(0, S, 2):
            x = [x_ref[pl.ds(j + d, nrow, stride=S), :] for d in range(2)]
            for half in range(CMP_LEN // S):
                l0 = half * S + j
                xp = jnp.concatenate([(x[d] + pe_ref[which, l0 + d:l0 + d + 1, :]).astype(BF16)
                                      for d in range(2)], axis=1)
                t = _dot(xp, w1_ref[which, l0 // 2])
                if half == 0:
                    lo = t if lo is None else lo + t
                else:
                    hi = t if hi is None else hi + t
        return _gelu_tanh(lo + pltpu.roll(hi, nrow - 1, 0)).astype(BF16)

    kc_ref[0] = _dot(hidden(k_ref, 0), w2k_ref[...]).astype(kc_ref.dtype)
    hv = hidden(v_ref, 1)
    ones_half = lax.broadcasted_iota(jnp.int32, (nrow, LANES), 1) >= NSA_DH
    for g, o_ref in enumerate((vc0_ref, vc1_ref)):
        o_ref[0] = jnp.where(ones_half, 1.0, _dot(hv, w2v_ref[g])).astype(o_ref.dtype)


def _compress(pr, cmp_pe, cmp_w1, cmp_w2, batch, seq):
    nrow = seq // CMP_STRIDE
    hid = NSA_KV * CMP_HIDDEN
    w1 = cmp_w1.astype(BF16).reshape(2, CMP_LEN, NSA_DH, CMP_HIDDEN)
    z1 = jnp.zeros_like(w1)
    w1e = jnp.stack([jnp.concatenate([w1, z1], axis=-1), jnp.concatenate([z1, w1], axis=-1)], axis=2)
    w1e = w1e.reshape(2, CMP_LEN // 2, 2 * NSA_KV_WIDTH, hid)
    w2 = cmp_w2.astype(BF16)
    z2 = jnp.zeros_like(w2[0])
    w2k = jnp.concatenate([jnp.concatenate([w2[0], z2], axis=1),
                           jnp.concatenate([z2, w2[0]], axis=1)], axis=0)
    w2v = jnp.stack([jnp.pad(w2[1], ((g * CMP_HIDDEN, (NSA_KV - 1 - g) * CMP_HIDDEN),
                                     (0, LANES - NSA_DH))) for g in range(NSA_KV)])
    pe = jnp.concatenate([cmp_pe] * NSA_KV, axis=-1)
    tok_blk = pl.BlockSpec((seq, NSA_KV_WIDTH), lambda b: (b, 0))
    out_blk = pl.BlockSpec((1, nrow, LANES), lambda b: (b, 0, 0))
    return pl.pallas_call(
        _compress_kernel,
        out_shape=(jax.ShapeDtypeStruct((batch, nrow, LANES), BF16),) * 3,
        grid=(batch,),
        in_specs=[tok_blk, tok_blk,
                  _resident((2, CMP_LEN, NSA_KV_WIDTH)),
                  _resident((2, CMP_LEN // 2, 2 * NSA_KV_WIDTH, hid)),
                  _resident((hid, NSA_KV_WIDTH)),
                  _resident((NSA_KV, hid, LANES))],
        out_specs=(out_blk, out_blk, out_blk),
        compiler_params=pltpu.CompilerParams(dimension_semantics=("parallel",),
                                             vmem_limit_bytes=VMEM_LIMIT),
        name="compress",
    )(pr["kc"], pr["vc"], pe, w1e, w2k, w2v)


def _alibi_slope(h):
    return float(2.0 ** (-8.0 * (h + 1) / NSA_HEADS))


def _nsa_kernel_v2(nbs, q_ref, kc_ref, vct_ref, ks_ref, vst_ref, kw_ref, vwt_ref, gt_ref, ovt_ref,
                y_ref, selx, qs_scr, ab_scr, m_scr, l_scr, acc_scr):
    QB = Q_BLOCK
    KT = SEL_KT
    t0 = pl.program_id(1) * QB
    ncmp = kc_ref.shape[1]
    gates = jax.nn.sigmoid(gt_ref[...])
    q_lane = lax.broadcasted_iota(jnp.int32, (1, QB), 1)
    tpos = t0 + q_lane
    qs_scr[...] = q_ref[...] * 0.125

    @pl.when(pl.program_id(1) == 0)
    def _():
        rel = (lax.broadcasted_iota(jnp.int32, (KT, QB), 1)
               - lax.broadcasted_iota(jnp.int32, (KT, QB), 0)).astype(F32)
        for h in range(NSA_HEADS):
            ab_scr[h] = -_alibi_slope(h) * rel

    def head_q(h):
        return qs_scr[:, h * LANES:(h + 1) * LANES]

    def group_rows(x, g):
        return x[g * NSA_DH:(g + 1) * NSA_DH, :]

    n_sub = lax.broadcasted_iota(jnp.int32, (ncmp, QB), 0)
    dist_c = tpos - (n_sub * CMP_STRIDE + (CMP_LEN - 1))
    ok_c = dist_c >= 0
    dist_cf = dist_c.astype(F32)
    kc = kc_ref[0]
    vct = vct_ref[0]
    o_cmp = []
    for g in range(NSA_KV):
        psum = jnp.zeros((ncmp, QB), F32)
        for r in range(NSA_REP):
            h = g * NSA_REP + r
            s = _dot_nt(kc, head_q(h)) - _alibi_slope(h) * dist_cf
            s = jnp.where(ok_c, s, NEG)
            e = jnp.exp(s - jnp.max(s, axis=0, keepdims=True))
            p = jnp.where(ok_c, e / jnp.sum(e, axis=0, keepdims=True), 0.0)
            psum = psum + p
            o_cmp.append(group_rows(_dot(vct, p.astype(BF16)), g))
        imp = jnp.dot(ovt_ref[...], psum, precision=lax.Precision.HIGHEST,
                      preferred_element_type=F32)
        j_sub = lax.broadcasted_iota(jnp.int32, (nbs, QB), 0)
        cur = tpos // SEL_LEN
        forced = (j_sub == 0) | (j_sub == cur) | (j_sub == cur - 1)
        score = jnp.where(j_sub <= cur, imp + jnp.where(forced, FORCE_BONUS, 0.0), NEG)
        n_grp = nbs // SUBLANES
        parts = [score[a * SUBLANES:(a + 1) * SUBLANES, :] for a in range(n_grp)]
        ranks = [jnp.zeros((SUBLANES, QB), F32) for _ in range(n_grp)]
        sub8 = lax.broadcasted_iota(jnp.int32, (SUBLANES, QB), 0)
        for i in range(nbs):
            row = jnp.broadcast_to(score[i:i + 1, :], (SUBLANES, QB))
            for a in range(n_grp):
                if a * SUBLANES > i:
                    beats = jnp.where(row >= parts[a], 1.0, 0.0)
                elif (a + 1) * SUBLANES - 1 < i:
                    beats = jnp.where(row > parts[a], 1.0, 0.0)
                else:
                    beats = jnp.where(sub8 + a * SUBLANES > i,
                                      jnp.where(row >= parts[a], 1.0, 0.0),
                                      jnp.where(row > parts[a], 1.0, 0.0))
                ranks[a] = ranks[a] + beats
        for a in range(n_grp):
            sel_bias = jnp.where(ranks[a] < float(min(SEL_TOP, nbs)), 0.0, NEG)
            for jj in range(SUBLANES):
                j = a * SUBLANES + jj
                selx[g, j * SUBLANES:(j + 1) * SUBLANES, :] = jnp.broadcast_to(
                    sel_bias[jj:jj + 1, :], (SUBLANES, QB))

    blocks_per_tile = KT // SEL_LEN
    m_scr[...] = jnp.full(m_scr.shape, NEG, F32)
    l_scr[...] = jnp.zeros(l_scr.shape, F32)
    acc_scr[...] = jnp.zeros(acc_scr.shape, F32)
    last_tile = t0 // KT

    def sel_tile(kt, causal):
        k0 = pl.multiple_of(kt * KT, KT)
        ktile = ks_ref[pl.ds(k0, KT), :]
        row0 = pl.multiple_of(kt * (blocks_per_tile * SUBLANES), SUBLANES)
        if causal:
            k_sub = lax.broadcasted_iota(jnp.int32, (KT, QB), 0)
            visible = (q_lane - k_sub) >= (k0 - t0)
        for g in range(NSA_KV):
            sel = [selx[g, pl.ds(row0 + jj * SUBLANES, SUBLANES), :] for jj in range(blocks_per_tile)]
            vt = vst_ref[g * NSA_DH:(g + 1) * NSA_DH, pl.ds(k0, KT)]
            for r in range(NSA_REP):
                h = g * NSA_REP + r
                shift = _alibi_slope(h) * k0.astype(F32)
                bias = jnp.concatenate([jnp.tile(sb + shift, (SEL_LEN // SUBLANES, 1)) for sb in sel],
                                       axis=0)
                s = _dot_nt(ktile, head_q(h)) + ab_scr[h] + bias
                if causal:
                    s = jnp.where(visible, s, NEG)
                m_old = m_scr[h]
                m_new = jnp.maximum(m_old, jnp.max(s, axis=0, keepdims=True))
                alpha = jnp.exp(m_old - m_new)
                p = jnp.exp(s - m_new)
                l_scr[h] = alpha * l_scr[h] + jnp.sum(p, axis=0, keepdims=True)
                acc_scr[h] = alpha * acc_scr[h] + _dot(vt, p.astype(BF16))
                m_scr[h] = m_new

    def full_tile(kt, carry):
        sel_tile(kt, False)
        return carry

    lax.fori_loop(0, last_tile, full_tile, 0)
    sel_tile(last_tile, True)
    o_sel = [acc_scr[h] / l_scr[h] for h in range(NSA_HEADS)]

    WK = WINDOW + QB
    ws = pl.multiple_of(jnp.maximum(t0 - WINDOW, 0), QB)
    kw_sub = lax.broadcasted_iota(jnp.int32, (WK, QB), 0)
    dist_w = (t0 - ws) + q_lane - kw_sub
    ok_w = (dist_w >= 0) & (dist_w < WINDOW)
    dist_wf = dist_w.astype(F32)
    kwin = kw_ref[pl.ds(ws, WK), :]
    vwin = vwt_ref[:, pl.ds(ws, WK)]
    o_win = []
    for g in range(NSA_KV):
        for r in range(NSA_REP):
            h = g * NSA_REP + r
            s = _dot_nt(kwin, head_q(h)) - _alibi_slope(h) * dist_wf
            s = jnp.where(ok_w, s, NEG)
            p = jnp.exp(s - jnp.max(s, axis=0, keepdims=True))
            l = jnp.sum(p, axis=0, keepdims=True)
            o_win.append(group_rows(_dot(vwin, p.astype(BF16)), g) / l)

    outs = []
    for h in range(NSA_HEADS):
        g0 = SUBLANES + 3 * h
        outs.append(gates[g0:g0 + 1, :] * o_cmp[h] + gates[g0 + 1:g0 + 2, :] * o_sel[h]
                    + gates[g0 + 2:g0 + 3, :] * o_win[h])
    for pair in range(NSA_HEADS // 2):
        both = jnp.concatenate([outs[2 * pair], outs[2 * pair + 1]], axis=0)
        y_ref[:, pair * LANES:(pair + 1) * LANES] = both.T.astype(y_ref.dtype)


def _nsa_v2(pr, kc, vct, batch, seq):
    QB = Q_BLOCK
    nq = seq // QB
    nbs = seq // SEL_LEN
    ncmp = kc.shape[1]
    cs = np.arange(ncmp) * CMP_STRIDE
    js = np.arange(nbs) * SEL_LEN
    ovt = jnp.asarray(((cs[None, :] < js[:, None] + SEL_LEN)
                       & (cs[None, :] + CMP_LEN > js[:, None])).astype(np.float32))
    per_b = lambda b, i: (b, 0)
    return pl.pallas_call(
        functools.partial(_nsa_kernel, nbs),
        out_shape=jax.ShapeDtypeStruct((batch * seq, NSA_WIDTH), BF16),
        grid=(batch, nq),
        in_specs=[pl.BlockSpec((QB, NSA_HEADS * LANES), lambda b, i: (b * nq + i, 0)),
                  pl.BlockSpec((1, ncmp, NSA_KV_WIDTH), lambda b, i: (b, 0, 0)),
                  pl.BlockSpec((1, NSA_KV_WIDTH, ncmp), lambda b, i: (b, 0, 0)),
                  pl.BlockSpec((seq, NSA_KV_WIDTH), per_b),
                  pl.BlockSpec((NSA_KV_WIDTH, seq), lambda b, i: (0, b)),
                  pl.BlockSpec((seq, NSA_KV_WIDTH), per_b),
                  pl.BlockSpec((NSA_KV_WIDTH, seq), lambda b, i: (0, b)),
                  pl.BlockSpec((32, QB), lambda b, i: (0, b * nq + i)),
                  pl.BlockSpec((nbs, ncmp), lambda b, i: (0, 0))],
        out_specs=pl.BlockSpec((QB, NSA_WIDTH), lambda b, i: (b * nq + i, 0)),
        scratch_shapes=[pltpu.VMEM((NSA_KV, nbs * SUBLANES, QB), F32),
                        pltpu.VMEM((QB, NSA_HEADS * LANES), BF16),
                        pltpu.VMEM((NSA_HEADS, SEL_KT, QB), F32),
                        pltpu.VMEM((NSA_HEADS, 1, QB), F32),
                        pltpu.VMEM((NSA_HEADS, 1, QB), F32),
                        pltpu.VMEM((NSA_HEADS, NSA_DH, QB), F32)],
        compiler_params=pltpu.CompilerParams(dimension_semantics=("parallel", "arbitrary"),
                                             vmem_limit_bytes=VMEM_LIMIT),
        name="nsa",
    )(pr["nq"], kc, vct, pr["ks"], pr["vsT"], pr["kw"], pr["vwT"], pr["smallT"], ovt)


NSA_ROWS = NSA_HEADS * Q_BLOCK
WIN_KEYS = WINDOW + Q_BLOCK
F_SEL_BLOCK = 0
F_SEL_OFF = SEL_LEN
F_WIN_OFF = SEL_LEN + 1
F_WIN_BLK = SEL_LEN + 2
F_CMP = SEL_LEN + 3


def _tile_heads(x):
    return jnp.concatenate([x] * NSA_HEADS, axis=0)


def _nsa_kernel(nbs, q_ref, g_ref, kc_ref, vc0_ref, vc1_ref, ks_ref, vs0_ref, vs1_ref,
                kw_ref, vw0_ref, vw1_ref, ov_ref, y_ref,
                qe_scr, posf_scr, wposf_scr, cposf_scr, s_scr, mrun_scr, mb_scr, acc_scr):
    QB = Q_BLOCK
    KT = SEL_KT
    HALF = NSA_REP * QB
    t0 = pl.program_id(1) * QB
    ncmp = kc_ref.shape[1]
    lane = lax.broadcasted_iota(jnp.int32, (1, LANES), 1)
    lo_half = lane < NSA_DH

    @pl.when(pl.program_id(1) == 0)
    def _():
        seq = posf_scr.shape[0]
        k = lax.broadcasted_iota(jnp.int32, (seq, LANES), 0)
        c = lax.broadcasted_iota(jnp.int32, (seq, LANES), 1)
        posf = jnp.where(c < SEL_LEN, jnp.where(k // SEL_LEN == c, 1.0, 0.0),
                         jnp.where(c == F_SEL_OFF, (k % SEL_LEN).astype(F32), 0.0))
        posf_scr[...] = posf.astype(BF16)
        k = lax.broadcasted_iota(jnp.int32, (WIN_KEYS, LANES), 0)
        c = lax.broadcasted_iota(jnp.int32, (WIN_KEYS, LANES), 1)
        wposf = jnp.where(c == F_WIN_OFF, (k % SEL_LEN).astype(F32),
                          jnp.where(c == F_WIN_BLK, (k // SEL_LEN).astype(F32), 0.0))
        wposf_scr[...] = wposf.astype(BF16)
        k = lax.broadcasted_iota(jnp.int32, (ncmp, LANES), 0)
        c = lax.broadcasted_iota(jnp.int32, (ncmp, LANES), 1)
        cposf_scr[...] = jnp.where(c == F_CMP, k.astype(F32), 0.0).astype(BF16)

    def const_feat(h):
        s = _alibi_slope(h)
        return jnp.where((lane == F_SEL_OFF) | (lane == F_WIN_OFF), s,
                         jnp.where(lane == F_WIN_BLK, SEL_LEN * s,
                                   jnp.where(lane == F_CMP, CMP_STRIDE * s, 0.0)))

    for h in range(NSA_HEADS):
        rows = slice(h * QB, (h + 1) * QB)
        qe_scr[rows, 0:LANES] = q_ref[:, h * LANES:(h + 1) * LANES] * 0.125
        qe_scr[rows, LANES:2 * LANES] = jnp.broadcast_to(const_feat(h), (QB, LANES)).astype(BF16)

    def normalised(o, guard):
        outs = []
        for pair in range(NSA_HEADS // 2):
            e = o[(2 * pair) * QB:(2 * pair + 1) * QB]
            d = o[(2 * pair + 1) * QB:(2 * pair + 2) * QB]
            num = jnp.where(lo_half, e, pltpu.roll(d, NSA_DH, 1))
            den = jnp.where(lo_half, pltpu.roll(e, NSA_DH, 1), d)
            if guard:
                den = jnp.where(den > 0.0, den, 1.0)
            outs.append(num / den)
        return outs

    q_sub = lax.broadcasted_iota(jnp.int32, (QB, ncmp), 0)
    n_lane = lax.broadcasted_iota(jnp.int32, (QB, ncmp), 1)
    ok_c = n_lane * CMP_STRIDE + (CMP_LEN - 1) <= t0 + q_sub
    s = _dot_nt(qe_scr[...], jnp.concatenate([kc_ref[0], cposf_scr[...]], axis=1))
    s = s + _tile_heads(jnp.where(ok_c, 0.0, NEG))
    e = jnp.exp(s - jnp.max(s, axis=-1, keepdims=True)) * _tile_heads(jnp.where(ok_c, 1.0, 0.0))
    e_bf = e.astype(BF16)
    oc = jnp.concatenate([_dot(e_bf[0:HALF], vc0_ref[0]), _dot(e_bf[HALF:2 * HALF], vc1_ref[0])],
                         axis=0)
    o_cmp = normalised(oc, True)
    l_c = jnp.where(lo_half, pltpu.roll(oc, NSA_DH, 1), oc)
    inv_l = 1.0 / jnp.where(l_c > 0.0, l_c, 1.0)
    p_c = e * jnp.concatenate([inv_l] * (ncmp // LANES), axis=1)

    imp_ts = []
    for g in range(NSA_KV):
        psum = p_c[(g * NSA_REP) * QB:(g * NSA_REP + 1) * QB]
        for r in range(1, NSA_REP):
            psum = psum + p_c[(g * NSA_REP + r) * QB:(g * NSA_REP + r + 1) * QB]
        p_hi = psum.astype(BF16)
        p_lo = (psum - p_hi.astype(F32)).astype(BF16)
        imp = _dot(p_hi, ov_ref[...]) + _dot(p_lo, ov_ref[...])
        imp_ts.append(imp.T[0:nbs, :])

    ws = pl.multiple_of(jnp.maximum(t0 - WINDOW, 0), QB)
    qs = lax.broadcasted_iota(jnp.int32, (QB, WIN_KEYS), 0)
    kl = lax.broadcasted_iota(jnp.int32, (QB, WIN_KEYS), 1)
    dist_w = (t0 - ws) + qs - kl
    ok_w = (dist_w >= 0) & (dist_w < WINDOW)
    s_w = _dot_nt(qe_scr[...], jnp.concatenate([kw_ref[pl.ds(ws, WIN_KEYS), :], wposf_scr[...]], axis=1))

    n_grp = nbs // SUBLANES
    sub8 = lax.broadcasted_iota(jnp.int32, (SUBLANES, QB), 0)
    tpos = t0 + lax.broadcasted_iota(jnp.int32, (1, QB), 1)
    cur = tpos // SEL_LEN
    j_lane = lane.astype(F32)
    for g in range(NSA_KV):
        imp_t = imp_ts[g]
        j_sub = lax.broadcasted_iota(jnp.int32, (nbs, QB), 0)
        forced = (j_sub == 0) | (j_sub == cur) | (j_sub == cur - 1)
        score = jnp.where(j_sub <= cur, imp_t + jnp.where(forced, FORCE_BONUS, 0.0), NEG)
        parts = [score[a * SUBLANES:(a + 1) * SUBLANES, :] for a in range(n_grp)]
        ranks = [jnp.zeros((SUBLANES, QB), F32) for _ in range(n_grp)]
        for i in range(nbs):
            row = jnp.broadcast_to(score[i:i + 1, :], (SUBLANES, QB))
            for a in range(n_grp):
                if a * SUBLANES > i:
                    beats = jnp.where(row >= parts[a], 1.0, 0.0)
                elif (a + 1) * SUBLANES - 1 < i:
                    beats = jnp.where(row > parts[a], 1.0, 0.0)
                else:
                    beats = jnp.where(sub8 + a * SUBLANES > i,
                                      jnp.where(row >= parts[a], 1.0, 0.0),
                                      jnp.where(row > parts[a], 1.0, 0.0))
                ranks[a] = ranks[a] + beats
        sel_t = [jnp.where(rk < float(min(SEL_TOP, nbs)), 0.0, NEG) for rk in ranks]
        if nbs < LANES:
            sel_t.append(jnp.zeros((LANES - nbs, QB), F32))
        sel_bias = jnp.concatenate(sel_t, axis=0).T
        for r in range(NSA_REP):
            h = g * NSA_REP + r
            feat = jnp.where(lo_half, sel_bias + (_alibi_slope(h) * SEL_LEN) * j_lane, const_feat(h))
            qe_scr[h * QB:(h + 1) * QB, LANES:2 * LANES] = feat.astype(BF16)

    s_w = s_w + _tile_heads(jnp.where(ok_w, 0.0, NEG))
    p_w = jnp.exp(s_w - jnp.max(s_w, axis=-1, keepdims=True)).astype(BF16)
    o_win = normalised(jnp.concatenate([_dot(p_w[0:HALF], vw0_ref[pl.ds(ws, WIN_KEYS), :]),
                                        _dot(p_w[HALF:2 * HALF], vw1_ref[pl.ds(ws, WIN_KEYS), :])],
                                       axis=0), False)

    last_tile = t0 // KT
    mrun_scr[...] = jnp.full(mrun_scr.shape, NEG, F32)

    def score_tiles(tiles):
        tile_max = None
        for kt, causal in tiles:
            k0 = pl.multiple_of(kt * KT, KT)
            kx = jnp.concatenate([ks_ref[pl.ds(k0, KT), :], posf_scr[pl.ds(k0, KT), :]], axis=1)
            s = _dot_nt(qe_scr[...], kx)
            if causal:
                qs = lax.broadcasted_iota(jnp.int32, (QB, KT), 0)
                kl = lax.broadcasted_iota(jnp.int32, (QB, KT), 1)
                s = s + _tile_heads(jnp.where(k0 + kl <= t0 + qs, 0.0, NEG))
            s_scr[kt] = s
            for c0 in range(0, KT, LANES):
                part = s[:, c0:c0 + LANES]
                tile_max = part if tile_max is None else jnp.maximum(tile_max, part)
        mrun_scr[...] = jnp.maximum(mrun_scr[...], tile_max)

    def score_pair(i, carry):
        score_tiles([(2 * i, False), (2 * i + 1, False)])
        return carry

    lax.fori_loop(0, last_tile // 2, score_pair, 0)

    @pl.when(last_tile % 2 == 1)
    def _():
        score_tiles([(last_tile - 1, False), (last_tile, True)])

    @pl.when(last_tile % 2 == 0)
    def _():
        score_tiles([(last_tile, True)])

    mb_scr[...] = jnp.broadcast_to(jnp.max(mrun_scr[...], axis=-1, keepdims=True), mb_scr.shape)
    acc_scr[...] = jnp.zeros(acc_scr.shape, F32)

    def value_tiles(tiles):
        mb = mb_scr[...]
        mbw = jnp.concatenate([mb] * (KT // LANES), axis=1)
        new = [None, None]
        for kt in tiles:
            k0 = pl.multiple_of(kt * KT, KT)
            p = jnp.exp(s_scr[kt] - mbw).astype(BF16)
            for g, v_ref in enumerate((vs0_ref, vs1_ref)):
                pv = _dot(p[g * HALF:(g + 1) * HALF], v_ref[pl.ds(k0, KT), :])
                new[g] = pv if new[g] is None else new[g] + pv
        acc_scr[0:HALF, :] += new[0]
        acc_scr[HALF:2 * HALF, :] += new[1]

    def value_pair(i, carry):
        value_tiles([2 * i, 2 * i + 1])
        return carry

    n_tiles = last_tile + 1
    lax.fori_loop(0, n_tiles // 2, value_pair, 0)

    @pl.when(n_tiles % 2 == 1)
    def _():
        value_tiles([last_tile])

    o_sel = normalised(acc_scr[...], False)

    gate = jax.nn.sigmoid(g_ref[...])
    for pair in range(NSA_HEADS // 2):
        y = None
        for b, o in enumerate((o_cmp, o_sel, o_win)):
            c_e = SUBLANES + 3 * (2 * pair) + b
            c_o = SUBLANES + 3 * (2 * pair + 1) + b
            term = jnp.where(lo_half, gate[:, c_e:c_e + 1], gate[:, c_o:c_o + 1]) * o[pair]
            y = term if y is None else y + term
        y_ref[:, pair * LANES:(pair + 1) * LANES] = y.astype(y_ref.dtype)


def _nsa(pr, kc, vc0, vc1, batch, seq):
    QB = Q_BLOCK
    nq = seq // QB
    nbs = seq // SEL_LEN
    ncmp = kc.shape[1]
    cs = np.arange(ncmp) * CMP_STRIDE
    js = np.arange(nbs) * SEL_LEN
    ov = np.zeros((ncmp, LANES), np.float32)
    ov[:, :nbs] = (cs[:, None] < js[None, :] + SEL_LEN) & (cs[:, None] + CMP_LEN > js[None, :])
    tok = lambda b, i: (b * nq + i, 0)
    per_b = lambda b, i: (b, 0)
    per_b3 = lambda b, i: (b, 0, 0)
    kv_spec = pl.BlockSpec((seq, LANES), per_b)
    cmp_spec = pl.BlockSpec((1, ncmp, LANES), per_b3)
    return pl.pallas_call(
        functools.partial(_nsa_kernel, nbs),
        out_shape=jax.ShapeDtypeStruct((batch * seq, NSA_WIDTH), BF16),
        grid=(batch, nq),
        in_specs=[pl.BlockSpec((QB, NSA_HEADS * LANES), tok),
                  pl.BlockSpec((QB, LANES), tok),
                  cmp_spec, cmp_spec, cmp_spec,
                  kv_spec, kv_spec, kv_spec, kv_spec, kv_spec, kv_spec,
                  pl.BlockSpec((ncmp, LANES), lambda b, i: (0, 0))],
        out_specs=pl.BlockSpec((QB, NSA_WIDTH), tok),
        scratch_shapes=[pltpu.VMEM((NSA_ROWS, 2 * LANES), BF16),
                        pltpu.VMEM((seq, LANES), BF16),
                        pltpu.VMEM((WIN_KEYS, LANES), BF16),
                        pltpu.VMEM((ncmp, LANES), BF16),
                        pltpu.VMEM((seq // SEL_KT, NSA_ROWS, SEL_KT), F32),
                        pltpu.VMEM((NSA_ROWS, LANES), F32),
                        pltpu.VMEM((NSA_ROWS, LANES), F32),
                        pltpu.VMEM((NSA_ROWS, LANES), F32)],
        compiler_params=pltpu.CompilerParams(dimension_semantics=("parallel", "arbitrary"),
                                             vmem_limit_bytes=VMEM_LIMIT),
        name="nsa",
    )(pr["nq"], pr["gcol"], kc, vc0, vc1, pr["ks"], pr["vs0"], pr["vs1"],
      pr["kw"], pr["vw0"], pr["vw1"], jnp.asarray(ov, BF16))


def _memkv_kernel(mem_ref, g_ref, w_ref, kv_ref):
    u = _rms(mem_ref[0], g_ref[...]).astype(BF16)
    kv_ref[0] = _dot(u, w_ref[...]).astype(kv_ref.dtype)


def _memkv(mem, g_mem, w_mem_kv):
    batch, n_mem, d = mem.shape
    return pl.pallas_call(
        _memkv_kernel,
        out_shape=jax.ShapeDtypeStruct((batch, n_mem, 2 * XA_WIDTH), BF16),
        grid=(batch,),
        in_specs=[pl.BlockSpec((1, n_mem, d), lambda b: (b, 0, 0)),
                  _resident((1, d)), _resident((d, 2 * XA_WIDTH))],
        out_specs=pl.BlockSpec((1, n_mem, 2 * XA_WIDTH), lambda b: (b, 0, 0)),
        compiler_params=pltpu.CompilerParams(dimension_semantics=("parallel",),
                                             vmem_limit_bytes=VMEM_LIMIT),
        name="memkv",
    )(mem, g_mem[None, :], w_mem_kv.astype(BF16))


def _tail_kernel(x_ref, yml_ref, ynsa_ref, xq_ref, kv_ref, mg_ref, wb_ref, wo_ref, gf_ref,
                 w1_ref, w2_ref, gl_ref, out_ref):
    d = x_ref.shape[1]
    kv = kv_ref[0]
    y_xa = []
    for hh in range(XA_HEADS):
        sl = slice(hh * XA_DH, (hh + 1) * XA_DH)
        s = _dot_nt(xq_ref[:, sl], kv[:, sl]) * (XA_DH ** -0.5)
        p = jnp.exp(s - jnp.max(s, axis=-1, keepdims=True))
        l = jnp.sum(p, axis=-1, keepdims=True)
        y_xa.append((_dot(p.astype(BF16), kv[:, XA_WIDTH + hh * XA_DH:XA_WIDTH + (hh + 1) * XA_DH])
                     / l).astype(BF16))
    ys = (yml_ref[...], ynsa_ref[...], jnp.concatenate(y_xa, axis=-1))
    merged = None
    for j in range(N_BRANCH):
        term = jax.nn.sigmoid(mg_ref[:, j * d:(j + 1) * d]) * _dot(ys[j], wb_ref[j])
        merged = term if merged is None else merged + term
    h = x_ref[...] + _dot(merged.astype(BF16), wo_ref[...])
    u = _rms(h, gf_ref[...]).astype(BF16)
    acc = h
    for c0 in range(0, w1_ref.shape[1], FF_SLAB):
        a = jnp.maximum(_dot(u, w1_ref[:, c0:c0 + FF_SLAB]), 0.0)
        acc = acc + _dot((a * a).astype(BF16), w2_ref[c0:c0 + FF_SLAB, :])
    out_ref[...] = _rms(acc, gl_ref[...])


def _tail(x2, y_ml, y_nsa, pr, mem_kv, w_branch, w_out, g_ffn, w_ff1, w_ff2, g_final, seq):
    n_tok, d = x2.shape
    tm = TM_TAIL
    d_ff = w_ff1.shape[1]
    n_mem = mem_kv.shape[1]
    tiles_per_b = seq // tm
    row = lambda i: (i, 0)
    return pl.pallas_call(
        _tail_kernel,
        out_shape=jax.ShapeDtypeStruct((n_tok, d), F32),
        grid=(n_tok // tm,),
        in_specs=[pl.BlockSpec((tm, d), row),
                  pl.BlockSpec((tm, ML_WIDTH), row),
                  pl.BlockSpec((tm, NSA_WIDTH), row),
                  pl.BlockSpec((tm, XA_WIDTH), row),
                  pl.BlockSpec((1, n_mem, 2 * XA_WIDTH), lambda i: (i // tiles_per_b, 0, 0)),
                  pl.BlockSpec((tm, N_BRANCH * d), row),
                  _resident((N_BRANCH, ML_WIDTH, d)), _resident((d, d)), _resident((1, d)),
                  _resident((d, d_ff)), _resident((d_ff, d)), _resident((1, d))],
        out_specs=pl.BlockSpec((tm, d), row),
        compiler_params=pltpu.CompilerParams(dimension_semantics=("parallel",),
                                             vmem_limit_bytes=VMEM_LIMIT),
        name="tail",
    )(x2, y_ml, y_nsa, pr["xq"], mem_kv, pr["mg"], w_branch.astype(BF16), w_out.astype(BF16),
      g_ffn[None, :], w_ff1.astype(BF16), w_ff2.astype(BF16), g_final[None, :])


def _layer(x, mem, g_mix, w_in, b_in, ml_conv, ml_norm_g, cmp_pe, cmp_w1, cmp_w2, g_mem, w_mem_kv,
           w_branch, w_out, g_ffn, w_ff1, w_ff2, g_final):
    batch, seq, d = x.shape
    x2 = x.reshape(batch * seq, d)
    pr = _inproj(x2, g_mix, w_in, b_in, ml_conv, seq)
    y_ml = _mlstm(pr, ml_norm_g, batch, seq)
    kc, vc0, vc1 = _compress(pr, cmp_pe, cmp_w1, cmp_w2, batch, seq)
    y_nsa = _nsa(pr, kc, vc0, vc1, batch, seq)
    mem_kv = _memkv(mem, g_mem, w_mem_kv)
    out = _tail(x2, y_ml, y_nsa, pr, mem_kv, w_branch, w_out, g_ffn, w_ff1, w_ff2, g_final, seq)
    return out.reshape(batch, seq, d)


def kernel(x, mem, g_mix, w_in, b_in, ml_conv, ml_norm_g, cmp_pe, cmp_w1, cmp_w2, g_mem, w_mem_kv,
           w_branch, w_out, g_ffn, w_ff1, w_ff2, g_final):
    assert g_mix.shape[0] == 1, "single-layer block"
    return _layer(x, mem, g_mix[0], w_in[0], b_in[0], ml_conv[0], ml_norm_g[0], cmp_pe[0],
                  cmp_w1[0], cmp_w2[0], g_mem[0], w_mem_kv[0], w_branch[0], w_out[0], g_ffn[0],
                  w_ff1[0], w_ff2[0], g_final)
```

```python
import functools

import numpy as np
import jax
import jax.numpy as jnp
from jax import lax
from jax.experimental import pallas as pl
from jax.experimental.pallas import tpu as pltpu

F32 = jnp.float32
BF16 = jnp.bfloat16

EPS = 1e-6
NEG = -1e30
ML_HEADS = 4
ML_DH = 128
ML_WIDTH = ML_HEADS * ML_DH
ML_CONV = 4
ML_CHUNK = 128
ML_BATCH_PER_STEP = 1
NSA_HEADS = 8
NSA_KV = 2
NSA_REP = NSA_HEADS // NSA_KV
NSA_DH = 64
NSA_WIDTH = NSA_HEADS * NSA_DH
NSA_KV_WIDTH = NSA_KV * NSA_DH
CMP_LEN = 32
CMP_STRIDE = 16
CMP_HIDDEN = 256
SEL_LEN = 64
SEL_TOP = 16
WINDOW = 512
Q_BLOCK = 128
FORCE_BONUS = 1e3
SEL_KT = 256
XA_HEADS = 4
XA_DH = 128
XA_WIDTH = XA_HEADS * XA_DH
N_BRANCH = 3

LANES = 128
SUBLANES = 8
VMEM_LIMIT = 56 * 1024 * 1024

TM_PROJ = 256
TM_TAIL = 256
FF_SLAB = 1024

_NT = (((1,), (1,)), ((), ()))
_TN = (((0,), (0,)), ((), ()))


def _dot(a, b):
    return jnp.dot(a, b, preferred_element_type=F32)


def _dot_nt(a, b):
    return lax.dot_general(a, b, _NT, preferred_element_type=F32)


def _resident(shape):
    nd = len(shape)
    return pl.BlockSpec(shape, lambda *_: (0,) * nd, pipeline_mode=pl.Buffered(1))


def _rms(x, g):
    return x * lax.rsqrt(jnp.mean(x * x, axis=-1, keepdims=True) + EPS) * g


_PROJ_COLS = (
    ("qk", 2 * ML_WIDTH, BF16),
    ("o", ML_WIDTH, BF16),
    ("mg", None, F32),
    ("v", ML_WIDTH, BF16),
    ("nq", NSA_HEADS * LANES, BF16),
    ("xq", XA_WIDTH, BF16),
    ("gcol", LANES, F32),
    ("kc", NSA_KV_WIDTH, F32),
    ("vc", NSA_KV_WIDTH, F32),
    ("ks", NSA_KV_WIDTH, BF16),
    ("kw", NSA_KV_WIDTH, BF16),
    ("vs0", LANES, BF16),
    ("vs1", LANES, BF16),
    ("vw0", LANES, BF16),
    ("vw1", LANES, BF16),
)
_PROJ_ROWS = (
    ("smallT", SUBLANES, F32),
)
_PROJ_CHUNK = 512


def _proj_layout(d_model):
    cols, off = [], 0
    for name, width, dt in _PROJ_COLS:
        width = N_BRANCH * d_model if width is None else width
        cols.append((name, off, width, dt))
        off += width
    rows, roff = [], 0
    for name, r, dt in _PROJ_ROWS:
        rows.append((name, roff, r, dt))
        roff += r
    return cols, off, rows, roff


def _inproj_kernel(cols, rows, tiles_per_b, x_ref, g_ref, w_ref, b_ref, wt_ref, bt_ref, conv_ref,
                   *refs):
    out_refs, xbuf = refs[:-1], refs[-1]
    tm = x_ref.shape[0]
    halo = SUBLANES

    @pl.when(pl.program_id(0) % tiles_per_b == 0)
    def _():
        xbuf[0:halo, :] = jnp.zeros((halo, xbuf.shape[1]), F32)

    u = _rms(x_ref[...], g_ref[...]).astype(BF16)
    col_refs = out_refs[:len(cols)]
    row_refs = out_refs[len(cols):]
    ncols = w_ref.shape[1]
    for c0 in range(0, ncols, _PROJ_CHUNK):
        cw = min(_PROJ_CHUNK, ncols - c0)
        chunk = _dot(u, w_ref[:, c0:c0 + cw]) + b_ref[:, c0:c0 + cw]
        for (name, off, width, dt), o_ref in zip(cols, col_refs):
            lo, hi = max(off, c0), min(off + width, c0 + cw)
            if lo >= hi:
                continue
            acc = chunk[:, lo - c0:hi - c0]
            cs = slice(lo - off, hi - off)
            if name == "qk":
                xbuf[halo:halo + tm, cs] = acc
                y = acc * conv_ref[0:1, cs]
                for j in range(1, ML_CONV):
                    y = y + xbuf[halo - j:halo - j + tm, cs] * conv_ref[j:j + 1, cs]
                xbuf[0:halo, cs] = xbuf[tm:tm + halo, cs]
                acc = y * jax.nn.sigmoid(y)
                if lo - off >= ML_WIDTH:
                    acc = acc * (ML_DH ** -0.5)
            elif name == "o":
                acc = jax.nn.sigmoid(acc)
            o_ref[:, cs] = acc.astype(dt)
    t = _dot_nt(wt_ref[...], u) + bt_ref[...]
    for (name, roff, r, dt), o_ref in zip(rows, row_refs):
        o_ref[0] = t[roff:roff + r, :].astype(dt)


def _split_w_in(w_in, b_in):
    widths = (ML_WIDTH, ML_WIDTH, ML_WIDTH, ML_WIDTH, ML_HEADS, ML_HEADS,
              NSA_WIDTH, NSA_KV_WIDTH, NSA_KV_WIDTH, NSA_KV_WIDTH, NSA_KV_WIDTH, NSA_KV_WIDTH,
              NSA_KV_WIDTH, 3 * NSA_HEADS, XA_WIDTH, w_in.shape[1])
    names = ("ml_q", "ml_k", "ml_v", "ml_o", "ml_i", "ml_f", "ns_q", "ns_kc", "ns_vc", "ns_ks",
             "ns_vs", "ns_kw", "ns_vw", "ns_g", "xa_q", "mg")
    out, off = {}, 0
    for n, wd in zip(names, widths):
        end = w_in.shape[1] if n == "mg" else off + wd
        out[n] = (w_in[:, off:end], b_in[off:end])
        off = end
    return out


def _inproj(x2, g_mix, w_in, b_in, ml_conv, seq):
    n_tok, d = x2.shape
    cols, ncols, rows, nrows = _proj_layout(d)
    p = _split_w_in(w_in, b_in)

    def pad_cols(w, b, width):
        return (jnp.pad(w, ((0, 0), (0, width - w.shape[1]))), jnp.pad(b, (0, width - b.shape[0])))

    wq, bq = p["ns_q"]
    wq = wq.reshape(d, NSA_HEADS, 1, NSA_DH)
    bq = bq.reshape(NSA_HEADS, 1, NSA_DH)
    in_group = jnp.asarray(np.arange(NSA_HEADS)[:, None] // NSA_REP == np.arange(NSA_KV)[None, :],
                           w_in.dtype)[:, :, None]
    wq_slots = wq * in_group
    bq_slots = bq * in_group
    gate_w = jnp.concatenate([p["ml_i"][0], p["ml_f"][0]], axis=1)
    gate_b = jnp.concatenate([p["ml_i"][1], p["ml_f"][1]])
    small_w = jnp.concatenate([gate_w, p["ns_g"][0]], axis=1)
    small_b = jnp.concatenate([gate_b, p["ns_g"][1]])

    def with_ones(wb, g):
        w, b = wb
        w = jnp.pad(w[:, g * NSA_DH:(g + 1) * NSA_DH], ((0, 0), (0, LANES - NSA_DH)))
        b = jnp.concatenate([b[g * NSA_DH:(g + 1) * NSA_DH], jnp.ones((LANES - NSA_DH,), b.dtype)])
        return w, b

    pieces = {
        "qk": (jnp.concatenate([p["ml_q"][0], p["ml_k"][0]], axis=1),
               jnp.concatenate([p["ml_q"][1], p["ml_k"][1]])),
        "o": p["ml_o"], "mg": p["mg"],
        "gcol": pad_cols(small_w, small_b, LANES),
        "kc": p["ns_kc"], "vc": p["ns_vc"], "v": p["ml_v"],
        "nq": (wq_slots.reshape(d, NSA_HEADS * LANES), bq_slots.reshape(NSA_HEADS * LANES)),
        "xq": p["xa_q"], "ks": p["ns_ks"], "kw": p["ns_kw"],
        "vs0": with_ones(p["ns_vs"], 0), "vs1": with_ones(p["ns_vs"], 1),
        "vw0": with_ones(p["ns_vw"], 0), "vw1": with_ones(p["ns_vw"], 1),
    }
    w_cols = jnp.concatenate([pieces[name][0] for name, *_ in cols], axis=1).astype(BF16)
    b_cols = jnp.concatenate([pieces[name][1] for name, *_ in cols])[None, :]
    w_rows = gate_w.T.astype(BF16)
    b_rows = gate_b[:, None]

    tm = TM_PROJ
    tiles_per_b = seq // tm
    out_shape = ([jax.ShapeDtypeStruct((n_tok, width), dt) for _, _, width, dt in cols]
                 + [jax.ShapeDtypeStruct((n_tok // seq, r, seq), dt) for _, _, r, dt in rows])
    out_specs = ([pl.BlockSpec((tm, width), lambda i: (i, 0)) for _, _, width, _ in cols]
                 + [pl.BlockSpec((1, r, tm), lambda i: (i // tiles_per_b, 0, i % tiles_per_b))
                    for _, _, r, _ in rows])
    outs = pl.pallas_call(
        functools.partial(_inproj_kernel, cols, rows, tiles_per_b),
        out_shape=out_shape,
        grid=(n_tok // tm,),
        in_specs=[pl.BlockSpec((tm, d), lambda i: (i, 0)),
                  _resident((1, d)), _resident((d, ncols)), _resident((1, ncols)),
                  _resident((nrows, d)), _resident((nrows, 1)),
                  _resident((ML_CONV, 2 * ML_WIDTH))],
        out_specs=out_specs,
        scratch_shapes=[pltpu.VMEM((tm + SUBLANES, 2 * ML_WIDTH), F32)],
        compiler_params=pltpu.CompilerParams(dimension_semantics=("arbitrary",),
                                             vmem_limit_bytes=VMEM_LIMIT),
        name="inproj",
    )(x2, g_mix[None, :], w_cols, b_cols, w_rows, b_rows, ml_conv)
    names = [c[0] for c in cols] + [r[0] for r in rows]
    return dict(zip(names, outs))


def _mlstm_kernel(qk_ref, v_ref, o_ref, gcol_ref, grow_ref, ng_ref, y_ref, c_scr, m_scr):
    L = ML_CHUNK
    nb = qk_ref.shape[0]

    @pl.when(pl.program_id(1) == 0)
    def _():
        c_scr[...] = jnp.zeros(c_scr.shape, F32)
        m_scr[...] = jnp.zeros(m_scr.shape, F32)

    r_i = lax.broadcasted_iota(jnp.int32, (L, L), 0)
    c_i = lax.broadcasted_iota(jnp.int32, (L, L), 1)
    causal = c_i <= r_i
    tril = causal.astype(F32)
    triu = (r_i <= c_i).astype(F32)
    gcols = [gcol_ref[b] for b in range(nb)]
    grows = [grow_ref[b] for b in range(nb)]
    b_cols = jnp.dot(tril, jax.nn.log_sigmoid(jnp.concatenate(gcols, axis=1)),
                     precision=lax.Precision.HIGHEST, preferred_element_type=F32)
    b_rows = jnp.dot(jax.nn.log_sigmoid(jnp.concatenate(grows, axis=0)), triu,
                     precision=lax.Precision.HIGHEST, preferred_element_type=F32)
    ones_col = jnp.where(lax.broadcasted_iota(jnp.int32, (L, ML_DH), 1) == 0, 1.0, 0.0).astype(BF16)

    chains = [(b, hh) for b in range(nb) for hh in range(ML_HEADS)]

    def operands(b, hh):
        sl = slice(hh * ML_DH, (hh + 1) * ML_DH)
        q = qk_ref[b, :, sl]
        k = qk_ref[b, :, ML_WIDTH + hh * ML_DH:ML_WIDTH + (hh + 1) * ML_DH]
        v1 = jnp.concatenate([v_ref[b, :, sl], ones_col], axis=1)
        return q, k, v1, c_scr[b, hh]

    ops = [operands(b, hh) for b, hh in chains]
    qk_t = [_dot_nt(q, k) for q, k, _, _ in ops]
    inter = [_dot(q, c_prev.astype(BF16)) for q, _, _, c_prev in ops]

    stab, ke_all, s_all = [], [], []
    for i, (b, hh) in enumerate(chains):
        gcol, grow = gcols[b], grows[b]
        li_c = gcol[:, hh:hh + 1]
        b_c = b_cols[:, b * LANES + ML_HEADS + hh:b * LANES + ML_HEADS + hh + 1]
        li_r = grow[hh:hh + 1, :]
        b_r = b_rows[b * SUBLANES + ML_HEADS + hh:b * SUBLANES + ML_HEADS + hh + 1, :]
        g = b_c[L - 1:L, :]
        m_prev = m_scr[b, hh:hh + 1, 0:1]
        log_d = jnp.where(causal, b_c - b_r + li_r, -jnp.inf)
        inter_log = b_c + m_prev
        m_t = jnp.maximum(inter_log, jnp.max(log_d, axis=-1, keepdims=True))
        s_all.append((qk_t[i] * jnp.exp(log_d - m_t)).astype(BF16))
        w_end = g - b_c + li_c
        m_loc = jnp.max(w_end, axis=0, keepdims=True)
        ke_all.append((ops[i][1] * jnp.exp(w_end - m_loc)).astype(BF16))
        m_new = jnp.maximum(g + m_prev, m_loc)
        stab.append((m_t, jnp.exp(inter_log - m_t), jnp.exp(g + m_prev - m_new),
                     jnp.exp(m_loc - m_new), m_new))

    intra = [_dot(s, o[2]) for s, o in zip(s_all, ops)]
    a_c = [lax.dot_general(ke, o[2], _TN, preferred_element_type=F32) for ke, o in zip(ke_all, ops)]

    ys, new_c, new_m = [], [], []
    for i, (b, hh) in enumerate(chains):
        sl = slice(hh * ML_DH, (hh + 1) * ML_DH)
        m_t, sc, a, bb, m_new = stab[i]
        nd = intra[i] + sc * inter[i]
        den = nd[:, ML_DH:ML_DH + 1]
        h = nd[:, 0:ML_DH] / jnp.maximum(jnp.abs(den), jnp.exp(-m_t))
        hn = h * lax.rsqrt(jnp.mean(h * h, axis=-1, keepdims=True) + EPS)
        ys.append((o_ref[b, :, sl] * hn * ng_ref[:, sl]).astype(y_ref.dtype))
        new_c.append(a * ops[i][3] + bb * a_c[i])
        new_m.append(jnp.broadcast_to(m_new, (1, m_scr.shape[2])))

    per_b = lambda xs, axis: [jnp.concatenate(xs[b * ML_HEADS:(b + 1) * ML_HEADS], axis=axis)
                              for b in range(nb)]
    y_ref[...] = jnp.stack(per_b(ys, 1))
    c_scr[...] = jnp.stack([jnp.stack(new_c[b * ML_HEADS:(b + 1) * ML_HEADS]) for b in range(nb)])
    m_scr[:, 0:ML_HEADS, :] = jnp.stack(per_b(new_m, 0))


def _mlstm(pr, ml_norm_g, batch, seq):
    L = ML_CHUNK
    nb = ML_BATCH_PER_STEP
    chunk = lambda b, c: (b, c, 0)
    per_tok = lambda name: pr[name].reshape(batch, seq, pr[name].shape[-1])
    y = pl.pallas_call(
        _mlstm_kernel,
        out_shape=jax.ShapeDtypeStruct((batch, seq, ML_WIDTH), BF16),
        grid=(batch // nb, seq // L),
        in_specs=[pl.BlockSpec((nb, L, 2 * ML_WIDTH), chunk),
                  pl.BlockSpec((nb, L, ML_WIDTH), chunk),
                  pl.BlockSpec((nb, L, ML_WIDTH), chunk),
                  pl.BlockSpec((nb, L, LANES), chunk),
                  pl.BlockSpec((nb, SUBLANES, L), lambda b, c: (b, 0, c)),
                  pl.BlockSpec((1, ML_WIDTH), lambda b, c: (0, 0))],
        out_specs=pl.BlockSpec((nb, L, ML_WIDTH), chunk),
        scratch_shapes=[pltpu.VMEM((nb, ML_HEADS, ML_DH, 2 * ML_DH), F32),
                        pltpu.VMEM((nb, SUBLANES, LANES), F32)],
        compiler_params=pltpu.CompilerParams(dimension_semantics=("parallel", "arbitrary"),
                                             vmem_limit_bytes=VMEM_LIMIT),
        name="mlstm",
    )(per_tok("qk"), per_tok("v"), per_tok("o"), per_tok("gcol"), pr["smallT"],
      ml_norm_g[None, :])
    return y.reshape(batch * seq, ML_WIDTH)


def _gelu_tanh(x):
    return 0.5 * x * (1.0 + jnp.tanh(np.sqrt(2.0 / np.pi) * (x + 0.044715 * (x * x * x))))


def _compress_kernel(k_ref, v_ref, pe_ref, w1_ref, w2k_ref, w2v_ref, kc_ref, vc0_ref, vc1_ref):
    nrow = kc_ref.shape[1]
    S = CMP_STRIDE

    def hidden(x_ref, which):
        lo = hi = None
        for j in range(0, S, 2):
            x = [x_ref[pl.ds(j + d, nrow, stride=S), :] for d in range(2)]
            for half in range(CMP_LEN // S):
                l0 = half * S + j
                xp = jnp.concatenate([(x[d] + pe_ref[which, l0 + d:l0 + d + 1, :]).astype(BF16)
                                      for d in range(2)], axis=1)
                t = _dot(xp, w1_ref[which, l0 // 2])
                if half == 0:
                    lo = t if lo is None else lo + t
                else:
                    hi = t if hi is None else hi + t
        return _gelu_tanh(lo + pltpu.roll(hi, nrow - 1, 0)).astype(BF16)

    kc_ref[0] = _dot(hidden(k_ref, 0), w2k_ref[...]).astype(kc_ref.dtype)
    hv = hidden(v_ref, 1)
    ones_half = lax.broadcasted_iota(jnp.int32, (nrow, LANES), 1) >= NSA_DH
    for g, o_ref in enumerate((vc0_ref, vc1_ref)):
        o_ref[0] = jnp.where(ones_half, 1.0, _dot(hv, w2v_ref[g])).astype(o_ref.dtype)


def _compress(pr, cmp_pe, cmp_w1, cmp_w2, batch, seq):
    nrow = seq // CMP_STRIDE
    hid = NSA_KV * CMP_HIDDEN
    w1 = cmp_w1.astype(BF16).reshape(2, CMP_LEN, NSA_DH, CMP_HIDDEN)
    z1 = jnp.zeros_like(w1)
    w1e = jnp.stack([jnp.concatenate([w1, z1], axis=-1), jnp.concatenate([z1, w1], axis=-1)], axis=2)
    w1e = w1e.reshape(2, CMP_LEN // 2, 2 * NSA_KV_WIDTH, hid)
    w2 = cmp_w2.astype(BF16)
    z2 = jnp.zeros_like(w2[0])
    w2k = jnp.concatenate([jnp.concatenate([w2[0], z2], axis=1),
                           jnp.concatenate([z2, w2[0]], axis=1)], axis=0)
    w2v = jnp.stack([jnp.pad(w2[1], ((g * CMP_HIDDEN, (NSA_KV - 1 - g) * CMP_HIDDEN),
                                     (0, LANES - NSA_DH))) for g in range(NSA_KV)])
    pe = jnp.concatenate([cmp_pe] * NSA_KV, axis=-1)
    tok_blk = pl.BlockSpec((seq, NSA_KV_WIDTH), lambda b: (b, 0))
    out_blk = pl.BlockSpec((1, nrow, LANES), lambda b: (b, 0, 0))
    return pl.pallas_call(
        _compress_kernel,
        out_shape=(jax.ShapeDtypeStruct((batch, nrow, LANES), BF16),) * 3,
        grid=(batch,),
        in_specs=[tok_blk, tok_blk,
                  _resident((2, CMP_LEN, NSA_KV_WIDTH)),
                  _resident((2, CMP_LEN // 2, 2 * NSA_KV_WIDTH, hid)),
                  _resident((hid, NSA_KV_WIDTH)),
                  _resident((NSA_KV, hid, LANES))],
        out_specs=(out_blk, out_blk, out_blk),
        compiler_params=pltpu.CompilerParams(dimension_semantics=("parallel",),
                                             vmem_limit_bytes=VMEM_LIMIT),
        name="compress",
    )(pr["kc"], pr["vc"], pe, w1e, w2k, w2v)


def _alibi_slope(h):
    return float(2.0 ** (-8.0 * (h + 1) / NSA_HEADS))


def _nsa_kernel_v2(nbs, q_ref, kc_ref, vct_ref, ks_ref, vst_ref, kw_ref, vwt_ref, gt_ref, ovt_ref,
                y_ref, selx, qs_scr, ab_scr, m_scr, l_scr, acc_scr):
    QB = Q_BLOCK
    KT = SEL_KT
    t0 = pl.program_id(1) * QB
    ncmp = kc_ref.shape[1]
    gates = jax.nn.sigmoid(gt_ref[...])
    q_lane = lax.broadcasted_iota(jnp.int32, (1, QB), 1)
    tpos = t0 + q_lane
    qs_scr[...] = q_ref[...] * 0.125

    @pl.when(pl.program_id(1) == 0)
    def _():
        rel = (lax.broadcasted_iota(jnp.int32, (KT, QB), 1)
               - lax.broadcasted_iota(jnp.int32, (KT, QB), 0)).astype(F32)
        for h in range(NSA_HEADS):
            ab_scr[h] = -_alibi_slope(h) * rel

    def head_q(h):
        return qs_scr[:, h * LANES:(h + 1) * LANES]

    def group_rows(x, g):
        return x[g * NSA_DH:(g + 1) * NSA_DH, :]

    n_sub = lax.broadcasted_iota(jnp.int32, (ncmp, QB), 0)
    dist_c = tpos - (n_sub * CMP_STRIDE + (CMP_LEN - 1))
    ok_c = dist_c >= 0
    dist_cf = dist_c.astype(F32)
    kc = kc_ref[0]
    vct = vct_ref[0]
    o_cmp = []
    for g in range(NSA_KV):
        psum = jnp.zeros((ncmp, QB), F32)
        for r in range(NSA_REP):
            h = g * NSA_REP + r
            s = _dot_nt(kc, head_q(h)) - _alibi_slope(h) * dist_cf
            s = jnp.where(ok_c, s, NEG)
            e = jnp.exp(s - jnp.max(s, axis=0, keepdims=True))
            p = jnp.where(ok_c, e / jnp.sum(e, axis=0, keepdims=True), 0.0)
            psum = psum + p
            o_cmp.append(group_rows(_dot(vct, p.astype(BF16)), g))
        imp = jnp.dot(ovt_ref[...], psum, precision=lax.Precision.HIGHEST,
                      preferred_element_type=F32)
        j_sub = lax.broadcasted_iota(jnp.int32, (nbs, QB), 0)
        cur = tpos // SEL_LEN
        forced = (j_sub == 0) | (j_sub == cur) | (j_sub == cur - 1)
        score = jnp.where(j_sub <= cur, imp + jnp.where(forced, FORCE_BONUS, 0.0), NEG)
        n_grp = nbs // SUBLANES
        parts = [score[a * SUBLANES:(a + 1) * SUBLANES, :] for a in range(n_grp)]
        ranks = [jnp.zeros((SUBLANES, QB), F32) for _ in range(n_grp)]
        sub8 = lax.broadcasted_iota(jnp.int32, (SUBLANES, QB), 0)
        for i in range(nbs):
            row = jnp.broadcast_to(score[i:i + 1, :], (SUBLANES, QB))
            for a in range(n_grp):
                if a * SUBLANES > i:
                    beats = jnp.where(row >= parts[a], 1.0, 0.0)
                elif (a + 1) * SUBLANES - 1 < i:
                    beats = jnp.where(row > parts[a], 1.0, 0.0)
                else:
                    beats = jnp.where(sub8 + a * SUBLANES > i,
                                      jnp.where(row >= parts[a], 1.0, 0.0),
                                      jnp.where(row > parts[a], 1.0, 0.0))
                ranks[a] = ranks[a] + beats
        for a in range(n_grp):
            sel_bias = jnp.where(ranks[a] < float(min(SEL_TOP, nbs)), 0.0, NEG)
            for jj in range(SUBLANES):
                j = a * SUBLANES + jj
                selx[g, j * SUBLANES:(j + 1) * SUBLANES, :] = jnp.broadcast_to(
                    sel_bias[jj:jj + 1, :], (SUBLANES, QB))

    blocks_per_tile = KT // SEL_LEN
    m_scr[...] = jnp.full(m_scr.shape, NEG, F32)
    l_scr[...] = jnp.zeros(l_scr.shape, F32)
    acc_scr[...] = jnp.zeros(acc_scr.shape, F32)
    last_tile = t0 // KT

    def sel_tile(kt, causal):
        k0 = pl.multiple_of(kt * KT, KT)
        ktile = ks_ref[pl.ds(k0, KT), :]
        row0 = pl.multiple_of(kt * (blocks_per_tile * SUBLANES), SUBLANES)
        if causal:
            k_sub = lax.broadcasted_iota(jnp.int32, (KT, QB), 0)
            visible = (q_lane - k_sub) >= (k0 - t0)
        for g in range(NSA_KV):
            sel = [selx[g, pl.ds(row0 + jj * SUBLANES, SUBLANES), :] for jj in range(blocks_per_tile)]
            vt = vst_ref[g * NSA_DH:(g + 1) * NSA_DH, pl.ds(k0, KT)]
            for r in range(NSA_REP):
                h = g * NSA_REP + r
                shift = _alibi_slope(h) * k0.astype(F32)
                bias = jnp.concatenate([jnp.tile(sb + shift, (SEL_LEN // SUBLANES, 1)) for sb in sel],
                                       axis=0)
                s = _dot_nt(ktile, head_q(h)) + ab_scr[h] + bias
                if causal:
                    s = jnp.where(visible, s, NEG)
                m_old = m_scr[h]
                m_new = jnp.maximum(m_old, jnp.max(s, axis=0, keepdims=True))
                alpha = jnp.exp(m_old - m_new)
                p = jnp.exp(s - m_new)
                l_scr[h] = alpha * l_scr[h] + jnp.sum(p, axis=0, keepdims=True)
                acc_scr[h] = alpha * acc_scr[h] + _dot(vt, p.astype(BF16))
                m_scr[h] = m_new

    def full_tile(kt, carry):
        sel_tile(kt, False)
        return carry

    lax.fori_loop(0, last_tile, full_tile, 0)
    sel_tile(last_tile, True)
    o_sel = [acc_scr[h] / l_scr[h] for h in range(NSA_HEADS)]

    WK = WINDOW + QB
    ws = pl.multiple_of(jnp.maximum(t0 - WINDOW, 0), QB)
    kw_sub = lax.broadcasted_iota(jnp.int32, (WK, QB), 0)
    dist_w = (t0 - ws) + q_lane - kw_sub
    ok_w = (dist_w >= 0) & (dist_w < WINDOW)
    dist_wf = dist_w.astype(F32)
    kwin = kw_ref[pl.ds(ws, WK), :]
    vwin = vwt_ref[:, pl.ds(ws, WK)]
    o_win = []
    for g in range(NSA_KV):
        for r in range(NSA_REP):
            h = g * NSA_REP + r
            s = _dot_nt(kwin, head_q(h)) - _alibi_slope(h) * dist_wf
            s = jnp.where(ok_w, s, NEG)
            p = jnp.exp(s - jnp.max(s, axis=0, keepdims=True))
            l = jnp.sum(p, axis=0, keepdims=True)
            o_win.append(group_rows(_dot(vwin, p.astype(BF16)), g) / l)

    outs = []
    for h in range(NSA_HEADS):
        g0 = SUBLANES + 3 * h
        outs.append(gates[g0:g0 + 1, :] * o_cmp[h] + gates[g0 + 1:g0 + 2, :] * o_sel[h]
                    + gates[g0 + 2:g0 + 3, :] * o_win[h])
    for pair in range(NSA_HEADS // 2):
        both = jnp.concatenate([outs[2 * pair], outs[2 * pair + 1]], axis=0)
        y_ref[:, pair * LANES:(pair + 1) * LANES] = both.T.astype(y_ref.dtype)


def _nsa_v2(pr, kc, vct, batch, seq):
    QB = Q_BLOCK
    nq = seq // QB
    nbs = seq // SEL_LEN
    ncmp = kc.shape[1]
    cs = np.arange(ncmp) * CMP_STRIDE
    js = np.arange(nbs) * SEL_LEN
    ovt = jnp.asarray(((cs[None, :] < js[:, None] + SEL_LEN)
                       & (cs[None, :] + CMP_LEN > js[:, None])).astype(np.float32))
    per_b = lambda b, i: (b, 0)
    return pl.pallas_call(
        functools.partial(_nsa_kernel, nbs),
        out_shape=jax.ShapeDtypeStruct((batch * seq, NSA_WIDTH), BF16),
        grid=(batch, nq),
        in_specs=[pl.BlockSpec((QB, NSA_HEADS * LANES), lambda b, i: (b * nq + i, 0)),
                  pl.BlockSpec((1, ncmp, NSA_KV_WIDTH), lambda b, i: (b, 0, 0)),
                  pl.BlockSpec((1, NSA_KV_WIDTH, ncmp), lambda b, i: (b, 0, 0)),
                  pl.BlockSpec((seq, NSA_KV_WIDTH), per_b),
                  pl.BlockSpec((NSA_KV_WIDTH, seq), lambda b, i: (0, b)),
                  pl.BlockSpec((seq, NSA_KV_WIDTH), per_b),
                  pl.BlockSpec((NSA_KV_WIDTH, seq), lambda b, i: (0, b)),
                  pl.BlockSpec((32, QB), lambda b, i: (0, b * nq + i)),
                  pl.BlockSpec((nbs, ncmp), lambda b, i: (0, 0))],
        out_specs=pl.BlockSpec((QB, NSA_WIDTH), lambda b, i: (b * nq + i, 0)),
        scratch_shapes=[pltpu.VMEM((NSA_KV, nbs * SUBLANES, QB), F32),
                        pltpu.VMEM((QB, NSA_HEADS * LANES), BF16),
                        pltpu.VMEM((NSA_HEADS, SEL_KT, QB), F32),
                        pltpu.VMEM((NSA_HEADS, 1, QB), F32),
                        pltpu.VMEM((NSA_HEADS, 1, QB), F32),
                        pltpu.VMEM((NSA_HEADS, NSA_DH, QB), F32)],
        compiler_params=pltpu.CompilerParams(dimension_semantics=("parallel", "arbitrary"),
                                             vmem_limit_bytes=VMEM_LIMIT),
        name="nsa",
    )(pr["nq"], kc, vct, pr["ks"], pr["vsT"], pr["kw"], pr["vwT"], pr["smallT"], ovt)


NSA_ROWS = NSA_HEADS * Q_BLOCK
WIN_KEYS = WINDOW + Q_BLOCK
F_SEL_BLOCK = 0
F_SEL_OFF = SEL_LEN
F_WIN_OFF = SEL_LEN + 1
F_WIN_BLK = SEL_LEN + 2
F_CMP = SEL_LEN + 3


def _tile_heads(x):
    return jnp.concatenate([x] * NSA_HEADS, axis=0)


def _nsa_kernel(nbs, q_ref, g_ref, kc_ref, vc0_ref, vc1_ref, ks_ref, vs0_ref, vs1_ref,
                kw_ref, vw0_ref, vw1_ref, ov_ref, y_ref,
                qe_scr, posf_scr, wposf_scr, cposf_scr, s_scr, mrun_scr, mb_scr, acc_scr):
    QB = Q_BLOCK
    KT = SEL_KT
    HALF = NSA_REP * QB
    t0 = pl.program_id(1) * QB
    ncmp = kc_ref.shape[1]
    lane = lax.broadcasted_iota(jnp.int32, (1, LANES), 1)
    lo_half = lane < NSA_DH

    @pl.when(pl.program_id(1) == 0)
    def _():
        seq = posf_scr.shape[0]
        k = lax.broadcasted_iota(jnp.int32, (seq, LANES), 0)
        c = lax.broadcasted_iota(jnp.int32, (seq, LANES), 1)
        posf = jnp.where(c < SEL_LEN, jnp.where(k // SEL_LEN == c, 1.0, 0.0),
                         jnp.where(c == F_SEL_OFF, (k % SEL_LEN).astype(F32), 0.0))
        posf_scr[...] = posf.astype(BF16)
        k = lax.broadcasted_iota(jnp.int32, (WIN_KEYS, LANES), 0)
        c = lax.broadcasted_iota(jnp.int32, (WIN_KEYS, LANES), 1)
        wposf = jnp.where(c == F_WIN_OFF, (k % SEL_LEN).astype(F32),
                          jnp.where(c == F_WIN_BLK, (k // SEL_LEN).astype(F32), 0.0))
        wposf_scr[...] = wposf.astype(BF16)
        k = lax.broadcasted_iota(jnp.int32, (ncmp, LANES), 0)
        c = lax.broadcasted_iota(jnp.int32, (ncmp, LANES), 1)
        cposf_scr[...] = jnp.where(c == F_CMP, k.astype(F32), 0.0).astype(BF16)

    def const_feat(h):
        s = _alibi_slope(h)
        return jnp.where((lane == F_SEL_OFF) | (lane == F_WIN_OFF), s,
                         jnp.where(lane == F_WIN_BLK, SEL_LEN * s,
                                   jnp.where(lane == F_CMP, CMP_STRIDE * s, 0.0)))

    for h in range(NSA_HEADS):
        rows = slice(h * QB, (h + 1) * QB)
        qe_scr[rows, 0:LANES] = q_ref[:, h * LANES:(h + 1) * LANES] * 0.125
        qe_scr[rows, LANES:2 * LANES] = jnp.broadcast_to(const_feat(h), (QB, LANES)).astype(BF16)

    def normalised(o, guard):
        outs = []
        for pair in range(NSA_HEADS // 2):
            e = o[(2 * pair) * QB:(2 * pair + 1) * QB]
            d = o[(2 * pair + 1) * QB:(2 * pair + 2) * QB]
            num = jnp.where(lo_half, e, pltpu.roll(d, NSA_DH, 1))
            den = jnp.where(lo_half, pltpu.roll(e, NSA_DH, 1), d)
            if guard:
                den = jnp.where(den > 0.0, den, 1.0)
            outs.append(num / den)
        return outs

    q_sub = lax.broadcasted_iota(jnp.int32, (QB, ncmp), 0)
    n_lane = lax.broadcasted_iota(jnp.int32, (QB, ncmp), 1)
    ok_c = n_lane * CMP_STRIDE + (CMP_LEN - 1) <= t0 + q_sub
    s = _dot_nt(qe_scr[...], jnp.concatenate([kc_ref[0], cposf_scr[...]], axis=1))
    s = s + _tile_heads(jnp.where(ok_c, 0.0, NEG))
    e = jnp.exp(s - jnp.max(s, axis=-1, keepdims=True)) * _tile_heads(jnp.where(ok_c, 1.0, 0.0))
    e_bf = e.astype(BF16)
    oc = jnp.concatenate([_dot(e_bf[0:HALF], vc0_ref[0]), _dot(e_bf[HALF:2 * HALF], vc1_ref[0])],
                         axis=0)
    o_cmp = normalised(oc, True)
    l_c = jnp.where(lo_half, pltpu.roll(oc, NSA_DH, 1), oc)
    inv_l = 1.0 / jnp.where(l_c > 0.0, l_c, 1.0)
    p_c = e * jnp.concatenate([inv_l] * (ncmp // LANES), axis=1)

    imp_ts = []
    for g in range(NSA_KV):
        psum = p_c[(g * NSA_REP) * QB:(g * NSA_REP + 1) * QB]
        for r in range(1, NSA_REP):
            psum = psum + p_c[(g * NSA_REP + r) * QB:(g * NSA_REP + r + 1) * QB]
        p_hi = psum.astype(BF16)
        p_lo = (psum - p_hi.astype(F32)).astype(BF16)
        imp = _dot(p_hi, ov_ref[...]) + _dot(p_lo, ov_ref[...])
        imp_ts.append(imp.T[0:nbs, :])

    ws = pl.multiple_of(jnp.maximum(t0 - WINDOW, 0), QB)
    qs = lax.broadcasted_iota(jnp.int32, (QB, WIN_KEYS), 0)
    kl = lax.broadcasted_iota(jnp.int32, (QB, WIN_KEYS), 1)
    dist_w = (t0 - ws) + qs - kl
    ok_w = (dist_w >= 0) & (dist_w < WINDOW)
    s_w = _dot_nt(qe_scr[...], jnp.concatenate([kw_ref[pl.ds(ws, WIN_KEYS), :], wposf_scr[...]], axis=1))

    n_grp = nbs // SUBLANES
    sub8 = lax.broadcasted_iota(jnp.int32, (SUBLANES, QB), 0)
    tpos = t0 + lax.broadcasted_iota(jnp.int32, (1, QB), 1)
    cur = tpos // SEL_LEN
    j_lane = lane.astype(F32)
    for g in range(NSA_KV):
        imp_t = imp_ts[g]
        j_sub = lax.broadcasted_iota(jnp.int32, (nbs, QB), 0)
        forced = (j_sub == 0) | (j_sub == cur) | (j_sub == cur - 1)
        score = jnp.where(j_sub <= cur, imp_t + jnp.where(forced, FORCE_BONUS, 0.0), NEG)
        parts = [score[a * SUBLANES:(a + 1) * SUBLANES, :] for a in range(n_grp)]
        ranks = [jnp.zeros((SUBLANES, QB), F32) for _ in range(n_grp)]
        for i in range(nbs):
            row = jnp.broadcast_to(score[i:i + 1, :], (SUBLANES, QB))
            for a in range(n_grp):
                if a * SUBLANES > i:
                    beats = jnp.where(row >= parts[a], 1.0, 0.0)
                elif (a + 1) * SUBLANES - 1 < i:
                    beats = jnp.where(row > parts[a], 1.0, 0.0)
                else:
                    beats = jnp.where(sub8 + a * SUBLANES > i,
                                      jnp.where(row >= parts[a], 1.0, 0.0),
                                      jnp.where(row > parts[a], 1.0, 0.0))
                ranks[a] = ranks[a] + beats
        sel_t = [jnp.where(rk < float(min(SEL_TOP, nbs)), 0.0, NEG) for rk in ranks]
        if nbs < LANES:
            sel_t.append(jnp.zeros((LANES - nbs, QB), F32))
        sel_bias = jnp.concatenate(sel_t, axis=0).T
        for r in range(NSA_REP):
            h = g * NSA_REP + r
            feat = jnp.where(lo_half, sel_bias + (_alibi_slope(h) * SEL_LEN) * j_lane, const_feat(h))
            qe_scr[h * QB:(h + 1) * QB, LANES:2 * LANES] = feat.astype(BF16)

    s_w = s_w + _tile_heads(jnp.where(ok_w, 0.0, NEG))
    p_w = jnp.exp(s_w - jnp.max(s_w, axis=-1, keepdims=True)).astype(BF16)
    o_win = normalised(jnp.concatenate([_dot(p_w[0:HALF], vw0_ref[pl.ds(ws, WIN_KEYS), :]),
                                        _dot(p_w[HALF:2 * HALF], vw1_ref[pl.ds(ws, WIN_KEYS), :])],
                                       axis=0), False)

    last_tile = t0 // KT
    mrun_scr[...] = jnp.full(mrun_scr.shape, NEG, F32)

    def score_tiles(tiles):
        tile_max = None
        for kt, causal in tiles:
            k0 = pl.multiple_of(kt * KT, KT)
            kx = jnp.concatenate([ks_ref[pl.ds(k0, KT), :], posf_scr[pl.ds(k0, KT), :]], axis=1)
            s = _dot_nt(qe_scr[...], kx)
            if causal:
                qs = lax.broadcasted_iota(jnp.int32, (QB, KT), 0)
                kl = lax.broadcasted_iota(jnp.int32, (QB, KT), 1)
                s = s + _tile_heads(jnp.where(k0 + kl <= t0 + qs, 0.0, NEG))
            s_scr[kt] = s
            for c0 in range(0, KT, LANES):
                part = s[:, c0:c0 + LANES]
                tile_max = part if tile_max is None else jnp.maximum(tile_max, part)
        mrun_scr[...] = jnp.maximum(mrun_scr[...], tile_max)

    def score_pair(i, carry):
        score_tiles([(2 * i, False), (2 * i + 1, False)])
        return carry

    lax.fori_loop(0, last_tile // 2, score_pair, 0)

    @pl.when(last_tile % 2 == 1)
    def _():
        score_tiles([(last_tile - 1, False), (last_tile, True)])

    @pl.when(last_tile % 2 == 0)
    def _():
        score_tiles([(last_tile, True)])

    mb_scr[...] = jnp.broadcast_to(jnp.max(mrun_scr[...], axis=-1, keepdims=True), mb_scr.shape)
    acc_scr[...] = jnp.zeros(acc_scr.shape, F32)

    def value_tiles(tiles):
        mb = mb_scr[...]
        mbw = jnp.concatenate([mb] * (KT // LANES), axis=1)
        new = [None, None]
        for kt in tiles:
            k0 = pl.multiple_of(kt * KT, KT)
            p = jnp.exp(s_scr[kt] - mbw).astype(BF16)
            for g, v_ref in enumerate((vs0_ref, vs1_ref)):
                pv = _dot(p[g * HALF:(g + 1) * HALF], v_ref[pl.ds(k0, KT), :])
                new[g] = pv if new[g] is None else new[g] + pv
        acc_scr[0:HALF, :] += new[0]
        acc_scr[HALF:2 * HALF, :] += new[1]

    def value_pair(i, carry):
        value_tiles([2 * i, 2 * i + 1])
        return carry

    n_tiles = last_tile + 1
    lax.fori_loop(0, n_tiles // 2, value_pair, 0)

    @pl.when(n_tiles % 2 == 1)
    def _():
        value_tiles([last_tile])

    o_sel = normalised(acc_scr[...], False)

    gate = jax.nn.sigmoid(g_ref[...])
    for pair in range(NSA_HEADS // 2):
        y = None
        for b, o in enumerate((o_cmp, o_sel, o_win)):
            c_e = SUBLANES + 3 * (2 * pair) + b
            c_o = SUBLANES + 3 * (2 * pair + 1) + b
            term = jnp.where(lo_half, gate[:, c_e:c_e + 1], gate[:, c_o:c_o + 1]) * o[pair]
            y = term if y is None else y + term
        y_ref[:, pair * LANES:(pair + 1) * LANES] = y.astype(y_ref.dtype)


def _nsa(pr, kc, vc0, vc1, batch, seq):
    QB = Q_BLOCK
    nq = seq // QB
    nbs = seq // SEL_LEN
    ncmp = kc.shape[1]
    cs = np.arange(ncmp) * CMP_STRIDE
    js = np.arange(nbs) * SEL_LEN
    ov = np.zeros((ncmp, LANES), np.float32)
    ov[:, :nbs] = (cs[:, None] < js[None, :] + SEL_LEN) & (cs[:, None] + CMP_LEN > js[None, :])
    tok = lambda b, i: (b * nq + i, 0)
    per_b = lambda b, i: (b, 0)
    per_b3 = lambda b, i: (b, 0, 0)
    kv_spec = pl.BlockSpec((seq, LANES), per_b)
    cmp_spec = pl.BlockSpec((1, ncmp, LANES), per_b3)
    return pl.pallas_call(
        functools.partial(_nsa_kernel, nbs),
        out_shape=jax.ShapeDtypeStruct((batch * seq, NSA_WIDTH), BF16),
        grid=(batch, nq),
        in_specs=[pl.BlockSpec((QB, NSA_HEADS * LANES), tok),
                  pl.BlockSpec((QB, LANES), tok),
                  cmp_spec, cmp_spec, cmp_spec,
                  kv_spec, kv_spec, kv_spec, kv_spec, kv_spec, kv_spec,
                  pl.BlockSpec((ncmp, LANES), lambda b, i: (0, 0))],
        out_specs=pl.BlockSpec((QB, NSA_WIDTH), tok),
        scratch_shapes=[pltpu.VMEM((NSA_ROWS, 2 * LANES), BF16),
                        pltpu.VMEM((seq, LANES), BF16),
                        pltpu.VMEM((WIN_KEYS, LANES), BF16),
                        pltpu.VMEM((ncmp, LANES), BF16),
                        pltpu.VMEM((seq // SEL_KT, NSA_ROWS, SEL_KT), F32),
                        pltpu.VMEM((NSA_ROWS, LANES), F32),
                        pltpu.VMEM((NSA_ROWS, LANES), F32),
                        pltpu.VMEM((NSA_ROWS, LANES), F32)],
        compiler_params=pltpu.CompilerParams(dimension_semantics=("parallel", "arbitrary"),
                                             vmem_limit_bytes=VMEM_LIMIT),
        name="nsa",
    )(pr["nq"], pr["gcol"], kc, vc0, vc1, pr["ks"], pr["vs0"], pr["vs1"],
      pr["kw"], pr["vw0"], pr["vw1"], jnp.asarray(ov, BF16))


def _memkv_kernel(mem_ref, g_ref, w_ref, kv_ref):
    u = _rms(mem_ref[0], g_ref[...]).astype(BF16)
    kv_ref[0] = _dot(u, w_ref[...]).astype(kv_ref.dtype)


def _memkv(mem, g_mem, w_mem_kv):
    batch, n_mem, d = mem.shape
    return pl.pallas_call(
        _memkv_kernel,
        out_shape=jax.ShapeDtypeStruct((batch, n_mem, 2 * XA_WIDTH), BF16),
        grid=(batch,),
        in_specs=[pl.BlockSpec((1, n_mem, d), lambda b: (b, 0, 0)),
                  _resident((1, d)), _resident((d, 2 * XA_WIDTH))],
        out_specs=pl.BlockSpec((1, n_mem, 2 * XA_WIDTH), lambda b: (b, 0, 0)),
        compiler_params=pltpu.CompilerParams(dimension_semantics=("parallel",),
                                             vmem_limit_bytes=VMEM_LIMIT),
        name="memkv",
    )(mem, g_mem[None, :], w_mem_kv.astype(BF16))


def _tail_kernel(x_ref, yml_ref, ynsa_ref, xq_ref, kv_ref, mg_ref, wb_ref, wo_ref, gf_ref,
                 w1_ref, w2_ref, gl_ref, out_ref):
    d = x_ref.shape[1]
    kv = kv_ref[0]
    heads = [slice(hh * XA_DH, (hh + 1) * XA_DH) for hh in range(XA_HEADS)]
    scores = [_dot_nt(xq_ref[:, sl], kv[:, sl]) * (XA_DH ** -0.5) for sl in heads]
    ups = [_dot(yml_ref[...], wb_ref[0]), _dot(ynsa_ref[...], wb_ref[1])]
    y_xa = []
    for hh, s in enumerate(scores):
        p = jnp.exp(s - jnp.max(s, axis=-1, keepdims=True))
        l = jnp.sum(p, axis=-1, keepdims=True)
        y_xa.append((_dot(p.astype(BF16), kv[:, XA_WIDTH + hh * XA_DH:XA_WIDTH + (hh + 1) * XA_DH])
                     / l).astype(BF16))
    ups.append(_dot(jnp.concatenate(y_xa, axis=-1), wb_ref[2]))
    merged = None
    for j in range(N_BRANCH):
        term = jax.nn.sigmoid(mg_ref[:, j * d:(j + 1) * d]) * ups[j]
        merged = term if merged is None else merged + term
    h = x_ref[...] + _dot(merged.astype(BF16), wo_ref[...])
    u = _rms(h, gf_ref[...]).astype(BF16)
    acc = h
    for c0 in range(0, w1_ref.shape[1], FF_SLAB):
        a = jnp.maximum(_dot(u, w1_ref[:, c0:c0 + FF_SLAB]), 0.0)
        acc = acc + _dot((a * a).astype(BF16), w2_ref[c0:c0 + FF_SLAB, :])
    out_ref[...] = _rms(acc, gl_ref[...])


def _tail(x2, y_ml, y_nsa, pr, mem_kv, w_branch, w_out, g_ffn, w_ff1, w_ff2, g_final, seq):
    n_tok, d = x2.shape
    tm = TM_TAIL
    d_ff = w_ff1.shape[1]
    n_mem = mem_kv.shape[1]
    tiles_per_b = seq // tm
    row = lambda i: (i, 0)
    return pl.pallas_call(
        _tail_kernel,
        out_shape=jax.ShapeDtypeStruct((n_tok, d), F32),
        grid=(n_tok // tm,),
        in_specs=[pl.BlockSpec((tm, d), row),
                  pl.BlockSpec((tm, ML_WIDTH), row),
                  pl.BlockSpec((tm, NSA_WIDTH), row),
                  pl.BlockSpec((tm, XA_WIDTH), row),
                  pl.BlockSpec((1, n_mem, 2 * XA_WIDTH), lambda i: (i // tiles_per_b, 0, 0)),
                  pl.BlockSpec((tm, N_BRANCH * d), row),
                  _resident((N_BRANCH, ML_WIDTH, d)), _resident((d, d)), _resident((1, d)),
                  _resident((d, d_ff)), _resident((d_ff, d)), _resident((1, d))],
        out_specs=pl.BlockSpec((tm, d), row),
        compiler_params=pltpu.CompilerParams(dimension_semantics=("parallel",),
                                             vmem_limit_bytes=VMEM_LIMIT),
        name="tail",
    )(x2, y_ml, y_nsa, pr["xq"], mem_kv, pr["mg"], w_branch.astype(BF16), w_out.astype(BF16),
      g_ffn[None, :], w_ff1.astype(BF16), w_ff2.astype(BF16), g_final[None, :])


def _layer(x, mem, g_mix, w_in, b_in, ml_conv, ml_norm_g, cmp_pe, cmp_w1, cmp_w2, g_mem, w_mem_kv,
           w_branch, w_out, g_ffn, w_ff1, w_ff2, g_final):
    batch, seq, d = x.shape
    x2 = x.reshape(batch * seq, d)
    pr = _inproj(x2, g_mix, w_in, b_in, ml_conv, seq)
    y_ml = _mlstm(pr, ml_norm_g, batch, seq)
    kc, vc0, vc1 = _compress(pr, cmp_pe, cmp_w1, cmp_w2, batch, seq)
    y_nsa = _nsa(pr, kc, vc0, vc1, batch, seq)
    mem_kv = _memkv(mem, g_mem, w_mem_kv)
    out = _tail(x2, y_ml, y_nsa, pr, mem_kv, w_branch, w_out, g_ffn, w_ff1, w_ff2, g_final, seq)
    return out.reshape(batch, seq, d)


def kernel(x, mem, g_mix, w_in, b_in, ml_conv, ml_norm_g, cmp_pe, cmp_w1, cmp_w2, g_mem, w_mem_kv,
           w_branch, w_out, g_ffn, w_ff1, w_ff2, g_final):
    assert g_mix.shape[0] == 1, "single-layer block"
    return _layer(x, mem, g_mix[0], w_in[0], b_in[0], ml_conv[0], ml_norm_g[0], cmp_pe[0],
                  cmp_w1[0], cmp_w2[0], g_mem[0], w_mem_kv[0], w_branch[0], w_out[0], g_ffn[0],
                  w_ff1[0], w_ff2[0], g_final)
```

```python
import functools

import numpy as np
import jax
import jax.numpy as jnp
from jax import lax
from jax.experimental import pallas as pl
from jax.experimental.pallas import tpu as pltpu

F32 = jnp.float32
BF16 = jnp.bfloat16

EPS = 1e-6
NEG = -1e30
ML_HEADS = 4
ML_DH = 128
ML_WIDTH = ML_HEADS * ML_DH
ML_CONV = 4
ML_CHUNK = 128
ML_BATCH_PER_STEP = 1
NSA_HEADS = 8
NSA_KV = 2
NSA_REP = NSA_HEADS // NSA_KV
NSA_DH = 64
NSA_WIDTH = NSA_HEADS * NSA_DH
NSA_KV_WIDTH = NSA_KV * NSA_DH
CMP_LEN = 32
CMP_STRIDE = 16
CMP_HIDDEN = 256
SEL_LEN = 64
SEL_TOP = 16
WINDOW = 512
Q_BLOCK = 128
FORCE_BONUS = 1e3
SEL_KT = 256
SEL_GROUP = 4
XA_HEADS = 4
XA_DH = 128
XA_WIDTH = XA_HEADS * XA_DH
N_BRANCH = 3

LANES = 128
SUBLANES = 8
VMEM_LIMIT = 56 * 1024 * 1024

TM_PROJ = 256
TM_TAIL = 256
FF_SLAB = 1024

_NT = (((1,), (1,)), ((), ()))
_TN = (((0,), (0,)), ((), ()))


def _dot(a, b):
    return jnp.dot(a, b, preferred_element_type=F32)


def _dot_nt(a, b):
    return lax.dot_general(a, b, _NT, preferred_element_type=F32)


def _resident(shape):
    nd = len(shape)
    return pl.BlockSpec(shape, lambda *_: (0,) * nd, pipeline_mode=pl.Buffered(1))


def _rms(x, g):
    return x * lax.rsqrt(jnp.mean(x * x, axis=-1, keepdims=True) + EPS) * g


_PROJ_COLS = (
    ("qk", 2 * ML_WIDTH, BF16),
    ("o", ML_WIDTH, BF16),
    ("mg", None, F32),
    ("v", ML_WIDTH, BF16),
    ("nq", NSA_HEADS * LANES, BF16),
    ("xq", XA_WIDTH, BF16),
    ("gcol", LANES, F32),
    ("kc", NSA_KV_WIDTH, F32),
    ("vc", NSA_KV_WIDTH, F32),
    ("ks", NSA_KV_WIDTH, BF16),
    ("kw", NSA_KV_WIDTH, BF16),
    ("vs0", LANES, BF16),
    ("vs1", LANES, BF16),
    ("vw0", LANES, BF16),
    ("vw1", LANES, BF16),
)
_PROJ_ROWS = (
    ("smallT", SUBLANES, F32),
)
_PROJ_CHUNK = 512


def _proj_layout(d_model):
    cols, off = [], 0
    for name, width, dt in _PROJ_COLS:
        width = N_BRANCH * d_model if width is None else width
        cols.append((name, off, width, dt))
        off += width
    rows, roff = [], 0
    for name, r, dt in _PROJ_ROWS:
        rows.append((name, roff, r, dt))
        roff += r
    return cols, off, rows, roff


def _inproj_kernel(cols, rows, tiles_per_b, x_ref, g_ref, w_ref, b_ref, wt_ref, bt_ref, conv_ref,
                   *refs):
    out_refs, xbuf = refs[:-1], refs[-1]
    tm = x_ref.shape[0]
    halo = SUBLANES

    @pl.when(pl.program_id(0) % tiles_per_b == 0)
    def _():
        xbuf[0:halo, :] = jnp.zeros((halo, xbuf.shape[1]), F32)

    u = _rms(x_ref[...], g_ref[...]).astype(BF16)
    col_refs = out_refs[:len(cols)]
    row_refs = out_refs[len(cols):]
    ncols = w_ref.shape[1]
    for c0 in range(0, ncols, _PROJ_CHUNK):
        cw = min(_PROJ_CHUNK, ncols - c0)
        chunk = _dot(u, w_ref[:, c0:c0 + cw]) + b_ref[:, c0:c0 + cw]
        for (name, off, width, dt), o_ref in zip(cols, col_refs):
            lo, hi = max(off, c0), min(off + width, c0 + cw)
            if lo >= hi:
                continue
            acc = chunk[:, lo - c0:hi - c0]
            cs = slice(lo - off, hi - off)
            if name == "qk":
                xbuf[halo:halo + tm, cs] = acc
                y = acc * conv_ref[0:1, cs]
                for j in range(1, ML_CONV):
                    y = y + xbuf[halo - j:halo - j + tm, cs] * conv_ref[j:j + 1, cs]
                xbuf[0:halo, cs] = xbuf[tm:tm + halo, cs]
                acc = y * jax.nn.sigmoid(y)
                if lo - off >= ML_WIDTH:
                    acc = acc * (ML_DH ** -0.5)
            elif name == "o":
                acc = jax.nn.sigmoid(acc)
            o_ref[:, cs] = acc.astype(dt)
    t = _dot_nt(wt_ref[...], u) + bt_ref[...]
    for (name, roff, r, dt), o_ref in zip(rows, row_refs):
        o_ref[0] = t[roff:roff + r, :].astype(dt)


def _split_w_in(w_in, b_in):
    widths = (ML_WIDTH, ML_WIDTH, ML_WIDTH, ML_WIDTH, ML_HEADS, ML_HEADS,
              NSA_WIDTH, NSA_KV_WIDTH, NSA_KV_WIDTH, NSA_KV_WIDTH, NSA_KV_WIDTH, NSA_KV_WIDTH,
              NSA_KV_WIDTH, 3 * NSA_HEADS, XA_WIDTH, w_in.shape[1])
    names = ("ml_q", "ml_k", "ml_v", "ml_o", "ml_i", "ml_f", "ns_q", "ns_kc", "ns_vc", "ns_ks",
             "ns_vs", "ns_kw", "ns_vw", "ns_g", "xa_q", "mg")
    out, off = {}, 0
    for n, wd in zip(names, widths):
        end = w_in.shape[1] if n == "mg" else off + wd
        out[n] = (w_in[:, off:end], b_in[off:end])
        off = end
    return out


def _inproj(x2, g_mix, w_in, b_in, ml_conv, seq):
    n_tok, d = x2.shape
    cols, ncols, rows, nrows = _proj_layout(d)
    p = _split_w_in(w_in, b_in)

    def pad_cols(w, b, width):
        return (jnp.pad(w, ((0, 0), (0, width - w.shape[1]))), jnp.pad(b, (0, width - b.shape[0])))

    wq, bq = p["ns_q"]
    wq = wq.reshape(d, NSA_HEADS, 1, NSA_DH)
    bq = bq.reshape(NSA_HEADS, 1, NSA_DH)
    in_group = jnp.asarray(np.arange(NSA_HEADS)[:, None] // NSA_REP == np.arange(NSA_KV)[None, :],
                           w_in.dtype)[:, :, None]
    wq_slots = wq * in_group
    bq_slots = bq * in_group
    gate_w = jnp.concatenate([p["ml_i"][0], p["ml_f"][0]], axis=1)
    gate_b = jnp.concatenate([p["ml_i"][1], p["ml_f"][1]])
    small_w = jnp.concatenate([gate_w, p["ns_g"][0]], axis=1)
    small_b = jnp.concatenate([gate_b, p["ns_g"][1]])

    def with_ones(wb, g):
        w, b = wb
        w = jnp.pad(w[:, g * NSA_DH:(g + 1) * NSA_DH], ((0, 0), (0, LANES - NSA_DH)))
        b = jnp.concatenate([b[g * NSA_DH:(g + 1) * NSA_DH], jnp.ones((LANES - NSA_DH,), b.dtype)])
        return w, b

    pieces = {
        "qk": (jnp.concatenate([p["ml_q"][0], p["ml_k"][0]], axis=1),
               jnp.concatenate([p["ml_q"][1], p["ml_k"][1]])),
        "o": p["ml_o"], "mg": p["mg"],
        "gcol": pad_cols(small_w, small_b, LANES),
        "kc": p["ns_kc"], "vc": p["ns_vc"], "v": p["ml_v"],
        "nq": (wq_slots.reshape(d, NSA_HEADS * LANES), bq_slots.reshape(NSA_HEADS * LANES)),
        "xq": p["xa_q"], "ks": p["ns_ks"], "kw": p["ns_kw"],
        "vs0": with_ones(p["ns_vs"], 0), "vs1": with_ones(p["ns_vs"], 1),
        "vw0": with_ones(p["ns_vw"], 0), "vw1": with_ones(p["ns_vw"], 1),
    }
    w_cols = jnp.concatenate([pieces[name][0] for name, *_ in cols], axis=1).astype(BF16)
    b_cols = jnp.concatenate([pieces[name][1] for name, *_ in cols])[None, :]
    w_rows = gate_w.T.astype(BF16)
    b_rows = gate_b[:, None]

    tm = TM_PROJ
    tiles_per_b = seq // tm
    out_shape = ([jax.ShapeDtypeStruct((n_tok, width), dt) for _, _, width, dt in cols]
                 + [jax.ShapeDtypeStruct((n_tok // seq, r, seq), dt) for _, _, r, dt in rows])
    out_specs = ([pl.BlockSpec((tm, width), lambda i: (i, 0)) for _, _, width, _ in cols]
                 + [pl.BlockSpec((1, r, tm), lambda i: (i // tiles_per_b, 0, i % tiles_per_b))
                    for _, _, r, _ in rows])
    outs = pl.pallas_call(
        functools.partial(_inproj_kernel, cols, rows, tiles_per_b),
        out_shape=out_shape,
        grid=(n_tok // tm,),
        in_specs=[pl.BlockSpec((tm, d), lambda i: (i, 0)),
                  _resident((1, d)), _resident((d, ncols)), _resident((1, ncols)),
                  _resident((nrows, d)), _resident((nrows, 1)),
                  _resident((ML_CONV, 2 * ML_WIDTH))],
        out_specs=out_specs,
        scratch_shapes=[pltpu.VMEM((tm + SUBLANES, 2 * ML_WIDTH), F32)],
        compiler_params=pltpu.CompilerParams(dimension_semantics=("arbitrary",),
                                             vmem_limit_bytes=VMEM_LIMIT),
        name="inproj",
    )(x2, g_mix[None, :], w_cols, b_cols, w_rows, b_rows, ml_conv)
    names = [c[0] for c in cols] + [r[0] for r in rows]
    return dict(zip(names, outs))


def _mlstm_kernel(qk_ref, v_ref, o_ref, gcol_ref, grow_ref, ng_ref, y_ref, c_scr, m_scr):
    L = ML_CHUNK
    nb = qk_ref.shape[0]

    @pl.when(pl.program_id(1) == 0)
    def _():
        c_scr[...] = jnp.zeros(c_scr.shape, F32)
        m_scr[...] = jnp.zeros(m_scr.shape, F32)

    r_i = lax.broadcasted_iota(jnp.int32, (L, L), 0)
    c_i = lax.broadcasted_iota(jnp.int32, (L, L), 1)
    causal = c_i <= r_i
    tril = causal.astype(F32)
    triu = (r_i <= c_i).astype(F32)
    gcols = [gcol_ref[b] for b in range(nb)]
    grows = [grow_ref[b] for b in range(nb)]
    b_cols = jnp.dot(tril, jax.nn.log_sigmoid(jnp.concatenate(gcols, axis=1)),
                     precision=lax.Precision.HIGHEST, preferred_element_type=F32)
    b_rows = jnp.dot(jax.nn.log_sigmoid(jnp.concatenate(grows, axis=0)), triu,
                     precision=lax.Precision.HIGHEST, preferred_element_type=F32)
    ones_col = jnp.where(lax.broadcasted_iota(jnp.int32, (L, ML_DH), 1) == 0, 1.0, 0.0).astype(BF16)

    chains = [(b, hh) for b in range(nb) for hh in range(ML_HEADS)]

    def operands(b, hh):
        sl = slice(hh * ML_DH, (hh + 1) * ML_DH)
        q = qk_ref[b, :, sl]
        k = qk_ref[b, :, ML_WIDTH + hh * ML_DH:ML_WIDTH + (hh + 1) * ML_DH]
        v1 = jnp.concatenate([v_ref[b, :, sl], ones_col], axis=1)
        return q, k, v1, c_scr[b, hh]

    ops = [operands(b, hh) for b, hh in chains]
    qk_t = [_dot_nt(q, k) for q, k, _, _ in ops]
    inter = [_dot(q, c_prev.astype(BF16)) for q, _, _, c_prev in ops]

    stab, ke_all, s_all = [], [], []
    for i, (b, hh) in enumerate(chains):
        gcol, grow = gcols[b], grows[b]
        li_c = gcol[:, hh:hh + 1]
        b_c = b_cols[:, b * LANES + ML_HEADS + hh:b * LANES + ML_HEADS + hh + 1]
        li_r = grow[hh:hh + 1, :]
        b_r = b_rows[b * SUBLANES + ML_HEADS + hh:b * SUBLANES + ML_HEADS + hh + 1, :]
        g = b_c[L - 1:L, :]
        m_prev = m_scr[b, hh:hh + 1, 0:1]
        log_d = jnp.where(causal, b_c - b_r + li_r, -jnp.inf)
        inter_log = b_c + m_prev
        m_t = jnp.maximum(inter_log, jnp.max(log_d, axis=-1, keepdims=True))
        s_all.append((qk_t[i] * jnp.exp(log_d - m_t)).astype(BF16))
        w_end = g - b_c + li_c
        m_loc = jnp.max(w_end, axis=0, keepdims=True)
        ke_all.append((ops[i][1] * jnp.exp(w_end - m_loc)).astype(BF16))
        m_new = jnp.maximum(g + m_prev, m_loc)
        stab.append((m_t, jnp.exp(inter_log - m_t), jnp.exp(g + m_prev - m_new),
                     jnp.exp(m_loc - m_new), m_new))

    intra = [_dot(s, o[2]) for s, o in zip(s_all, ops)]
    a_c = [lax.dot_general(ke, o[2], _TN, preferred_element_type=F32) for ke, o in zip(ke_all, ops)]

    ys, new_c, new_m = [], [], []
    for i, (b, hh) in enumerate(chains):
        sl = slice(hh * ML_DH, (hh + 1) * ML_DH)
        m_t, sc, a, bb, m_new = stab[i]
        nd = intra[i] + sc * inter[i]
        den = nd[:, ML_DH:ML_DH + 1]
        h = nd[:, 0:ML_DH] / jnp.maximum(jnp.abs(den), jnp.exp(-m_t))
        hn = h * lax.rsqrt(jnp.mean(h * h, axis=-1, keepdims=True) + EPS)
        ys.append((o_ref[b, :, sl] * hn * ng_ref[:, sl]).astype(y_ref.dtype))
        new_c.append(a * ops[i][3] + bb * a_c[i])
        new_m.append(jnp.broadcast_to(m_new, (1, m_scr.shape[2])))

    per_b = lambda xs, axis: [jnp.concatenate(xs[b * ML_HEADS:(b + 1) * ML_HEADS], axis=axis)
                              for b in range(nb)]
    y_ref[...] = jnp.stack(per_b(ys, 1))
    c_scr[...] = jnp.stack([jnp.stack(new_c[b * ML_HEADS:(b + 1) * ML_HEADS]) for b in range(nb)])
    m_scr[:, 0:ML_HEADS, :] = jnp.stack(per_b(new_m, 0))


def _mlstm(pr, ml_norm_g, batch, seq):
    L = ML_CHUNK
    nb = ML_BATCH_PER_STEP
    chunk = lambda b, c: (b, c, 0)
    per_tok = lambda name: pr[name].reshape(batch, seq, pr[name].shape[-1])
    y = pl.pallas_call(
        _mlstm_kernel,
        out_shape=jax.ShapeDtypeStruct((batch, seq, ML_WIDTH), BF16),
        grid=(batch // nb, seq // L),
        in_specs=[pl.BlockSpec((nb, L, 2 * ML_WIDTH), chunk),
                  pl.BlockSpec((nb, L, ML_WIDTH), chunk),
                  pl.BlockSpec((nb, L, ML_WIDTH), chunk),
                  pl.BlockSpec((nb, L, LANES), chunk),
                  pl.BlockSpec((nb, SUBLANES, L), lambda b, c: (b, 0, c)),
                  pl.BlockSpec((1, ML_WIDTH), lambda b, c: (0, 0))],
        out_specs=pl.BlockSpec((nb, L, ML_WIDTH), chunk),
        scratch_shapes=[pltpu.VMEM((nb, ML_HEADS, ML_DH, 2 * ML_DH), F32),
                        pltpu.VMEM((nb, SUBLANES, LANES), F32)],
        compiler_params=pltpu.CompilerParams(dimension_semantics=("parallel", "arbitrary"),
                                             vmem_limit_bytes=VMEM_LIMIT),
        name="mlstm",
    )(per_tok("qk"), per_tok("v"), per_tok("o"), per_tok("gcol"), pr["smallT"],
      ml_norm_g[None, :])
    return y.reshape(batch * seq, ML_WIDTH)


def _gelu_tanh(x):
    return 0.5 * x * (1.0 + jnp.tanh(np.sqrt(2.0 / np.pi) * (x + 0.044715 * (x * x * x))))


def _compress_kernel(k_ref, v_ref, pe_ref, w1_ref, w2k_ref, w2v_ref, kc_ref, vc0_ref, vc1_ref):
    nrow = kc_ref.shape[1]
    S = CMP_STRIDE

    def hidden(x_ref, which):
        lo = hi = None
        for j in range(0, S, 2):
            x = [x_ref[pl.ds(j + d, nrow, stride=S), :] for d in range(2)]
            for half in range(CMP_LEN // S):
                l0 = half * S + j
                xp = jnp.concatenate([(x[d] + pe_ref[which, l0 + d:l0 + d + 1, :]).astype(BF16)
                                      for d in range(2)], axis=1)
                t = _dot(xp, w1_ref[which, l0 // 2])
                if half == 0:
                    lo = t if lo is None else lo + t
                else:
                    hi = t if hi is None else hi + t
        return _gelu_tanh(lo + pltpu.roll(hi, nrow - 1, 0)).astype(BF16)

    kc_ref[0] = _dot(hidden(k_ref, 0), w2k_ref[...]).astype(kc_ref.dtype)
    hv = hidden(v_ref, 1)
    ones_half = lax.broadcasted_iota(jnp.int32, (nrow, LANES), 1) >= NSA_DH
    for g, o_ref in enumerate((vc0_ref, vc1_ref)):
        o_ref[0] = jnp.where(ones_half, 1.0, _dot(hv, w2v_ref[g])).astype(o_ref.dtype)


def _compress(pr, cmp_pe, cmp_w1, cmp_w2, batch, seq):
    nrow = seq // CMP_STRIDE
    hid = NSA_KV * CMP_HIDDEN
    w1 = cmp_w1.astype(BF16).reshape(2, CMP_LEN, NSA_DH, CMP_HIDDEN)
    z1 = jnp.zeros_like(w1)
    w1e = jnp.stack([jnp.concatenate([w1, z1], axis=-1), jnp.concatenate([z1, w1], axis=-1)], axis=2)
    w1e = w1e.reshape(2, CMP_LEN // 2, 2 * NSA_KV_WIDTH, hid)
    w2 = cmp_w2.astype(BF16)
    z2 = jnp.zeros_like(w2[0])
    w2k = jnp.concatenate([jnp.concatenate([w2[0], z2], axis=1),
                           jnp.concatenate([z2, w2[0]], axis=1)], axis=0)
    w2v = jnp.stack([jnp.pad(w2[1], ((g * CMP_HIDDEN, (NSA_KV - 1 - g) * CMP_HIDDEN),
                                     (0, LANES - NSA_DH))) for g in range(NSA_KV)])
    pe = jnp.concatenate([cmp_pe] * NSA_KV, axis=-1)
    tok_blk = pl.BlockSpec((seq, NSA_KV_WIDTH), lambda b: (b, 0))
    out_blk = pl.BlockSpec((1, nrow, LANES), lambda b: (b, 0, 0))
    return pl.pallas_call(
        _compress_kernel,
        out_shape=(jax.ShapeDtypeStruct((batch, nrow, LANES), BF16),) * 3,
        grid=(batch,),
        in_specs=[tok_blk, tok_blk,
                  _resident((2, CMP_LEN, NSA_KV_WIDTH)),
                  _resident((2, CMP_LEN // 2, 2 * NSA_KV_WIDTH, hid)),
                  _resident((hid, NSA_KV_WIDTH)),
                  _resident((NSA_KV, hid, LANES))],
        out_specs=(out_blk, out_blk, out_blk),
        compiler_params=pltpu.CompilerParams(dimension_semantics=("parallel",),
                                             vmem_limit_bytes=VMEM_LIMIT),
        name="compress",
    )(pr["kc"], pr["vc"], pe, w1e, w2k, w2v)


def _alibi_slope(h):
    return float(2.0 ** (-8.0 * (h + 1) / NSA_HEADS))


def _nsa_kernel_v2(nbs, q_ref, kc_ref, vct_ref, ks_ref, vst_ref, kw_ref, vwt_ref, gt_ref, ovt_ref,
                y_ref, selx, qs_scr, ab_scr, m_scr, l_scr, acc_scr):
    QB = Q_BLOCK
    KT = SEL_KT
    t0 = pl.program_id(1) * QB
    ncmp = kc_ref.shape[1]
    gates = jax.nn.sigmoid(gt_ref[...])
    q_lane = lax.broadcasted_iota(jnp.int32, (1, QB), 1)
    tpos = t0 + q_lane
    qs_scr[...] = q_ref[...] * 0.125

    @pl.when(pl.program_id(1) == 0)
    def _():
        rel = (lax.broadcasted_iota(jnp.int32, (KT, QB), 1)
               - lax.broadcasted_iota(jnp.int32, (KT, QB), 0)).astype(F32)
        for h in range(NSA_HEADS):
            ab_scr[h] = -_alibi_slope(h) * rel

    def head_q(h):
        return qs_scr[:, h * LANES:(h + 1) * LANES]

    def group_rows(x, g):
        return x[g * NSA_DH:(g + 1) * NSA_DH, :]

    n_sub = lax.broadcasted_iota(jnp.int32, (ncmp, QB), 0)
    dist_c = tpos - (n_sub * CMP_STRIDE + (CMP_LEN - 1))
    ok_c = dist_c >= 0
    dist_cf = dist_c.astype(F32)
    kc = kc_ref[0]
    vct = vct_ref[0]
    o_cmp = []
    for g in range(NSA_KV):
        psum = jnp.zeros((ncmp, QB), F32)
        for r in range(NSA_REP):
            h = g * NSA_REP + r
            s = _dot_nt(kc, head_q(h)) - _alibi_slope(h) * dist_cf
            s = jnp.where(ok_c, s, NEG)
            e = jnp.exp(s - jnp.max(s, axis=0, keepdims=True))
            p = jnp.where(ok_c, e / jnp.sum(e, axis=0, keepdims=True), 0.0)
            psum = psum + p
            o_cmp.append(group_rows(_dot(vct, p.astype(BF16)), g))
        imp = jnp.dot(ovt_ref[...], psum, precision=lax.Precision.HIGHEST,
                      preferred_element_type=F32)
        j_sub = lax.broadcasted_iota(jnp.int32, (nbs, QB), 0)
        cur = tpos // SEL_LEN
        forced = (j_sub == 0) | (j_sub == cur) | (j_sub == cur - 1)
        score = jnp.where(j_sub <= cur, imp + jnp.where(forced, FORCE_BONUS, 0.0), NEG)
        n_grp = nbs // SUBLANES
        parts = [score[a * SUBLANES:(a + 1) * SUBLANES, :] for a in range(n_grp)]
        ranks = [jnp.zeros((SUBLANES, QB), F32) for _ in range(n_grp)]
        sub8 = lax.broadcasted_iota(jnp.int32, (SUBLANES, QB), 0)
        for i in range(nbs):
            row = jnp.broadcast_to(score[i:i + 1, :], (SUBLANES, QB))
            for a in range(n_grp):
                if a * SUBLANES > i:
                    beats = jnp.where(row >= parts[a], 1.0, 0.0)
                elif (a + 1) * SUBLANES - 1 < i:
                    beats = jnp.where(row > parts[a], 1.0, 0.0)
                else:
                    beats = jnp.where(sub8 + a * SUBLANES > i,
                                      jnp.where(row >= parts[a], 1.0, 0.0),
                                      jnp.where(row > parts[a], 1.0, 0.0))
                ranks[a] = ranks[a] + beats
        for a in range(n_grp):
            sel_bias = jnp.where(ranks[a] < float(min(SEL_TOP, nbs)), 0.0, NEG)
            for jj in range(SUBLANES):
                j = a * SUBLANES + jj
                selx[g, j * SUBLANES:(j + 1) * SUBLANES, :] = jnp.broadcast_to(
                    sel_bias[jj:jj + 1, :], (SUBLANES, QB))

    blocks_per_tile = KT // SEL_LEN
    m_scr[...] = jnp.full(m_scr.shape, NEG, F32)
    l_scr[...] = jnp.zeros(l_scr.shape, F32)
    acc_scr[...] = jnp.zeros(acc_scr.shape, F32)
    last_tile = t0 // KT

    def sel_tile(kt, causal):
        k0 = pl.multiple_of(kt * KT, KT)
        ktile = ks_ref[pl.ds(k0, KT), :]
        row0 = pl.multiple_of(kt * (blocks_per_tile * SUBLANES), SUBLANES)
        if causal:
            k_sub = lax.broadcasted_iota(jnp.int32, (KT, QB), 0)
            visible = (q_lane - k_sub) >= (k0 - t0)
        for g in range(NSA_KV):
            sel = [selx[g, pl.ds(row0 + jj * SUBLANES, SUBLANES), :] for jj in range(blocks_per_tile)]
            vt = vst_ref[g * NSA_DH:(g + 1) * NSA_DH, pl.ds(k0, KT)]
            for r in range(NSA_REP):
                h = g * NSA_REP + r
                shift = _alibi_slope(h) * k0.astype(F32)
                bias = jnp.concatenate([jnp.tile(sb + shift, (SEL_LEN // SUBLANES, 1)) for sb in sel],
                                       axis=0)
                s = _dot_nt(ktile, head_q(h)) + ab_scr[h] + bias
                if causal:
                    s = jnp.where(visible, s, NEG)
                m_old = m_scr[h]
                m_new = jnp.maximum(m_old, jnp.max(s, axis=0, keepdims=True))
                alpha = jnp.exp(m_old - m_new)
                p = jnp.exp(s - m_new)
                l_scr[h] = alpha * l_scr[h] + jnp.sum(p, axis=0, keepdims=True)
                acc_scr[h] = alpha * acc_scr[h] + _dot(vt, p.astype(BF16))
                m_scr[h] = m_new

    def full_tile(kt, carry):
        sel_tile(kt, False)
        return carry

    lax.fori_loop(0, last_tile, full_tile, 0)
    sel_tile(last_tile, True)
    o_sel = [acc_scr[h] / l_scr[h] for h in range(NSA_HEADS)]

    WK = WINDOW + QB
    ws = pl.multiple_of(jnp.maximum(t0 - WINDOW, 0), QB)
    kw_sub = lax.broadcasted_iota(jnp.int32, (WK, QB), 0)
    dist_w = (t0 - ws) + q_lane - kw_sub
    ok_w = (dist_w >= 0) & (dist_w < WINDOW)
    dist_wf = dist_w.astype(F32)
    kwin = kw_ref[pl.ds(ws, WK), :]
    vwin = vwt_ref[:, pl.ds(ws, WK)]
    o_win = []
    for g in range(NSA_KV):
        for r in range(NSA_REP):
            h = g * NSA_REP + r
            s = _dot_nt(kwin, head_q(h)) - _alibi_slope(h) * dist_wf
            s = jnp.where(ok_w, s, NEG)
            p = jnp.exp(s - jnp.max(s, axis=0, keepdims=True))
            l = jnp.sum(p, axis=0, keepdims=True)
            o_win.append(group_rows(_dot(vwin, p.astype(BF16)), g) / l)

    outs = []
    for h in range(NSA_HEADS):
        g0 = SUBLANES + 3 * h
        outs.append(gates[g0:g0 + 1, :] * o_cmp[h] + gates[g0 + 1:g0 + 2, :] * o_sel[h]
                    + gates[g0 + 2:g0 + 3, :] * o_win[h])
    for pair in range(NSA_HEADS // 2):
        both = jnp.concatenate([outs[2 * pair], outs[2 * pair + 1]], axis=0)
        y_ref[:, pair * LANES:(pair + 1) * LANES] = both.T.astype(y_ref.dtype)


def _nsa_v2(pr, kc, vct, batch, seq):
    QB = Q_BLOCK
    nq = seq // QB
    nbs = seq // SEL_LEN
    ncmp = kc.shape[1]
    cs = np.arange(ncmp) * CMP_STRIDE
    js = np.arange(nbs) * SEL_LEN
    ovt = jnp.asarray(((cs[None, :] < js[:, None] + SEL_LEN)
                       & (cs[None, :] + CMP_LEN > js[:, None])).astype(np.float32))
    per_b = lambda b, i: (b, 0)
    return pl.pallas_call(
        functools.partial(_nsa_kernel, nbs),
        out_shape=jax.ShapeDtypeStruct((batch * seq, NSA_WIDTH), BF16),
        grid=(batch, nq),
        in_specs=[pl.BlockSpec((QB, NSA_HEADS * LANES), lambda b, i: (b * nq + i, 0)),
                  pl.BlockSpec((1, ncmp, NSA_KV_WIDTH), lambda b, i: (b, 0, 0)),
                  pl.BlockSpec((1, NSA_KV_WIDTH, ncmp), lambda b, i: (b, 0, 0)),
                  pl.BlockSpec((seq, NSA_KV_WIDTH), per_b),
                  pl.BlockSpec((NSA_KV_WIDTH, seq), lambda b, i: (0, b)),
                  pl.BlockSpec((seq, NSA_KV_WIDTH), per_b),
                  pl.BlockSpec((NSA_KV_WIDTH, seq), lambda b, i: (0, b)),
                  pl.BlockSpec((32, QB), lambda b, i: (0, b * nq + i)),
                  pl.BlockSpec((nbs, ncmp), lambda b, i: (0, 0))],
        out_specs=pl.BlockSpec((QB, NSA_WIDTH), lambda b, i: (b * nq + i, 0)),
        scratch_shapes=[pltpu.VMEM((NSA_KV, nbs * SUBLANES, QB), F32),
                        pltpu.VMEM((QB, NSA_HEADS * LANES), BF16),
                        pltpu.VMEM((NSA_HEADS, SEL_KT, QB), F32),
                        pltpu.VMEM((NSA_HEADS, 1, QB), F32),
                        pltpu.VMEM((NSA_HEADS, 1, QB), F32),
                        pltpu.VMEM((NSA_HEADS, NSA_DH, QB), F32)],
        compiler_params=pltpu.CompilerParams(dimension_semantics=("parallel", "arbitrary"),
                                             vmem_limit_bytes=VMEM_LIMIT),
        name="nsa",
    )(pr["nq"], kc, vct, pr["ks"], pr["vsT"], pr["kw"], pr["vwT"], pr["smallT"], ovt)


NSA_ROWS = NSA_HEADS * Q_BLOCK
WIN_KEYS = WINDOW + Q_BLOCK
F_SEL_BLOCK = 0
F_SEL_OFF = SEL_LEN
F_WIN_OFF = SEL_LEN + 1
F_WIN_BLK = SEL_LEN + 2
F_CMP = SEL_LEN + 3


def _tile_heads(x):
    return jnp.concatenate([x] * NSA_HEADS, axis=0)


def _nsa_kernel(nbs, q_ref, g_ref, kc_ref, vc0_ref, vc1_ref, ks_ref, vs0_ref, vs1_ref,
                kw_ref, vw0_ref, vw1_ref, ov_ref, y_ref,
                qe_scr, posf_scr, wposf_scr, cposf_scr, s_scr, mrun_scr, mb_scr, acc_scr):
    QB = Q_BLOCK
    KT = SEL_KT
    HALF = NSA_REP * QB
    t0 = pl.program_id(1) * QB
    ncmp = kc_ref.shape[1]
    lane = lax.broadcasted_iota(jnp.int32, (1, LANES), 1)
    lo_half = lane < NSA_DH

    @pl.when(pl.program_id(1) == 0)
    def _():
        seq = posf_scr.shape[0]
        k = lax.broadcasted_iota(jnp.int32, (seq, LANES), 0)
        c = lax.broadcasted_iota(jnp.int32, (seq, LANES), 1)
        posf = jnp.where(c < SEL_LEN, jnp.where(k // SEL_LEN == c, 1.0, 0.0),
                         jnp.where(c == F_SEL_OFF, (k % SEL_LEN).astype(F32), 0.0))
        posf_scr[...] = posf.astype(BF16)
        k = lax.broadcasted_iota(jnp.int32, (WIN_KEYS, LANES), 0)
        c = lax.broadcasted_iota(jnp.int32, (WIN_KEYS, LANES), 1)
        wposf = jnp.where(c == F_WIN_OFF, (k % SEL_LEN).astype(F32),
                          jnp.where(c == F_WIN_BLK, (k // SEL_LEN).astype(F32), 0.0))
        wposf_scr[...] = wposf.astype(BF16)
        k = lax.broadcasted_iota(jnp.int32, (ncmp, LANES), 0)
        c = lax.broadcasted_iota(jnp.int32, (ncmp, LANES), 1)
        cposf_scr[...] = jnp.where(c == F_CMP, k.astype(F32), 0.0).astype(BF16)

    def const_feat(h):
        s = _alibi_slope(h)
        return jnp.where((lane == F_SEL_OFF) | (lane == F_WIN_OFF), s,
                         jnp.where(lane == F_WIN_BLK, SEL_LEN * s,
                                   jnp.where(lane == F_CMP, CMP_STRIDE * s, 0.0)))

    for h in range(NSA_HEADS):
        rows = slice(h * QB, (h + 1) * QB)
        qe_scr[rows, 0:LANES] = q_ref[:, h * LANES:(h + 1) * LANES] * 0.125
        qe_scr[rows, LANES:2 * LANES] = jnp.broadcast_to(const_feat(h), (QB, LANES)).astype(BF16)

    def normalised(o, guard):
        outs = []
        for pair in range(NSA_HEADS // 2):
            e = o[(2 * pair) * QB:(2 * pair + 1) * QB]
            d = o[(2 * pair + 1) * QB:(2 * pair + 2) * QB]
            num = jnp.where(lo_half, e, pltpu.roll(d, NSA_DH, 1))
            den = jnp.where(lo_half, pltpu.roll(e, NSA_DH, 1), d)
            if guard:
                den = jnp.where(den > 0.0, den, 1.0)
            outs.append(num / den)
        return outs

    q_sub = lax.broadcasted_iota(jnp.int32, (QB, ncmp), 0)
    n_lane = lax.broadcasted_iota(jnp.int32, (QB, ncmp), 1)
    ok_c = n_lane * CMP_STRIDE + (CMP_LEN - 1) <= t0 + q_sub
    s = _dot_nt(qe_scr[...], jnp.concatenate([kc_ref[0], cposf_scr[...]], axis=1))
    s = s + _tile_heads(jnp.where(ok_c, 0.0, NEG))
    e = jnp.exp(s - jnp.max(s, axis=-1, keepdims=True)) * _tile_heads(jnp.where(ok_c, 1.0, 0.0))
    e_bf = e.astype(BF16)
    oc = jnp.concatenate([_dot(e_bf[0:HALF], vc0_ref[0]), _dot(e_bf[HALF:2 * HALF], vc1_ref[0])],
                         axis=0)
    o_cmp = normalised(oc, True)
    l_c = jnp.where(lo_half, pltpu.roll(oc, NSA_DH, 1), oc)
    inv_l = 1.0 / jnp.where(l_c > 0.0, l_c, 1.0)
    p_c = e * jnp.concatenate([inv_l] * (ncmp // LANES), axis=1)

    imp_ts = []
    for g in range(NSA_KV):
        psum = p_c[(g * NSA_REP) * QB:(g * NSA_REP + 1) * QB]
        for r in range(1, NSA_REP):
            psum = psum + p_c[(g * NSA_REP + r) * QB:(g * NSA_REP + r + 1) * QB]
        p_hi = psum.astype(BF16)
        p_lo = (psum - p_hi.astype(F32)).astype(BF16)
        imp = _dot(p_hi, ov_ref[...]) + _dot(p_lo, ov_ref[...])
        imp_ts.append(imp.T[0:nbs, :])

    ws = pl.multiple_of(jnp.maximum(t0 - WINDOW, 0), QB)
    qs = lax.broadcasted_iota(jnp.int32, (QB, WIN_KEYS), 0)
    kl = lax.broadcasted_iota(jnp.int32, (QB, WIN_KEYS), 1)
    dist_w = (t0 - ws) + qs - kl
    ok_w = (dist_w >= 0) & (dist_w < WINDOW)
    s_w = _dot_nt(qe_scr[...], jnp.concatenate([kw_ref[pl.ds(ws, WIN_KEYS), :], wposf_scr[...]], axis=1))

    n_grp = nbs // SUBLANES
    sub8 = lax.broadcasted_iota(jnp.int32, (SUBLANES, QB), 0)
    tpos = t0 + lax.broadcasted_iota(jnp.int32, (1, QB), 1)
    cur = tpos // SEL_LEN
    j_lane = lane.astype(F32)
    for g in range(NSA_KV):
        imp_t = imp_ts[g]
        j_sub = lax.broadcasted_iota(jnp.int32, (nbs, QB), 0)
        forced = (j_sub == 0) | (j_sub == cur) | (j_sub == cur - 1)
        score = jnp.where(j_sub <= cur, imp_t + jnp.where(forced, FORCE_BONUS, 0.0), NEG)
        parts = [score[a * SUBLANES:(a + 1) * SUBLANES, :] for a in range(n_grp)]
        ranks = [jnp.zeros((SUBLANES, QB), F32) for _ in range(n_grp)]
        for i in range(nbs):
            row = jnp.broadcast_to(score[i:i + 1, :], (SUBLANES, QB))
            for a in range(n_grp):
                if a * SUBLANES > i:
                    beats = jnp.where(row >= parts[a], 1.0, 0.0)
                elif (a + 1) * SUBLANES - 1 < i:
                    beats = jnp.where(row > parts[a], 1.0, 0.0)
                else:
                    beats = jnp.where(sub8 + a * SUBLANES > i,
                                      jnp.where(row >= parts[a], 1.0, 0.0),
                                      jnp.where(row > parts[a], 1.0, 0.0))
                ranks[a] = ranks[a] + beats
        sel_t = [jnp.where(rk < float(min(SEL_TOP, nbs)), 0.0, NEG) for rk in ranks]
        if nbs < LANES:
            sel_t.append(jnp.zeros((LANES - nbs, QB), F32))
        sel_bias = jnp.concatenate(sel_t, axis=0).T
        for r in range(NSA_REP):
            h = g * NSA_REP + r
            feat = jnp.where(lo_half, sel_bias + (_alibi_slope(h) * SEL_LEN) * j_lane, const_feat(h))
            qe_scr[h * QB:(h + 1) * QB, LANES:2 * LANES] = feat.astype(BF16)

    s_w = s_w + _tile_heads(jnp.where(ok_w, 0.0, NEG))
    p_w = jnp.exp(s_w - jnp.max(s_w, axis=-1, keepdims=True)).astype(BF16)
    o_win = normalised(jnp.concatenate([_dot(p_w[0:HALF], vw0_ref[pl.ds(ws, WIN_KEYS), :]),
                                        _dot(p_w[HALF:2 * HALF], vw1_ref[pl.ds(ws, WIN_KEYS), :])],
                                       axis=0), False)

    last_tile = t0 // KT
    mrun_scr[...] = jnp.full(mrun_scr.shape, NEG, F32)

    U = SEL_GROUP

    def score_group(kt0, count, causal):
        k0 = pl.multiple_of(kt0 * KT, KT)
        kx = jnp.concatenate([ks_ref[pl.ds(k0, count * KT), :],
                              posf_scr[pl.ds(k0, count * KT), :]], axis=1)
        s = _dot_nt(qe_scr[...], kx)
        tile_max = None
        for j in range(count):
            sj = s[:, j * KT:(j + 1) * KT]
            if causal and j == count - 1:
                qs = lax.broadcasted_iota(jnp.int32, (QB, KT), 0)
                kl = lax.broadcasted_iota(jnp.int32, (QB, KT), 1)
                sj = sj + _tile_heads(jnp.where(k0 + j * KT + kl <= t0 + qs, 0.0, NEG))
            s_scr[kt0 + j] = sj
            for c0 in range(0, KT, LANES):
                part = sj[:, c0:c0 + LANES]
                tile_max = part if tile_max is None else jnp.maximum(tile_max, part)
        mrun_scr[...] = jnp.maximum(mrun_scr[...], tile_max)

    def score_full_group(i, carry):
        score_group(i * U, U, False)
        return carry

    lax.fori_loop(0, last_tile // U, score_full_group, 0)
    for r in range(U):
        @pl.when(last_tile % U == r)
        def _(r=r):
            score_group(last_tile - r, r + 1, True)

    mb_scr[...] = jnp.broadcast_to(jnp.max(mrun_scr[...], axis=-1, keepdims=True), mb_scr.shape)
    acc_scr[...] = jnp.zeros(acc_scr.shape, F32)

    def value_group(kt0, count):
        k0 = pl.multiple_of(kt0 * KT, KT)
        mb = mb_scr[...]
        mbw = jnp.concatenate([mb] * (KT // LANES), axis=1)
        p = jnp.concatenate([jnp.exp(s_scr[kt0 + j] - mbw).astype(BF16) for j in range(count)],
                            axis=1)
        acc_scr[0:HALF, :] += _dot(p[0:HALF], vs0_ref[pl.ds(k0, count * KT), :])
        acc_scr[HALF:2 * HALF, :] += _dot(p[HALF:2 * HALF], vs1_ref[pl.ds(k0, count * KT), :])

    def value_full_group(i, carry):
        value_group(i * U, U)
        return carry

    n_tiles = last_tile + 1
    lax.fori_loop(0, n_tiles // U, value_full_group, 0)
    for r in range(1, U):
        @pl.when(n_tiles % U == r)
        def _(r=r):
            value_group(n_tiles - r, r)

    o_sel = normalised(acc_scr[...], False)

    gate = jax.nn.sigmoid(g_ref[...])
    for pair in range(NSA_HEADS // 2):
        y = None
        for b, o in enumerate((o_cmp, o_sel, o_win)):
            c_e = SUBLANES + 3 * (2 * pair) + b
            c_o = SUBLANES + 3 * (2 * pair + 1) + b
            term = jnp.where(lo_half, gate[:, c_e:c_e + 1], gate[:, c_o:c_o + 1]) * o[pair]
            y = term if y is None else y + term
        y_ref[:, pair * LANES:(pair + 1) * LANES] = y.astype(y_ref.dtype)


def _nsa(pr, kc, vc0, vc1, batch, seq):
    QB = Q_BLOCK
    nq = seq // QB
    nbs = seq // SEL_LEN
    ncmp = kc.shape[1]
    cs = np.arange(ncmp) * CMP_STRIDE
    js = np.arange(nbs) * SEL_LEN
    ov = np.zeros((ncmp, LANES), np.float32)
    ov[:, :nbs] = (cs[:, None] < js[None, :] + SEL_LEN) & (cs[:, None] + CMP_LEN > js[None, :])
    tok = lambda b, i: (b * nq + i, 0)
    per_b = lambda b, i: (b, 0)
    per_b3 = lambda b, i: (b, 0, 0)
    kv_spec = pl.BlockSpec((seq, LANES), per_b)
    cmp_spec = pl.BlockSpec((1, ncmp, LANES), per_b3)
    return pl.pallas_call(
        functools.partial(_nsa_kernel, nbs),
        out_shape=jax.ShapeDtypeStruct((batch * seq, NSA_WIDTH), BF16),
        grid=(batch, nq),
        in_specs=[pl.BlockSpec((QB, NSA_HEADS * LANES), tok),
                  pl.BlockSpec((QB, LANES), tok),
                  cmp_spec, cmp_spec, cmp_spec,
                  kv_spec, kv_spec, kv_spec, kv_spec, kv_spec, kv_spec,
                  pl.BlockSpec((ncmp, LANES), lambda b, i: (0, 0))],
        out_specs=pl.BlockSpec((QB, NSA_WIDTH), tok),
        scratch_shapes=[pltpu.VMEM((NSA_ROWS, 2 * LANES), BF16),
                        pltpu.VMEM((seq, LANES), BF16),
                        pltpu.VMEM((WIN_KEYS, LANES), BF16),
                        pltpu.VMEM((ncmp, LANES), BF16),
                        pltpu.VMEM((seq // SEL_KT, NSA_ROWS, SEL_KT), F32),
                        pltpu.VMEM((NSA_ROWS, LANES), F32),
                        pltpu.VMEM((NSA_ROWS, LANES), F32),
                        pltpu.VMEM((NSA_ROWS, LANES), F32)],
        compiler_params=pltpu.CompilerParams(dimension_semantics=("parallel", "arbitrary"),
                                             vmem_limit_bytes=VMEM_LIMIT),
        name="nsa",
    )(pr["nq"], pr["gcol"], kc, vc0, vc1, pr["ks"], pr["vs0"], pr["vs1"],
      pr["kw"], pr["vw0"], pr["vw1"], jnp.asarray(ov, BF16))


def _memkv_kernel(mem_ref, g_ref, w_ref, kv_ref):
    u = _rms(mem_ref[0], g_ref[...]).astype(BF16)
    kv_ref[0] = _dot(u, w_ref[...]).astype(kv_ref.dtype)


def _memkv(mem, g_mem, w_mem_kv):
    batch, n_mem, d = mem.shape
    return pl.pallas_call(
        _memkv_kernel,
        out_shape=jax.ShapeDtypeStruct((batch, n_mem, 2 * XA_WIDTH), BF16),
        grid=(batch,),
        in_specs=[pl.BlockSpec((1, n_mem, d), lambda b: (b, 0, 0)),
                  _resident((1, d)), _resident((d, 2 * XA_WIDTH))],
        out_specs=pl.BlockSpec((1, n_mem, 2 * XA_WIDTH), lambda b: (b, 0, 0)),
        compiler_params=pltpu.CompilerParams(dimension_semantics=("parallel",),
                                             vmem_limit_bytes=VMEM_LIMIT),
        name="memkv",
    )(mem, g_mem[None, :], w_mem_kv.astype(BF16))


def _tail_kernel(x_ref, yml_ref, ynsa_ref, xq_ref, kv_ref, mg_ref, wb_ref, wo_ref, gf_ref,
                 w1_ref, w2_ref, gl_ref, out_ref):
    d = x_ref.shape[1]
    kv = kv_ref[0]
    heads = [slice(hh * XA_DH, (hh + 1) * XA_DH) for hh in range(XA_HEADS)]
    scores = [_dot_nt(xq_ref[:, sl], kv[:, sl]) * (XA_DH ** -0.5) for sl in heads]
    ups = [_dot(yml_ref[...], wb_ref[0]), _dot(ynsa_ref[...], wb_ref[1])]
    y_xa = []
    for hh, s in enumerate(scores):
        p = jnp.exp(s - jnp.max(s, axis=-1, keepdims=True))
        l = jnp.sum(p, axis=-1, keepdims=True)
        y_xa.append((_dot(p.astype(BF16), kv[:, XA_WIDTH + hh * XA_DH:XA_WIDTH + (hh + 1) * XA_DH])
                     / l).astype(BF16))
    ups.append(_dot(jnp.concatenate(y_xa, axis=-1), wb_ref[2]))
    merged = None
    for j in range(N_BRANCH):
        term = jax.nn.sigmoid(mg_ref[:, j * d:(j + 1) * d]) * ups[j]
        merged = term if merged is None else merged + term
    h = x_ref[...] + _dot(merged.astype(BF16), wo_ref[...])
    u = _rms(h, gf_ref[...]).astype(BF16)
    acc = h
    for c0 in range(0, w1_ref.shape[1], FF_SLAB):
        a = jnp.maximum(_dot(u, w1_ref[:, c0:c0 + FF_SLAB]), 0.0)
        acc = acc + _dot((a * a).astype(BF16), w2_ref[c0:c0 + FF_SLAB, :])
    out_ref[...] = _rms(acc, gl_ref[...])


def _tail(x2, y_ml, y_nsa, pr, mem_kv, w_branch, w_out, g_ffn, w_ff1, w_ff2, g_final, seq):
    n_tok, d = x2.shape
    tm = TM_TAIL
    d_ff = w_ff1.shape[1]
    n_mem = mem_kv.shape[1]
    tiles_per_b = seq // tm
    row = lambda i: (i, 0)
    return pl.pallas_call(
        _tail_kernel,
        out_shape=jax.ShapeDtypeStruct((n_tok, d), F32),
        grid=(n_tok // tm,),
        in_specs=[pl.BlockSpec((tm, d), row),
                  pl.BlockSpec((tm, ML_WIDTH), row),
                  pl.BlockSpec((tm, NSA_WIDTH), row),
                  pl.BlockSpec((tm, XA_WIDTH), row),
                  pl.BlockSpec((1, n_mem, 2 * XA_WIDTH), lambda i: (i // tiles_per_b, 0, 0)),
                  pl.BlockSpec((tm, N_BRANCH * d), row),
                  _resident((N_BRANCH, ML_WIDTH, d)), _resident((d, d)), _resident((1, d)),
                  _resident((d, d_ff)), _resident((d_ff, d)), _resident((1, d))],
        out_specs=pl.BlockSpec((tm, d), row),
        compiler_params=pltpu.CompilerParams(dimension_semantics=("parallel",),
                                             vmem_limit_bytes=VMEM_LIMIT),
        name="tail",
    )(x2, y_ml, y_nsa, pr["xq"], mem_kv, pr["mg"], w_branch.astype(BF16), w_out.astype(BF16),
      g_ffn[None, :], w_ff1.astype(BF16), w_ff2.astype(BF16), g_final[None, :])


def _layer(x, mem, g_mix, w_in, b_in, ml_conv, ml_norm_g, cmp_pe, cmp_w1, cmp_w2, g_mem, w_mem_kv,
           w_branch, w_out, g_ffn, w_ff1, w_ff2, g_final):
    batch, seq, d = x.shape
    x2 = x.reshape(batch * seq, d)
    pr = _inproj(x2, g_mix, w_in, b_in, ml_conv, seq)
    y_ml = _mlstm(pr, ml_norm_g, batch, seq)
    kc, vc0, vc1 = _compress(pr, cmp_pe, cmp_w1, cmp_w2, batch, seq)
    y_nsa = _nsa(pr, kc, vc0, vc1, batch, seq)
    mem_kv = _memkv(mem, g_mem, w_mem_kv)
    out = _tail(x2, y_ml, y_nsa, pr, mem_kv, w_branch, w_out, g_ffn, w_ff1, w_ff2, g_final, seq)
    return out.reshape(batch, seq, d)


def kernel(x, mem, g_mix, w_in, b_in, ml_conv, ml_norm_g, cmp_pe, cmp_w1, cmp_w2, g_mem, w_mem_kv,
           w_branch, w_out, g_ffn, w_ff1, w_ff2, g_final):
    assert g_mix.shape[0] == 1, "single-layer block"
    return _layer(x, mem, g_mix[0], w_in[0], b_in[0], ml_conv[0], ml_norm_g[0], cmp_pe[0],
                  cmp_w1[0], cmp_w2[0], g_mem[0], w_mem_kv[0], w_branch[0], w_out[0], g_ffn[0],
                  w_ff1[0], w_ff2[0], g_final)
```

```python
import functools

import numpy as np
import jax
import jax.numpy as jnp
from jax import lax
from jax.experimental import pallas as pl
from jax.experimental.pallas import tpu as pltpu

F32 = jnp.float32
BF16 = jnp.bfloat16

EPS = 1e-6
NEG = -1e30
ML_HEADS = 4
ML_DH = 128
ML_WIDTH = ML_HEADS * ML_DH
ML_CONV = 4
ML_CHUNK = 128
ML_BATCH_PER_STEP = 1
NSA_HEADS = 8
NSA_KV = 2
NSA_REP = NSA_HEADS // NSA_KV
NSA_DH = 64
NSA_WIDTH = NSA_HEADS * NSA_DH
NSA_KV_WIDTH = NSA_KV * NSA_DH
CMP_LEN = 32
CMP_STRIDE = 16
CMP_HIDDEN = 256
SEL_LEN = 64
SEL_TOP = 16
WINDOW = 512
Q_BLOCK = 128
FORCE_BONUS = 1e3
SEL_KT = 256
SEL_GROUP = 8
XA_HEADS = 4
XA_DH = 128
XA_WIDTH = XA_HEADS * XA_DH
N_BRANCH = 3

LANES = 128
SUBLANES = 8
VMEM_LIMIT = 56 * 1024 * 1024

TM_PROJ = 256
TM_TAIL = 256
FF_SLAB = 1024

_NT = (((1,), (1,)), ((), ()))
_TN = (((0,), (0,)), ((), ()))


def _dot(a, b):
    return jnp.dot(a, b, preferred_element_type=F32)


def _dot_nt(a, b):
    return lax.dot_general(a, b, _NT, preferred_element_type=F32)


def _resident(shape):
    nd = len(shape)
    return pl.BlockSpec(shape, lambda *_: (0,) * nd, pipeline_mode=pl.Buffered(1))


def _rms(x, g):
    return x * lax.rsqrt(jnp.mean(x * x, axis=-1, keepdims=True) + EPS) * g


_PROJ_COLS = (
    ("qk", 2 * ML_WIDTH, BF16),
    ("o", ML_WIDTH, BF16),
    ("mg", None, F32),
    ("v", ML_WIDTH, BF16),
    ("nq", NSA_HEADS * LANES, BF16),
    ("xq", XA_WIDTH, BF16),
    ("gcol", LANES, F32),
    ("kc", NSA_KV_WIDTH, F32),
    ("vc", NSA_KV_WIDTH, F32),
    ("ks", NSA_KV_WIDTH, BF16),
    ("kw", NSA_KV_WIDTH, BF16),
    ("vs0", LANES, BF16),
    ("vs1", LANES, BF16),
    ("vw0", LANES, BF16),
    ("vw1", LANES, BF16),
)
_PROJ_ROWS = (
    ("smallT", SUBLANES, F32),
)
_PROJ_CHUNK = 512


def _proj_layout(d_model):
    cols, off = [], 0
    for name, width, dt in _PROJ_COLS:
        width = N_BRANCH * d_model if width is None else width
        cols.append((name, off, width, dt))
        off += width
    rows, roff = [], 0
    for name, r, dt in _PROJ_ROWS:
        rows.append((name, roff, r, dt))
        roff += r
    return cols, off, rows, roff


def _inproj_kernel(cols, rows, tiles_per_b, x_ref, g_ref, w_ref, b_ref, wt_ref, bt_ref, conv_ref,
                   *refs):
    out_refs, xbuf = refs[:-1], refs[-1]
    tm = x_ref.shape[0]
    halo = SUBLANES

    @pl.when(pl.program_id(0) % tiles_per_b == 0)
    def _():
        xbuf[0:halo, :] = jnp.zeros((halo, xbuf.shape[1]), F32)

    u = _rms(x_ref[...], g_ref[...]).astype(BF16)
    col_refs = out_refs[:len(cols)]
    row_refs = out_refs[len(cols):]
    ncols = w_ref.shape[1]
    for c0 in range(0, ncols, _PROJ_CHUNK):
        cw = min(_PROJ_CHUNK, ncols - c0)
        chunk = _dot(u, w_ref[:, c0:c0 + cw]) + b_ref[:, c0:c0 + cw]
        for (name, off, width, dt), o_ref in zip(cols, col_refs):
            lo, hi = max(off, c0), min(off + width, c0 + cw)
            if lo >= hi:
                continue
            acc = chunk[:, lo - c0:hi - c0]
            cs = slice(lo - off, hi - off)
            if name == "qk":
                xbuf[halo:halo + tm, cs] = acc
                y = acc * conv_ref[0:1, cs]
                for j in range(1, ML_CONV):
                    y = y + xbuf[halo - j:halo - j + tm, cs] * conv_ref[j:j + 1, cs]
                xbuf[0:halo, cs] = xbuf[tm:tm + halo, cs]
                acc = y * jax.nn.sigmoid(y)
                if lo - off >= ML_WIDTH:
                    acc = acc * (ML_DH ** -0.5)
            elif name == "o":
                acc = jax.nn.sigmoid(acc)
            o_ref[:, cs] = acc.astype(dt)
    t = _dot_nt(wt_ref[...], u) + bt_ref[...]
    for (name, roff, r, dt), o_ref in zip(rows, row_refs):
        o_ref[0] = t[roff:roff + r, :].astype(dt)


def _split_w_in(w_in, b_in):
    widths = (ML_WIDTH, ML_WIDTH, ML_WIDTH, ML_WIDTH, ML_HEADS, ML_HEADS,
              NSA_WIDTH, NSA_KV_WIDTH, NSA_KV_WIDTH, NSA_KV_WIDTH, NSA_KV_WIDTH, NSA_KV_WIDTH,
              NSA_KV_WIDTH, 3 * NSA_HEADS, XA_WIDTH, w_in.shape[1])
    names = ("ml_q", "ml_k", "ml_v", "ml_o", "ml_i", "ml_f", "ns_q", "ns_kc", "ns_vc", "ns_ks",
             "ns_vs", "ns_kw", "ns_vw", "ns_g", "xa_q", "mg")
    out, off = {}, 0
    for n, wd in zip(names, widths):
        end = w_in.shape[1] if n == "mg" else off + wd
        out[n] = (w_in[:, off:end], b_in[off:end])
        off = end
    return out


def _inproj(x2, g_mix, w_in, b_in, ml_conv, seq):
    n_tok, d = x2.shape
    cols, ncols, rows, nrows = _proj_layout(d)
    p = _split_w_in(w_in, b_in)

    def pad_cols(w, b, width):
        return (jnp.pad(w, ((0, 0), (0, width - w.shape[1]))), jnp.pad(b, (0, width - b.shape[0])))

    wq, bq = p["ns_q"]
    wq = wq.reshape(d, NSA_HEADS, 1, NSA_DH)
    bq = bq.reshape(NSA_HEADS, 1, NSA_DH)
    in_group = jnp.asarray(np.arange(NSA_HEADS)[:, None] // NSA_REP == np.arange(NSA_KV)[None, :],
                           w_in.dtype)[:, :, None]
    wq_slots = wq * in_group
    bq_slots = bq * in_group
    gate_w = jnp.concatenate([p["ml_i"][0], p["ml_f"][0]], axis=1)
    gate_b = jnp.concatenate([p["ml_i"][1], p["ml_f"][1]])
    small_w = jnp.concatenate([gate_w, p["ns_g"][0]], axis=1)
    small_b = jnp.concatenate([gate_b, p["ns_g"][1]])

    def with_ones(wb, g):
        w, b = wb
        w = jnp.pad(w[:, g * NSA_DH:(g + 1) * NSA_DH], ((0, 0), (0, LANES - NSA_DH)))
        b = jnp.concatenate([b[g * NSA_DH:(g + 1) * NSA_DH], jnp.ones((LANES - NSA_DH,), b.dtype)])
        return w, b

    pieces = {
        "qk": (jnp.concatenate([p["ml_q"][0], p["ml_k"][0]], axis=1),
               jnp.concatenate([p["ml_q"][1], p["ml_k"][1]])),
        "o": p["ml_o"], "mg": p["mg"],
        "gcol": pad_cols(small_w, small_b, LANES),
        "kc": p["ns_kc"], "vc": p["ns_vc"], "v": p["ml_v"],
        "nq": (wq_slots.reshape(d, NSA_HEADS * LANES), bq_slots.reshape(NSA_HEADS * LANES)),
        "xq": p["xa_q"], "ks": p["ns_ks"], "kw": p["ns_kw"],
        "vs0": with_ones(p["ns_vs"], 0), "vs1": with_ones(p["ns_vs"], 1),
        "vw0": with_ones(p["ns_vw"], 0), "vw1": with_ones(p["ns_vw"], 1),
    }
    w_cols = jnp.concatenate([pieces[name][0] for name, *_ in cols], axis=1).astype(BF16)
    b_cols = jnp.concatenate([pieces[name][1] for name, *_ in cols])[None, :]
    w_rows = gate_w.T.astype(BF16)
    b_rows = gate_b[:, None]

    tm = TM_PROJ
    tiles_per_b = seq // tm
    out_shape = ([jax.ShapeDtypeStruct((n_tok, width), dt) for _, _, width, dt in cols]
                 + [jax.ShapeDtypeStruct((n_tok // seq, r, seq), dt) for _, _, r, dt in rows])
    out_specs = ([pl.BlockSpec((tm, width), lambda i: (i, 0)) for _, _, width, _ in cols]
                 + [pl.BlockSpec((1, r, tm), lambda i: (i // tiles_per_b, 0, i % tiles_per_b))
                    for _, _, r, _ in rows])
    outs = pl.pallas_call(
        functools.partial(_inproj_kernel, cols, rows, tiles_per_b),
        out_shape=out_shape,
        grid=(n_tok // tm,),
        in_specs=[pl.BlockSpec((tm, d), lambda i: (i, 0)),
                  _resident((1, d)), _resident((d, ncols)), _resident((1, ncols)),
                  _resident((nrows, d)), _resident((nrows, 1)),
                  _resident((ML_CONV, 2 * ML_WIDTH))],
        out_specs=out_specs,
        scratch_shapes=[pltpu.VMEM((tm + SUBLANES, 2 * ML_WIDTH), F32)],
        compiler_params=pltpu.CompilerParams(dimension_semantics=("arbitrary",),
                                             vmem_limit_bytes=VMEM_LIMIT),
        name="inproj",
    )(x2, g_mix[None, :], w_cols, b_cols, w_rows, b_rows, ml_conv)
    names = [c[0] for c in cols] + [r[0] for r in rows]
    return dict(zip(names, outs))


def _mlstm_kernel(qk_ref, v_ref, o_ref, gcol_ref, grow_ref, ng_ref, y_ref, c_scr, m_scr):
    L = ML_CHUNK
    nb = qk_ref.shape[0]

    @pl.when(pl.program_id(1) == 0)
    def _():
        c_scr[...] = jnp.zeros(c_scr.shape, F32)
        m_scr[...] = jnp.zeros(m_scr.shape, F32)

    r_i = lax.broadcasted_iota(jnp.int32, (L, L), 0)
    c_i = lax.broadcasted_iota(jnp.int32, (L, L), 1)
    causal = c_i <= r_i
    tril = causal.astype(F32)
    triu = (r_i <= c_i).astype(F32)
    gcols = [gcol_ref[b] for b in range(nb)]
    grows = [grow_ref[b] for b in range(nb)]
    b_cols = jnp.dot(tril, jax.nn.log_sigmoid(jnp.concatenate(gcols, axis=1)),
                     precision=lax.Precision.HIGHEST, preferred_element_type=F32)
    b_rows = jnp.dot(jax.nn.log_sigmoid(jnp.concatenate(grows, axis=0)), triu,
                     precision=lax.Precision.HIGHEST, preferred_element_type=F32)
    ones_col = jnp.where(lax.broadcasted_iota(jnp.int32, (L, ML_DH), 1) == 0, 1.0, 0.0).astype(BF16)

    chains = [(b, hh) for b in range(nb) for hh in range(ML_HEADS)]

    def operands(b, hh):
        sl = slice(hh * ML_DH, (hh + 1) * ML_DH)
        q = qk_ref[b, :, sl]
        k = qk_ref[b, :, ML_WIDTH + hh * ML_DH:ML_WIDTH + (hh + 1) * ML_DH]
        v1 = jnp.concatenate([v_ref[b, :, sl], ones_col], axis=1)
        return q, k, v1, c_scr[b, hh]

    ops = [operands(b, hh) for b, hh in chains]
    qk_t = [_dot_nt(q, k) for q, k, _, _ in ops]
    inter = [_dot(q, c_prev.astype(BF16)) for q, _, _, c_prev in ops]

    stab, ke_all, s_all = [], [], []
    for i, (b, hh) in enumerate(chains):
        gcol, grow = gcols[b], grows[b]
        li_c = gcol[:, hh:hh + 1]
        b_c = b_cols[:, b * LANES + ML_HEADS + hh:b * LANES + ML_HEADS + hh + 1]
        li_r = grow[hh:hh + 1, :]
        b_r = b_rows[b * SUBLANES + ML_HEADS + hh:b * SUBLANES + ML_HEADS + hh + 1, :]
        g = b_c[L - 1:L, :]
        m_prev = m_scr[b, hh:hh + 1, 0:1]
        log_d = jnp.where(causal, b_c - b_r + li_r, -jnp.inf)
        inter_log = b_c + m_prev
        m_t = jnp.maximum(inter_log, jnp.max(log_d, axis=-1, keepdims=True))
        s_all.append((qk_t[i] * jnp.exp(log_d - m_t)).astype(BF16))
        w_end = g - b_c + li_c
        m_loc = jnp.max(w_end, axis=0, keepdims=True)
        ke_all.append((ops[i][1] * jnp.exp(w_end - m_loc)).astype(BF16))
        m_new = jnp.maximum(g + m_prev, m_loc)
        stab.append((m_t, jnp.exp(inter_log - m_t), jnp.exp(g + m_prev - m_new),
                     jnp.exp(m_loc - m_new), m_new))

    intra = [_dot(s, o[2]) for s, o in zip(s_all, ops)]
    a_c = [lax.dot_general(ke, o[2], _TN, preferred_element_type=F32) for ke, o in zip(ke_all, ops)]

    ys, new_c, new_m = [], [], []
    for i, (b, hh) in enumerate(chains):
        sl = slice(hh * ML_DH, (hh + 1) * ML_DH)
        m_t, sc, a, bb, m_new = stab[i]
        nd = intra[i] + sc * inter[i]
        den = nd[:, ML_DH:ML_DH + 1]
        h = nd[:, 0:ML_DH] / jnp.maximum(jnp.abs(den), jnp.exp(-m_t))
        hn = h * lax.rsqrt(jnp.mean(h * h, axis=-1, keepdims=True) + EPS)
        ys.append((o_ref[b, :, sl] * hn * ng_ref[:, sl]).astype(y_ref.dtype))
        new_c.append(a * ops[i][3] + bb * a_c[i])
        new_m.append(jnp.broadcast_to(m_new, (1, m_scr.shape[2])))

    per_b = lambda xs, axis: [jnp.concatenate(xs[b * ML_HEADS:(b + 1) * ML_HEADS], axis=axis)
                              for b in range(nb)]
    y_ref[...] = jnp.stack(per_b(ys, 1))
    c_scr[...] = jnp.stack([jnp.stack(new_c[b * ML_HEADS:(b + 1) * ML_HEADS]) for b in range(nb)])
    m_scr[:, 0:ML_HEADS, :] = jnp.stack(per_b(new_m, 0))


def _mlstm(pr, ml_norm_g, batch, seq):
    L = ML_CHUNK
    nb = ML_BATCH_PER_STEP
    chunk = lambda b, c: (b, c, 0)
    per_tok = lambda name: pr[name].reshape(batch, seq, pr[name].shape[-1])
    y = pl.pallas_call(
        _mlstm_kernel,
        out_shape=jax.ShapeDtypeStruct((batch, seq, ML_WIDTH), BF16),
        grid=(batch // nb, seq // L),
        in_specs=[pl.BlockSpec((nb, L, 2 * ML_WIDTH), chunk),
                  pl.BlockSpec((nb, L, ML_WIDTH), chunk),
                  pl.BlockSpec((nb, L, ML_WIDTH), chunk),
                  pl.BlockSpec((nb, L, LANES), chunk),
                  pl.BlockSpec((nb, SUBLANES, L), lambda b, c: (b, 0, c)),
                  pl.BlockSpec((1, ML_WIDTH), lambda b, c: (0, 0))],
        out_specs=pl.BlockSpec((nb, L, ML_WIDTH), chunk),
        scratch_shapes=[pltpu.VMEM((nb, ML_HEADS, ML_DH, 2 * ML_DH), F32),
                        pltpu.VMEM((nb, SUBLANES, LANES), F32)],
        compiler_params=pltpu.CompilerParams(dimension_semantics=("parallel", "arbitrary"),
                                             vmem_limit_bytes=VMEM_LIMIT),
        name="mlstm",
    )(per_tok("qk"), per_tok("v"), per_tok("o"), per_tok("gcol"), pr["smallT"],
      ml_norm_g[None, :])
    return y.reshape(batch * seq, ML_WIDTH)


def _gelu_tanh(x):
    return 0.5 * x * (1.0 + jnp.tanh(np.sqrt(2.0 / np.pi) * (x + 0.044715 * (x * x * x))))


def _compress_kernel(k_ref, v_ref, pe_ref, w1_ref, w2k_ref, w2v_ref, kc_ref, vc0_ref, vc1_ref):
    nrow = kc_ref.shape[1]
    S = CMP_STRIDE

    def hidden(x_ref, which):
        lo = hi = None
        for j in range(0, S, 2):
            x = [x_ref[pl.ds(j + d, nrow, stride=S), :] for d in range(2)]
            for half in range(CMP_LEN // S):
                l0 = half * S + j
                xp = jnp.concatenate([(x[d] + pe_ref[which, l0 + d:l0 + d + 1, :]).astype(BF16)
                                      for d in range(2)], axis=1)
                t = _dot(xp, w1_ref[which, l0 // 2])
                if half == 0:
                    lo = t if lo is None else lo + t
                else:
                    hi = t if hi is None else hi + t
        return _gelu_tanh(lo + pltpu.roll(hi, nrow - 1, 0)).astype(BF16)

    kc_ref[0] = _dot(hidden(k_ref, 0), w2k_ref[...]).astype(kc_ref.dtype)
    hv = hidden(v_ref, 1)
    ones_half = lax.broadcasted_iota(jnp.int32, (nrow, LANES), 1) >= NSA_DH
    for g, o_ref in enumerate((vc0_ref, vc1_ref)):
        o_ref[0] = jnp.where(ones_half, 1.0, _dot(hv, w2v_ref[g])).astype(o_ref.dtype)


def _compress(pr, cmp_pe, cmp_w1, cmp_w2, batch, seq):
    nrow = seq // CMP_STRIDE
    hid = NSA_KV * CMP_HIDDEN
    w1 = cmp_w1.astype(BF16).reshape(2, CMP_LEN, NSA_DH, CMP_HIDDEN)
    z1 = jnp.zeros_like(w1)
    w1e = jnp.stack([jnp.concatenate([w1, z1], axis=-1), jnp.concatenate([z1, w1], axis=-1)], axis=2)
    w1e = w1e.reshape(2, CMP_LEN // 2, 2 * NSA_KV_WIDTH, hid)
    w2 = cmp_w2.astype(BF16)
    z2 = jnp.zeros_like(w2[0])
    w2k = jnp.concatenate([jnp.concatenate([w2[0], z2], axis=1),
                           jnp.concatenate([z2, w2[0]], axis=1)], axis=0)
    w2v = jnp.stack([jnp.pad(w2[1], ((g * CMP_HIDDEN, (NSA_KV - 1 - g) * CMP_HIDDEN),
                                     (0, LANES - NSA_DH))) for g in range(NSA_KV)])
    pe = jnp.concatenate([cmp_pe] * NSA_KV, axis=-1)
    tok_blk = pl.BlockSpec((seq, NSA_KV_WIDTH), lambda b: (b, 0))
    out_blk = pl.BlockSpec((1, nrow, LANES), lambda b: (b, 0, 0))
    return pl.pallas_call(
        _compress_kernel,
        out_shape=(jax.ShapeDtypeStruct((batch, nrow, LANES), BF16),) * 3,
        grid=(batch,),
        in_specs=[tok_blk, tok_blk,
                  _resident((2, CMP_LEN, NSA_KV_WIDTH)),
                  _resident((2, CMP_LEN // 2, 2 * NSA_KV_WIDTH, hid)),
                  _resident((hid, NSA_KV_WIDTH)),
                  _resident((NSA_KV, hid, LANES))],
        out_specs=(out_blk, out_blk, out_blk),
        compiler_params=pltpu.CompilerParams(dimension_semantics=("parallel",),
                                             vmem_limit_bytes=VMEM_LIMIT),
        name="compress",
    )(pr["kc"], pr["vc"], pe, w1e, w2k, w2v)


def _alibi_slope(h):
    return float(2.0 ** (-8.0 * (h + 1) / NSA_HEADS))


NSA_ROWS = NSA_HEADS * Q_BLOCK
WIN_KEYS = WINDOW + Q_BLOCK
F_SEL_BLOCK = 0
F_SEL_OFF = SEL_LEN
F_WIN_OFF = SEL_LEN + 1
F_WIN_BLK = SEL_LEN + 2
F_CMP = SEL_LEN + 3


def _tile_heads(x):
    return jnp.concatenate([x] * NSA_HEADS, axis=0)


def _nsa_kernel(nbs, q_ref, g_ref, kc_ref, vc0_ref, vc1_ref, ks_ref, vs0_ref, vs1_ref,
                kw_ref, vw0_ref, vw1_ref, ov_ref, y_ref,
                qe_scr, posf_scr, wposf_scr, cposf_scr, s_scr, mrun_scr, mb_scr, acc_scr):
    QB = Q_BLOCK
    KT = SEL_KT
    HALF = NSA_REP * QB
    t0 = pl.program_id(1) * QB
    ncmp = kc_ref.shape[1]
    lane = lax.broadcasted_iota(jnp.int32, (1, LANES), 1)
    lo_half = lane < NSA_DH

    @pl.when(pl.program_id(1) == 0)
    def _():
        seq = posf_scr.shape[0]
        k = lax.broadcasted_iota(jnp.int32, (seq, LANES), 0)
        c = lax.broadcasted_iota(jnp.int32, (seq, LANES), 1)
        posf = jnp.where(c < SEL_LEN, jnp.where(k // SEL_LEN == c, 1.0, 0.0),
                         jnp.where(c == F_SEL_OFF, (k % SEL_LEN).astype(F32), 0.0))
        posf_scr[...] = posf.astype(BF16)
        k = lax.broadcasted_iota(jnp.int32, (WIN_KEYS, LANES), 0)
        c = lax.broadcasted_iota(jnp.int32, (WIN_KEYS, LANES), 1)
        wposf = jnp.where(c == F_WIN_OFF, (k % SEL_LEN).astype(F32),
                          jnp.where(c == F_WIN_BLK, (k // SEL_LEN).astype(F32), 0.0))
        wposf_scr[...] = wposf.astype(BF16)
        k = lax.broadcasted_iota(jnp.int32, (ncmp, LANES), 0)
        c = lax.broadcasted_iota(jnp.int32, (ncmp, LANES), 1)
        cposf_scr[...] = jnp.where(c == F_CMP, k.astype(F32), 0.0).astype(BF16)

    def const_feat(h):
        s = _alibi_slope(h)
        return jnp.where((lane == F_SEL_OFF) | (lane == F_WIN_OFF), s,
                         jnp.where(lane == F_WIN_BLK, SEL_LEN * s,
                                   jnp.where(lane == F_CMP, CMP_STRIDE * s, 0.0)))

    for h in range(NSA_HEADS):
        rows = slice(h * QB, (h + 1) * QB)
        qe_scr[rows, 0:LANES] = q_ref[:, h * LANES:(h + 1) * LANES] * 0.125
        qe_scr[rows, LANES:2 * LANES] = jnp.broadcast_to(const_feat(h), (QB, LANES)).astype(BF16)

    def normalised(o, guard):
        outs = []
        for pair in range(NSA_HEADS // 2):
            e = o[(2 * pair) * QB:(2 * pair + 1) * QB]
            d = o[(2 * pair + 1) * QB:(2 * pair + 2) * QB]
            num = jnp.where(lo_half, e, pltpu.roll(d, NSA_DH, 1))
            den = jnp.where(lo_half, pltpu.roll(e, NSA_DH, 1), d)
            if guard:
                den = jnp.where(den > 0.0, den, 1.0)
            outs.append(num / den)
        return outs

    ws = pl.multiple_of(jnp.maximum(t0 - WINDOW, 0), QB)
    qs = lax.broadcasted_iota(jnp.int32, (QB, WIN_KEYS), 0)
    kl = lax.broadcasted_iota(jnp.int32, (QB, WIN_KEYS), 1)
    dist_w = (t0 - ws) + qs - kl
    ok_w = (dist_w >= 0) & (dist_w < WINDOW)
    s_w = _dot_nt(qe_scr[...], jnp.concatenate([kw_ref[pl.ds(ws, WIN_KEYS), :], wposf_scr[...]], axis=1))

    q_sub = lax.broadcasted_iota(jnp.int32, (QB, ncmp), 0)
    n_lane = lax.broadcasted_iota(jnp.int32, (QB, ncmp), 1)
    ok_c = n_lane * CMP_STRIDE + (CMP_LEN - 1) <= t0 + q_sub
    s = _dot_nt(qe_scr[...], jnp.concatenate([kc_ref[0], cposf_scr[...]], axis=1))
    s = s + _tile_heads(jnp.where(ok_c, 0.0, NEG))
    e = jnp.exp(s - jnp.max(s, axis=-1, keepdims=True)) * _tile_heads(jnp.where(ok_c, 1.0, 0.0))
    e_bf = e.astype(BF16)
    oc = jnp.concatenate([_dot(e_bf[0:HALF], vc0_ref[0]), _dot(e_bf[HALF:2 * HALF], vc1_ref[0])],
                         axis=0)
    o_cmp = normalised(oc, True)
    l_c = jnp.where(lo_half, pltpu.roll(oc, NSA_DH, 1), oc)
    inv_l = 1.0 / jnp.where(l_c > 0.0, l_c, 1.0)
    p_c = e * jnp.concatenate([inv_l] * (ncmp // LANES), axis=1)

    imp_ts = []
    for g in range(NSA_KV):
        psum = p_c[(g * NSA_REP) * QB:(g * NSA_REP + 1) * QB]
        for r in range(1, NSA_REP):
            psum = psum + p_c[(g * NSA_REP + r) * QB:(g * NSA_REP + r + 1) * QB]
        p_hi = psum.astype(BF16)
        p_lo = (psum - p_hi.astype(F32)).astype(BF16)
        imp = _dot(p_hi, ov_ref[...]) + _dot(p_lo, ov_ref[...])
        imp_ts.append(imp.T[0:nbs, :])

    n_grp = nbs // SUBLANES
    sub8 = lax.broadcasted_iota(jnp.int32, (SUBLANES, QB), 0)
    tpos = t0 + lax.broadcasted_iota(jnp.int32, (1, QB), 1)
    cur = tpos // SEL_LEN
    j_lane = lane.astype(F32)
    for g in range(NSA_KV):
        imp_t = imp_ts[g]
        j_sub = lax.broadcasted_iota(jnp.int32, (nbs, QB), 0)
        forced = (j_sub == 0) | (j_sub == cur) | (j_sub == cur - 1)
        score = jnp.where(j_sub <= cur, imp_t + jnp.where(forced, FORCE_BONUS, 0.0), NEG)
        parts = [score[a * SUBLANES:(a + 1) * SUBLANES, :] for a in range(n_grp)]
        ranks = [jnp.zeros((SUBLANES, QB), F32) for _ in range(n_grp)]
        for i in range(nbs):
            row = jnp.broadcast_to(score[i:i + 1, :], (SUBLANES, QB))
            for a in range(n_grp):
                if a * SUBLANES > i:
                    beats = jnp.where(row >= parts[a], 1.0, 0.0)
                elif (a + 1) * SUBLANES - 1 < i:
                    beats = jnp.where(row > parts[a], 1.0, 0.0)
                else:
                    beats = jnp.where(sub8 + a * SUBLANES > i,
                                      jnp.where(row >= parts[a], 1.0, 0.0),
                                      jnp.where(row > parts[a], 1.0, 0.0))
                ranks[a] = ranks[a] + beats
        sel_t = [jnp.where(rk < float(min(SEL_TOP, nbs)), 0.0, NEG) for rk in ranks]
        if nbs < LANES:
            sel_t.append(jnp.zeros((LANES - nbs, QB), F32))
        sel_bias = jnp.concatenate(sel_t, axis=0).T
        for r in range(NSA_REP):
            h = g * NSA_REP + r
            feat = jnp.where(lo_half, sel_bias + (_alibi_slope(h) * SEL_LEN) * j_lane, const_feat(h))
            qe_scr[h * QB:(h + 1) * QB, LANES:2 * LANES] = feat.astype(BF16)

    s_w = s_w + _tile_heads(jnp.where(ok_w, 0.0, NEG))
    p_w = jnp.exp(s_w - jnp.max(s_w, axis=-1, keepdims=True)).astype(BF16)
    o_win = normalised(jnp.concatenate([_dot(p_w[0:HALF], vw0_ref[pl.ds(ws, WIN_KEYS), :]),
                                        _dot(p_w[HALF:2 * HALF], vw1_ref[pl.ds(ws, WIN_KEYS), :])],
                                       axis=0), False)

    last_tile = t0 // KT
    mrun_scr[...] = jnp.full(mrun_scr.shape, NEG, F32)

    U = SEL_GROUP

    def score_group(kt0, count, causal):
        k0 = pl.multiple_of(kt0 * KT, KT)
        kx = jnp.concatenate([ks_ref[pl.ds(k0, count * KT), :],
                              posf_scr[pl.ds(k0, count * KT), :]], axis=1)
        s = _dot_nt(qe_scr[...], kx)
        tile_max = None
        for j in range(count):
            sj = s[:, j * KT:(j + 1) * KT]
            if causal and j == count - 1:
                qs = lax.broadcasted_iota(jnp.int32, (QB, KT), 0)
                kl = lax.broadcasted_iota(jnp.int32, (QB, KT), 1)
                sj = sj + _tile_heads(jnp.where(k0 + j * KT + kl <= t0 + qs, 0.0, NEG))
            s_scr[kt0 + j] = sj
            for c0 in range(0, KT, LANES):
                part = sj[:, c0:c0 + LANES]
                tile_max = part if tile_max is None else jnp.maximum(tile_max, part)
        mrun_scr[...] = jnp.maximum(mrun_scr[...], tile_max)

    def score_full_group(i, carry):
        score_group(i * U, U, False)
        return carry

    lax.fori_loop(0, last_tile // U, score_full_group, 0)
    for r in range(U):
        @pl.when(last_tile % U == r)
        def _(r=r):
            score_group(last_tile - r, r + 1, True)

    mb_scr[...] = jnp.broadcast_to(jnp.max(mrun_scr[...], axis=-1, keepdims=True), mb_scr.shape)
    acc_scr[...] = jnp.zeros(acc_scr.shape, F32)

    def value_group(kt0, count):
        k0 = pl.multiple_of(kt0 * KT, KT)
        mb = mb_scr[...]
        mbw = jnp.concatenate([mb] * (KT // LANES), axis=1)
        p = jnp.concatenate([jnp.exp(s_scr[kt0 + j] - mbw).astype(BF16) for j in range(count)],
                            axis=1)
        acc_scr[0:HALF, :] += _dot(p[0:HALF], vs0_ref[pl.ds(k0, count * KT), :])
        acc_scr[HALF:2 * HALF, :] += _dot(p[HALF:2 * HALF], vs1_ref[pl.ds(k0, count * KT), :])

    def value_full_group(i, carry):
        value_group(i * U, U)
        return carry

    n_tiles = last_tile + 1
    lax.fori_loop(0, n_tiles // U, value_full_group, 0)
    for r in range(1, U):
        @pl.when(n_tiles % U == r)
        def _(r=r):
            value_group(n_tiles - r, r)

    o_sel = normalised(acc_scr[...], False)

    gate = jax.nn.sigmoid(g_ref[...])
    for pair in range(NSA_HEADS // 2):
        y = None
        for b, o in enumerate((o_cmp, o_sel, o_win)):
            c_e = SUBLANES + 3 * (2 * pair) + b
            c_o = SUBLANES + 3 * (2 * pair + 1) + b
            term = jnp.where(lo_half, gate[:, c_e:c_e + 1], gate[:, c_o:c_o + 1]) * o[pair]
            y = term if y is None else y + term
        y_ref[:, pair * LANES:(pair + 1) * LANES] = y.astype(y_ref.dtype)


def _nsa(pr, kc, vc0, vc1, batch, seq):
    QB = Q_BLOCK
    nq = seq // QB
    nbs = seq // SEL_LEN
    ncmp = kc.shape[1]
    cs = np.arange(ncmp) * CMP_STRIDE
    js = np.arange(nbs) * SEL_LEN
    ov = np.zeros((ncmp, LANES), np.float32)
    ov[:, :nbs] = (cs[:, None] < js[None, :] + SEL_LEN) & (cs[:, None] + CMP_LEN > js[None, :])
    tok = lambda b, i: (b * nq + i, 0)
    per_b = lambda b, i: (b, 0)
    per_b3 = lambda b, i: (b, 0, 0)
    kv_spec = pl.BlockSpec((seq, LANES), per_b)
    cmp_spec = pl.BlockSpec((1, ncmp, LANES), per_b3)
    return pl.pallas_call(
        functools.partial(_nsa_kernel, nbs),
        out_shape=jax.ShapeDtypeStruct((batch * seq, NSA_WIDTH), BF16),
        grid=(batch, nq),
        in_specs=[pl.BlockSpec((QB, NSA_HEADS * LANES), tok),
                  pl.BlockSpec((QB, LANES), tok),
                  cmp_spec, cmp_spec, cmp_spec,
                  kv_spec, kv_spec, kv_spec, kv_spec, kv_spec, kv_spec,
                  pl.BlockSpec((ncmp, LANES), lambda b, i: (0, 0))],
        out_specs=pl.BlockSpec((QB, NSA_WIDTH), tok),
        scratch_shapes=[pltpu.VMEM((NSA_ROWS, 2 * LANES), BF16),
                        pltpu.VMEM((seq, LANES), BF16),
                        pltpu.VMEM((WIN_KEYS, LANES), BF16),
                        pltpu.VMEM((ncmp, LANES), BF16),
                        pltpu.VMEM((seq // SEL_KT, NSA_ROWS, SEL_KT), F32),
                        pltpu.VMEM((NSA_ROWS, LANES), F32),
                        pltpu.VMEM((NSA_ROWS, LANES), F32),
                        pltpu.VMEM((NSA_ROWS, LANES), F32)],
        compiler_params=pltpu.CompilerParams(dimension_semantics=("parallel", "arbitrary"),
                                             vmem_limit_bytes=VMEM_LIMIT),
        name="nsa",
    )(pr["nq"], pr["gcol"], kc, vc0, vc1, pr["ks"], pr["vs0"], pr["vs1"],
      pr["kw"], pr["vw0"], pr["vw1"], jnp.asarray(ov, BF16))


def _memkv_kernel(mem_ref, g_ref, w_ref, kv_ref):
    u = _rms(mem_ref[0], g_ref[...]).astype(BF16)
    kv_ref[0] = _dot(u, w_ref[...]).astype(kv_ref.dtype)


def _memkv(mem, g_mem, w_mem_kv):
    batch, n_mem, d = mem.shape
    return pl.pallas_call(
        _memkv_kernel,
        out_shape=jax.ShapeDtypeStruct((batch, n_mem, 2 * XA_WIDTH), BF16),
        grid=(batch,),
        in_specs=[pl.BlockSpec((1, n_mem, d), lambda b: (b, 0, 0)),
                  _resident((1, d)), _resident((d, 2 * XA_WIDTH))],
        out_specs=pl.BlockSpec((1, n_mem, 2 * XA_WIDTH), lambda b: (b, 0, 0)),
        compiler_params=pltpu.CompilerParams(dimension_semantics=("parallel",),
                                             vmem_limit_bytes=VMEM_LIMIT),
        name="memkv",
    )(mem, g_mem[None, :], w_mem_kv.astype(BF16))


def _tail_kernel(x_ref, yml_ref, ynsa_ref, xq_ref, kv_ref, mg_ref, wb_ref, wo_ref, gf_ref,
                 w1_ref, w2_ref, gl_ref, out_ref):
    d = x_ref.shape[1]
    kv = kv_ref[0]
    heads = [slice(hh * XA_DH, (hh + 1) * XA_DH) for hh in range(XA_HEADS)]
    scores = [_dot_nt(xq_ref[:, sl], kv[:, sl]) * (XA_DH ** -0.5) for sl in heads]
    ups = [_dot(yml_ref[...], wb_ref[0]), _dot(ynsa_ref[...], wb_ref[1])]
    y_xa = []
    for hh, s in enumerate(scores):
        p = jnp.exp(s - jnp.max(s, axis=-1, keepdims=True))
        l = jnp.sum(p, axis=-1, keepdims=True)
        y_xa.append((_dot(p.astype(BF16), kv[:, XA_WIDTH + hh * XA_DH:XA_WIDTH + (hh + 1) * XA_DH])
                     / l).astype(BF16))
    ups.append(_dot(jnp.concatenate(y_xa, axis=-1), wb_ref[2]))
    merged = None
    for j in range(N_BRANCH):
        term = jax.nn.sigmoid(mg_ref[:, j * d:(j + 1) * d]) * ups[j]
        merged = term if merged is None else merged + term
    h = x_ref[...] + _dot(merged.astype(BF16), wo_ref[...])
    u = _rms(h, gf_ref[...]).astype(BF16)
    slabs = list(range(0, w1_ref.shape[1], FF_SLAB))
    up = lambda c0: _dot(u, w1_ref[:, c0:c0 + FF_SLAB])
    acc = h
    nxt = up(slabs[0])
    for i, c0 in enumerate(slabs):
        cur = nxt
        if i + 1 < len(slabs):
            nxt = up(slabs[i + 1])
        a = jnp.maximum(cur, 0.0)
        acc = acc + _dot((a * a).astype(BF16), w2_ref[c0:c0 + FF_SLAB, :])
    out_ref[...] = _rms(acc, gl_ref[...])


def _tail(x2, y_ml, y_nsa, pr, mem_kv, w_branch, w_out, g_ffn, w_ff1, w_ff2, g_final, seq):
    n_tok, d = x2.shape
    tm = TM_TAIL
    d_ff = w_ff1.shape[1]
    n_mem = mem_kv.shape[1]
    tiles_per_b = seq // tm
    row = lambda i: (i, 0)
    return pl.pallas_call(
        _tail_kernel,
        out_shape=jax.ShapeDtypeStruct((n_tok, d), F32),
        grid=(n_tok // tm,),
        in_specs=[pl.BlockSpec((tm, d), row),
                  pl.BlockSpec((tm, ML_WIDTH), row),
                  pl.BlockSpec((tm, NSA_WIDTH), row),
                  pl.BlockSpec((tm, XA_WIDTH), row),
                  pl.BlockSpec((1, n_mem, 2 * XA_WIDTH), lambda i: (i // tiles_per_b, 0, 0)),
                  pl.BlockSpec((tm, N_BRANCH * d), row),
                  _resident((N_BRANCH, ML_WIDTH, d)), _resident((d, d)), _resident((1, d)),
                  _resident((d, d_ff)), _resident((d_ff, d)), _resident((1, d))],
        out_specs=pl.BlockSpec((tm, d), row),
        compiler_params=pltpu.CompilerParams(dimension_semantics=("parallel",),
                                             vmem_limit_bytes=VMEM_LIMIT),
        name="tail",
    )(x2, y_ml, y_nsa, pr["xq"], mem_kv, pr["mg"], w_branch.astype(BF16), w_out.astype(BF16),
      g_ffn[None, :], w_ff1.astype(BF16), w_ff2.astype(BF16), g_final[None, :])


def _layer(x, mem, g_mix, w_in, b_in, ml_conv, ml_norm_g, cmp_pe, cmp_w1, cmp_w2, g_mem, w_mem_kv,
           w_branch, w_out, g_ffn, w_ff1, w_ff2, g_final):
    batch, seq, d = x.shape
    x2 = x.reshape(batch * seq, d)
    pr = _inproj(x2, g_mix, w_in, b_in, ml_conv, seq)
    y_ml = _mlstm(pr, ml_norm_g, batch, seq)
    kc, vc0, vc1 = _compress(pr, cmp_pe, cmp_w1, cmp_w2, batch, seq)
    y_nsa = _nsa(pr, kc, vc0, vc1, batch, seq)
    mem_kv = _memkv(mem, g_mem, w_mem_kv)
    out = _tail(x2, y_ml, y_nsa, pr, mem_kv, w_branch, w_out, g_ffn, w_ff1, w_ff2, g_final, seq)
    return out.reshape(batch, seq, d)


def kernel(x, mem, g_mix, w_in, b_in, ml_conv, ml_norm_g, cmp_pe, cmp_w1, cmp_w2, g_mem, w_mem_kv,
           w_branch, w_out, g_ffn, w_ff1, w_ff2, g_final):
    assert g_mix.shape[0] == 1, "single-layer block"
    return _layer(x, mem, g_mix[0], w_in[0], b_in[0], ml_conv[0], ml_norm_g[0], cmp_pe[0],
                  cmp_w1[0], cmp_w2[0], g_mem[0], w_mem_kv[0], w_branch[0], w_out[0], g_ffn[0],
                  w_ff1[0], w_ff2[0], g_final)
```

```python
import functools

import numpy as np
import jax
import jax.numpy as jnp
from jax import lax
from jax.experimental import pallas as pl
from jax.experimental.pallas import tpu as pltpu

F32 = jnp.float32
BF16 = jnp.bfloat16

EPS = 1e-6
NEG = -1e30
ML_HEADS = 4
ML_DH = 128
ML_WIDTH = ML_HEADS * ML_DH
ML_CONV = 4
ML_CHUNK = 128
ML_BATCH_PER_STEP = 1
NSA_HEADS = 8
NSA_KV = 2
NSA_REP = NSA_HEADS // NSA_KV
NSA_DH = 64
NSA_WIDTH = NSA_HEADS * NSA_DH
NSA_KV_WIDTH = NSA_KV * NSA_DH
CMP_LEN = 32
CMP_STRIDE = 16
CMP_HIDDEN = 256
SEL_LEN = 64
SEL_TOP = 16
WINDOW = 512
Q_BLOCK = 128
FORCE_BONUS = 1e3
SEL_KT = 256
SEL_GROUP = 8
XA_HEADS = 4
XA_DH = 128
XA_WIDTH = XA_HEADS * XA_DH
N_BRANCH = 3

LANES = 128
SUBLANES = 8
VMEM_LIMIT = 56 * 1024 * 1024

TM_PROJ = 256
TM_TAIL = 256
FF_SLAB = 1024

_NT = (((1,), (1,)), ((), ()))
_TN = (((0,), (0,)), ((), ()))


def _dot(a, b):
    return jnp.dot(a, b, preferred_element_type=F32)


def _dot_nt(a, b):
    return lax.dot_general(a, b, _NT, preferred_element_type=F32)


def _resident(shape):
    nd = len(shape)
    return pl.BlockSpec(shape, lambda *_: (0,) * nd, pipeline_mode=pl.Buffered(1))


def _rms(x, g):
    return x * lax.rsqrt(jnp.mean(x * x, axis=-1, keepdims=True) + EPS) * g


_PROJ_COLS = (
    ("mlstm", "qk", 2 * ML_WIDTH, BF16),
    ("mlstm", "v", ML_WIDTH, BF16),
    ("mlstm", "o", ML_WIDTH, BF16),
    ("merge", "mg", None, F32),
    ("nq", "nq", NSA_HEADS * LANES, BF16),
    ("xq", "xq", XA_WIDTH, BF16),
    ("narrow", "gcol", LANES, F32),
    ("narrow", "kc", NSA_KV_WIDTH, F32),
    ("narrow", "vc", NSA_KV_WIDTH, F32),
    ("narrow", "ks", NSA_KV_WIDTH, BF16),
    ("narrow", "kw", NSA_KV_WIDTH, BF16),
    ("narrow", "vs0", LANES, BF16),
    ("narrow", "vs1", LANES, BF16),
    ("narrow", "vw0", LANES, BF16),
    ("narrow", "vw1", LANES, BF16),
)
_PROJ_CHUNK = 512


def _proj_layout(d_model):
    groups = []
    for group, name, width, dt in _PROJ_COLS:
        width = N_BRANCH * d_model if width is None else width
        if not groups or groups[-1][0] != group:
            groups.append([group, [], 0])
        groups[-1][1].append((name, groups[-1][2], width, dt))
        groups[-1][2] += width
    return [tuple(g) for g in groups]


def _inproj_kernel(layout, tiles_per_b, x_ref, g_ref, *refs):
    n_grp = len(layout)
    n_out = sum(len(g[1]) for g in layout)
    w_refs, b_refs = refs[0:2 * n_grp:2], refs[1:2 * n_grp:2]
    wt_ref, bt_ref, conv_ref = refs[2 * n_grp:2 * n_grp + 3]
    out_refs = refs[2 * n_grp + 3:2 * n_grp + 3 + n_out]
    row_ref, xbuf = refs[-2], refs[-1]
    tm = x_ref.shape[0]
    halo = SUBLANES

    @pl.when(pl.program_id(0) % tiles_per_b == 0)
    def _():
        xbuf[0:halo, :] = jnp.zeros((halo, xbuf.shape[1]), F32)

    u = _rms(x_ref[...], g_ref[...]).astype(BF16)
    out_i = 0
    for (_, members, gwidth), w_ref, b_ref in zip(layout, w_refs, b_refs):
        o_refs = out_refs[out_i:out_i + len(members)]
        out_i += len(members)
        for c0 in range(0, gwidth, _PROJ_CHUNK):
            cw = min(_PROJ_CHUNK, gwidth - c0)
            chunk = _dot(u, w_ref[:, c0:c0 + cw]) + b_ref[:, c0:c0 + cw]
            for (name, off, width, dt), o_ref in zip(members, o_refs):
                lo, hi = max(off, c0), min(off + width, c0 + cw)
                if lo >= hi:
                    continue
                acc = chunk[:, lo - c0:hi - c0]
                cs = slice(lo - off, hi - off)
                if name == "qk":
                    xbuf[halo:halo + tm, cs] = acc
                    y = acc * conv_ref[0:1, cs]
                    for j in range(1, ML_CONV):
                        y = y + xbuf[halo - j:halo - j + tm, cs] * conv_ref[j:j + 1, cs]
                    xbuf[0:halo, cs] = xbuf[tm:tm + halo, cs]
                    acc = y * jax.nn.sigmoid(y)
                    if lo - off >= ML_WIDTH:
                        acc = acc * (ML_DH ** -0.5)
                elif name == "o":
                    acc = jax.nn.sigmoid(acc)
                o_ref[:, cs] = acc.astype(dt)
    row_ref[0] = _dot_nt(wt_ref[...], u) + bt_ref[...]


def _split_w_in(w_in, b_in):
    widths = (ML_WIDTH, ML_WIDTH, ML_WIDTH, ML_WIDTH, ML_HEADS, ML_HEADS,
              NSA_WIDTH, NSA_KV_WIDTH, NSA_KV_WIDTH, NSA_KV_WIDTH, NSA_KV_WIDTH, NSA_KV_WIDTH,
              NSA_KV_WIDTH, 3 * NSA_HEADS, XA_WIDTH, w_in.shape[1])
    names = ("ml_q", "ml_k", "ml_v", "ml_o", "ml_i", "ml_f", "ns_q", "ns_kc", "ns_vc", "ns_ks",
             "ns_vs", "ns_kw", "ns_vw", "ns_g", "xa_q", "mg")
    out, off = {}, 0
    for n, wd in zip(names, widths):
        end = w_in.shape[1] if n == "mg" else off + wd
        out[n] = (w_in[:, off:end], b_in[off:end])
        off = end
    return out


def _inproj(x2, g_mix, w_in, b_in, ml_conv, seq):
    n_tok, d = x2.shape
    layout = _proj_layout(d)
    p = _split_w_in(w_in, b_in)

    def pad_cols(w, b, width):
        return (jnp.pad(w, ((0, 0), (0, width - w.shape[1]))), jnp.pad(b, (0, width - b.shape[0])))

    wq, bq = p["ns_q"]
    wq = wq.reshape(d, NSA_HEADS, 1, NSA_DH)
    bq = bq.reshape(NSA_HEADS, 1, NSA_DH)
    in_group = jnp.asarray(np.arange(NSA_HEADS)[:, None] // NSA_REP == np.arange(NSA_KV)[None, :],
                           w_in.dtype)[:, :, None]
    wq_slots = wq * in_group
    bq_slots = bq * in_group
    gate_w = jnp.concatenate([p["ml_i"][0], p["ml_f"][0]], axis=1)
    gate_b = jnp.concatenate([p["ml_i"][1], p["ml_f"][1]])
    small_w = jnp.concatenate([gate_w, p["ns_g"][0]], axis=1)
    small_b = jnp.concatenate([gate_b, p["ns_g"][1]])

    def with_ones(wb, g):
        w, b = wb
        w = jnp.pad(w[:, g * NSA_DH:(g + 1) * NSA_DH], ((0, 0), (0, LANES - NSA_DH)))
        b = jnp.concatenate([b[g * NSA_DH:(g + 1) * NSA_DH], jnp.ones((LANES - NSA_DH,), b.dtype)])
        return w, b

    pieces = {
        "gcol": pad_cols(small_w, small_b, LANES),
        "kc": p["ns_kc"], "vc": p["ns_vc"], "ks": p["ns_ks"], "kw": p["ns_kw"],
        "vs0": with_ones(p["ns_vs"], 0), "vs1": with_ones(p["ns_vs"], 1),
        "vw0": with_ones(p["ns_vw"], 0), "vw1": with_ones(p["ns_vw"], 1),
    }
    narrow = [m[0] for m in layout[-1][1]]
    n_ml = 4 * ML_WIDTH
    group_wb = {
        "mlstm": (w_in[:, :n_ml], b_in[:n_ml]),
        "merge": p["mg"],
        "nq": (wq_slots.reshape(d, NSA_HEADS * LANES), bq_slots.reshape(NSA_HEADS * LANES)),
        "xq": p["xa_q"],
        "narrow": (jnp.concatenate([pieces[n][0].astype(BF16) for n in narrow], axis=1),
                   jnp.concatenate([pieces[n][1] for n in narrow])),
    }
    weight_args, weight_specs = [], []
    for group, _, gwidth in layout:
        w, b = group_wb[group]
        weight_args += [w.astype(BF16), b[None, :]]
        weight_specs += [_resident((d, gwidth)), _resident((1, gwidth))]

    tm = TM_PROJ
    tiles_per_b = seq // tm
    members = [m for _, ms, _ in layout for m in ms]
    out_shape = ([jax.ShapeDtypeStruct((n_tok, width), dt) for _, _, width, dt in members]
                 + [jax.ShapeDtypeStruct((n_tok // seq, SUBLANES, seq), F32)])
    out_specs = ([pl.BlockSpec((tm, width), lambda i: (i, 0)) for _, _, width, _ in members]
                 + [pl.BlockSpec((1, SUBLANES, tm),
                                 lambda i: (i // tiles_per_b, 0, i % tiles_per_b))])
    outs = pl.pallas_call(
        functools.partial(_inproj_kernel, layout, tiles_per_b),
        out_shape=out_shape,
        grid=(n_tok // tm,),
        in_specs=([pl.BlockSpec((tm, d), lambda i: (i, 0)), _resident((1, d))] + weight_specs
                  + [_resident((SUBLANES, d)), _resident((SUBLANES, 1)),
                     _resident((ML_CONV, 2 * ML_WIDTH))]),
        out_specs=out_specs,
        scratch_shapes=[pltpu.VMEM((tm + SUBLANES, 2 * ML_WIDTH), F32)],
        compiler_params=pltpu.CompilerParams(dimension_semantics=("arbitrary",),
                                             vmem_limit_bytes=VMEM_LIMIT),
        name="inproj",
    )(x2, g_mix[None, :], *weight_args, gate_w.T.astype(BF16), gate_b[:, None], ml_conv)
    return dict(zip([m[0] for m in members] + ["smallT"], outs))


def _mlstm_kernel(qk_ref, v_ref, o_ref, gcol_ref, grow_ref, ng_ref, y_ref, c_scr, m_scr):
    L = ML_CHUNK
    nb = qk_ref.shape[0]

    @pl.when(pl.program_id(1) == 0)
    def _():
        c_scr[...] = jnp.zeros(c_scr.shape, F32)
        m_scr[...] = jnp.zeros(m_scr.shape, F32)

    r_i = lax.broadcasted_iota(jnp.int32, (L, L), 0)
    c_i = lax.broadcasted_iota(jnp.int32, (L, L), 1)
    causal = c_i <= r_i
    tril = causal.astype(F32)
    triu = (r_i <= c_i).astype(F32)
    gcols = [gcol_ref[b] for b in range(nb)]
    grows = [grow_ref[b] for b in range(nb)]
    b_cols = jnp.dot(tril, jax.nn.log_sigmoid(jnp.concatenate(gcols, axis=1)),
                     precision=lax.Precision.HIGHEST, preferred_element_type=F32)
    b_rows = jnp.dot(jax.nn.log_sigmoid(jnp.concatenate(grows, axis=0)), triu,
                     precision=lax.Precision.HIGHEST, preferred_element_type=F32)
    ones_col = jnp.where(lax.broadcasted_iota(jnp.int32, (L, ML_DH), 1) == 0, 1.0, 0.0).astype(BF16)

    chains = [(b, hh) for b in range(nb) for hh in range(ML_HEADS)]

    def operands(b, hh):
        sl = slice(hh * ML_DH, (hh + 1) * ML_DH)
        q = qk_ref[b, :, sl]
        k = qk_ref[b, :, ML_WIDTH + hh * ML_DH:ML_WIDTH + (hh + 1) * ML_DH]
        v1 = jnp.concatenate([v_ref[b, :, sl], ones_col], axis=1)
        return q, k, v1, c_scr[b, hh]

    ops = [operands(b, hh) for b, hh in chains]
    qk_t = [_dot_nt(q, k) for q, k, _, _ in ops]
    inter = [_dot(q, c_prev.astype(BF16)) for q, _, _, c_prev in ops]

    stab, ke_all, s_all = [], [], []
    for i, (b, hh) in enumerate(chains):
        gcol, grow = gcols[b], grows[b]
        li_c = gcol[:, hh:hh + 1]
        b_c = b_cols[:, b * LANES + ML_HEADS + hh:b * LANES + ML_HEADS + hh + 1]
        li_r = grow[hh:hh + 1, :]
        b_r = b_rows[b * SUBLANES + ML_HEADS + hh:b * SUBLANES + ML_HEADS + hh + 1, :]
        g = b_c[L - 1:L, :]
        m_prev = m_scr[b, hh:hh + 1, 0:1]
        log_d = jnp.where(causal, b_c - b_r + li_r, -jnp.inf)
        inter_log = b_c + m_prev
        m_t = jnp.maximum(inter_log, jnp.max(log_d, axis=-1, keepdims=True))
        s_all.append((qk_t[i] * jnp.exp(log_d - m_t)).astype(BF16))
        w_end = g - b_c + li_c
        m_loc = jnp.max(w_end, axis=0, keepdims=True)
        ke_all.append((ops[i][1] * jnp.exp(w_end - m_loc)).astype(BF16))
        m_new = jnp.maximum(g + m_prev, m_loc)
        stab.append((m_t, jnp.exp(inter_log - m_t), jnp.exp(g + m_prev - m_new),
                     jnp.exp(m_loc - m_new), m_new))

    intra = [_dot(s, o[2]) for s, o in zip(s_all, ops)]
    a_c = [lax.dot_general(ke, o[2], _TN, preferred_element_type=F32) for ke, o in zip(ke_all, ops)]

    ys, new_c, new_m = [], [], []
    for i, (b, hh) in enumerate(chains):
        sl = slice(hh * ML_DH, (hh + 1) * ML_DH)
        m_t, sc, a, bb, m_new = stab[i]
        nd = intra[i] + sc * inter[i]
        den = nd[:, ML_DH:ML_DH + 1]
        h = nd[:, 0:ML_DH] / jnp.maximum(jnp.abs(den), jnp.exp(-m_t))
        hn = h * lax.rsqrt(jnp.mean(h * h, axis=-1, keepdims=True) + EPS)
        ys.append((o_ref[b, :, sl] * hn * ng_ref[:, sl]).astype(y_ref.dtype))
        new_c.append(a * ops[i][3] + bb * a_c[i])
        new_m.append(jnp.broadcast_to(m_new, (1, m_scr.shape[2])))

    per_b = lambda xs, axis: [jnp.concatenate(xs[b * ML_HEADS:(b + 1) * ML_HEADS], axis=axis)
                              for b in range(nb)]
    y_ref[...] = jnp.stack(per_b(ys, 1))
    c_scr[...] = jnp.stack([jnp.stack(new_c[b * ML_HEADS:(b + 1) * ML_HEADS]) for b in range(nb)])
    m_scr[:, 0:ML_HEADS, :] = jnp.stack(per_b(new_m, 0))


def _mlstm(pr, ml_norm_g, batch, seq):
    L = ML_CHUNK
    nb = ML_BATCH_PER_STEP
    chunk = lambda b, c: (b, c, 0)
    per_tok = lambda name: pr[name].reshape(batch, seq, pr[name].shape[-1])
    y = pl.pallas_call(
        _mlstm_kernel,
        out_shape=jax.ShapeDtypeStruct((batch, seq, ML_WIDTH), BF16),
        grid=(batch // nb, seq // L),
        in_specs=[pl.BlockSpec((nb, L, 2 * ML_WIDTH), chunk),
                  pl.BlockSpec((nb, L, ML_WIDTH), chunk),
                  pl.BlockSpec((nb, L, ML_WIDTH), chunk),
                  pl.BlockSpec((nb, L, LANES), chunk),
                  pl.BlockSpec((nb, SUBLANES, L), lambda b, c: (b, 0, c)),
                  pl.BlockSpec((1, ML_WIDTH), lambda b, c: (0, 0))],
        out_specs=pl.BlockSpec((nb, L, ML_WIDTH), chunk),
        scratch_shapes=[pltpu.VMEM((nb, ML_HEADS, ML_DH, 2 * ML_DH), F32),
                        pltpu.VMEM((nb, SUBLANES, LANES), F32)],
        compiler_params=pltpu.CompilerParams(dimension_semantics=("parallel", "arbitrary"),
                                             vmem_limit_bytes=VMEM_LIMIT),
        name="mlstm",
    )(per_tok("qk"), per_tok("v"), per_tok("o"), per_tok("gcol"), pr["smallT"],
      ml_norm_g[None, :])
    return y.reshape(batch * seq, ML_WIDTH)


def _gelu_tanh(x):
    return 0.5 * x * (1.0 + jnp.tanh(np.sqrt(2.0 / np.pi) * (x + 0.044715 * (x * x * x))))


def _compress_kernel(k_ref, v_ref, pe_ref, w1_ref, w2k_ref, w2v_ref, kc_ref, vc0_ref, vc1_ref):
    nrow = kc_ref.shape[1]
    S = CMP_STRIDE

    def hidden(x_ref, which):
        lo = hi = None
        for j in range(0, S, 2):
            x = [x_ref[pl.ds(j + d, nrow, stride=S), :] for d in range(2)]
            for half in range(CMP_LEN // S):
                l0 = half * S + j
                xp = jnp.concatenate([(x[d] + pe_ref[which, l0 + d:l0 + d + 1, :]).astype(BF16)
                                      for d in range(2)], axis=1)
                t = _dot(xp, w1_ref[which, l0 // 2])
                if half == 0:
                    lo = t if lo is None else lo + t
                else:
                    hi = t if hi is None else hi + t
        return _gelu_tanh(lo + pltpu.roll(hi, nrow - 1, 0)).astype(BF16)

    kc_ref[0] = _dot(hidden(k_ref, 0), w2k_ref[...]).astype(kc_ref.dtype)
    hv = hidden(v_ref, 1)
    ones_half = lax.broadcasted_iota(jnp.int32, (nrow, LANES), 1) >= NSA_DH
    for g, o_ref in enumerate((vc0_ref, vc1_ref)):
        o_ref[0] = jnp.where(ones_half, 1.0, _dot(hv, w2v_ref[g])).astype(o_ref.dtype)


def _compress(pr, cmp_pe, cmp_w1, cmp_w2, batch, seq):
    nrow = seq // CMP_STRIDE
    hid = NSA_KV * CMP_HIDDEN
    w1 = cmp_w1.astype(BF16).reshape(2, CMP_LEN, NSA_DH, CMP_HIDDEN)
    z1 = jnp.zeros_like(w1)
    w1e = jnp.stack([jnp.concatenate([w1, z1], axis=-1), jnp.concatenate([z1, w1], axis=-1)], axis=2)
    w1e = w1e.reshape(2, CMP_LEN // 2, 2 * NSA_KV_WIDTH, hid)
    w2 = cmp_w2.astype(BF16)
    z2 = jnp.zeros_like(w2[0])
    w2k = jnp.concatenate([jnp.concatenate([w2[0], z2], axis=1),
                           jnp.concatenate([z2, w2[0]], axis=1)], axis=0)
    w2v = jnp.stack([jnp.pad(w2[1], ((g * CMP_HIDDEN, (NSA_KV - 1 - g) * CMP_HIDDEN),
                                     (0, LANES - NSA_DH))) for g in range(NSA_KV)])
    pe = jnp.concatenate([cmp_pe] * NSA_KV, axis=-1)
    tok_blk = pl.BlockSpec((seq, NSA_KV_WIDTH), lambda b: (b, 0))
    out_blk = pl.BlockSpec((1, nrow, LANES), lambda b: (b, 0, 0))
    return pl.pallas_call(
        _compress_kernel,
        out_shape=(jax.ShapeDtypeStruct((batch, nrow, LANES), BF16),) * 3,
        grid=(batch,),
        in_specs=[tok_blk, tok_blk,
                  _resident((2, CMP_LEN, NSA_KV_WIDTH)),
                  _resident((2, CMP_LEN // 2, 2 * NSA_KV_WIDTH, hid)),
                  _resident((hid, NSA_KV_WIDTH)),
                  _resident((NSA_KV, hid, LANES))],
        out_specs=(out_blk, out_blk, out_blk),
        compiler_params=pltpu.CompilerParams(dimension_semantics=("parallel",),
                                             vmem_limit_bytes=VMEM_LIMIT),
        name="compress",
    )(pr["kc"], pr["vc"], pe, w1e, w2k, w2v)


def _alibi_slope(h):
    return float(2.0 ** (-8.0 * (h + 1) / NSA_HEADS))


NSA_ROWS = NSA_HEADS * Q_BLOCK
WIN_KEYS = WINDOW + Q_BLOCK
F_SEL_BLOCK = 0
F_SEL_OFF = SEL_LEN
F_WIN_OFF = SEL_LEN + 1
F_WIN_BLK = SEL_LEN + 2
F_CMP = SEL_LEN + 3


def _tile_heads(x):
    return jnp.concatenate([x] * NSA_HEADS, axis=0)


def _nsa_kernel(nbs, q_ref, g_ref, kc_ref, vc0_ref, vc1_ref, ks_ref, vs0_ref, vs1_ref,
                kw_ref, vw0_ref, vw1_ref, ov_ref, y_ref,
                qe_scr, posf_scr, wposf_scr, cposf_scr, s_scr, mrun_scr, mb_scr, acc_scr):
    QB = Q_BLOCK
    KT = SEL_KT
    HALF = NSA_REP * QB
    t0 = pl.program_id(1) * QB
    ncmp = kc_ref.shape[1]
    lane = lax.broadcasted_iota(jnp.int32, (1, LANES), 1)
    lo_half = lane < NSA_DH

    @pl.when(pl.program_id(1) == 0)
    def _():
        seq = posf_scr.shape[0]
        k = lax.broadcasted_iota(jnp.int32, (seq, LANES), 0)
        c = lax.broadcasted_iota(jnp.int32, (seq, LANES), 1)
        posf = jnp.where(c < SEL_LEN, jnp.where(k // SEL_LEN == c, 1.0, 0.0),
                         jnp.where(c == F_SEL_OFF, (k % SEL_LEN).astype(F32), 0.0))
        posf_scr[...] = posf.astype(BF16)
        k = lax.broadcasted_iota(jnp.int32, (WIN_KEYS, LANES), 0)
        c = lax.broadcasted_iota(jnp.int32, (WIN_KEYS, LANES), 1)
        wposf = jnp.where(c == F_WIN_OFF, (k % SEL_LEN).astype(F32),
                          jnp.where(c == F_WIN_BLK, (k // SEL_LEN).astype(F32), 0.0))
        wposf_scr[...] = wposf.astype(BF16)
        k = lax.broadcasted_iota(jnp.int32, (ncmp, LANES), 0)
        c = lax.broadcasted_iota(jnp.int32, (ncmp, LANES), 1)
        cposf_scr[...] = jnp.where(c == F_CMP, k.astype(F32), 0.0).astype(BF16)

    def const_feat(h):
        s = _alibi_slope(h)
        return jnp.where((lane == F_SEL_OFF) | (lane == F_WIN_OFF), s,
                         jnp.where(lane == F_WIN_BLK, SEL_LEN * s,
                                   jnp.where(lane == F_CMP, CMP_STRIDE * s, 0.0)))

    for h in range(NSA_HEADS):
        rows = slice(h * QB, (h + 1) * QB)
        qe_scr[rows, 0:LANES] = q_ref[:, h * LANES:(h + 1) * LANES] * 0.125
        qe_scr[rows, LANES:2 * LANES] = jnp.broadcast_to(const_feat(h), (QB, LANES)).astype(BF16)

    def normalised(o, guard):
        outs = []
        for pair in range(NSA_HEADS // 2):
            e = o[(2 * pair) * QB:(2 * pair + 1) * QB]
            d = o[(2 * pair + 1) * QB:(2 * pair + 2) * QB]
            num = jnp.where(lo_half, e, pltpu.roll(d, NSA_DH, 1))
            den = jnp.where(lo_half, pltpu.roll(e, NSA_DH, 1), d)
            if guard:
                den = jnp.where(den > 0.0, den, 1.0)
            outs.append(num / den)
        return outs

    ws = pl.multiple_of(jnp.maximum(t0 - WINDOW, 0), QB)
    qs = lax.broadcasted_iota(jnp.int32, (QB, WIN_KEYS), 0)
    kl = lax.broadcasted_iota(jnp.int32, (QB, WIN_KEYS), 1)
    dist_w = (t0 - ws) + qs - kl
    ok_w = (dist_w >= 0) & (dist_w < WINDOW)
    s_w = _dot_nt(qe_scr[...], jnp.concatenate([kw_ref[pl.ds(ws, WIN_KEYS), :], wposf_scr[...]], axis=1))

    q_sub = lax.broadcasted_iota(jnp.int32, (QB, ncmp), 0)
    n_lane = lax.broadcasted_iota(jnp.int32, (QB, ncmp), 1)
    ok_c = n_lane * CMP_STRIDE + (CMP_LEN - 1) <= t0 + q_sub
    s = _dot_nt(qe_scr[...], jnp.concatenate([kc_ref[0], cposf_scr[...]], axis=1))
    s = s + _tile_heads(jnp.where(ok_c, 0.0, NEG))
    e = jnp.exp(s - jnp.max(s, axis=-1, keepdims=True)) * _tile_heads(jnp.where(ok_c, 1.0, 0.0))
    e_bf = e.astype(BF16)
    oc = jnp.concatenate([_dot(e_bf[0:HALF], vc0_ref[0]), _dot(e_bf[HALF:2 * HALF], vc1_ref[0])],
                         axis=0)
    o_cmp = normalised(oc, True)
    l_c = jnp.where(lo_half, pltpu.roll(oc, NSA_DH, 1), oc)
    inv_l = 1.0 / jnp.where(l_c > 0.0, l_c, 1.0)
    p_c = e * jnp.concatenate([inv_l] * (ncmp // LANES), axis=1)

    imp_ts = []
    for g in range(NSA_KV):
        psum = p_c[(g * NSA_REP) * QB:(g * NSA_REP + 1) * QB]
        for r in range(1, NSA_REP):
            psum = psum + p_c[(g * NSA_REP + r) * QB:(g * NSA_REP + r + 1) * QB]
        p_hi = psum.astype(BF16)
        p_lo = (psum - p_hi.astype(F32)).astype(BF16)
        imp = _dot(p_hi, ov_ref[...]) + _dot(p_lo, ov_ref[...])
        imp_ts.append(imp.T[0:nbs, :])

    n_grp = nbs // SUBLANES
    sub8 = lax.broadcasted_iota(jnp.int32, (SUBLANES, QB), 0)
    tpos = t0 + lax.broadcasted_iota(jnp.int32, (1, QB), 1)
    cur = tpos // SEL_LEN
    j_lane = lane.astype(F32)
    for g in range(NSA_KV):
        imp_t = imp_ts[g]
        j_sub = lax.broadcasted_iota(jnp.int32, (nbs, QB), 0)
        forced = (j_sub == 0) | (j_sub == cur) | (j_sub == cur - 1)
        score = jnp.where(j_sub <= cur, imp_t + jnp.where(forced, FORCE_BONUS, 0.0), NEG)
        parts = [score[a * SUBLANES:(a + 1) * SUBLANES, :] for a in range(n_grp)]
        ranks = [jnp.zeros((SUBLANES, QB), F32) for _ in range(n_grp)]
        for i in range(nbs):
            row = jnp.broadcast_to(score[i:i + 1, :], (SUBLANES, QB))
            for a in range(n_grp):
                if a * SUBLANES > i:
                    beats = jnp.where(row >= parts[a], 1.0, 0.0)
                elif (a + 1) * SUBLANES - 1 < i:
                    beats = jnp.where(row > parts[a], 1.0, 0.0)
                else:
                    beats = jnp.where(sub8 + a * SUBLANES > i,
                                      jnp.where(row >= parts[a], 1.0, 0.0),
                                      jnp.where(row > parts[a], 1.0, 0.0))
                ranks[a] = ranks[a] + beats
        sel_t = [jnp.where(rk < float(min(SEL_TOP, nbs)), 0.0, NEG) for rk in ranks]
        if nbs < LANES:
            sel_t.append(jnp.zeros((LANES - nbs, QB), F32))
        sel_bias = jnp.concatenate(sel_t, axis=0).T
        for r in range(NSA_REP):
            h = g * NSA_REP + r
            feat = jnp.where(lo_half, sel_bias + (_alibi_slope(h) * SEL_LEN) * j_lane, const_feat(h))
            qe_scr[h * QB:(h + 1) * QB, LANES:2 * LANES] = feat.astype(BF16)

    s_w = s_w + _tile_heads(jnp.where(ok_w, 0.0, NEG))
    p_w = jnp.exp(s_w - jnp.max(s_w, axis=-1, keepdims=True)).astype(BF16)
    o_win = normalised(jnp.concatenate([_dot(p_w[0:HALF], vw0_ref[pl.ds(ws, WIN_KEYS), :]),
                                        _dot(p_w[HALF:2 * HALF], vw1_ref[pl.ds(ws, WIN_KEYS), :])],
                                       axis=0), False)

    last_tile = t0 // KT
    mrun_scr[...] = jnp.full(mrun_scr.shape, NEG, F32)

    U = SEL_GROUP

    def score_group(kt0, count, causal):
        k0 = pl.multiple_of(kt0 * KT, KT)
        kx = jnp.concatenate([ks_ref[pl.ds(k0, count * KT), :],
                              posf_scr[pl.ds(k0, count * KT), :]], axis=1)
        s = _dot_nt(qe_scr[...], kx)
        tile_max = None
        for j in range(count):
            sj = s[:, j * KT:(j + 1) * KT]
            if causal and j == count - 1:
                qs = lax.broadcasted_iota(jnp.int32, (QB, KT), 0)
                kl = lax.broadcasted_iota(jnp.int32, (QB, KT), 1)
                sj = sj + _tile_heads(jnp.where(k0 + j * KT + kl <= t0 + qs, 0.0, NEG))
            s_scr[kt0 + j] = sj
            for c0 in range(0, KT, LANES):
                part = sj[:, c0:c0 + LANES]
                tile_max = part if tile_max is None else jnp.maximum(tile_max, part)
        mrun_scr[...] = jnp.maximum(mrun_scr[...], tile_max)

    def score_full_group(i, carry):
        score_group(i * U, U, False)
        return carry

    lax.fori_loop(0, last_tile // U, score_full_group, 0)
    for r in range(U):
        @pl.when(last_tile % U == r)
        def _(r=r):
            score_group(last_tile - r, r + 1, True)

    mb_scr[...] = jnp.broadcast_to(jnp.max(mrun_scr[...], axis=-1, keepdims=True), mb_scr.shape)
    acc_scr[...] = jnp.zeros(acc_scr.shape, F32)

    def value_group(kt0, count):
        k0 = pl.multiple_of(kt0 * KT, KT)
        mb = mb_scr[...]
        mbw = jnp.concatenate([mb] * (KT // LANES), axis=1)
        p = jnp.concatenate([jnp.exp(s_scr[kt0 + j] - mbw).astype(BF16) for j in range(count)],
                            axis=1)
        acc_scr[0:HALF, :] += _dot(p[0:HALF], vs0_ref[pl.ds(k0, count * KT), :])
        acc_scr[HALF:2 * HALF, :] += _dot(p[HALF:2 * HALF], vs1_ref[pl.ds(k0, count * KT), :])

    def value_full_group(i, carry):
        value_group(i * U, U)
        return carry

    n_tiles = last_tile + 1
    lax.fori_loop(0, n_tiles // U, value_full_group, 0)
    for r in range(1, U):
        @pl.when(n_tiles % U == r)
        def _(r=r):
            value_group(n_tiles - r, r)

    o_sel = normalised(acc_scr[...], False)

    gate = jax.nn.sigmoid(g_ref[...])
    for pair in range(NSA_HEADS // 2):
        y = None
        for b, o in enumerate((o_cmp, o_sel, o_win)):
            c_e = SUBLANES + 3 * (2 * pair) + b
            c_o = SUBLANES + 3 * (2 * pair + 1) + b
            term = jnp.where(lo_half, gate[:, c_e:c_e + 1], gate[:, c_o:c_o + 1]) * o[pair]
            y = term if y is None else y + term
        y_ref[:, pair * LANES:(pair + 1) * LANES] = y.astype(y_ref.dtype)


def _nsa(pr, kc, vc0, vc1, batch, seq):
    QB = Q_BLOCK
    nq = seq // QB
    nbs = seq // SEL_LEN
    ncmp = kc.shape[1]
    cs = np.arange(ncmp) * CMP_STRIDE
    js = np.arange(nbs) * SEL_LEN
    ov = np.zeros((ncmp, LANES), np.float32)
    ov[:, :nbs] = (cs[:, None] < js[None, :] + SEL_LEN) & (cs[:, None] + CMP_LEN > js[None, :])
    tok = lambda b, i: (b * nq + i, 0)
    per_b = lambda b, i: (b, 0)
    per_b3 = lambda b, i: (b, 0, 0)
    kv_spec = pl.BlockSpec((seq, LANES), per_b)
    cmp_spec = pl.BlockSpec((1, ncmp, LANES), per_b3)
    return pl.pallas_call(
        functools.partial(_nsa_kernel, nbs),
        out_shape=jax.ShapeDtypeStruct((batch * seq, NSA_WIDTH), BF16),
        grid=(batch, nq),
        in_specs=[pl.BlockSpec((QB, NSA_HEADS * LANES), tok),
                  pl.BlockSpec((QB, LANES), tok),
                  cmp_spec, cmp_spec, cmp_spec,
                  kv_spec, kv_spec, kv_spec, kv_spec, kv_spec, kv_spec,
                  pl.BlockSpec((ncmp, LANES), lambda b, i: (0, 0))],
        out_specs=pl.BlockSpec((QB, NSA_WIDTH), tok),
        scratch_shapes=[pltpu.VMEM((NSA_ROWS, 2 * LANES), BF16),
                        pltpu.VMEM((seq, LANES), BF16),
                        pltpu.VMEM((WIN_KEYS, LANES), BF16),
                        pltpu.VMEM((ncmp, LANES), BF16),
                        pltpu.VMEM((seq // SEL_KT, NSA_ROWS, SEL_KT), F32),
                        pltpu.VMEM((NSA_ROWS, LANES), F32),
                        pltpu.VMEM((NSA_ROWS, LANES), F32),
                        pltpu.VMEM((NSA_ROWS, LANES), F32)],
        compiler_params=pltpu.CompilerParams(dimension_semantics=("parallel", "arbitrary"),
                                             vmem_limit_bytes=VMEM_LIMIT),
        name="nsa",
    )(pr["nq"], pr["gcol"], kc, vc0, vc1, pr["ks"], pr["vs0"], pr["vs1"],
      pr["kw"], pr["vw0"], pr["vw1"], jnp.asarray(ov, BF16))


def _memkv_kernel(mem_ref, g_ref, w_ref, kv_ref):
    u = _rms(mem_ref[0], g_ref[...]).astype(BF16)
    kv_ref[0] = _dot(u, w_ref[...]).astype(kv_ref.dtype)


def _memkv(mem, g_mem, w_mem_kv):
    batch, n_mem, d = mem.shape
    return pl.pallas_call(
        _memkv_kernel,
        out_shape=jax.ShapeDtypeStruct((batch, n_mem, 2 * XA_WIDTH), BF16),
        grid=(batch,),
        in_specs=[pl.BlockSpec((1, n_mem, d), lambda b: (b, 0, 0)),
                  _resident((1, d)), _resident((d, 2 * XA_WIDTH))],
        out_specs=pl.BlockSpec((1, n_mem, 2 * XA_WIDTH), lambda b: (b, 0, 0)),
        compiler_params=pltpu.CompilerParams(dimension_semantics=("parallel",),
                                             vmem_limit_bytes=VMEM_LIMIT),
        name="memkv",
    )(mem, g_mem[None, :], w_mem_kv.astype(BF16))


def _tail_kernel(x_ref, yml_ref, ynsa_ref, xq_ref, kv_ref, mg_ref, wb_ref, wo_ref, gf_ref,
                 w1_ref, w2_ref, gl_ref, out_ref):
    d = x_ref.shape[1]
    kv = kv_ref[0]
    heads = [slice(hh * XA_DH, (hh + 1) * XA_DH) for hh in range(XA_HEADS)]
    scores = [_dot_nt(xq_ref[:, sl], kv[:, sl]) * (XA_DH ** -0.5) for sl in heads]
    ups = [_dot(yml_ref[...], wb_ref[0]), _dot(ynsa_ref[...], wb_ref[1])]
    y_xa = []
    for hh, s in enumerate(scores):
        p = jnp.exp(s - jnp.max(s, axis=-1, keepdims=True))
        l = jnp.sum(p, axis=-1, keepdims=True)
        y_xa.append((_dot(p.astype(BF16), kv[:, XA_WIDTH + hh * XA_DH:XA_WIDTH + (hh + 1) * XA_DH])
                     / l).astype(BF16))
    ups.append(_dot(jnp.concatenate(y_xa, axis=-1), wb_ref[2]))
    merged = None
    for j in range(N_BRANCH):
        term = jax.nn.sigmoid(mg_ref[:, j * d:(j + 1) * d]) * ups[j]
        merged = term if merged is None else merged + term
    h = x_ref[...] + _dot(merged.astype(BF16), wo_ref[...])
    u = _rms(h, gf_ref[...]).astype(BF16)
    slabs = list(range(0, w1_ref.shape[1], FF_SLAB))
    up = lambda c0: _dot(u, w1_ref[:, c0:c0 + FF_SLAB])
    acc = h
    nxt = up(slabs[0])
    for i, c0 in enumerate(slabs):
        cur = nxt
        if i + 1 < len(slabs):
            nxt = up(slabs[i + 1])
        a = jnp.maximum(cur, 0.0)
        acc = acc + _dot((a * a).astype(BF16), w2_ref[c0:c0 + FF_SLAB, :])
    out_ref[...] = _rms(acc, gl_ref[...])


def _tail(x2, y_ml, y_nsa, pr, mem_kv, w_branch, w_out, g_ffn, w_ff1, w_ff2, g_final, seq):
    n_tok, d = x2.shape
    tm = TM_TAIL
    d_ff = w_ff1.shape[1]
    n_mem = mem_kv.shape[1]
    tiles_per_b = seq // tm
    row = lambda i: (i, 0)
    return pl.pallas_call(
        _tail_kernel,
        out_shape=jax.ShapeDtypeStruct((n_tok, d), F32),
        grid=(n_tok // tm,),
        in_specs=[pl.BlockSpec((tm, d), row),
                  pl.BlockSpec((tm, ML_WIDTH), row),
                  pl.BlockSpec((tm, NSA_WIDTH), row),
                  pl.BlockSpec((tm, XA_WIDTH), row),
                  pl.BlockSpec((1, n_mem, 2 * XA_WIDTH), lambda i: (i // tiles_per_b, 0, 0)),
                  pl.BlockSpec((tm, N_BRANCH * d), row),
                  _resident((N_BRANCH, ML_WIDTH, d)), _resident((d, d)), _resident((1, d)),
                  _resident((d, d_ff)), _resident((d_ff, d)), _resident((1, d))],
        out_specs=pl.BlockSpec((tm, d), row),
        compiler_params=pltpu.CompilerParams(dimension_semantics=("parallel",),
                                             vmem_limit_bytes=VMEM_LIMIT),
        name="tail",
    )(x2, y_ml, y_nsa, pr["xq"], mem_kv, pr["mg"], w_branch.astype(BF16), w_out.astype(BF16),
      g_ffn[None, :], w_ff1.astype(BF16), w_ff2.astype(BF16), g_final[None, :])


def _layer(x, mem, g_mix, w_in, b_in, ml_conv, ml_norm_g, cmp_pe, cmp_w1, cmp_w2, g_mem, w_mem_kv,
           w_branch, w_out, g_ffn, w_ff1, w_ff2, g_final):
    batch, seq, d = x.shape
    x2 = x.reshape(batch * seq, d)
    pr = _inproj(x2, g_mix, w_in, b_in, ml_conv, seq)
    y_ml = _mlstm(pr, ml_norm_g, batch, seq)
    kc, vc0, vc1 = _compress(pr, cmp_pe, cmp_w1, cmp_w2, batch, seq)
    y_nsa = _nsa(pr, kc, vc0, vc1, batch, seq)
    mem_kv = _memkv(mem, g_mem, w_mem_kv)
    out = _tail(x2, y_ml, y_nsa, pr, mem_kv, w_branch, w_out, g_ffn, w_ff1, w_ff2, g_final, seq)
    return out.reshape(batch, seq, d)


def kernel(x, mem, g_mix, w_in, b_in, ml_conv, ml_norm_g, cmp_pe, cmp_w1, cmp_w2, g_mem, w_mem_kv,
           w_branch, w_out, g_ffn, w_ff1, w_ff2, g_final):
    assert g_mix.shape[0] == 1, "single-layer block"
    return _layer(x, mem, g_mix[0], w_in[0], b_in[0], ml_conv[0], ml_norm_g[0], cmp_pe[0],
                  cmp_w1[0], cmp_w2[0], g_mem[0], w_mem_kv[0], w_branch[0], w_out[0], g_ffn[0],
                  w_ff1[0], w_ff2[0], g_final)
```

```python
import functools

import numpy as np
import jax
import jax.numpy as jnp
from jax import lax
from jax.experimental import pallas as pl
from jax.experimental.pallas import tpu as pltpu

F32 = jnp.float32
BF16 = jnp.bfloat16

EPS = 1e-6
NEG = -1e30
LOG2_E = 1.4426950408889634
ML_HEADS = 4
ML_DH = 128
ML_WIDTH = ML_HEADS * ML_DH
ML_CONV = 4
ML_CHUNK = 128
ML_BATCH_PER_STEP = 1
NSA_HEADS = 8
NSA_KV = 2
NSA_REP = NSA_HEADS // NSA_KV
NSA_DH = 64
NSA_WIDTH = NSA_HEADS * NSA_DH
NSA_KV_WIDTH = NSA_KV * NSA_DH
CMP_LEN = 32
CMP_STRIDE = 16
CMP_HIDDEN = 256
SEL_LEN = 64
SEL_TOP = 16
WINDOW = 512
Q_BLOCK = 128
FORCE_BONUS = 1e3
SEL_KT = 256
SEL_GROUP = 8
XA_HEADS = 4
XA_DH = 128
XA_WIDTH = XA_HEADS * XA_DH
N_BRANCH = 3

LANES = 128
SUBLANES = 8
VMEM_LIMIT = 56 * 1024 * 1024

TM_PROJ = 256
TM_TAIL = 256
FF_SLAB = 1024

_NT = (((1,), (1,)), ((), ()))
_TN = (((0,), (0,)), ((), ()))


def _dot(a, b):
    return jnp.dot(a, b, preferred_element_type=F32)


def _dot_nt(a, b):
    return lax.dot_general(a, b, _NT, preferred_element_type=F32)


def _resident(shape):
    nd = len(shape)
    return pl.BlockSpec(shape, lambda *_: (0,) * nd, pipeline_mode=pl.Buffered(1))


def _rms(x, g):
    return x * lax.rsqrt(jnp.mean(x * x, axis=-1, keepdims=True) + EPS) * g


_PROJ_COLS = (
    ("qk", 2 * ML_WIDTH, BF16),
    ("o", ML_WIDTH, BF16),
    ("mg", None, BF16),
    ("v", ML_WIDTH, BF16),
    ("nq", NSA_HEADS * LANES, BF16),
    ("xq", XA_WIDTH, BF16),
    ("gcol", LANES, F32),
    ("kc", NSA_KV_WIDTH, F32),
    ("vc", NSA_KV_WIDTH, F32),
    ("ks", NSA_KV_WIDTH, BF16),
    ("kw", NSA_KV_WIDTH, BF16),
    ("vs0", LANES, BF16),
    ("vs1", LANES, BF16),
    ("vw0", LANES, BF16),
    ("vw1", LANES, BF16),
)
_PROJ_ROWS = (
    ("smallT", SUBLANES, F32),
)
_PROJ_CHUNK = 512


def _proj_layout(d_model):
    cols, off = [], 0
    for name, width, dt in _PROJ_COLS:
        width = N_BRANCH * d_model if width is None else width
        cols.append((name, off, width, dt))
        off += width
    rows, roff = [], 0
    for name, r, dt in _PROJ_ROWS:
        rows.append((name, roff, r, dt))
        roff += r
    return cols, off, rows, roff


def _inproj_kernel(cols, rows, tiles_per_b, x_ref, g_ref, w_ref, b_ref, wt_ref, bt_ref, conv_ref,
                   *refs):
    out_refs, xbuf = refs[:-1], refs[-1]
    tm = x_ref.shape[0]
    halo = SUBLANES

    @pl.when(pl.program_id(0) % tiles_per_b == 0)
    def _():
        xbuf[0:halo, :] = jnp.zeros((halo, xbuf.shape[1]), F32)

    u = _rms(x_ref[...], g_ref[...]).astype(BF16)
    col_refs = out_refs[:len(cols)]
    row_refs = out_refs[len(cols):]
    ncols = w_ref.shape[1]
    for c0 in range(0, ncols, _PROJ_CHUNK):
        cw = min(_PROJ_CHUNK, ncols - c0)
        chunk = _dot(u, w_ref[:, c0:c0 + cw]) + b_ref[:, c0:c0 + cw]
        for (name, off, width, dt), o_ref in zip(cols, col_refs):
            lo, hi = max(off, c0), min(off + width, c0 + cw)
            if lo >= hi:
                continue
            acc = chunk[:, lo - c0:hi - c0]
            cs = slice(lo - off, hi - off)
            if name == "qk":
                xbuf[halo:halo + tm, cs] = acc
                y = acc * conv_ref[0:1, cs]
                for j in range(1, ML_CONV):
                    y = y + xbuf[halo - j:halo - j + tm, cs] * conv_ref[j:j + 1, cs]
                xbuf[0:halo, cs] = xbuf[tm:tm + halo, cs]
                acc = y * jax.nn.sigmoid(y)
                if lo - off >= ML_WIDTH:
                    acc = acc * (ML_DH ** -0.5)
            elif name in ("o", "mg"):
                acc = jax.nn.sigmoid(acc)
            o_ref[:, cs] = acc.astype(dt)
    t = _dot_nt(wt_ref[...], u) + bt_ref[...]
    for (name, roff, r, dt), o_ref in zip(rows, row_refs):
        o_ref[0] = t[roff:roff + r, :].astype(dt)


def _split_w_in(w_in, b_in):
    widths = (ML_WIDTH, ML_WIDTH, ML_WIDTH, ML_WIDTH, ML_HEADS, ML_HEADS,
              NSA_WIDTH, NSA_KV_WIDTH, NSA_KV_WIDTH, NSA_KV_WIDTH, NSA_KV_WIDTH, NSA_KV_WIDTH,
              NSA_KV_WIDTH, 3 * NSA_HEADS, XA_WIDTH, w_in.shape[1])
    names = ("ml_q", "ml_k", "ml_v", "ml_o", "ml_i", "ml_f", "ns_q", "ns_kc", "ns_vc", "ns_ks",
             "ns_vs", "ns_kw", "ns_vw", "ns_g", "xa_q", "mg")
    out, off = {}, 0
    for n, wd in zip(names, widths):
        end = w_in.shape[1] if n == "mg" else off + wd
        out[n] = (w_in[:, off:end], b_in[off:end])
        off = end
    return out


def _inproj(x2, g_mix, w_in, b_in, ml_conv, seq):
    n_tok, d = x2.shape
    cols, ncols, rows, nrows = _proj_layout(d)
    p = _split_w_in(w_in, b_in)

    def pad_cols(w, b, width):
        return (jnp.pad(w, ((0, 0), (0, width - w.shape[1]))), jnp.pad(b, (0, width - b.shape[0])))

    wq, bq = p["ns_q"]
    wq = wq.reshape(d, NSA_HEADS, 1, NSA_DH)
    bq = bq.reshape(NSA_HEADS, 1, NSA_DH)
    in_group = jnp.asarray(np.arange(NSA_HEADS)[:, None] // NSA_REP == np.arange(NSA_KV)[None, :],
                           w_in.dtype)[:, :, None]
    wq_slots = wq * in_group
    bq_slots = bq * in_group
    gate_w = jnp.concatenate([p["ml_i"][0], p["ml_f"][0]], axis=1)
    gate_b = jnp.concatenate([p["ml_i"][1], p["ml_f"][1]])
    small_w = jnp.concatenate([gate_w, p["ns_g"][0]], axis=1)
    small_b = jnp.concatenate([gate_b, p["ns_g"][1]])

    def with_ones(wb, g):
        w, b = wb
        w = jnp.pad(w[:, g * NSA_DH:(g + 1) * NSA_DH], ((0, 0), (0, LANES - NSA_DH)))
        b = jnp.concatenate([b[g * NSA_DH:(g + 1) * NSA_DH], jnp.ones((LANES - NSA_DH,), b.dtype)])
        return w, b

    pieces = {
        "qk": (jnp.concatenate([p["ml_q"][0], p["ml_k"][0]], axis=1),
               jnp.concatenate([p["ml_q"][1], p["ml_k"][1]])),
        "o": p["ml_o"], "mg": p["mg"],
        "gcol": pad_cols(small_w, small_b, LANES),
        "kc": p["ns_kc"], "vc": p["ns_vc"], "v": p["ml_v"],
        "nq": (wq_slots.reshape(d, NSA_HEADS * LANES), bq_slots.reshape(NSA_HEADS * LANES)),
        "xq": p["xa_q"], "ks": p["ns_ks"], "kw": p["ns_kw"],
        "vs0": with_ones(p["ns_vs"], 0), "vs1": with_ones(p["ns_vs"], 1),
        "vw0": with_ones(p["ns_vw"], 0), "vw1": with_ones(p["ns_vw"], 1),
    }
    w_cols = jnp.concatenate([pieces[name][0] for name, *_ in cols], axis=1).astype(BF16)
    b_cols = jnp.concatenate([pieces[name][1] for name, *_ in cols])[None, :]
    w_rows = gate_w.T.astype(BF16)
    b_rows = gate_b[:, None]

    tm = TM_PROJ
    tiles_per_b = seq // tm
    out_shape = ([jax.ShapeDtypeStruct((n_tok, width), dt) for _, _, width, dt in cols]
                 + [jax.ShapeDtypeStruct((n_tok // seq, r, seq), dt) for _, _, r, dt in rows])
    out_specs = ([pl.BlockSpec((tm, width), lambda i: (i, 0)) for _, _, width, _ in cols]
                 + [pl.BlockSpec((1, r, tm), lambda i: (i // tiles_per_b, 0, i % tiles_per_b))
                    for _, _, r, _ in rows])
    outs = pl.pallas_call(
        functools.partial(_inproj_kernel, cols, rows, tiles_per_b),
        out_shape=out_shape,
        grid=(n_tok // tm,),
        in_specs=[pl.BlockSpec((tm, d), lambda i: (i, 0)),
                  _resident((1, d)), _resident((d, ncols)), _resident((1, ncols)),
                  _resident((nrows, d)), _resident((nrows, 1)),
                  _resident((ML_CONV, 2 * ML_WIDTH))],
        out_specs=out_specs,
        scratch_shapes=[pltpu.VMEM((tm + SUBLANES, 2 * ML_WIDTH), F32)],
        compiler_params=pltpu.CompilerParams(dimension_semantics=("arbitrary",),
                                             vmem_limit_bytes=VMEM_LIMIT),
        name="inproj",
    )(x2, g_mix[None, :], w_cols, b_cols, w_rows, b_rows, ml_conv)
    names = [c[0] for c in cols] + [r[0] for r in rows]
    return dict(zip(names, outs))


def _mlstm_kernel(qk_ref, v_ref, o_ref, gcol_ref, grow_ref, ng_ref, y_ref, c_scr, m_scr):
    L = ML_CHUNK
    nb = qk_ref.shape[0]

    @pl.when(pl.program_id(1) == 0)
    def _():
        c_scr[...] = jnp.zeros(c_scr.shape, F32)
        m_scr[...] = jnp.zeros(m_scr.shape, F32)

    r_i = lax.broadcasted_iota(jnp.int32, (L, L), 0)
    c_i = lax.broadcasted_iota(jnp.int32, (L, L), 1)
    causal = c_i <= r_i
    tril = causal.astype(F32)
    triu = (r_i <= c_i).astype(F32)
    gcols = [gcol_ref[b] for b in range(nb)]
    grows = [grow_ref[b] for b in range(nb)]
    b_cols = jnp.dot(tril, jax.nn.log_sigmoid(jnp.concatenate(gcols, axis=1)),
                     precision=lax.Precision.HIGHEST, preferred_element_type=F32)
    b_rows = jnp.dot(jax.nn.log_sigmoid(jnp.concatenate(grows, axis=0)), triu,
                     precision=lax.Precision.HIGHEST, preferred_element_type=F32)
    ones_col = jnp.where(lax.broadcasted_iota(jnp.int32, (L, ML_DH), 1) == 0, 1.0, 0.0).astype(BF16)

    chains = [(b, hh) for b in range(nb) for hh in range(ML_HEADS)]

    def operands(b, hh):
        sl = slice(hh * ML_DH, (hh + 1) * ML_DH)
        q = qk_ref[b, :, sl]
        k = qk_ref[b, :, ML_WIDTH + hh * ML_DH:ML_WIDTH + (hh + 1) * ML_DH]
        v1 = jnp.concatenate([v_ref[b, :, sl], ones_col], axis=1)
        return q, k, v1, c_scr[b, hh]

    ops = [operands(b, hh) for b, hh in chains]
    qk_t = [_dot_nt(q, k) for q, k, _, _ in ops]
    inter = [_dot(q, c_prev.astype(BF16)) for q, _, _, c_prev in ops]

    stab, ke_all, s_all = [], [], []
    for i, (b, hh) in enumerate(chains):
        gcol, grow = gcols[b], grows[b]
        li_c = gcol[:, hh:hh + 1]
        b_c = b_cols[:, b * LANES + ML_HEADS + hh:b * LANES + ML_HEADS + hh + 1]
        li_r = grow[hh:hh + 1, :]
        b_r = b_rows[b * SUBLANES + ML_HEADS + hh:b * SUBLANES + ML_HEADS + hh + 1, :]
        g = b_c[L - 1:L, :]
        m_prev = m_scr[b, hh:hh + 1, 0:1]
        log_d = jnp.where(causal, b_c - b_r + li_r, -jnp.inf)
        inter_log = b_c + m_prev
        m_t = jnp.maximum(inter_log, jnp.max(log_d, axis=-1, keepdims=True))
        s_all.append((qk_t[i] * jnp.exp(log_d - m_t)).astype(BF16))
        w_end = g - b_c + li_c
        m_loc = jnp.max(w_end, axis=0, keepdims=True)
        ke_all.append((ops[i][1] * jnp.exp(w_end - m_loc)).astype(BF16))
        m_new = jnp.maximum(g + m_prev, m_loc)
        stab.append((m_t, jnp.exp(inter_log - m_t), jnp.exp(g + m_prev - m_new),
                     jnp.exp(m_loc - m_new), m_new))

    intra = [_dot(s, o[2]) for s, o in zip(s_all, ops)]
    a_c = [lax.dot_general(ke, o[2], _TN, preferred_element_type=F32) for ke, o in zip(ke_all, ops)]

    ys, new_c, new_m = [], [], []
    for i, (b, hh) in enumerate(chains):
        sl = slice(hh * ML_DH, (hh + 1) * ML_DH)
        m_t, sc, a, bb, m_new = stab[i]
        nd = intra[i] + sc * inter[i]
        den = nd[:, ML_DH:ML_DH + 1]
        h = nd[:, 0:ML_DH] / jnp.maximum(jnp.abs(den), jnp.exp(-m_t))
        hn = h * lax.rsqrt(jnp.mean(h * h, axis=-1, keepdims=True) + EPS)
        ys.append((o_ref[b, :, sl] * hn * ng_ref[:, sl]).astype(y_ref.dtype))
        new_c.append(a * ops[i][3] + bb * a_c[i])
        new_m.append(jnp.broadcast_to(m_new, (1, m_scr.shape[2])))

    per_b = lambda xs, axis: [jnp.concatenate(xs[b * ML_HEADS:(b + 1) * ML_HEADS], axis=axis)
                              for b in range(nb)]
    y_ref[...] = jnp.stack(per_b(ys, 1))
    c_scr[...] = jnp.stack([jnp.stack(new_c[b * ML_HEADS:(b + 1) * ML_HEADS]) for b in range(nb)])
    m_scr[:, 0:ML_HEADS, :] = jnp.stack(per_b(new_m, 0))


def _mlstm(pr, ml_norm_g, batch, seq):
    L = ML_CHUNK
    nb = ML_BATCH_PER_STEP
    chunk = lambda b, c: (b, c, 0)
    per_tok = lambda name: pr[name].reshape(batch, seq, pr[name].shape[-1])
    y = pl.pallas_call(
        _mlstm_kernel,
        out_shape=jax.ShapeDtypeStruct((batch, seq, ML_WIDTH), BF16),
        grid=(batch // nb, seq // L),
        in_specs=[pl.BlockSpec((nb, L, 2 * ML_WIDTH), chunk),
                  pl.BlockSpec((nb, L, ML_WIDTH), chunk),
                  pl.BlockSpec((nb, L, ML_WIDTH), chunk),
                  pl.BlockSpec((nb, L, LANES), chunk),
                  pl.BlockSpec((nb, SUBLANES, L), lambda b, c: (b, 0, c)),
                  pl.BlockSpec((1, ML_WIDTH), lambda b, c: (0, 0))],
        out_specs=pl.BlockSpec((nb, L, ML_WIDTH), chunk),
        scratch_shapes=[pltpu.VMEM((nb, ML_HEADS, ML_DH, 2 * ML_DH), F32),
                        pltpu.VMEM((nb, SUBLANES, LANES), F32)],
        compiler_params=pltpu.CompilerParams(dimension_semantics=("parallel", "arbitrary"),
                                             vmem_limit_bytes=VMEM_LIMIT),
        name="mlstm",
    )(per_tok("qk"), per_tok("v"), per_tok("o"), per_tok("gcol"), pr["smallT"],
      ml_norm_g[None, :])
    return y.reshape(batch * seq, ML_WIDTH)


def _gelu_tanh(x):
    return 0.5 * x * (1.0 + jnp.tanh(np.sqrt(2.0 / np.pi) * (x + 0.044715 * (x * x * x))))


def _compress_kernel(k_ref, v_ref, pe_ref, w1_ref, w2k_ref, w2v_ref, kc_ref, vc0_ref, vc1_ref):
    nrow = kc_ref.shape[1]
    S = CMP_STRIDE

    def hidden(x_ref, which):
        lo = hi = None
        for j in range(0, S, 2):
            x = [x_ref[pl.ds(j + d, nrow, stride=S), :] for d in range(2)]
            for half in range(CMP_LEN // S):
                l0 = half * S + j
                xp = jnp.concatenate([(x[d] + pe_ref[which, l0 + d:l0 + d + 1, :]).astype(BF16)
                                      for d in range(2)], axis=1)
                t = _dot(xp, w1_ref[which, l0 // 2])
                if half == 0:
                    lo = t if lo is None else lo + t
                else:
                    hi = t if hi is None else hi + t
        return _gelu_tanh(lo + pltpu.roll(hi, nrow - 1, 0)).astype(BF16)

    kc_ref[0] = _dot(hidden(k_ref, 0), w2k_ref[...]).astype(kc_ref.dtype)
    hv = hidden(v_ref, 1)
    ones_half = lax.broadcasted_iota(jnp.int32, (nrow, LANES), 1) >= NSA_DH
    for g, o_ref in enumerate((vc0_ref, vc1_ref)):
        o_ref[0] = jnp.where(ones_half, 1.0, _dot(hv, w2v_ref[g])).astype(o_ref.dtype)


def _compress(pr, cmp_pe, cmp_w1, cmp_w2, batch, seq):
    nrow = seq // CMP_STRIDE
    hid = NSA_KV * CMP_HIDDEN
    w1 = cmp_w1.astype(BF16).reshape(2, CMP_LEN, NSA_DH, CMP_HIDDEN)
    z1 = jnp.zeros_like(w1)
    w1e = jnp.stack([jnp.concatenate([w1, z1], axis=-1), jnp.concatenate([z1, w1], axis=-1)], axis=2)
    w1e = w1e.reshape(2, CMP_LEN // 2, 2 * NSA_KV_WIDTH, hid)
    w2 = cmp_w2.astype(BF16)
    z2 = jnp.zeros_like(w2[0])
    w2k = jnp.concatenate([jnp.concatenate([w2[0], z2], axis=1),
                           jnp.concatenate([z2, w2[0]], axis=1)], axis=0)
    w2v = jnp.stack([jnp.pad(w2[1], ((g * CMP_HIDDEN, (NSA_KV - 1 - g) * CMP_HIDDEN),
                                     (0, LANES - NSA_DH))) for g in range(NSA_KV)])
    pe = jnp.concatenate([cmp_pe] * NSA_KV, axis=-1)
    tok_blk = pl.BlockSpec((seq, NSA_KV_WIDTH), lambda b: (b, 0))
    out_blk = pl.BlockSpec((1, nrow, LANES), lambda b: (b, 0, 0))
    return pl.pallas_call(
        _compress_kernel,
        out_shape=(jax.ShapeDtypeStruct((batch, nrow, LANES), BF16),) * 3,
        grid=(batch,),
        in_specs=[tok_blk, tok_blk,
                  _resident((2, CMP_LEN, NSA_KV_WIDTH)),
                  _resident((2, CMP_LEN // 2, 2 * NSA_KV_WIDTH, hid)),
                  _resident((hid, NSA_KV_WIDTH)),
                  _resident((NSA_KV, hid, LANES))],
        out_specs=(out_blk, out_blk, out_blk),
        compiler_params=pltpu.CompilerParams(dimension_semantics=("parallel",),
                                             vmem_limit_bytes=VMEM_LIMIT),
        name="compress",
    )(pr["kc"], pr["vc"], pe, w1e, w2k, w2v)


def _alibi_slope(h):
    return float(2.0 ** (-8.0 * (h + 1) / NSA_HEADS))


NSA_ROWS = NSA_HEADS * Q_BLOCK
WIN_KEYS = WINDOW + Q_BLOCK
F_SEL_BLOCK = 0
F_SEL_OFF = SEL_LEN
F_WIN_OFF = SEL_LEN + 1
F_WIN_BLK = SEL_LEN + 2
F_CMP = SEL_LEN + 3


def _tile_heads(x):
    return jnp.concatenate([x] * NSA_HEADS, axis=0)


def _nsa_kernel(nbs, q_ref, g_ref, kc_ref, vc0_ref, vc1_ref, ks_ref, vs0_ref, vs1_ref,
                kw_ref, vw0_ref, vw1_ref, ov_ref, y_ref,
                qe_scr, posf_scr, wposf_scr, cposf_scr, s_scr, mrun_scr, mb_scr, acc_scr):
    QB = Q_BLOCK
    KT = SEL_KT
    HALF = NSA_REP * QB
    t0 = pl.program_id(1) * QB
    ncmp = kc_ref.shape[1]
    lane = lax.broadcasted_iota(jnp.int32, (1, LANES), 1)
    lo_half = lane < NSA_DH

    @pl.when(pl.program_id(1) == 0)
    def _():
        seq = posf_scr.shape[0]
        k = lax.broadcasted_iota(jnp.int32, (seq, LANES), 0)
        c = lax.broadcasted_iota(jnp.int32, (seq, LANES), 1)
        posf = jnp.where(c < SEL_LEN, jnp.where(k // SEL_LEN == c, 1.0, 0.0),
                         jnp.where(c == F_SEL_OFF, (k % SEL_LEN).astype(F32), 0.0))
        posf_scr[...] = posf.astype(BF16)
        k = lax.broadcasted_iota(jnp.int32, (WIN_KEYS, LANES), 0)
        c = lax.broadcasted_iota(jnp.int32, (WIN_KEYS, LANES), 1)
        wposf = jnp.where(c == F_WIN_OFF, (k % SEL_LEN).astype(F32),
                          jnp.where(c == F_WIN_BLK, (k // SEL_LEN).astype(F32), 0.0))
        wposf_scr[...] = wposf.astype(BF16)
        k = lax.broadcasted_iota(jnp.int32, (ncmp, LANES), 0)
        c = lax.broadcasted_iota(jnp.int32, (ncmp, LANES), 1)
        cposf_scr[...] = jnp.where(c == F_CMP, k.astype(F32), 0.0).astype(BF16)

    def const_feat(h):
        s = _alibi_slope(h)
        return jnp.where((lane == F_SEL_OFF) | (lane == F_WIN_OFF), s,
                         jnp.where(lane == F_WIN_BLK, SEL_LEN * s,
                                   jnp.where(lane == F_CMP, CMP_STRIDE * s, 0.0)))

    for h in range(NSA_HEADS):
        rows = slice(h * QB, (h + 1) * QB)
        qe_scr[rows, 0:LANES] = q_ref[:, h * LANES:(h + 1) * LANES] * 0.125
        qe_scr[rows, LANES:2 * LANES] = jnp.broadcast_to(const_feat(h), (QB, LANES)).astype(BF16)

    def normalised(o, guard):
        outs = []
        for pair in range(NSA_HEADS // 2):
            e = o[(2 * pair) * QB:(2 * pair + 1) * QB]
            d = o[(2 * pair + 1) * QB:(2 * pair + 2) * QB]
            num = jnp.where(lo_half, e, pltpu.roll(d, NSA_DH, 1))
            den = jnp.where(lo_half, pltpu.roll(e, NSA_DH, 1), d)
            if guard:
                den = jnp.where(den > 0.0, den, 1.0)
            outs.append(num / den)
        return outs

    ws = pl.multiple_of(jnp.maximum(t0 - WINDOW, 0), QB)
    qs = lax.broadcasted_iota(jnp.int32, (QB, WIN_KEYS), 0)
    kl = lax.broadcasted_iota(jnp.int32, (QB, WIN_KEYS), 1)
    dist_w = (t0 - ws) + qs - kl
    ok_w = (dist_w >= 0) & (dist_w < WINDOW)
    s_w = _dot_nt(qe_scr[...], jnp.concatenate([kw_ref[pl.ds(ws, WIN_KEYS), :], wposf_scr[...]], axis=1))

    q_sub = lax.broadcasted_iota(jnp.int32, (QB, ncmp), 0)
    n_lane = lax.broadcasted_iota(jnp.int32, (QB, ncmp), 1)
    ok_c = n_lane * CMP_STRIDE + (CMP_LEN - 1) <= t0 + q_sub
    s = _dot_nt(qe_scr[...], jnp.concatenate([kc_ref[0], cposf_scr[...]], axis=1))
    s = s + _tile_heads(jnp.where(ok_c, 0.0, NEG))
    e = jnp.exp(s - jnp.max(s, axis=-1, keepdims=True)) * _tile_heads(jnp.where(ok_c, 1.0, 0.0))
    e_bf = e.astype(BF16)
    oc = jnp.concatenate([_dot(e_bf[0:HALF], vc0_ref[0]), _dot(e_bf[HALF:2 * HALF], vc1_ref[0])],
                         axis=0)
    o_cmp = normalised(oc, True)
    l_c = jnp.where(lo_half, pltpu.roll(oc, NSA_DH, 1), oc)
    inv_l = 1.0 / jnp.where(l_c > 0.0, l_c, 1.0)
    p_c = e * jnp.concatenate([inv_l] * (ncmp // LANES), axis=1)

    imp_ts = []
    for g in range(NSA_KV):
        psum = p_c[(g * NSA_REP) * QB:(g * NSA_REP + 1) * QB]
        for r in range(1, NSA_REP):
            psum = psum + p_c[(g * NSA_REP + r) * QB:(g * NSA_REP + r + 1) * QB]
        p_hi = psum.astype(BF16)
        p_lo = (psum - p_hi.astype(F32)).astype(BF16)
        imp = _dot(p_hi, ov_ref[...]) + _dot(p_lo, ov_ref[...])
        imp_ts.append(imp.T[0:nbs, :])

    n_grp = nbs // SUBLANES
    sub8 = lax.broadcasted_iota(jnp.int32, (SUBLANES, QB), 0)
    tpos = t0 + lax.broadcasted_iota(jnp.int32, (1, QB), 1)
    cur = tpos // SEL_LEN
    j_lane = lane.astype(F32)
    for g in range(NSA_KV):
        imp_t = imp_ts[g]
        j_sub = lax.broadcasted_iota(jnp.int32, (nbs, QB), 0)
        forced = (j_sub == 0) | (j_sub == cur) | (j_sub == cur - 1)
        score = jnp.where(j_sub <= cur, imp_t + jnp.where(forced, FORCE_BONUS, 0.0), NEG)
        parts = [score[a * SUBLANES:(a + 1) * SUBLANES, :] for a in range(n_grp)]
        ranks = [jnp.zeros((SUBLANES, QB), F32) for _ in range(n_grp)]
        for i in range(nbs):
            row = jnp.broadcast_to(score[i:i + 1, :], (SUBLANES, QB))
            for a in range(n_grp):
                if a * SUBLANES > i:
                    beats = jnp.where(row >= parts[a], 1.0, 0.0)
                elif (a + 1) * SUBLANES - 1 < i:
                    beats = jnp.where(row > parts[a], 1.0, 0.0)
                else:
                    beats = jnp.where(sub8 + a * SUBLANES > i,
                                      jnp.where(row >= parts[a], 1.0, 0.0),
                                      jnp.where(row > parts[a], 1.0, 0.0))
                ranks[a] = ranks[a] + beats
        sel_t = [jnp.where(rk < float(min(SEL_TOP, nbs)), 0.0, NEG) for rk in ranks]
        if nbs < LANES:
            sel_t.append(jnp.zeros((LANES - nbs, QB), F32))
        sel_bias = jnp.concatenate(sel_t, axis=0).T
        for r in range(NSA_REP):
            h = g * NSA_REP + r
            feat = jnp.where(lo_half, sel_bias + (_alibi_slope(h) * SEL_LEN) * j_lane, const_feat(h))
            qe_scr[h * QB:(h + 1) * QB, LANES:2 * LANES] = feat.astype(BF16)

    s_w = s_w + _tile_heads(jnp.where(ok_w, 0.0, NEG))
    p_w = jnp.exp(s_w - jnp.max(s_w, axis=-1, keepdims=True)).astype(BF16)
    o_win = normalised(jnp.concatenate([_dot(p_w[0:HALF], vw0_ref[pl.ds(ws, WIN_KEYS), :]),
                                        _dot(p_w[HALF:2 * HALF], vw1_ref[pl.ds(ws, WIN_KEYS), :])],
                                       axis=0), False)

    last_tile = t0 // KT
    mrun_scr[...] = jnp.full(mrun_scr.shape, NEG, F32)

    U = SEL_GROUP

    def score_group(kt0, count, causal):
        k0 = pl.multiple_of(kt0 * KT, KT)
        kx = jnp.concatenate([ks_ref[pl.ds(k0, count * KT), :],
                              posf_scr[pl.ds(k0, count * KT), :]], axis=1)
        s = _dot_nt(qe_scr[...], kx)
        tile_max = None
        for j in range(count):
            sj = s[:, j * KT:(j + 1) * KT] * LOG2_E
            if causal and j == count - 1:
                qs = lax.broadcasted_iota(jnp.int32, (QB, KT), 0)
                kl = lax.broadcasted_iota(jnp.int32, (QB, KT), 1)
                sj = sj + _tile_heads(jnp.where(k0 + j * KT + kl <= t0 + qs, 0.0, NEG))
            s_scr[kt0 + j] = sj
            for c0 in range(0, KT, LANES):
                part = sj[:, c0:c0 + LANES]
                tile_max = part if tile_max is None else jnp.maximum(tile_max, part)
        mrun_scr[...] = jnp.maximum(mrun_scr[...], tile_max)

    def score_full_group(i, carry):
        score_group(i * U, U, False)
        return carry

    lax.fori_loop(0, last_tile // U, score_full_group, 0)
    for r in range(U):
        @pl.when(last_tile % U == r)
        def _(r=r):
            score_group(last_tile - r, r + 1, True)

    mb_scr[...] = jnp.broadcast_to(jnp.max(mrun_scr[...], axis=-1, keepdims=True), mb_scr.shape)
    acc_scr[...] = jnp.zeros(acc_scr.shape, F32)

    def value_group(kt0, count):
        k0 = pl.multiple_of(kt0 * KT, KT)
        mb = mb_scr[...]
        mbw = jnp.concatenate([mb] * (KT // LANES), axis=1)
        p = jnp.concatenate([jnp.exp2(s_scr[kt0 + j] - mbw).astype(BF16) for j in range(count)],
                            axis=1)
        acc_scr[0:HALF, :] += _dot(p[0:HALF], vs0_ref[pl.ds(k0, count * KT), :])
        acc_scr[HALF:2 * HALF, :] += _dot(p[HALF:2 * HALF], vs1_ref[pl.ds(k0, count * KT), :])

    def value_full_group(i, carry):
        value_group(i * U, U)
        return carry

    n_tiles = last_tile + 1
    lax.fori_loop(0, n_tiles // U, value_full_group, 0)
    for r in range(1, U):
        @pl.when(n_tiles % U == r)
        def _(r=r):
            value_group(n_tiles - r, r)

    o_sel = normalised(acc_scr[...], False)

    gate = jax.nn.sigmoid(g_ref[...])
    for pair in range(NSA_HEADS // 2):
        y = None
        for b, o in enumerate((o_cmp, o_sel, o_win)):
            c_e = SUBLANES + 3 * (2 * pair) + b
            c_o = SUBLANES + 3 * (2 * pair + 1) + b
            term = jnp.where(lo_half, gate[:, c_e:c_e + 1], gate[:, c_o:c_o + 1]) * o[pair]
            y = term if y is None else y + term
        y_ref[:, pair * LANES:(pair + 1) * LANES] = y.astype(y_ref.dtype)


def _nsa(pr, kc, vc0, vc1, batch, seq):
    QB = Q_BLOCK
    nq = seq // QB
    nbs = seq // SEL_LEN
    ncmp = kc.shape[1]
    cs = np.arange(ncmp) * CMP_STRIDE
    js = np.arange(nbs) * SEL_LEN
    ov = np.zeros((ncmp, LANES), np.float32)
    ov[:, :nbs] = (cs[:, None] < js[None, :] + SEL_LEN) & (cs[:, None] + CMP_LEN > js[None, :])
    tok = lambda b, i: (b * nq + i, 0)
    per_b = lambda b, i: (b, 0)
    per_b3 = lambda b, i: (b, 0, 0)
    kv_spec = pl.BlockSpec((seq, LANES), per_b)
    cmp_spec = pl.BlockSpec((1, ncmp, LANES), per_b3)
    return pl.pallas_call(
        functools.partial(_nsa_kernel, nbs),
        out_shape=jax.ShapeDtypeStruct((batch * seq, NSA_WIDTH), BF16),
        grid=(batch, nq),
        in_specs=[pl.BlockSpec((QB, NSA_HEADS * LANES), tok),
                  pl.BlockSpec((QB, LANES), tok),
                  cmp_spec, cmp_spec, cmp_spec,
                  kv_spec, kv_spec, kv_spec, kv_spec, kv_spec, kv_spec,
                  pl.BlockSpec((ncmp, LANES), lambda b, i: (0, 0))],
        out_specs=pl.BlockSpec((QB, NSA_WIDTH), tok),
        scratch_shapes=[pltpu.VMEM((NSA_ROWS, 2 * LANES), BF16),
                        pltpu.VMEM((seq, LANES), BF16),
                        pltpu.VMEM((WIN_KEYS, LANES), BF16),
                        pltpu.VMEM((ncmp, LANES), BF16),
                        pltpu.VMEM((seq // SEL_KT, NSA_ROWS, SEL_KT), F32),
                        pltpu.VMEM((NSA_ROWS, LANES), F32),
                        pltpu.VMEM((NSA_ROWS, LANES), F32),
                        pltpu.VMEM((NSA_ROWS, LANES), F32)],
        compiler_params=pltpu.CompilerParams(dimension_semantics=("parallel", "arbitrary"),
                                             vmem_limit_bytes=VMEM_LIMIT),
        name="nsa",
    )(pr["nq"], pr["gcol"], kc, vc0, vc1, pr["ks"], pr["vs0"], pr["vs1"],
      pr["kw"], pr["vw0"], pr["vw1"], jnp.asarray(ov, BF16))


def _memkv_kernel(mem_ref, g_ref, w_ref, kv_ref):
    u = _rms(mem_ref[0], g_ref[...]).astype(BF16)
    kv_ref[0] = _dot(u, w_ref[...]).astype(kv_ref.dtype)


def _memkv(mem, g_mem, w_mem_kv):
    batch, n_mem, d = mem.shape
    return pl.pallas_call(
        _memkv_kernel,
        out_shape=jax.ShapeDtypeStruct((batch, n_mem, 2 * XA_WIDTH), BF16),
        grid=(batch,),
        in_specs=[pl.BlockSpec((1, n_mem, d), lambda b: (b, 0, 0)),
                  _resident((1, d)), _resident((d, 2 * XA_WIDTH))],
        out_specs=pl.BlockSpec((1, n_mem, 2 * XA_WIDTH), lambda b: (b, 0, 0)),
        compiler_params=pltpu.CompilerParams(dimension_semantics=("parallel",),
                                             vmem_limit_bytes=VMEM_LIMIT),
        name="memkv",
    )(mem, g_mem[None, :], w_mem_kv.astype(BF16))


def _tail_kernel(x_ref, yml_ref, ynsa_ref, xq_ref, kv_ref, mg_ref, wb_ref, wo_ref, gf_ref,
                 w1_ref, w2_ref, gl_ref, out_ref):
    d = x_ref.shape[1]
    kv = kv_ref[0]
    heads = [slice(hh * XA_DH, (hh + 1) * XA_DH) for hh in range(XA_HEADS)]
    scores = [_dot_nt(xq_ref[:, sl], kv[:, sl]) * (XA_DH ** -0.5) for sl in heads]
    ups = [_dot(yml_ref[...], wb_ref[0]), _dot(ynsa_ref[...], wb_ref[1])]
    y_xa = []
    for hh, s in enumerate(scores):
        p = jnp.exp(s - jnp.max(s, axis=-1, keepdims=True))
        l = jnp.sum(p, axis=-1, keepdims=True)
        y_xa.append((_dot(p.astype(BF16), kv[:, XA_WIDTH + hh * XA_DH:XA_WIDTH + (hh + 1) * XA_DH])
                     / l).astype(BF16))
    ups.append(_dot(jnp.concatenate(y_xa, axis=-1), wb_ref[2]))
    merged = None
    for j in range(N_BRANCH):
        term = mg_ref[:, j * d:(j + 1) * d] * ups[j]
        merged = term if merged is None else merged + term
    h = x_ref[...] + _dot(merged.astype(BF16), wo_ref[...])
    u = _rms(h, gf_ref[...]).astype(BF16)
    slabs = list(range(0, w1_ref.shape[1], FF_SLAB))
    up = lambda c0: _dot(u, w1_ref[:, c0:c0 + FF_SLAB])
    acc = h
    nxt = up(slabs[0])
    for i, c0 in enumerate(slabs):
        cur = nxt
        if i + 1 < len(slabs):
            nxt = up(slabs[i + 1])
        a = jnp.maximum(cur, 0.0)
        acc = acc + _dot((a * a).astype(BF16), w2_ref[c0:c0 + FF_SLAB, :])
    out_ref[...] = _rms(acc, gl_ref[...])


def _tail(x2, y_ml, y_nsa, pr, mem_kv, w_branch, w_out, g_ffn, w_ff1, w_ff2, g_final, seq):
    n_tok, d = x2.shape
    tm = TM_TAIL
    d_ff = w_ff1.shape[1]
    n_mem = mem_kv.shape[1]
    tiles_per_b = seq // tm
    row = lambda i: (i, 0)
    return pl.pallas_call(
        _tail_kernel,
        out_shape=jax.ShapeDtypeStruct((n_tok, d), F32),
        grid=(n_tok // tm,),
        in_specs=[pl.BlockSpec((tm, d), row),
                  pl.BlockSpec((tm, ML_WIDTH), row),
                  pl.BlockSpec((tm, NSA_WIDTH), row),
                  pl.BlockSpec((tm, XA_WIDTH), row),
                  pl.BlockSpec((1, n_mem, 2 * XA_WIDTH), lambda i: (i // tiles_per_b, 0, 0)),
                  pl.BlockSpec((tm, N_BRANCH * d), row),
                  _resident((N_BRANCH, ML_WIDTH, d)), _resident((d, d)), _resident((1, d)),
                  _resident((d, d_ff)), _resident((d_ff, d)), _resident((1, d))],
        out_specs=pl.BlockSpec((tm, d), row),
        compiler_params=pltpu.CompilerParams(dimension_semantics=("parallel",),
                                             vmem_limit_bytes=VMEM_LIMIT),
        name="tail",
    )(x2, y_ml, y_nsa, pr["xq"], mem_kv, pr["mg"], w_branch.astype(BF16), w_out.astype(BF16),
      g_ffn[None, :], w_ff1.astype(BF16), w_ff2.astype(BF16), g_final[None, :])


def _layer(x, mem, g_mix, w_in, b_in, ml_conv, ml_norm_g, cmp_pe, cmp_w1, cmp_w2, g_mem, w_mem_kv,
           w_branch, w_out, g_ffn, w_ff1, w_ff2, g_final):
    batch, seq, d = x.shape
    x2 = x.reshape(batch * seq, d)
    pr = _inproj(x2, g_mix, w_in, b_in, ml_conv, seq)
    y_ml = _mlstm(pr, ml_norm_g, batch, seq)
    kc, vc0, vc1 = _compress(pr, cmp_pe, cmp_w1, cmp_w2, batch, seq)
    y_nsa = _nsa(pr, kc, vc0, vc1, batch, seq)
    mem_kv = _memkv(mem, g_mem, w_mem_kv)
    out = _tail(x2, y_ml, y_nsa, pr, mem_kv, w_branch, w_out, g_ffn, w_ff1, w_ff2, g_final, seq)
    return out.reshape(batch, seq, d)


def kernel(x, mem, g_mix, w_in, b_in, ml_conv, ml_norm_g, cmp_pe, cmp_w1, cmp_w2, g_mem, w_mem_kv,
           w_branch, w_out, g_ffn, w_ff1, w_ff2, g_final):
    assert g_mix.shape[0] == 1, "single-layer block"
    return _layer(x, mem, g_mix[0], w_in[0], b_in[0], ml_conv[0], ml_norm_g[0], cmp_pe[0],
                  cmp_w1[0], cmp_w2[0], g_mem[0], w_mem_kv[0], w_branch[0], w_out[0], g_ffn[0],
                  w_ff1[0], w_ff2[0], g_final)
```

```python
import functools

import numpy as np
import jax
import jax.numpy as jnp
from jax import lax
from jax.experimental import pallas as pl
from jax.experimental.pallas import tpu as pltpu

F32 = jnp.float32
BF16 = jnp.bfloat16

EPS = 1e-6
NEG = -1e30
LOG2_E = 1.4426950408889634
ML_HEADS = 4
ML_DH = 128
ML_WIDTH = ML_HEADS * ML_DH
ML_CONV = 4
ML_CHUNK = 128
NSA_HEADS = 8
NSA_KV = 2
NSA_REP = NSA_HEADS // NSA_KV
NSA_DH = 64
NSA_WIDTH = NSA_HEADS * NSA_DH
NSA_KV_WIDTH = NSA_KV * NSA_DH
CMP_LEN = 32
CMP_STRIDE = 16
CMP_HIDDEN = 256
SEL_LEN = 64
SEL_TOP = 16
WINDOW = 512
Q_BLOCK = 128
FORCE_BONUS = 1e3
SEL_KT = 256
SEL_GROUP = 8
XA_HEADS = 4
XA_DH = 128
XA_WIDTH = XA_HEADS * XA_DH
N_BRANCH = 3

LANES = 128
SUBLANES = 8
VMEM_LIMIT = 56 * 1024 * 1024

TM_PROJ = 256
TM_TAIL = 256
FF_SLAB = 1024

_NT = (((1,), (1,)), ((), ()))
_TN = (((0,), (0,)), ((), ()))


def _dot(a, b):
    return jnp.dot(a, b, preferred_element_type=F32)


def _dot_nt(a, b):
    return lax.dot_general(a, b, _NT, preferred_element_type=F32)


def _resident(shape):
    nd = len(shape)
    return pl.BlockSpec(shape, lambda *_: (0,) * nd, pipeline_mode=pl.Buffered(1))


def _rms(x, g):
    return x * lax.rsqrt(jnp.mean(x * x, axis=-1, keepdims=True) + EPS) * g


_PROJ_COLS = (
    ("qk", 2 * ML_WIDTH, BF16),
    ("v", ML_WIDTH, BF16),
    ("o", ML_WIDTH, BF16),
    ("gcol", LANES, F32),
    ("kc", NSA_KV_WIDTH, F32),
    ("vc", NSA_KV_WIDTH, F32),
    ("ks", NSA_KV_WIDTH, BF16),
    ("kw", NSA_KV_WIDTH, BF16),
    ("vs0", LANES, BF16),
    ("vs1", LANES, BF16),
    ("vw0", LANES, BF16),
    ("vw1", LANES, BF16),
    ("nq", NSA_HEADS * LANES, BF16),
    ("xq", XA_WIDTH, BF16),
    ("mg", None, BF16),
)
_PROJ_MLSTM_ONLY = ("qk", "v", "o")
_PROJ_CHUNK = 512


def _proj_layout(d_model):
    cols, off = [], 0
    for name, width, dt in _PROJ_COLS:
        width = N_BRANCH * d_model if width is None else width
        cols.append((name, off, width, dt))
        off += width
    return cols, off


def _inproj_kernel(cols, tiles_per_b, x_ref, g_ref, w_ref, b_ref, wt_ref, bt_ref, conv_ref, ng_ref,
                   *refs):
    out_names = [c[0] for c in cols if c[0] not in _PROJ_MLSTM_ONLY]
    out_refs = dict(zip(out_names, refs[:len(out_names)]))
    yml_ref, xbuf, c_scr, m_scr = refs[len(out_names):]
    tm = x_ref.shape[0]
    halo = SUBLANES

    @pl.when(pl.program_id(0) % tiles_per_b == 0)
    def _():
        xbuf[0:halo, :] = jnp.zeros((halo, xbuf.shape[1]), F32)
        c_scr[...] = jnp.zeros(c_scr.shape, F32)
        m_scr[...] = jnp.zeros(m_scr.shape, F32)

    u = _rms(x_ref[...], g_ref[...]).astype(BF16)
    grow = _dot_nt(wt_ref[...], u) + bt_ref[...]
    kept = {name: [] for name in _PROJ_MLSTM_ONLY + ("gcol",)}
    mlstm_ready = max(off + width for name, off, width, _ in cols if name in kept)
    ncols = w_ref.shape[1]
    for c0 in range(0, ncols, _PROJ_CHUNK):
        cw = min(_PROJ_CHUNK, ncols - c0)
        chunk = _dot(u, w_ref[:, c0:c0 + cw]) + b_ref[:, c0:c0 + cw]
        for name, off, width, dt in cols:
            lo, hi = max(off, c0), min(off + width, c0 + cw)
            if lo >= hi:
                continue
            acc = chunk[:, lo - c0:hi - c0]
            cs = slice(lo - off, hi - off)
            if name == "qk":
                xbuf[halo:halo + tm, cs] = acc
                y = acc * conv_ref[0:1, cs]
                for j in range(1, ML_CONV):
                    y = y + xbuf[halo - j:halo - j + tm, cs] * conv_ref[j:j + 1, cs]
                xbuf[0:halo, cs] = xbuf[tm:tm + halo, cs]
                acc = y * jax.nn.sigmoid(y)
                if lo - off >= ML_WIDTH:
                    acc = acc * (ML_DH ** -0.5)
            elif name in ("o", "mg"):
                acc = jax.nn.sigmoid(acc)
            acc = acc.astype(dt)
            if name in kept:
                kept[name].append(acc)
            if name in out_refs:
                out_refs[name][:, cs] = acc
        if c0 < mlstm_ready <= c0 + cw:
            qk, v, og = (jnp.concatenate(kept[n], axis=1) for n in _PROJ_MLSTM_ONLY)
            gcol = kept["gcol"][0]
            c_state = [c_scr[hh] for hh in range(ML_HEADS)]
            m_state = [m_scr[hh:hh + 1, 0:1] for hh in range(ML_HEADS)]
            for r0 in range(0, tm, ML_CHUNK):
                rows = slice(r0, r0 + ML_CHUNK)
                y_ml, c_state, m_state = _mlstm_chunk(qk[rows], v[rows], og[rows], gcol[rows],
                                                      grow[:, rows], ng_ref[...], c_state, m_state)
                yml_ref[rows, :] = y_ml
            for hh in range(ML_HEADS):
                c_scr[hh] = c_state[hh]
                m_scr[hh:hh + 1, :] = jnp.broadcast_to(m_state[hh], (1, m_scr.shape[1]))


def _split_w_in(w_in, b_in):
    widths = (ML_WIDTH, ML_WIDTH, ML_WIDTH, ML_WIDTH, ML_HEADS, ML_HEADS,
              NSA_WIDTH, NSA_KV_WIDTH, NSA_KV_WIDTH, NSA_KV_WIDTH, NSA_KV_WIDTH, NSA_KV_WIDTH,
              NSA_KV_WIDTH, 3 * NSA_HEADS, XA_WIDTH, w_in.shape[1])
    names = ("ml_q", "ml_k", "ml_v", "ml_o", "ml_i", "ml_f", "ns_q", "ns_kc", "ns_vc", "ns_ks",
             "ns_vs", "ns_kw", "ns_vw", "ns_g", "xa_q", "mg")
    out, off = {}, 0
    for n, wd in zip(names, widths):
        end = w_in.shape[1] if n == "mg" else off + wd
        out[n] = (w_in[:, off:end], b_in[off:end])
        off = end
    return out


def _inproj(x2, g_mix, w_in, b_in, ml_conv, ml_norm_g, seq):
    n_tok, d = x2.shape
    cols, ncols = _proj_layout(d)
    p = _split_w_in(w_in, b_in)

    def pad_cols(w, b, width):
        return (jnp.pad(w, ((0, 0), (0, width - w.shape[1]))), jnp.pad(b, (0, width - b.shape[0])))

    wq, bq = p["ns_q"]
    wq = wq.reshape(d, NSA_HEADS, 1, NSA_DH)
    bq = bq.reshape(NSA_HEADS, 1, NSA_DH)
    in_group = jnp.asarray(np.arange(NSA_HEADS)[:, None] // NSA_REP == np.arange(NSA_KV)[None, :],
                           w_in.dtype)[:, :, None]
    wq_slots = wq * in_group
    bq_slots = bq * in_group
    gate_w = jnp.concatenate([p["ml_i"][0], p["ml_f"][0]], axis=1)
    gate_b = jnp.concatenate([p["ml_i"][1], p["ml_f"][1]])
    small_w = jnp.concatenate([gate_w, p["ns_g"][0]], axis=1)
    small_b = jnp.concatenate([gate_b, p["ns_g"][1]])

    def with_ones(wb, g):
        w, b = wb
        w = jnp.pad(w[:, g * NSA_DH:(g + 1) * NSA_DH], ((0, 0), (0, LANES - NSA_DH)))
        b = jnp.concatenate([b[g * NSA_DH:(g + 1) * NSA_DH], jnp.ones((LANES - NSA_DH,), b.dtype)])
        return w, b

    pieces = {
        "qk": (jnp.concatenate([p["ml_q"][0], p["ml_k"][0]], axis=1),
               jnp.concatenate([p["ml_q"][1], p["ml_k"][1]])),
        "o": p["ml_o"], "mg": p["mg"],
        "gcol": pad_cols(small_w, small_b, LANES),
        "kc": p["ns_kc"], "vc": p["ns_vc"], "v": p["ml_v"],
        "nq": (wq_slots.reshape(d, NSA_HEADS * LANES), bq_slots.reshape(NSA_HEADS * LANES)),
        "xq": p["xa_q"], "ks": p["ns_ks"], "kw": p["ns_kw"],
        "vs0": with_ones(p["ns_vs"], 0), "vs1": with_ones(p["ns_vs"], 1),
        "vw0": with_ones(p["ns_vw"], 0), "vw1": with_ones(p["ns_vw"], 1),
    }
    w_cols = jnp.concatenate([pieces[name][0] for name, *_ in cols], axis=1).astype(BF16)
    b_cols = jnp.concatenate([pieces[name][1] for name, *_ in cols])[None, :]
    w_rows = gate_w.T.astype(BF16)
    b_rows = gate_b[:, None]

    tm = TM_PROJ
    tiles_per_b = seq // tm
    outs_hbm = [c for c in cols if c[0] not in _PROJ_MLSTM_ONLY] + [("y_ml", None, ML_WIDTH, BF16)]
    outs = pl.pallas_call(
        functools.partial(_inproj_kernel, cols, tiles_per_b),
        out_shape=[jax.ShapeDtypeStruct((n_tok, width), dt) for _, _, width, dt in outs_hbm],
        grid=(n_tok // tm,),
        in_specs=[pl.BlockSpec((tm, d), lambda i: (i, 0)),
                  _resident((1, d)), _resident((d, ncols)), _resident((1, ncols)),
                  _resident((SUBLANES, d)), _resident((SUBLANES, 1)),
                  _resident((ML_CONV, 2 * ML_WIDTH)), _resident((1, ML_WIDTH))],
        out_specs=[pl.BlockSpec((tm, width), lambda i: (i, 0)) for _, _, width, _ in outs_hbm],
        scratch_shapes=[pltpu.VMEM((tm + SUBLANES, 2 * ML_WIDTH), F32),
                        pltpu.VMEM((ML_HEADS, ML_DH, 2 * ML_DH), F32),
                        pltpu.VMEM((SUBLANES, LANES), F32)],
        compiler_params=pltpu.CompilerParams(dimension_semantics=("arbitrary",),
                                             vmem_limit_bytes=VMEM_LIMIT),
        name="inproj",
    )(x2, g_mix[None, :], w_cols, b_cols, w_rows, b_rows, ml_conv, ml_norm_g[None, :])
    return dict(zip([c[0] for c in outs_hbm], outs))


def _mlstm_chunk(qk, v, og, gcol, grow, ng, c_prev, m_prev):
    L = ML_CHUNK
    r_i = lax.broadcasted_iota(jnp.int32, (L, L), 0)
    c_i = lax.broadcasted_iota(jnp.int32, (L, L), 1)
    causal = c_i <= r_i
    tril = causal.astype(F32)
    triu = (r_i <= c_i).astype(F32)
    b_col = jnp.dot(tril, jax.nn.log_sigmoid(gcol), precision=lax.Precision.HIGHEST,
                    preferred_element_type=F32)
    b_row = jnp.dot(jax.nn.log_sigmoid(grow), triu, precision=lax.Precision.HIGHEST,
                    preferred_element_type=F32)
    ones_col = jnp.where(lax.broadcasted_iota(jnp.int32, (L, ML_DH), 1) == 0, 1.0, 0.0).astype(BF16)

    def operands(hh):
        sl = slice(hh * ML_DH, (hh + 1) * ML_DH)
        k = qk[:, ML_WIDTH + hh * ML_DH:ML_WIDTH + (hh + 1) * ML_DH]
        return qk[:, sl], k, jnp.concatenate([v[:, sl], ones_col], axis=1), c_prev[hh]

    ops = [operands(hh) for hh in range(ML_HEADS)]
    qk_t = [_dot_nt(q, k) for q, k, _, _ in ops]
    inter = [_dot(q, c.astype(BF16)) for q, _, _, c in ops]

    stab, ke_all, s_all = [], [], []
    for hh in range(ML_HEADS):
        li_c = gcol[:, hh:hh + 1]
        b_c = b_col[:, ML_HEADS + hh:ML_HEADS + hh + 1]
        li_r = grow[hh:hh + 1, :]
        b_r = b_row[ML_HEADS + hh:ML_HEADS + hh + 1, :]
        g = b_c[L - 1:L, :]
        log_d = jnp.where(causal, b_c - b_r + li_r, -jnp.inf)
        inter_log = b_c + m_prev[hh]
        m_t = jnp.maximum(inter_log, jnp.max(log_d, axis=-1, keepdims=True))
        s_all.append((qk_t[hh] * jnp.exp(log_d - m_t)).astype(BF16))
        w_end = g - b_c + li_c
        m_loc = jnp.max(w_end, axis=0, keepdims=True)
        ke_all.append((ops[hh][1] * jnp.exp(w_end - m_loc)).astype(BF16))
        m_new = jnp.maximum(g + m_prev[hh], m_loc)
        stab.append((m_t, jnp.exp(inter_log - m_t), jnp.exp(g + m_prev[hh] - m_new),
                     jnp.exp(m_loc - m_new), m_new))

    intra = [_dot(s, o[2]) for s, o in zip(s_all, ops)]
    a_c = [lax.dot_general(ke, o[2], _TN, preferred_element_type=F32) for ke, o in zip(ke_all, ops)]

    ys, new_c, new_m = [], [], []
    for hh in range(ML_HEADS):
        sl = slice(hh * ML_DH, (hh + 1) * ML_DH)
        m_t, sc, a, bb, m_new = stab[hh]
        nd = intra[hh] + sc * inter[hh]
        den = nd[:, ML_DH:ML_DH + 1]
        h = nd[:, 0:ML_DH] / jnp.maximum(jnp.abs(den), jnp.exp(-m_t))
        hn = h * lax.rsqrt(jnp.mean(h * h, axis=-1, keepdims=True) + EPS)
        ys.append((og[:, sl] * hn * ng[:, sl]).astype(BF16))
        new_c.append(a * ops[hh][3] + bb * a_c[hh])
        new_m.append(m_new)
    return jnp.concatenate(ys, axis=1), new_c, new_m


def _gelu_tanh(x):
    return 0.5 * x * (1.0 + jnp.tanh(np.sqrt(2.0 / np.pi) * (x + 0.044715 * (x * x * x))))


def _compress_kernel(k_ref, v_ref, pe_ref, w1_ref, w2k_ref, w2v_ref, kc_ref, vc0_ref, vc1_ref):
    nrow = kc_ref.shape[1]
    S = CMP_STRIDE

    def hidden(x_ref, which):
        lo = hi = None
        for j in range(0, S, 2):
            x = [x_ref[pl.ds(j + d, nrow, stride=S), :] for d in range(2)]
            for half in range(CMP_LEN // S):
                l0 = half * S + j
                xp = jnp.concatenate([(x[d] + pe_ref[which, l0 + d:l0 + d + 1, :]).astype(BF16)
                                      for d in range(2)], axis=1)
                t = _dot(xp, w1_ref[which, l0 // 2])
                if half == 0:
                    lo = t if lo is None else lo + t
                else:
                    hi = t if hi is None else hi + t
        return _gelu_tanh(lo + pltpu.roll(hi, nrow - 1, 0)).astype(BF16)

    kc_ref[0] = _dot(hidden(k_ref, 0), w2k_ref[...]).astype(kc_ref.dtype)
    hv = hidden(v_ref, 1)
    ones_half = lax.broadcasted_iota(jnp.int32, (nrow, LANES), 1) >= NSA_DH
    for g, o_ref in enumerate((vc0_ref, vc1_ref)):
        o_ref[0] = jnp.where(ones_half, 1.0, _dot(hv, w2v_ref[g])).astype(o_ref.dtype)


def _compress(pr, cmp_pe, cmp_w1, cmp_w2, batch, seq):
    nrow = seq // CMP_STRIDE
    hid = NSA_KV * CMP_HIDDEN
    w1 = cmp_w1.astype(BF16).reshape(2, CMP_LEN, NSA_DH, CMP_HIDDEN)
    z1 = jnp.zeros_like(w1)
    w1e = jnp.stack([jnp.concatenate([w1, z1], axis=-1), jnp.concatenate([z1, w1], axis=-1)], axis=2)
    w1e = w1e.reshape(2, CMP_LEN // 2, 2 * NSA_KV_WIDTH, hid)
    w2 = cmp_w2.astype(BF16)
    z2 = jnp.zeros_like(w2[0])
    w2k = jnp.concatenate([jnp.concatenate([w2[0], z2], axis=1),
                           jnp.concatenate([z2, w2[0]], axis=1)], axis=0)
    w2v = jnp.stack([jnp.pad(w2[1], ((g * CMP_HIDDEN, (NSA_KV - 1 - g) * CMP_HIDDEN),
                                     (0, LANES - NSA_DH))) for g in range(NSA_KV)])
    pe = jnp.concatenate([cmp_pe] * NSA_KV, axis=-1)
    tok_blk = pl.BlockSpec((seq, NSA_KV_WIDTH), lambda b: (b, 0))
    out_blk = pl.BlockSpec((1, nrow, LANES), lambda b: (b, 0, 0))
    return pl.pallas_call(
        _compress_kernel,
        out_shape=(jax.ShapeDtypeStruct((batch, nrow, LANES), BF16),) * 3,
        grid=(batch,),
        in_specs=[tok_blk, tok_blk,
                  _resident((2, CMP_LEN, NSA_KV_WIDTH)),
                  _resident((2, CMP_LEN // 2, 2 * NSA_KV_WIDTH, hid)),
                  _resident((hid, NSA_KV_WIDTH)),
                  _resident((NSA_KV, hid, LANES))],
        out_specs=(out_blk, out_blk, out_blk),
        compiler_params=pltpu.CompilerParams(dimension_semantics=("parallel",),
                                             vmem_limit_bytes=VMEM_LIMIT),
        name="compress",
    )(pr["kc"], pr["vc"], pe, w1e, w2k, w2v)


def _alibi_slope(h):
    return float(2.0 ** (-8.0 * (h + 1) / NSA_HEADS))


NSA_ROWS = NSA_HEADS * Q_BLOCK
WIN_KEYS = WINDOW + Q_BLOCK
F_SEL_BLOCK = 0
F_SEL_OFF = SEL_LEN
F_WIN_OFF = SEL_LEN + 1
F_WIN_BLK = SEL_LEN + 2
F_CMP = SEL_LEN + 3


def _tile_heads(x):
    return jnp.concatenate([x] * NSA_HEADS, axis=0)


def _nsa_kernel(nbs, q_ref, g_ref, kc_ref, vc0_ref, vc1_ref, ks_ref, vs0_ref, vs1_ref,
                kw_ref, vw0_ref, vw1_ref, ov_ref, y_ref,
                qe_scr, posf_scr, wposf_scr, cposf_scr, s_scr, mrun_scr, mb_scr, acc_scr):
    QB = Q_BLOCK
    KT = SEL_KT
    HALF = NSA_REP * QB
    t0 = pl.program_id(1) * QB
    ncmp = kc_ref.shape[1]
    lane = lax.broadcasted_iota(jnp.int32, (1, LANES), 1)
    lo_half = lane < NSA_DH

    @pl.when(pl.program_id(1) == 0)
    def _():
        seq = posf_scr.shape[0]
        k = lax.broadcasted_iota(jnp.int32, (seq, LANES), 0)
        c = lax.broadcasted_iota(jnp.int32, (seq, LANES), 1)
        posf = jnp.where(c < SEL_LEN, jnp.where(k // SEL_LEN == c, 1.0, 0.0),
                         jnp.where(c == F_SEL_OFF, (k % SEL_LEN).astype(F32), 0.0))
        posf_scr[...] = posf.astype(BF16)
        k = lax.broadcasted_iota(jnp.int32, (WIN_KEYS, LANES), 0)
        c = lax.broadcasted_iota(jnp.int32, (WIN_KEYS, LANES), 1)
        wposf = jnp.where(c == F_WIN_OFF, (k % SEL_LEN).astype(F32),
                          jnp.where(c == F_WIN_BLK, (k // SEL_LEN).astype(F32), 0.0))
        wposf_scr[...] = wposf.astype(BF16)
        k = lax.broadcasted_iota(jnp.int32, (ncmp, LANES), 0)
        c = lax.broadcasted_iota(jnp.int32, (ncmp, LANES), 1)
        cposf_scr[...] = jnp.where(c == F_CMP, k.astype(F32), 0.0).astype(BF16)

    def const_feat(h):
        s = _alibi_slope(h)
        return jnp.where((lane == F_SEL_OFF) | (lane == F_WIN_OFF), s,
                         jnp.where(lane == F_WIN_BLK, SEL_LEN * s,
                                   jnp.where(lane == F_CMP, CMP_STRIDE * s, 0.0)))

    for h in range(NSA_HEADS):
        rows = slice(h * QB, (h + 1) * QB)
        qe_scr[rows, 0:LANES] = q_ref[:, h * LANES:(h + 1) * LANES] * 0.125
        qe_scr[rows, LANES:2 * LANES] = jnp.broadcast_to(const_feat(h), (QB, LANES)).astype(BF16)

    def normalised(o, guard):
        outs = []
        for pair in range(NSA_HEADS // 2):
            e = o[(2 * pair) * QB:(2 * pair + 1) * QB]
            d = o[(2 * pair + 1) * QB:(2 * pair + 2) * QB]
            num = jnp.where(lo_half, e, pltpu.roll(d, NSA_DH, 1))
            den = jnp.where(lo_half, pltpu.roll(e, NSA_DH, 1), d)
            if guard:
                den = jnp.where(den > 0.0, den, 1.0)
            outs.append(num / den)
        return outs

    ws = pl.multiple_of(jnp.maximum(t0 - WINDOW, 0), QB)
    qs = lax.broadcasted_iota(jnp.int32, (QB, WIN_KEYS), 0)
    kl = lax.broadcasted_iota(jnp.int32, (QB, WIN_KEYS), 1)
    dist_w = (t0 - ws) + qs - kl
    ok_w = (dist_w >= 0) & (dist_w < WINDOW)
    s_w = _dot_nt(qe_scr[...], jnp.concatenate([kw_ref[pl.ds(ws, WIN_KEYS), :], wposf_scr[...]], axis=1))

    q_sub = lax.broadcasted_iota(jnp.int32, (QB, ncmp), 0)
    n_lane = lax.broadcasted_iota(jnp.int32, (QB, ncmp), 1)
    ok_c = n_lane * CMP_STRIDE + (CMP_LEN - 1) <= t0 + q_sub
    s = _dot_nt(qe_scr[...], jnp.concatenate([kc_ref[0], cposf_scr[...]], axis=1))
    s = s + _tile_heads(jnp.where(ok_c, 0.0, NEG))
    e = jnp.exp(s - jnp.max(s, axis=-1, keepdims=True)) * _tile_heads(jnp.where(ok_c, 1.0, 0.0))
    e_bf = e.astype(BF16)
    oc = jnp.concatenate([_dot(e_bf[0:HALF], vc0_ref[0]), _dot(e_bf[HALF:2 * HALF], vc1_ref[0])],
                         axis=0)
    o_cmp = normalised(oc, True)
    l_c = jnp.where(lo_half, pltpu.roll(oc, NSA_DH, 1), oc)
    inv_l = 1.0 / jnp.where(l_c > 0.0, l_c, 1.0)
    p_c = e * jnp.concatenate([inv_l] * (ncmp // LANES), axis=1)

    imp_ts = []
    for g in range(NSA_KV):
        psum = p_c[(g * NSA_REP) * QB:(g * NSA_REP + 1) * QB]
        for r in range(1, NSA_REP):
            psum = psum + p_c[(g * NSA_REP + r) * QB:(g * NSA_REP + r + 1) * QB]
        p_hi = psum.astype(BF16)
        p_lo = (psum - p_hi.astype(F32)).astype(BF16)
        imp = _dot(p_hi, ov_ref[...]) + _dot(p_lo, ov_ref[...])
        imp_ts.append(imp.T[0:nbs, :])

    n_grp = nbs // SUBLANES
    sub8 = lax.broadcasted_iota(jnp.int32, (SUBLANES, QB), 0)
    tpos = t0 + lax.broadcasted_iota(jnp.int32, (1, QB), 1)
    cur = tpos // SEL_LEN
    j_lane = lane.astype(F32)
    for g in range(NSA_KV):
        imp_t = imp_ts[g]
        j_sub = lax.broadcasted_iota(jnp.int32, (nbs, QB), 0)
        forced = (j_sub == 0) | (j_sub == cur) | (j_sub == cur - 1)
        score = jnp.where(j_sub <= cur, imp_t + jnp.where(forced, FORCE_BONUS, 0.0), NEG)
        parts = [score[a * SUBLANES:(a + 1) * SUBLANES, :] for a in range(n_grp)]
        ranks = [jnp.zeros((SUBLANES, QB), F32) for _ in range(n_grp)]
        for i in range(nbs):
            row = jnp.broadcast_to(score[i:i + 1, :], (SUBLANES, QB))
            for a in range(n_grp):
                if a * SUBLANES > i:
                    beats = jnp.where(row >= parts[a], 1.0, 0.0)
                elif (a + 1) * SUBLANES - 1 < i:
                    beats = jnp.where(row > parts[a], 1.0, 0.0)
                else:
                    beats = jnp.where(sub8 + a * SUBLANES > i,
                                      jnp.where(row >= parts[a], 1.0, 0.0),
                                      jnp.where(row > parts[a], 1.0, 0.0))
                ranks[a] = ranks[a] + beats
        sel_t = [jnp.where(rk < float(min(SEL_TOP, nbs)), 0.0, NEG) for rk in ranks]
        if nbs < LANES:
            sel_t.append(jnp.zeros((LANES - nbs, QB), F32))
        sel_bias = jnp.concatenate(sel_t, axis=0).T
        for r in range(NSA_REP):
            h = g * NSA_REP + r
            feat = jnp.where(lo_half, sel_bias + (_alibi_slope(h) * SEL_LEN) * j_lane, const_feat(h))
            qe_scr[h * QB:(h + 1) * QB, LANES:2 * LANES] = feat.astype(BF16)

    s_w = s_w + _tile_heads(jnp.where(ok_w, 0.0, NEG))
    p_w = jnp.exp(s_w - jnp.max(s_w, axis=-1, keepdims=True)).astype(BF16)
    o_win = normalised(jnp.concatenate([_dot(p_w[0:HALF], vw0_ref[pl.ds(ws, WIN_KEYS), :]),
                                        _dot(p_w[HALF:2 * HALF], vw1_ref[pl.ds(ws, WIN_KEYS), :])],
                                       axis=0), False)

    last_tile = t0 // KT
    mrun_scr[...] = jnp.full(mrun_scr.shape, NEG, F32)

    U = SEL_GROUP

    def score_group(kt0, count, causal):
        k0 = pl.multiple_of(kt0 * KT, KT)
        kx = jnp.concatenate([ks_ref[pl.ds(k0, count * KT), :],
                              posf_scr[pl.ds(k0, count * KT), :]], axis=1)
        s = _dot_nt(qe_scr[...], kx)
        tile_max = None
        for j in range(count):
            sj = s[:, j * KT:(j + 1) * KT] * LOG2_E
            if causal and j == count - 1:
                qs = lax.broadcasted_iota(jnp.int32, (QB, KT), 0)
                kl = lax.broadcasted_iota(jnp.int32, (QB, KT), 1)
                sj = sj + _tile_heads(jnp.where(k0 + j * KT + kl <= t0 + qs, 0.0, NEG))
            s_scr[kt0 + j] = sj
            for c0 in range(0, KT, LANES):
                part = sj[:, c0:c0 + LANES]
                tile_max = part if tile_max is None else jnp.maximum(tile_max, part)
        mrun_scr[...] = jnp.maximum(mrun_scr[...], tile_max)

    def score_full_group(i, carry):
        score_group(i * U, U, False)
        return carry

    lax.fori_loop(0, last_tile // U, score_full_group, 0)
    for r in range(U):
        @pl.when(last_tile % U == r)
        def _(r=r):
            score_group(last_tile - r, r + 1, True)

    mb_scr[...] = jnp.broadcast_to(jnp.max(mrun_scr[...], axis=-1, keepdims=True), mb_scr.shape)
    acc_scr[...] = jnp.zeros(acc_scr.shape, F32)

    def value_group(kt0, count):
        k0 = pl.multiple_of(kt0 * KT, KT)
        mb = mb_scr[...]
        mbw = jnp.concatenate([mb] * (KT // LANES), axis=1)
        p = jnp.concatenate([jnp.exp2(s_scr[kt0 + j] - mbw).astype(BF16) for j in range(count)],
                            axis=1)
        acc_scr[0:HALF, :] += _dot(p[0:HALF], vs0_ref[pl.ds(k0, count * KT), :])
        acc_scr[HALF:2 * HALF, :] += _dot(p[HALF:2 * HALF], vs1_ref[pl.ds(k0, count * KT), :])

    def value_full_group(i, carry):
        value_group(i * U, U)
        return carry

    n_tiles = last_tile + 1
    lax.fori_loop(0, n_tiles // U, value_full_group, 0)
    for r in range(1, U):
        @pl.when(n_tiles % U == r)
        def _(r=r):
            value_group(n_tiles - r, r)

    o_sel = normalised(acc_scr[...], False)

    gate = jax.nn.sigmoid(g_ref[...])
    for pair in range(NSA_HEADS // 2):
        y = None
        for b, o in enumerate((o_cmp, o_sel, o_win)):
            c_e = SUBLANES + 3 * (2 * pair) + b
            c_o = SUBLANES + 3 * (2 * pair + 1) + b
            term = jnp.where(lo_half, gate[:, c_e:c_e + 1], gate[:, c_o:c_o + 1]) * o[pair]
            y = term if y is None else y + term
        y_ref[:, pair * LANES:(pair + 1) * LANES] = y.astype(y_ref.dtype)


def _nsa(pr, kc, vc0, vc1, batch, seq):
    QB = Q_BLOCK
    nq = seq // QB
    nbs = seq // SEL_LEN
    ncmp = kc.shape[1]
    cs = np.arange(ncmp) * CMP_STRIDE
    js = np.arange(nbs) * SEL_LEN
    ov = np.zeros((ncmp, LANES), np.float32)
    ov[:, :nbs] = (cs[:, None] < js[None, :] + SEL_LEN) & (cs[:, None] + CMP_LEN > js[None, :])
    tok = lambda b, i: (b * nq + i, 0)
    per_b = lambda b, i: (b, 0)
    per_b3 = lambda b, i: (b, 0, 0)
    kv_spec = pl.BlockSpec((seq, LANES), per_b)
    cmp_spec = pl.BlockSpec((1, ncmp, LANES), per_b3)
    return pl.pallas_call(
        functools.partial(_nsa_kernel, nbs),
        out_shape=jax.ShapeDtypeStruct((batch * seq, NSA_WIDTH), BF16),
        grid=(batch, nq),
        in_specs=[pl.BlockSpec((QB, NSA_HEADS * LANES), tok),
                  pl.BlockSpec((QB, LANES), tok),
                  cmp_spec, cmp_spec, cmp_spec,
                  kv_spec, kv_spec, kv_spec, kv_spec, kv_spec, kv_spec,
                  pl.BlockSpec((ncmp, LANES), lambda b, i: (0, 0))],
        out_specs=pl.BlockSpec((QB, NSA_WIDTH), tok),
        scratch_shapes=[pltpu.VMEM((NSA_ROWS, 2 * LANES), BF16),
                        pltpu.VMEM((seq, LANES), BF16),
                        pltpu.VMEM((WIN_KEYS, LANES), BF16),
                        pltpu.VMEM((ncmp, LANES), BF16),
                        pltpu.VMEM((seq // SEL_KT, NSA_ROWS, SEL_KT), F32),
                        pltpu.VMEM((NSA_ROWS, LANES), F32),
                        pltpu.VMEM((NSA_ROWS, LANES), F32),
                        pltpu.VMEM((NSA_ROWS, LANES), F32)],
        compiler_params=pltpu.CompilerParams(dimension_semantics=("parallel", "arbitrary"),
                                             vmem_limit_bytes=VMEM_LIMIT),
        name="nsa",
    )(pr["nq"], pr["gcol"], kc, vc0, vc1, pr["ks"], pr["vs0"], pr["vs1"],
      pr["kw"], pr["vw0"], pr["vw1"], jnp.asarray(ov, BF16))


def _memkv_kernel(mem_ref, g_ref, w_ref, kv_ref):
    u = _rms(mem_ref[0], g_ref[...]).astype(BF16)
    kv_ref[0] = _dot(u, w_ref[...]).astype(kv_ref.dtype)


def _memkv(mem, g_mem, w_mem_kv):
    batch, n_mem, d = mem.shape
    return pl.pallas_call(
        _memkv_kernel,
        out_shape=jax.ShapeDtypeStruct((batch, n_mem, 2 * XA_WIDTH), BF16),
        grid=(batch,),
        in_specs=[pl.BlockSpec((1, n_mem, d), lambda b: (b, 0, 0)),
                  _resident((1, d)), _resident((d, 2 * XA_WIDTH))],
        out_specs=pl.BlockSpec((1, n_mem, 2 * XA_WIDTH), lambda b: (b, 0, 0)),
        compiler_params=pltpu.CompilerParams(dimension_semantics=("parallel",),
                                             vmem_limit_bytes=VMEM_LIMIT),
        name="memkv",
    )(mem, g_mem[None, :], w_mem_kv.astype(BF16))


def _tail_kernel(x_ref, yml_ref, ynsa_ref, xq_ref, kv_ref, mg_ref, wb_ref, wo_ref, gf_ref,
                 w1_ref, w2_ref, gl_ref, out_ref):
    d = x_ref.shape[1]
    kv = kv_ref[0]
    heads = [slice(hh * XA_DH, (hh + 1) * XA_DH) for hh in range(XA_HEADS)]
    scores = [_dot_nt(xq_ref[:, sl], kv[:, sl]) * (XA_DH ** -0.5) for sl in heads]
    ups = [_dot(yml_ref[...], wb_ref[0]), _dot(ynsa_ref[...], wb_ref[1])]
    y_xa = []
    for hh, s in enumerate(scores):
        p = jnp.exp(s - jnp.max(s, axis=-1, keepdims=True))
        l = jnp.sum(p, axis=-1, keepdims=True)
        y_xa.append((_dot(p.astype(BF16), kv[:, XA_WIDTH + hh * XA_DH:XA_WIDTH + (hh + 1) * XA_DH])
                     / l).astype(BF16))
    ups.append(_dot(jnp.concatenate(y_xa, axis=-1), wb_ref[2]))
    merged = None
    for j in range(N_BRANCH):
        term = mg_ref[:, j * d:(j + 1) * d] * ups[j]
        merged = term if merged is None else merged + term
    h = x_ref[...] + _dot(merged.astype(BF16), wo_ref[...])
    u = _rms(h, gf_ref[...]).astype(BF16)
    slabs = list(range(0, w1_ref.shape[1], FF_SLAB))
    up = lambda c0: _dot(u, w1_ref[:, c0:c0 + FF_SLAB])
    acc = h
    nxt = up(slabs[0])
    for i, c0 in enumerate(slabs):
        cur = nxt
        if i + 1 < len(slabs):
            nxt = up(slabs[i + 1])
        a = jnp.maximum(cur, 0.0)
        acc = acc + _dot((a * a).astype(BF16), w2_ref[c0:c0 + FF_SLAB, :])
    out_ref[...] = _rms(acc, gl_ref[...])


def _tail(x2, y_ml, y_nsa, pr, mem_kv, w_branch, w_out, g_ffn, w_ff1, w_ff2, g_final, seq):
    n_tok, d = x2.shape
    tm = TM_TAIL
    d_ff = w_ff1.shape[1]
    n_mem = mem_kv.shape[1]
    tiles_per_b = seq // tm
    row = lambda i: (i, 0)
    return pl.pallas_call(
        _tail_kernel,
        out_shape=jax.ShapeDtypeStruct((n_tok, d), F32),
        grid=(n_tok // tm,),
        in_specs=[pl.BlockSpec((tm, d), row),
                  pl.BlockSpec((tm, ML_WIDTH), row),
                  pl.BlockSpec((tm, NSA_WIDTH), row),
                  pl.BlockSpec((tm, XA_WIDTH), row),
                  pl.BlockSpec((1, n_mem, 2 * XA_WIDTH), lambda i: (i // tiles_per_b, 0, 0)),
                  pl.BlockSpec((tm, N_BRANCH * d), row),
                  _resident((N_BRANCH, ML_WIDTH, d)), _resident((d, d)), _resident((1, d)),
                  _resident((d, d_ff)), _resident((d_ff, d)), _resident((1, d))],
        out_specs=pl.BlockSpec((tm, d), row),
        compiler_params=pltpu.CompilerParams(dimension_semantics=("parallel",),
                                             vmem_limit_bytes=VMEM_LIMIT),
        name="tail",
    )(x2, y_ml, y_nsa, pr["xq"], mem_kv, pr["mg"], w_branch.astype(BF16), w_out.astype(BF16),
      g_ffn[None, :], w_ff1.astype(BF16), w_ff2.astype(BF16), g_final[None, :])


def _layer(x, mem, g_mix, w_in, b_in, ml_conv, ml_norm_g, cmp_pe, cmp_w1, cmp_w2, g_mem, w_mem_kv,
           w_branch, w_out, g_ffn, w_ff1, w_ff2, g_final):
    batch, seq, d = x.shape
    x2 = x.reshape(batch * seq, d)
    pr = _inproj(x2, g_mix, w_in, b_in, ml_conv, ml_norm_g, seq)
    y_ml = pr["y_ml"]
    kc, vc0, vc1 = _compress(pr, cmp_pe, cmp_w1, cmp_w2, batch, seq)
    y_nsa = _nsa(pr, kc, vc0, vc1, batch, seq)
    mem_kv = _memkv(mem, g_mem, w_mem_kv)
    out = _tail(x2, y_ml, y_nsa, pr, mem_kv, w_branch, w_out, g_ffn, w_ff1, w_ff2, g_final, seq)
    return out.reshape(batch, seq, d)


def kernel(x, mem, g_mix, w_in, b_in, ml_conv, ml_norm_g, cmp_pe, cmp_w1, cmp_w2, g_mem, w_mem_kv,
           w_branch, w_out, g_ffn, w_ff1, w_ff2, g_final):
    assert g_mix.shape[0] == 1, "single-layer block"
    return _layer(x, mem, g_mix[0], w_in[0], b_in[0], ml_conv[0], ml_norm_g[0], cmp_pe[0],
                  cmp_w1[0], cmp_w2[0], g_mem[0], w_mem_kv[0], w_branch[0], w_out[0], g_ffn[0],
                  w_ff1[0], w_ff2[0], g_final)
```

```python
import functools

import numpy as np
import jax
import jax.numpy as jnp
from jax import lax
from jax.experimental import pallas as pl
from jax.experimental.pallas import tpu as pltpu

F32 = jnp.float32
BF16 = jnp.bfloat16

EPS = 1e-6
NEG = -1e30
LOG2_E = 1.4426950408889634
ML_HEADS = 4
ML_DH = 128
ML_WIDTH = ML_HEADS * ML_DH
ML_CONV = 4
ML_CHUNK = 128
NSA_HEADS = 8
NSA_KV = 2
NSA_REP = NSA_HEADS // NSA_KV
NSA_DH = 64
NSA_WIDTH = NSA_HEADS * NSA_DH
NSA_KV_WIDTH = NSA_KV * NSA_DH
CMP_LEN = 32
CMP_STRIDE = 16
CMP_HIDDEN = 256
SEL_LEN = 64
SEL_TOP = 16
WINDOW = 512
Q_BLOCK = 128
FORCE_BONUS = 1e3
SEL_KT = 256
SEL_GROUP = 8
XA_HEADS = 4
XA_DH = 128
XA_WIDTH = XA_HEADS * XA_DH
N_BRANCH = 3

LANES = 128
SUBLANES = 8
VMEM_LIMIT = 56 * 1024 * 1024

TM_PROJ = 256
TM_TAIL = 256
FF_SLAB = 1024

_NT = (((1,), (1,)), ((), ()))
_TN = (((0,), (0,)), ((), ()))


def _dot(a, b):
    return jnp.dot(a, b, preferred_element_type=F32)


def _dot_nt(a, b):
    return lax.dot_general(a, b, _NT, preferred_element_type=F32)


def _resident(shape):
    nd = len(shape)
    return pl.BlockSpec(shape, lambda *_: (0,) * nd, pipeline_mode=pl.Buffered(1))


def _rms(x, g):
    return x * lax.rsqrt(jnp.mean(x * x, axis=-1, keepdims=True) + EPS) * g


_PROJ_COLS = (
    ("qk", 2 * ML_WIDTH, BF16),
    ("v", ML_WIDTH, BF16),
    ("o", ML_WIDTH, BF16),
    ("gcol", LANES, F32),
    ("kc", NSA_KV_WIDTH, F32),
    ("vc", NSA_KV_WIDTH, F32),
    ("ks", NSA_KV_WIDTH, BF16),
    ("kw", NSA_KV_WIDTH, BF16),
    ("vs0", LANES, BF16),
    ("vs1", LANES, BF16),
    ("vw0", LANES, BF16),
    ("vw1", LANES, BF16),
    ("nq", NSA_HEADS * LANES, BF16),
    ("xq", XA_WIDTH, BF16),
    ("mg", None, BF16),
)
_PROJ_MLSTM_ONLY = ("qk", "v", "o")
_PROJ_CHUNK = 512


def _proj_layout(d_model):
    cols, off = [], 0
    for name, width, dt in _PROJ_COLS:
        width = N_BRANCH * d_model if width is None else width
        cols.append((name, off, width, dt))
        off += width
    return cols, off


def _inproj_kernel(cols, tiles_per_b, x_ref, w_ref, b_ref, wt_ref, bt_ref, conv_ref, ng_ref,
                   *refs):
    out_names = [c[0] for c in cols if c[0] not in _PROJ_MLSTM_ONLY]
    out_refs = dict(zip(out_names, refs[:len(out_names)]))
    yml_ref, xbuf, c_scr, m_scr = refs[len(out_names):]
    tm = x_ref.shape[0]
    halo = SUBLANES

    @pl.when(pl.program_id(0) % tiles_per_b == 0)
    def _():
        xbuf[0:halo, :] = jnp.zeros((halo, xbuf.shape[1]), F32)
        c_scr[...] = jnp.zeros(c_scr.shape, F32)
        m_scr[...] = jnp.zeros(m_scr.shape, F32)

    x = x_ref[...]
    xb = x.astype(BF16)
    r = lax.rsqrt(jnp.mean(x * x, axis=-1, keepdims=True) + EPS)
    grow = _dot_nt(wt_ref[...], (x * r).astype(BF16)) + bt_ref[...]
    kept = {name: [] for name in _PROJ_MLSTM_ONLY + ("gcol",)}
    mlstm_ready = max(off + width for name, off, width, _ in cols if name in kept)
    ncols = w_ref.shape[1]
    for c0 in range(0, ncols, _PROJ_CHUNK):
        cw = min(_PROJ_CHUNK, ncols - c0)
        chunk = r * _dot(xb, w_ref[:, c0:c0 + cw]) + b_ref[:, c0:c0 + cw]
        for name, off, width, dt in cols:
            lo, hi = max(off, c0), min(off + width, c0 + cw)
            if lo >= hi:
                continue
            acc = chunk[:, lo - c0:hi - c0]
            cs = slice(lo - off, hi - off)
            if name == "qk":
                xbuf[halo:halo + tm, cs] = acc
                y = acc * conv_ref[0:1, cs]
                for j in range(1, ML_CONV):
                    y = y + xbuf[halo - j:halo - j + tm, cs] * conv_ref[j:j + 1, cs]
                xbuf[0:halo, cs] = xbuf[tm:tm + halo, cs]
                acc = y * jax.nn.sigmoid(y)
                if lo - off >= ML_WIDTH:
                    acc = acc * (ML_DH ** -0.5)
            elif name in ("o", "mg"):
                acc = jax.nn.sigmoid(acc)
            acc = acc.astype(dt)
            if name in kept:
                kept[name].append(acc)
            if name in out_refs:
                out_refs[name][:, cs] = acc
        if c0 < mlstm_ready <= c0 + cw:
            qk, v, og = (jnp.concatenate(kept[n], axis=1) for n in _PROJ_MLSTM_ONLY)
            gcol = kept["gcol"][0]
            c_state = [c_scr[hh] for hh in range(ML_HEADS)]
            m_state = [m_scr[hh:hh + 1, 0:1] for hh in range(ML_HEADS)]
            for r0 in range(0, tm, ML_CHUNK):
                rows = slice(r0, r0 + ML_CHUNK)
                y_ml, c_state, m_state = _mlstm_chunk(qk[rows], v[rows], og[rows], gcol[rows],
                                                      grow[:, rows], ng_ref[...], c_state, m_state)
                yml_ref[rows, :] = y_ml
            for hh in range(ML_HEADS):
                c_scr[hh] = c_state[hh]
                m_scr[hh:hh + 1, :] = jnp.broadcast_to(m_state[hh], (1, m_scr.shape[1]))


def _split_w_in(w_in, b_in):
    widths = (ML_WIDTH, ML_WIDTH, ML_WIDTH, ML_WIDTH, ML_HEADS, ML_HEADS,
              NSA_WIDTH, NSA_KV_WIDTH, NSA_KV_WIDTH, NSA_KV_WIDTH, NSA_KV_WIDTH, NSA_KV_WIDTH,
              NSA_KV_WIDTH, 3 * NSA_HEADS, XA_WIDTH, w_in.shape[1])
    names = ("ml_q", "ml_k", "ml_v", "ml_o", "ml_i", "ml_f", "ns_q", "ns_kc", "ns_vc", "ns_ks",
             "ns_vs", "ns_kw", "ns_vw", "ns_g", "xa_q", "mg")
    out, off = {}, 0
    for n, wd in zip(names, widths):
        end = w_in.shape[1] if n == "mg" else off + wd
        out[n] = (w_in[:, off:end], b_in[off:end])
        off = end
    return out


def _inproj(x2, g_mix, w_in, b_in, ml_conv, ml_norm_g, seq):
    n_tok, d = x2.shape
    cols, ncols = _proj_layout(d)
    p = _split_w_in(w_in, b_in)

    def pad_cols(w, b, width):
        return (jnp.pad(w, ((0, 0), (0, width - w.shape[1]))), jnp.pad(b, (0, width - b.shape[0])))

    wq, bq = p["ns_q"]
    wq = wq.reshape(d, NSA_HEADS, 1, NSA_DH)
    bq = bq.reshape(NSA_HEADS, 1, NSA_DH)
    in_group = jnp.asarray(np.arange(NSA_HEADS)[:, None] // NSA_REP == np.arange(NSA_KV)[None, :],
                           w_in.dtype)[:, :, None]
    wq_slots = wq * in_group
    bq_slots = bq * in_group
    gate_w = jnp.concatenate([p["ml_i"][0], p["ml_f"][0]], axis=1)
    gate_b = jnp.concatenate([p["ml_i"][1], p["ml_f"][1]])
    small_w = jnp.concatenate([gate_w, p["ns_g"][0]], axis=1)
    small_b = jnp.concatenate([gate_b, p["ns_g"][1]])

    def with_ones(wb, g):
        w, b = wb
        w = jnp.pad(w[:, g * NSA_DH:(g + 1) * NSA_DH], ((0, 0), (0, LANES - NSA_DH)))
        b = jnp.concatenate([b[g * NSA_DH:(g + 1) * NSA_DH], jnp.ones((LANES - NSA_DH,), b.dtype)])
        return w, b

    pieces = {
        "qk": (jnp.concatenate([p["ml_q"][0], p["ml_k"][0]], axis=1),
               jnp.concatenate([p["ml_q"][1], p["ml_k"][1]])),
        "o": p["ml_o"], "mg": p["mg"],
        "gcol": pad_cols(small_w, small_b, LANES),
        "kc": p["ns_kc"], "vc": p["ns_vc"], "v": p["ml_v"],
        "nq": (wq_slots.reshape(d, NSA_HEADS * LANES), bq_slots.reshape(NSA_HEADS * LANES)),
        "xq": p["xa_q"], "ks": p["ns_ks"], "kw": p["ns_kw"],
        "vs0": with_ones(p["ns_vs"], 0), "vs1": with_ones(p["ns_vs"], 1),
        "vw0": with_ones(p["ns_vw"], 0), "vw1": with_ones(p["ns_vw"], 1),
    }
    w_cols = (g_mix[:, None]
              * jnp.concatenate([pieces[name][0] for name, *_ in cols], axis=1)).astype(BF16)
    b_cols = jnp.concatenate([pieces[name][1] for name, *_ in cols])[None, :]
    w_rows = (g_mix[:, None] * gate_w).T.astype(BF16)
    b_rows = gate_b[:, None]

    tm = TM_PROJ
    tiles_per_b = seq // tm
    outs_hbm = [c for c in cols if c[0] not in _PROJ_MLSTM_ONLY] + [("y_ml", None, ML_WIDTH, BF16)]
    outs = pl.pallas_call(
        functools.partial(_inproj_kernel, cols, tiles_per_b),
        out_shape=[jax.ShapeDtypeStruct((n_tok, width), dt) for _, _, width, dt in outs_hbm],
        grid=(n_tok // tm,),
        in_specs=[pl.BlockSpec((tm, d), lambda i: (i, 0)),
                  _resident((d, ncols)), _resident((1, ncols)),
                  _resident((SUBLANES, d)), _resident((SUBLANES, 1)),
                  _resident((ML_CONV, 2 * ML_WIDTH)), _resident((1, ML_WIDTH))],
        out_specs=[pl.BlockSpec((tm, width), lambda i: (i, 0)) for _, _, width, _ in outs_hbm],
        scratch_shapes=[pltpu.VMEM((tm + SUBLANES, 2 * ML_WIDTH), F32),
                        pltpu.VMEM((ML_HEADS, ML_DH, 2 * ML_DH), F32),
                        pltpu.VMEM((SUBLANES, LANES), F32)],
        compiler_params=pltpu.CompilerParams(dimension_semantics=("arbitrary",),
                                             vmem_limit_bytes=VMEM_LIMIT),
        name="inproj",
    )(x2, w_cols, b_cols, w_rows, b_rows, ml_conv, ml_norm_g[None, :])
    return dict(zip([c[0] for c in outs_hbm], outs))


def _mlstm_chunk(qk, v, og, gcol, grow, ng, c_prev, m_prev):
    L = ML_CHUNK
    r_i = lax.broadcasted_iota(jnp.int32, (L, L), 0)
    c_i = lax.broadcasted_iota(jnp.int32, (L, L), 1)
    causal = c_i <= r_i
    tril = causal.astype(F32)
    triu = (r_i <= c_i).astype(F32)
    b_col = jnp.dot(tril, jax.nn.log_sigmoid(gcol), precision=lax.Precision.HIGHEST,
                    preferred_element_type=F32)
    b_row = jnp.dot(jax.nn.log_sigmoid(grow), triu, precision=lax.Precision.HIGHEST,
                    preferred_element_type=F32)
    ones_col = jnp.where(lax.broadcasted_iota(jnp.int32, (L, ML_DH), 1) == 0, 1.0, 0.0).astype(BF16)

    def operands(hh):
        sl = slice(hh * ML_DH, (hh + 1) * ML_DH)
        k = qk[:, ML_WIDTH + hh * ML_DH:ML_WIDTH + (hh + 1) * ML_DH]
        return qk[:, sl], k, jnp.concatenate([v[:, sl], ones_col], axis=1), c_prev[hh]

    ops = [operands(hh) for hh in range(ML_HEADS)]
    qk_t = [_dot_nt(q, k) for q, k, _, _ in ops]
    inter = [_dot(q, c.astype(BF16)) for q, _, _, c in ops]

    stab, ke_all, s_all = [], [], []
    for hh in range(ML_HEADS):
        li_c = gcol[:, hh:hh + 1]
        b_c = b_col[:, ML_HEADS + hh:ML_HEADS + hh + 1]
        li_r = grow[hh:hh + 1, :]
        b_r = b_row[ML_HEADS + hh:ML_HEADS + hh + 1, :]
        g = b_c[L - 1:L, :]
        log_d = jnp.where(causal, b_c - b_r + li_r, -jnp.inf)
        inter_log = b_c + m_prev[hh]
        m_t = jnp.maximum(inter_log, jnp.max(log_d, axis=-1, keepdims=True))
        s_all.append((qk_t[hh] * jnp.exp(log_d - m_t)).astype(BF16))
        w_end = g - b_c + li_c
        m_loc = jnp.max(w_end, axis=0, keepdims=True)
        ke_all.append((ops[hh][1] * jnp.exp(w_end - m_loc)).astype(BF16))
        m_new = jnp.maximum(g + m_prev[hh], m_loc)
        stab.append((m_t, jnp.exp(inter_log - m_t), jnp.exp(g + m_prev[hh] - m_new),
                     jnp.exp(m_loc - m_new), m_new))

    intra = [_dot(s, o[2]) for s, o in zip(s_all, ops)]
    a_c = [lax.dot_general(ke, o[2], _TN, preferred_element_type=F32) for ke, o in zip(ke_all, ops)]

    ys, new_c, new_m = [], [], []
    for hh in range(ML_HEADS):
        sl = slice(hh * ML_DH, (hh + 1) * ML_DH)
        m_t, sc, a, bb, m_new = stab[hh]
        nd = intra[hh] + sc * inter[hh]
        den = nd[:, ML_DH:ML_DH + 1]
        h = nd[:, 0:ML_DH] / jnp.maximum(jnp.abs(den), jnp.exp(-m_t))
        hn = h * lax.rsqrt(jnp.mean(h * h, axis=-1, keepdims=True) + EPS)
        ys.append((og[:, sl] * hn * ng[:, sl]).astype(BF16))
        new_c.append(a * ops[hh][3] + bb * a_c[hh])
        new_m.append(m_new)
    return jnp.concatenate(ys, axis=1), new_c, new_m


def _gelu_tanh(x):
    return 0.5 * x * (1.0 + jnp.tanh(np.sqrt(2.0 / np.pi) * (x + 0.044715 * (x * x * x))))


def _compress_kernel(k_ref, v_ref, pe_ref, w1_ref, w2k_ref, w2v_ref, kc_ref, vc0_ref, vc1_ref):
    nrow = kc_ref.shape[1]
    S = CMP_STRIDE

    def hidden(x_ref, which):
        lo = hi = None
        for j in range(0, S, 2):
            x = [x_ref[pl.ds(j + d, nrow, stride=S), :] for d in range(2)]
            for half in range(CMP_LEN // S):
                l0 = half * S + j
                xp = jnp.concatenate([(x[d] + pe_ref[which, l0 + d:l0 + d + 1, :]).astype(BF16)
                                      for d in range(2)], axis=1)
                t = _dot(xp, w1_ref[which, l0 // 2])
                if half == 0:
                    lo = t if lo is None else lo + t
                else:
                    hi = t if hi is None else hi + t
        return _gelu_tanh(lo + pltpu.roll(hi, nrow - 1, 0)).astype(BF16)

    kc_ref[0] = _dot(hidden(k_ref, 0), w2k_ref[...]).astype(kc_ref.dtype)
    hv = hidden(v_ref, 1)
    ones_half = lax.broadcasted_iota(jnp.int32, (nrow, LANES), 1) >= NSA_DH
    for g, o_ref in enumerate((vc0_ref, vc1_ref)):
        o_ref[0] = jnp.where(ones_half, 1.0, _dot(hv, w2v_ref[g])).astype(o_ref.dtype)


def _compress(pr, cmp_pe, cmp_w1, cmp_w2, batch, seq):
    nrow = seq // CMP_STRIDE
    hid = NSA_KV * CMP_HIDDEN
    w1 = cmp_w1.astype(BF16).reshape(2, CMP_LEN, NSA_DH, CMP_HIDDEN)
    z1 = jnp.zeros_like(w1)
    w1e = jnp.stack([jnp.concatenate([w1, z1], axis=-1), jnp.concatenate([z1, w1], axis=-1)], axis=2)
    w1e = w1e.reshape(2, CMP_LEN // 2, 2 * NSA_KV_WIDTH, hid)
    w2 = cmp_w2.astype(BF16)
    z2 = jnp.zeros_like(w2[0])
    w2k = jnp.concatenate([jnp.concatenate([w2[0], z2], axis=1),
                           jnp.concatenate([z2, w2[0]], axis=1)], axis=0)
    w2v = jnp.stack([jnp.pad(w2[1], ((g * CMP_HIDDEN, (NSA_KV - 1 - g) * CMP_HIDDEN),
                                     (0, LANES - NSA_DH))) for g in range(NSA_KV)])
    pe = jnp.concatenate([cmp_pe] * NSA_KV, axis=-1)
    tok_blk = pl.BlockSpec((seq, NSA_KV_WIDTH), lambda b: (b, 0))
    out_blk = pl.BlockSpec((1, nrow, LANES), lambda b: (b, 0, 0))
    return pl.pallas_call(
        _compress_kernel,
        out_shape=(jax.ShapeDtypeStruct((batch, nrow, LANES), BF16),) * 3,
        grid=(batch,),
        in_specs=[tok_blk, tok_blk,
                  _resident((2, CMP_LEN, NSA_KV_WIDTH)),
                  _resident((2, CMP_LEN // 2, 2 * NSA_KV_WIDTH, hid)),
                  _resident((hid, NSA_KV_WIDTH)),
                  _resident((NSA_KV, hid, LANES))],
        out_specs=(out_blk, out_blk, out_blk),
        compiler_params=pltpu.CompilerParams(dimension_semantics=("parallel",),
                                             vmem_limit_bytes=VMEM_LIMIT),
        name="compress",
    )(pr["kc"], pr["vc"], pe, w1e, w2k, w2v)


def _alibi_slope(h):
    return float(2.0 ** (-8.0 * (h + 1) / NSA_HEADS))


NSA_ROWS = NSA_HEADS * Q_BLOCK
WIN_KEYS = WINDOW + Q_BLOCK
F_SEL_BLOCK = 0
F_SEL_OFF = SEL_LEN
F_WIN_OFF = SEL_LEN + 1
F_WIN_BLK = SEL_LEN + 2
F_CMP = SEL_LEN + 3


def _tile_heads(x):
    return jnp.concatenate([x] * NSA_HEADS, axis=0)


def _nsa_kernel(nbs, q_ref, g_ref, kc_ref, vc0_ref, vc1_ref, ks_ref, vs0_ref, vs1_ref,
                kw_ref, vw0_ref, vw1_ref, ov_ref, y_ref,
                qe_scr, posf_scr, wposf_scr, cposf_scr, s_scr, mrun_scr, mb_scr, acc_scr):
    QB = Q_BLOCK
    KT = SEL_KT
    HALF = NSA_REP * QB
    t0 = pl.program_id(1) * QB
    ncmp = kc_ref.shape[1]
    lane = lax.broadcasted_iota(jnp.int32, (1, LANES), 1)
    lo_half = lane < NSA_DH

    @pl.when(pl.program_id(1) == 0)
    def _():
        seq = posf_scr.shape[0]
        k = lax.broadcasted_iota(jnp.int32, (seq, LANES), 0)
        c = lax.broadcasted_iota(jnp.int32, (seq, LANES), 1)
        posf = jnp.where(c < SEL_LEN, jnp.where(k // SEL_LEN == c, 1.0, 0.0),
                         jnp.where(c == F_SEL_OFF, (k % SEL_LEN).astype(F32), 0.0))
        posf_scr[...] = posf.astype(BF16)
        k = lax.broadcasted_iota(jnp.int32, (WIN_KEYS, LANES), 0)
        c = lax.broadcasted_iota(jnp.int32, (WIN_KEYS, LANES), 1)
        wposf = jnp.where(c == F_WIN_OFF, (k % SEL_LEN).astype(F32),
                          jnp.where(c == F_WIN_BLK, (k // SEL_LEN).astype(F32), 0.0))
        wposf_scr[...] = wposf.astype(BF16)
        k = lax.broadcasted_iota(jnp.int32, (ncmp, LANES), 0)
        c = lax.broadcasted_iota(jnp.int32, (ncmp, LANES), 1)
        cposf_scr[...] = jnp.where(c == F_CMP, k.astype(F32), 0.0).astype(BF16)

    def const_feat(h):
        s = _alibi_slope(h)
        return jnp.where((lane == F_SEL_OFF) | (lane == F_WIN_OFF), s,
                         jnp.where(lane == F_WIN_BLK, SEL_LEN * s,
                                   jnp.where(lane == F_CMP, CMP_STRIDE * s, 0.0)))

    for h in range(NSA_HEADS):
        rows = slice(h * QB, (h + 1) * QB)
        qe_scr[rows, 0:LANES] = q_ref[:, h * LANES:(h + 1) * LANES] * 0.125
        qe_scr[rows, LANES:2 * LANES] = jnp.broadcast_to(const_feat(h), (QB, LANES)).astype(BF16)

    def normalised(o, guard):
        outs = []
        for pair in range(NSA_HEADS // 2):
            e = o[(2 * pair) * QB:(2 * pair + 1) * QB]
            d = o[(2 * pair + 1) * QB:(2 * pair + 2) * QB]
            num = jnp.where(lo_half, e, pltpu.roll(d, NSA_DH, 1))
            den = jnp.where(lo_half, pltpu.roll(e, NSA_DH, 1), d)
            if guard:
                den = jnp.where(den > 0.0, den, 1.0)
            outs.append(num / den)
        return outs

    ws =pl.multiple_of(jnp.maximum(t0 - WINDOW, 0), QB)
    qs = lax.broadcasted_iota(jnp.int32, (QB, WIN_KEYS), 0)
    kl = lax.broadcasted_iota(jnp.int32, (QB, WIN_KEYS), 1)
    dist_w = (t0 - ws) + qs - kl
    ok_w = (dist_w >= 0) & (dist_w < WINDOW)
    s_w = _dot_nt(qe_scr[...], jnp.concatenate([kw_ref[pl.ds(ws, WIN_KEYS), :], wposf_scr[...]], axis=1))

    q_sub = lax.broadcasted_iota(jnp.int32, (QB, ncmp), 0)
    n_lane = lax.broadcasted_iota(jnp.int32, (QB, ncmp), 1)
    ok_c = n_lane * CMP_STRIDE + (CMP_LEN - 1) <= t0 + q_sub
    s = _dot_nt(qe_scr[...], jnp.concatenate([kc_ref[0], cposf_scr[...]], axis=1))
    s = s + _tile_heads(jnp.where(ok_c, 0.0, NEG))
    e = jnp.exp(s - jnp.max(s, axis=-1, keepdims=True)) * _tile_heads(jnp.where(ok_c, 1.0, 0.0))
    e_bf = e.astype(BF16)
    oc = jnp.concatenate([_dot(e_bf[0:HALF], vc0_ref[0]), _dot(e_bf[HALF:2 * HALF], vc1_ref[0])],
                         axis=0)
    o_cmp = normalised(oc, True)
    l_c = jnp.where(lo_half, pltpu.roll(oc, NSA_DH, 1), oc)
    inv_l = 1.0 / jnp.where(l_c > 0.0, l_c, 1.0)
    p_c = e * jnp.concatenate([inv_l] * (ncmp // LANES), axis=1)

    imp_ts = []
    for g in range(NSA_KV):
        psum = p_c[(g * NSA_REP) * QB:(g * NSA_REP + 1) * QB]
        for r in range(1, NSA_REP):
            psum = psum + p_c[(g * NSA_REP + r) * QB:(g * NSA_REP + r + 1) * QB]
        p_hi = psum.astype(BF16)
        p_lo = (psum - p_hi.astype(F32)).astype(BF16)
        imp = _dot(p_hi, ov_ref[...]) + _dot(p_lo, ov_ref[...])
        imp_ts.append(imp.T[0:nbs, :])

    n_grp = nbs // SUBLANES
    sub8 = lax.broadcasted_iota(jnp.int32, (SUBLANES, QB), 0)
    tpos = t0 + lax.broadcasted_iota(jnp.int32, (1, QB), 1)
    cur = tpos // SEL_LEN
    j_lane = lane.astype(F32)
    for g in range(NSA_KV):
        imp_t = imp_ts[g]
        j_sub = lax.broadcasted_iota(jnp.int32, (nbs, QB), 0)
        forced = (j_sub == 0) | (j_sub == cur) | (j_sub == cur - 1)
        score = jnp.where(j_sub <= cur, imp_t + jnp.where(forced, FORCE_BONUS, 0.0), NEG)
        parts = [score[a * SUBLANES:(a + 1) * SUBLANES, :] for a in range(n_grp)]
        ranks = [jnp.zeros((SUBLANES, QB), F32) for _ in range(n_grp)]
        for i in range(nbs):
            row = jnp.broadcast_to(score[i:i + 1, :], (SUBLANES, QB))
            for a in range(n_grp):
                if a * SUBLANES > i:
                    beats = jnp.where(row >= parts[a], 1.0, 0.0)
                elif (a + 1) * SUBLANES - 1 < i:
                    beats = jnp.where(row > parts[a], 1.0, 0.0)
                else:
                    beats = jnp.where(sub8 + a * SUBLANES > i,
                                      jnp.where(row >= parts[a], 1.0, 0.0),
                                      jnp.where(row > parts[a], 1.0, 0.0))
                ranks[a] = ranks[a] + beats
        sel_t = [jnp.where(rk < float(min(SEL_TOP, nbs)), 0.0, NEG) for rk in ranks]
        if nbs < LANES:
            sel_t.append(jnp.zeros((LANES - nbs, QB), F32))
        sel_bias = jnp.concatenate(sel_t, axis=0).T
        for r in range(NSA_REP):
            h = g * NSA_REP + r
            feat = jnp.where(lo_half, sel_bias + (_alibi_slope(h) * SEL_LEN) * j_lane, const_feat(h))
            qe_scr[h * QB:(h + 1) * QB, LANES:2 * LANES] = feat.astype(BF16)

    s_w = s_w + _tile_heads(jnp.where(ok_w, 0.0, NEG))
    p_w = jnp.exp(s_w - jnp.max(s_w, axis=-1, keepdims=True)).astype(BF16)
    o_win = normalised(jnp.concatenate([_dot(p_w[0:HALF], vw0_ref[pl.ds(ws, WIN_KEYS), :]),
                                        _dot(p_w[HALF:2 * HALF], vw1_ref[pl.ds(ws, WIN_KEYS), :])],
                                       axis=0), False)

    last_tile = t0 // KT
    mrun_scr[...] = jnp.full(mrun_scr.shape, NEG, F32)

    U = SEL_GROUP

    def score_group(kt0, count, causal):
        k0 = pl.multiple_of(kt0 * KT, KT)
        kx = jnp.concatenate([ks_ref[pl.ds(k0, count * KT), :],
                              posf_scr[pl.ds(k0, count * KT), :]], axis=1)
        s = _dot_nt(qe_scr[...], kx)
        tile_max = None
        for j in range(count):
            sj = s[:, j * KT:(j + 1) * KT] * LOG2_E
            if causal and j == count - 1:
                qs = lax.broadcasted_iota(jnp.int32, (QB, KT), 0)
                kl = lax.broadcasted_iota(jnp.int32, (QB, KT), 1)
                sj = sj + _tile_heads(jnp.where(k0 + j * KT + kl <= t0 + qs, 0.0, NEG))
            s_scr[kt0 + j] = sj
            for c0 in range(0, KT, LANES):
                part = sj[:, c0:c0 + LANES]
                tile_max = part if tile_max is None else jnp.maximum(tile_max, part)
        mrun_scr[...] = jnp.maximum(mrun_scr[...], tile_max)

    def score_full_group(i, carry):
        score_group(i * U, U, False)
        return carry

    lax.fori_loop(0, last_tile // U, score_full_group, 0)
    for r in range(U):
        @pl.when(last_tile % U == r)
        def _(r=r):
            score_group(last_tile - r, r + 1, True)

    mb_scr[...] = jnp.broadcast_to(jnp.max(mrun_scr[...], axis=-1, keepdims=True), mb_scr.shape)
    acc_scr[...] = jnp.zeros(acc_scr.shape, F32)

    def value_group(kt0, count):
        k0 = pl.multiple_of(kt0 * KT, KT)
        mb = mb_scr[...]
        mbw = jnp.concatenate([mb] * (KT // LANES), axis=1)
        p = jnp.concatenate([jnp.exp2(s_scr[kt0 + j] - mbw).astype(BF16) for j in range(count)],
                            axis=1)
        acc_scr[0:HALF, :] += _dot(p[0:HALF], vs0_ref[pl.ds(k0, count * KT), :])
        acc_scr[HALF:2 * HALF, :] += _dot(p[HALF:2 * HALF], vs1_ref[pl.ds(k0, count * KT), :])

    def value_full_group(i, carry):
        value_group(i * U, U)
        return carry

    n_tiles = last_tile + 1
    lax.fori_loop(0, n_tiles // U, value_full_group, 0)
    for r in range(1, U):
        @pl.when(n_tiles % U == r)
        def _(r=r):
            value_group(n_tiles - r, r)

    o_sel = normalised(acc_scr[...], False)

    gate = jax.nn.sigmoid(g_ref[...])
    for pair in range(NSA_HEADS // 2):
        y = None
        for b, o in enumerate((o_cmp, o_sel, o_win)):
            c_e = SUBLANES + 3 * (2 * pair) + b
            c_o = SUBLANES + 3 * (2 * pair + 1) + b
            term = jnp.where(lo_half, gate[:, c_e:c_e + 1], gate[:, c_o:c_o + 1]) * o[pair]
            y = term if y is None else y + term
        y_ref[:, pair * LANES:(pair + 1) * LANES] = y.astype(y_ref.dtype)


def _nsa(pr, kc, vc0, vc1, batch, seq):
    QB = Q_BLOCK
    nq = seq // QB
    nbs = seq // SEL_LEN
    ncmp = kc.shape[1]
    cs = np.arange(ncmp) * CMP_STRIDE
    js = np.arange(nbs) * SEL_LEN
    ov = np.zeros((ncmp, LANES), np.float32)
    ov[:, :nbs] = (cs[:, None] < js[None, :] + SEL_LEN) & (cs[:, None] + CMP_LEN > js[None, :])
    tok = lambda b, i: (b * nq + i, 0)
    per_b = lambda b, i: (b, 0)
    per_b3 = lambda b, i: (b, 0, 0)
    kv_spec = pl.BlockSpec((seq, LANES), per_b)
    cmp_spec = pl.BlockSpec((1, ncmp, LANES), per_b3)
    return pl.pallas_call(
        functools.partial(_nsa_kernel, nbs),
        out_shape=jax.ShapeDtypeStruct((batch * seq, NSA_WIDTH), BF16),
        grid=(batch, nq),
        in_specs=[pl.BlockSpec((QB, NSA_HEADS * LANES), tok),
                  pl.BlockSpec((QB, LANES), tok),
                  cmp_spec, cmp_spec, cmp_spec,
                  kv_spec, kv_spec, kv_spec, kv_spec, kv_spec, kv_spec,
                  pl.BlockSpec((ncmp, LANES), lambda b, i: (0, 0))],
        out_specs=pl.BlockSpec((QB, NSA_WIDTH), tok),
        scratch_shapes=[pltpu.VMEM((NSA_ROWS, 2 * LANES), BF16),
                        pltpu.VMEM((seq, LANES), BF16),
                        pltpu.VMEM((WIN_KEYS, LANES), BF16),
                        pltpu.VMEM((ncmp, LANES), BF16),
                        pltpu.VMEM((seq // SEL_KT, NSA_ROWS, SEL_KT), F32),
                        pltpu.VMEM((NSA_ROWS, LANES), F32),
                        pltpu.VMEM((NSA_ROWS, LANES), F32),
                        pltpu.VMEM((NSA_ROWS, LANES), F32)],
        compiler_params=pltpu.CompilerParams(dimension_semantics=("parallel", "arbitrary"),
                                             vmem_limit_bytes=VMEM_LIMIT),
        name="nsa",
    )(pr["nq"], pr["gcol"], kc, vc0, vc1, pr["ks"], pr["vs0"], pr["vs1"],
      pr["kw"], pr["vw0"], pr["vw1"], jnp.asarray(ov, BF16))


def _memkv_kernel(mem_ref, g_ref, w_ref, kv_ref):
    u = _rms(mem_ref[0], g_ref[...]).astype(BF16)
    kv_ref[0] = _dot(u, w_ref[...]).astype(kv_ref.dtype)


def _memkv(mem, g_mem, w_mem_kv):
    batch, n_mem, d = mem.shape
    return pl.pallas_call(
        _memkv_kernel,
        out_shape=jax.ShapeDtypeStruct((batch, n_mem, 2 * XA_WIDTH), BF16),
        grid=(batch,),
        in_specs=[pl.BlockSpec((1, n_mem, d), lambda b: (b, 0, 0)),
                  _resident((1, d)), _resident((d, 2 * XA_WIDTH))],
        out_specs=pl.BlockSpec((1, n_mem, 2 * XA_WIDTH), lambda b: (b, 0, 0)),
        compiler_params=pltpu.CompilerParams(dimension_semantics=("parallel",),
                                             vmem_limit_bytes=VMEM_LIMIT),
        name="memkv",
    )(mem, g_mem[None, :], w_mem_kv.astype(BF16))


def _tail_kernel(x_ref, yml_ref, ynsa_ref, xq_ref, kv_ref, mg_ref, wb_ref, wo_ref,
                 w1_ref, w2_ref, gl_ref, out_ref):
    d = x_ref.shape[1]
    kv = kv_ref[0]
    heads = [slice(hh * XA_DH, (hh + 1) * XA_DH) for hh in range(XA_HEADS)]
    scores = [_dot_nt(xq_ref[:, sl], kv[:, sl]) * (XA_DH ** -0.5) for sl in heads]
    ups = [_dot(yml_ref[...], wb_ref[0]), _dot(ynsa_ref[...], wb_ref[1])]
    y_xa = []
    for hh, s in enumerate(scores):
        p = jnp.exp(s - jnp.max(s, axis=-1, keepdims=True))
        l = jnp.sum(p, axis=-1, keepdims=True)
        y_xa.append((_dot(p.astype(BF16), kv[:, XA_WIDTH + hh * XA_DH:XA_WIDTH + (hh + 1) * XA_DH])
                     / l).astype(BF16))
    ups.append(_dot(jnp.concatenate(y_xa, axis=-1), wb_ref[2]))
    merged = None
    for j in range(N_BRANCH):
        term = mg_ref[:, j * d:(j + 1) * d] * ups[j]
        merged = term if merged is None else merged + term
    h = x_ref[...] + _dot(merged.astype(BF16), wo_ref[...])
    hb = h.astype(BF16)
    r = lax.rsqrt(jnp.mean(h * h, axis=-1, keepdims=True) + EPS)
    acc = h
    for c0 in range(0, w1_ref.shape[1], FF_SLAB):
        a = jnp.maximum(r * _dot(hb, w1_ref[:, c0:c0 + FF_SLAB]), 0.0)
        acc = acc + _dot((a * a).astype(BF16), w2_ref[c0:c0 + FF_SLAB, :])
    out_ref[...] = _rms(acc, gl_ref[...])


def _tail(x2, y_ml, y_nsa, pr, mem_kv, w_branch, w_out, g_ffn, w_ff1, w_ff2, g_final, seq):
    n_tok, d = x2.shape
    tm = TM_TAIL
    d_ff = w_ff1.shape[1]
    n_mem = mem_kv.shape[1]
    tiles_per_b = seq // tm
    row = lambda i: (i, 0)
    return pl.pallas_call(
        _tail_kernel,
        out_shape=jax.ShapeDtypeStruct((n_tok, d), F32),
        grid=(n_tok // tm,),
        in_specs=[pl.BlockSpec((tm, d), row),
                  pl.BlockSpec((tm, ML_WIDTH), row),
                  pl.BlockSpec((tm, NSA_WIDTH), row),
                  pl.BlockSpec((tm, XA_WIDTH), row),
                  pl.BlockSpec((1, n_mem, 2 * XA_WIDTH), lambda i: (i // tiles_per_b, 0, 0)),
                  pl.BlockSpec((tm, N_BRANCH * d), row),
                  _resident((N_BRANCH, ML_WIDTH, d)), _resident((d, d)),
                  _resident((d, d_ff)), _resident((d_ff, d)), _resident((1, d))],
        out_specs=pl.BlockSpec((tm, d), row),
        compiler_params=pltpu.CompilerParams(dimension_semantics=("parallel",),
                                             vmem_limit_bytes=VMEM_LIMIT),
        name="tail",
    )(x2, y_ml, y_nsa, pr["xq"], mem_kv, pr["mg"], w_branch.astype(BF16), w_out.astype(BF16),
      (g_ffn[:, None] * w_ff1).astype(BF16), w_ff2.astype(BF16), g_final[None, :])


def _layer(x, mem, g_mix, w_in, b_in, ml_conv, ml_norm_g, cmp_pe, cmp_w1, cmp_w2, g_mem, w_mem_kv,
           w_branch, w_out, g_ffn, w_ff1, w_ff2, g_final):
    batch, seq, d = x.shape
    assert seq % (SEL_LEN * SUBLANES) == 0 and seq % SEL_KT == 0 and seq >= WIN_KEYS
    assert seq // SEL_LEN <= NSA_DH, "one selection-block feature lane per block"
    assert seq // CMP_STRIDE <= 256, "compressed token index must be exact in bf16"
    assert seq % TM_PROJ == 0 and seq % TM_TAIL == 0 and TM_PROJ % ML_CHUNK == 0
    x2 = x.reshape(batch * seq, d)
    pr = _inproj(x2, g_mix, w_in, b_in, ml_conv, ml_norm_g, seq)
    y_ml = pr["y_ml"]
    kc, vc0, vc1 = _compress(pr, cmp_pe, cmp_w1, cmp_w2, batch, seq)
    y_nsa = _nsa(pr, kc, vc0, vc1, batch, seq)
    mem_kv = _memkv(mem, g_mem, w_mem_kv)
    out = _tail(x2, y_ml, y_nsa, pr, mem_kv, w_branch, w_out, g_ffn, w_ff1, w_ff2, g_final, seq)
    return out.reshape(batch, seq, d)


def kernel(x, mem, g_mix, w_in, b_in, ml_conv, ml_norm_g, cmp_pe, cmp_w1, cmp_w2, g_mem, w_mem_kv,
           w_branch, w_out, g_ffn, w_ff1, w_ff2, g_final):
    assert g_mix.shape[0] == 1, "single-layer block"
    return _layer(x, mem, g_mix[0], w_in[0], b_in[0], ml_conv[0], ml_norm_g[0], cmp_pe[0],
                  cmp_w1[0], cmp_w2[0], g_mem[0], w_mem_kv[0], w_branch[0], w_out[0], g_ffn[0],
                  w_ff1[0], w_ff2[0], g_final)
```

```python
import functools

import numpy as np
import jax
import jax.numpy as jnp
from jax import lax
from jax.experimental import pallas as pl
from jax.experimental.pallas import tpu as pltpu

F32 = jnp.float32
BF16 = jnp.bfloat16

EPS = 1e-6
NEG = -1e30
LOG2_E = 1.4426950408889634
ML_HEADS = 4
ML_DH = 128
ML_WIDTH = ML_HEADS * ML_DH
ML_CONV = 4
ML_CHUNK = 128
NSA_HEADS = 8
NSA_KV = 2
NSA_REP = NSA_HEADS // NSA_KV
NSA_DH = 64
NSA_WIDTH = NSA_HEADS * NSA_DH
NSA_KV_WIDTH = NSA_KV * NSA_DH
CMP_LEN = 32
CMP_STRIDE = 16
CMP_HIDDEN = 256
SEL_LEN = 64
SEL_TOP = 16
WINDOW = 512
Q_BLOCK = 128
FORCE_BONUS = 1e3
SEL_KT = 256
SEL_GROUP = 8
XA_HEADS = 4
XA_DH = 128
XA_WIDTH = XA_HEADS * XA_DH
N_BRANCH = 3

LANES = 128
SUBLANES = 8
VMEM_LIMIT = 56 * 1024 * 1024

TM_PROJ = 256
TM_TAIL = 256
FF_SLAB = 1024

_NT = (((1,), (1,)), ((), ()))
_TN = (((0,), (0,)), ((), ()))


def _dot(a, b):
    return jnp.dot(a, b, preferred_element_type=F32)


def _dot_nt(a, b):
    return lax.dot_general(a, b, _NT, preferred_element_type=F32)


def _resident(shape):
    nd = len(shape)
    return pl.BlockSpec(shape, lambda *_: (0,) * nd, pipeline_mode=pl.Buffered(1))


def _rms(x, g):
    return x * lax.rsqrt(jnp.mean(x * x, axis=-1, keepdims=True) + EPS) * g


_PROJ_COLS = (
    ("qk", 2 * ML_WIDTH, BF16),
    ("v", ML_WIDTH, BF16),
    ("o", ML_WIDTH, BF16),
    ("gcol", LANES, F32),
    ("kc", NSA_KV_WIDTH, F32),
    ("vc", NSA_KV_WIDTH, F32),
    ("ks", NSA_KV_WIDTH, BF16),
    ("kw", NSA_KV_WIDTH, BF16),
    ("vs0", LANES, BF16),
    ("vs1", LANES, BF16),
    ("vw0", LANES, BF16),
    ("vw1", LANES, BF16),
    ("nq", NSA_HEADS * LANES, BF16),
    ("xq", XA_WIDTH, BF16),
    ("mg", None, BF16),
)
_PROJ_MLSTM_ONLY = ("qk", "v", "o")
_PROJ_CHUNK = 512


def _proj_layout(d_model):
    cols, off = [], 0
    for name, width, dt in _PROJ_COLS:
        width = N_BRANCH * d_model if width is None else width
        cols.append((name, off, width, dt))
        off += width
    return cols, off


def _inproj_kernel(cols, tiles_per_b, x_ref, g_ref, w_ref, b_ref, wt_ref, bt_ref, conv_ref, ng_ref,
                   *refs):
    out_names = [c[0] for c in cols if c[0] not in _PROJ_MLSTM_ONLY]
    out_refs = dict(zip(out_names, refs[:len(out_names)]))
    yml_ref, xbuf, c_scr, m_scr = refs[len(out_names):]
    tm = x_ref.shape[0]
    halo = SUBLANES

    @pl.when(pl.program_id(0) % tiles_per_b == 0)
    def _():
        xbuf[0:halo, :] = jnp.zeros((halo, xbuf.shape[1]), F32)
        c_scr[...] = jnp.zeros(c_scr.shape, F32)
        m_scr[...] = jnp.zeros(m_scr.shape, F32)

    u = _rms(x_ref[...], g_ref[...]).astype(BF16)
    grow = _dot_nt(wt_ref[...], u) + bt_ref[...]
    kept = {name: [] for name in _PROJ_MLSTM_ONLY + ("gcol",)}
    mlstm_ready = max(off + width for name, off, width, _ in cols if name in kept)
    ncols = w_ref.shape[1]
    for c0 in range(0, ncols, _PROJ_CHUNK):
        cw = min(_PROJ_CHUNK, ncols - c0)
        chunk = _dot(u, w_ref[:, c0:c0 + cw]) + b_ref[:, c0:c0 + cw]
        for name, off, width, dt in cols:
            lo, hi = max(off, c0), min(off + width, c0 + cw)
            if lo >= hi:
                continue
            acc = chunk[:, lo - c0:hi - c0]
            cs = slice(lo - off, hi - off)
            if name == "qk":
                xbuf[halo:halo + tm, cs] = acc
                y = acc * conv_ref[0:1, cs]
                for j in range(1, ML_CONV):
                    y = y + xbuf[halo - j:halo - j + tm, cs] * conv_ref[j:j + 1, cs]
                xbuf[0:halo, cs] = xbuf[tm:tm + halo, cs]
                acc = y * jax.nn.sigmoid(y)
                if lo - off >= ML_WIDTH:
                    acc = acc * (ML_DH ** -0.5)
            elif name in ("o", "mg"):
                acc = jax.nn.sigmoid(acc)
            acc = acc.astype(dt)
            if name in kept:
                kept[name].append(acc)
            if name in out_refs:
                out_refs[name][:, cs] = acc
        if c0 < mlstm_ready <= c0 + cw:
            qk, v, og = (jnp.concatenate(kept[n], axis=1) for n in _PROJ_MLSTM_ONLY)
            gcol = kept["gcol"][0]
            c_state = [c_scr[hh] for hh in range(ML_HEADS)]
            m_state = [m_scr[hh:hh + 1, 0:1] for hh in range(ML_HEADS)]
            for r0 in range(0, tm, ML_CHUNK):
                rows = slice(r0, r0 + ML_CHUNK)
                y_ml, c_state, m_state = _mlstm_chunk(qk[rows], v[rows], og[rows], gcol[rows],
                                                      grow[:, rows], ng_ref[...], c_state, m_state)
                yml_ref[rows, :] = y_ml
            for hh in range(ML_HEADS):
                c_scr[hh] = c_state[hh]
                m_scr[hh:hh + 1, :] = jnp.broadcast_to(m_state[hh], (1, m_scr.shape[1]))


def _split_w_in(w_in, b_in):
    widths = (ML_WIDTH, ML_WIDTH, ML_WIDTH, ML_WIDTH, ML_HEADS, ML_HEADS,
              NSA_WIDTH, NSA_KV_WIDTH, NSA_KV_WIDTH, NSA_KV_WIDTH, NSA_KV_WIDTH, NSA_KV_WIDTH,
              NSA_KV_WIDTH, 3 * NSA_HEADS, XA_WIDTH, w_in.shape[1])
    names = ("ml_q", "ml_k", "ml_v", "ml_o", "ml_i", "ml_f", "ns_q", "ns_kc", "ns_vc", "ns_ks",
             "ns_vs", "ns_kw", "ns_vw", "ns_g", "xa_q", "mg")
    out, off = {}, 0
    for n, wd in zip(names, widths):
        end = w_in.shape[1] if n == "mg" else off + wd
        out[n] = (w_in[:, off:end], b_in[off:end])
        off = end
    return out


def _inproj(x2, g_mix, w_in, b_in, ml_conv, ml_norm_g, seq):
    n_tok, d = x2.shape
    cols, ncols = _proj_layout(d)
    p = _split_w_in(w_in, b_in)

    def pad_cols(w, b, width):
        return (jnp.pad(w, ((0, 0), (0, width - w.shape[1]))), jnp.pad(b, (0, width - b.shape[0])))

    wq, bq = p["ns_q"]
    wq = wq.reshape(d, NSA_HEADS, 1, NSA_DH)
    bq = bq.reshape(NSA_HEADS, 1, NSA_DH)
    in_group = jnp.asarray(np.arange(NSA_HEADS)[:, None] // NSA_REP == np.arange(NSA_KV)[None, :],
                           w_in.dtype)[:, :, None]
    wq_slots = wq * in_group
    bq_slots = bq * in_group
    gate_w = jnp.concatenate([p["ml_i"][0], p["ml_f"][0]], axis=1)
    gate_b = jnp.concatenate([p["ml_i"][1], p["ml_f"][1]])
    small_w = jnp.concatenate([gate_w, p["ns_g"][0]], axis=1)
    small_b = jnp.concatenate([gate_b, p["ns_g"][1]])

    def with_ones(wb, g):
        w, b = wb
        w = jnp.pad(w[:, g * NSA_DH:(g + 1) * NSA_DH], ((0, 0), (0, LANES - NSA_DH)))
        b = jnp.concatenate([b[g * NSA_DH:(g + 1) * NSA_DH], jnp.ones((LANES - NSA_DH,), b.dtype)])
        return w, b

    pieces = {
        "qk": (jnp.concatenate([p["ml_q"][0], p["ml_k"][0]], axis=1),
               jnp.concatenate([p["ml_q"][1], p["ml_k"][1]])),
        "o": p["ml_o"], "mg": p["mg"],
        "gcol": pad_cols(small_w, small_b, LANES),
        "kc": p["ns_kc"], "vc": p["ns_vc"], "v": p["ml_v"],
        "nq": (wq_slots.reshape(d, NSA_HEADS * LANES), bq_slots.reshape(NSA_HEADS * LANES)),
        "xq": p["xa_q"], "ks": p["ns_ks"], "kw": p["ns_kw"],
        "vs0": with_ones(p["ns_vs"], 0), "vs1": with_ones(p["ns_vs"], 1),
        "vw0": with_ones(p["ns_vw"], 0), "vw1": with_ones(p["ns_vw"], 1),
    }
    w_cols = jnp.concatenate([pieces[name][0] for name, *_ in cols], axis=1).astype(BF16)
    b_cols = jnp.concatenate([pieces[name][1] for name, *_ in cols])[None, :]
    w_rows = gate_w.T.astype(BF16)
    b_rows = gate_b[:, None]

    tm = TM_PROJ
    tiles_per_b = seq // tm
    outs_hbm = [c for c in cols if c[0] not in _PROJ_MLSTM_ONLY] + [("y_ml", None, ML_WIDTH, BF16)]
    outs = pl.pallas_call(
        functools.partial(_inproj_kernel, cols, tiles_per_b),
        out_shape=[jax.ShapeDtypeStruct((n_tok, width), dt) for _, _, width, dt in outs_hbm],
        grid=(n_tok // tm,),
        in_specs=[pl.BlockSpec((tm, d), lambda i: (i, 0)),
                  _resident((1, d)), _resident((d, ncols)), _resident((1, ncols)),
                  _resident((SUBLANES, d)), _resident((SUBLANES, 1)),
                  _resident((ML_CONV, 2 * ML_WIDTH)), _resident((1, ML_WIDTH))],
        out_specs=[pl.BlockSpec((tm, width), lambda i: (i, 0)) for _, _, width, _ in outs_hbm],
        scratch_shapes=[pltpu.VMEM((tm + SUBLANES, 2 * ML_WIDTH), F32),
                        pltpu.VMEM((ML_HEADS, ML_DH, 2 * ML_DH), F32),
                        pltpu.VMEM((SUBLANES, LANES), F32)],
        compiler_params=pltpu.CompilerParams(dimension_semantics=("arbitrary",),
                                             vmem_limit_bytes=VMEM_LIMIT),
        name="inproj",
    )(x2, g_mix[None, :], w_cols, b_cols, w_rows, b_rows, ml_conv, ml_norm_g[None, :])
    return dict(zip([c[0] for c in outs_hbm], outs))


def _mlstm_chunk(qk, v, og, gcol, grow, ng, c_prev, m_prev):
    L = ML_CHUNK
    r_i = lax.broadcasted_iota(jnp.int32, (L, L), 0)
    c_i = lax.broadcasted_iota(jnp.int32, (L, L), 1)
    causal = c_i <= r_i
    tril = causal.astype(F32)
    triu = (r_i <= c_i).astype(F32)
    b_col = jnp.dot(tril, jax.nn.log_sigmoid(gcol), precision=lax.Precision.HIGHEST,
                    preferred_element_type=F32)
    b_row = jnp.dot(jax.nn.log_sigmoid(grow), triu, precision=lax.Precision.HIGHEST,
                    preferred_element_type=F32)
    ones_col = jnp.where(lax.broadcasted_iota(jnp.int32, (L, ML_DH), 1) == 0, 1.0, 0.0).astype(BF16)

    def operands(hh):
        sl = slice(hh * ML_DH, (hh + 1) * ML_DH)
        k = qk[:, ML_WIDTH + hh * ML_DH:ML_WIDTH + (hh + 1) * ML_DH]
        return qk[:, sl], k, jnp.concatenate([v[:, sl], ones_col], axis=1), c_prev[hh]

    ops = [operands(hh) for hh in range(ML_HEADS)]
    qk_t = [_dot_nt(q, k) for q, k, _, _ in ops]
    inter = [_dot(q, c.astype(BF16)) for q, _, _, c in ops]

    stab, ke_all, s_all = [], [], []
    for hh in range(ML_HEADS):
        li_c = gcol[:, hh:hh + 1]
        b_c = b_col[:, ML_HEADS + hh:ML_HEADS + hh + 1]
        li_r = grow[hh:hh + 1, :]
        b_r = b_row[ML_HEADS + hh:ML_HEADS + hh + 1, :]
        g = b_c[L - 1:L, :]
        log_d = jnp.where(causal, b_c - b_r + li_r, -jnp.inf)
        inter_log = b_c + m_prev[hh]
        m_t = jnp.maximum(inter_log, jnp.max(log_d, axis=-1, keepdims=True))
        s_all.append((qk_t[hh] * jnp.exp(log_d - m_t)).astype(BF16))
        w_end = g - b_c + li_c
        m_loc = jnp.max(w_end, axis=0, keepdims=True)
        ke_all.append((ops[hh][1] * jnp.exp(w_end - m_loc)).astype(BF16))
        m_new = jnp.maximum(g + m_prev[hh], m_loc)
        stab.append((m_t, jnp.exp(inter_log - m_t), jnp.exp(g + m_prev[hh] - m_new),
                     jnp.exp(m_loc - m_new), m_new))

    intra = [_dot(s, o[2]) for s, o in zip(s_all, ops)]
    a_c = [lax.dot_general(ke, o[2], _TN, preferred_element_type=F32) for ke, o in zip(ke_all, ops)]

    ys, new_c, new_m = [], [], []
    for hh in range(ML_HEADS):
        sl = slice(hh * ML_DH, (hh + 1) * ML_DH)
        m_t, sc, a, bb, m_new = stab[hh]
        nd = intra[hh] + sc * inter[hh]
        den = nd[:, ML_DH:ML_DH + 1]
        h = nd[:, 0:ML_DH] / jnp.maximum(jnp.abs(den), jnp.exp(-m_t))
        hn = h * lax.rsqrt(jnp.mean(h * h, axis=-1, keepdims=True) + EPS)
        ys.append((og[:, sl] * hn * ng[:, sl]).astype(BF16))
        new_c.append(a * ops[hh][3] + bb * a_c[hh])
        new_m.append(m_new)
    return jnp.concatenate(ys, axis=1), new_c, new_m


def _gelu_tanh(x):
    return 0.5 * x * (1.0 + jnp.tanh(np.sqrt(2.0 / np.pi) * (x + 0.044715 * (x * x * x))))


def _compress_kernel(k_ref, v_ref, pe_ref, w1_ref, w2k_ref, w2v_ref, kc_ref, vc0_ref, vc1_ref):
    nrow = kc_ref.shape[1]
    S = CMP_STRIDE

    def hidden(x_ref, which):
        lo = hi = None
        for j in range(0, S, 2):
            x = [x_ref[pl.ds(j + d, nrow, stride=S), :] for d in range(2)]
            for half in range(CMP_LEN // S):
                l0 = half * S + j
                xp = jnp.concatenate([(x[d] + pe_ref[which, l0 + d:l0 + d + 1, :]).astype(BF16)
                                      for d in range(2)], axis=1)
                t = _dot(xp, w1_ref[which, l0 // 2])
                if half == 0:
                    lo = t if lo is None else lo + t
                else:
                    hi = t if hi is None else hi + t
        return _gelu_tanh(lo + pltpu.roll(hi, nrow - 1, 0)).astype(BF16)

    kc_ref[0] = _dot(hidden(k_ref, 0), w2k_ref[...]).astype(kc_ref.dtype)
    hv = hidden(v_ref, 1)
    ones_half = lax.broadcasted_iota(jnp.int32, (nrow, LANES), 1) >= NSA_DH
    for g, o_ref in enumerate((vc0_ref, vc1_ref)):
        o_ref[0] = jnp.where(ones_half, 1.0, _dot(hv, w2v_ref[g])).astype(o_ref.dtype)


def _compress(pr, cmp_pe, cmp_w1, cmp_w2, batch, seq):
    nrow = seq // CMP_STRIDE
    hid = NSA_KV * CMP_HIDDEN
    w1 = cmp_w1.astype(BF16).reshape(2, CMP_LEN, NSA_DH, CMP_HIDDEN)
    z1 = jnp.zeros_like(w1)
    w1e = jnp.stack([jnp.concatenate([w1, z1], axis=-1), jnp.concatenate([z1, w1], axis=-1)], axis=2)
    w1e = w1e.reshape(2, CMP_LEN // 2, 2 * NSA_KV_WIDTH, hid)
    w2 = cmp_w2.astype(BF16)
    z2 = jnp.zeros_like(w2[0])
    w2k = jnp.concatenate([jnp.concatenate([w2[0], z2], axis=1),
                           jnp.concatenate([z2, w2[0]], axis=1)], axis=0)
    w2v = jnp.stack([jnp.pad(w2[1], ((g * CMP_HIDDEN, (NSA_KV - 1 - g) * CMP_HIDDEN),
                                     (0, LANES - NSA_DH))) for g in range(NSA_KV)])
    pe = jnp.concatenate([cmp_pe] * NSA_KV, axis=-1)
    tok_blk = pl.BlockSpec((seq, NSA_KV_WIDTH), lambda b: (b, 0))
    out_blk = pl.BlockSpec((1, nrow, LANES), lambda b: (b, 0, 0))
    return pl.pallas_call(
        _compress_kernel,
        out_shape=(jax.ShapeDtypeStruct((batch, nrow, LANES), BF16),) * 3,
        grid=(batch,),
        in_specs=[tok_blk, tok_blk,
                  _resident((2, CMP_LEN, NSA_KV_WIDTH)),
                  _resident((2, CMP_LEN // 2, 2 * NSA_KV_WIDTH, hid)),
                  _resident((hid, NSA_KV_WIDTH)),
                  _resident((NSA_KV, hid, LANES))],
        out_specs=(out_blk, out_blk, out_blk),
        compiler_params=pltpu.CompilerParams(dimension_semantics=("parallel",),
                                             vmem_limit_bytes=VMEM_LIMIT),
        name="compress",
    )(pr["kc"], pr["vc"], pe, w1e, w2k, w2v)


def _alibi_slope(h):
    return float(2.0 ** (-8.0 * (h + 1) / NSA_HEADS))


NSA_ROWS = NSA_HEADS * Q_BLOCK
WIN_KEYS = WINDOW + Q_BLOCK
F_SEL_BLOCK = 0
F_SEL_OFF = SEL_LEN
F_WIN_OFF = SEL_LEN + 1
F_WIN_BLK = SEL_LEN + 2
F_CMP = SEL_LEN + 3


def _tile_heads(x):
    return jnp.concatenate([x] * NSA_HEADS, axis=0)


def _nsa_kernel(nbs, q_ref, g_ref, kc_ref, vc0_ref, vc1_ref, ks_ref, vs0_ref, vs1_ref,
                kw_ref, vw0_ref, vw1_ref, ov_ref, y_ref,
                qe_scr, posf_scr, wposf_scr, cposf_scr, s_scr, mrun_scr, mb_scr, acc_scr):
    QB = Q_BLOCK
    KT = SEL_KT
    HALF = NSA_REP * QB
    t0 = pl.program_id(1) * QB
    ncmp = kc_ref.shape[1]
    lane = lax.broadcasted_iota(jnp.int32, (1, LANES), 1)
    lo_half = lane < NSA_DH

    @pl.when(pl.program_id(1) == 0)
    def _():
        seq = posf_scr.shape[0]
        k = lax.broadcasted_iota(jnp.int32, (seq, LANES), 0)
        c = lax.broadcasted_iota(jnp.int32, (seq, LANES), 1)
        posf = jnp.where(c < SEL_LEN, jnp.where(k // SEL_LEN == c, 1.0, 0.0),
                         jnp.where(c == F_SEL_OFF, (k % SEL_LEN).astype(F32), 0.0))
        posf_scr[...] = posf.astype(BF16)
        k = lax.broadcasted_iota(jnp.int32, (WIN_KEYS, LANES), 0)
        c = lax.broadcasted_iota(jnp.int32, (WIN_KEYS, LANES), 1)
        wposf = jnp.where(c == F_WIN_OFF, (k % SEL_LEN).astype(F32),
                          jnp.where(c == F_WIN_BLK, (k // SEL_LEN).astype(F32), 0.0))
        wposf_scr[...] = wposf.astype(BF16)
        k = lax.broadcasted_iota(jnp.int32, (ncmp, LANES), 0)
        c = lax.broadcasted_iota(jnp.int32, (ncmp, LANES), 1)
        cposf_scr[...] = jnp.where(c == F_CMP, k.astype(F32), 0.0).astype(BF16)

    def const_feat(h):
        s = _alibi_slope(h)
        return jnp.where((lane == F_SEL_OFF) | (lane == F_WIN_OFF), s,
                         jnp.where(lane == F_WIN_BLK, SEL_LEN * s,
                                   jnp.where(lane == F_CMP, CMP_STRIDE * s, 0.0)))

    for h in range(NSA_HEADS):
        rows = slice(h * QB, (h + 1) * QB)
        qe_scr[rows, 0:LANES] = q_ref[:, h * LANES:(h + 1) * LANES] * 0.125
        qe_scr[rows, LANES:2 * LANES] = jnp.broadcast_to(const_feat(h), (QB, LANES)).astype(BF16)

    def normalised(o, guard):
        outs = []
        for pair in range(NSA_HEADS // 2):
            e = o[(2 * pair) * QB:(2 * pair + 1) * QB]
            d = o[(2 * pair + 1) * QB:(2 * pair + 2) * QB]
            num = jnp.where(lo_half, e, pltpu.roll(d, NSA_DH, 1))
            den = jnp.where(lo_half, pltpu.roll(e, NSA_DH, 1), d)
            if guard:
                den = jnp.where(den > 0.0, den, 1.0)
            outs.append(num / den)
        return outs

    ws = pl.multiple_of(jnp.maximum(t0 - WINDOW, 0), QB)
    qs = lax.broadcasted_iota(jnp.int32, (QB, WIN_KEYS), 0)
    kl = lax.broadcasted_iota(jnp.int32, (QB, WIN_KEYS), 1)
    dist_w = (t0 - ws) + qs - kl
    ok_w = (dist_w >= 0) & (dist_w < WINDOW)
    s_w = _dot_nt(qe_scr[...], jnp.concatenate([kw_ref[pl.ds(ws, WIN_KEYS), :], wposf_scr[...]], axis=1))

    q_sub = lax.broadcasted_iota(jnp.int32, (QB, ncmp), 0)
    n_lane = lax.broadcasted_iota(jnp.int32, (QB, ncmp), 1)
    ok_c = n_lane * CMP_STRIDE + (CMP_LEN - 1) <= t0 + q_sub
    s = _dot_nt(qe_scr[...], jnp.concatenate([kc_ref[0], cposf_scr[...]], axis=1))
    s = s + _tile_heads(jnp.where(ok_c, 0.0, NEG))
    e = jnp.exp(s - jnp.max(s, axis=-1, keepdims=True)) * _tile_heads(jnp.where(ok_c, 1.0, 0.0))
    e_bf = e.astype(BF16)
    oc = jnp.concatenate([_dot(e_bf[0:HALF], vc0_ref[0]), _dot(e_bf[HALF:2 * HALF], vc1_ref[0])],
                         axis=0)
    o_cmp = normalised(oc, True)
    l_c = jnp.where(lo_half, pltpu.roll(oc, NSA_DH, 1), oc)
    inv_l = 1.0 / jnp.where(l_c > 0.0, l_c, 1.0)
    p_c = e * jnp.concatenate([inv_l] * (ncmp // LANES), axis=1)

    imp_ts = []
    for g in range(NSA_KV):
        psum = p_c[(g * NSA_REP) * QB:(g * NSA_REP + 1) * QB]
        for r in range(1, NSA_REP):
            psum = psum + p_c[(g * NSA_REP + r) * QB:(g * NSA_REP + r + 1) * QB]
        p_hi = psum.astype(BF16)
        p_lo = (psum - p_hi.astype(F32)).astype(BF16)
        imp = _dot(p_hi, ov_ref[...]) + _dot(p_lo, ov_ref[...])
        imp_ts.append(imp.T[0:nbs, :])

    n_grp = nbs // SUBLANES
    sub8 = lax.broadcasted_iota(jnp.int32, (SUBLANES, QB), 0)
    tpos = t0 + lax.broadcasted_iota(jnp.int32, (1, QB), 1)
    cur = tpos // SEL_LEN
    j_lane = lane.astype(F32)
    for g in range(NSA_KV):
        imp_t = imp_ts[g]
        j_sub = lax.broadcasted_iota(jnp.int32, (nbs, QB), 0)
        forced = (j_sub == 0) | (j_sub == cur) | (j_sub == cur - 1)
        score = jnp.where(j_sub <= cur, imp_t + jnp.where(forced, FORCE_BONUS, 0.0), NEG)
        parts = [score[a * SUBLANES:(a + 1) * SUBLANES, :] for a in range(n_grp)]
        ranks = [jnp.zeros((SUBLANES, QB), F32) for _ in range(n_grp)]
        for i in range(nbs):
            row = jnp.broadcast_to(score[i:i + 1, :], (SUBLANES, QB))
            for a in range(n_grp):
                if a * SUBLANES > i:
                    beats = jnp.where(row >= parts[a], 1.0, 0.0)
                elif (a + 1) * SUBLANES - 1 < i:
                    beats = jnp.where(row > parts[a], 1.0, 0.0)
                else:
                    beats = jnp.where(sub8 + a * SUBLANES > i,
                                      jnp.where(row >= parts[a], 1.0, 0.0),
                                      jnp.where(row > parts[a], 1.0, 0.0))
                ranks[a] = ranks[a] + beats
        sel_t = [jnp.where(rk < float(min(SEL_TOP, nbs)), 0.0, NEG) for rk in ranks]
        if nbs < LANES:
            sel_t.append(jnp.zeros((LANES - nbs, QB), F32))
        sel_bias = jnp.concatenate(sel_t, axis=0).T
        for r in range(NSA_REP):
            h = g * NSA_REP + r
            feat = jnp.where(lo_half, sel_bias + (_alibi_slope(h) * SEL_LEN) * j_lane, const_feat(h))
            qe_scr[h * QB:(h + 1) * QB, LANES:2 * LANES] = feat.astype(BF16)

    s_w = s_w + _tile_heads(jnp.where(ok_w, 0.0, NEG))
    p_w = jnp.exp(s_w - jnp.max(s_w, axis=-1, keepdims=True)).astype(BF16)
    o_win = normalised(jnp.concatenate([_dot(p_w[0:HALF], vw0_ref[pl.ds(ws, WIN_KEYS), :]),
                                        _dot(p_w[HALF:2 * HALF], vw1_ref[pl.ds(ws, WIN_KEYS), :])],
                                       axis=0), False)

    last_tile = t0 // KT
    mrun_scr[...] = jnp.full(mrun_scr.shape, NEG, F32)

    U = SEL_GROUP

    def score_group(kt0, count, causal):
        k0 = pl.multiple_of(kt0 * KT, KT)
        kx = jnp.concatenate([ks_ref[pl.ds(k0, count * KT), :],
                              posf_scr[pl.ds(k0, count * KT), :]], axis=1)
        s = _dot_nt(qe_scr[...], kx)
        tile_max = None
        for j in range(count):
            sj = s[:, j * KT:(j + 1) * KT] * LOG2_E
            if causal and j == count - 1:
                qs = lax.broadcasted_iota(jnp.int32, (QB, KT), 0)
                kl = lax.broadcasted_iota(jnp.int32, (QB, KT), 1)
                sj = sj + _tile_heads(jnp.where(k0 + j * KT + kl <= t0 + qs, 0.0, NEG))
            s_scr[kt0 + j] = sj
            for c0 in range(0, KT, LANES):
                part = sj[:, c0:c0 + LANES]
                tile_max = part if tile_max is None else jnp.maximum(tile_max, part)
        mrun_scr[...] = jnp.maximum(mrun_scr[...], tile_max)

    def score_full_group(i, carry):
        score_group(i * U, U, False)
        return carry

    lax.fori_loop(0, last_tile // U, score_full_group, 0)
    for r in range(U):
        @pl.when(last_tile % U == r)
        def _(r=r):
            score_group(last_tile - r, r + 1, True)

    mb_scr[...] = jnp.broadcast_to(jnp.max(mrun_scr[...], axis=-1, keepdims=True), mb_scr.shape)
    acc_scr[...] = jnp.zeros(acc_scr.shape, F32)

    def value_group(kt0, count):
        k0 = pl.multiple_of(kt0 * KT, KT)
        mb = mb_scr[...]
        mbw = jnp.concatenate([mb] * (KT // LANES), axis=1)
        p = jnp.concatenate([jnp.exp2(s_scr[kt0 + j] - mbw).astype(BF16) for j in range(count)],
                            axis=1)
        acc_scr[0:HALF, :] += _dot(p[0:HALF], vs0_ref[pl.ds(k0, count * KT), :])
        acc_scr[HALF:2 * HALF, :] += _dot(p[HALF:2 * HALF], vs1_ref[pl.ds(k0, count * KT), :])

    def value_full_group(i, carry):
        value_group(i * U, U)
        return carry

    n_tiles = last_tile + 1
    lax.fori_loop(0, n_tiles // U, value_full_group, 0)
    for r in range(1, U):
        @pl.when(n_tiles % U == r)
        def _(r=r):
            value_group(n_tiles - r, r)

    o_sel = normalised(acc_scr[...], False)

    gate = jax.nn.sigmoid(g_ref[...])
    for pair in range(NSA_HEADS // 2):
        y = None
        for b, o in enumerate((o_cmp, o_sel, o_win)):
            c_e = SUBLANES + 3 * (2 * pair) + b
            c_o = SUBLANES + 3 * (2 * pair + 1) + b
            term = jnp.where(lo_half, gate[:, c_e:c_e + 1], gate[:, c_o:c_o + 1]) * o[pair]
            y = term if y is None else y + term
        y_ref[:, pair * LANES:(pair + 1) * LANES] = y.astype(y_ref.dtype)


def _nsa(pr, kc, vc0, vc1, batch, seq):
    QB = Q_BLOCK
    nq = seq // QB
    nbs = seq // SEL_LEN
    ncmp = kc.shape[1]
    cs = np.arange(ncmp) * CMP_STRIDE
    js = np.arange(nbs) * SEL_LEN
    ov = np.zeros((ncmp, LANES), np.float32)
    ov[:, :nbs] = (cs[:, None] < js[None, :] + SEL_LEN) & (cs[:, None] + CMP_LEN > js[None, :])
    tok = lambda b, i: (b * nq + i, 0)
    per_b = lambda b, i: (b, 0)
    per_b3 = lambda b, i: (b, 0, 0)
    kv_spec = pl.BlockSpec((seq, LANES), per_b)
    cmp_spec = pl.BlockSpec((1, ncmp, LANES), per_b3)
    return pl.pallas_call(
        functools.partial(_nsa_kernel, nbs),
        out_shape=jax.ShapeDtypeStruct((batch * seq, NSA_WIDTH), BF16),
        grid=(batch, nq),
        in_specs=[pl.BlockSpec((QB, NSA_HEADS * LANES), tok),
                  pl.BlockSpec((QB, LANES), tok),
                  cmp_spec, cmp_spec, cmp_spec,
                  kv_spec, kv_spec, kv_spec, kv_spec, kv_spec, kv_spec,
                  pl.BlockSpec((ncmp, LANES), lambda b, i: (0, 0))],
        out_specs=pl.BlockSpec((QB, NSA_WIDTH), tok),
        scratch_shapes=[pltpu.VMEM((NSA_ROWS, 2 * LANES), BF16),
                        pltpu.VMEM((seq, LANES), BF16),
                        pltpu.VMEM((WIN_KEYS, LANES), BF16),
                        pltpu.VMEM((ncmp, LANES), BF16),
                        pltpu.VMEM((seq // SEL_KT, NSA_ROWS, SEL_KT), F32),
                        pltpu.VMEM((NSA_ROWS, LANES), F32),
                        pltpu.VMEM((NSA_ROWS, LANES), F32),
                        pltpu.VMEM((NSA_ROWS, LANES), F32)],
        compiler_params=pltpu.CompilerParams(dimension_semantics=("parallel", "arbitrary"),
                                             vmem_limit_bytes=VMEM_LIMIT),
        name="nsa",
    )(pr["nq"], pr["gcol"], kc, vc0, vc1, pr["ks"], pr["vs0"], pr["vs1"],
      pr["kw"], pr["vw0"], pr["vw1"], jnp.asarray(ov, BF16))


def _memkv_kernel(mem_ref, g_ref, w_ref, kv_ref):
    u = _rms(mem_ref[0], g_ref[...]).astype(BF16)
    kv_ref[0] = _dot(u, w_ref[...]).astype(kv_ref.dtype)


def _memkv(mem, g_mem, w_mem_kv):
    batch, n_mem, d = mem.shape
    return pl.pallas_call(
        _memkv_kernel,
        out_shape=jax.ShapeDtypeStruct((batch, n_mem, 2 * XA_WIDTH), BF16),
        grid=(batch,),
        in_specs=[pl.BlockSpec((1, n_mem, d), lambda b: (b, 0, 0)),
                  _resident((1, d)), _resident((d, 2 * XA_WIDTH))],
        out_specs=pl.BlockSpec((1, n_mem, 2 * XA_WIDTH), lambda b: (b, 0, 0)),
        compiler_params=pltpu.CompilerParams(dimension_semantics=("parallel",),
                                             vmem_limit_bytes=VMEM_LIMIT),
        name="memkv",
    )(mem, g_mem[None, :], w_mem_kv.astype(BF16))


def _tail_kernel(x_ref, yml_ref, ynsa_ref, xq_ref, kv_ref, mg_ref, wb_ref, wo_ref, gf_ref,
                 w1_ref, w2_ref, gl_ref, out_ref):
    d = x_ref.shape[1]
    kv = kv_ref[0]
    heads = [slice(hh * XA_DH, (hh + 1) * XA_DH) for hh in range(XA_HEADS)]
    scores = [_dot_nt(xq_ref[:, sl], kv[:, sl]) * (XA_DH ** -0.5) for sl in heads]
    ups = [_dot(yml_ref[...], wb_ref[0]), _dot(ynsa_ref[...], wb_ref[1])]
    y_xa = []
    for hh, s in enumerate(scores):
        p = jnp.exp(s - jnp.max(s, axis=-1, keepdims=True))
        l = jnp.sum(p, axis=-1, keepdims=True)
        y_xa.append((_dot(p.astype(BF16), kv[:, XA_WIDTH + hh * XA_DH:XA_WIDTH + (hh + 1) * XA_DH])
                     / l).astype(BF16))
    ups.append(_dot(jnp.concatenate(y_xa, axis=-1), wb_ref[2]))
    merged = None
    for j in range(N_BRANCH):
        term = mg_ref[:, j * d:(j + 1) * d] * ups[j]
        merged = term if merged is None else merged + term
    h = x_ref[...] + _dot(merged.astype(BF16), wo_ref[...])
    u = _rms(h, gf_ref[...]).astype(BF16)
    acc = h
    for c0 in range(0, w1_ref.shape[1], FF_SLAB):
        a = jnp.maximum(_dot(u, w1_ref[:, c0:c0 + FF_SLAB]), 0.0)
        acc = acc + _dot((a * a).astype(BF16), w2_ref[c0:c0 + FF_SLAB, :])
    out_ref[...] = _rms(acc, gl_ref[...])


def _tail(x2, y_ml, y_nsa, pr, mem_kv, w_branch, w_out, g_ffn, w_ff1, w_ff2, g_final, seq):
    n_tok, d = x2.shape
    tm = TM_TAIL
    d_ff = w_ff1.shape[1]
    n_mem = mem_kv.shape[1]
    tiles_per_b = seq // tm
    row = lambda i: (i, 0)
    return pl.pallas_call(
        _tail_kernel,
        out_shape=jax.ShapeDtypeStruct((n_tok, d), F32),
        grid=(n_tok // tm,),
        in_specs=[pl.BlockSpec((tm, d), row),
                  pl.BlockSpec((tm, ML_WIDTH), row),
                  pl.BlockSpec((tm, NSA_WIDTH), row),
                  pl.BlockSpec((tm, XA_WIDTH), row),
                  pl.BlockSpec((1, n_mem, 2 * XA_WIDTH), lambda i: (i // tiles_per_b, 0, 0)),
                  pl.BlockSpec((tm, N_BRANCH * d), row),
                  _resident((N_BRANCH, ML_WIDTH, d)), _resident((d, d)), _resident((1, d)),
                  _resident((d, d_ff)), _resident((d_ff, d)), _resident((1, d))],
        out_specs=pl.BlockSpec((tm, d), row),
        compiler_params=pltpu.CompilerParams(dimension_semantics=("parallel",),
                                             vmem_limit_bytes=VMEM_LIMIT),
        name="tail",
    )(x2, y_ml, y_nsa, pr["xq"], mem_kv, pr["mg"], w_branch.astype(BF16), w_out.astype(BF16),
      g_ffn[None, :], w_ff1.astype(BF16), w_ff2.astype(BF16), g_final[None, :])


def _layer(x, mem, g_mix, w_in, b_in, ml_conv, ml_norm_g, cmp_pe, cmp_w1, cmp_w2, g_mem, w_mem_kv,
           w_branch, w_out, g_ffn, w_ff1, w_ff2, g_final):
    batch, seq, d = x.shape
    assert seq % (SEL_LEN * SUBLANES) == 0 and seq % SEL_KT == 0 and seq >= WIN_KEYS
    assert seq // SEL_LEN <= NSA_DH, "one selection-block feature lane per block"
    assert seq // CMP_STRIDE <= 256, "compressed token index must be exact in bf16"
    assert seq % TM_PROJ == 0 and seq % TM_TAIL == 0 and TM_PROJ % ML_CHUNK == 0
    x2 = x.reshape(batch * seq, d)
    pr = _inproj(x2, g_mix, w_in, b_in, ml_conv, ml_norm_g, seq)
    y_ml = pr["y_ml"]
    kc, vc0, vc1 = _compress(pr, cmp_pe, cmp_w1, cmp_w2, batch, seq)
    y_nsa = _nsa(pr, kc, vc0, vc1, batch, seq)
    mem_kv = _memkv(mem, g_mem, w_mem_kv)
    out = _tail(x2, y_ml, y_nsa, pr, mem_kv, w_branch, w_out, g_ffn, w_ff1, w_ff2, g_final, seq)
    return out.reshape(batch, seq, d)


def kernel(x, mem, g_mix, w_in, b_in, ml_conv, ml_norm_g, cmp_pe, cmp_w1, cmp_w2, g_mem, w_mem_kv,
           w_branch, w_out, g_ffn, w_ff1, w_ff2, g_final):
    assert g_mix.shape[0] == 1, "single-layer block"
    return _layer(x, mem, g_mix[0], w_in[0], b_in[0], ml_conv[0], ml_norm_g[0], cmp_pe[0],
                  cmp_w1[0], cmp_w2[0], g_mem[0], w_mem_kv[0], w_branch[0], w_out[0], g_ffn[0],
                  w_ff1[0], w_ff2[0], g_final)
```

```python
import functools

import numpy as np
import jax
import jax.numpy as jnp
from jax import lax
from jax.experimental import pallas as pl
from jax.experimental.pallas import tpu as pltpu

F32 = jnp.float32
BF16 = jnp.bfloat16

EPS = 1e-6
NEG = -1e30
LOG2_E = 1.4426950408889634
ML_HEADS = 4
ML_DH = 128
ML_WIDTH = ML_HEADS * ML_DH
ML_CONV = 4
ML_CHUNK = 128
NSA_HEADS = 8
NSA_KV = 2
NSA_REP = NSA_HEADS // NSA_KV
NSA_DH = 64
NSA_WIDTH = NSA_HEADS * NSA_DH
NSA_KV_WIDTH = NSA_KV * NSA_DH
CMP_LEN = 32
CMP_STRIDE = 16
CMP_HIDDEN = 256
SEL_LEN = 64
SEL_TOP = 16
WINDOW = 512
Q_BLOCK = 128
FORCE_BONUS = 1e3
SEL_KT = 256
SEL_GROUP = 16
XA_HEADS = 4
XA_DH = 128
XA_WIDTH = XA_HEADS * XA_DH
N_BRANCH = 3

LANES = 128
SUBLANES = 8
VMEM_LIMIT = 56 * 1024 * 1024

TM_PROJ = 256
TM_TAIL = 256
FF_SLAB = 1024

_NT = (((1,), (1,)), ((), ()))
_TN = (((0,), (0,)), ((), ()))


def _dot(a, b):
    return jnp.dot(a, b, preferred_element_type=F32)


def _dot_nt(a, b):
    return lax.dot_general(a, b, _NT, preferred_element_type=F32)


def _resident(shape):
    nd = len(shape)
    return pl.BlockSpec(shape, lambda *_: (0,) * nd, pipeline_mode=pl.Buffered(1))


def _rms(x, g):
    return x * lax.rsqrt(jnp.mean(x * x, axis=-1, keepdims=True) + EPS) * g


_PROJ_COLS = (
    ("qk", 2 * ML_WIDTH, BF16),
    ("v", ML_WIDTH, BF16),
    ("o", ML_WIDTH, BF16),
    ("gcol", LANES, F32),
    ("kc", NSA_KV_WIDTH, F32),
    ("vc", NSA_KV_WIDTH, F32),
    ("ks", NSA_KV_WIDTH, BF16),
    ("kw", NSA_KV_WIDTH, BF16),
    ("vs0", LANES, BF16),
    ("vs1", LANES, BF16),
    ("vw0", LANES, BF16),
    ("vw1", LANES, BF16),
    ("nq", NSA_HEADS * LANES, BF16),
    ("xq", XA_WIDTH, BF16),
    ("mg", None, BF16),
)
_PROJ_MLSTM_ONLY = ("qk", "v", "o")
_PROJ_CHUNK = 512


def _proj_layout(d_model):
    cols, off = [], 0
    for name, width, dt in _PROJ_COLS:
        width = N_BRANCH * d_model if width is None else width
        cols.append((name, off, width, dt))
        off += width
    return cols, off


def _inproj_kernel(cols, tiles_per_b, x_ref, g_ref, w_ref, b_ref, wt_ref, bt_ref, conv_ref, ng_ref,
                   *refs):
    out_names = [c[0] for c in cols if c[0] not in _PROJ_MLSTM_ONLY]
    out_refs = dict(zip(out_names, refs[:len(out_names)]))
    yml_ref, xbuf, c_scr, m_scr = refs[len(out_names):]
    tm = x_ref.shape[0]
    halo = SUBLANES

    @pl.when(pl.program_id(0) % tiles_per_b == 0)
    def _():
        xbuf[0:halo, :] = jnp.zeros((halo, xbuf.shape[1]), F32)
        c_scr[...] = jnp.zeros(c_scr.shape, F32)
        m_scr[...] = jnp.zeros(m_scr.shape, F32)

    u = _rms(x_ref[...], g_ref[...]).astype(BF16)
    grow = _dot_nt(wt_ref[...], u) + bt_ref[...]
    kept = {name: [] for name in _PROJ_MLSTM_ONLY + ("gcol",)}
    mlstm_ready = max(off + width for name, off, width, _ in cols if name in kept)
    ncols = w_ref.shape[1]
    for c0 in range(0, ncols, _PROJ_CHUNK):
        cw = min(_PROJ_CHUNK, ncols - c0)
        chunk = _dot(u, w_ref[:, c0:c0 + cw]) + b_ref[:, c0:c0 + cw]
        for name, off, width, dt in cols:
            lo, hi = max(off, c0), min(off + width, c0 + cw)
            if lo >= hi:
                continue
            acc = chunk[:, lo - c0:hi - c0]
            cs = slice(lo - off, hi - off)
            if name == "qk":
                xbuf[halo:halo + tm, cs] = acc
                y = acc * conv_ref[0:1, cs]
                for j in range(1, ML_CONV):
                    y = y + xbuf[halo - j:halo - j + tm, cs] * conv_ref[j:j + 1, cs]
                xbuf[0:halo, cs] = xbuf[tm:tm + halo, cs]
                acc = y * jax.nn.sigmoid(y)
                if lo - off >= ML_WIDTH:
                    acc = acc * (ML_DH ** -0.5)
            elif name in ("o", "mg"):
                acc = jax.nn.sigmoid(acc)
            acc = acc.astype(dt)
            if name in kept:
                kept[name].append(acc)
            if name in out_refs:
                out_refs[name][:, cs] = acc
        if c0 < mlstm_ready <= c0 + cw:
            qk, v, og = (jnp.concatenate(kept[n], axis=1) for n in _PROJ_MLSTM_ONLY)
            gcol = kept["gcol"][0]
            c_state = [c_scr[hh] for hh in range(ML_HEADS)]
            m_state = [m_scr[hh:hh + 1, 0:1] for hh in range(ML_HEADS)]
            for r0 in range(0, tm, ML_CHUNK):
                rows = slice(r0, r0 + ML_CHUNK)
                y_ml, c_state, m_state = _mlstm_chunk(qk[rows], v[rows], og[rows], gcol[rows],
                                                      grow[:, rows], ng_ref[...], c_state, m_state)
                yml_ref[rows, :] = y_ml
            for hh in range(ML_HEADS):
                c_scr[hh] = c_state[hh]
                m_scr[hh:hh + 1, :] = jnp.broadcast_to(m_state[hh], (1, m_scr.shape[1]))


def _split_w_in(w_in, b_in):
    widths = (ML_WIDTH, ML_WIDTH, ML_WIDTH, ML_WIDTH, ML_HEADS, ML_HEADS,
              NSA_WIDTH, NSA_KV_WIDTH, NSA_KV_WIDTH, NSA_KV_WIDTH, NSA_KV_WIDTH, NSA_KV_WIDTH,
              NSA_KV_WIDTH, 3 * NSA_HEADS, XA_WIDTH, w_in.shape[1])
    names = ("ml_q", "ml_k", "ml_v", "ml_o", "ml_i", "ml_f", "ns_q", "ns_kc", "ns_vc", "ns_ks",
             "ns_vs", "ns_kw", "ns_vw", "ns_g", "xa_q", "mg")
    out, off = {}, 0
    for n, wd in zip(names, widths):
        end = w_in.shape[1] if n == "mg" else off + wd
        out[n] = (w_in[:, off:end], b_in[off:end])
        off = end
    return out


def _inproj(x2, g_mix, w_in, b_in, ml_conv, ml_norm_g, seq):
    n_tok, d = x2.shape
    cols, ncols = _proj_layout(d)
    p = _split_w_in(w_in, b_in)

    def pad_cols(w, b, width):
        return (jnp.pad(w, ((0, 0), (0, width - w.shape[1]))), jnp.pad(b, (0, width - b.shape[0])))

    wq, bq = p["ns_q"]
    wq = wq.reshape(d, NSA_HEADS, 1, NSA_DH)
    bq = bq.reshape(NSA_HEADS, 1, NSA_DH)
    in_group = jnp.asarray(np.arange(NSA_HEADS)[:, None] // NSA_REP == np.arange(NSA_KV)[None, :],
                           w_in.dtype)[:, :, None]
    wq_slots = wq * in_group
    bq_slots = bq * in_group
    gate_w = jnp.concatenate([p["ml_i"][0], p["ml_f"][0]], axis=1)
    gate_b = jnp.concatenate([p["ml_i"][1], p["ml_f"][1]])
    small_w = jnp.concatenate([gate_w, p["ns_g"][0]], axis=1)
    small_b = jnp.concatenate([gate_b, p["ns_g"][1]])

    def with_ones(wb, g):
        w, b = wb
        w = jnp.pad(w[:, g * NSA_DH:(g + 1) * NSA_DH], ((0, 0), (0, LANES - NSA_DH)))
        b = jnp.concatenate([b[g * NSA_DH:(g + 1) * NSA_DH], jnp.ones((LANES - NSA_DH,), b.dtype)])
        return w, b

    pieces = {
        "qk": (jnp.concatenate([p["ml_q"][0], p["ml_k"][0]], axis=1),
               jnp.concatenate([p["ml_q"][1], p["ml_k"][1]])),
        "o": p["ml_o"], "mg": p["mg"],
        "gcol": pad_cols(small_w, small_b, LANES),
        "kc": p["ns_kc"], "vc": p["ns_vc"], "v": p["ml_v"],
        "nq": (wq_slots.reshape(d, NSA_HEADS * LANES), bq_slots.reshape(NSA_HEADS * LANES)),
        "xq": p["xa_q"], "ks": p["ns_ks"], "kw": p["ns_kw"],
        "vs0": with_ones(p["ns_vs"], 0), "vs1": with_ones(p["ns_vs"], 1),
        "vw0": with_ones(p["ns_vw"], 0), "vw1": with_ones(p["ns_vw"], 1),
    }
    w_cols = jnp.concatenate([pieces[name][0] for name, *_ in cols], axis=1).astype(BF16)
    b_cols = jnp.concatenate([pieces[name][1] for name, *_ in cols])[None, :]
    w_rows = gate_w.T.astype(BF16)
    b_rows = gate_b[:, None]

    tm = TM_PROJ
    tiles_per_b = seq // tm
    outs_hbm = [c for c in cols if c[0] not in _PROJ_MLSTM_ONLY] + [("y_ml", None, ML_WIDTH, BF16)]
    outs = pl.pallas_call(
        functools.partial(_inproj_kernel, cols, tiles_per_b),
        out_shape=[jax.ShapeDtypeStruct((n_tok, width), dt) for _, _, width, dt in outs_hbm],
        grid=(n_tok // tm,),
        in_specs=[pl.BlockSpec((tm, d), lambda i: (i, 0)),
                  _resident((1, d)), _resident((d, ncols)), _resident((1, ncols)),
                  _resident((SUBLANES, d)), _resident((SUBLANES, 1)),
                  _resident((ML_CONV, 2 * ML_WIDTH)), _resident((1, ML_WIDTH))],
        out_specs=[pl.BlockSpec((tm, width), lambda i: (i, 0)) for _, _, width, _ in outs_hbm],
        scratch_shapes=[pltpu.VMEM((tm + SUBLANES, 2 * ML_WIDTH), F32),
                        pltpu.VMEM((ML_HEADS, ML_DH, 2 * ML_DH), F32),
                        pltpu.VMEM((SUBLANES, LANES), F32)],
        compiler_params=pltpu.CompilerParams(dimension_semantics=("arbitrary",),
                                             vmem_limit_bytes=VMEM_LIMIT),
        name="inproj",
    )(x2, g_mix[None, :], w_cols, b_cols, w_rows, b_rows, ml_conv, ml_norm_g[None, :])
    return dict(zip([c[0] for c in outs_hbm], outs))


def _mlstm_chunk(qk, v, og, gcol, grow, ng, c_prev, m_prev):
    L = ML_CHUNK
    r_i = lax.broadcasted_iota(jnp.int32, (L, L), 0)
    c_i = lax.broadcasted_iota(jnp.int32, (L, L), 1)
    causal = c_i <= r_i
    tril = causal.astype(F32)
    triu = (r_i <= c_i).astype(F32)
    b_col = jnp.dot(tril, jax.nn.log_sigmoid(gcol), precision=lax.Precision.HIGHEST,
                    preferred_element_type=F32)
    b_row = jnp.dot(jax.nn.log_sigmoid(grow), triu, precision=lax.Precision.HIGHEST,
                    preferred_element_type=F32)
    ones_col = jnp.where(lax.broadcasted_iota(jnp.int32, (L, ML_DH), 1) == 0, 1.0, 0.0).astype(BF16)

    def operands(hh):
        sl = slice(hh * ML_DH, (hh + 1) * ML_DH)
        k = qk[:, ML_WIDTH + hh * ML_DH:ML_WIDTH + (hh + 1) * ML_DH]
        return qk[:, sl], k, jnp.concatenate([v[:, sl], ones_col], axis=1), c_prev[hh]

    ops = [operands(hh) for hh in range(ML_HEADS)]
    qk_t = [_dot_nt(q, k) for q, k, _, _ in ops]
    inter = [_dot(q, c.astype(BF16)) for q, _, _, c in ops]

    stab, ke_all, s_all = [], [], []
    for hh in range(ML_HEADS):
        li_c = gcol[:, hh:hh + 1]
        b_c = b_col[:, ML_HEADS + hh:ML_HEADS + hh + 1]
        li_r = grow[hh:hh + 1, :]
        b_r = b_row[ML_HEADS + hh:ML_HEADS + hh + 1, :]
        g = b_c[L - 1:L, :]
        rel = jnp.where(causal, li_r - b_r, -jnp.inf)
        mm = jnp.maximum(m_prev[hh], jnp.max(rel, axis=-1, keepdims=True))
        m_t = b_c + mm
        s_all.append((qk_t[hh] * jnp.exp(rel - mm)).astype(BF16))
        w_end = g - b_c + li_c
        m_loc = jnp.max(w_end, axis=0, keepdims=True)
        ke_all.append((ops[hh][1] * jnp.exp(w_end - m_loc)).astype(BF16))
        m_new = jnp.maximum(g + m_prev[hh], m_loc)
        stab.append((m_t, jnp.exp(m_prev[hh] - mm), jnp.exp(g + m_prev[hh] - m_new),
                     jnp.exp(m_loc - m_new), m_new))

    intra = [_dot(s, o[2]) for s, o in zip(s_all, ops)]
    a_c = [lax.dot_general(ke, o[2], _TN, preferred_element_type=F32) for ke, o in zip(ke_all, ops)]

    ys, new_c, new_m = [], [], []
    for hh in range(ML_HEADS):
        sl = slice(hh * ML_DH, (hh + 1) * ML_DH)
        m_t, sc, a, bb, m_new = stab[hh]
        nd = intra[hh] + sc * inter[hh]
        den = nd[:, ML_DH:ML_DH + 1]
        h = nd[:, 0:ML_DH] / jnp.maximum(jnp.abs(den), jnp.exp(-m_t))
        hn = h * lax.rsqrt(jnp.mean(h * h, axis=-1, keepdims=True) + EPS)
        ys.append((og[:, sl] * hn * ng[:, sl]).astype(BF16))
        new_c.append(a * ops[hh][3] + bb * a_c[hh])
        new_m.append(m_new)
    return jnp.concatenate(ys, axis=1), new_c, new_m


def _gelu_tanh(x):
    return 0.5 * x * (1.0 + jnp.tanh(np.sqrt(2.0 / np.pi) * (x + 0.044715 * (x * x * x))))


def _compress_kernel(k_ref, v_ref, pe_ref, w1_ref, w2k_ref, w2v_ref, kc_ref, vc0_ref, vc1_ref):
    nrow = kc_ref.shape[1]
    S = CMP_STRIDE

    def hidden(x_ref, which):
        lo = hi = None
        for j in range(0, S, 2):
            x = [x_ref[pl.ds(j + d, nrow, stride=S), :] for d in range(2)]
            for half in range(CMP_LEN // S):
                l0 = half * S + j
                xp = jnp.concatenate([(x[d] + pe_ref[which, l0 + d:l0 + d + 1, :]).astype(BF16)
                                      for d in range(2)], axis=1)
                t = _dot(xp, w1_ref[which, l0 // 2])
                if half == 0:
                    lo = t if lo is None else lo + t
                else:
                    hi = t if hi is None else hi + t
        return _gelu_tanh(lo + pltpu.roll(hi, nrow - 1, 0)).astype(BF16)

    kc_ref[0] = _dot(hidden(k_ref, 0), w2k_ref[...]).astype(kc_ref.dtype)
    hv = hidden(v_ref, 1)
    ones_half = lax.broadcasted_iota(jnp.int32, (nrow, LANES), 1) >= NSA_DH
    for g, o_ref in enumerate((vc0_ref, vc1_ref)):
        o_ref[0] = jnp.where(ones_half, 1.0, _dot(hv, w2v_ref[g])).astype(o_ref.dtype)


def _compress(pr, cmp_pe, cmp_w1, cmp_w2, batch, seq):
    nrow = seq // CMP_STRIDE
    hid = NSA_KV * CMP_HIDDEN
    w1 = cmp_w1.astype(BF16).reshape(2, CMP_LEN, NSA_DH, CMP_HIDDEN)
    z1 = jnp.zeros_like(w1)
    w1e = jnp.stack([jnp.concatenate([w1, z1], axis=-1), jnp.concatenate([z1, w1], axis=-1)], axis=2)
    w1e = w1e.reshape(2, CMP_LEN // 2, 2 * NSA_KV_WIDTH, hid)
    w2 = cmp_w2.astype(BF16)
    z2 = jnp.zeros_like(w2[0])
    w2k = jnp.concatenate([jnp.concatenate([w2[0], z2], axis=1),
                           jnp.concatenate([z2, w2[0]], axis=1)], axis=0)
    w2v = jnp.stack([jnp.pad(w2[1], ((g * CMP_HIDDEN, (NSA_KV - 1 - g) * CMP_HIDDEN),
                                     (0, LANES - NSA_DH))) for g in range(NSA_KV)])
    pe = jnp.concatenate([cmp_pe] * NSA_KV, axis=-1)
    tok_blk = pl.BlockSpec((seq, NSA_KV_WIDTH), lambda b: (b, 0))
    out_blk = pl.BlockSpec((1, nrow, LANES), lambda b: (b, 0, 0))
    return pl.pallas_call(
        _compress_kernel,
        out_shape=(jax.ShapeDtypeStruct((batch, nrow, LANES), BF16),) * 3,
        grid=(batch,),
        in_specs=[tok_blk, tok_blk,
                  _resident((2, CMP_LEN, NSA_KV_WIDTH)),
                  _resident((2, CMP_LEN // 2, 2 * NSA_KV_WIDTH, hid)),
                  _resident((hid, NSA_KV_WIDTH)),
                  _resident((NSA_KV, hid, LANES))],
        out_specs=(out_blk, out_blk, out_blk),
        compiler_params=pltpu.CompilerParams(dimension_semantics=("parallel",),
                                             vmem_limit_bytes=VMEM_LIMIT),
        name="compress",
    )(pr["kc"], pr["vc"], pe, w1e, w2k, w2v)


def _alibi_slope(h):
    return float(2.0 ** (-8.0 * (h + 1) / NSA_HEADS))


NSA_ROWS = NSA_HEADS * Q_BLOCK
WIN_KEYS = WINDOW + Q_BLOCK
F_SEL_BLOCK = 0
F_SEL_OFF = SEL_LEN
F_WIN_OFF = SEL_LEN + 1
F_WIN_BLK = SEL_LEN + 2
F_CMP = SEL_LEN + 3


def _tile_heads(x):
    return jnp.concatenate([x] * NSA_HEADS, axis=0)


def _nsa_kernel(nbs, q_ref, g_ref, kc_ref, vc0_ref, vc1_ref, ks_ref, vs0_ref, vs1_ref,
                kw_ref, vw0_ref, vw1_ref, ov_ref, y_ref,
                qe_scr, posf_scr, wposf_scr, cposf_scr, s_scr, mrun_scr, mb_scr, acc_scr):
    QB = Q_BLOCK
    KT = SEL_KT
    HALF = NSA_REP * QB
    t0 = pl.program_id(1) * QB
    ncmp = kc_ref.shape[1]
    lane = lax.broadcasted_iota(jnp.int32, (1, LANES), 1)
    lo_half = lane < NSA_DH

    @pl.when(pl.program_id(1) == 0)
    def _():
        seq = posf_scr.shape[0]
        k = lax.broadcasted_iota(jnp.int32, (seq, LANES), 0)
        c = lax.broadcasted_iota(jnp.int32, (seq, LANES), 1)
        posf = jnp.where(c < SEL_LEN, jnp.where(k // SEL_LEN == c, 1.0, 0.0),
                         jnp.where(c == F_SEL_OFF, (k % SEL_LEN).astype(F32), 0.0))
        posf_scr[...] = posf.astype(BF16)
        k = lax.broadcasted_iota(jnp.int32, (WIN_KEYS, LANES), 0)
        c = lax.broadcasted_iota(jnp.int32, (WIN_KEYS, LANES), 1)
        wposf = jnp.where(c == F_WIN_OFF, (k % SEL_LEN).astype(F32),
                          jnp.where(c == F_WIN_BLK, (k // SEL_LEN).astype(F32), 0.0))
        wposf_scr[...] = wposf.astype(BF16)
        k = lax.broadcasted_iota(jnp.int32, (ncmp, LANES), 0)
        c = lax.broadcasted_iota(jnp.int32, (ncmp, LANES), 1)
        cposf_scr[...] = jnp.where(c == F_CMP, k.astype(F32), 0.0).astype(BF16)

    def const_feat(h):
        s = _alibi_slope(h)
        return jnp.where((lane == F_SEL_OFF) | (lane == F_WIN_OFF), s,
                         jnp.where(lane == F_WIN_BLK, SEL_LEN * s,
                                   jnp.where(lane == F_CMP, CMP_STRIDE * s, 0.0)))

    for h in range(NSA_HEADS):
        rows = slice(h * QB, (h + 1) * QB)
        qe_scr[rows, 0:LANES] = q_ref[:, h * LANES:(h + 1) * LANES] * 0.125
        qe_scr[rows, LANES:2 * LANES] = jnp.broadcast_to(const_feat(h), (QB, LANES)).astype(BF16)

    def normalised(o, guard):
        outs = []
        for pair in range(NSA_HEADS // 2):
            e = o[(2 * pair) * QB:(2 * pair + 1) * QB]
            d = o[(2 * pair + 1) * QB:(2 * pair + 2) * QB]
            num = jnp.where(lo_half, e, pltpu.roll(d, NSA_DH, 1))
            den = jnp.where(lo_half, pltpu.roll(e, NSA_DH, 1), d)
            if guard:
                den = jnp.where(den > 0.0, den, 1.0)
            outs.append(num / den)
        return outs

    ws = pl.multiple_of(jnp.maximum(t0 - WINDOW, 0), QB)
    qs = lax.broadcasted_iota(jnp.int32, (QB, WIN_KEYS), 0)
    kl = lax.broadcasted_iota(jnp.int32, (QB, WIN_KEYS), 1)
    dist_w = (t0 - ws) + qs - kl
    ok_w = (dist_w >= 0) & (dist_w < WINDOW)
    s_w = _dot_nt(qe_scr[...], jnp.concatenate([kw_ref[pl.ds(ws, WIN_KEYS), :], wposf_scr[...]], axis=1))

    q_sub = lax.broadcasted_iota(jnp.int32, (QB, ncmp), 0)
    n_lane = lax.broadcasted_iota(jnp.int32, (QB, ncmp), 1)
    ok_c = n_lane * CMP_STRIDE + (CMP_LEN - 1) <= t0 + q_sub
    s = _dot_nt(qe_scr[...], jnp.concatenate([kc_ref[0], cposf_scr[...]], axis=1))
    s = s + _tile_heads(jnp.where(ok_c, 0.0, NEG))
    e = jnp.exp(s - jnp.max(s, axis=-1, keepdims=True)) * _tile_heads(jnp.where(ok_c, 1.0, 0.0))
    e_bf = e.astype(BF16)
    oc = jnp.concatenate([_dot(e_bf[0:HALF], vc0_ref[0]), _dot(e_bf[HALF:2 * HALF], vc1_ref[0])],
                         axis=0)
    o_cmp = normalised(oc, True)
    l_c = jnp.where(lo_half, pltpu.roll(oc, NSA_DH, 1), oc)
    inv_l = 1.0 / jnp.where(l_c > 0.0, l_c, 1.0)
    p_c = e * jnp.concatenate([inv_l] * (ncmp // LANES), axis=1)

    imp_ts = []
    for g in range(NSA_KV):
        psum = p_c[(g * NSA_REP) * QB:(g * NSA_REP + 1) * QB]
        for r in range(1, NSA_REP):
            psum = psum + p_c[(g * NSA_REP + r) * QB:(g * NSA_REP + r + 1) * QB]
        p_hi = psum.astype(BF16)
        p_lo = (psum - p_hi.astype(F32)).astype(BF16)
        imp = _dot(p_hi, ov_ref[...]) + _dot(p_lo, ov_ref[...])
        imp_ts.append(imp.T[0:nbs, :])

    n_grp = nbs // SUBLANES
    sub8 = lax.broadcasted_iota(jnp.int32, (SUBLANES, QB), 0)
    tpos = t0 + lax.broadcasted_iota(jnp.int32, (1, QB), 1)
    cur = tpos // SEL_LEN
    j_lane = lane.astype(F32)
    for g in range(NSA_KV):
        imp_t = imp_ts[g]
        j_sub = lax.broadcasted_iota(jnp.int32, (nbs, QB), 0)
        forced = (j_sub == 0) | (j_sub == cur) | (j_sub == cur - 1)
        score = jnp.where(j_sub <= cur, imp_t + jnp.where(forced, FORCE_BONUS, 0.0), NEG)
        parts = [score[a * SUBLANES:(a + 1) * SUBLANES, :] for a in range(n_grp)]
        ranks = [jnp.zeros((SUBLANES, QB), F32) for _ in range(n_grp)]
        for i in range(nbs):
            row = jnp.broadcast_to(score[i:i + 1, :], (SUBLANES, QB))
            for a in range(n_grp):
                if a * SUBLANES > i:
                    beats = jnp.where(row >= parts[a], 1.0, 0.0)
                elif (a + 1) * SUBLANES - 1 < i:
                    beats = jnp.where(row > parts[a], 1.0, 0.0)
                else:
                    beats = jnp.where(sub8 + a * SUBLANES > i,
                                      jnp.where(row >= parts[a], 1.0, 0.0),
                                      jnp.where(row > parts[a], 1.0, 0.0))
                ranks[a] = ranks[a] + beats
        sel_t = [jnp.where(rk < float(min(SEL_TOP, nbs)), 0.0, NEG) for rk in ranks]
        if nbs < LANES:
            sel_t.append(jnp.zeros((LANES - nbs, QB), F32))
        sel_bias = jnp.concatenate(sel_t, axis=0).T
        for r in range(NSA_REP):
            h = g * NSA_REP + r
            feat = jnp.where(lo_half, sel_bias + (_alibi_slope(h) * SEL_LEN) * j_lane, const_feat(h))
            qe_scr[h * QB:(h + 1) * QB, LANES:2 * LANES] = feat.astype(BF16)

    s_w = s_w + _tile_heads(jnp.where(ok_w, 0.0, NEG))
    p_w = jnp.exp(s_w - jnp.max(s_w, axis=-1, keepdims=True)).astype(BF16)
    o_win = normalised(jnp.concatenate([_dot(p_w[0:HALF], vw0_ref[pl.ds(ws, WIN_KEYS), :]),
                                        _dot(p_w[HALF:2 * HALF], vw1_ref[pl.ds(ws, WIN_KEYS), :])],
                                       axis=0), False)

    last_tile = t0 // KT
    mrun_scr[...] = jnp.full(mrun_scr.shape, NEG, F32)

    U = SEL_GROUP

    def score_group(kt0, count, causal):
        k0 = pl.multiple_of(kt0 * KT, KT)
        kx = jnp.concatenate([ks_ref[pl.ds(k0, count * KT), :],
                              posf_scr[pl.ds(k0, count * KT), :]], axis=1)
        s = _dot_nt(qe_scr[...], kx)
        tile_max = None
        for j in range(count):
            sj = s[:, j * KT:(j + 1) * KT] * LOG2_E
            if causal and j == count - 1:
                qs = lax.broadcasted_iota(jnp.int32, (QB, KT), 0)
                kl = lax.broadcasted_iota(jnp.int32, (QB, KT), 1)
                sj = sj + _tile_heads(jnp.where(k0 + j * KT + kl <= t0 + qs, 0.0, NEG))
            s_scr[kt0 + j] = sj
            for c0 in range(0, KT, LANES):
                part = sj[:, c0:c0 + LANES]
                tile_max = part if tile_max is None else jnp.maximum(tile_max, part)
        mrun_scr[...] = jnp.maximum(mrun_scr[...], tile_max)

    def score_full_group(i, carry):
        score_group(i * U, U, False)
        return carry

    lax.fori_loop(0, last_tile // U, score_full_group, 0)
    for r in range(U):
        @pl.when(last_tile % U == r)
        def _(r=r):
            score_group(last_tile - r, r + 1, True)

    mb_scr[...] = jnp.broadcast_to(jnp.max(mrun_scr[...], axis=-1, keepdims=True), mb_scr.shape)
    acc_scr[...] = jnp.zeros(acc_scr.shape, F32)

    def value_group(kt0, count):
        k0 = pl.multiple_of(kt0 * KT, KT)
        mb = mb_scr[...]
        mbw = jnp.concatenate([mb] * (KT // LANES), axis=1)
        p = jnp.concatenate([jnp.exp2(s_scr[kt0 + j] - mbw).astype(BF16) for j in range(count)],
                            axis=1)
        acc_scr[0:HALF, :] += _dot(p[0:HALF], vs0_ref[pl.ds(k0, count * KT), :])
        acc_scr[HALF:2 * HALF, :] += _dot(p[HALF:2 * HALF], vs1_ref[pl.ds(k0, count * KT), :])

    def value_full_group(i, carry):
        value_group(i * U, U)
        return carry

    n_tiles = last_tile + 1
    lax.fori_loop(0, n_tiles // U, value_full_group, 0)
    for r in range(1, U):
        @pl.when(n_tiles % U == r)
        def _(r=r):
            value_group(n_tiles - r, r)

    o_sel = normalised(acc_scr[...], False)

    gate = jax.nn.sigmoid(g_ref[...])
    for pair in range(NSA_HEADS // 2):
        y = None
        for b, o in enumerate((o_cmp, o_sel, o_win)):
            c_e = SUBLANES + 3 * (2 * pair) + b
            c_o = SUBLANES + 3 * (2 * pair + 1) + b
            term = jnp.where(lo_half, gate[:, c_e:c_e + 1], gate[:, c_o:c_o + 1]) * o[pair]
            y = term if y is None else y + term
        y_ref[:, pair * LANES:(pair + 1) * LANES] = y.astype(y_ref.dtype)


def _nsa(pr, kc, vc0, vc1, batch, seq):
    QB = Q_BLOCK
    nq = seq // QB
    nbs = seq // SEL_LEN
    ncmp = kc.shape[1]
    cs = np.arange(ncmp) * CMP_STRIDE
    js = np.arange(nbs) * SEL_LEN
    ov = np.zeros((ncmp, LANES), np.float32)
    ov[:, :nbs] = (cs[:, None] < js[None, :] + SEL_LEN) & (cs[:, None] + CMP_LEN > js[None, :])
    tok = lambda b, i: (b * nq + i, 0)
    per_b = lambda b, i: (b, 0)
    per_b3 = lambda b, i: (b, 0, 0)
    kv_spec = pl.BlockSpec((seq, LANES), per_b)
    cmp_spec = pl.BlockSpec((1, ncmp, LANES), per_b3)
    return pl.pallas_call(
        functools.partial(_nsa_kernel, nbs),
        out_shape=jax.ShapeDtypeStruct((batch * seq, NSA_WIDTH), BF16),
        grid=(batch, nq),
        in_specs=[pl.BlockSpec((QB, NSA_HEADS * LANES), tok),
                  pl.BlockSpec((QB, LANES), tok),
                  cmp_spec, cmp_spec, cmp_spec,
                  kv_spec, kv_spec, kv_spec, kv_spec, kv_spec, kv_spec,
                  pl.BlockSpec((ncmp, LANES), lambda b, i: (0, 0))],
        out_specs=pl.BlockSpec((QB, NSA_WIDTH), tok),
        scratch_shapes=[pltpu.VMEM((NSA_ROWS, 2 * LANES), BF16),
                        pltpu.VMEM((seq, LANES), BF16),
                        pltpu.VMEM((WIN_KEYS, LANES), BF16),
                        pltpu.VMEM((ncmp, LANES), BF16),
                        pltpu.VMEM((seq // SEL_KT, NSA_ROWS, SEL_KT), F32),
                        pltpu.VMEM((NSA_ROWS, LANES), F32),
                        pltpu.VMEM((NSA_ROWS, LANES), F32),
                        pltpu.VMEM((NSA_ROWS, LANES), F32)],
        compiler_params=pltpu.CompilerParams(dimension_semantics=("parallel", "arbitrary"),
                                             vmem_limit_bytes=VMEM_LIMIT),
        name="nsa",
    )(pr["nq"], pr["gcol"], kc, vc0, vc1, pr["ks"], pr["vs0"], pr["vs1"],
      pr["kw"], pr["vw0"], pr["vw1"], jnp.asarray(ov, BF16))


def _memkv_kernel(mem_ref, g_ref, w_ref, kv_ref):
    u = _rms(mem_ref[0], g_ref[...]).astype(BF16)
    kv_ref[0] = _dot(u, w_ref[...]).astype(kv_ref.dtype)


def _memkv(mem, g_mem, w_mem_kv):
    batch, n_mem, d = mem.shape
    return pl.pallas_call(
        _memkv_kernel,
        out_shape=jax.ShapeDtypeStruct((batch, n_mem, 2 * XA_WIDTH), BF16),
        grid=(batch,),
        in_specs=[pl.BlockSpec((1, n_mem, d), lambda b: (b, 0, 0)),
                  _resident((1, d)), _resident((d, 2 * XA_WIDTH))],
        out_specs=pl.BlockSpec((1, n_mem, 2 * XA_WIDTH), lambda b: (b, 0, 0)),
        compiler_params=pltpu.CompilerParams(dimension_semantics=("parallel",),
                                             vmem_limit_bytes=VMEM_LIMIT),
        name="memkv",
    )(mem, g_mem[None, :], w_mem_kv.astype(BF16))


def _tail_kernel(x_ref, yml_ref, ynsa_ref, xq_ref, kv_ref, mg_ref, wb_ref, wo_ref, gf_ref,
                 w1_ref, w2_ref, gl_ref, out_ref):
    d = x_ref.shape[1]
    kv = kv_ref[0]
    heads = [slice(hh * XA_DH, (hh + 1) * XA_DH) for hh in range(XA_HEADS)]
    scores = [_dot_nt(xq_ref[:, sl], kv[:, sl]) * (XA_DH ** -0.5) for sl in heads]
    ups = [_dot(yml_ref[...], wb_ref[0]), _dot(ynsa_ref[...], wb_ref[1])]
    y_xa = []
    for hh, s in enumerate(scores):
        p = jnp.exp(s - jnp.max(s, axis=-1, keepdims=True))
        l = jnp.sum(p, axis=-1, keepdims=True)
        y_xa.append((_dot(p.astype(BF16), kv[:, XA_WIDTH + hh * XA_DH:XA_WIDTH + (hh + 1) * XA_DH])
                     / l).astype(BF16))
    ups.append(_dot(jnp.concatenate(y_xa, axis=-1), wb_ref[2]))
    merged = None
    for j in range(N_BRANCH):
        term = mg_ref[:, j * d:(j + 1) * d] * ups[j]
        merged = term if merged is None else merged + term
    h = x_ref[...] + _dot(merged.astype(BF16), wo_ref[...])
    u = _rms(h, gf_ref[...]).astype(BF16)
    acc = h
    for c0 in range(0, w1_ref.shape[1], FF_SLAB):
        a = jnp.maximum(_dot(u, w1_ref[:, c0:c0 + FF_SLAB]), 0.0)
        acc = acc + _dot((a * a).astype(BF16), w2_ref[c0:c0 + FF_SLAB, :])
    out_ref[...] = _rms(acc, gl_ref[...])


def _tail(x2, y_ml, y_nsa, pr, mem_kv, w_branch, w_out, g_ffn, w_ff1, w_ff2, g_final, seq):
    n_tok, d = x2.shape
    tm = TM_TAIL
    d_ff = w_ff1.shape[1]
    n_mem = mem_kv.shape[1]
    tiles_per_b = seq // tm
    row = lambda i: (i, 0)
    return pl.pallas_call(
        _tail_kernel,
        out_shape=jax.ShapeDtypeStruct((n_tok, d), F32),
        grid=(n_tok // tm,),
        in_specs=[pl.BlockSpec((tm, d), row),
                  pl.BlockSpec((tm, ML_WIDTH), row),
                  pl.BlockSpec((tm, NSA_WIDTH), row),
                  pl.BlockSpec((tm, XA_WIDTH), row),
                  pl.BlockSpec((1, n_mem, 2 * XA_WIDTH), lambda i: (i // tiles_per_b, 0, 0)),
                  pl.BlockSpec((tm, N_BRANCH * d), row),
                  _resident((N_BRANCH, ML_WIDTH, d)), _resident((d, d)), _resident((1, d)),
                  _resident((d, d_ff)), _resident((d_ff, d)), _resident((1, d))],
        out_specs=pl.BlockSpec((tm, d), row),
        compiler_params=pltpu.CompilerParams(dimension_semantics=("parallel",),
                                             vmem_limit_bytes=VMEM_LIMIT),
        name="tail",
    )(x2, y_ml, y_nsa, pr["xq"], mem_kv, pr["mg"], w_branch.astype(BF16), w_out.astype(BF16),
      g_ffn[None, :], w_ff1.astype(BF16), w_ff2.astype(BF16), g_final[None, :])


def _layer(x, mem, g_mix, w_in, b_in, ml_conv, ml_norm_g, cmp_pe, cmp_w1, cmp_w2, g_mem, w_mem_kv,
           w_branch, w_out, g_ffn, w_ff1, w_ff2, g_final):
    batch, seq, d = x.shape
    assert seq % (SEL_LEN * SUBLANES) == 0 and seq % SEL_KT == 0 and seq >= WIN_KEYS
    assert seq // SEL_LEN <= NSA_DH, "one selection-block feature lane per block"
    assert seq // CMP_STRIDE <= 256, "compressed token index must be exact in bf16"
    assert seq % TM_PROJ == 0 and seq % TM_TAIL == 0 and TM_PROJ % ML_CHUNK == 0
    x2 = x.reshape(batch * seq, d)
    pr = _inproj(x2, g_mix, w_in, b_in, ml_conv, ml_norm_g, seq)
    y_ml = pr["y_ml"]
    kc, vc0, vc1 = _compress(pr, cmp_pe, cmp_w1, cmp_w2, batch, seq)
    y_nsa = _nsa(pr, kc, vc0, vc1, batch, seq)
    mem_kv = _memkv(mem, g_mem, w_mem_kv)
    out = _tail(x2, y_ml, y_nsa, pr, mem_kv, w_branch, w_out, g_ffn, w_ff1, w_ff2, g_final, seq)
    return out.reshape(batch, seq, d)


def kernel(x, mem, g_mix, w_in, b_in, ml_conv, ml_norm_g, cmp_pe, cmp_w1, cmp_w2, g_mem, w_mem_kv,
           w_branch, w_out, g_ffn, w_ff1, w_ff2, g_final):
    assert g_mix.shape[0] == 1, "single-layer block"
    return _layer(x, mem, g_mix[0], w_in[0], b_in[0], ml_conv[0], ml_norm_g[0], cmp_pe[0],
                  cmp_w1[0], cmp_w2[0], g_mem[0], w_mem_kv[0], w_branch[0], w_out[0], g_ffn[0],
                  w_ff1[0], w_ff2[0], g_final)
```

```python
import functools

import numpy as np
import jax
import jax.numpy as jnp
from jax import lax
from jax.experimental import pallas as pl
from jax.experimental.pallas import tpu as pltpu

F32 = jnp.float32
BF16 = jnp.bfloat16

EPS = 1e-6
NEG = -1e30
LOG2_E = 1.4426950408889634
ML_HEADS = 4
ML_DH = 128
ML_WIDTH = ML_HEADS * ML_DH
ML_CONV = 4
ML_CHUNK = 128
NSA_HEADS = 8
NSA_KV = 2
NSA_REP = NSA_HEADS // NSA_KV
NSA_DH = 64
NSA_WIDTH = NSA_HEADS * NSA_DH
NSA_KV_WIDTH = NSA_KV * NSA_DH
CMP_LEN = 32
CMP_STRIDE = 16
CMP_HIDDEN = 256
SEL_LEN = 64
SEL_TOP = 16
WINDOW = 512
Q_BLOCK = 128
FORCE_BONUS = 1e3
SEL_KT = 256
SEL_GROUP = 8
XA_HEADS = 4
XA_DH = 128
XA_WIDTH = XA_HEADS * XA_DH
N_BRANCH = 3

LANES = 128
SUBLANES = 8
VMEM_LIMIT = 56 * 1024 * 1024

TM_PROJ = 256
TM_TAIL = 256
FF_SLAB = 1024

_NT = (((1,), (1,)), ((), ()))
_TN = (((0,), (0,)), ((), ()))


def _dot(a, b):
    return jnp.dot(a, b, preferred_element_type=F32)


def _dot_nt(a, b):
    return lax.dot_general(a, b, _NT, preferred_element_type=F32)


def _resident(shape):
    nd = len(shape)
    return pl.BlockSpec(shape, lambda *_: (0,) * nd, pipeline_mode=pl.Buffered(1))


def _rms(x, g):
    return x * lax.rsqrt(jnp.mean(x * x, axis=-1, keepdims=True) + EPS) * g


_PROJ_COLS = (
    ("qk", 2 * ML_WIDTH, BF16),
    ("v", ML_WIDTH, BF16),
    ("o", ML_WIDTH, BF16),
    ("gcol", LANES, F32),
    ("kc", NSA_KV_WIDTH, F32),
    ("vc", NSA_KV_WIDTH, F32),
    ("ks", NSA_KV_WIDTH, BF16),
    ("kw", NSA_KV_WIDTH, BF16),
    ("vs0", LANES, BF16),
    ("vs1", LANES, BF16),
    ("vw0", LANES, BF16),
    ("vw1", LANES, BF16),
    ("nq", NSA_HEADS * LANES, BF16),
    ("xq", XA_WIDTH, BF16),
    ("mg", None, BF16),
)
_PROJ_MLSTM_ONLY = ("qk", "v", "o")
_PROJ_CHUNK = 512


def _proj_layout(d_model):
    cols, off = [], 0
    for name, width, dt in _PROJ_COLS:
        width = N_BRANCH * d_model if width is None else width
        cols.append((name, off, width, dt))
        off += width
    return cols, off


def _inproj_kernel(cols, tiles_per_b, x_ref, g_ref, w_ref, b_ref, wt_ref, bt_ref, conv_ref, ng_ref,
                   *refs):
    out_names = [c[0] for c in cols if c[0] not in _PROJ_MLSTM_ONLY]
    out_refs = dict(zip(out_names, refs[:len(out_names)]))
    yml_ref, xbuf, c_scr, m_scr = refs[len(out_names):]
    tm = x_ref.shape[0]
    halo = SUBLANES

    @pl.when(pl.program_id(0) % tiles_per_b == 0)
    def _():
        xbuf[0:halo, :] = jnp.zeros((halo, xbuf.shape[1]), F32)
        c_scr[...] = jnp.zeros(c_scr.shape, F32)
        m_scr[...] = jnp.zeros(m_scr.shape, F32)

    u = _rms(x_ref[...], g_ref[...]).astype(BF16)
    grow = _dot_nt(wt_ref[...], u) + bt_ref[...]
    kept = {name: [] for name in _PROJ_MLSTM_ONLY + ("gcol",)}
    mlstm_ready = max(off + width for name, off, width, _ in cols if name in kept)
    ncols = w_ref.shape[1]
    for c0 in range(0, ncols, _PROJ_CHUNK):
        cw = min(_PROJ_CHUNK, ncols - c0)
        chunk = _dot(u, w_ref[:, c0:c0 + cw]) + b_ref[:, c0:c0 + cw]
        for name, off, width, dt in cols:
            lo, hi = max(off, c0), min(off + width, c0 + cw)
            if lo >= hi:
                continue
            acc = chunk[:, lo - c0:hi - c0]
            cs = slice(lo - off, hi - off)
            if name == "qk":
                xbuf[halo:halo + tm, cs] = acc
                y = acc * conv_ref[0:1, cs]
                for j in range(1, ML_CONV):
                    y = y + xbuf[halo - j:halo - j + tm, cs] * conv_ref[j:j + 1, cs]
                xbuf[0:halo, cs] = xbuf[tm:tm + halo, cs]
                acc = y * jax.nn.sigmoid(y)
                if lo - off >= ML_WIDTH:
                    acc = acc * (ML_DH ** -0.5)
            elif name in ("o", "mg"):
                acc = jax.nn.sigmoid(acc)
            acc = acc.astype(dt)
            if name in kept:
                kept[name].append(acc)
            if name in out_refs:
                out_refs[name][:, cs] = acc
        if c0 < mlstm_ready <= c0 + cw:
            qk, v, og = (jnp.concatenate(kept[n], axis=1) for n in _PROJ_MLSTM_ONLY)
            gcol = kept["gcol"][0]
            c_state = [c_scr[hh] for hh in range(ML_HEADS)]
            m_state = [m_scr[hh:hh + 1, 0:1] for hh in range(ML_HEADS)]
            for r0 in range(0, tm, ML_CHUNK):
                rows = slice(r0, r0 + ML_CHUNK)
                y_ml, c_state, m_state = _mlstm_chunk(qk[rows], v[rows], og[rows], gcol[rows],
                                                      grow[:, rows], ng_ref[...], c_state, m_state)
                yml_ref[rows, :] = y_ml
            for hh in range(ML_HEADS):
                c_scr[hh] = c_state[hh]
                m_scr[hh:hh + 1, :] = jnp.broadcast_to(m_state[hh], (1, m_scr.shape[1]))


def _split_w_in(w_in, b_in):
    widths = (ML_WIDTH, ML_WIDTH, ML_WIDTH, ML_WIDTH, ML_HEADS, ML_HEADS,
              NSA_WIDTH, NSA_KV_WIDTH, NSA_KV_WIDTH, NSA_KV_WIDTH, NSA_KV_WIDTH, NSA_KV_WIDTH,
              NSA_KV_WIDTH, 3 * NSA_HEADS, XA_WIDTH, w_in.shape[1])
    names = ("ml_q", "ml_k", "ml_v", "ml_o", "ml_i", "ml_f", "ns_q", "ns_kc", "ns_vc", "ns_ks",
             "ns_vs", "ns_kw", "ns_vw", "ns_g", "xa_q", "mg")
    out, off = {}, 0
    for n, wd in zip(names, widths):
        end = w_in.shape[1] if n == "mg" else off + wd
        out[n] = (w_in[:, off:end], b_in[off:end])
        off = end
    return out


def _inproj(x2, g_mix, w_in, b_in, ml_conv, ml_norm_g, seq):
    n_tok, d = x2.shape
    cols, ncols = _proj_layout(d)
    p = _split_w_in(w_in, b_in)

    def pad_cols(w, b, width):
        return (jnp.pad(w, ((0, 0), (0, width - w.shape[1]))), jnp.pad(b, (0, width - b.shape[0])))

    wq, bq = p["ns_q"]
    wq = wq.reshape(d, NSA_HEADS, 1, NSA_DH)
    bq = bq.reshape(NSA_HEADS, 1, NSA_DH)
    in_group = jnp.asarray(np.arange(NSA_HEADS)[:, None] // NSA_REP == np.arange(NSA_KV)[None, :],
                           w_in.dtype)[:, :, None]
    wq_slots = wq * in_group
    bq_slots = bq * in_group
    gate_w = jnp.concatenate([p["ml_i"][0], p["ml_f"][0]], axis=1)
    gate_b = jnp.concatenate([p["ml_i"][1], p["ml_f"][1]])
    small_w = jnp.concatenate([gate_w, p["ns_g"][0]], axis=1)
    small_b = jnp.concatenate([gate_b, p["ns_g"][1]])

    def with_ones(wb, g):
        w, b = wb
        w = jnp.pad(w[:, g * NSA_DH:(g + 1) * NSA_DH], ((0, 0), (0, LANES - NSA_DH)))
        b = jnp.concatenate([b[g * NSA_DH:(g + 1) * NSA_DH], jnp.ones((LANES - NSA_DH,), b.dtype)])
        return w, b

    pieces = {
        "qk": (jnp.concatenate([p["ml_q"][0], p["ml_k"][0]], axis=1),
               jnp.concatenate([p["ml_q"][1], p["ml_k"][1]])),
        "o": p["ml_o"], "mg": p["mg"],
        "gcol": pad_cols(small_w, small_b, LANES),
        "kc": p["ns_kc"], "vc": p["ns_vc"], "v": p["ml_v"],
        "nq": (wq_slots.reshape(d, NSA_HEADS * LANES), bq_slots.reshape(NSA_HEADS * LANES)),
        "xq": p["xa_q"], "ks": p["ns_ks"], "kw": p["ns_kw"],
        "vs0": with_ones(p["ns_vs"], 0), "vs1": with_ones(p["ns_vs"], 1),
        "vw0": with_ones(p["ns_vw"], 0), "vw1": with_ones(p["ns_vw"], 1),
    }
    w_cols = jnp.concatenate([pieces[name][0] for name, *_ in cols], axis=1).astype(BF16)
    b_cols = jnp.concatenate([pieces[name][1] for name, *_ in cols])[None, :]
    w_rows = gate_w.T.astype(BF16)
    b_rows = gate_b[:, None]

    tm = TM_PROJ
    tiles_per_b = seq // tm
    outs_hbm = [c for c in cols if c[0] not in _PROJ_MLSTM_ONLY] + [("y_ml", None, ML_WIDTH, BF16)]
    outs = pl.pallas_call(
        functools.partial(_inproj_kernel, cols, tiles_per_b),
        out_shape=[jax.ShapeDtypeStruct((n_tok, width), dt) for _, _, width, dt in outs_hbm],
        grid=(n_tok // tm,),
        in_specs=[pl.BlockSpec((tm, d), lambda i: (i, 0)),
                  _resident((1, d)), _resident((d, ncols)), _resident((1, ncols)),
                  _resident((SUBLANES, d)), _resident((SUBLANES, 1)),
                  _resident((ML_CONV, 2 * ML_WIDTH)), _resident((1, ML_WIDTH))],
        out_specs=[pl.BlockSpec((tm, width), lambda i: (i, 0)) for _, _, width, _ in outs_hbm],
        scratch_shapes=[pltpu.VMEM((tm + SUBLANES, 2 * ML_WIDTH), F32),
                        pltpu.VMEM((ML_HEADS, ML_DH, 2 * ML_DH), F32),
                        pltpu.VMEM((SUBLANES, LANES), F32)],
        compiler_params=pltpu.CompilerParams(dimension_semantics=("arbitrary",),
                                             vmem_limit_bytes=VMEM_LIMIT),
        name="inproj",
    )(x2, g_mix[None, :], w_cols, b_cols, w_rows, b_rows, ml_conv, ml_norm_g[None, :])
    return dict(zip([c[0] for c in outs_hbm], outs))


def _mlstm_chunk(qk, v, og, gcol, grow, ng, c_prev, m_prev):
    L = ML_CHUNK
    r_i = lax.broadcasted_iota(jnp.int32, (L, L), 0)
    c_i = lax.broadcasted_iota(jnp.int32, (L, L), 1)
    causal = c_i <= r_i
    tril = causal.astype(F32)
    triu = (r_i <= c_i).astype(F32)
    b_col = jnp.dot(tril, jax.nn.log_sigmoid(gcol), precision=lax.Precision.HIGHEST,
                    preferred_element_type=F32)
    b_row = jnp.dot(jax.nn.log_sigmoid(grow), triu, precision=lax.Precision.HIGHEST,
                    preferred_element_type=F32)
    ones_col = jnp.where(lax.broadcasted_iota(jnp.int32, (L, ML_DH), 1) == 0, 1.0, 0.0).astype(BF16)

    def operands(hh):
        sl = slice(hh * ML_DH, (hh + 1) * ML_DH)
        k = qk[:, ML_WIDTH + hh * ML_DH:ML_WIDTH + (hh + 1) * ML_DH]
        return qk[:, sl], k, jnp.concatenate([v[:, sl], ones_col], axis=1), c_prev[hh]

    ops = [operands(hh) for hh in range(ML_HEADS)]
    qk_t = [_dot_nt(q, k) for q, k, _, _ in ops]
    inter = [_dot(q, c.astype(BF16)) for q, _, _, c in ops]

    stab, ke_all, s_all = [], [], []
    for hh in range(ML_HEADS):
        li_c = gcol[:, hh:hh + 1]
        b_c = b_col[:, ML_HEADS + hh:ML_HEADS + hh + 1]
        li_r = grow[hh:hh + 1, :]
        b_r = b_row[ML_HEADS + hh:ML_HEADS + hh + 1, :]
        g = b_c[L - 1:L, :]
        rel = jnp.where(causal, li_r - b_r, -jnp.inf)
        mm = jnp.maximum(m_prev[hh], jnp.max(rel, axis=-1, keepdims=True))
        m_t = b_c + mm
        s_all.append((qk_t[hh] * jnp.exp(rel - mm)).astype(BF16))
        w_end = g - b_c + li_c
        m_loc = jnp.max(w_end, axis=0, keepdims=True)
        ke_all.append((ops[hh][1] * jnp.exp(w_end - m_loc)).astype(BF16))
        m_new = jnp.maximum(g + m_prev[hh], m_loc)
        stab.append((m_t, jnp.exp(m_prev[hh] - mm), jnp.exp(g + m_prev[hh] - m_new),
                     jnp.exp(m_loc - m_new), m_new))

    intra = [_dot(s, o[2]) for s, o in zip(s_all, ops)]
    a_c = [lax.dot_general(ke, o[2], _TN, preferred_element_type=F32) for ke, o in zip(ke_all, ops)]

    ys, new_c, new_m = [], [], []
    for hh in range(ML_HEADS):
        sl = slice(hh * ML_DH, (hh + 1) * ML_DH)
        m_t, sc, a, bb, m_new = stab[hh]
        nd = intra[hh] + sc * inter[hh]
        den = nd[:, ML_DH:ML_DH + 1]
        h = nd[:, 0:ML_DH] / jnp.maximum(jnp.abs(den), jnp.exp(-m_t))
        hn = h * lax.rsqrt(jnp.mean(h * h, axis=-1, keepdims=True) + EPS)
        ys.append((og[:, sl] * hn * ng[:, sl]).astype(BF16))
        new_c.append(a * ops[hh][3] + bb * a_c[hh])
        new_m.append(m_new)
    return jnp.concatenate(ys, axis=1), new_c, new_m


def _gelu_tanh(x):
    return 0.5 * x * (1.0 + jnp.tanh(np.sqrt(2.0 / np.pi) * (x + 0.044715 * (x * x * x))))


def _compress_kernel(k_ref, v_ref, pe_ref, w1_ref, w2k_ref, w2v_ref, kc_ref, vc0_ref, vc1_ref):
    nrow = kc_ref.shape[1]
    S = CMP_STRIDE

    def hidden(x_ref, which):
        lo = hi = None
        for j in range(0, S, 2):
            x = [x_ref[pl.ds(j + d, nrow, stride=S), :] for d in range(2)]
            for half in range(CMP_LEN // S):
                l0 = half * S + j
                xp = jnp.concatenate([(x[d] + pe_ref[which, l0 + d:l0 + d + 1, :]).astype(BF16)
                                      for d in range(2)], axis=1)
                t = _dot(xp, w1_ref[which, l0 // 2])
                if half == 0:
                    lo = t if lo is None else lo + t
                else:
                    hi = t if hi is None else hi + t
        return _gelu_tanh(lo + pltpu.roll(hi, nrow - 1, 0)).astype(BF16)

    kc_ref[0] = _dot(hidden(k_ref, 0), w2k_ref[...]).astype(kc_ref.dtype)
    hv = hidden(v_ref, 1)
    ones_half = lax.broadcasted_iota(jnp.int32, (nrow, LANES), 1) >= NSA_DH
    for g, o_ref in enumerate((vc0_ref, vc1_ref)):
        o_ref[0] = jnp.where(ones_half, 1.0, _dot(hv, w2v_ref[g])).astype(o_ref.dtype)


def _compress(pr, cmp_pe, cmp_w1, cmp_w2, batch, seq):
    nrow = seq // CMP_STRIDE
    hid = NSA_KV * CMP_HIDDEN
    w1 = cmp_w1.astype(BF16).reshape(2, CMP_LEN, NSA_DH, CMP_HIDDEN)
    z1 = jnp.zeros_like(w1)
    w1e = jnp.stack([jnp.concatenate([w1, z1], axis=-1), jnp.concatenate([z1, w1], axis=-1)], axis=2)
    w1e = w1e.reshape(2, CMP_LEN // 2, 2 * NSA_KV_WIDTH, hid)
    w2 = cmp_w2.astype(BF16)
    z2 = jnp.zeros_like(w2[0])
    w2k = jnp.concatenate([jnp.concatenate([w2[0], z2], axis=1),
                           jnp.concatenate([z2, w2[0]], axis=1)], axis=0)
    w2v = jnp.stack([jnp.pad(w2[1], ((g * CMP_HIDDEN, (NSA_KV - 1 - g) * CMP_HIDDEN),
                                     (0, LANES - NSA_DH))) for g in range(NSA_KV)])
    pe = jnp.concatenate([cmp_pe] * NSA_KV, axis=-1)
    tok_blk = pl.BlockSpec((seq, NSA_KV_WIDTH), lambda b: (b, 0))
    out_blk = pl.BlockSpec((1, nrow, LANES), lambda b: (b, 0, 0))
    return pl.pallas_call(
        _compress_kernel,
        out_shape=(jax.ShapeDtypeStruct((batch, nrow, LANES), BF16),) * 3,
        grid=(batch,),
        in_specs=[tok_blk, tok_blk,
                  _resident((2, CMP_LEN, NSA_KV_WIDTH)),
                  _resident((2, CMP_LEN // 2, 2 * NSA_KV_WIDTH, hid)),
                  _resident((hid, NSA_KV_WIDTH)),
                  _resident((NSA_KV, hid, LANES))],
        out_specs=(out_blk, out_blk, out_blk),
        compiler_params=pltpu.CompilerParams(dimension_semantics=("parallel",),
                                             vmem_limit_bytes=VMEM_LIMIT),
        name="compress",
    )(pr["kc"], pr["vc"], pe, w1e, w2k, w2v)


def _alibi_slope(h):
    return float(2.0 ** (-8.0 * (h + 1) / NSA_HEADS))


NSA_ROWS = NSA_HEADS * Q_BLOCK
WIN_KEYS = WINDOW + Q_BLOCK
F_SEL_BLOCK = 0
F_SEL_OFF = SEL_LEN
F_WIN_OFF = SEL_LEN + 1
F_WIN_BLK = SEL_LEN + 2
F_CMP = SEL_LEN + 3


def _tile_heads(x):
    return jnp.concatenate([x] * NSA_HEADS, axis=0)


def _nsa_kernel(nbs, q_ref, g_ref, kc_ref, vc0_ref, vc1_ref, ks_ref, vs0_ref, vs1_ref,
                kw_ref, vw0_ref, vw1_ref, ov_ref, y_ref,
                qe_scr, posf_scr, wposf_scr, cposf_scr, s_scr, mrun_scr, mb_scr, acc_scr):
    QB = Q_BLOCK
    KT = SEL_KT
    HALF = NSA_REP * QB
    t0 = pl.program_id(1) * QB
    ncmp = kc_ref.shape[1]
    lane = lax.broadcasted_iota(jnp.int32, (1, LANES), 1)
    lo_half = lane < NSA_DH

    @pl.when(pl.program_id(1) == 0)
    def _():
        seq = posf_scr.shape[0]
        k = lax.broadcasted_iota(jnp.int32, (seq, LANES), 0)
        c = lax.broadcasted_iota(jnp.int32, (seq, LANES), 1)
        posf = jnp.where(c < SEL_LEN, jnp.where(k // SEL_LEN == c, 1.0, 0.0),
                         jnp.where(c == F_SEL_OFF, (k % SEL_LEN).astype(F32), 0.0))
        posf_scr[...] = posf.astype(BF16)
        k = lax.broadcasted_iota(jnp.int32, (WIN_KEYS, LANES), 0)
        c = lax.broadcasted_iota(jnp.int32, (WIN_KEYS, LANES), 1)
        wposf = jnp.where(c == F_WIN_OFF, (k % SEL_LEN).astype(F32),
                          jnp.where(c == F_WIN_BLK, (k // SEL_LEN).astype(F32), 0.0))
        wposf_scr[...] = wposf.astype(BF16)
        k = lax.broadcasted_iota(jnp.int32, (ncmp, LANES), 0)
        c = lax.broadcasted_iota(jnp.int32, (ncmp, LANES), 1)
        cposf_scr[...] = jnp.where(c == F_CMP, k.astype(F32), 0.0).astype(BF16)

    def const_feat(h):
        s = _alibi_slope(h)
        return jnp.where((lane == F_SEL_OFF) | (lane == F_WIN_OFF), s,
                         jnp.where(lane == F_WIN_BLK, SEL_LEN * s,
                                   jnp.where(lane == F_CMP, CMP_STRIDE * s, 0.0)))

    for h in range(NSA_HEADS):
        rows = slice(h * QB, (h + 1) * QB)
        qe_scr[rows, 0:LANES] = q_ref[:, h * LANES:(h + 1) * LANES] * 0.125
        qe_scr[rows, LANES:2 * LANES] = jnp.broadcast_to(const_feat(h), (QB, LANES)).astype(BF16)

    def normalised(o, guard):
        outs = []
        for pair in range(NSA_HEADS // 2):
            e = o[(2 * pair) * QB:(2 * pair + 1) * QB]
            d = o[(2 * pair + 1) * QB:(2 * pair + 2) * QB]
            num = jnp.where(lo_half, e, pltpu.roll(d, NSA_DH, 1))
            den = jnp.where(lo_half, pltpu.roll(e, NSA_DH, 1), d)
            if guard:
                den = jnp.where(den > 0.0, den, 1.0)
            outs.append(num / den)
        return outs

    ws = pl.multiple_of(jnp.maximum(t0 - WINDOW, 0), QB)
    qs = lax.broadcasted_iota(jnp.int32, (QB, WIN_KEYS), 0)
    kl = lax.broadcasted_iota(jnp.int32, (QB, WIN_KEYS), 1)
    dist_w = (t0 - ws) + qs - kl
    ok_w = (dist_w >= 0) & (dist_w < WINDOW)
    s_w = _dot_nt(qe_scr[...], jnp.concatenate([kw_ref[pl.ds(ws, WIN_KEYS), :], wposf_scr[...]], axis=1))

    q_sub = lax.broadcasted_iota(jnp.int32, (QB, ncmp), 0)
    n_lane = lax.broadcasted_iota(jnp.int32, (QB, ncmp), 1)
    ok_c = n_lane * CMP_STRIDE + (CMP_LEN - 1) <= t0 + q_sub
    s = _dot_nt(qe_scr[...], jnp.concatenate([kc_ref[0], cposf_scr[...]], axis=1))
    s = s + _tile_heads(jnp.where(ok_c, 0.0, NEG))
    e = jnp.exp(s - jnp.max(s, axis=-1, keepdims=True)) * _tile_heads(jnp.where(ok_c, 1.0, 0.0))
    e_bf = e.astype(BF16)
    oc = jnp.concatenate([_dot(e_bf[0:HALF], vc0_ref[0]), _dot(e_bf[HALF:2 * HALF], vc1_ref[0])],
                         axis=0)
    o_cmp = normalised(oc, True)
    l_c = jnp.where(lo_half, pltpu.roll(oc, NSA_DH, 1), oc)
    inv_l = 1.0 / jnp.where(l_c > 0.0, l_c, 1.0)
    p_c = e * jnp.concatenate([inv_l] * (ncmp // LANES), axis=1)

    imp_ts = []
    for g in range(NSA_KV):
        psum = p_c[(g * NSA_REP) * QB:(g * NSA_REP + 1) * QB]
        for r in range(1, NSA_REP):
            psum = psum + p_c[(g * NSA_REP + r) * QB:(g * NSA_REP + r + 1) * QB]
        p_hi = psum.astype(BF16)
        p_lo = (psum - p_hi.astype(F32)).astype(BF16)
        imp = _dot(p_hi, ov_ref[...]) + _dot(p_lo, ov_ref[...])
        imp_ts.append(imp.T[0:nbs, :])

    n_grp = nbs // SUBLANES
    sub8 = lax.broadcasted_iota(jnp.int32, (SUBLANES, QB), 0)
    tpos = t0 + lax.broadcasted_iota(jnp.int32, (1, QB), 1)
    cur = tpos // SEL_LEN
    j_lane = lane.astype(F32)
    for g in range(NSA_KV):
        imp_t = imp_ts[g]
        j_sub = lax.broadcasted_iota(jnp.int32, (nbs, QB), 0)
        forced = (j_sub == 0) | (j_sub == cur) | (j_sub == cur - 1)
        score = jnp.where(j_sub <= cur, imp_t + jnp.where(forced, FORCE_BONUS, 0.0), NEG)
        parts = [score[a * SUBLANES:(a + 1) * SUBLANES, :] for a in range(n_grp)]
        ranks = [jnp.zeros((SUBLANES, QB), F32) for _ in range(n_grp)]
        for i in range(nbs):
            row = jnp.broadcast_to(score[i:i + 1, :], (SUBLANES, QB))
            for a in range(n_grp):
                if a * SUBLANES > i:
                    beats = jnp.where(row >= parts[a], 1.0, 0.0)
                elif (a + 1) * SUBLANES - 1 < i:
                    beats = jnp.where(row > parts[a], 1.0, 0.0)
                else:
                    beats = jnp.where(sub8 + a * SUBLANES > i,
                                      jnp.where(row >= parts[a], 1.0, 0.0),
                                      jnp.where(row > parts[a], 1.0, 0.0))
                ranks[a] = ranks[a] + beats
        sel_t = [jnp.where(rk < float(min(SEL_TOP, nbs)), 0.0, NEG) for rk in ranks]
        if nbs < LANES:
            sel_t.append(jnp.zeros((LANES - nbs, QB), F32))
        sel_bias = jnp.concatenate(sel_t, axis=0).T
        for r in range(NSA_REP):
            h = g * NSA_REP + r
            feat = jnp.where(lo_half, sel_bias + (_alibi_slope(h) * SEL_LEN) * j_lane, const_feat(h))
            qe_scr[h * QB:(h + 1) * QB, LANES:2 * LANES] = feat.astype(BF16)

    s_w = s_w + _tile_heads(jnp.where(ok_w, 0.0, NEG))
    p_w = jnp.exp(s_w - jnp.max(s_w, axis=-1, keepdims=True)).astype(BF16)
    o_win = normalised(jnp.concatenate([_dot(p_w[0:HALF], vw0_ref[pl.ds(ws, WIN_KEYS), :]),
                                        _dot(p_w[HALF:2 * HALF], vw1_ref[pl.ds(ws, WIN_KEYS), :])],
                                       axis=0), False)

    last_tile = t0 // KT
    mrun_scr[...] = jnp.full(mrun_scr.shape, NEG, F32)

    U = SEL_GROUP

    def score_group(kt0, count, causal):
        k0 = pl.multiple_of(kt0 * KT, KT)
        kx = jnp.concatenate([ks_ref[pl.ds(k0, count * KT), :],
                              posf_scr[pl.ds(k0, count * KT), :]], axis=1)
        s = _dot_nt(qe_scr[...], kx)
        tile_max = None
        for j in range(count):
            sj = s[:, j * KT:(j + 1) * KT] * LOG2_E
            if causal and j == count - 1:
                qs = lax.broadcasted_iota(jnp.int32, (QB, KT), 0)
                kl = lax.broadcasted_iota(jnp.int32, (QB, KT), 1)
                sj = sj + _tile_heads(jnp.where(k0 + j * KT + kl <= t0 + qs, 0.0, NEG))
            s_scr[kt0 + j] = sj
            for c0 in range(0, KT, LANES):
                part = sj[:, c0:c0 + LANES]
                tile_max = part if tile_max is None else jnp.maximum(tile_max, part)
        mrun_scr[...] = jnp.maximum(mrun_scr[...], tile_max)

    def score_full_group(i, carry):
        score_group(i * U, U, False)
        return carry

    lax.fori_loop(0, last_tile // U, score_full_group, 0)
    for r in range(U):
        @pl.when(last_tile % U == r)
        def _(r=r):
            score_group(last_tile - r, r + 1, True)

    mb_scr[...] = jnp.broadcast_to(jnp.max(mrun_scr[...], axis=-1, keepdims=True), mb_scr.shape)
    acc_scr[...] = jnp.zeros(acc_scr.shape, F32)

    def value_group(kt0, count):
        k0 = pl.multiple_of(kt0 * KT, KT)
        mb = mb_scr[...]
        mbw = jnp.concatenate([mb] * (KT // LANES), axis=1)
        p = jnp.concatenate([jnp.exp2(s_scr[kt0 + j] - mbw).astype(BF16) for j in range(count)],
                            axis=1)
        acc_scr[0:HALF, :] += _dot(p[0:HALF], vs0_ref[pl.ds(k0, count * KT), :])
        acc_scr[HALF:2 * HALF, :] += _dot(p[HALF:2 * HALF], vs1_ref[pl.ds(k0, count * KT), :])

    def value_full_group(i, carry):
        value_group(i * U, U)
        return carry

    n_tiles = last_tile + 1
    lax.fori_loop(0, n_tiles // U, value_full_group, 0)
    for r in range(1, U):
        @pl.when(n_tiles % U == r)
        def _(r=r):
            value_group(n_tiles - r, r)

    o_sel = normalised(acc_scr[...], False)

    gate = jax.nn.sigmoid(g_ref[...])
    for pair in range(NSA_HEADS // 2):
        y = None
        for b, o in enumerate((o_cmp, o_sel, o_win)):
            c_e = SUBLANES + 3 * (2 * pair) + b
            c_o = SUBLANES + 3 * (2 * pair + 1) + b
            term = jnp.where(lo_half, gate[:, c_e:c_e + 1], gate[:, c_o:c_o + 1]) * o[pair]
            y = term if y is None else y + term
        y_ref[:, pair * LANES:(pair + 1) * LANES] = y.astype(y_ref.dtype)


def _nsa(pr, kc, vc0, vc1, batch, seq):
    QB = Q_BLOCK
    nq = seq // QB
    nbs = seq // SEL_LEN
    ncmp = kc.shape[1]
    cs = np.arange(ncmp) * CMP_STRIDE
    js = np.arange(nbs) * SEL_LEN
    ov = np.zeros((ncmp, LANES), np.float32)
    ov[:, :nbs] = (cs[:, None] < js[None, :] + SEL_LEN) & (cs[:, None] + CMP_LEN > js[None, :])
    tok = lambda b, i: (b * nq + i, 0)
    per_b = lambda b, i: (b, 0)
    per_b3 = lambda b, i: (b, 0, 0)
    kv_spec = pl.BlockSpec((seq, LANES), per_b)
    cmp_spec = pl.BlockSpec((1, ncmp, LANES), per_b3)
    return pl.pallas_call(
        functools.partial(_nsa_kernel, nbs),
        out_shape=jax.ShapeDtypeStruct((batch * seq, NSA_WIDTH), BF16),
        grid=(batch, nq),
        in_specs=[pl.BlockSpec((QB, NSA_HEADS * LANES), tok),
                  pl.BlockSpec((QB, LANES), tok),
                  cmp_spec, cmp_spec, cmp_spec,
                  kv_spec, kv_spec, kv_spec, kv_spec, kv_spec, kv_spec,
                  pl.BlockSpec((ncmp, LANES), lambda b, i: (0, 0))],
        out_specs=pl.BlockSpec((QB, NSA_WIDTH), tok),
        scratch_shapes=[pltpu.VMEM((NSA_ROWS, 2 * LANES), BF16),
                        pltpu.VMEM((seq, LANES), BF16),
                        pltpu.VMEM((WIN_KEYS, LANES), BF16),
                        pltpu.VMEM((ncmp, LANES), BF16),
                        pltpu.VMEM((seq // SEL_KT, NSA_ROWS, SEL_KT), F32),
                        pltpu.VMEM((NSA_ROWS, LANES), F32),
                        pltpu.VMEM((NSA_ROWS, LANES), F32),
                        pltpu.VMEM((NSA_ROWS, LANES), F32)],
        compiler_params=pltpu.CompilerParams(dimension_semantics=("parallel", "arbitrary"),
                                             vmem_limit_bytes=VMEM_LIMIT),
        name="nsa",
    )(pr["nq"], pr["gcol"], kc, vc0, vc1, pr["ks"], pr["vs0"], pr["vs1"],
      pr["kw"], pr["vw0"], pr["vw1"], jnp.asarray(ov, BF16))


def _memkv_kernel(mem_ref, g_ref, w_ref, kv_ref):
    u = _rms(mem_ref[0], g_ref[...]).astype(BF16)
    kv_ref[0] = _dot(u, w_ref[...]).astype(kv_ref.dtype)


def _memkv(mem, g_mem, w_mem_kv):
    batch, n_mem, d = mem.shape
    return pl.pallas_call(
        _memkv_kernel,
        out_shape=jax.ShapeDtypeStruct((batch, n_mem, 2 * XA_WIDTH), BF16),
        grid=(batch,),
        in_specs=[pl.BlockSpec((1, n_mem, d), lambda b: (b, 0, 0)),
                  _resident((1, d)), _resident((d, 2 * XA_WIDTH))],
        out_specs=pl.BlockSpec((1, n_mem, 2 * XA_WIDTH), lambda b: (b, 0, 0)),
        compiler_params=pltpu.CompilerParams(dimension_semantics=("parallel",),
                                             vmem_limit_bytes=VMEM_LIMIT),
        name="memkv",
    )(mem, g_mem[None, :], w_mem_kv.astype(BF16))


def _tail_kernel(x_ref, yml_ref, ynsa_ref, xq_ref, kv_ref, mg_ref, wb_ref, wo_ref, gf_ref,
                 w1_ref, w2_ref, gl_ref, out_ref):
    d = x_ref.shape[1]
    kv = kv_ref[0]
    heads = [slice(hh * XA_DH, (hh + 1) * XA_DH) for hh in range(XA_HEADS)]
    scores = [_dot_nt(xq_ref[:, sl], kv[:, sl]) * (XA_DH ** -0.5) for sl in heads]
    ups = [_dot(yml_ref[...], wb_ref[0]), _dot(ynsa_ref[...], wb_ref[1])]
    y_xa = []
    for hh, s in enumerate(scores):
        p = jnp.exp(s - jnp.max(s, axis=-1, keepdims=True))
        l = jnp.sum(p, axis=-1, keepdims=True)
        y_xa.append((_dot(p.astype(BF16), kv[:, XA_WIDTH + hh * XA_DH:XA_WIDTH + (hh + 1) * XA_DH])
                     / l).astype(BF16))
    ups.append(_dot(jnp.concatenate(y_xa, axis=-1), wb_ref[2]))
    merged = None
    for j in range(N_BRANCH):
        term = mg_ref[:, j * d:(j + 1) * d] * ups[j]
        merged = term if merged is None else merged + term
    h = x_ref[...] + _dot(merged.astype(BF16), wo_ref[...])
    u = _rms(h, gf_ref[...]).astype(BF16)
    acc = h
    for c0 in range(0, w1_ref.shape[1], FF_SLAB):
        a = jnp.maximum(_dot(u, w1_ref[:, c0:c0 + FF_SLAB]), 0.0)
        acc = acc + _dot((a * a).astype(BF16), w2_ref[c0:c0 + FF_SLAB, :])
    out_ref[...] = _rms(acc, gl_ref[...])


def _tail(x2, y_ml, y_nsa, pr, mem_kv, w_branch, w_out, g_ffn, w_ff1, w_ff2, g_final, seq):
    n_tok, d = x2.shape
    tm = TM_TAIL
    d_ff = w_ff1.shape[1]
    n_mem = mem_kv.shape[1]
    tiles_per_b = seq // tm
    row = lambda i: (i, 0)
    return pl.pallas_call(
        _tail_kernel,
        out_shape=jax.ShapeDtypeStruct((n_tok, d), F32),
        grid=(n_tok // tm,),
        in_specs=[pl.BlockSpec((tm, d), row),
                  pl.BlockSpec((tm, ML_WIDTH), row),
                  pl.BlockSpec((tm, NSA_WIDTH), row),
                  pl.BlockSpec((tm, XA_WIDTH), row),
                  pl.BlockSpec((1, n_mem, 2 * XA_WIDTH), lambda i: (i // tiles_per_b, 0, 0)),
                  pl.BlockSpec((tm, N_BRANCH * d), row),
                  _resident((N_BRANCH, ML_WIDTH, d)), _resident((d, d)), _resident((1, d)),
                  _resident((d, d_ff)), _resident((d_ff, d)), _resident((1, d))],
        out_specs=pl.BlockSpec((tm, d), row),
        compiler_params=pltpu.CompilerParams(dimension_semantics=("parallel",),
                                             vmem_limit_bytes=VMEM_LIMIT),
        name="tail",
    )(x2, y_ml, y_nsa, pr["xq"], mem_kv, pr["mg"], w_branch.astype(BF16), w_out.astype(BF16),
      g_ffn[None, :], w_ff1.astype(BF16), w_ff2.astype(BF16), g_final[None, :])


def _layer(x, mem, g_mix, w_in, b_in, ml_conv, ml_norm_g, cmp_pe, cmp_w1, cmp_w2, g_mem, w_mem_kv,
           w_branch, w_out, g_ffn, w_ff1, w_ff2, g_final):
    batch, seq, d = x.shape
    assert seq % (SEL_LEN * SUBLANES) == 0 and seq % SEL_KT == 0 and seq >= WIN_KEYS
    assert seq // SEL_LEN <= NSA_DH, "one selection-block feature lane per block"
    assert seq // CMP_STRIDE <= 256, "compressed token index must be exact in bf16"
    assert seq % TM_PROJ == 0 and seq % TM_TAIL == 0 and TM_PROJ % ML_CHUNK == 0
    x2 = x.reshape(batch * seq, d)
    pr = _inproj(x2, g_mix, w_in, b_in, ml_conv, ml_norm_g, seq)
    y_ml = pr["y_ml"]
    kc, vc0, vc1 = _compress(pr, cmp_pe, cmp_w1, cmp_w2, batch, seq)
    y_nsa = _nsa(pr, kc, vc0, vc1, batch, seq)
    mem_kv = _memkv(mem, g_mem, w_mem_kv)
    out = _tail(x2, y_ml, y_nsa, pr, mem_kv, w_branch, w_out, g_ffn, w_ff1, w_ff2, g_final, seq)
    return out.reshape(batch, seq, d)


def kernel(x, mem, g_mix, w_in, b_in, ml_conv, ml_norm_g, cmp_pe, cmp_w1, cmp_w2, g_mem, w_mem_kv,
           w_branch, w_out, g_ffn, w_ff1, w_ff2, g_final):
    assert g_mix.shape[0] == 1, "single-layer block"
    return _layer(x, mem, g_mix[0], w_in[0], b_in[0], ml_conv[0], ml_norm_g[0], cmp_pe[0],
                  cmp_w1[0], cmp_w2[0], g_mem[0], w_mem_kv[0], w_branch[0], w_out[0], g_ffn[0],
                  w_ff1[0], w_ff2[0], g_final)
```

```python
import functools

import numpy as np
import jax
import jax.numpy as jnp
from jax import lax
from jax.experimental import pallas as pl
from jax.experimental.pallas import tpu as pltpu

F32 = jnp.float32
BF16 = jnp.bfloat16

EPS = 1e-6
NEG = -1e30
LOG2_E = 1.4426950408889634
ML_HEADS = 4
ML_DH = 128
ML_WIDTH = ML_HEADS * ML_DH
ML_CONV = 4
ML_CHUNK = 128
NSA_HEADS = 8
NSA_KV = 2
NSA_REP = NSA_HEADS // NSA_KV
NSA_DH = 64
NSA_WIDTH = NSA_HEADS * NSA_DH
NSA_KV_WIDTH = NSA_KV * NSA_DH
CMP_LEN = 32
CMP_STRIDE = 16
CMP_HIDDEN = 256
SEL_LEN = 64
SEL_TOP = 16
WINDOW = 512
Q_BLOCK = 128
FORCE_BONUS = 1e3
SEL_KT = 256
SEL_GROUP = 8
XA_HEADS = 4
XA_DH = 128
XA_WIDTH = XA_HEADS * XA_DH
N_BRANCH = 3

LANES = 128
SUBLANES = 8
VMEM_LIMIT = 56 * 1024 * 1024

TM_PROJ = 256
TM_TAIL = 256
FF_SLAB = 1024

_NT = (((1,), (1,)), ((), ()))
_TN = (((0,), (0,)), ((), ()))


def _dot(a, b):
    return jnp.dot(a, b, preferred_element_type=F32)


def _dot_nt(a, b):
    return lax.dot_general(a, b, _NT, preferred_element_type=F32)


def _resident(shape):
    nd = len(shape)
    return pl.BlockSpec(shape, lambda *_: (0,) * nd, pipeline_mode=pl.Buffered(1))


def _rms(x, g):
    return x * lax.rsqrt(jnp.mean(x * x, axis=-1, keepdims=True) + EPS) * g


_PROJ_COLS = (
    ("qk", 2 * ML_WIDTH, BF16),
    ("v", ML_WIDTH, BF16),
    ("o", ML_WIDTH, BF16),
    ("gcol", LANES, F32),
    ("kc", NSA_KV_WIDTH, F32),
    ("vc", NSA_KV_WIDTH, F32),
    ("ks", NSA_KV_WIDTH, BF16),
    ("kw", NSA_KV_WIDTH, BF16),
    ("vs0", LANES, BF16),
    ("vs1", LANES, BF16),
    ("vw0", LANES, BF16),
    ("vw1", LANES, BF16),
    ("nq", NSA_HEADS * LANES, BF16),
    ("xq", XA_WIDTH, BF16),
    ("mg", None, BF16),
)
_PROJ_MLSTM_ONLY = ("qk", "v", "o")
_PROJ_CHUNK = 512


def _proj_layout(d_model):
    cols, off = [], 0
    for name, width, dt in _PROJ_COLS:
        width = N_BRANCH * d_model if width is None else width
        cols.append((name, off, width, dt))
        off += width
    return cols, off


def _inproj_kernel(cols, tiles_per_b, x_ref, g_ref, w_ref, b_ref, wt_ref, bt_ref, conv_ref, ng_ref,
                   *refs):
    out_names = [c[0] for c in cols if c[0] not in _PROJ_MLSTM_ONLY]
    out_refs = dict(zip(out_names, refs[:len(out_names)]))
    yml_ref, xbuf, c_scr, m_scr = refs[len(out_names):]
    tm = x_ref.shape[0]
    halo = SUBLANES

    @pl.when(pl.program_id(0) % tiles_per_b == 0)
    def _():
        xbuf[0:halo, :] = jnp.zeros((halo, xbuf.shape[1]), F32)
        c_scr[...] = jnp.zeros(c_scr.shape, F32)
        m_scr[...] = jnp.zeros(m_scr.shape, F32)

    x = x_ref[...]
    xg = x * g_ref[...]
    xb = xg.astype(BF16)
    r = lax.rsqrt(jnp.mean(x * x, axis=-1, keepdims=True) + EPS)
    grow = _dot_nt(wt_ref[...], (xg * r).astype(BF16)) + bt_ref[...]
    kept = {name: [] for name in _PROJ_MLSTM_ONLY + ("gcol",)}
    mlstm_ready = max(off + width for name, off, width, _ in cols if name in kept)
    ncols = w_ref.shape[1]
    for c0 in range(0, ncols, _PROJ_CHUNK):
        cw = min(_PROJ_CHUNK, ncols - c0)
        chunk = r * _dot(xb, w_ref[:, c0:c0 + cw]) + b_ref[:, c0:c0 + cw]
        for name, off, width, dt in cols:
            lo, hi = max(off, c0), min(off + width, c0 + cw)
            if lo >= hi:
                continue
            acc = chunk[:, lo - c0:hi - c0]
            cs = slice(lo - off, hi - off)
            if name == "qk":
                xbuf[halo:halo + tm, cs] = acc
                y = acc * conv_ref[0:1, cs]
                for j in range(1, ML_CONV):
                    y = y + xbuf[halo - j:halo - j + tm, cs] * conv_ref[j:j + 1, cs]
                xbuf[0:halo, cs] = xbuf[tm:tm + halo, cs]
                acc = y * jax.nn.sigmoid(y)
                if lo - off >= ML_WIDTH:
                    acc = acc * (ML_DH ** -0.5)
            elif name in ("o", "mg"):
                acc = jax.nn.sigmoid(acc)
            acc = acc.astype(dt)
            if name in kept:
                kept[name].append(acc)
            if name in out_refs:
                out_refs[name][:, cs] = acc
        if c0 < mlstm_ready <= c0 + cw:
            qk, v, og = (jnp.concatenate(kept[n], axis=1) for n in _PROJ_MLSTM_ONLY)
            gcol = kept["gcol"][0]
            c_state = [c_scr[hh] for hh in range(ML_HEADS)]
            m_state = [m_scr[hh:hh + 1, 0:1] for hh in range(ML_HEADS)]
            for r0 in range(0, tm, ML_CHUNK):
                rows = slice(r0, r0 + ML_CHUNK)
                y_ml, c_state, m_state = _mlstm_chunk(qk[rows], v[rows], og[rows], gcol[rows],
                                                      grow[:, rows], ng_ref[...], c_state, m_state)
                yml_ref[rows, :] = y_ml
            for hh in range(ML_HEADS):
                c_scr[hh] = c_state[hh]
                m_scr[hh:hh + 1, :] = jnp.broadcast_to(m_state[hh], (1, m_scr.shape[1]))


def _split_w_in(w_in, b_in):
    widths = (ML_WIDTH, ML_WIDTH, ML_WIDTH, ML_WIDTH, ML_HEADS, ML_HEADS,
              NSA_WIDTH, NSA_KV_WIDTH, NSA_KV_WIDTH, NSA_KV_WIDTH, NSA_KV_WIDTH, NSA_KV_WIDTH,
              NSA_KV_WIDTH, 3 * NSA_HEADS, XA_WIDTH, w_in.shape[1])
    names = ("ml_q", "ml_k", "ml_v", "ml_o", "ml_i", "ml_f", "ns_q", "ns_kc", "ns_vc", "ns_ks",
             "ns_vs", "ns_kw", "ns_vw", "ns_g", "xa_q", "mg")
    out, off = {}, 0
    for n, wd in zip(names, widths):
        end = w_in.shape[1] if n == "mg" else off + wd
        out[n] = (w_in[:, off:end], b_in[off:end])
        off = end
    return out


def _inproj(x2, g_mix, w_in, b_in, ml_conv, ml_norm_g, seq):
    n_tok, d = x2.shape
    cols, ncols = _proj_layout(d)
    p = _split_w_in(w_in, b_in)

    def pad_cols(w, b, width):
        return (jnp.pad(w, ((0, 0), (0, width - w.shape[1]))), jnp.pad(b, (0, width - b.shape[0])))

    wq, bq = p["ns_q"]
    wq = wq.reshape(d, NSA_HEADS, 1, NSA_DH)
    bq = bq.reshape(NSA_HEADS, 1, NSA_DH)
    in_group = jnp.asarray(np.arange(NSA_HEADS)[:, None] // NSA_REP == np.arange(NSA_KV)[None, :],
                           w_in.dtype)[:, :, None]
    wq_slots = wq * in_group
    bq_slots = bq * in_group
    gate_w = jnp.concatenate([p["ml_i"][0], p["ml_f"][0]], axis=1)
    gate_b = jnp.concatenate([p["ml_i"][1], p["ml_f"][1]])
    small_w = jnp.concatenate([gate_w, p["ns_g"][0]], axis=1)
    small_b = jnp.concatenate([gate_b, p["ns_g"][1]])

    def with_ones(wb, g):
        w, b = wb
        w = jnp.pad(w[:, g * NSA_DH:(g + 1) * NSA_DH], ((0, 0), (0, LANES - NSA_DH)))
        b = jnp.concatenate([b[g * NSA_DH:(g + 1) * NSA_DH], jnp.ones((LANES - NSA_DH,), b.dtype)])
        return w, b

    pieces = {
        "qk": (jnp.concatenate([p["ml_q"][0], p["ml_k"][0]], axis=1),
               jnp.concatenate([p["ml_q"][1], p["ml_k"][1]])),
        "o": p["ml_o"], "mg": p["mg"],
        "gcol": pad_cols(small_w, small_b, LANES),
        "kc": p["ns_kc"], "vc": p["ns_vc"], "v": p["ml_v"],
        "nq": (wq_slots.reshape(d, NSA_HEADS * LANES), bq_slots.reshape(NSA_HEADS * LANES)),
        "xq": p["xa_q"], "ks": p["ns_ks"], "kw": p["ns_kw"],
        "vs0": with_ones(p["ns_vs"], 0), "vs1": with_ones(p["ns_vs"], 1),
        "vw0": with_ones(p["ns_vw"], 0), "vw1": with_ones(p["ns_vw"], 1),
    }
    w_cols = jnp.concatenate([pieces[name][0] for name, *_ in cols], axis=1).astype(BF16)
    b_cols = jnp.concatenate([pieces[name][1] for name, *_ in cols])[None, :]
    w_rows = gate_w.T.astype(BF16)
    b_rows = gate_b[:, None]

    tm = TM_PROJ
    tiles_per_b = seq // tm
    outs_hbm = [c for c in cols if c[0] not in _PROJ_MLSTM_ONLY] + [("y_ml", None, ML_WIDTH, BF16)]
    outs = pl.pallas_call(
        functools.partial(_inproj_kernel, cols, tiles_per_b),
        out_shape=[jax.ShapeDtypeStruct((n_tok, width), dt) for _, _, width, dt in outs_hbm],
        grid=(n_tok // tm,),
        in_specs=[pl.BlockSpec((tm, d), lambda i: (i, 0)),
                  _resident((1, d)), _resident((d, ncols)), _resident((1, ncols)),
                  _resident((SUBLANES, d)), _resident((SUBLANES, 1)),
                  _resident((ML_CONV, 2 * ML_WIDTH)), _resident((1, ML_WIDTH))],
        out_specs=[pl.BlockSpec((tm, width), lambda i: (i, 0)) for _, _, width, _ in outs_hbm],
        scratch_shapes=[pltpu.VMEM((tm + SUBLANES, 2 * ML_WIDTH), F32),
                        pltpu.VMEM((ML_HEADS, ML_DH, 2 * ML_DH), F32),
                        pltpu.VMEM((SUBLANES, LANES), F32)],
        compiler_params=pltpu.CompilerParams(dimension_semantics=("arbitrary",),
                                             vmem_limit_bytes=VMEM_LIMIT),
        name="inproj",
    )(x2, g_mix[None, :], w_cols, b_cols, w_rows, b_rows, ml_conv, ml_norm_g[None, :])
    return dict(zip([c[0] for c in outs_hbm], outs))


def _mlstm_chunk(qk, v, og, gcol, grow, ng, c_prev, m_prev):
    L = ML_CHUNK
    r_i = lax.broadcasted_iota(jnp.int32, (L, L), 0)
    c_i = lax.broadcasted_iota(jnp.int32, (L, L), 1)
    causal = c_i <= r_i
    tril = causal.astype(F32)
    triu = (r_i <= c_i).astype(F32)
    b_col = jnp.dot(tril, jax.nn.log_sigmoid(gcol), precision=lax.Precision.HIGHEST,
                    preferred_element_type=F32)
    b_row = jnp.dot(jax.nn.log_sigmoid(grow), triu, precision=lax.Precision.HIGHEST,
                    preferred_element_type=F32)
    ones_col = jnp.where(lax.broadcasted_iota(jnp.int32, (L, ML_DH), 1) == 0, 1.0, 0.0).astype(BF16)

    def operands(hh):
        sl = slice(hh * ML_DH, (hh + 1) * ML_DH)
        k = qk[:, ML_WIDTH + hh * ML_DH:ML_WIDTH + (hh + 1) * ML_DH]
        return qk[:, sl], k, jnp.concatenate([v[:, sl], ones_col], axis=1), c_prev[hh]

    ops = [operands(hh) for hh in range(ML_HEADS)]
    qk_t = [_dot_nt(q, k) for q, k, _, _ in ops]
    inter = [_dot(q, c.astype(BF16)) for q, _, _, c in ops]

    stab, ke_all, s_all = [], [], []
    for hh in range(ML_HEADS):
        li_c = gcol[:, hh:hh + 1]
        b_c = b_col[:, ML_HEADS + hh:ML_HEADS + hh + 1]
        li_r = grow[hh:hh + 1, :]
        b_r = b_row[ML_HEADS + hh:ML_HEADS + hh + 1, :]
        g = b_c[L - 1:L, :]
        rel = jnp.where(causal, li_r - b_r, -jnp.inf)
        mm = jnp.maximum(m_prev[hh], jnp.max(rel, axis=-1, keepdims=True))
        m_t = b_c + mm
        s_all.append((qk_t[hh] * jnp.exp(rel - mm)).astype(BF16))
        w_end = g - b_c + li_c
        m_loc = jnp.max(w_end, axis=0, keepdims=True)
        ke_all.append((ops[hh][1] * jnp.exp(w_end - m_loc)).astype(BF16))
        m_new = jnp.maximum(g + m_prev[hh], m_loc)
        stab.append((m_t, jnp.exp(m_prev[hh] - mm), jnp.exp(g + m_prev[hh] - m_new),
                     jnp.exp(m_loc - m_new), m_new))

    intra = [_dot(s, o[2]) for s, o in zip(s_all, ops)]
    a_c = [lax.dot_general(ke, o[2], _TN, preferred_element_type=F32) for ke, o in zip(ke_all, ops)]

    ys, new_c, new_m = [], [], []
    for hh in range(ML_HEADS):
        sl = slice(hh * ML_DH, (hh + 1) * ML_DH)
        m_t, sc, a, bb, m_new = stab[hh]
        nd = intra[hh] + sc * inter[hh]
        den = nd[:, ML_DH:ML_DH + 1]
        h = nd[:, 0:ML_DH] / jnp.maximum(jnp.abs(den), jnp.exp(-m_t))
        hn = h * lax.rsqrt(jnp.mean(h * h, axis=-1, keepdims=True) + EPS)
        ys.append((og[:, sl] * hn * ng[:, sl]).astype(BF16))
        new_c.append(a * ops[hh][3] + bb * a_c[hh])
        new_m.append(m_new)
    return jnp.concatenate(ys, axis=1), new_c, new_m


def _gelu_tanh(x):
    return 0.5 * x * (1.0 + jnp.tanh(np.sqrt(2.0 / np.pi) * (x + 0.044715 * (x * x * x))))


def _compress_kernel(k_ref, v_ref, pe_ref, w1_ref, w2k_ref, w2v_ref, kc_ref, vc0_ref, vc1_ref):
    nrow = kc_ref.shape[1]
    S = CMP_STRIDE

    def hidden(x_ref, which):
        lo = hi = None
        for j in range(0, S, 2):
            x = [x_ref[pl.ds(j + d, nrow, stride=S), :] for d in range(2)]
            for half in range(CMP_LEN // S):
                l0 = half * S + j
                xp = jnp.concatenate([(x[d] + pe_ref[which, l0 + d:l0 + d + 1, :]).astype(BF16)
                                      for d in range(2)], axis=1)
                t = _dot(xp, w1_ref[which, l0 // 2])
                if half == 0:
                    lo = t if lo is None else lo + t
                else:
                    hi = t if hi is None else hi + t
        return _gelu_tanh(lo + pltpu.roll(hi, nrow - 1, 0)).astype(BF16)

    kc_ref[0] = _dot(hidden(k_ref, 0), w2k_ref[...]).astype(kc_ref.dtype)
    hv = hidden(v_ref, 1)
    ones_half = lax.broadcasted_iota(jnp.int32, (nrow, LANES), 1) >= NSA_DH
    for g, o_ref in enumerate((vc0_ref, vc1_ref)):
        o_ref[0] = jnp.where(ones_half, 1.0, _dot(hv, w2v_ref[g])).astype(o_ref.dtype)


def _compress(pr, cmp_pe, cmp_w1, cmp_w2, batch, seq):
    nrow = seq // CMP_STRIDE
    hid = NSA_KV * CMP_HIDDEN
    w1 = cmp_w1.astype(BF16).reshape(2, CMP_LEN, NSA_DH, CMP_HIDDEN)
    z1 = jnp.zeros_like(w1)
    w1e = jnp.stack([jnp.concatenate([w1, z1], axis=-1), jnp.concatenate([z1, w1], axis=-1)], axis=2)
    w1e = w1e.reshape(2, CMP_LEN // 2, 2 * NSA_KV_WIDTH, hid)
    w2 = cmp_w2.astype(BF16)
    z2 = jnp.zeros_like(w2[0])
    w2k = jnp.concatenate([jnp.concatenate([w2[0], z2], axis=1),
                           jnp.concatenate([z2, w2[0]], axis=1)], axis=0)
    w2v = jnp.stack([jnp.pad(w2[1], ((g * CMP_HIDDEN, (NSA_KV - 1 - g) * CMP_HIDDEN),
                                     (0, LANES - NSA_DH))) for g in range(NSA_KV)])
    pe = jnp.concatenate([cmp_pe] * NSA_KV, axis=-1)
    tok_blk = pl.BlockSpec((seq, NSA_KV_WIDTH), lambda b: (b, 0))
    out_blk = pl.BlockSpec((1, nrow, LANES), lambda b: (b, 0, 0))
    return pl.pallas_call(
        _compress_kernel,
        out_shape=(jax.ShapeDtypeStruct((batch, nrow, LANES), BF16),) * 3,
        grid=(batch,),
        in_specs=[tok_blk, tok_blk,
                  _resident((2, CMP_LEN, NSA_KV_WIDTH)),
                  _resident((2, CMP_LEN // 2, 2 * NSA_KV_WIDTH, hid)),
                  _resident((hid, NSA_KV_WIDTH)),
                  _resident((NSA_KV, hid, LANES))],
        out_specs=(out_blk, out_blk, out_blk),
        compiler_params=pltpu.CompilerParams(dimension_semantics=("parallel",),
                                             vmem_limit_bytes=VMEM_LIMIT),
        name="compress",
    )(pr["kc"], pr["vc"], pe, w1e, w2k, w2v)


def _alibi_slope(h):
    return float(2.0 ** (-8.0 * (h + 1) / NSA_HEADS))


NSA_ROWS = NSA_HEADS * Q_BLOCK
WIN_KEYS = WINDOW + Q_BLOCK
F_SEL_BLOCK = 0
F_SEL_OFF = SEL_LEN
F_WIN_OFF = SEL_LEN + 1
F_WIN_BLK = SEL_LEN + 2
F_CMP = SEL_LEN + 3


def _tile_heads(x):
    return jnp.concatenate([x] * NSA_HEADS, axis=0)


def _nsa_kernel(nbs, q_ref, g_ref, kc_ref, vc0_ref, vc1_ref, ks_ref, vs0_ref, vs1_ref,
                kw_ref, vw0_ref, vw1_ref, ov_ref, y_ref,
                qe_scr, posf_scr, wposf_scr, cposf_scr, s_scr, mrun_scr, mb_scr, acc_scr):
    QB = Q_BLOCK
    KT = SEL_KT
    HALF = NSA_REP * QB
    t0 = pl.program_id(1) * QB
    ncmp = kc_ref.shape[1]
    lane = lax.broadcasted_iota(jnp.int32, (1, LANES), 1)
    lo_half = lane < NSA_DH

    @pl.when(pl.program_id(1) == 0)
    def _():
        seq = posf_scr.shape[0]
        k = lax.broadcasted_iota(jnp.int32, (seq, LANES), 0)
        c = lax.broadcasted_iota(jnp.int32, (seq, LANES), 1)
        posf = jnp.where(c < SEL_LEN, jnp.where(k // SEL_LEN == c, 1.0, 0.0),
                         jnp.where(c == F_SEL_OFF, (k % SEL_LEN).astype(F32), 0.0))
        posf_scr[...] = posf.astype(BF16)
        k = lax.broadcasted_iota(jnp.int32, (WIN_KEYS, LANES), 0)
        c = lax.broadcasted_iota(jnp.int32, (WIN_KEYS, LANES), 1)
        wposf = jnp.where(c == F_WIN_OFF, (k % SEL_LEN).astype(F32),
                          jnp.where(c == F_WIN_BLK, (k // SEL_LEN).astype(F32), 0.0))
        wposf_scr[...] = wposf.astype(BF16)
        k = lax.broadcasted_iota(jnp.int32, (ncmp, LANES), 0)
        c = lax.broadcasted_iota(jnp.int32, (ncmp, LANES), 1)
        cposf_scr[...] = jnp.where(c == F_CMP, k.astype(F32), 0.0).astype(BF16)

    def const_feat(h):
        s = _alibi_slope(h)
        return jnp.where((lane == F_SEL_OFF) | (lane == F_WIN_OFF), s,
                         jnp.where(lane == F_WIN_BLK, SEL_LEN * s,
                                   jnp.where(lane == F_CMP, CMP_STRIDE * s, 0.0)))

    for h in range(NSA_HEADS):
        rows = slice(h * QB, (h + 1) * QB)
        qe_scr[rows, 0:LANES] = q_ref[:, h * LANES:(h + 1) * LANES] * 0.125
        qe_scr[rows, LANES:2 * LANES] = jnp.broadcast_to(const_feat(h), (QB, LANES)).astype(BF16)

    def normalised(o, guard):
        outs = []
        for pair in range(NSA_HEADS // 2):
            e = o[(2 * pair) * QB:(2 * pair + 1) * QB]
            d = o[(2 * pair + 1) * QB:(2 * pair + 2) * QB]
            num = jnp.where(lo_half, e, pltpu.roll(d, NSA_DH, 1))
            den = jnp.where(lo_half, pltpu.roll(e, NSA_DH, 1), d)
            if guard:
                den = jnp.where(den > 0.0, den, 1.0)
            outs.append(num / den)
        return outs

    ws = pl.multiple_of(jnp.maximum(t0 - WINDOW, 0), QB)
    qs = lax.broadcasted_iota(jnp.int32, (QB, WIN_KEYS), 0)
    kl = lax.broadcasted_iota(jnp.int32, (QB, WIN_KEYS), 1)
    dist_w = (t0 - ws) + qs - kl
    ok_w = (dist_w >= 0) & (dist_w < WINDOW)
    s_w = _dot_nt(qe_scr[...], jnp.concatenate([kw_ref[pl.ds(ws, WIN_KEYS), :], wposf_scr[...]], axis=1))

    q_sub = lax.broadcasted_iota(jnp.int32, (QB, ncmp), 0)
    n_lane = lax.broadcasted_iota(jnp.int32, (QB, ncmp), 1)
    ok_c = n_lane * CMP_STRIDE + (CMP_LEN - 1) <= t0 + q_sub
    s = _dot_nt(qe_scr[...], jnp.concatenate([kc_ref[0], cposf_scr[...]], axis=1))
    s = s + _tile_heads(jnp.where(ok_c, 0.0, NEG))
    e = jnp.exp(s - jnp.max(s, axis=-1, keepdims=True)) * _tile_heads(jnp.where(ok_c, 1.0, 0.0))
    e_bf = e.astype(BF16)
    oc = jnp.concatenate([_dot(e_bf[0:HALF], vc0_ref[0]), _dot(e_bf[HALF:2 * HALF], vc1_ref[0])],
                         axis=0)
    o_cmp = normalised(oc, True)
    l_c = jnp.where(lo_half, pltpu.roll(oc, NSA_DH, 1), oc)
    inv_l = 1.0 / jnp.where(l_c > 0.0, l_c, 1.0)
    p_c = e * jnp.concatenate([inv_l] * (ncmp // LANES), axis=1)

    imp_ts = []
    for g in range(NSA_KV):
        psum = p_c[(g * NSA_REP) * QB:(g * NSA_REP + 1) * QB]
        for r in range(1, NSA_REP):
            psum = psum + p_c[(g * NSA_REP + r) * QB:(g * NSA_REP + r + 1) * QB]
        p_hi = psum.astype(BF16)
        p_lo = (psum - p_hi.astype(F32)).astype(BF16)
        imp = _dot(p_hi, ov_ref[...]) + _dot(p_lo, ov_ref[...])
        imp_ts.append(imp.T[0:nbs, :])

    n_grp = nbs // SUBLANES
    sub8 = lax.broadcasted_iota(jnp.int32, (SUBLANES, QB), 0)
    tpos = t0 + lax.broadcasted_iota(jnp.int32, (1, QB), 1)
    cur = tpos // SEL_LEN
    j_lane = lane.astype(F32)
    for g in range(NSA_KV):
        imp_t = imp_ts[g]
        j_sub = lax.broadcasted_iota(jnp.int32, (nbs, QB), 0)
        forced = (j_sub == 0) | (j_sub == cur) | (j_sub == cur - 1)
        score = jnp.where(j_sub <= cur, imp_t + jnp.where(forced, FORCE_BONUS, 0.0), NEG)
        parts = [score[a * SUBLANES:(a + 1) * SUBLANES, :] for a in range(n_grp)]
        ranks = [jnp.zeros((SUBLANES, QB), F32) for _ in range(n_grp)]
        for i in range(nbs):
            row = jnp.broadcast_to(score[i:i + 1, :], (SUBLANES, QB))
            for a in range(n_grp):
                if a * SUBLANES > i:
                    beats = jnp.where(row >= parts[a], 1.0, 0.0)
                elif (a + 1) * SUBLANES - 1 < i:
                    beats = jnp.where(row > parts[a], 1.0, 0.0)
                else:
                    beats = jnp.where(sub8 + a * SUBLANES > i,
                                      jnp.where(row >= parts[a], 1.0, 0.0),
                                      jnp.where(row > parts[a], 1.0, 0.0))
                ranks[a] = ranks[a] + beats
        sel_t = [jnp.where(rk < float(min(SEL_TOP, nbs)), 0.0, NEG) for rk in ranks]
        if nbs < LANES:
            sel_t.append(jnp.zeros((LANES - nbs, QB), F32))
        sel_bias = jnp.concatenate(sel_t, axis=0).T
        for r in range(NSA_REP):
            h = g * NSA_REP + r
            feat = jnp.where(lo_half, sel_bias + (_alibi_slope(h) * SEL_LEN) * j_lane, const_feat(h))
            qe_scr[h * QB:(h + 1) * QB, LANES:2 * LANES] = feat.astype(BF16)

    s_w = s_w + _tile_heads(jnp.where(ok_w, 0.0, NEG))
    p_w = jnp.exp(s_w - jnp.max(s_w, axis=-1, keepdims=True)).astype(BF16)
    o_win = normalised(jnp.concatenate([_dot(p_w[0:HALF], vw0_ref[pl.ds(ws, WIN_KEYS), :]),
                                        _dot(p_w[HALF:2 * HALF], vw1_ref[pl.ds(ws, WIN_KEYS), :])],
                                       axis=0), False)

    last_tile = t0 // KT
    mrun_scr[...] = jnp.full(mrun_scr.shape, NEG, F32)

    U = SEL_GROUP

    def score_group(kt0, count, causal):
        k0 = pl.multiple_of(kt0 * KT, KT)
        kx = jnp.concatenate([ks_ref[pl.ds(k0, count * KT), :],
                              posf_scr[pl.ds(k0, count * KT), :]], axis=1)
        s = _dot_nt(qe_scr[...], kx)
        tile_max = None
        for j in range(count):
            sj = s[:, j * KT:(j + 1) * KT] * LOG2_E
            if causal and j == count - 1:
                qs = lax.broadcasted_iota(jnp.int32, (QB, KT), 0)
                kl = lax.broadcasted_iota(jnp.int32, (QB, KT), 1)
                sj = sj + _tile_heads(jnp.where(k0 + j * KT + kl <= t0 + qs, 0.0, NEG))
            s_scr[kt0 + j] = sj
            for c0 in range(0, KT, LANES):
                part = sj[:, c0:c0 + LANES]
                tile_max = part if tile_max is None else jnp.maximum(tile_max, part)
        mrun_scr[...] = jnp.maximum(mrun_scr[...], tile_max)

    def score_full_group(i, carry):
        score_group(i * U, U, False)
        return carry

    lax.fori_loop(0, last_tile // U, score_full_group, 0)
    for r in range(U):
        @pl.when(last_tile % U == r)
        def _(r=r):
            score_group(last_tile - r, r + 1, True)

    mb_scr[...] = jnp.broadcast_to(jnp.max(mrun_scr[...], axis=-1, keepdims=True), mb_scr.shape)
    acc_scr[...] = jnp.zeros(acc_scr.shape, F32)

    def value_group(kt0, count):
        k0 = pl.multiple_of(kt0 * KT, KT)
        mb = mb_scr[...]
        mbw = jnp.concatenate([mb] * (KT // LANES), axis=1)
        p = jnp.concatenate([jnp.exp2(s_scr[kt0 + j] - mbw).astype(BF16) for j in range(count)],
                            axis=1)
        acc_scr[0:HALF, :] += _dot(p[0:HALF], vs0_ref[pl.ds(k0, count * KT), :])
        acc_scr[HALF:2 * HALF, :] += _dot(p[HALF:2 * HALF], vs1_ref[pl.ds(k0, count * KT), :])

    def value_full_group(i, carry):
        value_group(i * U, U)
        return carry

    n_tiles = last_tile + 1
    lax.fori_loop(0, n_tiles // U, value_full_group, 0)
    for r in range(1, U):
        @pl.when(n_tiles % U == r)
        def _(r=r):
            value_group(n_tiles - r, r)

    o_sel = normalised(acc_scr[...], False)

    gate = jax.nn.sigmoid(g_ref[...])
    for pair in range(NSA_HEADS // 2):
        y = None
        for b, o in enumerate((o_cmp, o_sel, o_win)):
            c_e = SUBLANES + 3 * (2 * pair) + b
            c_o = SUBLANES + 3 * (2 * pair + 1) + b
            term = jnp.where(lo_half, gate[:, c_e:c_e + 1], gate[:, c_o:c_o + 1]) * o[pair]
            y = term if y is None else y + term
        y_ref[:, pair * LANES:(pair + 1) * LANES] = y.astype(y_ref.dtype)


def _nsa(pr, kc, vc0, vc1, batch, seq):
    QB = Q_BLOCK
    nq = seq // QB
    nbs = seq // SEL_LEN
    ncmp = kc.shape[1]
    cs = np.arange(ncmp) * CMP_STRIDE
    js = np.arange(nbs) * SEL_LEN
    ov = np.zeros((ncmp, LANES), np.float32)
    ov[:, :nbs] = (cs[:, None] < js[None, :] + SEL_LEN) & (cs[:, None] + CMP_LEN > js[None, :])
    tok = lambda b, i: (b * nq + i, 0)
    per_b = lambda b, i: (b, 0)
    per_b3 = lambda b, i: (b, 0, 0)
    kv_spec = pl.BlockSpec((seq, LANES), per_b)
    cmp_spec = pl.BlockSpec((1, ncmp, LANES), per_b3)
    return pl.pallas_call(
        functools.partial(_nsa_kernel, nbs),
        out_shape=jax.ShapeDtypeStruct((batch * seq, NSA_WIDTH), BF16),
        grid=(batch, nq),
        in_specs=[pl.BlockSpec((QB, NSA_HEADS * LANES), tok),
                  pl.BlockSpec((QB, LANES), tok),
                  cmp_spec, cmp_spec, cmp_spec,
                  kv_spec, kv_spec, kv_spec, kv_spec, kv_spec, kv_spec,
                  pl.BlockSpec((ncmp, LANES), lambda b, i: (0, 0))],
        out_specs=pl.BlockSpec((QB, NSA_WIDTH), tok),
        scratch_shapes=[pltpu.VMEM((NSA_ROWS, 2 * LANES), BF16),
                        pltpu.VMEM((seq, LANES), BF16),
                        pltpu.VMEM((WIN_KEYS, LANES), BF16),
                        pltpu.VMEM((ncmp, LANES), BF16),
                        pltpu.VMEM((seq // SEL_KT, NSA_ROWS, SEL_KT), F32),
                        pltpu.VMEM((NSA_ROWS, LANES), F32),
                        pltpu.VMEM((NSA_ROWS, LANES), F32),
                        pltpu.VMEM((NSA_ROWS, LANES), F32)],
        compiler_params=pltpu.CompilerParams(dimension_semantics=("parallel", "arbitrary"),
                                             vmem_limit_bytes=VMEM_LIMIT),
        name="nsa",
    )(pr["nq"], pr["gcol"], kc, vc0, vc1, pr["ks"], pr["vs0"], pr["vs1"],
      pr["kw"], pr["vw0"], pr["vw1"], jnp.asarray(ov, BF16))


def _memkv_kernel(mem_ref, g_ref, w_ref, kv_ref):
    u = _rms(mem_ref[0], g_ref[...]).astype(BF16)
    kv_ref[0] = _dot(u, w_ref[...]).astype(kv_ref.dtype)


def _memkv(mem, g_mem, w_mem_kv):
    batch, n_mem, d = mem.shape
    return pl.pallas_call(
        _memkv_kernel,
        out_shape=jax.ShapeDtypeStruct((batch, n_mem, 2 * XA_WIDTH), BF16),
        grid=(batch,),
        in_specs=[pl.BlockSpec((1, n_mem, d), lambda b: (b, 0, 0)),
                  _resident((1, d)), _resident((d, 2 * XA_WIDTH))],
        out_specs=pl.BlockSpec((1, n_mem, 2 * XA_WIDTH), lambda b: (b, 0, 0)),
        compiler_params=pltpu.CompilerParams(dimension_semantics=("parallel",),
                                             vmem_limit_bytes=VMEM_LIMIT),
        name="memkv",
    )(mem, g_mem[None, :], w_mem_kv.astype(BF16))


def _tail_kernel(x_ref, yml_ref, ynsa_ref, xq_ref, kv_ref, mg_ref, wb_ref, wo_ref, gf_ref,
                 w1_ref, w2_ref, gl_ref, out_ref):
    d = x_ref.shape[1]
    kv = kv_ref[0]
    heads = [slice(hh * XA_DH, (hh + 1) * XA_DH) for hh in range(XA_HEADS)]
    scores = [_dot_nt(xq_ref[:, sl], kv[:, sl]) * (XA_DH ** -0.5) for sl in heads]
    ups = [_dot(yml_ref[...], wb_ref[0]), _dot(ynsa_ref[...], wb_ref[1])]
    y_xa = []
    for hh, s in enumerate(scores):
        p = jnp.exp(s - jnp.max(s, axis=-1, keepdims=True))
        l = jnp.sum(p, axis=-1, keepdims=True)
        y_xa.append((_dot(p.astype(BF16), kv[:, XA_WIDTH + hh * XA_DH:XA_WIDTH + (hh + 1) * XA_DH])
                     / l).astype(BF16))
    ups.append(_dot(jnp.concatenate(y_xa, axis=-1), wb_ref[2]))
    merged = None
    for j in range(N_BRANCH):
        term = mg_ref[:, j * d:(j + 1) * d] * ups[j]
        merged = term if merged is None else merged + term
    h = x_ref[...] + _dot(merged.astype(BF16), wo_ref[...])
    hb = (h * gf_ref[...]).astype(BF16)
    r = lax.rsqrt(jnp.mean(h * h, axis=-1, keepdims=True) + EPS)
    acc = h
    for c0 in range(0, w1_ref.shape[1], FF_SLAB):
        a = jnp.maximum(r * _dot(hb, w1_ref[:, c0:c0 + FF_SLAB]), 0.0)
        acc = acc + _dot((a * a).astype(BF16), w2_ref[c0:c0 + FF_SLAB, :])
    out_ref[...] = _rms(acc, gl_ref[...])


def _tail(x2, y_ml, y_nsa, pr, mem_kv, w_branch, w_out, g_ffn, w_ff1, w_ff2, g_final, seq):
    n_tok, d = x2.shape
    tm = TM_TAIL
    d_ff = w_ff1.shape[1]
    n_mem = mem_kv.shape[1]
    tiles_per_b = seq // tm
    row = lambda i: (i, 0)
    return pl.pallas_call(
        _tail_kernel,
        out_shape=jax.ShapeDtypeStruct((n_tok, d), F32),
        grid=(n_tok // tm,),
        in_specs=[pl.BlockSpec((tm, d), row),
                  pl.BlockSpec((tm, ML_WIDTH), row),
                  pl.BlockSpec((tm, NSA_WIDTH), row),
                  pl.BlockSpec((tm, XA_WIDTH), row),
                  pl.BlockSpec((1, n_mem, 2 * XA_WIDTH), lambda i: (i // tiles_per_b, 0, 0)),
                  pl.BlockSpec((tm, N_BRANCH * d), row),
                  _resident((N_BRANCH, ML_WIDTH, d)), _resident((d, d)), _resident((1, d)),
                  _resident((d, d_ff)), _resident((d_ff, d)), _resident((1, d))],
        out_specs=pl.BlockSpec((tm, d), row),
        compiler_params=pltpu.CompilerParams(dimension_semantics=("parallel",),
                                             vmem_limit_bytes=VMEM_LIMIT),
        name="tail",
    )(x2, y_ml, y_nsa, pr["xq"], mem_kv, pr["mg"], w_branch.astype(BF16), w_out.astype(BF16),
      g_ffn[None, :], w_ff1.astype(BF16), w_ff2.astype(BF16), g_final[None, :])


def _layer(x, mem, g_mix, w_in, b_in, ml_conv, ml_norm_g, cmp_pe, cmp_w1, cmp_w2, g_mem, w_mem_kv,
           w_branch, w_out, g_ffn, w_ff1, w_ff2, g_final):
    batch, seq, d = x.shape
    assert seq % (SEL_LEN * SUBLANES) == 0 and seq % SEL_KT == 0 and seq >= WIN_KEYS
    assert seq // SEL_LEN <= NSA_DH, "one selection-block feature lane per block"
    assert seq // CMP_STRIDE <= 256, "compressed token index must be exact in bf16"
    assert seq % TM_PROJ == 0 and seq % TM_TAIL == 0 and TM_PROJ % ML_CHUNK == 0
    x2 = x.reshape(batch * seq, d)
    pr = _inproj(x2, g_mix, w_in, b_in, ml_conv, ml_norm_g, seq)
    y_ml = pr["y_ml"]
    kc, vc0, vc1 = _compress(pr, cmp_pe, cmp_w1, cmp_w2, batch, seq)
    y_nsa = _nsa(pr, kc, vc0, vc1, batch, seq)
    mem_kv = _memkv(mem, g_mem, w_mem_kv)
    out = _tail(x2, y_ml, y_nsa, pr, mem_kv, w_branch, w_out, g_ffn, w_ff1, w_ff2, g_final, seq)
    return out.reshape(batch, seq, d)


def kernel(x, mem, g_mix, w_in, b_in, ml_conv, ml_norm_g, cmp_pe, cmp_w1, cmp_w2, g_mem, w_mem_kv,
           w_branch, w_out, g_ffn, w_ff1, w_ff2, g_final):
    assert g_mix.shape[0] == 1, "single-layer block"
    return _layer(x, mem, g_mix[0], w_in[0], b_in[0], ml_conv[0], ml_norm_g[0], cmp_pe[0],
                  cmp_w1[0], cmp_w2[0], g_mem[0], w_mem_kv[0], w_branch[0], w_out[0], g_ffn[0],
                  w_ff1[0], w_ff2[0], g_final)
```

```python
import functools

import numpy as np
import jax
import jax.numpy as jnp
from jax import lax
from jax.experimental import pallas as pl
from jax.experimental.pallas import tpu as pltpu

F32 = jnp.float32
BF16 = jnp.bfloat16

EPS = 1e-6
NEG = -1e30
LOG2_E = 1.4426950408889634
ML_HEADS = 4
ML_DH = 128
ML_WIDTH = ML_HEADS * ML_DH
ML_CONV = 4
ML_CHUNK = 128
NSA_HEADS = 8
NSA_KV = 2
NSA_REP = NSA_HEADS // NSA_KV
NSA_DH = 64
NSA_WIDTH = NSA_HEADS * NSA_DH
NSA_KV_WIDTH = NSA_KV * NSA_DH
CMP_LEN = 32
CMP_STRIDE = 16
CMP_HIDDEN = 256
SEL_LEN = 64
SEL_TOP = 16
WINDOW = 512
Q_BLOCK = 128
FORCE_BONUS = 1e3
SEL_KT = 256
SEL_GROUP = 8
XA_HEADS = 4
XA_DH = 128
XA_WIDTH = XA_HEADS * XA_DH
N_BRANCH = 3

LANES = 128
SUBLANES = 8
VMEM_LIMIT = 56 * 1024 * 1024

TM_PROJ = 256
TM_TAIL = 256
FF_SLAB = 1024

_NT = (((1,), (1,)), ((), ()))
_TN = (((0,), (0,)), ((), ()))


def _dot(a, b):
    return jnp.dot(a, b, preferred_element_type=F32)


def _dot_nt(a, b):
    return lax.dot_general(a, b, _NT, preferred_element_type=F32)


def _resident(shape):
    nd = len(shape)
    return pl.BlockSpec(shape, lambda *_: (0,) * nd, pipeline_mode=pl.Buffered(1))


def _rms(x, g):
    return x * lax.rsqrt(jnp.mean(x * x, axis=-1, keepdims=True) + EPS) * g


_PROJ_COLS = (
    ("qk", 2 * ML_WIDTH, BF16),
    ("v", ML_WIDTH, BF16),
    ("o", ML_WIDTH, BF16),
    ("gcol", LANES, F32),
    ("kc", NSA_KV_WIDTH, F32),
    ("vc", NSA_KV_WIDTH, F32),
    ("ks", NSA_KV_WIDTH, BF16),
    ("kw", NSA_KV_WIDTH, BF16),
    ("vs0", LANES, BF16),
    ("vs1", LANES, BF16),
    ("vw0", LANES, BF16),
    ("vw1", LANES, BF16),
    ("nq", NSA_HEADS * LANES, BF16),
    ("xq", XA_WIDTH, BF16),
    ("mg", None, BF16),
)
_PROJ_MLSTM_ONLY = ("qk", "v", "o")
_PROJ_CHUNK = 512


def _proj_layout(d_model):
    cols, off = [], 0
    for name, width, dt in _PROJ_COLS:
        width = N_BRANCH * d_model if width is None else width
        cols.append((name, off, width, dt))
        off += width
    return cols, off


def _inproj_kernel(cols, tiles_per_b, x_ref, g_ref, w_ref, b_ref, wt_ref, bt_ref, conv_ref, ng_ref,
                   *refs):
    out_names = [c[0] for c in cols if c[0] not in _PROJ_MLSTM_ONLY]
    out_refs = dict(zip(out_names, refs[:len(out_names)]))
    yml_ref, xbuf, c_scr, m_scr = refs[len(out_names):]
    tm = x_ref.shape[0]
    halo = SUBLANES

    @pl.when(pl.program_id(0) % tiles_per_b == 0)
    def _():
        xbuf[0:halo, :] = jnp.zeros((halo, xbuf.shape[1]), F32)
        c_scr[...] = jnp.zeros(c_scr.shape, F32)
        m_scr[...] = jnp.zeros(m_scr.shape, F32)

    x = x_ref[...]
    xg = x * g_ref[...]
    xb = xg.astype(BF16)
    r = lax.rsqrt(jnp.mean(x * x, axis=-1, keepdims=True) + EPS)
    grow = _dot_nt(wt_ref[...], (xg * r).astype(BF16)) + bt_ref[...]
    kept = {name: [] for name in _PROJ_MLSTM_ONLY + ("gcol",)}
    mlstm_ready = max(off + width for name, off, width, _ in cols if name in kept)
    ncols = w_ref.shape[1]
    for c0 in range(0, ncols, _PROJ_CHUNK):
        cw = min(_PROJ_CHUNK, ncols - c0)
        chunk = r * _dot(xb, w_ref[:, c0:c0 + cw]) + b_ref[:, c0:c0 + cw]
        for name, off, width, dt in cols:
            lo, hi = max(off, c0), min(off + width, c0 + cw)
            if lo >= hi:
                continue
            acc = chunk[:, lo - c0:hi - c0]
            cs = slice(lo - off, hi - off)
            if name == "qk":
                xbuf[halo:halo + tm, cs] = acc
                y = acc * conv_ref[0:1, cs]
                for j in range(1, ML_CONV):
                    y = y + xbuf[halo - j:halo - j + tm, cs] * conv_ref[j:j + 1, cs]
                xbuf[0:halo, cs] = xbuf[tm:tm + halo, cs]
                acc = y * jax.nn.sigmoid(y)
                if lo - off >= ML_WIDTH:
                    acc = acc * (ML_DH ** -0.5)
            elif name in ("o", "mg"):
                acc = jax.nn.sigmoid(acc)
            acc = acc.astype(dt)
            if name in kept:
                kept[name].append(acc)
            if name in out_refs:
                out_refs[name][:, cs] = acc
        if c0 < mlstm_ready <= c0 + cw:
            qk, v, og = (jnp.concatenate(kept[n], axis=1) for n in _PROJ_MLSTM_ONLY)
            gcol = kept["gcol"][0]
            c_state = [c_scr[hh] for hh in range(ML_HEADS)]
            m_state = [m_scr[hh:hh + 1, 0:1] for hh in range(ML_HEADS)]
            for r0 in range(0, tm, ML_CHUNK):
                rows = slice(r0, r0 + ML_CHUNK)
                y_ml, c_state, m_state = _mlstm_chunk(qk[rows], v[rows], og[rows], gcol[rows],
                                                      grow[:, rows], ng_ref[...], c_state, m_state)
                yml_ref[rows, :] = y_ml
            for hh in range(ML_HEADS):
                c_scr[hh] = c_state[hh]
                m_scr[hh:hh + 1, :] = jnp.broadcast_to(m_state[hh], (1, m_scr.shape[1]))


def _split_w_in(w_in, b_in):
    widths = (ML_WIDTH, ML_WIDTH, ML_WIDTH, ML_WIDTH, ML_HEADS, ML_HEADS,
              NSA_WIDTH, NSA_KV_WIDTH, NSA_KV_WIDTH, NSA_KV_WIDTH, NSA_KV_WIDTH, NSA_KV_WIDTH,
              NSA_KV_WIDTH, 3 * NSA_HEADS, XA_WIDTH, w_in.shape[1])
    names = ("ml_q", "ml_k", "ml_v", "ml_o", "ml_i", "ml_f", "ns_q", "ns_kc", "ns_vc", "ns_ks",
             "ns_vs", "ns_kw", "ns_vw", "ns_g", "xa_q", "mg")
    out, off = {}, 0
    for n, wd in zip(names, widths):
        end = w_in.shape[1] if n == "mg" else off + wd
        out[n] = (w_in[:, off:end], b_in[off:end])
        off = end
    return out


def _inproj(x2, g_mix, w_in, b_in, ml_conv, ml_norm_g, seq):
    n_tok, d = x2.shape
    cols, ncols = _proj_layout(d)
    p = _split_w_in(w_in, b_in)

    def pad_cols(w, b, width):
        return (jnp.pad(w, ((0, 0), (0, width - w.shape[1]))), jnp.pad(b, (0, width - b.shape[0])))

    wq, bq = p["ns_q"]
    wq = wq.reshape(d, NSA_HEADS, 1, NSA_DH)
    bq = bq.reshape(NSA_HEADS, 1, NSA_DH)
    in_group = jnp.asarray(np.arange(NSA_HEADS)[:, None] // NSA_REP == np.arange(NSA_KV)[None, :],
                           w_in.dtype)[:, :, None]
    wq_slots = wq * in_group
    bq_slots = bq * in_group
    gate_w = jnp.concatenate([p["ml_i"][0], p["ml_f"][0]], axis=1)
    gate_b = jnp.concatenate([p["ml_i"][1], p["ml_f"][1]])
    small_w = jnp.concatenate([gate_w, p["ns_g"][0]], axis=1)
    small_b = jnp.concatenate([gate_b, p["ns_g"][1]])

    def with_ones(wb, g):
        w, b = wb
        w = jnp.pad(w[:, g * NSA_DH:(g + 1) * NSA_DH], ((0, 0), (0, LANES - NSA_DH)))
        b = jnp.concatenate([b[g * NSA_DH:(g + 1) * NSA_DH], jnp.ones((LANES - NSA_DH,), b.dtype)])
        return w, b

    pieces = {
        "qk": (jnp.concatenate([p["ml_q"][0], p["ml_k"][0]], axis=1),
               jnp.concatenate([p["ml_q"][1], p["ml_k"][1]])),
        "o": p["ml_o"], "mg": p["mg"],
        "gcol": pad_cols(small_w, small_b, LANES),
        "kc": p["ns_kc"], "vc": p["ns_vc"], "v": p["ml_v"],
        "nq": (wq_slots.reshape(d, NSA_HEADS * LANES), bq_slots.reshape(NSA_HEADS * LANES)),
        "xq": p["xa_q"], "ks": p["ns_ks"], "kw": p["ns_kw"],
        "vs0": with_ones(p["ns_vs"], 0), "vs1": with_ones(p["ns_vs"], 1),
        "vw0": with_ones(p["ns_vw"], 0), "vw1": with_ones(p["ns_vw"], 1),
    }
    w_cols = jnp.concatenate([pieces[name][0] for name, *_ in cols], axis=1).astype(BF16)
    b_cols = jnp.concatenate([pieces[name][1] for name, *_ in cols])[None, :]
    w_rows = gate_w.T.astype(BF16)
    b_rows = gate_b[:, None]

    tm = TM_PROJ
    tiles_per_b = seq // tm
    outs_hbm = [c for c in cols if c[0] not in _PROJ_MLSTM_ONLY] + [("y_ml", None, ML_WIDTH, BF16)]
    outs = pl.pallas_call(
        functools.partial(_inproj_kernel, cols, tiles_per_b),
        out_shape=[jax.ShapeDtypeStruct((n_tok, width), dt) for _, _, width, dt in outs_hbm],
        grid=(n_tok // tm,),
        in_specs=[pl.BlockSpec((tm, d), lambda i: (i, 0)),
                  _resident((1, d)), _resident((d, ncols)), _resident((1, ncols)),
                  _resident((SUBLANES, d)), _resident((SUBLANES, 1)),
                  _resident((ML_CONV, 2 * ML_WIDTH)), _resident((1, ML_WIDTH))],
        out_specs=[pl.BlockSpec((tm, width), lambda i: (i, 0)) for _, _, width, _ in outs_hbm],
        scratch_shapes=[pltpu.VMEM((tm + SUBLANES, 2 * ML_WIDTH), F32),
                        pltpu.VMEM((ML_HEADS, ML_DH, 2 * ML_DH), F32),
                        pltpu.VMEM((SUBLANES, LANES), F32)],
        compiler_params=pltpu.CompilerParams(dimension_semantics=("arbitrary",),
                                             vmem_limit_bytes=VMEM_LIMIT),
        name="inproj",
    )(x2, g_mix[None, :], w_cols, b_cols, w_rows, b_rows, ml_conv, ml_norm_g[None, :])
    return dict(zip([c[0] for c in outs_hbm], outs))


def _mlstm_chunk(qk, v, og, gcol, grow, ng, c_prev, m_prev):
    L = ML_CHUNK
    r_i = lax.broadcasted_iota(jnp.int32, (L, L), 0)
    c_i = lax.broadcasted_iota(jnp.int32, (L, L), 1)
    causal = c_i <= r_i
    tril = causal.astype(F32)
    triu = (r_i <= c_i).astype(F32)
    b_col = jnp.dot(tril, jax.nn.log_sigmoid(gcol), precision=lax.Precision.HIGHEST,
                    preferred_element_type=F32)
    b_row = jnp.dot(jax.nn.log_sigmoid(grow), triu, precision=lax.Precision.HIGHEST,
                    preferred_element_type=F32)
    ones_col = jnp.where(lax.broadcasted_iota(jnp.int32, (L, ML_DH), 1) == 0, 1.0, 0.0).astype(BF16)

    def operands(hh):
        sl = slice(hh * ML_DH, (hh + 1) * ML_DH)
        k = qk[:, ML_WIDTH + hh * ML_DH:ML_WIDTH + (hh + 1) * ML_DH]
        return qk[:, sl], k, jnp.concatenate([v[:, sl], ones_col], axis=1), c_prev[hh]

    ops = [operands(hh) for hh in range(ML_HEADS)]
    qk_t = [_dot_nt(q, k) for q, k, _, _ in ops]
    inter = [_dot(q, c.astype(BF16)) for q, _, _, c in ops]

    stab, ke_all, s_all = [], [], []
    for hh in range(ML_HEADS):
        li_c = gcol[:, hh:hh + 1]
        b_c = b_col[:, ML_HEADS + hh:ML_HEADS + hh + 1]
        li_r = grow[hh:hh + 1, :]
        b_r = b_row[ML_HEADS + hh:ML_HEADS + hh + 1, :]
        g = b_c[L - 1:L, :]
        rel = jnp.where(causal, li_r - b_r, -jnp.inf)
        mm = jnp.maximum(m_prev[hh], jnp.max(rel, axis=-1, keepdims=True))
        m_t = b_c + mm
        s_all.append((qk_t[hh] * jnp.exp(rel - mm)).astype(BF16))
        w_end = g - b_c + li_c
        m_loc = jnp.max(w_end, axis=0, keepdims=True)
        ke_all.append((ops[hh][1] * jnp.exp(w_end - m_loc)).astype(BF16))
        m_new = jnp.maximum(g + m_prev[hh], m_loc)
        stab.append((m_t, jnp.exp(m_prev[hh] - mm), jnp.exp(g + m_prev[hh] - m_new),
                     jnp.exp(m_loc - m_new), m_new))

    intra = [_dot(s, o[2]) for s, o in zip(s_all, ops)]
    a_c = [lax.dot_general(ke, o[2], _TN, preferred_element_type=F32) for ke, o in zip(ke_all, ops)]

    ys, new_c, new_m = [], [], []
    for hh in range(ML_HEADS):
        sl = slice(hh * ML_DH, (hh + 1) * ML_DH)
        m_t, sc, a, bb, m_new = stab[hh]
        nd = intra[hh] + sc * inter[hh]
        den = nd[:, ML_DH:ML_DH + 1]
        h = nd[:, 0:ML_DH] / jnp.maximum(jnp.abs(den), jnp.exp(-m_t))
        hn = h * lax.rsqrt(jnp.mean(h * h, axis=-1, keepdims=True) + EPS)
        ys.append((og[:, sl] * hn * ng[:, sl]).astype(BF16))
        new_c.append(a * ops[hh][3] + bb * a_c[hh])
        new_m.append(m_new)
    return jnp.concatenate(ys, axis=1), new_c, new_m


def _gelu_tanh(x):
    return 0.5 * x * (1.0 + jnp.tanh(np.sqrt(2.0 / np.pi) * (x + 0.044715 * (x * x * x))))


def _compress_kernel(k_ref, v_ref, pe_ref, w1_ref, w2k_ref, w2v_ref, kc_ref, vc0_ref, vc1_ref):
    nrow = kc_ref.shape[1]
    S = CMP_STRIDE

    def hidden(x_ref, which):
        lo = hi = None
        for j in range(0, S, 2):
            x = [x_ref[pl.ds(j + d, nrow, stride=S), :] for d in range(2)]
            for half in range(CMP_LEN // S):
                l0 = half * S + j
                xp = jnp.concatenate([(x[d] + pe_ref[which, l0 + d:l0 + d + 1, :]).astype(BF16)
                                      for d in range(2)], axis=1)
                t = _dot(xp, w1_ref[which, l0 // 2])
                if half == 0:
                    lo = t if lo is None else lo + t
                else:
                    hi = t if hi is None else hi + t
        return _gelu_tanh(lo + pltpu.roll(hi, nrow - 1, 0)).astype(BF16)

    kc_ref[0] = _dot(hidden(k_ref, 0), w2k_ref[...]).astype(kc_ref.dtype)
    hv = hidden(v_ref, 1)
    ones_half = lax.broadcasted_iota(jnp.int32, (nrow, LANES), 1) >= NSA_DH
    for g, o_ref in enumerate((vc0_ref, vc1_ref)):
        o_ref[0] = jnp.where(ones_half, 1.0, _dot(hv, w2v_ref[g])).astype(o_ref.dtype)


def _compress(pr, cmp_pe, cmp_w1, cmp_w2, batch, seq):
    nrow = seq // CMP_STRIDE
    hid = NSA_KV * CMP_HIDDEN
    w1 = cmp_w1.astype(BF16).reshape(2, CMP_LEN, NSA_DH, CMP_HIDDEN)
    z1 = jnp.zeros_like(w1)
    w1e = jnp.stack([jnp.concatenate([w1, z1], axis=-1), jnp.concatenate([z1, w1], axis=-1)], axis=2)
    w1e = w1e.reshape(2, CMP_LEN // 2, 2 * NSA_KV_WIDTH, hid)
    w2 = cmp_w2.astype(BF16)
    z2 = jnp.zeros_like(w2[0])
    w2k = jnp.concatenate([jnp.concatenate([w2[0], z2], axis=1),
                           jnp.concatenate([z2, w2[0]], axis=1)], axis=0)
    w2v = jnp.stack([jnp.pad(w2[1], ((g * CMP_HIDDEN, (NSA_KV - 1 - g) * CMP_HIDDEN),
                                     (0, LANES - NSA_DH))) for g in range(NSA_KV)])
    pe = jnp.concatenate([cmp_pe] * NSA_KV, axis=-1)
    tok_blk = pl.BlockSpec((seq, NSA_KV_WIDTH), lambda b: (b, 0))
    out_blk = pl.BlockSpec((1, nrow, LANES), lambda b: (b, 0, 0))
    return pl.pallas_call(
        _compress_kernel,
        out_shape=(jax.ShapeDtypeStruct((batch, nrow, LANES), BF16),) * 3,
        grid=(batch,),
        in_specs=[tok_blk, tok_blk,
                  _resident((2, CMP_LEN, NSA_KV_WIDTH)),
                  _resident((2, CMP_LEN // 2, 2 * NSA_KV_WIDTH, hid)),
                  _resident((hid, NSA_KV_WIDTH)),
                  _resident((NSA_KV, hid, LANES))],
        out_specs=(out_blk, out_blk, out_blk),
        compiler_params=pltpu.CompilerParams(dimension_semantics=("parallel",),
                                             vmem_limit_bytes=VMEM_LIMIT),
        name="compress",
    )(pr["kc"], pr["vc"], pe, w1e, w2k, w2v)


def _alibi_slope(h):
    return float(2.0 ** (-8.0 * (h + 1) / NSA_HEADS))


NSA_ROWS = NSA_HEADS * Q_BLOCK
WIN_KEYS = WINDOW + Q_BLOCK
F_SEL_BLOCK = 0
F_SEL_OFF = SEL_LEN
F_WIN_OFF = SEL_LEN + 1
F_WIN_BLK = SEL_LEN + 2
F_CMP = SEL_LEN + 3


def _tile_heads(x):
    return jnp.concatenate([x] * NSA_HEADS, axis=0)


def _nsa_kernel(nbs, q_ref, g_ref, kc_ref, vc0_ref, vc1_ref, ks_ref, vs0_ref, vs1_ref,
                kw_ref, vw0_ref, vw1_ref, ov_ref, gx_ref, y_ref,
                qe_scr, posf_scr, wposf_scr, cposf_scr, s_scr, mrun_scr, mb_scr, acc_scr):
    QB = Q_BLOCK
    KT = SEL_KT
    HALF = NSA_REP * QB
    t0 = pl.program_id(1) * QB
    ncmp = kc_ref.shape[1]
    lane = lax.broadcasted_iota(jnp.int32, (1, LANES), 1)
    lo_half = lane < NSA_DH

    @pl.when(pl.program_id(1) == 0)
    def _():
        seq = posf_scr.shape[0]
        k = lax.broadcasted_iota(jnp.int32, (seq, LANES), 0)
        c = lax.broadcasted_iota(jnp.int32, (seq, LANES), 1)
        posf = jnp.where(c < SEL_LEN, jnp.where(k // SEL_LEN == c, 1.0, 0.0),
                         jnp.where(c == F_SEL_OFF, (k % SEL_LEN).astype(F32), 0.0))
        posf_scr[...] = posf.astype(BF16)
        k = lax.broadcasted_iota(jnp.int32, (WIN_KEYS, LANES), 0)
        c = lax.broadcasted_iota(jnp.int32, (WIN_KEYS, LANES), 1)
        wposf = jnp.where(c == F_WIN_OFF, (k % SEL_LEN).astype(F32),
                          jnp.where(c == F_WIN_BLK, (k // SEL_LEN).astype(F32), 0.0))
        wposf_scr[...] = wposf.astype(BF16)
        k = lax.broadcasted_iota(jnp.int32, (ncmp, LANES), 0)
        c = lax.broadcasted_iota(jnp.int32, (ncmp, LANES), 1)
        cposf_scr[...] = jnp.where(c == F_CMP, k.astype(F32), 0.0).astype(BF16)

    def const_feat(h):
        s = _alibi_slope(h)
        return jnp.where((lane == F_SEL_OFF) | (lane == F_WIN_OFF), s,
                         jnp.where(lane == F_WIN_BLK, SEL_LEN * s,
                                   jnp.where(lane == F_CMP, CMP_STRIDE * s, 0.0)))

    for h in range(NSA_HEADS):
        rows = slice(h * QB, (h + 1) * QB)
        qe_scr[rows, 0:LANES] = q_ref[:, h * LANES:(h + 1) * LANES] * 0.125
        qe_scr[rows, LANES:2 * LANES] = jnp.broadcast_to(const_feat(h), (QB, LANES)).astype(BF16)

    def normalised(o, guard):
        outs = []
        for pair in range(NSA_HEADS // 2):
            e = o[(2 * pair) * QB:(2 * pair + 1) * QB]
            d = o[(2 * pair + 1) * QB:(2 * pair + 2) * QB]
            num = jnp.where(lo_half, e, pltpu.roll(d, NSA_DH, 1))
            den = jnp.where(lo_half, pltpu.roll(e, NSA_DH, 1), d)
            if guard:
                den = jnp.where(den > 0.0, den, 1.0)
            outs.append(num / den)
        return outs

    ws = pl.multiple_of(jnp.maximum(t0 - WINDOW, 0), QB)
    qs = lax.broadcasted_iota(jnp.int32, (QB, WIN_KEYS), 0)
    kl = lax.broadcasted_iota(jnp.int32, (QB, WIN_KEYS), 1)
    dist_w = (t0 - ws) + qs - kl
    ok_w = (dist_w >= 0) & (dist_w < WINDOW)
    s_w = _dot_nt(qe_scr[...], jnp.concatenate([kw_ref[pl.ds(ws, WIN_KEYS), :], wposf_scr[...]], axis=1))

    q_sub = lax.broadcasted_iota(jnp.int32, (QB, ncmp), 0)
    n_lane = lax.broadcasted_iota(jnp.int32, (QB, ncmp), 1)
    ok_c = n_lane * CMP_STRIDE + (CMP_LEN - 1) <= t0 + q_sub
    s = _dot_nt(qe_scr[...], jnp.concatenate([kc_ref[0], cposf_scr[...]], axis=1))
    s = s + _tile_heads(jnp.where(ok_c, 0.0, NEG))
    e = jnp.exp(s - jnp.max(s, axis=-1, keepdims=True)) * _tile_heads(jnp.where(ok_c, 1.0, 0.0))
    e_bf = e.astype(BF16)
    oc = jnp.concatenate([_dot(e_bf[0:HALF], vc0_ref[0]), _dot(e_bf[HALF:2 * HALF], vc1_ref[0])],
                         axis=0)
    o_cmp = normalised(oc, True)
    l_c = jnp.where(lo_half, pltpu.roll(oc, NSA_DH, 1), oc)
    inv_l = 1.0 / jnp.where(l_c > 0.0, l_c, 1.0)
    p_c = e * jnp.concatenate([inv_l] * (ncmp // LANES), axis=1)

    imp_ts = []
    for g in range(NSA_KV):
        psum = p_c[(g * NSA_REP) * QB:(g * NSA_REP + 1) * QB]
        for r in range(1, NSA_REP):
            psum = psum + p_c[(g * NSA_REP + r) * QB:(g * NSA_REP + r + 1) * QB]
        p_hi = psum.astype(BF16)
        p_lo = (psum - p_hi.astype(F32)).astype(BF16)
        imp = _dot(p_hi, ov_ref[...]) + _dot(p_lo, ov_ref[...])
        imp_ts.append(imp.T[0:nbs, :])

    n_grp = nbs // SUBLANES
    sub8 = lax.broadcasted_iota(jnp.int32, (SUBLANES, QB), 0)
    tpos = t0 + lax.broadcasted_iota(jnp.int32, (1, QB), 1)
    cur = tpos // SEL_LEN
    j_lane = lane.astype(F32)
    for g in range(NSA_KV):
        imp_t = imp_ts[g]
        j_sub = lax.broadcasted_iota(jnp.int32, (nbs, QB), 0)
        forced = (j_sub == 0) | (j_sub == cur) | (j_sub == cur - 1)
        score = jnp.where(j_sub <= cur, imp_t + jnp.where(forced, FORCE_BONUS, 0.0), NEG)
        parts = [score[a * SUBLANES:(a + 1) * SUBLANES, :] for a in range(n_grp)]
        ranks = [jnp.zeros((SUBLANES, QB), F32) for _ in range(n_grp)]
        for i in range(nbs):
            row = jnp.broadcast_to(score[i:i + 1, :], (SUBLANES, QB))
            for a in range(n_grp):
                if a * SUBLANES > i:
                    beats = jnp.where(row >= parts[a], 1.0, 0.0)
                elif (a + 1) * SUBLANES - 1 < i:
                    beats = jnp.where(row > parts[a], 1.0, 0.0)
                else:
                    beats = jnp.where(sub8 + a * SUBLANES > i,
                                      jnp.where(row >= parts[a], 1.0, 0.0),
                                      jnp.where(row > parts[a], 1.0, 0.0))
                ranks[a] = ranks[a] + beats
        sel_t = [jnp.where(rk < float(min(SEL_TOP, nbs)), 0.0, NEG) for rk in ranks]
        if nbs < LANES:
            sel_t.append(jnp.zeros((LANES - nbs, QB), F32))
        sel_bias = jnp.concatenate(sel_t, axis=0).T
        for r in range(NSA_REP):
            h = g * NSA_REP + r
            feat = jnp.where(lo_half, sel_bias + (_alibi_slope(h) * SEL_LEN) * j_lane, const_feat(h))
            qe_scr[h * QB:(h + 1) * QB, LANES:2 * LANES] = feat.astype(BF16)

    s_w = s_w + _tile_heads(jnp.where(ok_w, 0.0, NEG))
    p_w = jnp.exp(s_w - jnp.max(s_w, axis=-1, keepdims=True)).astype(BF16)
    o_win = normalised(jnp.concatenate([_dot(p_w[0:HALF], vw0_ref[pl.ds(ws, WIN_KEYS), :]),
                                        _dot(p_w[HALF:2 * HALF], vw1_ref[pl.ds(ws, WIN_KEYS), :])],
                                       axis=0), False)

    last_tile = t0 // KT
    mrun_scr[...] = jnp.full(mrun_scr.shape, NEG, F32)

    U = SEL_GROUP

    def score_group(kt0, count, causal):
        k0 = pl.multiple_of(kt0 * KT, KT)
        kx = jnp.concatenate([ks_ref[pl.ds(k0, count * KT), :],
                              posf_scr[pl.ds(k0, count * KT), :]], axis=1)
        s = _dot_nt(qe_scr[...], kx)
        tile_max = None
        for j in range(count):
            sj = s[:, j * KT:(j + 1) * KT] * LOG2_E
            if causal and j == count - 1:
                qs = lax.broadcasted_iota(jnp.int32, (QB, KT), 0)
                kl = lax.broadcasted_iota(jnp.int32, (QB, KT), 1)
                sj = sj + _tile_heads(jnp.where(k0 + j * KT + kl <= t0 + qs, 0.0, NEG))
            s_scr[kt0 + j] = sj
            for c0 in range(0, KT, LANES):
                part = sj[:, c0:c0 + LANES]
                tile_max = part if tile_max is None else jnp.maximum(tile_max, part)
        mrun_scr[...] = jnp.maximum(mrun_scr[...], tile_max)

    def score_full_group(i, carry):
        score_group(i * U, U, False)
        return carry

    lax.fori_loop(0, last_tile // U, score_full_group, 0)
    for r in range(U):
        @pl.when(last_tile % U == r)
        def _(r=r):
            score_group(last_tile - r, r + 1, True)

    mb_scr[...] = jnp.broadcast_to(jnp.max(mrun_scr[...], axis=-1, keepdims=True), mb_scr.shape)
    acc_scr[...] = jnp.zeros(acc_scr.shape, F32)

    def value_group(kt0, count):
        k0 = pl.multiple_of(kt0 * KT, KT)
        mb = mb_scr[...]
        mbw = jnp.concatenate([mb] * (KT // LANES), axis=1)
        p = jnp.concatenate([jnp.exp2(s_scr[kt0 + j] - mbw).astype(BF16) for j in range(count)],
                            axis=1)
        acc_scr[0:HALF, :] += _dot(p[0:HALF], vs0_ref[pl.ds(k0, count * KT), :])
        acc_scr[HALF:2 * HALF, :] += _dot(p[HALF:2 * HALF], vs1_ref[pl.ds(k0, count * KT), :])

    def value_full_group(i, carry):
        value_group(i * U, U)
        return carry

    n_tiles = last_tile + 1
    lax.fori_loop(0, n_tiles // U, value_full_group, 0)
    for r in range(1, U):
        @pl.when(n_tiles % U == r)
        def _(r=r):
            value_group(n_tiles - r, r)

    o_sel = normalised(acc_scr[...], False)

    gate = jax.nn.sigmoid(g_ref[...])
    g_hi = gate.astype(BF16)
    g_lo = (gate - g_hi.astype(F32)).astype(BF16)
    spread = _dot(g_hi, gx_ref[...]) + _dot(g_lo, gx_ref[...])
    for pair in range(NSA_HEADS // 2):
        y = None
        for b, o in enumerate((o_cmp, o_sel, o_win)):
            c0 = b * NSA_WIDTH + pair * LANES
            term = spread[:, c0:c0 + LANES] * o[pair]
            y = term if y is None else y + term
        y_ref[:, pair * LANES:(pair + 1) * LANES] = y.astype(y_ref.dtype)


def _nsa(pr, kc, vc0, vc1, batch, seq):
    QB = Q_BLOCK
    nq = seq // QB
    nbs = seq // SEL_LEN
    ncmp = kc.shape[1]
    cs = np.arange(ncmp) * CMP_STRIDE
    js = np.arange(nbs) * SEL_LEN
    ov = np.zeros((ncmp, LANES), np.float32)
    ov[:, :nbs] = (cs[:, None] < js[None, :] + SEL_LEN) & (cs[:, None] + CMP_LEN > js[None, :])
    gx = np.zeros((LANES, 3 * NSA_WIDTH), np.float32)
    for h in range(NSA_HEADS):
        for b in range(3):
            gx[SUBLANES + 3 * h + b, b * NSA_WIDTH + h * NSA_DH:b * NSA_WIDTH + (h + 1) * NSA_DH] = 1.0
    tok = lambda b, i: (b * nq + i, 0)
    per_b = lambda b, i: (b, 0)
    per_b3 = lambda b, i: (b, 0, 0)
    kv_spec = pl.BlockSpec((seq, LANES), per_b)
    cmp_spec = pl.BlockSpec((1, ncmp, LANES), per_b3)
    return pl.pallas_call(
        functools.partial(_nsa_kernel, nbs),
        out_shape=jax.ShapeDtypeStruct((batch * seq, NSA_WIDTH), BF16),
        grid=(batch, nq),
        in_specs=[pl.BlockSpec((QB, NSA_HEADS * LANES), tok),
                  pl.BlockSpec((QB, LANES), tok),
                  cmp_spec, cmp_spec, cmp_spec,
                  kv_spec, kv_spec, kv_spec, kv_spec, kv_spec, kv_spec,
                  pl.BlockSpec((ncmp, LANES), lambda b, i: (0, 0)),
                  pl.BlockSpec((LANES, 3 * NSA_WIDTH), lambda b, i: (0, 0))],
        out_specs=pl.BlockSpec((QB, NSA_WIDTH), tok),
        scratch_shapes=[pltpu.VMEM((NSA_ROWS, 2 * LANES), BF16),
                        pltpu.VMEM((seq, LANES), BF16),
                        pltpu.VMEM((WIN_KEYS, LANES), BF16),
                        pltpu.VMEM((ncmp, LANES), BF16),
                        pltpu.VMEM((seq // SEL_KT, NSA_ROWS, SEL_KT), F32),
                        pltpu.VMEM((NSA_ROWS, LANES), F32),
                        pltpu.VMEM((NSA_ROWS, LANES), F32),
                        pltpu.VMEM((NSA_ROWS, LANES), F32)],
        compiler_params=pltpu.CompilerParams(dimension_semantics=("parallel", "arbitrary"),
                                             vmem_limit_bytes=VMEM_LIMIT),
        name="nsa",
    )(pr["nq"], pr["gcol"], kc, vc0, vc1, pr["ks"], pr["vs0"], pr["vs1"],
      pr["kw"], pr["vw0"], pr["vw1"], jnp.asarray(ov, BF16), jnp.asarray(gx, BF16))


def _memkv_kernel(mem_ref, g_ref, w_ref, kv_ref):
    u = _rms(mem_ref[0], g_ref[...]).astype(BF16)
    kv_ref[0] = _dot(u, w_ref[...]).astype(kv_ref.dtype)


def _memkv(mem, g_mem, w_mem_kv):
    batch, n_mem, d = mem.shape
    return pl.pallas_call(
        _memkv_kernel,
        out_shape=jax.ShapeDtypeStruct((batch, n_mem, 2 * XA_WIDTH), BF16),
        grid=(batch,),
        in_specs=[pl.BlockSpec((1, n_mem, d), lambda b: (b, 0, 0)),
                  _resident((1, d)), _resident((d, 2 * XA_WIDTH))],
        out_specs=pl.BlockSpec((1, n_mem, 2 * XA_WIDTH), lambda b: (b, 0, 0)),
        compiler_params=pltpu.CompilerParams(dimension_semantics=("parallel",),
                                             vmem_limit_bytes=VMEM_LIMIT),
        name="memkv",
    )(mem, g_mem[None, :], w_mem_kv.astype(BF16))


def _tail_kernel(x_ref, yml_ref, ynsa_ref, xq_ref, kv_ref, mg_ref, wb_ref, wo_ref, gf_ref,
                 w1_ref, w2_ref, gl_ref, out_ref):
    d = x_ref.shape[1]
    kv = kv_ref[0]
    heads = [slice(hh * XA_DH, (hh + 1) * XA_DH) for hh in range(XA_HEADS)]
    scores = [_dot_nt(xq_ref[:, sl], kv[:, sl]) * (XA_DH ** -0.5) for sl in heads]
    ups = [_dot(yml_ref[...], wb_ref[0]), _dot(ynsa_ref[...], wb_ref[1])]
    y_xa = []
    for hh, s in enumerate(scores):
        p = jnp.exp(s - jnp.max(s, axis=-1, keepdims=True))
        l = jnp.sum(p, axis=-1, keepdims=True)
        y_xa.append((_dot(p.astype(BF16), kv[:, XA_WIDTH + hh * XA_DH:XA_WIDTH + (hh + 1) * XA_DH])
                     / l).astype(BF16))
    ups.append(_dot(jnp.concatenate(y_xa, axis=-1), wb_ref[2]))
    merged = None
    for j in range(N_BRANCH):
        term = mg_ref[:, j * d:(j + 1) * d] * ups[j]
        merged = term if merged is None else merged + term
    h = x_ref[...] + _dot(merged.astype(BF16), wo_ref[...])
    hb = (h * gf_ref[...]).astype(BF16)
    r = lax.rsqrt(jnp.mean(h * h, axis=-1, keepdims=True) + EPS)
    acc = h
    for c0 in range(0, w1_ref.shape[1], FF_SLAB):
        a = jnp.maximum(r * _dot(hb, w1_ref[:, c0:c0 + FF_SLAB]), 0.0)
        acc = acc + _dot((a * a).astype(BF16), w2_ref[c0:c0 + FF_SLAB, :])
    out_ref[...] = _rms(acc, gl_ref[...])


def _tail(x2, y_ml, y_nsa, pr, mem_kv, w_branch, w_out, g_ffn, w_ff1, w_ff2, g_final, seq):
    n_tok, d = x2.shape
    tm = TM_TAIL
    d_ff = w_ff1.shape[1]
    n_mem = mem_kv.shape[1]
    tiles_per_b = seq // tm
    row = lambda i: (i, 0)
    return pl.pallas_call(
        _tail_kernel,
        out_shape=jax.ShapeDtypeStruct((n_tok, d), F32),
        grid=(n_tok // tm,),
        in_specs=[pl.BlockSpec((tm, d), row),
                  pl.BlockSpec((tm, ML_WIDTH), row),
                  pl.BlockSpec((tm, NSA_WIDTH), row),
                  pl.BlockSpec((tm, XA_WIDTH), row),
                  pl.BlockSpec((1, n_mem, 2 * XA_WIDTH), lambda i: (i // tiles_per_b, 0, 0)),
                  pl.BlockSpec((tm, N_BRANCH * d), row),
                  _resident((N_BRANCH, ML_WIDTH, d)), _resident((d, d)), _resident((1, d)),
                  _resident((d, d_ff)), _resident((d_ff, d)), _resident((1, d))],
        out_specs=pl.BlockSpec((tm, d), row),
        compiler_params=pltpu.CompilerParams(dimension_semantics=("parallel",),
                                             vmem_limit_bytes=VMEM_LIMIT),
        name="tail",
    )(x2, y_ml, y_nsa, pr["xq"], mem_kv, pr["mg"], w_branch.astype(BF16), w_out.astype(BF16),
      g_ffn[None, :], w_ff1.astype(BF16), w_ff2.astype(BF16), g_final[None, :])


def _layer(x, mem, g_mix, w_in, b_in, ml_conv, ml_norm_g, cmp_pe, cmp_w1, cmp_w2, g_mem, w_mem_kv,
           w_branch, w_out, g_ffn, w_ff1, w_ff2, g_final):
    batch, seq, d = x.shape
    assert seq % (SEL_LEN * SUBLANES) == 0 and seq % SEL_KT == 0 and seq >= WIN_KEYS
    assert seq // SEL_LEN <= NSA_DH, "one selection-block feature lane per block"
    assert seq // CMP_STRIDE <= 256, "compressed token index must be exact in bf16"
    assert seq % TM_PROJ == 0 and seq % TM_TAIL == 0 and TM_PROJ % ML_CHUNK == 0
    x2 = x.reshape(batch * seq, d)
    pr = _inproj(x2, g_mix, w_in, b_in, ml_conv, ml_norm_g, seq)
    y_ml = pr["y_ml"]
    kc, vc0, vc1 = _compress(pr, cmp_pe, cmp_w1, cmp_w2, batch, seq)
    y_nsa = _nsa(pr, kc, vc0, vc1, batch, seq)
    mem_kv = _memkv(mem, g_mem, w_mem_kv)
    out = _tail(x2, y_ml, y_nsa, pr, mem_kv, w_branch, w_out, g_ffn, w_ff1, w_ff2, g_final, seq)
    return out.reshape(batch, seq, d)


def kernel(x, mem, g_mix, w_in, b_in, ml_conv, ml_norm_g, cmp_pe, cmp_w1, cmp_w2, g_mem, w_mem_kv,
           w_branch, w_out, g_ffn, w_ff1, w_ff2, g_final):
    assert g_mix.shape[0] == 1, "single-layer block"
    return _layer(x, mem, g_mix[0], w_in[0], b_in[0], ml_conv[0], ml_norm_g[0], cmp_pe[0],
                  cmp_w1[0], cmp_w2[0], g_mem[0], w_mem_kv[0], w_branch[0], w_out[0], g_ffn[0],
                  w_ff1[0], w_ff2[0], g_final)
```

```python
import functools

import numpy as np
import jax
import jax.numpy as jnp
from jax import lax
from jax.experimental import pallas as pl
from jax.experimental.pallas import tpu as pltpu

F32 = jnp.float32
BF16 = jnp.bfloat16

EPS = 1e-6
NEG = -1e30
LOG2_E = 1.4426950408889634
ML_HEADS = 4
ML_DH = 128
ML_WIDTH = ML_HEADS * ML_DH
ML_CONV = 4
ML_CHUNK = 128
NSA_HEADS = 8
NSA_KV = 2
NSA_REP = NSA_HEADS // NSA_KV
NSA_DH = 64
NSA_WIDTH = NSA_HEADS * NSA_DH
NSA_KV_WIDTH = NSA_KV * NSA_DH
CMP_LEN = 32
CMP_STRIDE = 16
CMP_HIDDEN = 256
SEL_LEN = 64
SEL_TOP = 16
WINDOW = 512
Q_BLOCK = 128
FORCE_BONUS = 1e3
SEL_KT = 256
SEL_GROUP = 8
XA_HEADS = 4
XA_DH = 128
XA_WIDTH = XA_HEADS * XA_DH
N_BRANCH = 3

LANES = 128
SUBLANES = 8
VMEM_LIMIT = 56 * 1024 * 1024

TM_PROJ = 256
TM_TAIL = 256
FF_SLAB = 1024

_NT = (((1,), (1,)), ((), ()))
_TN = (((0,), (0,)), ((), ()))


def _dot(a, b):
    return jnp.dot(a, b, preferred_element_type=F32)


def _dot_nt(a, b):
    return lax.dot_general(a, b, _NT, preferred_element_type=F32)


def _resident(shape):
    nd = len(shape)
    return pl.BlockSpec(shape, lambda *_: (0,) * nd, pipeline_mode=pl.Buffered(1))


def _rms(x, g):
    return x * lax.rsqrt(jnp.mean(x * x, axis=-1, keepdims=True) + EPS) * g


_PROJ_COLS = (
    ("qk", 2 * ML_WIDTH, BF16),
    ("v", ML_WIDTH, BF16),
    ("o", ML_WIDTH, BF16),
    ("gcol", LANES, F32),
    ("kc", NSA_KV_WIDTH, F32),
    ("vc", NSA_KV_WIDTH, F32),
    ("ks", NSA_KV_WIDTH, BF16),
    ("kw", NSA_KV_WIDTH, BF16),
    ("vs0", LANES, BF16),
    ("vs1", LANES, BF16),
    ("vw0", LANES, BF16),
    ("vw1", LANES, BF16),
    ("nq", NSA_HEADS * LANES, BF16),
    ("xq", XA_WIDTH, BF16),
    ("mg", None, BF16),
)
_PROJ_MLSTM_ONLY = ("qk", "v", "o")
_PROJ_CHUNK = 512


def _proj_layout(d_model):
    cols, off = [], 0
    for name, width, dt in _PROJ_COLS:
        width = N_BRANCH * d_model if width is None else width
        cols.append((name, off, width, dt))
        off += width
    return cols, off


def _inproj_kernel(cols, tiles_per_b, x_ref, g_ref, w_ref, b_ref, wt_ref, bt_ref, conv_ref, ng_ref,
                   *refs):
    out_names = [c[0] for c in cols if c[0] not in _PROJ_MLSTM_ONLY]
    out_refs = dict(zip(out_names, refs[:len(out_names)]))
    yml_ref, xbuf, c_scr, m_scr = refs[len(out_names):]
    tm = x_ref.shape[0]
    halo = SUBLANES

    @pl.when(pl.program_id(0) % tiles_per_b == 0)
    def _():
        xbuf[0:halo, :] = jnp.zeros((halo, xbuf.shape[1]), F32)
        c_scr[...] = jnp.zeros(c_scr.shape, F32)
        m_scr[...] = jnp.zeros(m_scr.shape, F32)

    x = x_ref[...]
    xg = x * g_ref[...]
    xb = xg.astype(BF16)
    r = lax.rsqrt(jnp.mean(x * x, axis=-1, keepdims=True) + EPS)
    grow = _dot_nt(wt_ref[...], (xg * r).astype(BF16)) + bt_ref[...]
    kept = {name: [] for name in _PROJ_MLSTM_ONLY + ("gcol",)}
    mlstm_ready = max(off + width for name, off, width, _ in cols if name in kept)
    ncols = w_ref.shape[1]
    for c0 in range(0, ncols, _PROJ_CHUNK):
        cw = min(_PROJ_CHUNK, ncols - c0)
        chunk = r * _dot(xb, w_ref[:, c0:c0 + cw]) + b_ref[:, c0:c0 + cw]
        for name, off, width, dt in cols:
            lo, hi = max(off, c0), min(off + width, c0 + cw)
            if lo >= hi:
                continue
            acc = chunk[:, lo - c0:hi - c0]
            cs = slice(lo - off, hi - off)
            if name == "qk":
                xbuf[halo:halo + tm, cs] = acc
                y = acc * conv_ref[0:1, cs]
                for j in range(1, ML_CONV):
                    y = y + xbuf[halo - j:halo - j + tm, cs] * conv_ref[j:j + 1, cs]
                xbuf[0:halo, cs] = xbuf[tm:tm + halo, cs]
                acc = y * jax.nn.sigmoid(y)
                if lo - off >= ML_WIDTH:
                    acc = acc * (ML_DH ** -0.5)
            elif name in ("o", "mg"):
                acc = jax.nn.sigmoid(acc)
            acc = acc.astype(dt)
            if name in kept:
                kept[name].append(acc)
            if name in out_refs:
                out_refs[name][:, cs] = acc
        if c0 < mlstm_ready <= c0 + cw:
            qk, v, og = (jnp.concatenate(kept[n], axis=1) for n in _PROJ_MLSTM_ONLY)
            gcol = kept["gcol"][0]
            c_state = [c_scr[hh] for hh in range(ML_HEADS)]
            m_state = [m_scr[hh:hh + 1, 0:1] for hh in range(ML_HEADS)]
            for r0 in range(0, tm, ML_CHUNK):
                rows = slice(r0, r0 + ML_CHUNK)
                y_ml, c_state, m_state = _mlstm_chunk(qk[rows], v[rows], og[rows], gcol[rows],
                                                      grow[:, rows], ng_ref[...], c_state, m_state)
                yml_ref[rows, :] = y_ml
            for hh in range(ML_HEADS):
                c_scr[hh] = c_state[hh]
                m_scr[hh:hh + 1, :] = jnp.broadcast_to(m_state[hh], (1, m_scr.shape[1]))


def _split_w_in(w_in, b_in):
    widths = (ML_WIDTH, ML_WIDTH, ML_WIDTH, ML_WIDTH, ML_HEADS, ML_HEADS,
              NSA_WIDTH, NSA_KV_WIDTH, NSA_KV_WIDTH, NSA_KV_WIDTH, NSA_KV_WIDTH, NSA_KV_WIDTH,
              NSA_KV_WIDTH, 3 * NSA_HEADS, XA_WIDTH, w_in.shape[1])
    names = ("ml_q", "ml_k", "ml_v", "ml_o", "ml_i", "ml_f", "ns_q", "ns_kc", "ns_vc", "ns_ks",
             "ns_vs", "ns_kw", "ns_vw", "ns_g", "xa_q", "mg")
    out, off = {}, 0
    for n, wd in zip(names, widths):
        end = w_in.shape[1] if n == "mg" else off + wd
        out[n] = (w_in[:, off:end], b_in[off:end])
        off = end
    return out


def _inproj(x2, g_mix, w_in, b_in, ml_conv, ml_norm_g, seq):
    n_tok, d = x2.shape
    cols, ncols = _proj_layout(d)
    p = _split_w_in(w_in, b_in)

    def pad_cols(w, b, width):
        return (jnp.pad(w, ((0, 0), (0, width - w.shape[1]))), jnp.pad(b, (0, width - b.shape[0])))

    wq, bq = p["ns_q"]
    wq = wq.reshape(d, NSA_HEADS, 1, NSA_DH)
    bq = bq.reshape(NSA_HEADS, 1, NSA_DH)
    in_group = jnp.asarray(np.arange(NSA_HEADS)[:, None] // NSA_REP == np.arange(NSA_KV)[None, :],
                           w_in.dtype)[:, :, None]
    wq_slots = wq * in_group
    bq_slots = bq * in_group
    gate_w = jnp.concatenate([p["ml_i"][0], p["ml_f"][0]], axis=1)
    gate_b = jnp.concatenate([p["ml_i"][1], p["ml_f"][1]])
    small_w = jnp.concatenate([gate_w, p["ns_g"][0]], axis=1)
    small_b = jnp.concatenate([gate_b, p["ns_g"][1]])

    def with_ones(wb, g):
        w, b = wb
        w = jnp.pad(w[:, g * NSA_DH:(g + 1) * NSA_DH], ((0, 0), (0, LANES - NSA_DH)))
        b = jnp.concatenate([b[g * NSA_DH:(g + 1) * NSA_DH], jnp.ones((LANES - NSA_DH,), b.dtype)])
        return w, b

    pieces = {
        "qk": (jnp.concatenate([p["ml_q"][0], p["ml_k"][0]], axis=1),
               jnp.concatenate([p["ml_q"][1], p["ml_k"][1]])),
        "o": p["ml_o"], "mg": p["mg"],
        "gcol": pad_cols(small_w, small_b, LANES),
        "kc": p["ns_kc"], "vc": p["ns_vc"], "v": p["ml_v"],
        "nq": (wq_slots.reshape(d, NSA_HEADS * LANES), bq_slots.reshape(NSA_HEADS * LANES)),
        "xq": p["xa_q"], "ks": p["ns_ks"], "kw": p["ns_kw"],
        "vs0": with_ones(p["ns_vs"], 0), "vs1": with_ones(p["ns_vs"], 1),
        "vw0": with_ones(p["ns_vw"], 0), "vw1": with_ones(p["ns_vw"], 1),
    }
    w_cols = jnp.concatenate([pieces[name][0] for name, *_ in cols], axis=1).astype(BF16)
    b_cols = jnp.concatenate([pieces[name][1] for name, *_ in cols])[None, :]
    w_rows = gate_w.T.astype(BF16)
    b_rows = gate_b[:, None]

    tm = TM_PROJ
    tiles_per_b = seq // tm
    outs_hbm = [c for c in cols if c[0] not in _PROJ_MLSTM_ONLY] + [("y_ml", None, ML_WIDTH, BF16)]
    outs = pl.pallas_call(
        functools.partial(_inproj_kernel, cols, tiles_per_b),
        out_shape=[jax.ShapeDtypeStruct((n_tok, width), dt) for _, _, width, dt in outs_hbm],
        grid=(n_tok // tm,),
        in_specs=[pl.BlockSpec((tm, d), lambda i: (i, 0)),
                  _resident((1, d)), _resident((d, ncols)), _resident((1, ncols)),
                  _resident((SUBLANES, d)), _resident((SUBLANES, 1)),
                  _resident((ML_CONV, 2 * ML_WIDTH)), _resident((1, ML_WIDTH))],
        out_specs=[pl.BlockSpec((tm, width), lambda i: (i, 0)) for _, _, width, _ in outs_hbm],
        scratch_shapes=[pltpu.VMEM((tm + SUBLANES, 2 * ML_WIDTH), F32),
                        pltpu.VMEM((ML_HEADS, ML_DH, 2 * ML_DH), F32),
                        pltpu.VMEM((SUBLANES, LANES), F32)],
        compiler_params=pltpu.CompilerParams(dimension_semantics=("arbitrary",),
                                             vmem_limit_bytes=VMEM_LIMIT),
        name="inproj",
    )(x2, g_mix[None, :], w_cols, b_cols, w_rows, b_rows, ml_conv, ml_norm_g[None, :])
    return dict(zip([c[0] for c in outs_hbm], outs))


def _mlstm_chunk(qk, v, og, gcol, grow, ng, c_prev, m_prev):
    L = ML_CHUNK
    r_i = lax.broadcasted_iota(jnp.int32, (L, L), 0)
    c_i = lax.broadcasted_iota(jnp.int32, (L, L), 1)
    causal = c_i <= r_i
    tril = causal.astype(F32)
    triu = (r_i <= c_i).astype(F32)
    b_col = jnp.dot(tril, jax.nn.log_sigmoid(gcol), precision=lax.Precision.HIGHEST,
                    preferred_element_type=F32)
    b_row = jnp.dot(jax.nn.log_sigmoid(grow), triu, precision=lax.Precision.HIGHEST,
                    preferred_element_type=F32)
    ones_col = jnp.where(lax.broadcasted_iota(jnp.int32, (L, ML_DH), 1) == 0, 1.0, 0.0).astype(BF16)

    def operands(hh):
        sl = slice(hh * ML_DH, (hh + 1) * ML_DH)
        k = qk[:, ML_WIDTH + hh * ML_DH:ML_WIDTH + (hh + 1) * ML_DH]
        return qk[:, sl], k, jnp.concatenate([v[:, sl], ones_col], axis=1), c_prev[hh]

    ops = [operands(hh) for hh in range(ML_HEADS)]
    qk_t = [_dot_nt(q, k) for q, k, _, _ in ops]
    inter = [_dot(q, c.astype(BF16)) for q, _, _, c in ops]

    stab, ke_all, s_all = [], [], []
    for hh in range(ML_HEADS):
        li_c = gcol[:, hh:hh + 1]
        b_c = b_col[:, ML_HEADS + hh:ML_HEADS + hh + 1]
        li_r = grow[hh:hh + 1, :]
        b_r = b_row[ML_HEADS + hh:ML_HEADS + hh + 1, :]
        g = b_c[L - 1:L, :]
        rel = jnp.where(causal, li_r - b_r, -jnp.inf)
        mm = jnp.maximum(m_prev[hh], jnp.max(rel, axis=-1, keepdims=True))
        m_t = b_c + mm
        s_all.append((qk_t[hh] * jnp.exp(rel - mm)).astype(BF16))
        w_end = g - b_c + li_c
        m_loc = jnp.max(w_end, axis=0, keepdims=True)
        ke_all.append((ops[hh][1] * jnp.exp(w_end - m_loc)).astype(BF16))
        m_new = jnp.maximum(g + m_prev[hh], m_loc)
        stab.append((m_t, jnp.exp(m_prev[hh] - mm), jnp.exp(g + m_prev[hh] - m_new),
                     jnp.exp(m_loc - m_new), m_new))

    intra = [_dot(s, o[2]) for s, o in zip(s_all, ops)]
    a_c = [lax.dot_general(ke, o[2], _TN, preferred_element_type=F32) for ke, o in zip(ke_all, ops)]

    ys, new_c, new_m = [], [], []
    for hh in range(ML_HEADS):
        sl = slice(hh * ML_DH, (hh + 1) * ML_DH)
        m_t, sc, a, bb, m_new = stab[hh]
        nd = intra[hh] + sc * inter[hh]
        den = nd[:, ML_DH:ML_DH + 1]
        h = nd[:, 0:ML_DH] / jnp.maximum(jnp.abs(den), jnp.exp(-m_t))
        hn = h * lax.rsqrt(jnp.mean(h * h, axis=-1, keepdims=True) + EPS)
        ys.append((og[:, sl] * hn * ng[:, sl]).astype(BF16))
        new_c.append(a * ops[hh][3] + bb * a_c[hh])
        new_m.append(m_new)
    return jnp.concatenate(ys, axis=1), new_c, new_m


def _gelu_tanh(x):
    return 0.5 * x * (1.0 + jnp.tanh(np.sqrt(2.0 / np.pi) * (x + 0.044715 * (x * x * x))))


def _compress_kernel(k_ref, v_ref, pe_ref, w1_ref, w2k_ref, w2v_ref, kc_ref, vc0_ref, vc1_ref):
    nrow = kc_ref.shape[1]
    S = CMP_STRIDE

    def hidden(x_ref, which):
        lo = hi = None
        for j in range(0, S, 2):
            x = [x_ref[pl.ds(j + d, nrow, stride=S), :] for d in range(2)]
            for half in range(CMP_LEN // S):
                l0 = half * S + j
                xp = jnp.concatenate([(x[d] + pe_ref[which, l0 + d:l0 + d + 1, :]).astype(BF16)
                                      for d in range(2)], axis=1)
                t = _dot(xp, w1_ref[which, l0 // 2])
                if half == 0:
                    lo = t if lo is None else lo + t
                else:
                    hi = t if hi is None else hi + t
        return _gelu_tanh(lo + pltpu.roll(hi, nrow - 1, 0)).astype(BF16)

    kc_ref[0] = _dot(hidden(k_ref, 0), w2k_ref[...]).astype(kc_ref.dtype)
    hv = hidden(v_ref, 1)
    ones_half = lax.broadcasted_iota(jnp.int32, (nrow, LANES), 1) >= NSA_DH
    for g, o_ref in enumerate((vc0_ref, vc1_ref)):
        o_ref[0] = jnp.where(ones_half, 1.0, _dot(hv, w2v_ref[g])).astype(o_ref.dtype)


def _compress(pr, cmp_pe, cmp_w1, cmp_w2, batch, seq):
    nrow = seq // CMP_STRIDE
    hid = NSA_KV * CMP_HIDDEN
    w1 = cmp_w1.astype(BF16).reshape(2, CMP_LEN, NSA_DH, CMP_HIDDEN)
    z1 = jnp.zeros_like(w1)
    w1e = jnp.stack([jnp.concatenate([w1, z1], axis=-1), jnp.concatenate([z1, w1], axis=-1)], axis=2)
    w1e = w1e.reshape(2, CMP_LEN // 2, 2 * NSA_KV_WIDTH, hid)
    w2 = cmp_w2.astype(BF16)
    z2 = jnp.zeros_like(w2[0])
    w2k = jnp.concatenate([jnp.concatenate([w2[0], z2], axis=1),
                           jnp.concatenate([z2, w2[0]], axis=1)], axis=0)
    w2v = jnp.stack([jnp.pad(w2[1], ((g * CMP_HIDDEN, (NSA_KV - 1 - g) * CMP_HIDDEN),
                                     (0, LANES - NSA_DH))) for g in range(NSA_KV)])
    pe = jnp.concatenate([cmp_pe] * NSA_KV, axis=-1)
    tok_blk = pl.BlockSpec((seq, NSA_KV_WIDTH), lambda b: (b, 0))
    out_blk = pl.BlockSpec((1, nrow, LANES), lambda b: (b, 0, 0))
    return pl.pallas_call(
        _compress_kernel,
        out_shape=(jax.ShapeDtypeStruct((batch, nrow, LANES), BF16),) * 3,
        grid=(batch,),
        in_specs=[tok_blk, tok_blk,
                  _resident((2, CMP_LEN, NSA_KV_WIDTH)),
                  _resident((2, CMP_LEN // 2, 2 * NSA_KV_WIDTH, hid)),
                  _resident((hid, NSA_KV_WIDTH)),
                  _resident((NSA_KV, hid, LANES))],
        out_specs=(out_blk, out_blk, out_blk),
        compiler_params=pltpu.CompilerParams(dimension_semantics=("parallel",),
                                             vmem_limit_bytes=VMEM_LIMIT),
        name="compress",
    )(pr["kc"], pr["vc"], pe, w1e, w2k, w2v)


def _alibi_slope(h):
    return float(2.0 ** (-8.0 * (h + 1) / NSA_HEADS))


NSA_ROWS = NSA_HEADS * Q_BLOCK
WIN_KEYS = WINDOW + Q_BLOCK
F_SEL_BLOCK = 0
F_SEL_OFF = SEL_LEN
F_WIN_OFF = SEL_LEN + 1
F_WIN_BLK = SEL_LEN + 2
F_CMP = SEL_LEN + 3


def _tile_heads(x):
    return jnp.concatenate([x] * NSA_HEADS, axis=0)


def _nsa_kernel(nbs, q_ref, g_ref, kc_ref, vc0_ref, vc1_ref, ks_ref, vs0_ref, vs1_ref,
                kw_ref, vw0_ref, vw1_ref, ov_ref, gx_ref, y_ref,
                qe_scr, posf_scr, wposf_scr, cposf_scr, s_scr, mb_scr, acc_scr):
    QB = Q_BLOCK
    KT = SEL_KT
    HALF = NSA_REP * QB
    t0 = pl.program_id(1) * QB
    ncmp = kc_ref.shape[1]
    lane = lax.broadcasted_iota(jnp.int32, (1, LANES), 1)
    lo_half = lane < NSA_DH

    @pl.when(pl.program_id(1) == 0)
    def _():
        seq = posf_scr.shape[0]
        k = lax.broadcasted_iota(jnp.int32, (seq, LANES), 0)
        c = lax.broadcasted_iota(jnp.int32, (seq, LANES), 1)
        posf = jnp.where(c < SEL_LEN, jnp.where(k // SEL_LEN == c, 1.0, 0.0),
                         jnp.where(c == F_SEL_OFF, (k % SEL_LEN).astype(F32), 0.0))
        posf_scr[...] = posf.astype(BF16)
        k = lax.broadcasted_iota(jnp.int32, (WIN_KEYS, LANES), 0)
        c = lax.broadcasted_iota(jnp.int32, (WIN_KEYS, LANES), 1)
        wposf = jnp.where(c == F_WIN_OFF, (k % SEL_LEN).astype(F32),
                          jnp.where(c == F_WIN_BLK, (k // SEL_LEN).astype(F32), 0.0))
        wposf_scr[...] = wposf.astype(BF16)
        k = lax.broadcasted_iota(jnp.int32, (ncmp, LANES), 0)
        c = lax.broadcasted_iota(jnp.int32, (ncmp, LANES), 1)
        cposf_scr[...] = jnp.where(c == F_CMP, k.astype(F32), 0.0).astype(BF16)

    def const_feat(h):
        s = _alibi_slope(h)
        return jnp.where((lane == F_SEL_OFF) | (lane == F_WIN_OFF), s,
                         jnp.where(lane == F_WIN_BLK, SEL_LEN * s,
                                   jnp.where(lane == F_CMP, CMP_STRIDE * s, 0.0)))

    for h in range(NSA_HEADS):
        rows = slice(h * QB, (h + 1) * QB)
        qe_scr[rows, 0:LANES] = q_ref[:, h * LANES:(h + 1) * LANES] * 0.125
        qe_scr[rows, LANES:2 * LANES] = jnp.broadcast_to(const_feat(h), (QB, LANES)).astype(BF16)

    def normalised(o, guard):
        outs = []
        for pair in range(NSA_HEADS // 2):
            e = o[(2 * pair) * QB:(2 * pair + 1) * QB]
            d = o[(2 * pair + 1) * QB:(2 * pair + 2) * QB]
            num = jnp.where(lo_half, e, pltpu.roll(d, NSA_DH, 1))
            den = jnp.where(lo_half, pltpu.roll(e, NSA_DH, 1), d)
            if guard:
                den = jnp.where(den > 0.0, den, 1.0)
            outs.append(num / den)
        return outs

    ws = pl.multiple_of(jnp.maximum(t0 - WINDOW, 0), QB)
    qs = lax.broadcasted_iota(jnp.int32, (QB, WIN_KEYS), 0)
    kl = lax.broadcasted_iota(jnp.int32, (QB, WIN_KEYS), 1)
    dist_w = (t0 - ws) + qs - kl
    ok_w = (dist_w >= 0) & (dist_w < WINDOW)
    s_w = _dot_nt(qe_scr[...], jnp.concatenate([kw_ref[pl.ds(ws, WIN_KEYS), :], wposf_scr[...]], axis=1))

    q_sub = lax.broadcasted_iota(jnp.int32, (QB, ncmp), 0)
    n_lane = lax.broadcasted_iota(jnp.int32, (QB, ncmp), 1)
    ok_c = n_lane * CMP_STRIDE + (CMP_LEN - 1) <= t0 + q_sub
    s = _dot_nt(qe_scr[...], jnp.concatenate([kc_ref[0], cposf_scr[...]], axis=1))
    s = s + _tile_heads(jnp.where(ok_c, 0.0, NEG))
    e = jnp.exp(s - jnp.max(s, axis=-1, keepdims=True)) * _tile_heads(jnp.where(ok_c, 1.0, 0.0))
    e_bf = e.astype(BF16)
    oc = jnp.concatenate([_dot(e_bf[0:HALF], vc0_ref[0]), _dot(e_bf[HALF:2 * HALF], vc1_ref[0])],
                         axis=0)
    o_cmp = normalised(oc, True)
    l_c = jnp.where(lo_half, pltpu.roll(oc, NSA_DH, 1), oc)
    inv_l = 1.0 / jnp.where(l_c > 0.0, l_c, 1.0)
    p_c = e * jnp.concatenate([inv_l] * (ncmp // LANES), axis=1)

    imp_ts = []
    for g in range(NSA_KV):
        psum = p_c[(g * NSA_REP) * QB:(g * NSA_REP + 1) * QB]
        for r in range(1, NSA_REP):
            psum = psum + p_c[(g * NSA_REP + r) * QB:(g * NSA_REP + r + 1) * QB]
        p_hi = psum.astype(BF16)
        p_lo = (psum - p_hi.astype(F32)).astype(BF16)
        imp = _dot(p_hi, ov_ref[...]) + _dot(p_lo, ov_ref[...])
        imp_ts.append(imp.T[0:nbs, :])

    n_grp = nbs // SUBLANES
    sub8 = lax.broadcasted_iota(jnp.int32, (SUBLANES, QB), 0)
    tpos = t0 + lax.broadcasted_iota(jnp.int32, (1, QB), 1)
    cur = tpos // SEL_LEN
    j_lane = lane.astype(F32)
    for g in range(NSA_KV):
        imp_t = imp_ts[g]
        j_sub = lax.broadcasted_iota(jnp.int32, (nbs, QB), 0)
        forced = (j_sub == 0) | (j_sub == cur) | (j_sub == cur - 1)
        score = jnp.where(j_sub <= cur, imp_t + jnp.where(forced, FORCE_BONUS, 0.0), NEG)
        parts = [score[a * SUBLANES:(a + 1) * SUBLANES, :] for a in range(n_grp)]
        ranks = [jnp.zeros((SUBLANES, QB), F32) for _ in range(n_grp)]
        for i in range(nbs):
            row = jnp.broadcast_to(score[i:i + 1, :], (SUBLANES, QB))
            for a in range(n_grp):
                if a * SUBLANES > i:
                    beats = jnp.where(row >= parts[a], 1.0, 0.0)
                elif (a + 1) * SUBLANES - 1 < i:
                    beats = jnp.where(row > parts[a], 1.0, 0.0)
                else:
                    beats = jnp.where(sub8 + a * SUBLANES > i,
                                      jnp.where(row >= parts[a], 1.0, 0.0),
                                      jnp.where(row > parts[a], 1.0, 0.0))
                ranks[a] = ranks[a] + beats
        sel_t = [jnp.where(rk < float(min(SEL_TOP, nbs)), 0.0, NEG) for rk in ranks]
        if nbs < LANES:
            sel_t.append(jnp.zeros((LANES - nbs, QB), F32))
        sel_bias = jnp.concatenate(sel_t, axis=0).T
        for r in range(NSA_REP):
            h = g * NSA_REP + r
            feat = jnp.where(lo_half, sel_bias + (_alibi_slope(h) * SEL_LEN) * j_lane, const_feat(h))
            qe_scr[h * QB:(h + 1) * QB, LANES:2 * LANES] = feat.astype(BF16)

    s_w = s_w + _tile_heads(jnp.where(ok_w, 0.0, NEG))
    p_w = jnp.exp(s_w - jnp.max(s_w, axis=-1, keepdims=True)).astype(BF16)
    o_win = normalised(jnp.concatenate([_dot(p_w[0:HALF], vw0_ref[pl.ds(ws, WIN_KEYS), :]),
                                        _dot(p_w[HALF:2 * HALF], vw1_ref[pl.ds(ws, WIN_KEYS), :])],
                                       axis=0), False)

    last_tile = t0 // KT
    mb_scr[...] = jnp.full(mb_scr.shape, NEG, F32)

    U = SEL_GROUP

    def score_group(kt0, count, causal):
        k0 = pl.multiple_of(kt0 * KT, KT)
        kx = jnp.concatenate([ks_ref[pl.ds(k0, count * KT), :],
                              posf_scr[pl.ds(k0, count * KT), :]], axis=1)
        s = _dot_nt(qe_scr[...], kx)
        tile_max = None
        for j in range(count):
            sj = s[:, j * KT:(j + 1) * KT] * LOG2_E
            if causal and j == count - 1:
                qs = lax.broadcasted_iota(jnp.int32, (QB, KT), 0)
                kl = lax.broadcasted_iota(jnp.int32, (QB, KT), 1)
                sj = sj + _tile_heads(jnp.where(k0 + j * KT + kl <= t0 + qs, 0.0, NEG))
            s_scr[kt0 + j] = sj
            for c0 in range(0, KT, LANES):
                part = sj[:, c0:c0 + LANES]
                tile_max = part if tile_max is None else jnp.maximum(tile_max, part)
        group_max = jnp.max(tile_max, axis=-1, keepdims=True)
        mb_scr[...] = jnp.maximum(mb_scr[...], jnp.broadcast_to(group_max, mb_scr.shape))

    def score_full_group(i, carry):
        score_group(i * U, U, False)
        return carry

    lax.fori_loop(0, last_tile // U, score_full_group, 0)
    for r in range(U):
        @pl.when(last_tile % U == r)
        def _(r=r):
            score_group(last_tile - r, r + 1, True)

    acc_scr[...] = jnp.zeros(acc_scr.shape, F32)

    def value_group(kt0, count):
        k0 = pl.multiple_of(kt0 * KT, KT)
        mb = mb_scr[...]
        mbw = jnp.concatenate([mb] * (KT // LANES), axis=1)
        p = jnp.concatenate([jnp.exp2(s_scr[kt0 + j] - mbw).astype(BF16) for j in range(count)],
                            axis=1)
        acc_scr[0:HALF, :] += _dot(p[0:HALF], vs0_ref[pl.ds(k0, count * KT), :])
        acc_scr[HALF:2 * HALF, :] += _dot(p[HALF:2 * HALF], vs1_ref[pl.ds(k0, count * KT), :])

    def value_full_group(i, carry):
        value_group(i * U, U)
        return carry

    n_tiles = last_tile + 1
    lax.fori_loop(0, n_tiles // U, value_full_group, 0)
    for r in range(1, U):
        @pl.when(n_tiles % U == r)
        def _(r=r):
            value_group(n_tiles - r, r)

    o_sel = normalised(acc_scr[...], False)

    gate = jax.nn.sigmoid(g_ref[...])
    g_hi = gate.astype(BF16)
    g_lo = (gate - g_hi.astype(F32)).astype(BF16)
    spread = _dot(g_hi, gx_ref[...]) + _dot(g_lo, gx_ref[...])
    for pair in range(NSA_HEADS // 2):
        y = None
        for b, o in enumerate((o_cmp, o_sel, o_win)):
            c0 = b * NSA_WIDTH + pair * LANES
            term = spread[:, c0:c0 + LANES] * o[pair]
            y = term if y is None else y + term
        y_ref[:, pair * LANES:(pair + 1) * LANES] = y.astype(y_ref.dtype)


def _nsa(pr, kc, vc0, vc1, batch, seq):
    QB = Q_BLOCK
    nq = seq // QB
    nbs = seq // SEL_LEN
    ncmp = kc.shape[1]
    cs = np.arange(ncmp) * CMP_STRIDE
    js = np.arange(nbs) * SEL_LEN
    ov = np.zeros((ncmp, LANES), np.float32)
    ov[:, :nbs] = (cs[:, None] < js[None, :] + SEL_LEN) & (cs[:, None] + CMP_LEN > js[None, :])
    gx = np.zeros((LANES, 3 * NSA_WIDTH), np.float32)
    for h in range(NSA_HEADS):
        for b in range(3):
            gx[SUBLANES + 3 * h + b, b * NSA_WIDTH + h * NSA_DH:b * NSA_WIDTH + (h + 1) * NSA_DH] = 1.0
    tok = lambda b, i: (b * nq + i, 0)
    per_b = lambda b, i: (b, 0)
    per_b3 = lambda b, i: (b, 0, 0)
    kv_spec = pl.BlockSpec((seq, LANES), per_b)
    cmp_spec = pl.BlockSpec((1, ncmp, LANES), per_b3)
    return pl.pallas_call(
        functools.partial(_nsa_kernel, nbs),
        out_shape=jax.ShapeDtypeStruct((batch * seq, NSA_WIDTH), BF16),
        grid=(batch, nq),
        in_specs=[pl.BlockSpec((QB, NSA_HEADS * LANES), tok),
                  pl.BlockSpec((QB, LANES), tok),
                  cmp_spec, cmp_spec, cmp_spec,
                  kv_spec, kv_spec, kv_spec, kv_spec, kv_spec, kv_spec,
                  pl.BlockSpec((ncmp, LANES), lambda b, i: (0, 0)),
                  pl.BlockSpec((LANES, 3 * NSA_WIDTH), lambda b, i: (0, 0))],
        out_specs=pl.BlockSpec((QB, NSA_WIDTH), tok),
        scratch_shapes=[pltpu.VMEM((NSA_ROWS, 2 * LANES), BF16),
                        pltpu.VMEM((seq, LANES), BF16),
                        pltpu.VMEM((WIN_KEYS, LANES), BF16),
                        pltpu.VMEM((ncmp, LANES), BF16),
                        pltpu.VMEM((seq // SEL_KT, NSA_ROWS, SEL_KT), F32),
                        pltpu.VMEM((NSA_ROWS, LANES), F32),
                        pltpu.VMEM((NSA_ROWS, LANES), F32)],
        compiler_params=pltpu.CompilerParams(dimension_semantics=("parallel", "arbitrary"),
                                             vmem_limit_bytes=VMEM_LIMIT),
        name="nsa",
    )(pr["nq"], pr["gcol"], kc, vc0, vc1, pr["ks"], pr["vs0"], pr["vs1"],
      pr["kw"], pr["vw0"], pr["vw1"], jnp.asarray(ov, BF16), jnp.asarray(gx, BF16))


def _memkv_kernel(mem_ref, g_ref, w_ref, kv_ref):
    u = _rms(mem_ref[0], g_ref[...]).astype(BF16)
    kv_ref[0] = _dot(u, w_ref[...]).astype(kv_ref.dtype)


def _memkv(mem, g_mem, w_mem_kv):
    batch, n_mem, d = mem.shape
    return pl.pallas_call(
        _memkv_kernel,
        out_shape=jax.ShapeDtypeStruct((batch, n_mem, 2 * XA_WIDTH), BF16),
        grid=(batch,),
        in_specs=[pl.BlockSpec((1, n_mem, d), lambda b: (b, 0, 0)),
                  _resident((1, d)), _resident((d, 2 * XA_WIDTH))],
        out_specs=pl.BlockSpec((1, n_mem, 2 * XA_WIDTH), lambda b: (b, 0, 0)),
        compiler_params=pltpu.CompilerParams(dimension_semantics=("parallel",),
                                             vmem_limit_bytes=VMEM_LIMIT),
        name="memkv",
    )(mem, g_mem[None, :], w_mem_kv.astype(BF16))


def _tail_kernel(x_ref, yml_ref, ynsa_ref, xq_ref, kv_ref, mg_ref, wb_ref, wo_ref, gf_ref,
                 w1_ref, w2_ref, gl_ref, out_ref):
    d = x_ref.shape[1]
    kv = kv_ref[0]
    heads = [slice(hh * XA_DH, (hh + 1) * XA_DH) for hh in range(XA_HEADS)]
    scores = [_dot_nt(xq_ref[:, sl], kv[:, sl]) * (XA_DH ** -0.5) for sl in heads]
    ups = [_dot(yml_ref[...], wb_ref[0]), _dot(ynsa_ref[...], wb_ref[1])]
    y_xa = []
    for hh, s in enumerate(scores):
        p = jnp.exp(s - jnp.max(s, axis=-1, keepdims=True))
        l = jnp.sum(p, axis=-1, keepdims=True)
        y_xa.append((_dot(p.astype(BF16), kv[:, XA_WIDTH + hh * XA_DH:XA_WIDTH + (hh + 1) * XA_DH])
                     / l).astype(BF16))
    ups.append(_dot(jnp.concatenate(y_xa, axis=-1), wb_ref[2]))
    merged = None
    for j in range(N_BRANCH):
        term = mg_ref[:, j * d:(j + 1) * d] * ups[j]
        merged = term if merged is None else merged + term
    h = x_ref[...] + _dot(merged.astype(BF16), wo_ref[...])
    hb = (h * gf_ref[...]).astype(BF16)
    r = lax.rsqrt(jnp.mean(h * h, axis=-1, keepdims=True) + EPS)
    acc = h
    for c0 in range(0, w1_ref.shape[1], FF_SLAB):
        a = jnp.maximum(r * _dot(hb, w1_ref[:, c0:c0 + FF_SLAB]), 0.0)
        acc = acc + _dot((a * a).astype(BF16), w2_ref[c0:c0 + FF_SLAB, :])
    out_ref[...] = _rms(acc, gl_ref[...])


def _tail(x2, y_ml, y_nsa, pr, mem_kv, w_branch, w_out, g_ffn, w_ff1, w_ff2, g_final, seq):
    n_tok, d = x2.shape
    tm = TM_TAIL
    d_ff = w_ff1.shape[1]
    n_mem = mem_kv.shape[1]
    tiles_per_b = seq // tm
    row = lambda i: (i, 0)
    return pl.pallas_call(
        _tail_kernel,
        out_shape=jax.ShapeDtypeStruct((n_tok, d), F32),
        grid=(n_tok // tm,),
        in_specs=[pl.BlockSpec((tm, d), row),
                  pl.BlockSpec((tm, ML_WIDTH), row),
                  pl.BlockSpec((tm, NSA_WIDTH), row),
                  pl.BlockSpec((tm, XA_WIDTH), row),
                  pl.BlockSpec((1, n_mem, 2 * XA_WIDTH), lambda i: (i // tiles_per_b, 0, 0)),
                  pl.BlockSpec((tm, N_BRANCH * d), row),
                  _resident((N_BRANCH, ML_WIDTH, d)), _resident((d, d)), _resident((1, d)),
                  _resident((d, d_ff)), _resident((d_ff, d)), _resident((1, d))],
        out_specs=pl.BlockSpec((tm, d), row),
        compiler_params=pltpu.CompilerParams(dimension_semantics=("parallel",),
                                             vmem_limit_bytes=VMEM_LIMIT),
        name="tail",
    )(x2, y_ml, y_nsa, pr["xq"], mem_kv, pr["mg"], w_branch.astype(BF16), w_out.astype(BF16),
      g_ffn[None, :], w_ff1.astype(BF16), w_ff2.astype(BF16), g_final[None, :])


def _layer(x, mem, g_mix, w_in, b_in, ml_conv, ml_norm_g, cmp_pe, cmp_w1, cmp_w2, g_mem, w_mem_kv,
           w_branch, w_out, g_ffn, w_ff1, w_ff2, g_final):
    batch, seq, d = x.shape
    assert seq % (SEL_LEN * SUBLANES) == 0 and seq % SEL_KT == 0 and seq >= WIN_KEYS
    assert seq // SEL_LEN <= NSA_DH, "one selection-block feature lane per block"
    assert seq // CMP_STRIDE <= 256, "compressed token index must be exact in bf16"
    assert seq % TM_PROJ == 0 and seq % TM_TAIL == 0 and TM_PROJ % ML_CHUNK == 0
    x2 = x.reshape(batch * seq, d)
    pr = _inproj(x2, g_mix, w_in, b_in, ml_conv, ml_norm_g, seq)
    y_ml = pr["y_ml"]
    kc, vc0, vc1 = _compress(pr, cmp_pe, cmp_w1, cmp_w2, batch, seq)
    y_nsa = _nsa(pr, kc, vc0, vc1, batch, seq)
    mem_kv = _memkv(mem, g_mem, w_mem_kv)
    out = _tail(x2, y_ml, y_nsa, pr, mem_kv, w_branch, w_out, g_ffn, w_ff1, w_ff2, g_final, seq)
    return out.reshape(batch, seq, d)


def kernel(x, mem, g_mix, w_in, b_in, ml_conv, ml_norm_g, cmp_pe, cmp_w1, cmp_w2, g_mem, w_mem_kv,
           w_branch, w_out, g_ffn, w_ff1, w_ff2, g_final):
    assert g_mix.shape[0] == 1, "single-layer block"
    return _layer(x, mem, g_mix[0], w_in[0], b_in[0], ml_conv[0], ml_norm_g[0], cmp_pe[0],
                  cmp_w1[0], cmp_w2[0], g_mem[0], w_mem_kv[0], w_branch[0], w_out[0], g_ffn[0],
                  w_ff1[0], w_ff2[0], g_final)
```

```python
import functools

import numpy as np
import jax
import jax.numpy as jnp
from jax import lax
from jax.experimental import pallas as pl
from jax.experimental.pallas import tpu as pltpu

F32 = jnp.float32
BF16 = jnp.bfloat16

EPS = 1e-6
NEG = -1e30
LOG2_E = 1.4426950408889634
ML_HEADS = 4
ML_DH = 128
ML_WIDTH = ML_HEADS * ML_DH
ML_CONV = 4
ML_CHUNK = 128
NSA_HEADS = 8
NSA_KV = 2
NSA_REP = NSA_HEADS // NSA_KV
NSA_DH = 64
NSA_WIDTH = NSA_HEADS * NSA_DH
NSA_KV_WIDTH = NSA_KV * NSA_DH
CMP_LEN = 32
CMP_STRIDE = 16
CMP_HIDDEN = 256
SEL_LEN = 64
SEL_TOP = 16
WINDOW = 512
Q_BLOCK = 128
FORCE_BONUS = 1e3
SEL_KT = 256
SEL_GROUP = 8
XA_HEADS = 4
XA_DH = 128
XA_WIDTH = XA_HEADS * XA_DH
N_BRANCH = 3

LANES = 128
SUBLANES = 8
VMEM_LIMIT = 56 * 1024 * 1024

TM_PROJ = 256
TM_TAIL = 512
FF_SLAB = 1024

_NT = (((1,), (1,)), ((), ()))
_TN = (((0,), (0,)), ((), ()))


def _dot(a, b):
    return jnp.dot(a, b, preferred_element_type=F32)


def _dot_nt(a, b):
    return lax.dot_general(a, b, _NT, preferred_element_type=F32)


def _resident(shape):
    nd = len(shape)
    return pl.BlockSpec(shape, lambda *_: (0,) * nd, pipeline_mode=pl.Buffered(1))


def _rms(x, g):
    return x * lax.rsqrt(jnp.mean(x * x, axis=-1, keepdims=True) + EPS) * g


_PROJ_COLS = (
    ("qk", 2 * ML_WIDTH, BF16),
    ("v", ML_WIDTH, BF16),
    ("o", ML_WIDTH, BF16),
    ("gcol", LANES, F32),
    ("kc", NSA_KV_WIDTH, F32),
    ("vc", NSA_KV_WIDTH, F32),
    ("ks", NSA_KV_WIDTH, BF16),
    ("kw", NSA_KV_WIDTH, BF16),
    ("vs0", LANES, BF16),
    ("vs1", LANES, BF16),
    ("vw0", LANES, BF16),
    ("vw1", LANES, BF16),
    ("nq", NSA_HEADS * LANES, BF16),
    ("xq", XA_WIDTH, BF16),
    ("mg", None, BF16),
)
_PROJ_MLSTM_ONLY = ("qk", "v", "o")
_PROJ_CHUNK = 512


def _proj_layout(d_model):
    cols, off = [], 0
    for name, width, dt in _PROJ_COLS:
        width = N_BRANCH * d_model if width is None else width
        cols.append((name, off, width, dt))
        off += width
    return cols, off


def _inproj_kernel(cols, tiles_per_b, x_ref, g_ref, w_ref, b_ref, wt_ref, bt_ref, conv_ref, ng_ref,
                   *refs):
    out_names = [c[0] for c in cols if c[0] not in _PROJ_MLSTM_ONLY]
    out_refs = dict(zip(out_names, refs[:len(out_names)]))
    yml_ref, xbuf, c_scr, m_scr = refs[len(out_names):]
    tm = x_ref.shape[0]
    halo = SUBLANES

    @pl.when(pl.program_id(0) % tiles_per_b == 0)
    def _():
        xbuf[0:halo, :] = jnp.zeros((halo, xbuf.shape[1]), F32)
        c_scr[...] = jnp.zeros(c_scr.shape, F32)
        m_scr[...] = jnp.zeros(m_scr.shape, F32)

    x = x_ref[...]
    xg = x * g_ref[...]
    xb = xg.astype(BF16)
    r = lax.rsqrt(jnp.mean(x * x, axis=-1, keepdims=True) + EPS)
    grow = _dot_nt(wt_ref[...], (xg * r).astype(BF16)) + bt_ref[...]
    kept = {name: [] for name in _PROJ_MLSTM_ONLY + ("gcol",)}
    mlstm_ready = max(off + width for name, off, width, _ in cols if name in kept)
    ncols = w_ref.shape[1]
    for c0 in range(0, ncols, _PROJ_CHUNK):
        cw = min(_PROJ_CHUNK, ncols - c0)
        chunk = r * _dot(xb, w_ref[:, c0:c0 + cw]) + b_ref[:, c0:c0 + cw]
        for name, off, width, dt in cols:
            lo, hi = max(off, c0), min(off + width, c0 + cw)
            if lo >= hi:
                continue
            acc = chunk[:, lo - c0:hi - c0]
            cs = slice(lo - off, hi - off)
            if name == "qk":
                xbuf[halo:halo + tm, cs] = acc
                y = acc * conv_ref[0:1, cs]
                for j in range(1, ML_CONV):
                    y = y + xbuf[halo - j:halo - j + tm, cs] * conv_ref[j:j + 1, cs]
                xbuf[0:halo, cs] = xbuf[tm:tm + halo, cs]
                acc = y * jax.nn.sigmoid(y)
                if lo - off >= ML_WIDTH:
                    acc = acc * (ML_DH ** -0.5)
            elif name in ("o", "mg"):
                acc = jax.nn.sigmoid(acc)
            acc = acc.astype(dt)
            if name in kept:
                kept[name].append(acc)
            if name in out_refs:
                out_refs[name][:, cs] = acc
        if c0 < mlstm_ready <= c0 + cw:
            qk, v, og = (jnp.concatenate(kept[n], axis=1) for n in _PROJ_MLSTM_ONLY)
            gcol = kept["gcol"][0]
            c_state = [c_scr[hh] for hh in range(ML_HEADS)]
            m_state = [m_scr[hh:hh + 1, 0:1] for hh in range(ML_HEADS)]
            for r0 in range(0, tm, ML_CHUNK):
                rows = slice(r0, r0 + ML_CHUNK)
                y_ml, c_state, m_state = _mlstm_chunk(qk[rows], v[rows], og[rows], gcol[rows],
                                                      grow[:, rows], ng_ref[...], c_state, m_state)
                yml_ref[rows, :] = y_ml
            for hh in range(ML_HEADS):
                c_scr[hh] = c_state[hh]
                m_scr[hh:hh + 1, :] = jnp.broadcast_to(m_state[hh], (1, m_scr.shape[1]))


def _split_w_in(w_in, b_in):
    widths = (ML_WIDTH, ML_WIDTH, ML_WIDTH, ML_WIDTH, ML_HEADS, ML_HEADS,
              NSA_WIDTH, NSA_KV_WIDTH, NSA_KV_WIDTH, NSA_KV_WIDTH, NSA_KV_WIDTH, NSA_KV_WIDTH,
              NSA_KV_WIDTH, 3 * NSA_HEADS, XA_WIDTH, w_in.shape[1])
    names = ("ml_q", "ml_k", "ml_v", "ml_o", "ml_i", "ml_f", "ns_q", "ns_kc", "ns_vc", "ns_ks",
             "ns_vs", "ns_kw", "ns_vw", "ns_g", "xa_q", "mg")
    out, off = {}, 0
    for n, wd in zip(names, widths):
        end = w_in.shape[1] if n == "mg" else off + wd
        out[n] = (w_in[:, off:end], b_in[off:end])
        off = end
    return out


def _inproj(x2, g_mix, w_in, b_in, ml_conv, ml_norm_g, seq):
    n_tok, d = x2.shape
    cols, ncols = _proj_layout(d)
    p = _split_w_in(w_in, b_in)

    def pad_cols(w, b, width):
        return (jnp.pad(w, ((0, 0), (0, width - w.shape[1]))), jnp.pad(b, (0, width - b.shape[0])))

    wq, bq = p["ns_q"]
    wq = wq.reshape(d, NSA_HEADS, 1, NSA_DH)
    bq = bq.reshape(NSA_HEADS, 1, NSA_DH)
    in_group = jnp.asarray(np.arange(NSA_HEADS)[:, None] // NSA_REP == np.arange(NSA_KV)[None, :],
                           w_in.dtype)[:, :, None]
    wq_slots = wq * in_group
    bq_slots = bq * in_group
    gate_w = jnp.concatenate([p["ml_i"][0], p["ml_f"][0]], axis=1)
    gate_b = jnp.concatenate([p["ml_i"][1], p["ml_f"][1]])
    small_w = jnp.concatenate([gate_w, p["ns_g"][0]], axis=1)
    small_b = jnp.concatenate([gate_b, p["ns_g"][1]])

    def with_ones(wb, g):
        w, b = wb
        w = jnp.pad(w[:, g * NSA_DH:(g + 1) * NSA_DH], ((0, 0), (0, LANES - NSA_DH)))
        b = jnp.concatenate([b[g * NSA_DH:(g + 1) * NSA_DH], jnp.ones((LANES - NSA_DH,), b.dtype)])
        return w, b

    pieces = {
        "qk": (jnp.concatenate([p["ml_q"][0], p["ml_k"][0]], axis=1),
               jnp.concatenate([p["ml_q"][1], p["ml_k"][1]])),
        "o": p["ml_o"], "mg": p["mg"],
        "gcol": pad_cols(small_w, small_b, LANES),
        "kc": p["ns_kc"], "vc": p["ns_vc"], "v": p["ml_v"],
        "nq": (wq_slots.reshape(d, NSA_HEADS * LANES), bq_slots.reshape(NSA_HEADS * LANES)),
        "xq": p["xa_q"], "ks": p["ns_ks"], "kw": p["ns_kw"],
        "vs0": with_ones(p["ns_vs"], 0), "vs1": with_ones(p["ns_vs"], 1),
        "vw0": with_ones(p["ns_vw"], 0), "vw1": with_ones(p["ns_vw"], 1),
    }
    w_cols = jnp.concatenate([pieces[name][0] for name, *_ in cols], axis=1).astype(BF16)
    b_cols = jnp.concatenate([pieces[name][1] for name, *_ in cols])[None, :]
    w_rows = gate_w.T.astype(BF16)
    b_rows = gate_b[:, None]

    tm = TM_PROJ
    tiles_per_b = seq // tm
    outs_hbm = [c for c in cols if c[0] not in _PROJ_MLSTM_ONLY] + [("y_ml", None, ML_WIDTH, BF16)]
    outs = pl.pallas_call(
        functools.partial(_inproj_kernel, cols, tiles_per_b),
        out_shape=[jax.ShapeDtypeStruct((n_tok, width), dt) for _, _, width, dt in outs_hbm],
        grid=(n_tok // tm,),
        in_specs=[pl.BlockSpec((tm, d), lambda i: (i, 0)),
                  _resident((1, d)), _resident((d, ncols)), _resident((1, ncols)),
                  _resident((SUBLANES, d)), _resident((SUBLANES, 1)),
                  _resident((ML_CONV, 2 * ML_WIDTH)), _resident((1, ML_WIDTH))],
        out_specs=[pl.BlockSpec((tm, width), lambda i: (i, 0)) for _, _, width, _ in outs_hbm],
        scratch_shapes=[pltpu.VMEM((tm + SUBLANES, 2 * ML_WIDTH), F32),
                        pltpu.VMEM((ML_HEADS, ML_DH, 2 * ML_DH), F32),
                        pltpu.VMEM((SUBLANES, LANES), F32)],
        compiler_params=pltpu.CompilerParams(dimension_semantics=("arbitrary",),
                                             vmem_limit_bytes=VMEM_LIMIT),
        name="inproj",
    )(x2, g_mix[None, :], w_cols, b_cols, w_rows, b_rows, ml_conv, ml_norm_g[None, :])
    return dict(zip([c[0] for c in outs_hbm], outs))


def _mlstm_chunk(qk, v, og, gcol, grow, ng, c_prev, m_prev):
    L = ML_CHUNK
    r_i = lax.broadcasted_iota(jnp.int32, (L, L), 0)
    c_i = lax.broadcasted_iota(jnp.int32, (L, L), 1)
    causal = c_i <= r_i
    tril = causal.astype(F32)
    triu = (r_i <= c_i).astype(F32)
    b_col = jnp.dot(tril, jax.nn.log_sigmoid(gcol), precision=lax.Precision.HIGHEST,
                    preferred_element_type=F32)
    b_row = jnp.dot(jax.nn.log_sigmoid(grow), triu, precision=lax.Precision.HIGHEST,
                    preferred_element_type=F32)
    ones_col = jnp.where(lax.broadcasted_iota(jnp.int32, (L, ML_DH), 1) == 0, 1.0, 0.0).astype(BF16)

    def operands(hh):
        sl = slice(hh * ML_DH, (hh + 1) * ML_DH)
        k = qk[:, ML_WIDTH + hh * ML_DH:ML_WIDTH + (hh + 1) * ML_DH]
        return qk[:, sl], k, jnp.concatenate([v[:, sl], ones_col], axis=1), c_prev[hh]

    ops = [operands(hh) for hh in range(ML_HEADS)]
    qk_t = [_dot_nt(q, k) for q, k, _, _ in ops]
    inter = [_dot(q, c.astype(BF16)) for q, _, _, c in ops]

    stab, ke_all, s_all = [], [], []
    for hh in range(ML_HEADS):
        li_c = gcol[:, hh:hh + 1]
        b_c = b_col[:, ML_HEADS + hh:ML_HEADS + hh + 1]
        li_r = grow[hh:hh + 1, :]
        b_r = b_row[ML_HEADS + hh:ML_HEADS + hh + 1, :]
        g = b_c[L - 1:L, :]
        rel = jnp.where(causal, li_r - b_r, -jnp.inf)
        mm = jnp.maximum(m_prev[hh], jnp.max(rel, axis=-1, keepdims=True))
        m_t = b_c + mm
        s_all.append((qk_t[hh] * jnp.exp(rel - mm)).astype(BF16))
        w_end = g - b_c + li_c
        m_loc = jnp.max(w_end, axis=0, keepdims=True)
        ke_all.append((ops[hh][1] * jnp.exp(w_end - m_loc)).astype(BF16))
        m_new = jnp.maximum(g + m_prev[hh], m_loc)
        stab.append((m_t, jnp.exp(m_prev[hh] - mm), jnp.exp(g + m_prev[hh] - m_new),
                     jnp.exp(m_loc - m_new), m_new))

    intra = [_dot(s, o[2]) for s, o in zip(s_all, ops)]
    a_c = [lax.dot_general(ke, o[2], _TN, preferred_element_type=F32) for ke, o in zip(ke_all, ops)]

    ys, new_c, new_m = [], [], []
    for hh in range(ML_HEADS):
        sl = slice(hh * ML_DH, (hh + 1) * ML_DH)
        m_t, sc, a, bb, m_new = stab[hh]
        nd = intra[hh] + sc * inter[hh]
        den = nd[:, ML_DH:ML_DH + 1]
        h = nd[:, 0:ML_DH] / jnp.maximum(jnp.abs(den), jnp.exp(-m_t))
        hn = h * lax.rsqrt(jnp.mean(h * h, axis=-1, keepdims=True) + EPS)
        ys.append((og[:, sl] * hn * ng[:, sl]).astype(BF16))
        new_c.append(a * ops[hh][3] + bb * a_c[hh])
        new_m.append(m_new)
    return jnp.concatenate(ys, axis=1), new_c, new_m


def _gelu_tanh(x):
    return 0.5 * x * (1.0 + jnp.tanh(np.sqrt(2.0 / np.pi) * (x + 0.044715 * (x * x * x))))


def _compress_kernel(k_ref, v_ref, pe_ref, w1_ref, w2k_ref, w2v_ref, kc_ref, vc0_ref, vc1_ref):
    nrow = kc_ref.shape[1]
    S = CMP_STRIDE

    def hidden(x_ref, which):
        lo = hi = None
        for j in range(0, S, 2):
            x = [x_ref[pl.ds(j + d, nrow, stride=S), :] for d in range(2)]
            for half in range(CMP_LEN // S):
                l0 = half * S + j
                xp = jnp.concatenate([(x[d] + pe_ref[which, l0 + d:l0 + d + 1, :]).astype(BF16)
                                      for d in range(2)], axis=1)
                t = _dot(xp, w1_ref[which, l0 // 2])
                if half == 0:
                    lo = t if lo is None else lo + t
                else:
                    hi = t if hi is None else hi + t
        return _gelu_tanh(lo + pltpu.roll(hi, nrow - 1, 0)).astype(BF16)

    kc_ref[0] = _dot(hidden(k_ref, 0), w2k_ref[...]).astype(kc_ref.dtype)
    hv = hidden(v_ref, 1)
    ones_half = lax.broadcasted_iota(jnp.int32, (nrow, LANES), 1) >= NSA_DH
    for g, o_ref in enumerate((vc0_ref, vc1_ref)):
        o_ref[0] = jnp.where(ones_half, 1.0, _dot(hv, w2v_ref[g])).astype(o_ref.dtype)


def _compress(pr, cmp_pe, cmp_w1, cmp_w2, batch, seq):
    nrow = seq // CMP_STRIDE
    hid = NSA_KV * CMP_HIDDEN
    w1 = cmp_w1.astype(BF16).reshape(2, CMP_LEN, NSA_DH, CMP_HIDDEN)
    z1 = jnp.zeros_like(w1)
    w1e = jnp.stack([jnp.concatenate([w1, z1], axis=-1), jnp.concatenate([z1, w1], axis=-1)], axis=2)
    w1e = w1e.reshape(2, CMP_LEN // 2, 2 * NSA_KV_WIDTH, hid)
    w2 = cmp_w2.astype(BF16)
    z2 = jnp.zeros_like(w2[0])
    w2k = jnp.concatenate([jnp.concatenate([w2[0], z2], axis=1),
                           jnp.concatenate([z2, w2[0]], axis=1)], axis=0)
    w2v = jnp.stack([jnp.pad(w2[1], ((g * CMP_HIDDEN, (NSA_KV - 1 - g) * CMP_HIDDEN),
                                     (0, LANES - NSA_DH))) for g in range(NSA_KV)])
    pe = jnp.concatenate([cmp_pe] * NSA_KV, axis=-1)
    tok_blk = pl.BlockSpec((seq, NSA_KV_WIDTH), lambda b: (b, 0))
    out_blk = pl.BlockSpec((1, nrow, LANES), lambda b: (b, 0, 0))
    return pl.pallas_call(
        _compress_kernel,
        out_shape=(jax.ShapeDtypeStruct((batch, nrow, LANES), BF16),) * 3,
        grid=(batch,),
        in_specs=[tok_blk, tok_blk,
                  _resident((2, CMP_LEN, NSA_KV_WIDTH)),
                  _resident((2, CMP_LEN // 2, 2 * NSA_KV_WIDTH, hid)),
                  _resident((hid, NSA_KV_WIDTH)),
                  _resident((NSA_KV, hid, LANES))],
        out_specs=(out_blk, out_blk, out_blk),
        compiler_params=pltpu.CompilerParams(dimension_semantics=("parallel",),
                                             vmem_limit_bytes=VMEM_LIMIT),
        name="compress",
    )(pr["kc"], pr["vc"], pe, w1e, w2k, w2v)


def _alibi_slope(h):
    return float(2.0 ** (-8.0 * (h + 1) / NSA_HEADS))


NSA_ROWS = NSA_HEADS * Q_BLOCK
WIN_KEYS = WINDOW + Q_BLOCK
F_SEL_BLOCK = 0
F_SEL_OFF = SEL_LEN
F_WIN_OFF = SEL_LEN + 1
F_WIN_BLK = SEL_LEN + 2
F_CMP = SEL_LEN + 3


def _tile_heads(x):
    return jnp.concatenate([x] * NSA_HEADS, axis=0)


def _nsa_kernel(nbs, q_ref, g_ref, kc_ref, vc0_ref, vc1_ref, ks_ref, vs0_ref, vs1_ref,
                kw_ref, vw0_ref, vw1_ref, ov_ref, gx_ref, y_ref,
                qe_scr, posf_scr, wposf_scr, cposf_scr, s_scr, mb_scr, acc_scr):
    QB = Q_BLOCK
    KT = SEL_KT
    HALF = NSA_REP * QB
    t0 = pl.program_id(1) * QB
    ncmp = kc_ref.shape[1]
    lane = lax.broadcasted_iota(jnp.int32, (1, LANES), 1)
    lo_half = lane < NSA_DH

    @pl.when(pl.program_id(1) == 0)
    def _():
        seq = posf_scr.shape[0]
        k = lax.broadcasted_iota(jnp.int32, (seq, LANES), 0)
        c = lax.broadcasted_iota(jnp.int32, (seq, LANES), 1)
        posf = jnp.where(c < SEL_LEN, jnp.where(k // SEL_LEN == c, 1.0, 0.0),
                         jnp.where(c == F_SEL_OFF, (k % SEL_LEN).astype(F32), 0.0))
        posf_scr[...] = posf.astype(BF16)
        k = lax.broadcasted_iota(jnp.int32, (WIN_KEYS, LANES), 0)
        c = lax.broadcasted_iota(jnp.int32, (WIN_KEYS, LANES), 1)
        wposf = jnp.where(c == F_WIN_OFF, (k % SEL_LEN).astype(F32),
                          jnp.where(c == F_WIN_BLK, (k // SEL_LEN).astype(F32), 0.0))
        wposf_scr[...] = wposf.astype(BF16)
        k = lax.broadcasted_iota(jnp.int32, (ncmp, LANES), 0)
        c = lax.broadcasted_iota(jnp.int32, (ncmp, LANES), 1)
        cposf_scr[...] = jnp.where(c == F_CMP, k.astype(F32), 0.0).astype(BF16)

    def const_feat(h):
        s = _alibi_slope(h)
        return jnp.where((lane == F_SEL_OFF) | (lane == F_WIN_OFF), s,
                         jnp.where(lane == F_WIN_BLK, SEL_LEN * s,
                                   jnp.where(lane == F_CMP, CMP_STRIDE * s, 0.0)))

    for h in range(NSA_HEADS):
        rows = slice(h * QB, (h + 1) * QB)
        qe_scr[rows, 0:LANES] = q_ref[:, h * LANES:(h + 1) * LANES] * 0.125
        qe_scr[rows, LANES:2 * LANES] = jnp.broadcast_to(const_feat(h), (QB, LANES)).astype(BF16)

    def normalised(o, guard):
        outs = []
        for pair in range(NSA_HEADS // 2):
            e = o[(2 * pair) * QB:(2 * pair + 1) * QB]
            d = o[(2 * pair + 1) * QB:(2 * pair + 2) * QB]
            num = jnp.where(lo_half, e, pltpu.roll(d, NSA_DH, 1))
            den = jnp.where(lo_half, pltpu.roll(e, NSA_DH, 1), d)
            if guard:
                den = jnp.where(den > 0.0, den, 1.0)
            outs.append(num / den)
        return outs

    ws = pl.multiple_of(jnp.maximum(t0 - WINDOW, 0), QB)
    qs = lax.broadcasted_iota(jnp.int32, (QB, WIN_KEYS), 0)
    kl = lax.broadcasted_iota(jnp.int32, (QB, WIN_KEYS), 1)
    dist_w = (t0 - ws) + qs - kl
    ok_w = (dist_w >= 0) & (dist_w < WINDOW)
    s_w = _dot_nt(qe_scr[...], jnp.concatenate([kw_ref[pl.ds(ws, WIN_KEYS), :], wposf_scr[...]], axis=1))

    q_sub = lax.broadcasted_iota(jnp.int32, (QB, ncmp), 0)
    n_lane = lax.broadcasted_iota(jnp.int32, (QB, ncmp), 1)
    ok_c = n_lane * CMP_STRIDE + (CMP_LEN - 1) <= t0 + q_sub
    s = _dot_nt(qe_scr[...], jnp.concatenate([kc_ref[0], cposf_scr[...]], axis=1))
    s = s + _tile_heads(jnp.where(ok_c, 0.0, NEG))
    e = jnp.exp(s - jnp.max(s, axis=-1, keepdims=True)) * _tile_heads(jnp.where(ok_c, 1.0, 0.0))
    e_bf = e.astype(BF16)
    oc = jnp.concatenate([_dot(e_bf[0:HALF], vc0_ref[0]), _dot(e_bf[HALF:2 * HALF], vc1_ref[0])],
                         axis=0)
    o_cmp = normalised(oc, True)
    l_c = jnp.where(lo_half, pltpu.roll(oc, NSA_DH, 1), oc)
    inv_l = 1.0 / jnp.where(l_c > 0.0, l_c, 1.0)
    p_c = e * jnp.concatenate([inv_l] * (ncmp // LANES), axis=1)

    imp_ts = []
    for g in range(NSA_KV):
        psum = p_c[(g * NSA_REP) * QB:(g * NSA_REP + 1) * QB]
        for r in range(1, NSA_REP):
            psum = psum + p_c[(g * NSA_REP + r) * QB:(g * NSA_REP + r + 1) * QB]
        p_hi = psum.astype(BF16)
        p_lo = (psum - p_hi.astype(F32)).astype(BF16)
        imp = _dot(p_hi, ov_ref[...]) + _dot(p_lo, ov_ref[...])
        imp_ts.append(imp.T[0:nbs, :])

    n_grp = nbs // SUBLANES
    sub8 = lax.broadcasted_iota(jnp.int32, (SUBLANES, QB), 0)
    tpos = t0 + lax.broadcasted_iota(jnp.int32, (1, QB), 1)
    cur = tpos // SEL_LEN
    j_lane = lane.astype(F32)
    for g in range(NSA_KV):
        imp_t = imp_ts[g]
        j_sub = lax.broadcasted_iota(jnp.int32, (nbs, QB), 0)
        forced = (j_sub == 0) | (j_sub == cur) | (j_sub == cur - 1)
        score = jnp.where(j_sub <= cur, imp_t + jnp.where(forced, FORCE_BONUS, 0.0), NEG)
        parts = [score[a * SUBLANES:(a + 1) * SUBLANES, :] for a in range(n_grp)]
        ranks = [jnp.zeros((SUBLANES, QB), F32) for _ in range(n_grp)]
        for i in range(nbs):
            row = jnp.broadcast_to(score[i:i + 1, :], (SUBLANES, QB))
            for a in range(n_grp):
                if a * SUBLANES > i:
                    beats = jnp.where(row >= parts[a], 1.0, 0.0)
                elif (a + 1) * SUBLANES - 1 < i:
                    beats = jnp.where(row > parts[a], 1.0, 0.0)
                else:
                    beats = jnp.where(sub8 + a * SUBLANES > i,
                                      jnp.where(row >= parts[a], 1.0, 0.0),
                                      jnp.where(row > parts[a], 1.0, 0.0))
                ranks[a] = ranks[a] + beats
        sel_t = [jnp.where(rk < float(min(SEL_TOP, nbs)), 0.0, NEG) for rk in ranks]
        if nbs < LANES:
            sel_t.append(jnp.zeros((LANES - nbs, QB), F32))
        sel_bias = jnp.concatenate(sel_t, axis=0).T
        for r in range(NSA_REP):
            h = g * NSA_REP + r
            feat = jnp.where(lo_half, sel_bias + (_alibi_slope(h) * SEL_LEN) * j_lane, const_feat(h))
            qe_scr[h * QB:(h + 1) * QB, LANES:2 * LANES] = feat.astype(BF16)

    s_w = s_w + _tile_heads(jnp.where(ok_w, 0.0, NEG))
    p_w = jnp.exp(s_w - jnp.max(s_w, axis=-1, keepdims=True)).astype(BF16)
    o_win = normalised(jnp.concatenate([_dot(p_w[0:HALF], vw0_ref[pl.ds(ws, WIN_KEYS), :]),
                                        _dot(p_w[HALF:2 * HALF], vw1_ref[pl.ds(ws, WIN_KEYS), :])],
                                       axis=0), False)

    last_tile = t0 // KT
    mb_scr[...] = jnp.full(mb_scr.shape, NEG, F32)

    U = SEL_GROUP

    def score_group(kt0, count, causal):
        k0 = pl.multiple_of(kt0 * KT, KT)
        kx = jnp.concatenate([ks_ref[pl.ds(k0, count * KT), :],
                              posf_scr[pl.ds(k0, count * KT), :]], axis=1)
        s = _dot_nt(qe_scr[...], kx)
        tile_max = None
        for j in range(count):
            sj = s[:, j * KT:(j + 1) * KT] * LOG2_E
            if causal and j == count - 1:
                qs = lax.broadcasted_iota(jnp.int32, (QB, KT), 0)
                kl = lax.broadcasted_iota(jnp.int32, (QB, KT), 1)
                sj = sj + _tile_heads(jnp.where(k0 + j * KT + kl <= t0 + qs, 0.0, NEG))
            s_scr[kt0 + j] = sj
            for c0 in range(0, KT, LANES):
                part = sj[:, c0:c0 + LANES]
                tile_max = part if tile_max is None else jnp.maximum(tile_max, part)
        group_max = jnp.max(tile_max, axis=-1, keepdims=True)
        mb_scr[...] = jnp.maximum(mb_scr[...], jnp.broadcast_to(group_max, mb_scr.shape))

    def score_full_group(i, carry):
        score_group(i * U, U, False)
        return carry

    lax.fori_loop(0, last_tile // U, score_full_group, 0)
    for r in range(U):
        @pl.when(last_tile % U == r)
        def _(r=r):
            score_group(last_tile - r, r + 1, True)

    acc_scr[...] = jnp.zeros(acc_scr.shape, F32)

    def value_group(kt0, count):
        k0 = pl.multiple_of(kt0 * KT, KT)
        mb = mb_scr[...]
        mbw = jnp.concatenate([mb] * (KT // LANES), axis=1)
        p = jnp.concatenate([jnp.exp2(s_scr[kt0 + j] - mbw).astype(BF16) for j in range(count)],
                            axis=1)
        acc_scr[0:HALF, :] += _dot(p[0:HALF], vs0_ref[pl.ds(k0, count * KT), :])
        acc_scr[HALF:2 * HALF, :] += _dot(p[HALF:2 * HALF], vs1_ref[pl.ds(k0, count * KT), :])

    def value_full_group(i, carry):
        value_group(i * U, U)
        return carry

    n_tiles = last_tile + 1
    lax.fori_loop(0, n_tiles // U, value_full_group, 0)
    for r in range(1, U):
        @pl.when(n_tiles % U == r)
        def _(r=r):
            value_group(n_tiles - r, r)

    o_sel = normalised(acc_scr[...], False)

    gate = jax.nn.sigmoid(g_ref[...])
    g_hi = gate.astype(BF16)
    g_lo = (gate - g_hi.astype(F32)).astype(BF16)
    spread = _dot(g_hi, gx_ref[...]) + _dot(g_lo, gx_ref[...])
    for pair in range(NSA_HEADS // 2):
        y = None
        for b, o in enumerate((o_cmp, o_sel, o_win)):
            c0 = b * NSA_WIDTH + pair * LANES
            term = spread[:, c0:c0 + LANES] * o[pair]
            y = term if y is None else y + term
        y_ref[:, pair * LANES:(pair + 1) * LANES] = y.astype(y_ref.dtype)


def _nsa(pr, kc, vc0, vc1, batch, seq):
    QB = Q_BLOCK
    nq = seq // QB
    nbs = seq // SEL_LEN
    ncmp = kc.shape[1]
    cs = np.arange(ncmp) * CMP_STRIDE
    js = np.arange(nbs) * SEL_LEN
    ov = np.zeros((ncmp, LANES), np.float32)
    ov[:, :nbs] = (cs[:, None] < js[None, :] + SEL_LEN) & (cs[:, None] + CMP_LEN > js[None, :])
    gx = np.zeros((LANES, 3 * NSA_WIDTH), np.float32)
    for h in range(NSA_HEADS):
        for b in range(3):
            gx[SUBLANES + 3 * h + b, b * NSA_WIDTH + h * NSA_DH:b * NSA_WIDTH + (h + 1) * NSA_DH] = 1.0
    tok = lambda b, i: (b * nq + i, 0)
    per_b = lambda b, i: (b, 0)
    per_b3 = lambda b, i: (b, 0, 0)
    kv_spec = pl.BlockSpec((seq, LANES), per_b)
    cmp_spec = pl.BlockSpec((1, ncmp, LANES), per_b3)
    return pl.pallas_call(
        functools.partial(_nsa_kernel, nbs),
        out_shape=jax.ShapeDtypeStruct((batch * seq, NSA_WIDTH), BF16),
        grid=(batch, nq),
        in_specs=[pl.BlockSpec((QB, NSA_HEADS * LANES), tok),
                  pl.BlockSpec((QB, LANES), tok),
                  cmp_spec, cmp_spec, cmp_spec,
                  kv_spec, kv_spec, kv_spec, kv_spec, kv_spec, kv_spec,
                  pl.BlockSpec((ncmp, LANES), lambda b, i: (0, 0)),
                  pl.BlockSpec((LANES, 3 * NSA_WIDTH), lambda b, i: (0, 0))],
        out_specs=pl.BlockSpec((QB, NSA_WIDTH), tok),
        scratch_shapes=[pltpu.VMEM((NSA_ROWS, 2 * LANES), BF16),
                        pltpu.VMEM((seq, LANES), BF16),
                        pltpu.VMEM((WIN_KEYS, LANES), BF16),
                        pltpu.VMEM((ncmp, LANES), BF16),
                        pltpu.VMEM((seq // SEL_KT, NSA_ROWS, SEL_KT), F32),
                        pltpu.VMEM((NSA_ROWS, LANES), F32),
                        pltpu.VMEM((NSA_ROWS, LANES), F32)],
        compiler_params=pltpu.CompilerParams(dimension_semantics=("parallel", "arbitrary"),
                                             vmem_limit_bytes=VMEM_LIMIT),
        name="nsa",
    )(pr["nq"], pr["gcol"], kc, vc0, vc1, pr["ks"], pr["vs0"], pr["vs1"],
      pr["kw"], pr["vw0"], pr["vw1"], jnp.asarray(ov, BF16), jnp.asarray(gx, BF16))


def _memkv_kernel(mem_ref, g_ref, w_ref, kv_ref):
    u = _rms(mem_ref[0], g_ref[...]).astype(BF16)
    kv_ref[0] = _dot(u, w_ref[...]).astype(kv_ref.dtype)


def _memkv(mem, g_mem, w_mem_kv):
    batch, n_mem, d = mem.shape
    return pl.pallas_call(
        _memkv_kernel,
        out_shape=jax.ShapeDtypeStruct((batch, n_mem, 2 * XA_WIDTH), BF16),
        grid=(batch,),
        in_specs=[pl.BlockSpec((1, n_mem, d), lambda b: (b, 0, 0)),
                  _resident((1, d)), _resident((d, 2 * XA_WIDTH))],
        out_specs=pl.BlockSpec((1, n_mem, 2 * XA_WIDTH), lambda b: (b, 0, 0)),
        compiler_params=pltpu.CompilerParams(dimension_semantics=("parallel",),
                                             vmem_limit_bytes=VMEM_LIMIT),
        name="memkv",
    )(mem, g_mem[None, :], w_mem_kv.astype(BF16))


def _tail_kernel(x_ref, yml_ref, ynsa_ref, xq_ref, kv_ref, mg_ref, wb_ref, wo_ref, gf_ref,
                 w1_ref, w2_ref, gl_ref, out_ref):
    d = x_ref.shape[1]
    kv = kv_ref[0]
    heads = [slice(hh * XA_DH, (hh + 1) * XA_DH) for hh in range(XA_HEADS)]
    scores = [_dot_nt(xq_ref[:, sl], kv[:, sl]) * (XA_DH ** -0.5) for sl in heads]
    ups = [_dot(yml_ref[...], wb_ref[0]), _dot(ynsa_ref[...], wb_ref[1])]
    y_xa = []
    for hh, s in enumerate(scores):
        p = jnp.exp(s - jnp.max(s, axis=-1, keepdims=True))
        l = jnp.sum(p, axis=-1, keepdims=True)
        y_xa.append((_dot(p.astype(BF16), kv[:, XA_WIDTH + hh * XA_DH:XA_WIDTH + (hh + 1) * XA_DH])
                     / l).astype(BF16))
    ups.append(_dot(jnp.concatenate(y_xa, axis=-1), wb_ref[2]))
    merged = None
    for j in range(N_BRANCH):
        term = mg_ref[:, j * d:(j + 1) * d] * ups[j]
        merged = term if merged is None else merged + term
    h = x_ref[...] + _dot(merged.astype(BF16), wo_ref[...])
    hb = (h * gf_ref[...]).astype(BF16)
    r = lax.rsqrt(jnp.mean(h * h, axis=-1, keepdims=True) + EPS)
    acc = h
    for c0 in range(0, w1_ref.shape[1], FF_SLAB):
        a = jnp.maximum(r * _dot(hb, w1_ref[:, c0:c0 + FF_SLAB]), 0.0)
        acc = acc + _dot((a * a).astype(BF16), w2_ref[c0:c0 + FF_SLAB, :])
    out_ref[...] = _rms(acc, gl_ref[...])


def _tail(x2, y_ml, y_nsa, pr, mem_kv, w_branch, w_out, g_ffn, w_ff1, w_ff2, g_final, seq):
    n_tok, d = x2.shape
    tm = TM_TAIL
    d_ff = w_ff1.shape[1]
    n_mem = mem_kv.shape[1]
    tiles_per_b = seq // tm
    row = lambda i: (i, 0)
    return pl.pallas_call(
        _tail_kernel,
        out_shape=jax.ShapeDtypeStruct((n_tok, d), F32),
        grid=(n_tok // tm,),
        in_specs=[pl.BlockSpec((tm, d), row),
                  pl.BlockSpec((tm, ML_WIDTH), row),
                  pl.BlockSpec((tm, NSA_WIDTH), row),
                  pl.BlockSpec((tm, XA_WIDTH), row),
                  pl.BlockSpec((1, n_mem, 2 * XA_WIDTH), lambda i: (i // tiles_per_b, 0, 0)),
                  pl.BlockSpec((tm, N_BRANCH * d), row),
                  _resident((N_BRANCH, ML_WIDTH, d)), _resident((d, d)), _resident((1, d)),
                  _resident((d, d_ff)), _resident((d_ff, d)), _resident((1, d))],
        out_specs=pl.BlockSpec((tm, d), row),
        compiler_params=pltpu.CompilerParams(dimension_semantics=("parallel",),
                                             vmem_limit_bytes=VMEM_LIMIT),
        name="tail",
    )(x2, y_ml, y_nsa, pr["xq"], mem_kv, pr["mg"], w_branch.astype(BF16), w_out.astype(BF16),
      g_ffn[None, :], w_ff1.astype(BF16), w_ff2.astype(BF16), g_final[None, :])


def _layer(x, mem, g_mix, w_in, b_in, ml_conv, ml_norm_g, cmp_pe, cmp_w1, cmp_w2, g_mem, w_mem_kv,
           w_branch, w_out, g_ffn, w_ff1, w_ff2, g_final):
    batch, seq, d = x.shape
    assert seq % (SEL_LEN * SUBLANES) == 0 and seq % SEL_KT == 0 and seq >= WIN_KEYS
    assert seq // SEL_LEN <= NSA_DH, "one selection-block feature lane per block"
    assert seq // CMP_STRIDE <= 256, "compressed token index must be exact in bf16"
    assert seq % TM_PROJ == 0 and seq % TM_TAIL == 0 and TM_PROJ % ML_CHUNK == 0
    x2 = x.reshape(batch * seq, d)
    pr = _inproj(x2, g_mix, w_in, b_in, ml_conv, ml_norm_g, seq)
    y_ml = pr["y_ml"]
    kc, vc0, vc1 = _compress(pr, cmp_pe, cmp_w1, cmp_w2, batch, seq)
    y_nsa = _nsa(pr, kc, vc0, vc1, batch, seq)
    mem_kv = _memkv(mem, g_mem, w_mem_kv)
    out = _tail(x2, y_ml, y_nsa, pr, mem_kv, w_branch, w_out, g_ffn, w_ff1, w_ff2, g_final, seq)
    return out.reshape(batch, seq, d)


def kernel(x, mem, g_mix, w_in, b_in, ml_conv, ml_norm_g, cmp_pe, cmp_w1, cmp_w2, g_mem, w_mem_kv,
           w_branch, w_out, g_ffn, w_ff1, w_ff2, g_final):
    assert g_mix.shape[0] == 1, "single-layer block"
    return _layer(x, mem, g_mix[0], w_in[0], b_in[0], ml_conv[0], ml_norm_g[0], cmp_pe[0],
                  cmp_w1[0], cmp_w2[0], g_mem[0], w_mem_kv[0], w_branch[0], w_out[0], g_ffn[0],
                  w_ff1[0], w_ff2[0], g_final)
```

```python
import functools

import numpy as np
import jax
import jax.numpy as jnp
from jax import lax
from jax.experimental import pallas as pl
from jax.experimental.pallas import tpu as pltpu

F32 = jnp.float32
BF16 = jnp.bfloat16

EPS = 1e-6
NEG = -1e30
LOG2_E = 1.4426950408889634
ML_HEADS = 4
ML_DH = 128
ML_WIDTH = ML_HEADS * ML_DH
ML_CONV = 4
ML_CHUNK = 128
NSA_HEADS = 8
NSA_KV = 2
NSA_REP = NSA_HEADS // NSA_KV
NSA_DH = 64
NSA_WIDTH = NSA_HEADS * NSA_DH
NSA_KV_WIDTH = NSA_KV * NSA_DH
CMP_LEN = 32
CMP_STRIDE = 16
CMP_HIDDEN = 256
SEL_LEN = 64
SEL_TOP = 16
WINDOW = 512
Q_BLOCK = 128
FORCE_BONUS = 1e3
SEL_KT = 256
SEL_GROUP = 8
XA_HEADS = 4
XA_DH = 128
XA_WIDTH = XA_HEADS * XA_DH
N_BRANCH = 3

LANES = 128
SUBLANES = 8
VMEM_LIMIT = 56 * 1024 * 1024

TM_PROJ = 256
TM_TAIL = 512
FF_SLAB = 1024

_NT = (((1,), (1,)), ((), ()))
_TN = (((0,), (0,)), ((), ()))


def _dot(a, b):
    return jnp.dot(a, b, preferred_element_type=F32)


def _dot_nt(a, b):
    return lax.dot_general(a, b, _NT, preferred_element_type=F32)


def _resident(shape):
    nd = len(shape)
    return pl.BlockSpec(shape, lambda *_: (0,) * nd, pipeline_mode=pl.Buffered(1))


def _rms(x, g):
    return x * lax.rsqrt(jnp.mean(x * x, axis=-1, keepdims=True) + EPS) * g


_PROJ_COLS = (
    ("qk", 2 * ML_WIDTH, BF16),
    ("v", ML_WIDTH, BF16),
    ("o", ML_WIDTH, BF16),
    ("gcol", LANES, F32),
    ("kc", NSA_KV_WIDTH, F32),
    ("vc", NSA_KV_WIDTH, F32),
    ("ks", NSA_KV_WIDTH, BF16),
    ("kw", NSA_KV_WIDTH, BF16),
    ("vs0", LANES, BF16),
    ("vs1", LANES, BF16),
    ("vw0", LANES, BF16),
    ("vw1", LANES, BF16),
    ("nq", NSA_HEADS * LANES, BF16),
    ("xq", XA_WIDTH, BF16),
    ("mg", None, BF16),
)
_PROJ_MLSTM_ONLY = ("qk", "v", "o")
_PROJ_CHUNK = 1024


def _proj_layout(d_model):
    cols, off = [], 0
    for name, width, dt in _PROJ_COLS:
        width = N_BRANCH * d_model if width is None else width
        cols.append((name, off, width, dt))
        off += width
    return cols, off


def _inproj_kernel(cols, tiles_per_b, x_ref, g_ref, w_ref, b_ref, wt_ref, bt_ref, conv_ref, ng_ref,
                   *refs):
    out_names = [c[0] for c in cols if c[0] not in _PROJ_MLSTM_ONLY]
    out_refs = dict(zip(out_names, refs[:len(out_names)]))
    yml_ref, xbuf, c_scr, m_scr = refs[len(out_names):]
    tm = x_ref.shape[0]
    halo = SUBLANES

    @pl.when(pl.program_id(0) % tiles_per_b == 0)
    def _():
        xbuf[0:halo, :] = jnp.zeros((halo, xbuf.shape[1]), F32)
        c_scr[...] = jnp.zeros(c_scr.shape, F32)
        m_scr[...] = jnp.zeros(m_scr.shape, F32)

    x = x_ref[...]
    xg = x * g_ref[...]
    xb = xg.astype(BF16)
    r = lax.rsqrt(jnp.mean(x * x, axis=-1, keepdims=True) + EPS)
    grow = _dot_nt(wt_ref[...], (xg * r).astype(BF16)) + bt_ref[...]
    kept = {name: [] for name in _PROJ_MLSTM_ONLY + ("gcol",)}
    mlstm_ready = max(off + width for name, off, width, _ in cols if name in kept)
    ncols = w_ref.shape[1]
    for c0 in range(0, ncols, _PROJ_CHUNK):
        cw = min(_PROJ_CHUNK, ncols - c0)
        chunk = r * _dot(xb, w_ref[:, c0:c0 + cw]) + b_ref[:, c0:c0 + cw]
        for name, off, width, dt in cols:
            lo, hi = max(off, c0), min(off + width, c0 + cw)
            if lo >= hi:
                continue
            acc = chunk[:, lo - c0:hi - c0]
            cs = slice(lo - off, hi - off)
            if name == "qk":
                xbuf[halo:halo + tm, cs] = acc
                y = acc * conv_ref[0:1, cs]
                for j in range(1, ML_CONV):
                    y = y + xbuf[halo - j:halo - j + tm, cs] * conv_ref[j:j + 1, cs]
                xbuf[0:halo, cs] = xbuf[tm:tm + halo, cs]
                col = lax.broadcasted_iota(jnp.int32, (1, hi - lo), 1) + (lo - off)
                acc = y * jax.nn.sigmoid(y) * jnp.where(col >= ML_WIDTH, ML_DH ** -0.5, 1.0)
            elif name in ("o", "mg"):
                acc = jax.nn.sigmoid(acc)
            acc = acc.astype(dt)
            if name in kept:
                kept[name].append(acc)
            if name in out_refs:
                out_refs[name][:, cs] = acc
        if c0 < mlstm_ready <= c0 + cw:
            qk, v, og = (jnp.concatenate(kept[n], axis=1) for n in _PROJ_MLSTM_ONLY)
            gcol = kept["gcol"][0]
            c_state = [c_scr[hh] for hh in range(ML_HEADS)]
            m_state = [m_scr[hh:hh + 1, 0:1] for hh in range(ML_HEADS)]
            for r0 in range(0, tm, ML_CHUNK):
                rows = slice(r0, r0 + ML_CHUNK)
                y_ml, c_state, m_state = _mlstm_chunk(qk[rows], v[rows], og[rows], gcol[rows],
                                                      grow[:, rows], ng_ref[...], c_state, m_state)
                yml_ref[rows, :] = y_ml
            for hh in range(ML_HEADS):
                c_scr[hh] = c_state[hh]
                m_scr[hh:hh + 1, :] = jnp.broadcast_to(m_state[hh], (1, m_scr.shape[1]))


def _split_w_in(w_in, b_in):
    widths = (ML_WIDTH, ML_WIDTH, ML_WIDTH, ML_WIDTH, ML_HEADS, ML_HEADS,
              NSA_WIDTH, NSA_KV_WIDTH, NSA_KV_WIDTH, NSA_KV_WIDTH, NSA_KV_WIDTH, NSA_KV_WIDTH,
              NSA_KV_WIDTH, 3 * NSA_HEADS, XA_WIDTH, w_in.shape[1])
    names = ("ml_q", "ml_k", "ml_v", "ml_o", "ml_i", "ml_f", "ns_q", "ns_kc", "ns_vc", "ns_ks",
             "ns_vs", "ns_kw", "ns_vw", "ns_g", "xa_q", "mg")
    out, off = {}, 0
    for n, wd in zip(names, widths):
        end = w_in.shape[1] if n == "mg" else off + wd
        out[n] = (w_in[:, off:end], b_in[off:end])
        off = end
    return out


def _inproj(x2, g_mix, w_in, b_in, ml_conv, ml_norm_g, seq):
    n_tok, d = x2.shape
    cols, ncols = _proj_layout(d)
    p = _split_w_in(w_in, b_in)

    def pad_cols(w, b, width):
        return (jnp.pad(w, ((0, 0), (0, width - w.shape[1]))), jnp.pad(b, (0, width - b.shape[0])))

    wq, bq = p["ns_q"]
    wq = wq.reshape(d, NSA_HEADS, 1, NSA_DH)
    bq = bq.reshape(NSA_HEADS, 1, NSA_DH)
    in_group = jnp.asarray(np.arange(NSA_HEADS)[:, None] // NSA_REP == np.arange(NSA_KV)[None, :],
                           w_in.dtype)[:, :, None]
    wq_slots = wq * in_group
    bq_slots = bq * in_group
    gate_w = jnp.concatenate([p["ml_i"][0], p["ml_f"][0]], axis=1)
    gate_b = jnp.concatenate([p["ml_i"][1], p["ml_f"][1]])
    small_w = jnp.concatenate([gate_w, p["ns_g"][0]], axis=1)
    small_b = jnp.concatenate([gate_b, p["ns_g"][1]])

    def with_ones(wb, g):
        w, b = wb
        w = jnp.pad(w[:, g * NSA_DH:(g + 1) * NSA_DH], ((0, 0), (0, LANES - NSA_DH)))
        b = jnp.concatenate([b[g * NSA_DH:(g + 1) * NSA_DH], jnp.ones((LANES - NSA_DH,), b.dtype)])
        return w, b

    pieces = {
        "qk": (jnp.concatenate([p["ml_q"][0], p["ml_k"][0]], axis=1),
               jnp.concatenate([p["ml_q"][1], p["ml_k"][1]])),
        "o": p["ml_o"], "mg": p["mg"],
        "gcol": pad_cols(small_w, small_b, LANES),
        "kc": p["ns_kc"], "vc": p["ns_vc"], "v": p["ml_v"],
        "nq": (wq_slots.reshape(d, NSA_HEADS * LANES), bq_slots.reshape(NSA_HEADS * LANES)),
        "xq": p["xa_q"], "ks": p["ns_ks"], "kw": p["ns_kw"],
        "vs0": with_ones(p["ns_vs"], 0), "vs1": with_ones(p["ns_vs"], 1),
        "vw0": with_ones(p["ns_vw"], 0), "vw1": with_ones(p["ns_vw"], 1),
    }
    w_cols = jnp.concatenate([pieces[name][0] for name, *_ in cols], axis=1).astype(BF16)
    b_cols = jnp.concatenate([pieces[name][1] for name, *_ in cols])[None, :]
    w_rows = gate_w.T.astype(BF16)
    b_rows = gate_b[:, None]

    tm = TM_PROJ
    tiles_per_b = seq // tm
    outs_hbm = [c for c in cols if c[0] not in _PROJ_MLSTM_ONLY] + [("y_ml", None, ML_WIDTH, BF16)]
    outs = pl.pallas_call(
        functools.partial(_inproj_kernel, cols, tiles_per_b),
        out_shape=[jax.ShapeDtypeStruct((n_tok, width), dt) for _, _, width, dt in outs_hbm],
        grid=(n_tok // tm,),
        in_specs=[pl.BlockSpec((tm, d), lambda i: (i, 0)),
                  _resident((1, d)), _resident((d, ncols)), _resident((1, ncols)),
                  _resident((SUBLANES, d)), _resident((SUBLANES, 1)),
                  _resident((ML_CONV, 2 * ML_WIDTH)), _resident((1, ML_WIDTH))],
        out_specs=[pl.BlockSpec((tm, width), lambda i: (i, 0)) for _, _, width, _ in outs_hbm],
        scratch_shapes=[pltpu.VMEM((tm + SUBLANES, 2 * ML_WIDTH), F32),
                        pltpu.VMEM((ML_HEADS, ML_DH, 2 * ML_DH), F32),
                        pltpu.VMEM((SUBLANES, LANES), F32)],
        compiler_params=pltpu.CompilerParams(dimension_semantics=("arbitrary",),
                                             vmem_limit_bytes=VMEM_LIMIT),
        name="inproj",
    )(x2, g_mix[None, :], w_cols, b_cols, w_rows, b_rows, ml_conv, ml_norm_g[None, :])
    return dict(zip([c[0] for c in outs_hbm], outs))


def _mlstm_chunk(qk, v, og, gcol, grow, ng, c_prev, m_prev):
    L = ML_CHUNK
    r_i = lax.broadcasted_iota(jnp.int32, (L, L), 0)
    c_i = lax.broadcasted_iota(jnp.int32, (L, L), 1)
    causal = c_i <= r_i
    tril = causal.astype(F32)
    triu = (r_i <= c_i).astype(F32)
    b_col = jnp.dot(tril, jax.nn.log_sigmoid(gcol), precision=lax.Precision.HIGHEST,
                    preferred_element_type=F32)
    b_row = jnp.dot(jax.nn.log_sigmoid(grow), triu, precision=lax.Precision.HIGHEST,
                    preferred_element_type=F32)
    ones_col = jnp.where(lax.broadcasted_iota(jnp.int32, (L, ML_DH), 1) == 0, 1.0, 0.0).astype(BF16)

    def operands(hh):
        sl = slice(hh * ML_DH, (hh + 1) * ML_DH)
        k = qk[:, ML_WIDTH + hh * ML_DH:ML_WIDTH + (hh + 1) * ML_DH]
        return qk[:, sl], k, jnp.concatenate([v[:, sl], ones_col], axis=1), c_prev[hh]

    ops = [operands(hh) for hh in range(ML_HEADS)]
    qk_t = [_dot_nt(q, k) for q, k, _, _ in ops]
    inter = [_dot(q, c.astype(BF16)) for q, _, _, c in ops]

    stab, ke_all, s_all = [], [], []
    for hh in range(ML_HEADS):
        li_c = gcol[:, hh:hh + 1]
        b_c = b_col[:, ML_HEADS + hh:ML_HEADS + hh + 1]
        li_r = grow[hh:hh + 1, :]
        b_r = b_row[ML_HEADS + hh:ML_HEADS + hh + 1, :]
        g = b_c[L - 1:L, :]
        rel = jnp.where(causal, li_r - b_r, -jnp.inf)
        mm = jnp.maximum(m_prev[hh], jnp.max(rel, axis=-1, keepdims=True))
        m_t = b_c + mm
        s_all.append((qk_t[hh] * jnp.exp(rel - mm)).astype(BF16))
        w_end = g - b_c + li_c
        m_loc = jnp.max(w_end, axis=0, keepdims=True)
        ke_all.append((ops[hh][1] * jnp.exp(w_end - m_loc)).astype(BF16))
        m_new = jnp.maximum(g + m_prev[hh], m_loc)
        stab.append((m_t, jnp.exp(m_prev[hh] - mm), jnp.exp(g + m_prev[hh] - m_new),
                     jnp.exp(m_loc - m_new), m_new))

    intra = [_dot(s, o[2]) for s, o in zip(s_all, ops)]
    a_c = [lax.dot_general(ke, o[2], _TN, preferred_element_type=F32) for ke, o in zip(ke_all, ops)]

    ys, new_c, new_m = [], [], []
    for hh in range(ML_HEADS):
        sl = slice(hh * ML_DH, (hh + 1) * ML_DH)
        m_t, sc, a, bb, m_new = stab[hh]
        nd = intra[hh] + sc * inter[hh]
        den = nd[:, ML_DH:ML_DH + 1]
        h = nd[:, 0:ML_DH] / jnp.maximum(jnp.abs(den), jnp.exp(-m_t))
        hn = h * lax.rsqrt(jnp.mean(h * h, axis=-1, keepdims=True) + EPS)
        ys.append((og[:, sl] * hn * ng[:, sl]).astype(BF16))
        new_c.append(a * ops[hh][3] + bb * a_c[hh])
        new_m.append(m_new)
    return jnp.concatenate(ys, axis=1), new_c, new_m


def _gelu_tanh(x):
    return 0.5 * x * (1.0 + jnp.tanh(np.sqrt(2.0 / np.pi) * (x + 0.044715 * (x * x * x))))


def _compress_kernel(k_ref, v_ref, pe_ref, w1_ref, w2k_ref, w2v_ref, kc_ref, vc0_ref, vc1_ref):
    nrow = kc_ref.shape[1]
    S = CMP_STRIDE

    def hidden(x_ref, which):
        lo = hi = None
        for j in range(0, S, 2):
            x = [x_ref[pl.ds(j + d, nrow, stride=S), :] for d in range(2)]
            for half in range(CMP_LEN // S):
                l0 = half * S + j
                xp = jnp.concatenate([(x[d] + pe_ref[which, l0 + d:l0 + d + 1, :]).astype(BF16)
                                      for d in range(2)], axis=1)
                t = _dot(xp, w1_ref[which, l0 // 2])
                if half == 0:
                    lo = t if lo is None else lo + t
                else:
                    hi = t if hi is None else hi + t
        return _gelu_tanh(lo + pltpu.roll(hi, nrow - 1, 0)).astype(BF16)

    kc_ref[0] = _dot(hidden(k_ref, 0), w2k_ref[...]).astype(kc_ref.dtype)
    hv = hidden(v_ref, 1)
    ones_half = lax.broadcasted_iota(jnp.int32, (nrow, LANES), 1) >= NSA_DH
    for g, o_ref in enumerate((vc0_ref, vc1_ref)):
        o_ref[0] = jnp.where(ones_half, 1.0, _dot(hv, w2v_ref[g])).astype(o_ref.dtype)


def _compress(pr, cmp_pe, cmp_w1, cmp_w2, batch, seq):
    nrow = seq // CMP_STRIDE
    hid = NSA_KV * CMP_HIDDEN
    w1 = cmp_w1.astype(BF16).reshape(2, CMP_LEN, NSA_DH, CMP_HIDDEN)
    z1 = jnp.zeros_like(w1)
    w1e = jnp.stack([jnp.concatenate([w1, z1], axis=-1), jnp.concatenate([z1, w1], axis=-1)], axis=2)
    w1e = w1e.reshape(2, CMP_LEN // 2, 2 * NSA_KV_WIDTH, hid)
    w2 = cmp_w2.astype(BF16)
    z2 = jnp.zeros_like(w2[0])
    w2k = jnp.concatenate([jnp.concatenate([w2[0], z2], axis=1),
                           jnp.concatenate([z2, w2[0]], axis=1)], axis=0)
    w2v = jnp.stack([jnp.pad(w2[1], ((g * CMP_HIDDEN, (NSA_KV - 1 - g) * CMP_HIDDEN),
                                     (0, LANES - NSA_DH))) for g in range(NSA_KV)])
    pe = jnp.concatenate([cmp_pe] * NSA_KV, axis=-1)
    tok_blk = pl.BlockSpec((seq, NSA_KV_WIDTH), lambda b: (b, 0))
    out_blk = pl.BlockSpec((1, nrow, LANES), lambda b: (b, 0, 0))
    return pl.pallas_call(
        _compress_kernel,
        out_shape=(jax.ShapeDtypeStruct((batch, nrow, LANES), BF16),) * 3,
        grid=(batch,),
        in_specs=[tok_blk, tok_blk,
                  _resident((2, CMP_LEN, NSA_KV_WIDTH)),
                  _resident((2, CMP_LEN // 2, 2 * NSA_KV_WIDTH, hid)),
                  _resident((hid, NSA_KV_WIDTH)),
                  _resident((NSA_KV, hid, LANES))],
        out_specs=(out_blk, out_blk, out_blk),
        compiler_params=pltpu.CompilerParams(dimension_semantics=("parallel",),
                                             vmem_limit_bytes=VMEM_LIMIT),
        name="compress",
    )(pr["kc"], pr["vc"], pe, w1e, w2k, w2v)


def _alibi_slope(h):
    return float(2.0 ** (-8.0 * (h + 1) / NSA_HEADS))


NSA_ROWS = NSA_HEADS * Q_BLOCK
WIN_KEYS = WINDOW + Q_BLOCK
F_SEL_BLOCK = 0
F_SEL_OFF = SEL_LEN
F_WIN_OFF = SEL_LEN + 1
F_WIN_BLK = SEL_LEN + 2
F_CMP = SEL_LEN + 3


def _tile_heads(x):
    return jnp.concatenate([x] * NSA_HEADS, axis=0)


def _nsa_kernel(nbs, q_ref, g_ref, kc_ref, vc0_ref, vc1_ref, ks_ref, vs0_ref, vs1_ref,
                kw_ref, vw0_ref, vw1_ref, ov_ref, gx_ref, y_ref,
                qe_scr, posf_scr, wposf_scr, cposf_scr, s_scr, mb_scr, acc_scr):
    QB = Q_BLOCK
    KT = SEL_KT
    HALF = NSA_REP * QB
    t0 = pl.program_id(1) * QB
    ncmp = kc_ref.shape[1]
    lane = lax.broadcasted_iota(jnp.int32, (1, LANES), 1)
    lo_half = lane < NSA_DH

    @pl.when(pl.program_id(1) == 0)
    def _():
        seq = posf_scr.shape[0]
        k = lax.broadcasted_iota(jnp.int32, (seq, LANES), 0)
        c = lax.broadcasted_iota(jnp.int32, (seq, LANES), 1)
        posf = jnp.where(c < SEL_LEN, jnp.where(k // SEL_LEN == c, 1.0, 0.0),
                         jnp.where(c == F_SEL_OFF, (k % SEL_LEN).astype(F32), 0.0))
        posf_scr[...] = posf.astype(BF16)
        k = lax.broadcasted_iota(jnp.int32, (WIN_KEYS, LANES), 0)
        c = lax.broadcasted_iota(jnp.int32, (WIN_KEYS, LANES), 1)
        wposf = jnp.where(c == F_WIN_OFF, (k % SEL_LEN).astype(F32),
                          jnp.where(c == F_WIN_BLK, (k // SEL_LEN).astype(F32), 0.0))
        wposf_scr[...] = wposf.astype(BF16)
        k = lax.broadcasted_iota(jnp.int32, (ncmp, LANES), 0)
        c = lax.broadcasted_iota(jnp.int32, (ncmp, LANES), 1)
        cposf_scr[...] = jnp.where(c == F_CMP, k.astype(F32), 0.0).astype(BF16)

    def const_feat(h):
        s = _alibi_slope(h)
        return jnp.where((lane == F_SEL_OFF) | (lane == F_WIN_OFF), s,
                         jnp.where(lane == F_WIN_BLK, SEL_LEN * s,
                                   jnp.where(lane == F_CMP, CMP_STRIDE * s, 0.0)))

    for h in range(NSA_HEADS):
        rows = slice(h * QB, (h + 1) * QB)
        qe_scr[rows, 0:LANES] = q_ref[:, h * LANES:(h + 1) * LANES] * 0.125
        qe_scr[rows, LANES:2 * LANES] = jnp.broadcast_to(const_feat(h), (QB, LANES)).astype(BF16)

    def normalised(o, guard):
        outs = []
        for pair in range(NSA_HEADS // 2):
            e = o[(2 * pair) * QB:(2 * pair + 1) * QB]
            d = o[(2 * pair + 1) * QB:(2 * pair + 2) * QB]
            num = jnp.where(lo_half, e, pltpu.roll(d, NSA_DH, 1))
            den = jnp.where(lo_half, pltpu.roll(e, NSA_DH, 1), d)
            if guard:
                den = jnp.where(den > 0.0, den, 1.0)
            outs.append(num / den)
        return outs

    ws = pl.multiple_of(jnp.maximum(t0 - WINDOW, 0), QB)
    qs = lax.broadcasted_iota(jnp.int32, (QB, WIN_KEYS), 0)
    kl = lax.broadcasted_iota(jnp.int32, (QB, WIN_KEYS), 1)
    dist_w = (t0 - ws) + qs - kl
    ok_w = (dist_w >= 0) & (dist_w < WINDOW)
    s_w = _dot_nt(qe_scr[...], jnp.concatenate([kw_ref[pl.ds(ws, WIN_KEYS), :], wposf_scr[...]], axis=1))

    q_sub = lax.broadcasted_iota(jnp.int32, (QB, ncmp), 0)
    n_lane = lax.broadcasted_iota(jnp.int32, (QB, ncmp), 1)
    ok_c = n_lane * CMP_STRIDE + (CMP_LEN - 1) <= t0 + q_sub
    s = _dot_nt(qe_scr[...], jnp.concatenate([kc_ref[0], cposf_scr[...]], axis=1))
    s = s + _tile_heads(jnp.where(ok_c, 0.0, NEG))
    e = jnp.exp(s - jnp.max(s, axis=-1, keepdims=True)) * _tile_heads(jnp.where(ok_c, 1.0, 0.0))
    e_bf = e.astype(BF16)
    oc = jnp.concatenate([_dot(e_bf[0:HALF], vc0_ref[0]), _dot(e_bf[HALF:2 * HALF], vc1_ref[0])],
                         axis=0)
    o_cmp = normalised(oc, True)
    l_c = jnp.where(lo_half, pltpu.roll(oc, NSA_DH, 1), oc)
    inv_l = 1.0 / jnp.where(l_c > 0.0, l_c, 1.0)
    p_c = e * jnp.concatenate([inv_l] * (ncmp // LANES), axis=1)

    imp_ts = []
    for g in range(NSA_KV):
        psum = p_c[(g * NSA_REP) * QB:(g * NSA_REP + 1) * QB]
        for r in range(1, NSA_REP):
            psum = psum + p_c[(g * NSA_REP + r) * QB:(g * NSA_REP + r + 1) * QB]
        p_hi = psum.astype(BF16)
        p_lo = (psum - p_hi.astype(F32)).astype(BF16)
        imp = _dot(p_hi, ov_ref[...]) + _dot(p_lo, ov_ref[...])
        imp_ts.append(imp.T[0:nbs, :])

    n_grp = nbs // SUBLANES
    sub8 = lax.broadcasted_iota(jnp.int32, (SUBLANES, QB), 0)
    tpos = t0 + lax.broadcasted_iota(jnp.int32, (1, QB), 1)
    cur = tpos // SEL_LEN
    j_lane = lane.astype(F32)
    for g in range(NSA_KV):
        imp_t = imp_ts[g]
        j_sub = lax.broadcasted_iota(jnp.int32, (nbs, QB), 0)
        forced = (j_sub == 0) | (j_sub == cur) | (j_sub == cur - 1)
        score = jnp.where(j_sub <= cur, imp_t + jnp.where(forced, FORCE_BONUS, 0.0), NEG)
        parts = [score[a * SUBLANES:(a + 1) * SUBLANES, :] for a in range(n_grp)]
        ranks = [jnp.zeros((SUBLANES, QB), F32) for _ in range(n_grp)]
        for i in range(nbs):
            row = jnp.broadcast_to(score[i:i + 1, :], (SUBLANES, QB))
            for a in range(n_grp):
                if a * SUBLANES > i:
                    beats = jnp.where(row >= parts[a], 1.0, 0.0)
                elif (a + 1) * SUBLANES - 1 < i:
                    beats = jnp.where(row > parts[a], 1.0, 0.0)
                else:
                    beats = jnp.where(sub8 + a * SUBLANES > i,
                                      jnp.where(row >= parts[a], 1.0, 0.0),
                                      jnp.where(row > parts[a], 1.0, 0.0))
                ranks[a] = ranks[a] + beats
        sel_t = [jnp.where(rk < float(min(SEL_TOP, nbs)), 0.0, NEG) for rk in ranks]
        if nbs < LANES:
            sel_t.append(jnp.zeros((LANES - nbs, QB), F32))
        sel_bias = jnp.concatenate(sel_t, axis=0).T
        for r in range(NSA_REP):
            h = g * NSA_REP + r
            feat = jnp.where(lo_half, sel_bias + (_alibi_slope(h) * SEL_LEN) * j_lane, const_feat(h))
            qe_scr[h * QB:(h + 1) * QB, LANES:2 * LANES] = feat.astype(BF16)

    s_w = s_w + _tile_heads(jnp.where(ok_w, 0.0, NEG))
    p_w = jnp.exp(s_w - jnp.max(s_w, axis=-1, keepdims=True)).astype(BF16)
    o_win = normalised(jnp.concatenate([_dot(p_w[0:HALF], vw0_ref[pl.ds(ws, WIN_KEYS), :]),
                                        _dot(p_w[HALF:2 * HALF], vw1_ref[pl.ds(ws, WIN_KEYS), :])],
                                       axis=0), False)

    last_tile = t0 // KT
    mb_scr[...] = jnp.full(mb_scr.shape, NEG, F32)

    U = SEL_GROUP

    def score_group(kt0, count, causal):
        k0 = pl.multiple_of(kt0 * KT, KT)
        kx = jnp.concatenate([ks_ref[pl.ds(k0, count * KT), :],
                              posf_scr[pl.ds(k0, count * KT), :]], axis=1)
        s = _dot_nt(qe_scr[...], kx)
        tile_max = None
        for j in range(count):
            sj = s[:, j * KT:(j + 1) * KT] * LOG2_E
            if causal and j == count - 1:
                qs = lax.broadcasted_iota(jnp.int32, (QB, KT), 0)
                kl = lax.broadcasted_iota(jnp.int32, (QB, KT), 1)
                sj = sj + _tile_heads(jnp.where(k0 + j * KT + kl <= t0 + qs, 0.0, NEG))
            s_scr[kt0 + j] = sj
            for c0 in range(0, KT, LANES):
                part = sj[:, c0:c0 + LANES]
                tile_max = part if tile_max is None else jnp.maximum(tile_max, part)
        group_max = jnp.max(tile_max, axis=-1, keepdims=True)
        mb_scr[...] = jnp.maximum(mb_scr[...], jnp.broadcast_to(group_max, mb_scr.shape))

    def score_full_group(i, carry):
        score_group(i * U, U, False)
        return carry

    lax.fori_loop(0, last_tile // U, score_full_group, 0)
    for r in range(U):
        @pl.when(last_tile % U == r)
        def _(r=r):
            score_group(last_tile - r, r + 1, True)

    acc_scr[...] = jnp.zeros(acc_scr.shape, F32)

    def value_group(kt0, count):
        k0 = pl.multiple_of(kt0 * KT, KT)
        mb = mb_scr[...]
        mbw = jnp.concatenate([mb] * (KT // LANES), axis=1)
        p = jnp.concatenate([jnp.exp2(s_scr[kt0 + j] - mbw).astype(BF16) for j in range(count)],
                            axis=1)
        acc_scr[0:HALF, :] += _dot(p[0:HALF], vs0_ref[pl.ds(k0, count * KT), :])
        acc_scr[HALF:2 * HALF, :] += _dot(p[HALF:2 * HALF], vs1_ref[pl.ds(k0, count * KT), :])

    def value_full_group(i, carry):
        value_group(i * U, U)
        return carry

    n_tiles = last_tile + 1
    lax.fori_loop(0, n_tiles // U, value_full_group, 0)
    for r in range(1, U):
        @pl.when(n_tiles % U == r)
        def _(r=r):
            value_group(n_tiles - r, r)

    o_sel = normalised(acc_scr[...], False)

    gate = jax.nn.sigmoid(g_ref[...])
    g_hi = gate.astype(BF16)
    g_lo = (gate - g_hi.astype(F32)).astype(BF16)
    spread = _dot(g_hi, gx_ref[...]) + _dot(g_lo, gx_ref[...])
    for pair in range(NSA_HEADS // 2):
        y = None
        for b, o in enumerate((o_cmp, o_sel, o_win)):
            c0 = b * NSA_WIDTH + pair * LANES
            term = spread[:, c0:c0 + LANES] * o[pair]
            y = term if y is None else y + term
        y_ref[:, pair * LANES:(pair + 1) * LANES] = y.astype(y_ref.dtype)


def _nsa(pr, kc, vc0, vc1, batch, seq):
    QB = Q_BLOCK
    nq = seq // QB
    nbs = seq // SEL_LEN
    ncmp = kc.shape[1]
    cs = np.arange(ncmp) * CMP_STRIDE
    js = np.arange(nbs) * SEL_LEN
    ov = np.zeros((ncmp, LANES), np.float32)
    ov[:, :nbs] = (cs[:, None] < js[None, :] + SEL_LEN) & (cs[:, None] + CMP_LEN > js[None, :])
    gx = np.zeros((LANES, 3 * NSA_WIDTH), np.float32)
    for h in range(NSA_HEADS):
        for b in range(3):
            gx[SUBLANES + 3 * h + b, b * NSA_WIDTH + h * NSA_DH:b * NSA_WIDTH + (h + 1) * NSA_DH] = 1.0
    tok = lambda b, i: (b * nq + i, 0)
    per_b = lambda b, i: (b, 0)
    per_b3 = lambda b, i: (b, 0, 0)
    kv_spec = pl.BlockSpec((seq, LANES), per_b)
    cmp_spec = pl.BlockSpec((1, ncmp, LANES), per_b3)
    return pl.pallas_call(
        functools.partial(_nsa_kernel, nbs),
        out_shape=jax.ShapeDtypeStruct((batch * seq, NSA_WIDTH), BF16),
        grid=(batch, nq),
        in_specs=[pl.BlockSpec((QB, NSA_HEADS * LANES), tok),
                  pl.BlockSpec((QB, LANES), tok),
                  cmp_spec, cmp_spec, cmp_spec,
                  kv_spec, kv_spec, kv_spec, kv_spec, kv_spec, kv_spec,
                  pl.BlockSpec((ncmp, LANES), lambda b, i: (0, 0)),
                  pl.BlockSpec((LANES, 3 * NSA_WIDTH), lambda b, i: (0, 0))],
        out_specs=pl.BlockSpec((QB, NSA_WIDTH), tok),
        scratch_shapes=[pltpu.VMEM((NSA_ROWS, 2 * LANES), BF16),
                        pltpu.VMEM((seq, LANES), BF16),
                        pltpu.VMEM((WIN_KEYS, LANES), BF16),
                        pltpu.VMEM((ncmp, LANES), BF16),
                        pltpu.VMEM((seq // SEL_KT, NSA_ROWS, SEL_KT), F32),
                        pltpu.VMEM((NSA_ROWS, LANES), F32),
                        pltpu.VMEM((NSA_ROWS, LANES), F32)],
        compiler_params=pltpu.CompilerParams(dimension_semantics=("parallel", "arbitrary"),
                                             vmem_limit_bytes=VMEM_LIMIT),
        name="nsa",
    )(pr["nq"], pr["gcol"], kc, vc0, vc1, pr["ks"], pr["vs0"], pr["vs1"],
      pr["kw"], pr["vw0"], pr["vw1"], jnp.asarray(ov, BF16), jnp.asarray(gx, BF16))


def _memkv_kernel(mem_ref, g_ref, w_ref, kv_ref):
    u = _rms(mem_ref[0], g_ref[...]).astype(BF16)
    kv_ref[0] = _dot(u, w_ref[...]).astype(kv_ref.dtype)


def _memkv(mem, g_mem, w_mem_kv):
    batch, n_mem, d = mem.shape
    return pl.pallas_call(
        _memkv_kernel,
        out_shape=jax.ShapeDtypeStruct((batch, n_mem, 2 * XA_WIDTH), BF16),
        grid=(batch,),
        in_specs=[pl.BlockSpec((1, n_mem, d), lambda b: (b, 0, 0)),
                  _resident((1, d)), _resident((d, 2 * XA_WIDTH))],
        out_specs=pl.BlockSpec((1, n_mem, 2 * XA_WIDTH), lambda b: (b, 0, 0)),
        compiler_params=pltpu.CompilerParams(dimension_semantics=("parallel",),
                                             vmem_limit_bytes=VMEM_LIMIT),
        name="memkv",
    )(mem, g_mem[None, :], w_mem_kv.astype(BF16))


def _tail_kernel(x_ref, yml_ref, ynsa_ref, xq_ref, kv_ref, mg_ref, wb_ref, wo_ref, gf_ref,
                 w1_ref, w2_ref, gl_ref, out_ref):
    d = x_ref.shape[1]
    kv = kv_ref[0]
    heads = [slice(hh * XA_DH, (hh + 1) * XA_DH) for hh in range(XA_HEADS)]
    scores = [_dot_nt(xq_ref[:, sl], kv[:, sl]) * (XA_DH ** -0.5) for sl in heads]
    ups = [_dot(yml_ref[...], wb_ref[0]), _dot(ynsa_ref[...], wb_ref[1])]
    y_xa = []
    for hh, s in enumerate(scores):
        p = jnp.exp(s - jnp.max(s, axis=-1, keepdims=True))
        l = jnp.sum(p, axis=-1, keepdims=True)
        y_xa.append((_dot(p.astype(BF16), kv[:, XA_WIDTH + hh * XA_DH:XA_WIDTH + (hh + 1) * XA_DH])
                     / l).astype(BF16))
    ups.append(_dot(jnp.concatenate(y_xa, axis=-1), wb_ref[2]))
    merged = None
    for j in range(N_BRANCH):
        term = mg_ref[:, j * d:(j + 1) * d] * ups[j]
        merged = term if merged is None else merged + term
    h = x_ref[...] + _dot(merged.astype(BF16), wo_ref[...])
    hb = (h * gf_ref[...]).astype(BF16)
    r = lax.rsqrt(jnp.mean(h * h, axis=-1, keepdims=True) + EPS)
    acc = h
    for c0 in range(0, w1_ref.shape[1], FF_SLAB):
        a = jnp.maximum(r * _dot(hb, w1_ref[:, c0:c0 + FF_SLAB]), 0.0)
        acc = acc + _dot((a * a).astype(BF16), w2_ref[c0:c0 + FF_SLAB, :])
    out_ref[...] = _rms(acc, gl_ref[...])


def _tail(x2, y_ml, y_nsa, pr, mem_kv, w_branch, w_out, g_ffn, w_ff1, w_ff2, g_final, seq):
    n_tok, d = x2.shape
    tm = TM_TAIL
    d_ff = w_ff1.shape[1]
    n_mem = mem_kv.shape[1]
    tiles_per_b = seq // tm
    row = lambda i: (i, 0)
    return pl.pallas_call(
        _tail_kernel,
        out_shape=jax.ShapeDtypeStruct((n_tok, d), F32),
        grid=(n_tok // tm,),
        in_specs=[pl.BlockSpec((tm, d), row),
                  pl.BlockSpec((tm, ML_WIDTH), row),
                  pl.BlockSpec((tm, NSA_WIDTH), row),
                  pl.BlockSpec((tm, XA_WIDTH), row),
                  pl.BlockSpec((1, n_mem, 2 * XA_WIDTH), lambda i: (i // tiles_per_b, 0, 0)),
                  pl.BlockSpec((tm, N_BRANCH * d), row),
                  _resident((N_BRANCH, ML_WIDTH, d)), _resident((d, d)), _resident((1, d)),
                  _resident((d, d_ff)), _resident((d_ff, d)), _resident((1, d))],
        out_specs=pl.BlockSpec((tm, d), row),
        compiler_params=pltpu.CompilerParams(dimension_semantics=("parallel",),
                                             vmem_limit_bytes=VMEM_LIMIT),
        name="tail",
    )(x2, y_ml, y_nsa, pr["xq"], mem_kv, pr["mg"], w_branch.astype(BF16), w_out.astype(BF16),
      g_ffn[None, :], w_ff1.astype(BF16), w_ff2.astype(BF16), g_final[None, :])


def _layer(x, mem, g_mix, w_in, b_in, ml_conv, ml_norm_g, cmp_pe, cmp_w1, cmp_w2, g_mem, w_mem_kv,
           w_branch, w_out, g_ffn, w_ff1, w_ff2, g_final):
    batch, seq, d = x.shape
    assert seq % (SEL_LEN * SUBLANES) == 0 and seq % SEL_KT == 0 and seq >= WIN_KEYS
    assert seq // SEL_LEN <= NSA_DH, "one selection-block feature lane per block"
    assert seq // CMP_STRIDE <= 256, "compressed token index must be exact in bf16"
    assert seq % TM_PROJ == 0 and seq % TM_TAIL == 0 and TM_PROJ % ML_CHUNK == 0
    x2 = x.reshape(batch * seq, d)
    pr = _inproj(x2, g_mix, w_in, b_in, ml_conv, ml_norm_g, seq)
    y_ml = pr["y_ml"]
    kc, vc0, vc1 = _compress(pr, cmp_pe, cmp_w1, cmp_w2, batch, seq)
    y_nsa = _nsa(pr, kc, vc0, vc1, batch, seq)
    mem_kv = _memkv(mem, g_mem, w_mem_kv)
    out = _tail(x2, y_ml, y_nsa, pr, mem_kv, w_branch, w_out, g_ffn, w_ff1, w_ff2, g_final, seq)
    return out.reshape(batch, seq, d)


def kernel(x, mem, g_mix, w_in, b_in, ml_conv, ml_norm_g, cmp_pe, cmp_w1, cmp_w2, g_mem, w_mem_kv,
           w_branch, w_out, g_ffn, w_ff1, w_ff2, g_final):
    assert g_mix.shape[0] == 1, "single-layer block"
    return _layer(x, mem, g_mix[0], w_in[0], b_in[0], ml_conv[0], ml_norm_g[0], cmp_pe[0],
                  cmp_w1[0], cmp_w2[0], g_mem[0], w_mem_kv[0], w_branch[0], w_out[0], g_ffn[0],
                  w_ff1[0], w_ff2[0], g_final)
```

```python
import functools

import numpy as np
import jax
import jax.numpy as jnp
from jax import lax
from jax.experimental import pallas as pl
from jax.experimental.pallas import tpu as pltpu

F32 = jnp.float32
BF16 = jnp.bfloat16

EPS = 1e-6
NEG = -1e30
LOG2_E = 1.4426950408889634
ML_HEADS = 4
ML_DH = 128
ML_WIDTH = ML_HEADS * ML_DH
ML_CONV = 4
ML_CHUNK = 256
NSA_HEADS = 8
NSA_KV = 2
NSA_REP = NSA_HEADS // NSA_KV
NSA_DH = 64
NSA_WIDTH = NSA_HEADS * NSA_DH
NSA_KV_WIDTH = NSA_KV * NSA_DH
CMP_LEN = 32
CMP_STRIDE = 16
CMP_HIDDEN = 256
SEL_LEN = 64
SEL_TOP = 16
WINDOW = 512
Q_BLOCK = 128
FORCE_BONUS = 1e3
SEL_KT = 256
SEL_GROUP = 8
XA_HEADS = 4
XA_DH = 128
XA_WIDTH = XA_HEADS * XA_DH
N_BRANCH = 3

LANES = 128
SUBLANES = 8
VMEM_LIMIT = 56 * 1024 * 1024

TM_PROJ = 256
TM_TAIL = 512
FF_SLAB = 1024

_NT = (((1,), (1,)), ((), ()))
_TN = (((0,), (0,)), ((), ()))


def _dot(a, b):
    return jnp.dot(a, b, preferred_element_type=F32)


def _dot_nt(a, b):
    return lax.dot_general(a, b, _NT, preferred_element_type=F32)


def _resident(shape):
    nd = len(shape)
    return pl.BlockSpec(shape, lambda *_: (0,) * nd, pipeline_mode=pl.Buffered(1))


def _rms(x, g):
    return x * lax.rsqrt(jnp.mean(x * x, axis=-1, keepdims=True) + EPS) * g


_PROJ_COLS = (
    ("qk", 2 * ML_WIDTH, BF16),
    ("v", ML_WIDTH, BF16),
    ("o", ML_WIDTH, BF16),
    ("gcol", LANES, F32),
    ("kc", NSA_KV_WIDTH, F32),
    ("vc", NSA_KV_WIDTH, F32),
    ("ks", NSA_KV_WIDTH, BF16),
    ("kw", NSA_KV_WIDTH, BF16),
    ("vs0", LANES, BF16),
    ("vs1", LANES, BF16),
    ("vw0", LANES, BF16),
    ("vw1", LANES, BF16),
    ("nq", NSA_HEADS * LANES, BF16),
    ("xq", XA_WIDTH, BF16),
    ("mg", None, BF16),
)
_PROJ_MLSTM_ONLY = ("qk", "v", "o")
_PROJ_CHUNK = 1024


def _proj_layout(d_model):
    cols, off = [], 0
    for name, width, dt in _PROJ_COLS:
        width = N_BRANCH * d_model if width is None else width
        cols.append((name, off, width, dt))
        off += width
    return cols, off


def _inproj_kernel(cols, tiles_per_b, x_ref, g_ref, w_ref, b_ref, wt_ref, bt_ref, conv_ref, ng_ref,
                   *refs):
    out_names = [c[0] for c in cols if c[0] not in _PROJ_MLSTM_ONLY]
    out_refs = dict(zip(out_names, refs[:len(out_names)]))
    yml_ref, xbuf, c_scr, m_scr = refs[len(out_names):]
    tm = x_ref.shape[0]
    halo = SUBLANES

    @pl.when(pl.program_id(0) % tiles_per_b == 0)
    def _():
        xbuf[0:halo, :] = jnp.zeros((halo, xbuf.shape[1]), F32)
        c_scr[...] = jnp.zeros(c_scr.shape, F32)
        m_scr[...] = jnp.zeros(m_scr.shape, F32)

    x = x_ref[...]
    xg = x * g_ref[...]
    xb = xg.astype(BF16)
    r = lax.rsqrt(jnp.mean(x * x, axis=-1, keepdims=True) + EPS)
    grow = _dot_nt(wt_ref[...], (xg * r).astype(BF16)) + bt_ref[...]
    kept = {name: [] for name in _PROJ_MLSTM_ONLY + ("gcol",)}
    mlstm_ready = max(off + width for name, off, width, _ in cols if name in kept)
    ncols = w_ref.shape[1]
    for c0 in range(0, ncols, _PROJ_CHUNK):
        cw = min(_PROJ_CHUNK, ncols - c0)
        chunk = r * _dot(xb, w_ref[:, c0:c0 + cw]) + b_ref[:, c0:c0 + cw]
        for name, off, width, dt in cols:
            lo, hi = max(off, c0), min(off + width, c0 + cw)
            if lo >= hi:
                continue
            acc = chunk[:, lo - c0:hi - c0]
            cs = slice(lo - off, hi - off)
            if name == "qk":
                xbuf[halo:halo + tm, cs] = acc
                y = acc * conv_ref[0:1, cs]
                for j in range(1, ML_CONV):
                    y = y + xbuf[halo - j:halo - j + tm, cs] * conv_ref[j:j + 1, cs]
                xbuf[0:halo, cs] = xbuf[tm:tm + halo, cs]
                col = lax.broadcasted_iota(jnp.int32, (1, hi - lo), 1) + (lo - off)
                acc = y * jax.nn.sigmoid(y) * jnp.where(col >= ML_WIDTH, ML_DH ** -0.5, 1.0)
            elif name in ("o", "mg"):
                acc = jax.nn.sigmoid(acc)
            acc = acc.astype(dt)
            if name in kept:
                kept[name].append(acc)
            if name in out_refs:
                out_refs[name][:, cs] = acc
        if c0 < mlstm_ready <= c0 + cw:
            qk, v, og = (jnp.concatenate(kept[n], axis=1) for n in _PROJ_MLSTM_ONLY)
            gcol = kept["gcol"][0]
            c_state = [c_scr[hh] for hh in range(ML_HEADS)]
            m_state = [m_scr[hh:hh + 1, 0:1] for hh in range(ML_HEADS)]
            for r0 in range(0, tm, ML_CHUNK):
                rows = slice(r0, r0 + ML_CHUNK)
                y_ml, c_state, m_state = _mlstm_chunk(qk[rows], v[rows], og[rows], gcol[rows],
                                                      grow[:, rows], ng_ref[...], c_state, m_state)
                yml_ref[rows, :] = y_ml
            for hh in range(ML_HEADS):
                c_scr[hh] = c_state[hh]
                m_scr[hh:hh + 1, :] = jnp.broadcast_to(m_state[hh], (1, m_scr.shape[1]))


def _split_w_in(w_in, b_in):
    widths = (ML_WIDTH, ML_WIDTH, ML_WIDTH, ML_WIDTH, ML_HEADS, ML_HEADS,
              NSA_WIDTH, NSA_KV_WIDTH, NSA_KV_WIDTH, NSA_KV_WIDTH, NSA_KV_WIDTH, NSA_KV_WIDTH,
              NSA_KV_WIDTH, 3 * NSA_HEADS, XA_WIDTH, w_in.shape[1])
    names = ("ml_q", "ml_k", "ml_v", "ml_o", "ml_i", "ml_f", "ns_q", "ns_kc", "ns_vc", "ns_ks",
             "ns_vs", "ns_kw", "ns_vw", "ns_g", "xa_q", "mg")
    out, off = {}, 0
    for n, wd in zip(names, widths):
        end = w_in.shape[1] if n == "mg" else off + wd
        out[n] = (w_in[:, off:end], b_in[off:end])
        off = end
    return out


def _inproj(x2, g_mix, w_in, b_in, ml_conv, ml_norm_g, seq):
    n_tok, d = x2.shape
    cols, ncols = _proj_layout(d)
    p = _split_w_in(w_in, b_in)

    def pad_cols(w, b, width):
        return (jnp.pad(w, ((0, 0), (0, width - w.shape[1]))), jnp.pad(b, (0, width - b.shape[0])))

    wq, bq = p["ns_q"]
    wq = wq.reshape(d, NSA_HEADS, 1, NSA_DH)
    bq = bq.reshape(NSA_HEADS, 1, NSA_DH)
    in_group = jnp.asarray(np.arange(NSA_HEADS)[:, None] // NSA_REP == np.arange(NSA_KV)[None, :],
                           w_in.dtype)[:, :, None]
    wq_slots = wq * in_group
    bq_slots = bq * in_group
    gate_w = jnp.concatenate([p["ml_i"][0], p["ml_f"][0]], axis=1)
    gate_b = jnp.concatenate([p["ml_i"][1], p["ml_f"][1]])
    small_w = jnp.concatenate([gate_w, p["ns_g"][0]], axis=1)
    small_b = jnp.concatenate([gate_b, p["ns_g"][1]])

    def with_ones(wb, g):
        w, b = wb
        w = jnp.pad(w[:, g * NSA_DH:(g + 1) * NSA_DH], ((0, 0), (0, LANES - NSA_DH)))
        b = jnp.concatenate([b[g * NSA_DH:(g + 1) * NSA_DH], jnp.ones((LANES - NSA_DH,), b.dtype)])
        return w, b

    pieces = {
        "qk": (jnp.concatenate([p["ml_q"][0], p["ml_k"][0]], axis=1),
               jnp.concatenate([p["ml_q"][1], p["ml_k"][1]])),
        "o": p["ml_o"], "mg": p["mg"],
        "gcol": pad_cols(small_w, small_b, LANES),
        "kc": p["ns_kc"], "vc": p["ns_vc"], "v": p["ml_v"],
        "nq": (wq_slots.reshape(d, NSA_HEADS * LANES), bq_slots.reshape(NSA_HEADS * LANES)),
        "xq": p["xa_q"], "ks": p["ns_ks"], "kw": p["ns_kw"],
        "vs0": with_ones(p["ns_vs"], 0), "vs1": with_ones(p["ns_vs"], 1),
        "vw0": with_ones(p["ns_vw"], 0), "vw1": with_ones(p["ns_vw"], 1),
    }
    w_cols = jnp.concatenate([pieces[name][0] for name, *_ in cols], axis=1).astype(BF16)
    b_cols = jnp.concatenate([pieces[name][1] for name, *_ in cols])[None, :]
    w_rows = gate_w.T.astype(BF16)
    b_rows = gate_b[:, None]

    tm = TM_PROJ
    tiles_per_b = seq // tm
    outs_hbm = [c for c in cols if c[0] not in _PROJ_MLSTM_ONLY] + [("y_ml", None, ML_WIDTH, BF16)]
    outs = pl.pallas_call(
        functools.partial(_inproj_kernel, cols, tiles_per_b),
        out_shape=[jax.ShapeDtypeStruct((n_tok, width), dt) for _, _, width, dt in outs_hbm],
        grid=(n_tok // tm,),
        in_specs=[pl.BlockSpec((tm, d), lambda i: (i, 0)),
                  _resident((1, d)), _resident((d, ncols)), _resident((1, ncols)),
                  _resident((SUBLANES, d)), _resident((SUBLANES, 1)),
                  _resident((ML_CONV, 2 * ML_WIDTH)), _resident((1, ML_WIDTH))],
        out_specs=[pl.BlockSpec((tm, width), lambda i: (i, 0)) for _, _, width, _ in outs_hbm],
        scratch_shapes=[pltpu.VMEM((tm + SUBLANES, 2 * ML_WIDTH), F32),
                        pltpu.VMEM((ML_HEADS, ML_DH, 2 * ML_DH), F32),
                        pltpu.VMEM((SUBLANES, LANES), F32)],
        compiler_params=pltpu.CompilerParams(dimension_semantics=("arbitrary",),
                                             vmem_limit_bytes=VMEM_LIMIT),
        name="inproj",
    )(x2, g_mix[None, :], w_cols, b_cols, w_rows, b_rows, ml_conv, ml_norm_g[None, :])
    return dict(zip([c[0] for c in outs_hbm], outs))


def _mlstm_chunk(qk, v, og, gcol, grow, ng, c_prev, m_prev):
    L = ML_CHUNK
    r_i = lax.broadcasted_iota(jnp.int32, (L, L), 0)
    c_i = lax.broadcasted_iota(jnp.int32, (L, L), 1)
    causal = c_i <= r_i
    tril = causal.astype(F32)
    triu = (r_i <= c_i).astype(F32)
    b_col = jnp.dot(tril, jax.nn.log_sigmoid(gcol), precision=lax.Precision.HIGHEST,
                    preferred_element_type=F32)
    b_row = jnp.dot(jax.nn.log_sigmoid(grow), triu, precision=lax.Precision.HIGHEST,
                    preferred_element_type=F32)
    ones_col = jnp.where(lax.broadcasted_iota(jnp.int32, (L, ML_DH), 1) == 0, 1.0, 0.0).astype(BF16)

    def operands(hh):
        sl = slice(hh * ML_DH, (hh + 1) * ML_DH)
        k = qk[:, ML_WIDTH + hh * ML_DH:ML_WIDTH + (hh + 1) * ML_DH]
        return qk[:, sl], k, jnp.concatenate([v[:, sl], ones_col], axis=1), c_prev[hh]

    ops = [operands(hh) for hh in range(ML_HEADS)]
    qk_t = [_dot_nt(q, k) for q, k, _, _ in ops]
    inter = [_dot(q, c.astype(BF16)) for q, _, _, c in ops]

    stab, ke_all, s_all = [], [], []
    for hh in range(ML_HEADS):
        li_c = gcol[:, hh:hh + 1]
        b_c = b_col[:, ML_HEADS + hh:ML_HEADS + hh + 1]
        li_r = grow[hh:hh + 1, :]
        b_r = b_row[ML_HEADS + hh:ML_HEADS + hh + 1, :]
        g = b_c[L - 1:L, :]
        rel = jnp.where(causal, li_r - b_r, -jnp.inf)
        mm = jnp.maximum(m_prev[hh], jnp.max(rel, axis=-1, keepdims=True))
        m_t = b_c + mm
        s_all.append((qk_t[hh] * jnp.exp(rel - mm)).astype(BF16))
        w_end = g - b_c + li_c
        m_loc = jnp.max(w_end, axis=0, keepdims=True)
        ke_all.append((ops[hh][1] * jnp.exp(w_end - m_loc)).astype(BF16))
        m_new = jnp.maximum(g + m_prev[hh], m_loc)
        stab.append((m_t, jnp.exp(m_prev[hh] - mm), jnp.exp(g + m_prev[hh] - m_new),
                     jnp.exp(m_loc - m_new), m_new))

    intra = [_dot(s, o[2]) for s, o in zip(s_all, ops)]
    a_c = [lax.dot_general(ke, o[2], _TN, preferred_element_type=F32) for ke, o in zip(ke_all, ops)]

    ys, new_c, new_m = [], [], []
    for hh in range(ML_HEADS):
        sl = slice(hh * ML_DH, (hh + 1) * ML_DH)
        m_t, sc, a, bb, m_new = stab[hh]
        nd = intra[hh] + sc * inter[hh]
        den = nd[:, ML_DH:ML_DH + 1]
        h = nd[:, 0:ML_DH] / jnp.maximum(jnp.abs(den), jnp.exp(-m_t))
        hn = h * lax.rsqrt(jnp.mean(h * h, axis=-1, keepdims=True) + EPS)
        ys.append((og[:, sl] * hn * ng[:, sl]).astype(BF16))
        new_c.append(a * ops[hh][3] + bb * a_c[hh])
        new_m.append(m_new)
    return jnp.concatenate(ys, axis=1), new_c, new_m


def _gelu_tanh(x):
    return 0.5 * x * (1.0 + jnp.tanh(np.sqrt(2.0 / np.pi) * (x + 0.044715 * (x * x * x))))


def _compress_kernel(k_ref, v_ref, pe_ref, w1_ref, w2k_ref, w2v_ref, kc_ref, vc0_ref, vc1_ref):
    nrow = kc_ref.shape[1]
    S = CMP_STRIDE

    def hidden(x_ref, which):
        lo = hi = None
        for j in range(0, S, 2):
            x = [x_ref[pl.ds(j + d, nrow, stride=S), :] for d in range(2)]
            for half in range(CMP_LEN // S):
                l0 = half * S + j
                xp = jnp.concatenate([(x[d] + pe_ref[which, l0 + d:l0 + d + 1, :]).astype(BF16)
                                      for d in range(2)], axis=1)
                t = _dot(xp, w1_ref[which, l0 // 2])
                if half == 0:
                    lo = t if lo is None else lo + t
                else:
                    hi = t if hi is None else hi + t
        return _gelu_tanh(lo + pltpu.roll(hi, nrow - 1, 0)).astype(BF16)

    kc_ref[0] = _dot(hidden(k_ref, 0), w2k_ref[...]).astype(kc_ref.dtype)
    hv = hidden(v_ref, 1)
    ones_half = lax.broadcasted_iota(jnp.int32, (nrow, LANES), 1) >= NSA_DH
    for g, o_ref in enumerate((vc0_ref, vc1_ref)):
        o_ref[0] = jnp.where(ones_half, 1.0, _dot(hv, w2v_ref[g])).astype(o_ref.dtype)


def _compress(pr, cmp_pe, cmp_w1, cmp_w2, batch, seq):
    nrow = seq // CMP_STRIDE
    hid = NSA_KV * CMP_HIDDEN
    w1 = cmp_w1.astype(BF16).reshape(2, CMP_LEN, NSA_DH, CMP_HIDDEN)
    z1 = jnp.zeros_like(w1)
    w1e = jnp.stack([jnp.concatenate([w1, z1], axis=-1), jnp.concatenate([z1, w1], axis=-1)], axis=2)
    w1e = w1e.reshape(2, CMP_LEN // 2, 2 * NSA_KV_WIDTH, hid)
    w2 = cmp_w2.astype(BF16)
    z2 = jnp.zeros_like(w2[0])
    w2k = jnp.concatenate([jnp.concatenate([w2[0], z2], axis=1),
                           jnp.concatenate([z2, w2[0]], axis=1)], axis=0)
    w2v = jnp.stack([jnp.pad(w2[1], ((g * CMP_HIDDEN, (NSA_KV - 1 - g) * CMP_HIDDEN),
                                     (0, LANES - NSA_DH))) for g in range(NSA_KV)])
    pe = jnp.concatenate([cmp_pe] * NSA_KV, axis=-1)
    tok_blk = pl.BlockSpec((seq, NSA_KV_WIDTH), lambda b: (b, 0))
    out_blk = pl.BlockSpec((1, nrow, LANES), lambda b: (b, 0, 0))
    return pl.pallas_call(
        _compress_kernel,
        out_shape=(jax.ShapeDtypeStruct((batch, nrow, LANES), BF16),) * 3,
        grid=(batch,),
        in_specs=[tok_blk, tok_blk,
                  _resident((2, CMP_LEN, NSA_KV_WIDTH)),
                  _resident((2, CMP_LEN // 2, 2 * NSA_KV_WIDTH, hid)),
                  _resident((hid, NSA_KV_WIDTH)),
                  _resident((NSA_KV, hid, LANES))],
        out_specs=(out_blk, out_blk, out_blk),
        compiler_params=pltpu.CompilerParams(dimension_semantics=("parallel",),
                                             vmem_limit_bytes=VMEM_LIMIT),
        name="compress",
    )(pr["kc"], pr["vc"], pe, w1e, w2k, w2v)


def _alibi_slope(h):
    return float(2.0 ** (-8.0 * (h + 1) / NSA_HEADS))


NSA_ROWS = NSA_HEADS * Q_BLOCK
WIN_KEYS = WINDOW + Q_BLOCK
F_SEL_BLOCK = 0
F_SEL_OFF = SEL_LEN
F_WIN_OFF = SEL_LEN + 1
F_WIN_BLK = SEL_LEN + 2
F_CMP = SEL_LEN + 3


def _tile_heads(x):
    return jnp.concatenate([x] * NSA_HEADS, axis=0)


def _nsa_kernel(nbs, q_ref, g_ref, kc_ref, vc0_ref, vc1_ref, ks_ref, vs0_ref, vs1_ref,
                kw_ref, vw0_ref, vw1_ref, ov_ref, gx_ref, y_ref,
                qe_scr, posf_scr, wposf_scr, cposf_scr, s_scr, mb_scr, acc_scr):
    QB = Q_BLOCK
    KT = SEL_KT
    HALF = NSA_REP * QB
    t0 = pl.program_id(1) * QB
    ncmp = kc_ref.shape[1]
    lane = lax.broadcasted_iota(jnp.int32, (1, LANES), 1)
    lo_half = lane < NSA_DH

    @pl.when(pl.program_id(1) == 0)
    def _():
        seq = posf_scr.shape[0]
        k = lax.broadcasted_iota(jnp.int32, (seq, LANES), 0)
        c = lax.broadcasted_iota(jnp.int32, (seq, LANES), 1)
        posf = jnp.where(c < SEL_LEN, jnp.where(k // SEL_LEN == c, 1.0, 0.0),
                         jnp.where(c == F_SEL_OFF, (k % SEL_LEN).astype(F32), 0.0))
        posf_scr[...] = posf.astype(BF16)
        k = lax.broadcasted_iota(jnp.int32, (WIN_KEYS, LANES), 0)
        c = lax.broadcasted_iota(jnp.int32, (WIN_KEYS, LANES), 1)
        wposf = jnp.where(c == F_WIN_OFF, (k % SEL_LEN).astype(F32),
                          jnp.where(c == F_WIN_BLK, (k // SEL_LEN).astype(F32), 0.0))
        wposf_scr[...] = wposf.astype(BF16)
        k = lax.broadcasted_iota(jnp.int32, (ncmp, LANES), 0)
        c = lax.broadcasted_iota(jnp.int32, (ncmp, LANES), 1)
        cposf_scr[...] = jnp.where(c == F_CMP, k.astype(F32), 0.0).astype(BF16)

    def const_feat(h):
        s = _alibi_slope(h)
        return jnp.where((lane == F_SEL_OFF) | (lane == F_WIN_OFF), s,
                         jnp.where(lane == F_WIN_BLK, SEL_LEN * s,
                                   jnp.where(lane == F_CMP, CMP_STRIDE * s, 0.0)))

    for h in range(NSA_HEADS):
        rows = slice(h * QB, (h + 1) * QB)
        qe_scr[rows, 0:LANES] = q_ref[:, h * LANES:(h + 1) * LANES] * 0.125
        qe_scr[rows, LANES:2 * LANES] = jnp.broadcast_to(const_feat(h), (QB, LANES)).astype(BF16)

    def normalised(o, guard):
        outs = []
        for pair in range(NSA_HEADS // 2):
            e = o[(2 * pair) * QB:(2 * pair + 1) * QB]
            d = o[(2 * pair + 1) * QB:(2 * pair + 2) * QB]
            num = jnp.where(lo_half, e, pltpu.roll(d, NSA_DH, 1))
            den = jnp.where(lo_half, pltpu.roll(e, NSA_DH, 1), d)
            if guard:
                den = jnp.where(den > 0.0, den, 1.0)
            outs.append(num / den)
        return outs

    ws = pl.multiple_of(jnp.maximum(t0 - WINDOW, 0), QB)
    qs = lax.broadcasted_iota(jnp.int32, (QB, WIN_KEYS), 0)
    kl = lax.broadcasted_iota(jnp.int32, (QB, WIN_KEYS), 1)
    dist_w = (t0 - ws) + qs - kl
    ok_w = (dist_w >= 0) & (dist_w < WINDOW)
    s_w = _dot_nt(qe_scr[...], jnp.concatenate([kw_ref[pl.ds(ws, WIN_KEYS), :], wposf_scr[...]], axis=1))

    q_sub = lax.broadcasted_iota(jnp.int32, (QB, ncmp), 0)
    n_lane = lax.broadcasted_iota(jnp.int32, (QB, ncmp), 1)
    ok_c = n_lane * CMP_STRIDE + (CMP_LEN - 1) <= t0 + q_sub
    s = _dot_nt(qe_scr[...], jnp.concatenate([kc_ref[0], cposf_scr[...]], axis=1))
    s = s + _tile_heads(jnp.where(ok_c, 0.0, NEG))
    e = jnp.exp(s - jnp.max(s, axis=-1, keepdims=True)) * _tile_heads(jnp.where(ok_c, 1.0, 0.0))
    e_bf = e.astype(BF16)
    oc = jnp.concatenate([_dot(e_bf[0:HALF], vc0_ref[0]), _dot(e_bf[HALF:2 * HALF], vc1_ref[0])],
                         axis=0)
    o_cmp = normalised(oc, True)
    l_c = jnp.where(lo_half, pltpu.roll(oc, NSA_DH, 1), oc)
    inv_l = 1.0 / jnp.where(l_c > 0.0, l_c, 1.0)
    p_c = e * jnp.concatenate([inv_l] * (ncmp // LANES), axis=1)

    imp_ts = []
    for g in range(NSA_KV):
        psum = p_c[(g * NSA_REP) * QB:(g * NSA_REP + 1) * QB]
        for r in range(1, NSA_REP):
            psum = psum + p_c[(g * NSA_REP + r) * QB:(g * NSA_REP + r + 1) * QB]
        p_hi = psum.astype(BF16)
        p_lo = (psum - p_hi.astype(F32)).astype(BF16)
        imp = _dot(p_hi, ov_ref[...]) + _dot(p_lo, ov_ref[...])
        imp_ts.append(imp.T[0:nbs, :])

    n_grp = nbs // SUBLANES
    sub8 = lax.broadcasted_iota(jnp.int32, (SUBLANES, QB), 0)
    tpos = t0 + lax.broadcasted_iota(jnp.int32, (1, QB), 1)
    cur = tpos // SEL_LEN
    j_lane = lane.astype(F32)
    for g in range(NSA_KV):
        imp_t = imp_ts[g]
        j_sub = lax.broadcasted_iota(jnp.int32, (nbs, QB), 0)
        forced = (j_sub == 0) | (j_sub == cur) | (j_sub == cur - 1)
        score = jnp.where(j_sub <= cur, imp_t + jnp.where(forced, FORCE_BONUS, 0.0), NEG)
        parts = [score[a * SUBLANES:(a + 1) * SUBLANES, :] for a in range(n_grp)]
        ranks = [jnp.zeros((SUBLANES, QB), F32) for _ in range(n_grp)]
        for i in range(nbs):
            row = jnp.broadcast_to(score[i:i + 1, :], (SUBLANES, QB))
            for a in range(n_grp):
                if a * SUBLANES > i:
                    beats = jnp.where(row >= parts[a], 1.0, 0.0)
                elif (a + 1) * SUBLANES - 1 < i:
                    beats = jnp.where(row > parts[a], 1.0, 0.0)
                else:
                    beats = jnp.where(sub8 + a * SUBLANES > i,
                                      jnp.where(row >= parts[a], 1.0, 0.0),
                                      jnp.where(row > parts[a], 1.0, 0.0))
                ranks[a] = ranks[a] + beats
        sel_t = [jnp.where(rk < float(min(SEL_TOP, nbs)), 0.0, NEG) for rk in ranks]
        if nbs < LANES:
            sel_t.append(jnp.zeros((LANES - nbs, QB), F32))
        sel_bias = jnp.concatenate(sel_t, axis=0).T
        for r in range(NSA_REP):
            h = g * NSA_REP + r
            feat = jnp.where(lo_half, sel_bias + (_alibi_slope(h) * SEL_LEN) * j_lane, const_feat(h))
            qe_scr[h * QB:(h + 1) * QB, LANES:2 * LANES] = feat.astype(BF16)

    s_w = s_w + _tile_heads(jnp.where(ok_w, 0.0, NEG))
    p_w = jnp.exp(s_w - jnp.max(s_w, axis=-1, keepdims=True)).astype(BF16)
    o_win = normalised(jnp.concatenate([_dot(p_w[0:HALF], vw0_ref[pl.ds(ws, WIN_KEYS), :]),
                                        _dot(p_w[HALF:2 * HALF], vw1_ref[pl.ds(ws, WIN_KEYS), :])],
                                       axis=0), False)

    last_tile = t0 // KT
    mb_scr[...] = jnp.full(mb_scr.shape, NEG, F32)

    U = SEL_GROUP

    def score_group(kt0, count, causal):
        k0 = pl.multiple_of(kt0 * KT, KT)
        kx = jnp.concatenate([ks_ref[pl.ds(k0, count * KT), :],
                              posf_scr[pl.ds(k0, count * KT), :]], axis=1)
        s = _dot_nt(qe_scr[...], kx)
        tile_max = None
        for j in range(count):
            sj = s[:, j * KT:(j + 1) * KT] * LOG2_E
            if causal and j == count - 1:
                qs = lax.broadcasted_iota(jnp.int32, (QB, KT), 0)
                kl = lax.broadcasted_iota(jnp.int32, (QB, KT), 1)
                sj = sj + _tile_heads(jnp.where(k0 + j * KT + kl <= t0 + qs, 0.0, NEG))
            s_scr[kt0 + j] = sj
            for c0 in range(0, KT, LANES):
                part = sj[:, c0:c0 + LANES]
                tile_max = part if tile_max is None else jnp.maximum(tile_max, part)
        group_max = jnp.max(tile_max, axis=-1, keepdims=True)
        mb_scr[...] = jnp.maximum(mb_scr[...], jnp.broadcast_to(group_max, mb_scr.shape))

    def score_full_group(i, carry):
        score_group(i * U, U, False)
        return carry

    lax.fori_loop(0, last_tile // U, score_full_group, 0)
    for r in range(U):
        @pl.when(last_tile % U == r)
        def _(r=r):
            score_group(last_tile - r, r + 1, True)

    acc_scr[...] = jnp.zeros(acc_scr.shape, F32)

    def value_group(kt0, count):
        k0 = pl.multiple_of(kt0 * KT, KT)
        mb = mb_scr[...]
        mbw = jnp.concatenate([mb] * (KT // LANES), axis=1)
        p = jnp.concatenate([jnp.exp2(s_scr[kt0 + j] - mbw).astype(BF16) for j in range(count)],
                            axis=1)
        acc_scr[0:HALF, :] += _dot(p[0:HALF], vs0_ref[pl.ds(k0, count * KT), :])
        acc_scr[HALF:2 * HALF, :] += _dot(p[HALF:2 * HALF], vs1_ref[pl.ds(k0, count * KT), :])

    def value_full_group(i, carry):
        value_group(i * U, U)
        return carry

    n_tiles = last_tile + 1
    lax.fori_loop(0, n_tiles // U, value_full_group, 0)
    for r in range(1, U):
        @pl.when(n_tiles % U == r)
        def _(r=r):
            value_group(n_tiles - r, r)

    o_sel = normalised(acc_scr[...], False)

    gate = jax.nn.sigmoid(g_ref[...])
    g_hi = gate.astype(BF16)
    g_lo = (gate - g_hi.astype(F32)).astype(BF16)
    spread = _dot(g_hi, gx_ref[...]) + _dot(g_lo, gx_ref[...])
    for pair in range(NSA_HEADS // 2):
        y = None
        for b, o in enumerate((o_cmp, o_sel, o_win)):
            c0 = b * NSA_WIDTH + pair * LANES
            term = spread[:, c0:c0 + LANES] * o[pair]
            y = term if y is None else y + term
        y_ref[:, pair * LANES:(pair + 1) * LANES] = y.astype(y_ref.dtype)


def _nsa(pr, kc, vc0, vc1, batch, seq):
    QB = Q_BLOCK
    nq = seq // QB
    nbs = seq // SEL_LEN
    ncmp = kc.shape[1]
    cs = np.arange(ncmp) * CMP_STRIDE
    js = np.arange(nbs) * SEL_LEN
    ov = np.zeros((ncmp, LANES), np.float32)
    ov[:, :nbs] = (cs[:, None] < js[None, :] + SEL_LEN) & (cs[:, None] + CMP_LEN > js[None, :])
    gx = np.zeros((LANES, 3 * NSA_WIDTH), np.float32)
    for h in range(NSA_HEADS):
        for b in range(3):
            gx[SUBLANES + 3 * h + b, b * NSA_WIDTH + h * NSA_DH:b * NSA_WIDTH + (h + 1) * NSA_DH] = 1.0
    tok = lambda b, i: (b * nq + i, 0)
    per_b = lambda b, i: (b, 0)
    per_b3 = lambda b, i: (b, 0, 0)
    kv_spec = pl.BlockSpec((seq, LANES), per_b)
    cmp_spec = pl.BlockSpec((1, ncmp, LANES), per_b3)
    return pl.pallas_call(
        functools.partial(_nsa_kernel, nbs),
        out_shape=jax.ShapeDtypeStruct((batch * seq, NSA_WIDTH), BF16),
        grid=(batch, nq),
        in_specs=[pl.BlockSpec((QB, NSA_HEADS * LANES), tok),
                  pl.BlockSpec((QB, LANES), tok),
                  cmp_spec, cmp_spec, cmp_spec,
                  kv_spec, kv_spec, kv_spec, kv_spec, kv_spec, kv_spec,
                  pl.BlockSpec((ncmp, LANES), lambda b, i: (0, 0)),
                  pl.BlockSpec((LANES, 3 * NSA_WIDTH), lambda b, i: (0, 0))],
        out_specs=pl.BlockSpec((QB, NSA_WIDTH), tok),
        scratch_shapes=[pltpu.VMEM((NSA_ROWS, 2 * LANES), BF16),
                        pltpu.VMEM((seq, LANES), BF16),
                        pltpu.VMEM((WIN_KEYS, LANES), BF16),
                        pltpu.VMEM((ncmp, LANES), BF16),
                        pltpu.VMEM((seq // SEL_KT, NSA_ROWS, SEL_KT), F32),
                        pltpu.VMEM((NSA_ROWS, LANES), F32),
                        pltpu.VMEM((NSA_ROWS, LANES), F32)],
        compiler_params=pltpu.CompilerParams(dimension_semantics=("parallel", "arbitrary"),
                                             vmem_limit_bytes=VMEM_LIMIT),
        name="nsa",
    )(pr["nq"], pr["gcol"], kc, vc0, vc1, pr["ks"], pr["vs0"], pr["vs1"],
      pr["kw"], pr["vw0"], pr["vw1"], jnp.asarray(ov, BF16), jnp.asarray(gx, BF16))


def _memkv_kernel(mem_ref, g_ref, w_ref, kv_ref):
    u = _rms(mem_ref[0], g_ref[...]).astype(BF16)
    kv_ref[0] = _dot(u, w_ref[...]).astype(kv_ref.dtype)


def _memkv(mem, g_mem, w_mem_kv):
    batch, n_mem, d = mem.shape
    return pl.pallas_call(
        _memkv_kernel,
        out_shape=jax.ShapeDtypeStruct((batch, n_mem, 2 * XA_WIDTH), BF16),
        grid=(batch,),
        in_specs=[pl.BlockSpec((1, n_mem, d), lambda b: (b, 0, 0)),
                  _resident((1, d)), _resident((d, 2 * XA_WIDTH))],
        out_specs=pl.BlockSpec((1, n_mem, 2 * XA_WIDTH), lambda b: (b, 0, 0)),
        compiler_params=pltpu.CompilerParams(dimension_semantics=("parallel",),
                                             vmem_limit_bytes=VMEM_LIMIT),
        name="memkv",
    )(mem, g_mem[None, :], w_mem_kv.astype(BF16))


def _tail_kernel(x_ref, yml_ref, ynsa_ref, xq_ref, kv_ref, mg_ref, wb_ref, wo_ref, gf_ref,
                 w1_ref, w2_ref, gl_ref, out_ref):
    d = x_ref.shape[1]
    kv = kv_ref[0]
    heads = [slice(hh * XA_DH, (hh + 1) * XA_DH) for hh in range(XA_HEADS)]
    scores = [_dot_nt(xq_ref[:, sl], kv[:, sl]) * (XA_DH ** -0.5) for sl in heads]
    ups = [_dot(yml_ref[...], wb_ref[0]), _dot(ynsa_ref[...], wb_ref[1])]
    y_xa = []
    for hh, s in enumerate(scores):
        p = jnp.exp(s - jnp.max(s, axis=-1, keepdims=True))
        l = jnp.sum(p, axis=-1, keepdims=True)
        y_xa.append((_dot(p.astype(BF16), kv[:, XA_WIDTH + hh * XA_DH:XA_WIDTH + (hh + 1) * XA_DH])
                     / l).astype(BF16))
    ups.append(_dot(jnp.concatenate(y_xa, axis=-1), wb_ref[2]))
    merged = None
    for j in range(N_BRANCH):
        term = mg_ref[:, j * d:(j + 1) * d] * ups[j]
        merged = term if merged is None else merged + term
    h = x_ref[...] + _dot(merged.astype(BF16), wo_ref[...])
    hb = (h * gf_ref[...]).astype(BF16)
    r = lax.rsqrt(jnp.mean(h * h, axis=-1, keepdims=True) + EPS)
    acc = h
    for c0 in range(0, w1_ref.shape[1], FF_SLAB):
        a = jnp.maximum(r * _dot(hb, w1_ref[:, c0:c0 + FF_SLAB]), 0.0)
        acc = acc + _dot((a * a).astype(BF16), w2_ref[c0:c0 + FF_SLAB, :])
    out_ref[...] = _rms(acc, gl_ref[...])


def _tail(x2, y_ml, y_nsa, pr, mem_kv, w_branch, w_out, g_ffn, w_ff1, w_ff2, g_final, seq):
    n_tok, d = x2.shape
    tm = TM_TAIL
    d_ff = w_ff1.shape[1]
    n_mem = mem_kv.shape[1]
    tiles_per_b = seq // tm
    row = lambda i: (i, 0)
    return pl.pallas_call(
        _tail_kernel,
        out_shape=jax.ShapeDtypeStruct((n_tok, d), F32),
        grid=(n_tok // tm,),
        in_specs=[pl.BlockSpec((tm, d), row),
                  pl.BlockSpec((tm, ML_WIDTH), row),
                  pl.BlockSpec((tm, NSA_WIDTH), row),
                  pl.BlockSpec((tm, XA_WIDTH), row),
                  pl.BlockSpec((1, n_mem, 2 * XA_WIDTH), lambda i: (i // tiles_per_b, 0, 0)),
                  pl.BlockSpec((tm, N_BRANCH * d), row),
                  _resident((N_BRANCH, ML_WIDTH, d)), _resident((d, d)), _resident((1, d)),
                  _resident((d, d_ff)), _resident((d_ff, d)), _resident((1, d))],
        out_specs=pl.BlockSpec((tm, d), row),
        compiler_params=pltpu.CompilerParams(dimension_semantics=("parallel",),
                                             vmem_limit_bytes=VMEM_LIMIT),
        name="tail",
    )(x2, y_ml, y_nsa, pr["xq"], mem_kv, pr["mg"], w_branch.astype(BF16), w_out.astype(BF16),
      g_ffn[None, :], w_ff1.astype(BF16), w_ff2.astype(BF16), g_final[None, :])


def _layer(x, mem, g_mix, w_in, b_in, ml_conv, ml_norm_g, cmp_pe, cmp_w1, cmp_w2, g_mem, w_mem_kv,
           w_branch, w_out, g_ffn, w_ff1, w_ff2, g_final):
    batch, seq, d = x.shape
    assert seq % (SEL_LEN * SUBLANES) == 0 and seq % SEL_KT == 0 and seq >= WIN_KEYS
    assert seq // SEL_LEN <= NSA_DH, "one selection-block feature lane per block"
    assert seq // CMP_STRIDE <= 256, "compressed token index must be exact in bf16"
    assert seq % TM_PROJ == 0 and seq % TM_TAIL == 0 and TM_PROJ % ML_CHUNK == 0
    x2 = x.reshape(batch * seq, d)
    pr = _inproj(x2, g_mix, w_in, b_in, ml_conv, ml_norm_g, seq)
    y_ml = pr["y_ml"]
    kc, vc0, vc1 = _compress(pr, cmp_pe, cmp_w1, cmp_w2, batch, seq)
    y_nsa = _nsa(pr, kc, vc0, vc1, batch, seq)
    mem_kv = _memkv(mem, g_mem, w_mem_kv)
    out = _tail(x2, y_ml, y_nsa, pr, mem_kv, w_branch, w_out, g_ffn, w_ff1, w_ff2, g_final, seq)
    return out.reshape(batch, seq, d)


def kernel(x, mem, g_mix, w_in, b_in, ml_conv, ml_norm_g, cmp_pe, cmp_w1, cmp_w2, g_mem, w_mem_kv,
           w_branch, w_out, g_ffn, w_ff1, w_ff2, g_final):
    assert g_mix.shape[0] == 1, "single-layer block"
    return _layer(x, mem, g_mix[0], w_in[0], b_in[0], ml_conv[0], ml_norm_g[0], cmp_pe[0],
                  cmp_w1[0], cmp_w2[0], g_mem[0], w_mem_kv[0], w_branch[0], w_out[0], g_ffn[0],
                  w_ff1[0], w_ff2[0], g_final)
```
